```python
import jax, jax.numpy as jnp
from jax import lax
import numpy as np

D_MODEL = 1024
BATCH = 8
SEQ = 2048
DEPTH = 1

PLE_DIM = 256
LRU_WIDTH = D_MODEL
LRU_BLOCKS = 16
LRU_BLOCK_DIM = LRU_WIDTH // LRU_BLOCKS
CONV_WIDTH = 4
LRU_C = 8.0
SGU_WIDTH = D_MODEL
SGU_GROUPS = 8
SGU_GROUP_DIM = SGU_WIDTH // SGU_GROUPS
CHUNK = 128
N_GROUPS = 4
EXPERTS_PER_GROUP = 8
N_EXPERTS = N_GROUPS * EXPERTS_PER_GROUP
TOP_K = 2
D_EXPERT = 256
EPS = 1e-6
IN_COLS = 2 * LRU_WIDTH + 2 * SGU_WIDTH + 2 * D_MODEL

kernel_name = "hybrid_rglru_sgu_hmoe_block"


def rmsnorm(x, g):
    xf = x.astype(jnp.float32)
    y = xf * lax.rsqrt(jnp.mean(xf * xf, axis=-1, keepdims=True) + EPS)
    return (y * g.astype(jnp.float32)).astype(x.dtype)


def layernorm(x, g, b):
    xf = x.astype(jnp.float32)
    mu = jnp.mean(xf, axis=-1, keepdims=True)
    xc = xf - mu
    y = xc * lax.rsqrt(jnp.mean(xc * xc, axis=-1, keepdims=True) + EPS)
    return (y * g.astype(jnp.float32) + b.astype(jnp.float32)).astype(x.dtype)


def causal_depthwise_conv(x, w, b):
    s = x.shape[1]
    xp = jnp.pad(x, ((0, 0), (CONV_WIDTH - 1, 0), (0, 0)))
    y = b
    for k in range(CONV_WIDTH):
        y = y + xp[:, k:k + s, :] * w[k]
    return y


def rg_lru(x, w_a, b_a, w_i, b_i, lam):
    bsz, s, w = x.shape
    xb = x.reshape(bsz, s, LRU_BLOCKS, LRU_BLOCK_DIM)
    r = jax.nn.sigmoid(jnp.einsum('bshi,hij->bshj', xb, w_a).reshape(bsz, s, w) + b_a)
    i = jax.nn.sigmoid(jnp.einsum('bshi,hij->bshj', xb, w_i).reshape(bsz, s, w) + b_i)
    log_a = -LRU_C * r.astype(jnp.float32) * jax.nn.softplus(-lam.astype(jnp.float32))
    a = jnp.exp(log_a)
    u = jnp.sqrt(-jnp.expm1(2.0 * log_a)) * (i * x).astype(jnp.float32)

    def combine(left, right):
        a1, b1 = left
        a2, b2 = right
        return a1 * a2, a2 * b1 + b2

    _, h = lax.associative_scan(combine, (a, u), axis=1)
    return h.astype(x.dtype)


def spatial_gating(z, ln_g, ln_b, w_s, b_s):
    bsz, s, _ = z.shape
    u, v = jnp.split(z, 2, axis=-1)
    v = layernorm(v, ln_g, ln_b)
    vb = v.reshape(bsz, s // CHUNK, CHUNK, SGU_GROUPS, SGU_GROUP_DIM)
    mask = jnp.tril(jnp.ones((CHUNK, CHUNK), dtype=bool))
    ws = jnp.where(mask[None], w_s, jnp.zeros_like(w_s))
    sp = jnp.einsum('gts,bcsgd->bctgd', ws, vb) + jnp.transpose(b_s)[None, None, :, :, None]
    return u * sp.reshape(bsz, s, SGU_WIDTH)


def hierarchical_moe(h, w_rg, b_rg, w_re, b_re, w1, w3, w2):
    bsz, s, d = h.shape
    t = h.reshape(bsz * s, d)
    n_tok = t.shape[0]
    g_prob = jax.nn.softmax((t @ w_rg + b_rg).astype(jnp.float32), axis=-1)
    g_top, g_idx = lax.top_k(g_prob, 1)
    e_logits = (t @ w_re + b_re).astype(jnp.float32).reshape(n_tok, N_GROUPS, EXPERTS_PER_GROUP)
    e_sel = jnp.take_along_axis(e_logits, g_idx[:, :, None], axis=1)[:, 0]
    e_prob = jax.nn.softmax(e_sel, axis=-1)
    e_top, e_idx = lax.top_k(e_prob, TOP_K)
    e_top = e_top / jnp.sum(e_top, axis=-1, keepdims=True)
    gate = g_top * e_top
    gid = g_idx * EXPERTS_PER_GROUP + e_idx
    combine = jnp.sum(jax.nn.one_hot(gid, N_EXPERTS, dtype=jnp.float32) * gate[..., None], axis=1)
    hid = jax.nn.silu(jnp.einsum('td,edf->tef', t, w1)) * jnp.einsum('td,edf->tef', t, w3)
    hid = hid * combine[:, :, None].astype(hid.dtype)
    out = jnp.einsum('tef,efd->td', hid, w2)
    return out.reshape(bsz, s, d)


def setup_inputs(seed: int = 0) -> dict:
    key = jax.random.key(seed)
    ks = jax.random.split(key, 32)
    f32 = jnp.float32
    L, D = DEPTH, D_MODEL

    def nrm(k, shape, fan_in):
        return jax.random.normal(k, shape, f32) * (fan_in ** -0.5)

    def gain(k, shape):
        return 1.0 + 0.02 * jax.random.normal(k, shape, f32)

    def bias(k, shape):
        return 0.01 * jax.random.normal(k, shape, f32)

    a_c = jax.random.uniform(ks[9], (L, LRU_WIDTH), f32, 0.9, 0.999)
    a0 = a_c ** (1.0 / LRU_C)
    lam = jnp.log(a0) - jnp.log1p(-a0)

    return {
        "x": jax.random.normal(ks[0], (BATCH, SEQ, D), f32),
        "p": jax.random.normal(ks[1], (DEPTH, BATCH, SEQ, PLE_DIM), f32),
        "mix_norm": gain(ks[2], (L, D)),
        "w_in": nrm(ks[3], (L, D, IN_COLS), D),
        "conv_w": nrm(ks[4], (L, CONV_WIDTH, LRU_WIDTH), CONV_WIDTH),
        "conv_b": bias(ks[5], (L, LRU_WIDTH)),
        "lru_wa": nrm(ks[6], (L, LRU_BLOCKS, LRU_BLOCK_DIM, LRU_BLOCK_DIM), LRU_BLOCK_DIM),
        "lru_ba": bias(ks[7], (L, LRU_WIDTH)),
        "lru_wi": nrm(ks[8], (L, LRU_BLOCKS, LRU_BLOCK_DIM, LRU_BLOCK_DIM), LRU_BLOCK_DIM),
        "lru_bi": bias(ks[10], (L, LRU_WIDTH)),
        "lru_lambda": lam,
        "sgu_ln_g": gain(ks[11], (L, SGU_WIDTH)),
        "sgu_ln_b": bias(ks[12], (L, SGU_WIDTH)),
        "sgu_ws": nrm(ks[13], (L, SGU_GROUPS, CHUNK, CHUNK), CHUNK),
        "sgu_bs": gain(ks[14], (L, SGU_GROUPS, CHUNK)),
        "w_out": nrm(ks[15], (L, D, D), D),
        "ffn_norm": gain(ks[16], (L, D)),
        "router_group_w": nrm(ks[17], (L, D, N_GROUPS), D),
        "router_group_b": bias(ks[18], (L, N_GROUPS)),
        "router_expert_w": nrm(ks[19], (L, D, N_EXPERTS), D),
        "router_expert_b": bias(ks[20], (L, N_EXPERTS)),
        "expert_w1": nrm(ks[21], (L, N_EXPERTS, D, D_EXPERT), D),
        "expert_w3": nrm(ks[22], (L, N_EXPERTS, D, D_EXPERT), D),
        "expert_w2": nrm(ks[23], (L, N_EXPERTS, D_EXPERT, D), D_EXPERT),
        "ple_norm": gain(ks[24], (L, D)),
        "ple_gate_w": nrm(ks[25], (L, D, D), D),
        "ple_up_w": nrm(ks[26], (L, PLE_DIM, D), PLE_DIM),
        "final_norm": gain(ks[27], (D,)),
    }


def reference(x, p, mix_norm, w_in, conv_w, conv_b, lru_wa, lru_ba, lru_wi, lru_bi, lru_lambda,
              sgu_ln_g, sgu_ln_b, sgu_ws, sgu_bs, w_out, ffn_norm, router_group_w, router_group_b,
              router_expert_w, router_expert_b, expert_w1, expert_w3, expert_w2, ple_norm,
              ple_gate_w, ple_up_w, final_norm):
    splits = [LRU_WIDTH, 2 * LRU_WIDTH, 2 * LRU_WIDTH + 2 * SGU_WIDTH]
    for l in range(DEPTH):
        h = rmsnorm(x, mix_norm[l])
        z = h @ w_in[l]
        z_lru, z_gate, z_sgu, z_merge = jnp.split(z, splits, axis=-1)
        xa = causal_depthwise_conv(z_lru, conv_w[l], conv_b[l])
        y_a = rg_lru(xa, lru_wa[l], lru_ba[l], lru_wi[l], lru_bi[l], lru_lambda[l]) * jax.nn.gelu(z_gate)
        y_b = spatial_gating(jax.nn.gelu(z_sgu), sgu_ln_g[l], sgu_ln_b[l], sgu_ws[l], sgu_bs[l])
        m_a, m_b = jnp.split(z_merge, 2, axis=-1)
        merged = jax.nn.sigmoid(m_a) * y_a + jax.nn.sigmoid(m_b) * y_b
        x = x + merged @ w_out[l]
        h = rmsnorm(x, ffn_norm[l])
        x = x + hierarchical_moe(h, router_group_w[l], router_group_b[l], router_expert_w[l],
                                 router_expert_b[l], expert_w1[l], expert_w3[l], expert_w2[l])
        g = jax.nn.sigmoid(rmsnorm(x, ple_norm[l]) @ ple_gate_w[l])
        x = x + g * (p[l] @ ple_up_w[l])
    return rmsnorm(x, final_norm)
```

```python
import functools

import jax
import jax.numpy as jnp
from jax import lax
from jax.experimental import pallas as pl
from jax.experimental.pallas import tpu as pltpu

F32 = jnp.float32
BF16 = jnp.bfloat16
U32 = jnp.uint32

LRU_BLOCKS = 16
CONV_WIDTH = 4
LRU_C = 8.0
SGU_GROUPS = 8
CHUNK = 128
N_GROUPS = 4
EXPERTS_PER_GROUP = 8
N_EXPERTS = N_GROUPS * EXPERTS_PER_GROUP
TOP_K = 2
EPS = 1e-6

V7X_MXU_DIM = 256
V7X_SUBLANES = 8
V7X_LANES = 128
V7X_VMEM_BYTES = 64 * 1024 * 1024

ROUTER_ROWS = V7X_LANES
EXPERT_ROW0 = V7X_SUBLANES


def _tiles():
    return dict(
        mixer_rows=256,
        expert_rows=256,
        ple_rows=512,
        mixer_vmem=52 * 1024 * 1024,
        expert_vmem=40 * 1024 * 1024,
        ple_vmem=40 * 1024 * 1024,
    )


def _dot(a, b):
    return jnp.dot(a, b, preferred_element_type=F32)


def _sigmoid(x):
    return 0.5 * jnp.tanh(0.5 * x) + 0.5


def _gelu_tanh(x):
    c = 0.7978845608028654
    return 0.5 * x * (1.0 + jnp.tanh(c * (x + 0.044715 * (x * x * x))))


def _rmsnorm(x, g):
    ms = jnp.mean(x * x, axis=-1, keepdims=True)
    return x * lax.rsqrt(ms + EPS) * g


def _pack_bf16_pair(lo, hi):
    lo_b = lax.bitcast_convert_type(lo.astype(BF16).astype(F32), U32)
    hi_b = lax.bitcast_convert_type(hi.astype(BF16).astype(F32), U32)
    return (hi_b & jnp.uint32(0xFFFF0000)) | lax.shift_right_logical(lo_b, jnp.uint32(16))


def _unpack_bf16_pair(w):
    lo = lax.bitcast_convert_type(lax.shift_left(w, jnp.uint32(16)), F32)
    hi = lax.bitcast_convert_type(w & jnp.uint32(0xFFFF0000), F32)
    return lo, hi


def _const_spec(shape):
    zeros = (0,) * len(shape)
    return pl.BlockSpec(shape, lambda *_: zeros, pipeline_mode=pl.Buffered(1))


def _lru_scan(a, u, h0):
    rows, d = a.shape
    n = rows // V7X_SUBLANES
    a3 = a.reshape(n, V7X_SUBLANES, d)
    u3 = u.reshape(n, V7X_SUBLANES, d)
    row = lax.broadcasted_iota(jnp.int32, a3.shape, 1)
    shift = 1
    while shift < V7X_SUBLANES:
        keep = row >= shift
        a_sh = pltpu.roll(a3, shift, axis=1)
        u_sh = pltpu.roll(u3, shift, axis=1)
        u3 = jnp.where(keep, a3 * u_sh + u3, u3)
        a3 = jnp.where(keep, a3 * a_sh, a3)
        shift *= 2
    out = []
    h = h0
    for g in range(n):
        hg = a3[g] * h + u3[g]
        out.append(hg)
        h = hg[V7X_SUBLANES - 1:V7X_SUBLANES, :]
    return jnp.concatenate(out, axis=0), h


def _mixer_kernel(x_ref, mixn_ref, win_ref, convw_ref, convb_ref, wa_ref, ba_ref, wi_ref, bi_ref,
                  lam_ref, lng_ref, lnb_ref, ws_ref, bst_ref, wout_ref, ffn_ref, wr_ref, br_ref,
                  x1_ref, hp_ref, route_ref, cnt_ref,
                  zprev_ref, hcar_ref, ccar_ref):
    b = pl.program_id(0)
    s = pl.program_id(1)
    rows, d = x_ref.shape

    @pl.when(s == 0)
    def _():
        zprev_ref[...] = jnp.zeros_like(zprev_ref)
        hcar_ref[...] = jnp.zeros_like(hcar_ref)

    @pl.when((b == 0) & (s == 0))
    def _():
        ccar_ref[...] = jnp.zeros_like(ccar_ref)

    x = x_ref[...]
    h = _rmsnorm(x, mixn_ref[...]).astype(BF16)

    def proj(k):
        return _dot(h, win_ref[:, k * d:(k + 1) * d])

    z_lru = proj(0)
    zcat = jnp.concatenate([zprev_ref[...], z_lru], axis=0)
    zprev_ref[...] = z_lru[rows - V7X_SUBLANES:, :]
    xa = convb_ref[...] + convw_ref[CONV_WIDTH - 1:CONV_WIDTH, :] * z_lru
    for j in range(1, CONV_WIDTH):
        shifted = pltpu.roll(zcat, j, axis=0)[V7X_SUBLANES:, :]
        xa = xa + convw_ref[CONV_WIDTH - 1 - j:CONV_WIDTH - j, :] * shifted
    xa_bf = xa.astype(BF16)
    nblk = d // V7X_MXU_DIM

    def blockdiag(w_ref):
        return jnp.concatenate(
            [_dot(xa_bf[:, j * V7X_MXU_DIM:(j + 1) * V7X_MXU_DIM], w_ref[j])
             for j in range(nblk)], axis=1)

    r = _sigmoid(blockdiag(wa_ref) + ba_ref[...])
    gate_i = _sigmoid(blockdiag(wi_ref) + bi_ref[...])
    neg_lam = -lam_ref[...]
    softplus = jnp.maximum(neg_lam, 0.0) + jnp.log1p(jnp.exp(-jnp.abs(neg_lam)))
    log_a = (-LRU_C) * r * softplus
    a = jnp.exp(log_a)
    u = jnp.sqrt(1.0 - a * a) * (gate_i * xa)
    hseq, hlast = _lru_scan(a, u, hcar_ref[...])
    hcar_ref[...] = hlast
    y_a = hseq * _gelu_tanh(proj(1))
    merged = _sigmoid(proj(4)) * y_a

    gv = _gelu_tanh(proj(3))
    mu = jnp.mean(gv, axis=-1, keepdims=True)
    xc = gv - mu
    var = jnp.mean(xc * xc, axis=-1, keepdims=True)
    v_bf = (xc * lax.rsqrt(var + EPS) * lng_ref[...] + lnb_ref[...]).astype(BF16)
    t_idx = lax.broadcasted_iota(jnp.int32, (CHUNK, CHUNK), 0)
    s_idx = lax.broadcasted_iota(jnp.int32, (CHUNK, CHUNK), 1)
    causal = t_idx >= s_idx
    gdim = d // SGU_GROUPS
    sp_cols = []
    for g in range(SGU_GROUPS):
        wsg = jnp.where(causal, ws_ref[g], 0.0).astype(BF16)
        bias = bst_ref[:, g:g + 1]
        sp_rows = []
        for c in range(rows // CHUNK):
            vb = v_bf[c * CHUNK:(c + 1) * CHUNK, g * gdim:(g + 1) * gdim]
            sp_rows.append(_dot(wsg, vb) + bias)
        sp_cols.append(jnp.concatenate(sp_rows, axis=0))
    sp = jnp.concatenate(sp_cols, axis=1)
    y_b = _gelu_tanh(proj(2)) * sp
    merged = merged + _sigmoid(proj(5)) * y_b

    x1 = x + _dot(merged.astype(BF16), wout_ref[...])
    x1_ref[...] = x1

    hn = _rmsnorm(x1, ffn_ref[...])
    half = d // 2
    hp_ref[...] = _pack_bf16_pair(hn[:, :half], hn[:, half:])
    logits = _dot(hn.astype(BF16), wr_ref[...])
    lt = jnp.transpose(logits) + br_ref[...]
    sub = lax.broadcasted_iota(jnp.int32, (V7X_SUBLANES, rows), 0)
    subf = sub.astype(F32)
    big = jnp.float32(1e9)

    lg = jnp.where(sub < N_GROUPS, lt[0:V7X_SUBLANES, :], -jnp.inf)
    g_exp = jnp.exp(lg - jnp.max(lg, axis=0, keepdims=True))
    g_prob = g_exp / jnp.sum(g_exp, axis=0, keepdims=True)
    g_top = jnp.max(g_prob, axis=0, keepdims=True)
    g_idx = jnp.min(jnp.where(g_prob == g_top, subf, big), axis=0, keepdims=True)

    e_sel = jnp.zeros((EXPERTS_PER_GROUP, rows), F32)
    for g in range(N_GROUPS):
        r0 = EXPERT_ROW0 + g * EXPERTS_PER_GROUP
        e_sel = jnp.where(g_idx == g, lt[r0:r0 + EXPERTS_PER_GROUP, :], e_sel)
    e_exp = jnp.exp(e_sel - jnp.max(e_sel, axis=0, keepdims=True))
    e_prob = e_exp / jnp.sum(e_exp, axis=0, keepdims=True)
    p1 = jnp.max(e_prob, axis=0, keepdims=True)
    i1 = jnp.min(jnp.where(e_prob == p1, subf, big), axis=0, keepdims=True)
    rest = jnp.where(subf == i1, -1.0, e_prob)
    p2 = jnp.max(rest, axis=0, keepdims=True)
    i2 = jnp.min(jnp.where(rest == p2, subf, big), axis=0, keepdims=True)
    psum = p1 + p2
    gate1 = g_top * (p1 / psum)
    gate2 = g_top * (p2 / psum)
    gid1 = g_idx * EXPERTS_PER_GROUP + i1
    gid2 = g_idx * EXPERTS_PER_GROUP + i2

    eid = lax.broadcasted_iota(jnp.int32, (N_EXPERTS, rows), 0).astype(F32)
    hit1 = eid == gid1
    hit2 = eid == gid2
    cnt = jnp.where(hit1 | hit2, 1.0, 0.0)
    before = (lax.broadcasted_iota(jnp.int32, (rows, rows), 0)
              < lax.broadcasted_iota(jnp.int32, (rows, rows), 1))
    excl = _dot(cnt.astype(BF16), jnp.where(before, 1.0, 0.0).astype(BF16))
    base = ccar_ref[:, 0:1] + excl
    rank1 = jnp.sum(jnp.where(hit1, base, 0.0), axis=0, keepdims=True)
    rank2 = jnp.sum(jnp.where(hit2, base, 0.0), axis=0, keepdims=True)
    total = ccar_ref[...] + jnp.sum(cnt, axis=1, keepdims=True)
    ccar_ref[...] = total
    cnt_ref[...] = total
    zero = jnp.zeros_like(gate1)
    route_ref[...] = jnp.concatenate([gid1, gid2, rank1, rank2, gate1, gate2, zero, zero], axis=0)


def _mixer_call(x, mix_norm, w_in, conv_w, conv_b, wa_blk, ba, wi_blk, bi, lam, ln_g, ln_b, ws, bs_t,
                w_out, ffn_norm, w_router, b_router):
    cfg = _tiles()
    bsz, seq, d = x.shape
    ts = cfg["mixer_rows"]
    ntok = bsz * seq
    nseq = seq // ts
    grid = (bsz, nseq)
    row1 = (1, d)
    in_specs = [
        pl.BlockSpec((None, ts, d), lambda b, s: (b, s, 0)),
        _const_spec(row1),
        _const_spec(w_in.shape),
        _const_spec(conv_w.shape), _const_spec(row1),
        _const_spec(wa_blk.shape), _const_spec(row1),
        _const_spec(wi_blk.shape), _const_spec(row1),
        _const_spec(row1),
        _const_spec(row1), _const_spec(row1),
        _const_spec(ws.shape), _const_spec(bs_t.shape),
        _const_spec(w_out.shape), _const_spec(row1),
        _const_spec(w_router.shape), _const_spec(b_router.shape),
    ]
    out_shape = [
        jax.ShapeDtypeStruct((bsz, seq, d), F32),
        jax.ShapeDtypeStruct((ntok, d // 2), U32),
        jax.ShapeDtypeStruct((V7X_SUBLANES, ntok), F32),
        jax.ShapeDtypeStruct((N_EXPERTS, V7X_LANES), F32),
    ]
    out_specs = [
        pl.BlockSpec((None, ts, d), lambda b, s: (b, s, 0)),
        pl.BlockSpec((ts, d // 2), lambda b, s: (b * nseq + s, 0)),
        pl.BlockSpec((V7X_SUBLANES, ts), lambda b, s: (0, b * nseq + s)),
        pl.BlockSpec((N_EXPERTS, V7X_LANES), lambda b, s: (0, 0)),
    ]
    scratch = [
        pltpu.VMEM((V7X_SUBLANES, d), F32),
        pltpu.VMEM((1, d), F32),
        pltpu.VMEM((N_EXPERTS, V7X_LANES), F32),
    ]
    return pl.pallas_call(
        _mixer_kernel,
        grid=grid,
        in_specs=in_specs,
        out_specs=out_specs,
        out_shape=out_shape,
        scratch_shapes=scratch,
        compiler_params=pltpu.CompilerParams(
            dimension_semantics=("arbitrary", "arbitrary"),
            vmem_limit_bytes=cfg["mixer_vmem"]),
        name="mixer",
    )(x, mix_norm, w_in, conv_w, conv_b, wa_blk, ba, wi_blk, bi, lam, ln_g, ln_b, ws, bs_t,
      w_out, ffn_norm, w_router, b_router)


def _expert_kernel(te_ref, nt_ref, hs_ref, w1_ref, w3_ref, w2_ref, ys_ref):
    del te_ref

    @pl.when(pl.program_id(0) < nt_ref[0])
    def _():
        lo, hi = _unpack_bf16_pair(hs_ref[...])
        h = jnp.concatenate([lo, hi], axis=1).astype(BF16)
        a = _dot(h, w1_ref[...].astype(BF16))
        b = _dot(h, w3_ref[...].astype(BF16))
        hid = (a * _sigmoid(a)) * b
        y = _dot(hid.astype(BF16), w2_ref[...].astype(BF16))
        half = y.shape[1] // 2
        ys_ref[...] = _pack_bf16_pair(y[:, :half], y[:, half:])


def _expert_call(tile_expert, n_tiles, hs, w1, w3, w2):
    cfg = _tiles()
    tm = cfg["expert_rows"]
    prow, half = hs.shape
    _, d, f = w1.shape
    max_tiles = prow // tm

    def row_map(i, te, nt):
        return (jnp.minimum(i, nt[0] - 1), 0)

    def w_map(i, te, nt):
        return (te[i], 0, 0)

    grid_spec = pltpu.PrefetchScalarGridSpec(
        num_scalar_prefetch=2,
        grid=(max_tiles,),
        in_specs=[
            pl.BlockSpec((tm, half), row_map),
            pl.BlockSpec((None, d, f), w_map),
            pl.BlockSpec((None, d, f), w_map),
            pl.BlockSpec((None, f, d), w_map),
        ],
        out_specs=pl.BlockSpec((tm, half), row_map),
    )
    return pl.pallas_call(
        _expert_kernel,
        grid_spec=grid_spec,
        out_shape=jax.ShapeDtypeStruct((prow, half), U32),
        compiler_params=pltpu.CompilerParams(
            dimension_semantics=("arbitrary",),
            vmem_limit_bytes=cfg["expert_vmem"]),
        name="experts",
    )(tile_expert, n_tiles, hs, w1, w3, w2)


def _ple_kernel(x1_ref, yg_ref, gate_ref, p_ref, plen_ref, wg_ref, wu_ref, fin_ref, o_ref):
    half = x1_ref.shape[1] // 2
    yg = yg_ref[...]
    lo0, hi0 = _unpack_bf16_pair(yg[:, :half])
    lo1, hi1 = _unpack_bf16_pair(yg[:, half:])
    g0 = gate_ref[:, 0:1]
    g1 = gate_ref[:, 1:2]
    moe = g0 * jnp.concatenate([lo0, hi0], axis=1) + g1 * jnp.concatenate([lo1, hi1], axis=1)
    x2 = x1_ref[...] + moe
    r = _rmsnorm(x2, plen_ref[...]).astype(BF16)
    gt = _sigmoid(_dot(r, wg_ref[...]))
    up = _dot(p_ref[...].astype(BF16), wu_ref[...])
    x3 = x2 + gt * up
    o_ref[...] = _rmsnorm(x3, fin_ref[...])


def _ple_call(x1, yg, gates, p, ple_norm, wg, wu, final_norm):
    cfg = _tiles()
    ntok, d = x1.shape
    tp = cfg["ple_rows"]
    pdim = p.shape[1]
    return pl.pallas_call(
        _ple_kernel,
        grid=(ntok // tp,),
        in_specs=[
            pl.BlockSpec((tp, d), lambda i: (i, 0)),
            pl.BlockSpec((tp, d), lambda i: (i, 0)),
            pl.BlockSpec((tp, TOP_K), lambda i: (i, 0)),
            pl.BlockSpec((tp, pdim), lambda i: (i, 0)),
            _const_spec((1, d)),
            _const_spec(wg.shape),
            _const_spec(wu.shape),
            _const_spec((1, d)),
        ],
        out_specs=pl.BlockSpec((tp, d), lambda i: (i, 0)),
        out_shape=jax.ShapeDtypeStruct((ntok, d), F32),
        compiler_params=pltpu.CompilerParams(
            dimension_semantics=("arbitrary",),
            vmem_limit_bytes=cfg["ple_vmem"]),
        name="ple",
    )(x1, yg, gates, p, ple_norm, wg, wu, final_norm)


def _blockdiag_pack(w):
    nb, bd, _ = w.shape
    per = V7X_MXU_DIM // bd
    w4 = w.reshape(nb // per, per, bd, bd)
    eye = jnp.eye(per, dtype=w.dtype)
    out = jnp.einsum("jpab,pq->jpaqb", w4, eye)
    return out.reshape(nb // per, V7X_MXU_DIM, V7X_MXU_DIM).astype(BF16)


def kernel(x, p, mix_norm, w_in, conv_w, conv_b, lru_wa, lru_ba, lru_wi, lru_bi, lru_lambda, sgu_ln_g, sgu_ln_b, sgu_ws, sgu_bs, w_out, ffn_norm, router_group_w, router_group_b, router_expert_w, router_expert_b, expert_w1, expert_w3, expert_w2, ple_norm, ple_gate_w, ple_up_w, final_norm):
    cfg = _tiles()
    bsz, seq, d = x.shape
    ntok = bsz * seq
    tm = cfg["expert_rows"]
    depth = w_in.shape[0]
    assert depth == 1, "the ple kernel applies the final norm, so it must be the last layer"
    for l in range(depth):
        w_router = jnp.zeros((d, ROUTER_ROWS), F32)
        w_router = w_router.at[:, :N_GROUPS].set(router_group_w[l])
        w_router = w_router.at[:, EXPERT_ROW0:EXPERT_ROW0 + N_EXPERTS].set(router_expert_w[l])
        b_router = jnp.zeros((ROUTER_ROWS, 1), F32)
        b_router = b_router.at[:N_GROUPS, 0].set(router_group_b[l])
        b_router = b_router.at[EXPERT_ROW0:EXPERT_ROW0 + N_EXPERTS, 0].set(router_expert_b[l])
        x1, hp, route, cnt = _mixer_call(
            x, mix_norm[l][None], w_in[l].astype(BF16), conv_w[l], conv_b[l][None],
            _blockdiag_pack(lru_wa[l]), lru_ba[l][None], _blockdiag_pack(lru_wi[l]), lru_bi[l][None],
            lru_lambda[l][None], sgu_ln_g[l][None], sgu_ln_b[l][None], sgu_ws[l],
            jnp.transpose(sgu_bs[l]), w_out[l].astype(BF16), ffn_norm[l][None],
            w_router.astype(BF16), b_router)

        counts = cnt[:, 0].astype(jnp.int32)
        padded = ((counts + tm - 1) // tm) * tm
        ends = jnp.cumsum(padded)
        offs = ends - padded
        gid = route[0:TOP_K].astype(jnp.int32)
        rank = route[TOP_K:2 * TOP_K].astype(jnp.int32)
        pos = jnp.take(offs, gid) + rank
        prow = ntok * TOP_K + N_EXPERTS * tm
        max_tiles = prow // tm
        n_tiles = (ends[-1] // tm).astype(jnp.int32)[None]
        tile_row0 = jnp.arange(max_tiles, dtype=jnp.int32) * tm
        tile_expert = jnp.minimum(
            jnp.sum((ends[None, :] <= tile_row0[:, None]).astype(jnp.int32), axis=1), N_EXPERTS - 1)

        tok = jnp.broadcast_to(jnp.arange(ntok, dtype=jnp.int32)[None], (TOP_K, ntok))
        src = jnp.zeros((prow,), jnp.int32).at[pos.reshape(-1)].set(tok.reshape(-1))
        hs = jnp.take(hp, src, axis=0)
        ys = _expert_call(tile_expert, n_tiles, hs, expert_w1[l], expert_w3[l], expert_w2[l])
        yg = jnp.take(ys, jnp.transpose(pos).reshape(-1), axis=0).reshape(ntok, d)
        gates = jnp.transpose(route[2 * TOP_K:3 * TOP_K])

        out = _ple_call(x1.reshape(ntok, d), yg, gates, p[l].reshape(ntok, -1), ple_norm[l][None],
                        ple_gate_w[l].astype(BF16), ple_up_w[l].astype(BF16), final_norm[None])
        x = out.reshape(bsz, seq, d)
    return x
```

```python
import functools

import jax
import jax.numpy as jnp
from jax import lax
from jax.experimental import pallas as pl
from jax.experimental.pallas import tpu as pltpu
from jax.experimental.pallas import tpu_sc as plsc

F32 = jnp.float32
BF16 = jnp.bfloat16
U32 = jnp.uint32

LRU_BLOCKS = 16
CONV_WIDTH = 4
LRU_C = 8.0
SGU_GROUPS = 8
CHUNK = 128
N_GROUPS = 4
EXPERTS_PER_GROUP = 8
N_EXPERTS = N_GROUPS * EXPERTS_PER_GROUP
TOP_K = 2
EPS = 1e-6

V7X_MXU_DIM = 256
V7X_SUBLANES = 8
V7X_LANES = 128
V7X_VMEM_BYTES = 64 * 1024 * 1024
V7X_SC_CORES = 2
V7X_SC_SUBCORES = 16

ROUTER_ROWS = V7X_LANES
EXPERT_ROW0 = V7X_SUBLANES


def _tiles():
    return dict(
        mixer_rows=256,
        expert_rows=256,
        ple_rows=512,
        sc_rows=64,
        mixer_vmem=52 * 1024 * 1024,
        expert_vmem=40 * 1024 * 1024,
        ple_vmem=40 * 1024 * 1024,
    )


def _dot(a, b):
    return jnp.dot(a, b, preferred_element_type=F32)


def _sigmoid(x):
    return 0.5 * jnp.tanh(0.5 * x) + 0.5


def _gelu_tanh(x):
    c = 0.7978845608028654
    return 0.5 * x * (1.0 + jnp.tanh(c * (x + 0.044715 * (x * x * x))))


def _rmsnorm(x, g):
    ms = jnp.mean(x * x, axis=-1, keepdims=True)
    return x * lax.rsqrt(ms + EPS) * g


def _pack_bf16_pair(lo, hi):
    lo_b = lax.bitcast_convert_type(lo.astype(BF16).astype(F32), U32)
    hi_b = lax.bitcast_convert_type(hi.astype(BF16).astype(F32), U32)
    return (hi_b & jnp.uint32(0xFFFF0000)) | lax.shift_right_logical(lo_b, jnp.uint32(16))


def _unpack_bf16_pair(w):
    lo = lax.bitcast_convert_type(lax.shift_left(w, jnp.uint32(16)), F32)
    hi = lax.bitcast_convert_type(w & jnp.uint32(0xFFFF0000), F32)
    return lo, hi


def _const_spec(shape):
    zeros = (0,) * len(shape)
    return pl.BlockSpec(shape, lambda *_: zeros, pipeline_mode=pl.Buffered(1))


def _lru_scan(a, u, h0):
    rows, d = a.shape
    n = rows // V7X_SUBLANES
    a3 = a.reshape(n, V7X_SUBLANES, d)
    u3 = u.reshape(n, V7X_SUBLANES, d)
    row = lax.broadcasted_iota(jnp.int32, a3.shape, 1)
    shift = 1
    while shift < V7X_SUBLANES:
        keep = row >= shift
        a_sh = pltpu.roll(a3, shift, axis=1)
        u_sh = pltpu.roll(u3, shift, axis=1)
        u3 = jnp.where(keep, a3 * u_sh + u3, u3)
        a3 = jnp.where(keep, a3 * a_sh, a3)
        shift *= 2
    out = []
    h = h0
    for g in range(n):
        hg = a3[g] * h + u3[g]
        out.append(hg)
        h = hg[V7X_SUBLANES - 1:V7X_SUBLANES, :]
    return jnp.concatenate(out, axis=0), h


def _mixer_kernel(x_ref, mixn_ref, win_ref, convw_ref, convb_ref, wa_ref, ba_ref, wi_ref, bi_ref,
                  lam_ref, lng_ref, lnb_ref, ws_ref, bst_ref, wout_ref, ffn_ref, wr_ref, br_ref,
                  x1_ref, hp_ref, pos_ref, gate_ref, cnt_ref,
                  zprev_ref, hcar_ref, ccar_ref, *, expert_capacity):
    b = pl.program_id(0)
    s = pl.program_id(1)
    rows, d = x_ref.shape

    @pl.when(s == 0)
    def _():
        zprev_ref[...] = jnp.zeros_like(zprev_ref)
        hcar_ref[...] = jnp.zeros_like(hcar_ref)

    @pl.when((b == 0) & (s == 0))
    def _():
        ccar_ref[...] = jnp.zeros_like(ccar_ref)

    x = x_ref[...]
    h = _rmsnorm(x, mixn_ref[...]).astype(BF16)

    def proj(k):
        return _dot(h, win_ref[:, k * d:(k + 1) * d])

    z_lru = proj(0)
    zcat = jnp.concatenate([zprev_ref[...], z_lru], axis=0)
    zprev_ref[...] = z_lru[rows - V7X_SUBLANES:, :]
    xa = convb_ref[...] + convw_ref[CONV_WIDTH - 1:CONV_WIDTH, :] * z_lru
    for j in range(1, CONV_WIDTH):
        shifted = pltpu.roll(zcat, j, axis=0)[V7X_SUBLANES:, :]
        xa = xa + convw_ref[CONV_WIDTH - 1 - j:CONV_WIDTH - j, :] * shifted
    xa_bf = xa.astype(BF16)
    nblk = d // V7X_MXU_DIM

    def blockdiag(w_ref):
        return jnp.concatenate(
            [_dot(xa_bf[:, j * V7X_MXU_DIM:(j + 1) * V7X_MXU_DIM], w_ref[j])
             for j in range(nblk)], axis=1)

    r = _sigmoid(blockdiag(wa_ref) + ba_ref[...])
    gate_i = _sigmoid(blockdiag(wi_ref) + bi_ref[...])
    neg_lam = -lam_ref[...]
    softplus = jnp.maximum(neg_lam, 0.0) + jnp.log1p(jnp.exp(-jnp.abs(neg_lam)))
    log_a = (-LRU_C) * r * softplus
    a = jnp.exp(log_a)
    u = jnp.sqrt(1.0 - a * a) * (gate_i * xa)
    hseq, hlast = _lru_scan(a, u, hcar_ref[...])
    hcar_ref[...] = hlast
    y_a = hseq * _gelu_tanh(proj(1))
    merged = _sigmoid(proj(4)) * y_a

    gv = _gelu_tanh(proj(3))
    mu = jnp.mean(gv, axis=-1, keepdims=True)
    xc = gv - mu
    var = jnp.mean(xc * xc, axis=-1, keepdims=True)
    v_bf = (xc * lax.rsqrt(var + EPS) * lng_ref[...] + lnb_ref[...]).astype(BF16)
    t_idx = lax.broadcasted_iota(jnp.int32, (CHUNK, CHUNK), 0)
    s_idx = lax.broadcasted_iota(jnp.int32, (CHUNK, CHUNK), 1)
    causal = t_idx >= s_idx
    gdim = d // SGU_GROUPS
    sp_cols = []
    for g in range(SGU_GROUPS):
        wsg = jnp.where(causal, ws_ref[g], 0.0).astype(BF16)
        bias = bst_ref[:, g:g + 1]
        sp_rows = []
        for c in range(rows // CHUNK):
            vb = v_bf[c * CHUNK:(c + 1) * CHUNK, g * gdim:(g + 1) * gdim]
            sp_rows.append(_dot(wsg, vb) + bias)
        sp_cols.append(jnp.concatenate(sp_rows, axis=0))
    sp = jnp.concatenate(sp_cols, axis=1)
    y_b = _gelu_tanh(proj(2)) * sp
    merged = merged + _sigmoid(proj(5)) * y_b

    x1 = x + _dot(merged.astype(BF16), wout_ref[...])
    x1_ref[...] = x1

    hn = _rmsnorm(x1, ffn_ref[...])
    half = d // 2
    hp_ref[...] = _pack_bf16_pair(hn[:, :half], hn[:, half:])
    logits = _dot(hn.astype(BF16), wr_ref[...])
    lt = jnp.transpose(logits) + br_ref[...]
    sub = lax.broadcasted_iota(jnp.int32, (V7X_SUBLANES, rows), 0)
    subf = sub.astype(F32)
    big = jnp.float32(1e9)

    lg = jnp.where(sub < N_GROUPS, lt[0:V7X_SUBLANES, :], -jnp.inf)
    g_exp = jnp.exp(lg - jnp.max(lg, axis=0, keepdims=True))
    g_prob = g_exp / jnp.sum(g_exp, axis=0, keepdims=True)
    g_top = jnp.max(g_prob, axis=0, keepdims=True)
    g_idx = jnp.min(jnp.where(g_prob == g_top, subf, big), axis=0, keepdims=True)

    e_sel = jnp.zeros((EXPERTS_PER_GROUP, rows), F32)
    for g in range(N_GROUPS):
        r0 = EXPERT_ROW0 + g * EXPERTS_PER_GROUP
        e_sel = jnp.where(g_idx == g, lt[r0:r0 + EXPERTS_PER_GROUP, :], e_sel)
    e_exp = jnp.exp(e_sel - jnp.max(e_sel, axis=0, keepdims=True))
    e_prob = e_exp / jnp.sum(e_exp, axis=0, keepdims=True)
    p1 = jnp.max(e_prob, axis=0, keepdims=True)
    i1 = jnp.min(jnp.where(e_prob == p1, subf, big), axis=0, keepdims=True)
    rest = jnp.where(subf == i1, -1.0, e_prob)
    p2 = jnp.max(rest, axis=0, keepdims=True)
    i2 = jnp.min(jnp.where(rest == p2, subf, big), axis=0, keepdims=True)
    psum = p1 + p2
    gate1 = g_top * (p1 / psum)
    gate2 = g_top * (p2 / psum)
    gid1 = g_idx * EXPERTS_PER_GROUP + i1
    gid2 = g_idx * EXPERTS_PER_GROUP + i2

    eid = lax.broadcasted_iota(jnp.int32, (N_EXPERTS, rows), 0).astype(F32)
    hit1 = eid == gid1
    hit2 = eid == gid2
    cnt = jnp.where(hit1 | hit2, 1.0, 0.0)
    before = (lax.broadcasted_iota(jnp.int32, (rows, rows), 0)
              < lax.broadcasted_iota(jnp.int32, (rows, rows), 1))
    excl = _dot(cnt.astype(BF16), jnp.where(before, 1.0, 0.0).astype(BF16))
    base = ccar_ref[:, 0:1] + excl
    rank1 = jnp.sum(jnp.where(hit1, base, 0.0), axis=0, keepdims=True)
    rank2 = jnp.sum(jnp.where(hit2, base, 0.0), axis=0, keepdims=True)
    total = ccar_ref[...] + jnp.sum(cnt, axis=1, keepdims=True)
    ccar_ref[...] = total
    cnt_ref[...] = total
    cap = float(expert_capacity)
    zero = jnp.zeros((V7X_SUBLANES - TOP_K, rows), F32)
    pos = jnp.concatenate([gid1 * cap + rank1, gid2 * cap + rank2, zero], axis=0)
    pos_ref[...] = pos.astype(jnp.int32)
    gate_ref[...] = jnp.concatenate([gate1, gate2, zero], axis=0)


def _mixer_call(x, mix_norm, w_in, conv_w, conv_b, wa_blk, ba, wi_blk, bi, lam, ln_g, ln_b, ws, bs_t,
                w_out, ffn_norm, w_router, b_router):
    cfg = _tiles()
    bsz, seq, d = x.shape
    ts = cfg["mixer_rows"]
    ntok = bsz * seq
    nseq = seq // ts
    grid = (bsz, nseq)
    row1 = (1, d)
    in_specs = [
        pl.BlockSpec((None, ts, d), lambda b, s: (b, s, 0)),
        _const_spec(row1),
        _const_spec(w_in.shape),
        _const_spec(conv_w.shape), _const_spec(row1),
        _const_spec(wa_blk.shape), _const_spec(row1),
        _const_spec(wi_blk.shape), _const_spec(row1),
        _const_spec(row1),
        _const_spec(row1), _const_spec(row1),
        _const_spec(ws.shape), _const_spec(bs_t.shape),
        _const_spec(w_out.shape), _const_spec(row1),
        _const_spec(w_router.shape), _const_spec(b_router.shape),
    ]
    out_shape = [
        jax.ShapeDtypeStruct((bsz, seq, d), F32),
        jax.ShapeDtypeStruct((ntok, d // 2), U32),
        jax.ShapeDtypeStruct((V7X_SUBLANES, ntok), jnp.int32),
        jax.ShapeDtypeStruct((V7X_SUBLANES, ntok), F32),
        jax.ShapeDtypeStruct((N_EXPERTS, V7X_LANES), F32),
    ]
    out_specs = [
        pl.BlockSpec((None, ts, d), lambda b, s: (b, s, 0)),
        pl.BlockSpec((ts, d // 2), lambda b, s: (b * nseq + s, 0)),
        pl.BlockSpec((V7X_SUBLANES, ts), lambda b, s: (0, b * nseq + s)),
        pl.BlockSpec((V7X_SUBLANES, ts), lambda b, s: (0, b * nseq + s)),
        pl.BlockSpec((N_EXPERTS, V7X_LANES), lambda b, s: (0, 0)),
    ]
    scratch = [
        pltpu.VMEM((V7X_SUBLANES, d), F32),
        pltpu.VMEM((1, d), F32),
        pltpu.VMEM((N_EXPERTS, V7X_LANES), F32),
    ]
    return pl.pallas_call(
        functools.partial(_mixer_kernel, expert_capacity=ntok),
        grid=grid,
        in_specs=in_specs,
        out_specs=out_specs,
        out_shape=out_shape,
        scratch_shapes=scratch,
        compiler_params=pltpu.CompilerParams(
            dimension_semantics=("arbitrary", "arbitrary"),
            vmem_limit_bytes=cfg["mixer_vmem"]),
        name="mixer",
    )(x, mix_norm, w_in, conv_w, conv_b, wa_blk, ba, wi_blk, bi, lam, ln_g, ln_b, ws, bs_t,
      w_out, ffn_norm, w_router, b_router)


def _expert_kernel(te_ref, blk_ref, nt_ref, hs_ref, w1_ref, w3_ref, w2_ref, ys_ref):
    del te_ref, blk_ref

    @pl.when(pl.program_id(0) < nt_ref[0])
    def _():
        lo, hi = _unpack_bf16_pair(hs_ref[...])
        h = jnp.concatenate([lo, hi], axis=1).astype(BF16)
        a = _dot(h, w1_ref[...].astype(BF16))
        b = _dot(h, w3_ref[...].astype(BF16))
        hid = (a * _sigmoid(a)) * b
        y = _dot(hid.astype(BF16), w2_ref[...].astype(BF16))
        half = y.shape[1] // 2
        ys_ref[...] = _pack_bf16_pair(y[:, :half], y[:, half:])


def _expert_call(tile_expert, tile_block, n_tiles, hs, w1, w3, w2):
    cfg = _tiles()
    tm = cfg["expert_rows"]
    prow, half = hs.shape
    _, d, f = w1.shape
    max_tiles = tile_expert.shape[0]

    def row_map(i, te, blk, nt):
        return (blk[i], 0)

    def w_map(i, te, blk, nt):
        return (te[i], 0, 0)

    grid_spec = pltpu.PrefetchScalarGridSpec(
        num_scalar_prefetch=3,
        grid=(max_tiles,),
        in_specs=[
            pl.BlockSpec((tm, half), row_map),
            pl.BlockSpec((None, d, f), w_map),
            pl.BlockSpec((None, d, f), w_map),
            pl.BlockSpec((None, f, d), w_map),
        ],
        out_specs=pl.BlockSpec((tm, half), row_map),
    )
    return pl.pallas_call(
        _expert_kernel,
        grid_spec=grid_spec,
        out_shape=jax.ShapeDtypeStruct((prow, half), U32),
        compiler_params=pltpu.CompilerParams(
            dimension_semantics=("arbitrary",),
            vmem_limit_bytes=cfg["expert_vmem"]),
        name="experts",
    )(tile_expert, tile_block, n_tiles, hs, w1, w3, w2)


def _sc_mesh():
    return plsc.VectorSubcoreMesh(core_axis_name="c", subcore_axis_name="s",
                                  num_cores=V7X_SC_CORES, num_subcores=V7X_SC_SUBCORES)


def _sc_worker_id():
    return lax.axis_index("s") * V7X_SC_CORES + lax.axis_index("c")


def _dispatch_call(hp, pos_w, out_rows):
    cfg = _tiles()
    ntok, half = hp.shape
    nw, topk, nch, ch = pos_w.shape
    per_w = nch * ch

    def body(hp_hbm, pos_hbm, hs_hbm, idx_v, buf, sem):
        del sem
        wid = _sc_worker_id()
        pltpu.sync_copy(pos_hbm.at[wid], idx_v)

        @pl.loop(0, nch)
        def _(c):
            pltpu.sync_copy(hp_hbm.at[pl.ds(wid * per_w + c * ch, ch)], buf)
            for k in range(topk):
                pltpu.sync_copy(buf, hs_hbm.at[idx_v.at[k, c]])

    assert nw == V7X_SC_CORES * V7X_SC_SUBCORES and nw * per_w == ntok and ch == cfg["sc_rows"]
    return pl.kernel(
        body,
        out_type=jax.ShapeDtypeStruct((out_rows, half), U32),
        mesh=_sc_mesh(),
        scratch_types=[
            pltpu.VMEM((topk, nch, ch), jnp.int32),
            pltpu.VMEM((ch, half), U32),
            pltpu.SemaphoreType.DMA,
        ],
        name="dispatch",
    )(hp, pos_w)


def _combine_call(ys, pos_w):
    cfg = _tiles()
    _, half = ys.shape
    nw, topk, nch, ch = pos_w.shape
    per_w = nch * ch
    ntok = nw * per_w

    def body(ys_hbm, pos_hbm, *rest):
        outs = rest[:topk]
        idx_v, buf, sem = rest[topk:]
        del sem
        wid = _sc_worker_id()
        pltpu.sync_copy(pos_hbm.at[wid], idx_v)

        @pl.loop(0, nch)
        def _(c):
            for k in range(topk):
                pltpu.sync_copy(ys_hbm.at[idx_v.at[k, c]], buf)
                pltpu.sync_copy(buf, outs[k].at[pl.ds(wid * per_w + c * ch, ch)])

    assert nw == V7X_SC_CORES * V7X_SC_SUBCORES and ch == cfg["sc_rows"]
    return pl.kernel(
        body,
        out_type=[jax.ShapeDtypeStruct((ntok, half), U32)] * topk,
        mesh=_sc_mesh(),
        scratch_types=[
            pltpu.VMEM((topk, nch, ch), jnp.int32),
            pltpu.VMEM((ch, half), U32),
            pltpu.SemaphoreType.DMA,
        ],
        name="combine",
    )(ys, pos_w)


def _ple_kernel(x1_ref, yg0_ref, yg1_ref, gate_ref, p_ref, plen_ref, wg_ref, wu_ref, fin_ref, o_ref):
    lo0, hi0 = _unpack_bf16_pair(yg0_ref[...])
    lo1, hi1 = _unpack_bf16_pair(yg1_ref[...])
    g0 = gate_ref[:, 0:1]
    g1 = gate_ref[:, 1:2]
    moe = g0 * jnp.concatenate([lo0, hi0], axis=1) + g1 * jnp.concatenate([lo1, hi1], axis=1)
    x2 = x1_ref[...] + moe
    r = _rmsnorm(x2, plen_ref[...]).astype(BF16)
    gt = _sigmoid(_dot(r, wg_ref[...]))
    up = _dot(p_ref[...].astype(BF16), wu_ref[...])
    x3 = x2 + gt * up
    o_ref[...] = _rmsnorm(x3, fin_ref[...])


def _ple_call(x1, yg0, yg1, gates, p, ple_norm, wg, wu, final_norm):
    cfg = _tiles()
    ntok, d = x1.shape
    tp = cfg["ple_rows"]
    pdim = p.shape[1]
    return pl.pallas_call(
        _ple_kernel,
        grid=(ntok // tp,),
        in_specs=[
            pl.BlockSpec((tp, d), lambda i: (i, 0)),
            pl.BlockSpec((tp, d // 2), lambda i: (i, 0)),
            pl.BlockSpec((tp, d // 2), lambda i: (i, 0)),
            pl.BlockSpec((tp, TOP_K), lambda i: (i, 0)),
            pl.BlockSpec((tp, pdim), lambda i: (i, 0)),
            _const_spec((1, d)),
            _const_spec(wg.shape),
            _const_spec(wu.shape),
            _const_spec((1, d)),
        ],
        out_specs=pl.BlockSpec((tp, d), lambda i: (i, 0)),
        out_shape=jax.ShapeDtypeStruct((ntok, d), F32),
        compiler_params=pltpu.CompilerParams(
            dimension_semantics=("arbitrary",),
            vmem_limit_bytes=cfg["ple_vmem"]),
        name="ple",
    )(x1, yg0, yg1, gates, p, ple_norm, wg, wu, final_norm)


def _blockdiag_pack(w):
    nb, bd, _ = w.shape
    per = V7X_MXU_DIM // bd
    w4 = w.reshape(nb // per, per, bd, bd)
    eye = jnp.eye(per, dtype=w.dtype)
    out = jnp.einsum("jpab,pq->jpaqb", w4, eye)
    return out.reshape(nb // per, V7X_MXU_DIM, V7X_MXU_DIM).astype(BF16)


def _tile_map(counts, tm, cap, max_tiles):
    tiles_e = (counts + tm - 1) // tm
    ends = jnp.cumsum(tiles_e)
    starts = ends - tiles_e
    n_tiles = ends[-1]
    step = jnp.minimum(jnp.arange(max_tiles, dtype=jnp.int32), n_tiles - 1)
    expert = jnp.sum((ends[None, :] <= step[:, None]).astype(jnp.int32), axis=1)
    onehot = expert[:, None] == jnp.arange(N_EXPERTS, dtype=jnp.int32)[None, :]
    start = jnp.sum(jnp.where(onehot, starts[None, :], 0), axis=1)
    block = expert * (cap // tm) + (step - start)
    return expert, block, n_tiles[None]


def kernel(x, p, mix_norm, w_in, conv_w, conv_b, lru_wa, lru_ba, lru_wi, lru_bi, lru_lambda, sgu_ln_g, sgu_ln_b, sgu_ws, sgu_bs, w_out, ffn_norm, router_group_w, router_group_b, router_expert_w, router_expert_b, expert_w1, expert_w3, expert_w2, ple_norm, ple_gate_w, ple_up_w, final_norm):
    cfg = _tiles()
    bsz, seq, d = x.shape
    ntok = bsz * seq
    tm = cfg["expert_rows"]
    depth = w_in.shape[0]
    assert depth == 1, "the ple kernel applies the final norm, so it must be the last layer"
    l = 0
    w_router = jnp.zeros((d, ROUTER_ROWS), F32)
    w_router = w_router.at[:, :N_GROUPS].set(router_group_w[l])
    w_router = w_router.at[:, EXPERT_ROW0:EXPERT_ROW0 + N_EXPERTS].set(router_expert_w[l])
    b_router = jnp.zeros((ROUTER_ROWS, 1), F32)
    b_router = b_router.at[:N_GROUPS, 0].set(router_group_b[l])
    b_router = b_router.at[EXPERT_ROW0:EXPERT_ROW0 + N_EXPERTS, 0].set(router_expert_b[l])
    x1, hp, pos, gate, cnt = _mixer_call(
        x, mix_norm[l][None], w_in[l].astype(BF16), conv_w[l], conv_b[l][None],
        _blockdiag_pack(lru_wa[l]), lru_ba[l][None], _blockdiag_pack(lru_wi[l]), lru_bi[l][None],
        lru_lambda[l][None], sgu_ln_g[l][None], sgu_ln_b[l][None], sgu_ws[l],
        jnp.transpose(sgu_bs[l]), w_out[l].astype(BF16), ffn_norm[l][None],
        w_router.astype(BF16), b_router)

    cap = ntok
    max_tiles = (ntok * TOP_K) // tm + N_EXPERTS
    tile_expert, tile_block, n_tiles = _tile_map(cnt[:, 0].astype(jnp.int32), tm, cap, max_tiles)
    nw = V7X_SC_CORES * V7X_SC_SUBCORES
    ch = cfg["sc_rows"]
    pos_w = jnp.transpose(pos[:TOP_K].reshape(TOP_K, nw, ntok // (nw * ch), ch), (1, 0, 2, 3))

    hs = _dispatch_call(hp, pos_w, N_EXPERTS * cap)
    ys = _expert_call(tile_expert, tile_block, n_tiles, hs, expert_w1[l], expert_w3[l], expert_w2[l])
    yg0, yg1 = _combine_call(ys, pos_w)
    gates = jnp.transpose(gate[:TOP_K])

    out = _ple_call(x1.reshape(ntok, d), yg0, yg1, gates, p[l].reshape(ntok, -1), ple_norm[l][None],
                    ple_gate_w[l].astype(BF16), ple_up_w[l].astype(BF16), final_norm[None])
    return out.reshape(bsz, seq, d)
```

```python
import functools

import jax
import jax.numpy as jnp
from jax import lax
from jax.experimental import pallas as pl
from jax.experimental.pallas import tpu as pltpu
from jax.experimental.pallas import tpu_sc as plsc

F32 = jnp.float32
BF16 = jnp.bfloat16
U32 = jnp.uint32

LRU_BLOCKS = 16
CONV_WIDTH = 4
LRU_C = 8.0
SGU_GROUPS = 8
CHUNK = 128
N_GROUPS = 4
EXPERTS_PER_GROUP = 8
N_EXPERTS = N_GROUPS * EXPERTS_PER_GROUP
TOP_K = 2
EPS = 1e-6

V7X_MXU_DIM = 256
V7X_SUBLANES = 8
V7X_LANES = 128
V7X_VMEM_BYTES = 64 * 1024 * 1024
V7X_SC_CORES = 2
V7X_SC_SUBCORES = 16

ROUTER_ROWS = V7X_LANES
EXPERT_ROW0 = V7X_SUBLANES


def _tiles():
    return dict(
        mixer_rows=256,
        expert_rows=512,
        ple_rows=512,
        sc_rows=64,
        mixer_vmem=52 * 1024 * 1024,
        expert_vmem=40 * 1024 * 1024,
        ple_vmem=40 * 1024 * 1024,
    )


def _dot(a, b):
    return jnp.dot(a, b, preferred_element_type=F32)


def _sigmoid(x):
    return 0.5 * jnp.tanh(0.5 * x) + 0.5


def _gelu_tanh(x):
    c = 0.7978845608028654
    return 0.5 * x * (1.0 + jnp.tanh(c * (x + 0.044715 * (x * x * x))))


def _rmsnorm(x, g):
    ms = jnp.mean(x * x, axis=-1, keepdims=True)
    return x * lax.rsqrt(ms + EPS) * g


def _pack_bf16_pair(lo, hi):
    lo_b = lax.bitcast_convert_type(lo.astype(BF16).astype(F32), U32)
    hi_b = lax.bitcast_convert_type(hi.astype(BF16).astype(F32), U32)
    return (hi_b & jnp.uint32(0xFFFF0000)) | lax.shift_right_logical(lo_b, jnp.uint32(16))


def _unpack_bf16_pair(w):
    lo = lax.bitcast_convert_type(lax.shift_left(w, jnp.uint32(16)), F32)
    hi = lax.bitcast_convert_type(w & jnp.uint32(0xFFFF0000), F32)
    return lo, hi


def _const_spec(shape):
    zeros = (0,) * len(shape)
    return pl.BlockSpec(shape, lambda *_: zeros, pipeline_mode=pl.Buffered(1))


def _lru_scan(a, u, h0):
    rows, d = a.shape
    n = rows // V7X_SUBLANES
    a3 = a.reshape(n, V7X_SUBLANES, d)
    u3 = u.reshape(n, V7X_SUBLANES, d)
    row = lax.broadcasted_iota(jnp.int32, a3.shape, 1)
    shift = 1
    while shift < V7X_SUBLANES:
        keep = row >= shift
        a_sh = pltpu.roll(a3, shift, axis=1)
        u_sh = pltpu.roll(u3, shift, axis=1)
        u3 = jnp.where(keep, a3 * u_sh + u3, u3)
        a3 = jnp.where(keep, a3 * a_sh, a3)
        shift *= 2
    out = []
    h = h0
    for g in range(n):
        hg = a3[g] * h + u3[g]
        out.append(hg)
        h = hg[V7X_SUBLANES - 1:V7X_SUBLANES, :]
    return jnp.concatenate(out, axis=0), h


def _mixer_kernel(x_ref, mixn_ref, win_ref, convw_ref, convb_ref, wa_ref, ba_ref, wi_ref, bi_ref,
                  lam_ref, lng_ref, lnb_ref, ws_ref, bst_ref, wout_ref, ffn_ref, wr_ref, br_ref,
                  x1_ref, hp_ref, pos_ref, gate_ref, cnt_ref,
                  zprev_ref, hcar_ref, ccar_ref, *, expert_capacity):
    b = pl.program_id(0)
    s = pl.program_id(1)
    rows, d = x_ref.shape

    @pl.when(s == 0)
    def _():
        zprev_ref[...] = jnp.zeros_like(zprev_ref)
        hcar_ref[...] = jnp.zeros_like(hcar_ref)

    @pl.when((b == 0) & (s == 0))
    def _():
        ccar_ref[...] = jnp.zeros_like(ccar_ref)

    x = x_ref[...]
    h = _rmsnorm(x, mixn_ref[...]).astype(BF16)

    def proj(k):
        return _dot(h, win_ref[:, k * d:(k + 1) * d])

    z_lru = proj(0)
    zcat = jnp.concatenate([zprev_ref[...], z_lru], axis=0)
    zprev_ref[...] = z_lru[rows - V7X_SUBLANES:, :]
    xa = convb_ref[...] + convw_ref[CONV_WIDTH - 1:CONV_WIDTH, :] * z_lru
    for j in range(1, CONV_WIDTH):
        shifted = pltpu.roll(zcat, j, axis=0)[V7X_SUBLANES:, :]
        xa = xa + convw_ref[CONV_WIDTH - 1 - j:CONV_WIDTH - j, :] * shifted
    xa_bf = xa.astype(BF16)
    nblk = d // V7X_MXU_DIM

    def blockdiag(w_ref):
        return jnp.concatenate(
            [_dot(xa_bf[:, j * V7X_MXU_DIM:(j + 1) * V7X_MXU_DIM], w_ref[j])
             for j in range(nblk)], axis=1)

    r = _sigmoid(blockdiag(wa_ref) + ba_ref[...])
    gate_i = _sigmoid(blockdiag(wi_ref) + bi_ref[...])
    neg_lam = -lam_ref[...]
    softplus = jnp.maximum(neg_lam, 0.0) + jnp.log1p(jnp.exp(-jnp.abs(neg_lam)))
    log_a = (-LRU_C) * r * softplus
    a = jnp.exp(log_a)
    u = jnp.sqrt(1.0 - a * a) * (gate_i * xa)
    hseq, hlast = _lru_scan(a, u, hcar_ref[...])
    hcar_ref[...] = hlast
    y_a = hseq * _gelu_tanh(proj(1))
    merged = _sigmoid(proj(4)) * y_a

    gv = _gelu_tanh(proj(3))
    mu = jnp.mean(gv, axis=-1, keepdims=True)
    xc = gv - mu
    var = jnp.mean(xc * xc, axis=-1, keepdims=True)
    v_bf = (xc * lax.rsqrt(var + EPS) * lng_ref[...] + lnb_ref[...]).astype(BF16)
    t_idx = lax.broadcasted_iota(jnp.int32, (CHUNK, CHUNK), 0)
    s_idx = lax.broadcasted_iota(jnp.int32, (CHUNK, CHUNK), 1)
    causal = t_idx >= s_idx
    gdim = d // SGU_GROUPS
    sp_cols = []
    for g in range(SGU_GROUPS):
        wsg = jnp.where(causal, ws_ref[g], 0.0).astype(BF16)
        bias = bst_ref[:, g:g + 1]
        sp_rows = []
        for c in range(rows // CHUNK):
            vb = v_bf[c * CHUNK:(c + 1) * CHUNK, g * gdim:(g + 1) * gdim]
            sp_rows.append(_dot(wsg, vb) + bias)
        sp_cols.append(jnp.concatenate(sp_rows, axis=0))
    sp = jnp.concatenate(sp_cols, axis=1)
    y_b = _gelu_tanh(proj(2)) * sp
    merged = merged + _sigmoid(proj(5)) * y_b

    x1 = x + _dot(merged.astype(BF16), wout_ref[...])
    x1_ref[...] = x1

    hn = _rmsnorm(x1, ffn_ref[...])
    half = d // 2
    hp_ref[...] = _pack_bf16_pair(hn[:, :half], hn[:, half:])
    logits = _dot(hn.astype(BF16), wr_ref[...])
    lt = jnp.transpose(logits) + br_ref[...]
    sub = lax.broadcasted_iota(jnp.int32, (V7X_SUBLANES, rows), 0)
    subf = sub.astype(F32)
    big = jnp.float32(1e9)

    lg = jnp.where(sub < N_GROUPS, lt[0:V7X_SUBLANES, :], -jnp.inf)
    g_exp = jnp.exp(lg - jnp.max(lg, axis=0, keepdims=True))
    g_prob = g_exp / jnp.sum(g_exp, axis=0, keepdims=True)
    g_top = jnp.max(g_prob, axis=0, keepdims=True)
    g_idx = jnp.min(jnp.where(g_prob == g_top, subf, big), axis=0, keepdims=True)

    e_sel = jnp.zeros((EXPERTS_PER_GROUP, rows), F32)
    for g in range(N_GROUPS):
        r0 = EXPERT_ROW0 + g * EXPERTS_PER_GROUP
        e_sel = jnp.where(g_idx == g, lt[r0:r0 + EXPERTS_PER_GROUP, :], e_sel)
    e_exp = jnp.exp(e_sel - jnp.max(e_sel, axis=0, keepdims=True))
    e_prob = e_exp / jnp.sum(e_exp, axis=0, keepdims=True)
    p1 = jnp.max(e_prob, axis=0, keepdims=True)
    i1 = jnp.min(jnp.where(e_prob == p1, subf, big), axis=0, keepdims=True)
    rest = jnp.where(subf == i1, -1.0, e_prob)
    p2 = jnp.max(rest, axis=0, keepdims=True)
    i2 = jnp.min(jnp.where(rest == p2, subf, big), axis=0, keepdims=True)
    psum = p1 + p2
    gate1 = g_top * (p1 / psum)
    gate2 = g_top * (p2 / psum)
    gid1 = g_idx * EXPERTS_PER_GROUP + i1
    gid2 = g_idx * EXPERTS_PER_GROUP + i2

    eid = lax.broadcasted_iota(jnp.int32, (N_EXPERTS, rows), 0).astype(F32)
    hit1 = eid == gid1
    hit2 = eid == gid2
    cnt = jnp.where(hit1 | hit2, 1.0, 0.0)
    before = (lax.broadcasted_iota(jnp.int32, (rows, rows), 0)
              < lax.broadcasted_iota(jnp.int32, (rows, rows), 1))
    excl = _dot(cnt.astype(BF16), jnp.where(before, 1.0, 0.0).astype(BF16))
    base = ccar_ref[:, 0:1] + excl
    rank1 = jnp.sum(jnp.where(hit1, base, 0.0), axis=0, keepdims=True)
    rank2 = jnp.sum(jnp.where(hit2, base, 0.0), axis=0, keepdims=True)
    total = ccar_ref[...] + jnp.sum(cnt, axis=1, keepdims=True)
    ccar_ref[...] = total
    cnt_ref[...] = total
    cap = float(expert_capacity)
    zero = jnp.zeros((V7X_SUBLANES - TOP_K, rows), F32)
    pos = jnp.concatenate([gid1 * cap + rank1, gid2 * cap + rank2, zero], axis=0)
    pos_ref[...] = pos.astype(jnp.int32)
    gate_ref[...] = jnp.concatenate([gate1, gate2, zero], axis=0)


def _mixer_call(x, mix_norm, w_in, conv_w, conv_b, wa_blk, ba, wi_blk, bi, lam, ln_g, ln_b, ws, bs_t,
                w_out, ffn_norm, w_router, b_router):
    cfg = _tiles()
    bsz, seq, d = x.shape
    ts = cfg["mixer_rows"]
    ntok = bsz * seq
    nseq = seq // ts
    grid = (bsz, nseq)
    row1 = (1, d)
    in_specs = [
        pl.BlockSpec((None, ts, d), lambda b, s: (b, s, 0)),
        _const_spec(row1),
        _const_spec(w_in.shape),
        _const_spec(conv_w.shape), _const_spec(row1),
        _const_spec(wa_blk.shape), _const_spec(row1),
        _const_spec(wi_blk.shape), _const_spec(row1),
        _const_spec(row1),
        _const_spec(row1), _const_spec(row1),
        _const_spec(ws.shape), _const_spec(bs_t.shape),
        _const_spec(w_out.shape), _const_spec(row1),
        _const_spec(w_router.shape), _const_spec(b_router.shape),
    ]
    out_shape = [
        jax.ShapeDtypeStruct((bsz, seq, d), F32),
        jax.ShapeDtypeStruct((ntok, d // 2), U32),
        jax.ShapeDtypeStruct((V7X_SUBLANES, ntok), jnp.int32),
        jax.ShapeDtypeStruct((V7X_SUBLANES, ntok), F32),
        jax.ShapeDtypeStruct((N_EXPERTS, V7X_LANES), F32),
    ]
    out_specs = [
        pl.BlockSpec((None, ts, d), lambda b, s: (b, s, 0)),
        pl.BlockSpec((ts, d // 2), lambda b, s: (b * nseq + s, 0)),
        pl.BlockSpec((V7X_SUBLANES, ts), lambda b, s: (0, b * nseq + s)),
        pl.BlockSpec((V7X_SUBLANES, ts), lambda b, s: (0, b * nseq + s)),
        pl.BlockSpec((N_EXPERTS, V7X_LANES), lambda b, s: (0, 0)),
    ]
    scratch = [
        pltpu.VMEM((V7X_SUBLANES, d), F32),
        pltpu.VMEM((1, d), F32),
        pltpu.VMEM((N_EXPERTS, V7X_LANES), F32),
    ]
    return pl.pallas_call(
        functools.partial(_mixer_kernel, expert_capacity=ntok),
        grid=grid,
        in_specs=in_specs,
        out_specs=out_specs,
        out_shape=out_shape,
        scratch_shapes=scratch,
        compiler_params=pltpu.CompilerParams(
            dimension_semantics=("arbitrary", "arbitrary"),
            vmem_limit_bytes=cfg["mixer_vmem"]),
        name="mixer",
    )(x, mix_norm, w_in, conv_w, conv_b, wa_blk, ba, wi_blk, bi, lam, ln_g, ln_b, ws, bs_t,
      w_out, ffn_norm, w_router, b_router)


def _expert_kernel(te_ref, blk_ref, nt_ref, hs_ref, w1_ref, w3_ref, w2_ref, ys_ref):
    del te_ref, blk_ref

    @pl.when(pl.program_id(0) < nt_ref[0])
    def _():
        lo, hi = _unpack_bf16_pair(hs_ref[...])
        h = jnp.concatenate([lo, hi], axis=1).astype(BF16)
        a = _dot(h, w1_ref[...].astype(BF16))
        b = _dot(h, w3_ref[...].astype(BF16))
        hid = (a * _sigmoid(a)) * b
        y = _dot(hid.astype(BF16), w2_ref[...].astype(BF16))
        half = y.shape[1] // 2
        ys_ref[...] = _pack_bf16_pair(y[:, :half], y[:, half:])


def _expert_call(tile_expert, tile_block, n_tiles, hs, w1, w3, w2):
    cfg = _tiles()
    tm = cfg["expert_rows"]
    prow, half = hs.shape
    _, d, f = w1.shape
    max_tiles = tile_expert.shape[0]

    def row_map(i, te, blk, nt):
        return (blk[i], 0)

    def w_map(i, te, blk, nt):
        return (te[i], 0, 0)

    grid_spec = pltpu.PrefetchScalarGridSpec(
        num_scalar_prefetch=3,
        grid=(max_tiles,),
        in_specs=[
            pl.BlockSpec((tm, half), row_map),
            pl.BlockSpec((None, d, f), w_map),
            pl.BlockSpec((None, d, f), w_map),
            pl.BlockSpec((None, f, d), w_map),
        ],
        out_specs=pl.BlockSpec((tm, half), row_map),
    )
    return pl.pallas_call(
        _expert_kernel,
        grid_spec=grid_spec,
        out_shape=jax.ShapeDtypeStruct((prow, half), U32),
        compiler_params=pltpu.CompilerParams(
            dimension_semantics=("arbitrary",),
            vmem_limit_bytes=cfg["expert_vmem"]),
        name="experts",
    )(tile_expert, tile_block, n_tiles, hs, w1, w3, w2)


def _sc_mesh():
    return plsc.VectorSubcoreMesh(core_axis_name="c", subcore_axis_name="s",
                                  num_cores=V7X_SC_CORES, num_subcores=V7X_SC_SUBCORES)


def _sc_worker_id():
    return lax.axis_index("s") * V7X_SC_CORES + lax.axis_index("c")


def _dispatch_call(hp, pos_w, out_rows):
    cfg = _tiles()
    ntok, half = hp.shape
    nw, topk, nch, ch = pos_w.shape
    per_w = nch * ch

    def body(hp_hbm, pos_hbm, hs_hbm, idx_v, buf, sem):
        del sem
        wid = _sc_worker_id()
        pltpu.sync_copy(pos_hbm.at[wid], idx_v)

        @pl.loop(0, nch)
        def _(c):
            pltpu.sync_copy(hp_hbm.at[pl.ds(wid * per_w + c * ch, ch)], buf)
            for k in range(topk):
                pltpu.sync_copy(buf, hs_hbm.at[idx_v.at[k, c]])

    assert nw == V7X_SC_CORES * V7X_SC_SUBCORES and nw * per_w == ntok and ch == cfg["sc_rows"]
    return pl.kernel(
        body,
        out_type=jax.ShapeDtypeStruct((out_rows, half), U32),
        mesh=_sc_mesh(),
        scratch_types=[
            pltpu.VMEM((topk, nch, ch), jnp.int32),
            pltpu.VMEM((ch, half), U32),
            pltpu.SemaphoreType.DMA,
        ],
        name="dispatch",
    )(hp, pos_w)


def _combine_call(ys, pos_w):
    cfg = _tiles()
    _, half = ys.shape
    nw, topk, nch, ch = pos_w.shape
    per_w = nch * ch
    ntok = nw * per_w

    def body(ys_hbm, pos_hbm, *rest):
        outs = rest[:topk]
        idx_v, buf, sem = rest[topk:]
        del sem
        wid = _sc_worker_id()
        pltpu.sync_copy(pos_hbm.at[wid], idx_v)

        @pl.loop(0, nch)
        def _(c):
            for k in range(topk):
                pltpu.sync_copy(ys_hbm.at[idx_v.at[k, c]], buf)
                pltpu.sync_copy(buf, outs[k].at[pl.ds(wid * per_w + c * ch, ch)])

    assert nw == V7X_SC_CORES * V7X_SC_SUBCORES and ch == cfg["sc_rows"]
    return pl.kernel(
        body,
        out_type=[jax.ShapeDtypeStruct((ntok, half), U32)] * topk,
        mesh=_sc_mesh(),
        scratch_types=[
            pltpu.VMEM((topk, nch, ch), jnp.int32),
            pltpu.VMEM((ch, half), U32),
            pltpu.SemaphoreType.DMA,
        ],
        name="combine",
    )(ys, pos_w)


def _ple_kernel(x1_ref, yg0_ref, yg1_ref, gate_ref, p_ref, plen_ref, wg_ref, wu_ref, fin_ref, o_ref):
    lo0, hi0 = _unpack_bf16_pair(yg0_ref[...])
    lo1, hi1 = _unpack_bf16_pair(yg1_ref[...])
    g0 = gate_ref[:, 0:1]
    g1 = gate_ref[:, 1:2]
    moe = g0 * jnp.concatenate([lo0, hi0], axis=1) + g1 * jnp.concatenate([lo1, hi1], axis=1)
    x2 = x1_ref[...] + moe
    r = _rmsnorm(x2, plen_ref[...]).astype(BF16)
    gt = _sigmoid(_dot(r, wg_ref[...]))
    up = _dot(p_ref[...].astype(BF16), wu_ref[...])
    x3 = x2 + gt * up
    o_ref[...] = _rmsnorm(x3, fin_ref[...])


def _ple_call(x1, yg0, yg1, gates, p, ple_norm, wg, wu, final_norm):
    cfg = _tiles()
    ntok, d = x1.shape
    tp = cfg["ple_rows"]
    pdim = p.shape[1]
    return pl.pallas_call(
        _ple_kernel,
        grid=(ntok // tp,),
        in_specs=[
            pl.BlockSpec((tp, d), lambda i: (i, 0)),
            pl.BlockSpec((tp, d // 2), lambda i: (i, 0)),
            pl.BlockSpec((tp, d // 2), lambda i: (i, 0)),
            pl.BlockSpec((tp, TOP_K), lambda i: (i, 0)),
            pl.BlockSpec((tp, pdim), lambda i: (i, 0)),
            _const_spec((1, d)),
            _const_spec(wg.shape),
            _const_spec(wu.shape),
            _const_spec((1, d)),
        ],
        out_specs=pl.BlockSpec((tp, d), lambda i: (i, 0)),
        out_shape=jax.ShapeDtypeStruct((ntok, d), F32),
        compiler_params=pltpu.CompilerParams(
            dimension_semantics=("arbitrary",),
            vmem_limit_bytes=cfg["ple_vmem"]),
        name="ple",
    )(x1, yg0, yg1, gates, p, ple_norm, wg, wu, final_norm)


def _blockdiag_pack(w):
    nb, bd, _ = w.shape
    per = V7X_MXU_DIM // bd
    w4 = w.reshape(nb // per, per, bd, bd)
    eye = jnp.eye(per, dtype=w.dtype)
    out = jnp.einsum("jpab,pq->jpaqb", w4, eye)
    return out.reshape(nb // per, V7X_MXU_DIM, V7X_MXU_DIM).astype(BF16)


def _tile_map(counts, tm, cap, max_tiles):
    tiles_e = (counts + tm - 1) // tm
    ends = jnp.cumsum(tiles_e)
    starts = ends - tiles_e
    n_tiles = ends[-1]
    step = jnp.minimum(jnp.arange(max_tiles, dtype=jnp.int32), n_tiles - 1)
    expert = jnp.sum((ends[None, :] <= step[:, None]).astype(jnp.int32), axis=1)
    onehot = expert[:, None] == jnp.arange(N_EXPERTS, dtype=jnp.int32)[None, :]
    start = jnp.sum(jnp.where(onehot, starts[None, :], 0), axis=1)
    block = expert * (cap // tm) + (step - start)
    return expert, block, n_tiles[None]


def kernel(x, p, mix_norm, w_in, conv_w, conv_b, lru_wa, lru_ba, lru_wi, lru_bi, lru_lambda, sgu_ln_g, sgu_ln_b, sgu_ws, sgu_bs, w_out, ffn_norm, router_group_w, router_group_b, router_expert_w, router_expert_b, expert_w1, expert_w3, expert_w2, ple_norm, ple_gate_w, ple_up_w, final_norm):
    cfg = _tiles()
    bsz, seq, d = x.shape
    ntok = bsz * seq
    tm = cfg["expert_rows"]
    depth = w_in.shape[0]
    assert depth == 1, "the ple kernel applies the final norm, so it must be the last layer"
    l = 0
    w_router = jnp.zeros((d, ROUTER_ROWS), F32)
    w_router = w_router.at[:, :N_GROUPS].set(router_group_w[l])
    w_router = w_router.at[:, EXPERT_ROW0:EXPERT_ROW0 + N_EXPERTS].set(router_expert_w[l])
    b_router = jnp.zeros((ROUTER_ROWS, 1), F32)
    b_router = b_router.at[:N_GROUPS, 0].set(router_group_b[l])
    b_router = b_router.at[EXPERT_ROW0:EXPERT_ROW0 + N_EXPERTS, 0].set(router_expert_b[l])
    x1, hp, pos, gate, cnt = _mixer_call(
        x, mix_norm[l][None], w_in[l].astype(BF16), conv_w[l], conv_b[l][None],
        _blockdiag_pack(lru_wa[l]), lru_ba[l][None], _blockdiag_pack(lru_wi[l]), lru_bi[l][None],
        lru_lambda[l][None], sgu_ln_g[l][None], sgu_ln_b[l][None], sgu_ws[l],
        jnp.transpose(sgu_bs[l]), w_out[l].astype(BF16), ffn_norm[l][None],
        w_router.astype(BF16), b_router)

    cap = ntok
    max_tiles = (ntok * TOP_K) // tm + N_EXPERTS
    tile_expert, tile_block, n_tiles = _tile_map(cnt[:, 0].astype(jnp.int32), tm, cap, max_tiles)
    nw = V7X_SC_CORES * V7X_SC_SUBCORES
    ch = cfg["sc_rows"]
    pos_w = jnp.transpose(pos[:TOP_K].reshape(TOP_K, nw, ntok // (nw * ch), ch), (1, 0, 2, 3))

    hs = _dispatch_call(hp, pos_w, N_EXPERTS * cap)
    ys = _expert_call(tile_expert, tile_block, n_tiles, hs, expert_w1[l], expert_w3[l], expert_w2[l])
    yg0, yg1 = _combine_call(ys, pos_w)
    gates = jnp.transpose(gate[:TOP_K])

    out = _ple_call(x1.reshape(ntok, d), yg0, yg1, gates, p[l].reshape(ntok, -1), ple_norm[l][None],
                    ple_gate_w[l].astype(BF16), ple_up_w[l].astype(BF16), final_norm[None])
    return out.reshape(bsz, seq, d)
```

```python
import functools

import jax
import jax.numpy as jnp
from jax import lax
from jax.experimental import pallas as pl
from jax.experimental.pallas import tpu as pltpu
from jax.experimental.pallas import tpu_sc as plsc

F32 = jnp.float32
BF16 = jnp.bfloat16
U32 = jnp.uint32

LRU_BLOCKS = 16
CONV_WIDTH = 4
LRU_C = 8.0
SGU_GROUPS = 8
CHUNK = 128
N_GROUPS = 4
EXPERTS_PER_GROUP = 8
N_EXPERTS = N_GROUPS * EXPERTS_PER_GROUP
TOP_K = 2
EPS = 1e-6

V7X_MXU_DIM = 256
V7X_SUBLANES = 8
V7X_LANES = 128
V7X_VMEM_BYTES = 64 * 1024 * 1024
V7X_SC_CORES = 2
V7X_SC_SUBCORES = 16

ROUTER_ROWS = V7X_LANES
EXPERT_ROW0 = V7X_SUBLANES


def _tiles():
    return dict(
        mixer_rows=256,
        expert_rows=512,
        ple_rows=512,
        sc_rows=64,
        router_rows=1024,
        mixer_vmem=52 * 1024 * 1024,
        expert_vmem=40 * 1024 * 1024,
        ple_vmem=40 * 1024 * 1024,
        router_vmem=32 * 1024 * 1024,
    )


def _dot(a, b):
    return jnp.dot(a, b, preferred_element_type=F32)


def _sigmoid(x):
    return 0.5 * jnp.tanh(0.5 * x) + 0.5


def _rmsnorm(x, g):
    ms = jnp.mean(x * x, axis=-1, keepdims=True)
    return x * lax.rsqrt(ms + EPS) * g


def _pack_bf16_pair(lo, hi):
    lo_b = lax.bitcast_convert_type(lo.astype(BF16).astype(F32), U32)
    hi_b = lax.bitcast_convert_type(hi.astype(BF16).astype(F32), U32)
    return (hi_b & jnp.uint32(0xFFFF0000)) | lax.shift_right_logical(lo_b, jnp.uint32(16))


def _unpack_bf16_pair(w):
    lo = lax.bitcast_convert_type(lax.shift_left(w, jnp.uint32(16)), F32)
    hi = lax.bitcast_convert_type(w & jnp.uint32(0xFFFF0000), F32)
    return lo, hi


def _const_spec(shape):
    zeros = (0,) * len(shape)
    return pl.BlockSpec(shape, lambda *_: zeros, pipeline_mode=pl.Buffered(1))


def _tile_copies(hbm, buf, sem, b, row0, slot, to_hbm):
    group = buf.shape[1]
    copies = []
    for r in range(V7X_SUBLANES):
        hbm_rows = hbm.at[b, pl.ds(row0 + group * r, group), :]
        vmem_rows = buf.at[slot, :, r, :]
        src, dst = (vmem_rows, hbm_rows) if to_hbm else (hbm_rows, vmem_rows)
        copies.append(pltpu.make_async_copy(src, dst, sem.at[slot]))
    return copies


def _lru_scan(a, u, h0):
    group = a.shape[0]
    acc_a = [a[0]]
    acc_u = [u[0]]
    for g in range(1, group):
        acc_a.append(a[g] * acc_a[-1])
        acc_u.append(a[g] * acc_u[-1] + u[g])
    end_a, end_u = acc_a[-1], acc_u[-1]
    sub = lax.broadcasted_iota(jnp.int32, end_a.shape, 0)
    shift = 1
    while shift < V7X_SUBLANES:
        keep = sub >= shift
        a_sh = pltpu.roll(end_a, shift, axis=0)
        u_sh = pltpu.roll(end_u, shift, axis=0)
        end_u = jnp.where(keep, end_a * u_sh + end_u, end_u)
        end_a = jnp.where(keep, end_a * a_sh, end_a)
        shift *= 2
    h_end = end_a * h0 + end_u
    h_in = jnp.where(sub == 0, h0, pltpu.roll(h_end, 1, axis=0))
    out = [acc_a[g] * h_in + acc_u[g] for g in range(group)]
    return jnp.stack(out, axis=0), h_end[V7X_SUBLANES - 1:V7X_SUBLANES, :]


def _mixer_kernel(x_hbm, mixn_ref, win_ref, convw_ref, convb_ref, wa_ref, ba_ref, wi_ref, bi_ref,
                  lam_ref, lng_ref, lnb_ref, wsp_ref, bsp_ref, wout_ref, ffn_ref,
                  x1_hbm, hp_hbm,
                  xbuf, z0_ref, z1_ref, x1buf, hpbuf, xsem, x1sem, hpsem, wsm_ref, ztail_ref, hcar_ref,
                  *, nseq):
    j = pl.program_id(0)
    ntile = pl.num_programs(0) - 1
    _, group, _, d = xbuf.shape
    rows = group * V7X_SUBLANES
    half = d // 2
    ta = jnp.minimum(j, ntile - 1)
    tb = jnp.maximum(j - 1, 0)
    s = lax.rem(tb, nseq)
    slot = lax.rem(tb, 2)

    def fetch(t):
        return _tile_copies(x_hbm, xbuf, xsem, lax.div(t, nseq), lax.rem(t, nseq) * rows,
                            lax.rem(t, 3), to_hbm=False)

    def put(t):
        tb_, ts_, sl = lax.div(t, nseq), lax.rem(t, nseq) * rows, lax.rem(t, 2)
        return (_tile_copies(x1_hbm, x1buf, x1sem, tb_, ts_, sl, to_hbm=True)
                + _tile_copies(hp_hbm, hpbuf, hpsem, tb_, ts_, sl, to_hbm=True))

    @pl.when(j == 0)
    def _():
        for c in fetch(0):
            c.start()
        z1_ref[...] = jnp.zeros_like(z1_ref)
        i_idx = lax.broadcasted_iota(jnp.int32, (rows, rows), 0)
        j_idx = lax.broadcasted_iota(jnp.int32, (rows, rows), 1)
        t_i = group * lax.rem(i_idx, V7X_SUBLANES) + lax.div(i_idx, V7X_SUBLANES)
        t_j = group * lax.rem(j_idx, V7X_SUBLANES) + lax.div(j_idx, V7X_SUBLANES)
        keep = (t_i >= t_j) & (lax.div(t_i, CHUNK) == lax.div(t_j, CHUNK))
        for g in range(SGU_GROUPS):
            wsm_ref[g] = jnp.where(keep, wsp_ref[g], 0.0).astype(BF16)

    @pl.when(j + 1 < ntile)
    def _():
        for c in fetch(j + 1):
            c.start()

    @pl.when(j < ntile)
    def _():
        for c in fetch(j):
            c.wait()

    @pl.when(s == 0)
    def _():
        ztail_ref[...] = jnp.zeros_like(ztail_ref)
        hcar_ref[...] = jnp.zeros_like(hcar_ref)

    def compute(z_w, z_r):
        xa_in = xbuf[lax.rem(ta, 3)].reshape(rows, d)
        h_next = _rmsnorm(xa_in, mixn_ref[...]).astype(BF16)
        pw = d // 2

        def project(k):
            z_w[:, k * pw:(k + 1) * pw] = _dot(h_next, win_ref[:, k * pw:(k + 1) * pw])

        x = xbuf[lax.rem(tb, 3)].reshape(rows, d)

        def sec(k, c0, c1):
            return z_r[:, k * d + c0:k * d + c1]

        def one_plus_tanh_gelu(v):
            c = 0.7978845608028654
            return 1.0 + jnp.tanh(v * (c + (c * 0.044715) * (v * v)))

        cw = 0.5 * convw_ref[...]
        cb_h = 0.5 * convb_ref[...]
        ba_h = 0.5 * ba_ref[...]
        bi_h = 0.5 * bi_ref[...]
        neg_lam = -lam_ref[...]
        softplus = jnp.maximum(neg_lam, 0.0) + jnp.log1p(jnp.exp(-jnp.abs(neg_lam)))
        c_a = (-0.5 * LRU_C) * softplus
        blk = V7X_MXU_DIM
        sub3 = lax.broadcasted_iota(jnp.int32, (CONV_WIDTH - 1, V7X_SUBLANES, blk), 1)
        term_a = []
        for n in range(d // blk):
            project(n)
            c0, c1 = n * blk, (n + 1) * blk
            z3 = sec(0, c0, c1).reshape(group, V7X_SUBLANES, blk)
            tail = z3[group - (CONV_WIDTH - 1):]
            halo = jnp.where(sub3 == 0, pltpu.roll(ztail_ref[:, :, c0:c1], 1, axis=1),
                             pltpu.roll(tail, 1, axis=1))
            ztail_ref[:, :, c0:c1] = tail
            zext = jnp.concatenate([halo, z3], axis=0)
            xa_h = cb_h[:, c0:c1] + cw[CONV_WIDTH - 1:CONV_WIDTH, c0:c1] * z3
            for k in range(1, CONV_WIDTH):
                lo = CONV_WIDTH - 1 - k
                xa_h = xa_h + cw[lo:lo + 1, c0:c1] * zext[lo:lo + group]
            xa2 = xa_h.reshape(rows, blk)
            xa_bf = xa2.astype(BF16)
            th_r = jnp.tanh(_dot(xa_bf, wa_ref[n]) + ba_h[:, c0:c1])
            th_i = jnp.tanh(_dot(xa_bf, wi_ref[n]) + bi_h[:, c0:c1])
            a = jnp.exp(c_a[:, c0:c1] + c_a[:, c0:c1] * th_r)
            u = jnp.sqrt(1.0 - a * a) * ((1.0 + th_i) * xa2)
            hseq, hlast = _lru_scan(a.reshape(group, V7X_SUBLANES, blk),
                                    u.reshape(group, V7X_SUBLANES, blk), hcar_ref[:, c0:c1])
            hcar_ref[:, c0:c1] = hlast
            zg = sec(1, c0, c1)
            term_a.append(((1.0 + jnp.tanh(sec(4, c0, c1))) * one_plus_tanh_gelu(zg))
                          * (zg * hseq.reshape(rows, blk)))

        project(4)
        zv = sec(3, 0, d)
        gv2 = zv * one_plus_tanh_gelu(zv)
        project(5)
        mu = jnp.mean(gv2, axis=-1, keepdims=True)
        xc = gv2 - mu
        var = jnp.mean(xc * xc, axis=-1, keepdims=True)
        v_bf = (xc * lax.rsqrt(var + 4.0 * EPS) * lng_ref[...] + lnb_ref[...]).astype(BF16)
        project(6)
        gdim = d // SGU_GROUPS
        term_b = []
        for g in range(SGU_GROUPS):
            c0, c1 = g * gdim, (g + 1) * gdim
            if g in (1, 3, 5, 6, 7):
                project({1: 7, 3: 8, 5: 9, 6: 10, 7: 11}[g])
            sp = _dot(wsm_ref[g], v_bf[:, c0:c1]) + bsp_ref[:, g:g + 1]
            zu = sec(2, c0, c1)
            term_b.append(((1.0 + jnp.tanh(sec(5, c0, c1))) * one_plus_tanh_gelu(zu)) * (zu * sp))
        merged4 = jnp.concatenate(term_a, axis=1) + jnp.concatenate(term_b, axis=1)

        x1 = x + _dot(merged4.astype(BF16), wout_ref[...])

        hn = _rmsnorm(x1, ffn_ref[...])
        hp = _pack_bf16_pair(hn[:, :half], hn[:, half:])

        @pl.when(j >= 3)
        def _():
            for c in put(tb - 2):
                c.wait()

        x1buf[slot] = x1.reshape(group, V7X_SUBLANES, d)
        hpbuf[slot] = hp.reshape(group, V7X_SUBLANES, half)

        @pl.when(j >= 1)
        def _():
            for c in put(tb):
                c.start()

    @pl.when(lax.rem(j, 2) == 0)
    def _():
        compute(z0_ref, z1_ref)

    @pl.when(lax.rem(j, 2) == 1)
    def _():
        compute(z1_ref, z0_ref)

    @pl.when(j == ntile)
    def _():
        for c in put(tb):
            c.wait()

        @pl.when(ntile >= 2)
        def _():
            for c in put(tb - 1):
                c.wait()


def _mixer_call(x, mix_norm, w_in, conv_w, conv_b, wa_blk, ba, wi_blk, bi, lam, ln_g, ln_b, ws_tile,
                bs_tile, w_out, ffn_norm):
    cfg = _tiles()
    bsz, seq, d = x.shape
    ts = cfg["mixer_rows"]
    group = ts // V7X_SUBLANES
    nseq = seq // ts
    ntile = bsz * nseq
    row1 = (1, d)
    in_specs = [
        pl.BlockSpec(memory_space=pl.ANY),
        _const_spec(row1),
        _const_spec(w_in.shape),
        _const_spec(conv_w.shape), _const_spec(row1),
        _const_spec(wa_blk.shape), _const_spec(row1),
        _const_spec(wi_blk.shape), _const_spec(row1),
        _const_spec(row1),
        _const_spec(row1), _const_spec(row1),
        _const_spec(ws_tile.shape), _const_spec(bs_tile.shape),
        _const_spec(w_out.shape), _const_spec(row1),
    ]
    out_shape = [
        jax.ShapeDtypeStruct((bsz, seq, d), F32),
        jax.ShapeDtypeStruct((bsz, seq, d // 2), U32),
    ]
    out_specs = [
        pl.BlockSpec(memory_space=pl.ANY),
        pl.BlockSpec(memory_space=pl.ANY),
    ]
    scratch = [
        pltpu.VMEM((3, group, V7X_SUBLANES, d), F32),
        pltpu.VMEM((ts, w_in.shape[1]), F32),
        pltpu.VMEM((ts, w_in.shape[1]), F32),
        pltpu.VMEM((2, group, V7X_SUBLANES, d), F32),
        pltpu.VMEM((2, group, V7X_SUBLANES, d // 2), U32),
        pltpu.SemaphoreType.DMA((3,)),
        pltpu.SemaphoreType.DMA((2,)),
        pltpu.SemaphoreType.DMA((2,)),
        pltpu.VMEM(ws_tile.shape, BF16),
        pltpu.VMEM((CONV_WIDTH - 1, V7X_SUBLANES, d), F32),
        pltpu.VMEM((1, d), F32),
    ]
    return pl.pallas_call(
        functools.partial(_mixer_kernel, nseq=nseq),
        grid=(ntile + 1,),
        in_specs=in_specs,
        out_specs=out_specs,
        out_shape=out_shape,
        scratch_shapes=scratch,
        compiler_params=pltpu.CompilerParams(
            dimension_semantics=("arbitrary",),
            vmem_limit_bytes=cfg["mixer_vmem"]),
        name="mixer",
    )(x, mix_norm, w_in, conv_w, conv_b, wa_blk, ba, wi_blk, bi, lam, ln_g, ln_b, ws_tile, bs_tile,
      w_out, ffn_norm)


def _router_kernel(hp_ref, wr_ref, br_ref, pos_ref, gate_ref, cnt_ref, ccar_ref, *, expert_capacity):
    rows = hp_ref.shape[0]

    @pl.when(pl.program_id(0) == 0)
    def _():
        ccar_ref[...] = jnp.zeros_like(ccar_ref)

    lo, hi = _unpack_bf16_pair(hp_ref[...])
    hn = jnp.concatenate([lo, hi], axis=1)
    logits = _dot(hn.astype(BF16), wr_ref[...])
    lt = jnp.transpose(logits) + br_ref[...]
    sub = lax.broadcasted_iota(jnp.int32, (V7X_SUBLANES, rows), 0)
    subf = sub.astype(F32)
    big = jnp.float32(1e9)

    lg = jnp.where(sub < N_GROUPS, lt[0:V7X_SUBLANES, :], -jnp.inf)
    g_exp = jnp.exp(lg - jnp.max(lg, axis=0, keepdims=True))
    g_prob = g_exp / jnp.sum(g_exp, axis=0, keepdims=True)
    g_top = jnp.max(g_prob, axis=0, keepdims=True)
    g_idx = jnp.min(jnp.where(g_prob == g_top, subf, big), axis=0, keepdims=True)

    e_sel = jnp.zeros((EXPERTS_PER_GROUP, rows), F32)
    for g in range(N_GROUPS):
        r0 = EXPERT_ROW0 + g * EXPERTS_PER_GROUP
        e_sel = jnp.where(g_idx == g, lt[r0:r0 + EXPERTS_PER_GROUP, :], e_sel)
    e_exp = jnp.exp(e_sel - jnp.max(e_sel, axis=0, keepdims=True))
    e_prob = e_exp / jnp.sum(e_exp, axis=0, keepdims=True)
    p1 = jnp.max(e_prob, axis=0, keepdims=True)
    i1 = jnp.min(jnp.where(e_prob == p1, subf, big), axis=0, keepdims=True)
    rest = jnp.where(subf == i1, -1.0, e_prob)
    p2 = jnp.max(rest, axis=0, keepdims=True)
    i2 = jnp.min(jnp.where(rest == p2, subf, big), axis=0, keepdims=True)
    psum = p1 + p2
    gate1 = g_top * (p1 / psum)
    gate2 = g_top * (p2 / psum)
    gid1 = g_idx * EXPERTS_PER_GROUP + i1
    gid2 = g_idx * EXPERTS_PER_GROUP + i2

    eid = lax.broadcasted_iota(jnp.int32, (N_EXPERTS, rows), 0).astype(F32)
    hit1 = eid == gid1
    hit2 = eid == gid2
    cnt = jnp.where(hit1 | hit2, 1.0, 0.0)
    sb = V7X_MXU_DIM
    before = (lax.broadcasted_iota(jnp.int32, (sb, sb), 0)
              < lax.broadcasted_iota(jnp.int32, (sb, sb), 1))
    before = jnp.where(before, 1.0, 0.0).astype(BF16)
    running = ccar_ref[:, 0:1]
    base = []
    for q in range(rows // sb):
        part = cnt[:, q * sb:(q + 1) * sb]
        base.append(running + _dot(part.astype(BF16), before))
        running = running + jnp.sum(part, axis=1, keepdims=True)
    base = jnp.concatenate(base, axis=1)
    rank1 = jnp.sum(jnp.where(hit1, base, 0.0), axis=0, keepdims=True)
    rank2 = jnp.sum(jnp.where(hit2, base, 0.0), axis=0, keepdims=True)
    total = jnp.broadcast_to(running, ccar_ref.shape)
    ccar_ref[...] = total
    cnt_ref[...] = total
    cap = float(expert_capacity)
    zero = jnp.zeros((V7X_SUBLANES - TOP_K, rows), F32)
    pos = jnp.concatenate([gid1 * cap + rank1, gid2 * cap + rank2, zero], axis=0)
    pos_ref[...] = pos.astype(jnp.int32)
    gate_ref[...] = jnp.concatenate([gate1, gate2, zero], axis=0)


def _router_call(hp, w_router, b_router):
    cfg = _tiles()
    ntok, half = hp.shape
    tr = cfg["router_rows"]
    return pl.pallas_call(
        functools.partial(_router_kernel, expert_capacity=ntok),
        grid=(ntok // tr,),
        in_specs=[
            pl.BlockSpec((tr, half), lambda i: (i, 0)),
            _const_spec(w_router.shape),
            _const_spec(b_router.shape),
        ],
        out_specs=[
            pl.BlockSpec((V7X_SUBLANES, tr), lambda i: (0, i)),
            pl.BlockSpec((V7X_SUBLANES, tr), lambda i: (0, i)),
            pl.BlockSpec((N_EXPERTS, V7X_LANES), lambda i: (0, 0)),
        ],
        out_shape=[
            jax.ShapeDtypeStruct((V7X_SUBLANES, ntok), jnp.int32),
            jax.ShapeDtypeStruct((V7X_SUBLANES, ntok), F32),
            jax.ShapeDtypeStruct((N_EXPERTS, V7X_LANES), F32),
        ],
        scratch_shapes=[pltpu.VMEM((N_EXPERTS, V7X_LANES), F32)],
        compiler_params=pltpu.CompilerParams(
            dimension_semantics=("arbitrary",),
            vmem_limit_bytes=cfg["router_vmem"]),
        name="router",
    )(hp, w_router, b_router)


def _expert_kernel(te_ref, blk_ref, nt_ref, hs_ref, w1_ref, w3_ref, w2_ref, ys_ref):
    del te_ref, blk_ref

    @pl.when(pl.program_id(0) < nt_ref[0])
    def _():
        lo, hi = _unpack_bf16_pair(hs_ref[...])
        h = jnp.concatenate([lo, hi], axis=1).astype(BF16)
        a = _dot(h, w1_ref[...].astype(BF16))
        b = _dot(h, w3_ref[...].astype(BF16))
        hid = (a * _sigmoid(a)) * b
        y = _dot(hid.astype(BF16), w2_ref[...].astype(BF16))
        half = y.shape[1] // 2
        ys_ref[...] = _pack_bf16_pair(y[:, :half], y[:, half:])


def _expert_call(tile_expert, tile_block, n_tiles, hs, w1, w3, w2):
    cfg = _tiles()
    tm = cfg["expert_rows"]
    prow, half = hs.shape
    _, d, f = w1.shape
    max_tiles = tile_expert.shape[0]

    def row_map(i, te, blk, nt):
        return (blk[i], 0)

    def w_map(i, te, blk, nt):
        return (te[i], 0, 0)

    grid_spec = pltpu.PrefetchScalarGridSpec(
        num_scalar_prefetch=3,
        grid=(max_tiles,),
        in_specs=[
            pl.BlockSpec((tm, half), row_map),
            pl.BlockSpec((None, d, f), w_map),
            pl.BlockSpec((None, d, f), w_map),
            pl.BlockSpec((None, f, d), w_map),
        ],
        out_specs=pl.BlockSpec((tm, half), row_map),
    )
    return pl.pallas_call(
        _expert_kernel,
        grid_spec=grid_spec,
        out_shape=jax.ShapeDtypeStruct((prow, half), U32),
        compiler_params=pltpu.CompilerParams(
            dimension_semantics=("arbitrary",),
            vmem_limit_bytes=cfg["expert_vmem"]),
        name="experts",
    )(tile_expert, tile_block, n_tiles, hs, w1, w3, w2)


def _sc_mesh():
    return plsc.VectorSubcoreMesh(core_axis_name="c", subcore_axis_name="s",
                                  num_cores=V7X_SC_CORES, num_subcores=V7X_SC_SUBCORES)


def _sc_worker_id():
    return lax.axis_index("s") * V7X_SC_CORES + lax.axis_index("c")


def _dispatch_call(hp, pos_w, out_rows):
    cfg = _tiles()
    ntok, half = hp.shape
    nw, topk, nch, ch = pos_w.shape
    per_w = nch * ch

    def body(hp_hbm, pos_hbm, hs_hbm, idx_v, buf, sem):
        del sem
        wid = _sc_worker_id()
        pltpu.sync_copy(pos_hbm.at[wid], idx_v)

        @pl.loop(0, nch)
        def _(c):
            pltpu.sync_copy(hp_hbm.at[pl.ds(wid * per_w + c * ch, ch)], buf)
            for k in range(topk):
                pltpu.sync_copy(buf, hs_hbm.at[idx_v.at[k, c]])

    assert nw == V7X_SC_CORES * V7X_SC_SUBCORES and nw * per_w == ntok and ch == cfg["sc_rows"]
    return pl.kernel(
        body,
        out_type=jax.ShapeDtypeStruct((out_rows, half), U32),
        mesh=_sc_mesh(),
        scratch_types=[
            pltpu.VMEM((topk, nch, ch), jnp.int32),
            pltpu.VMEM((ch, half), U32),
            pltpu.SemaphoreType.DMA,
        ],
        name="dispatch",
    )(hp, pos_w)


def _combine_call(ys, pos_w):
    cfg = _tiles()
    _, half = ys.shape
    nw, topk, nch, ch = pos_w.shape
    per_w = nch * ch
    ntok = nw * per_w

    def body(ys_hbm, pos_hbm, *rest):
        outs = rest[:topk]
        idx_v, buf, sem = rest[topk:]
        del sem
        wid = _sc_worker_id()
        pltpu.sync_copy(pos_hbm.at[wid], idx_v)

        @pl.loop(0, nch)
        def _(c):
            for k in range(topk):
                pltpu.sync_copy(ys_hbm.at[idx_v.at[k, c]], buf)
                pltpu.sync_copy(buf, outs[k].at[pl.ds(wid * per_w + c * ch, ch)])

    assert nw == V7X_SC_CORES * V7X_SC_SUBCORES and ch == cfg["sc_rows"]
    return pl.kernel(
        body,
        out_type=[jax.ShapeDtypeStruct((ntok, half), U32)] * topk,
        mesh=_sc_mesh(),
        scratch_types=[
            pltpu.VMEM((topk, nch, ch), jnp.int32),
            pltpu.VMEM((ch, half), U32),
            pltpu.SemaphoreType.DMA,
        ],
        name="combine",
    )(ys, pos_w)


def _ple_kernel(x1_ref, yg0_ref, yg1_ref, gate_ref, p_ref, plen_ref, wg_ref, wu_ref, fin_ref, o_ref):
    lo0, hi0 = _unpack_bf16_pair(yg0_ref[...])
    lo1, hi1 = _unpack_bf16_pair(yg1_ref[...])
    g0 = gate_ref[:, 0:1]
    g1 = gate_ref[:, 1:2]
    moe = g0 * jnp.concatenate([lo0, hi0], axis=1) + g1 * jnp.concatenate([lo1, hi1], axis=1)
    x2 = x1_ref[...] + moe
    r = _rmsnorm(x2, plen_ref[...]).astype(BF16)
    gt = _sigmoid(_dot(r, wg_ref[...]))
    up = _dot(p_ref[...].astype(BF16), wu_ref[...])
    x3 = x2 + gt * up
    o_ref[...] = _rmsnorm(x3, fin_ref[...])


def _ple_call(x1, yg0, yg1, gates, p, ple_norm, wg, wu, final_norm):
    cfg = _tiles()
    ntok, d = x1.shape
    tp = cfg["ple_rows"]
    pdim = p.shape[1]
    return pl.pallas_call(
        _ple_kernel,
        grid=(ntok // tp,),
        in_specs=[
            pl.BlockSpec((tp, d), lambda i: (i, 0)),
            pl.BlockSpec((tp, d // 2), lambda i: (i, 0)),
            pl.BlockSpec((tp, d // 2), lambda i: (i, 0)),
            pl.BlockSpec((tp, TOP_K), lambda i: (i, 0)),
            pl.BlockSpec((tp, pdim), lambda i: (i, 0)),
            _const_spec((1, d)),
            _const_spec(wg.shape),
            _const_spec(wu.shape),
            _const_spec((1, d)),
        ],
        out_specs=pl.BlockSpec((tp, d), lambda i: (i, 0)),
        out_shape=jax.ShapeDtypeStruct((ntok, d), F32),
        compiler_params=pltpu.CompilerParams(
            dimension_semantics=("arbitrary",),
            vmem_limit_bytes=cfg["ple_vmem"]),
        name="ple",
    )(x1, yg0, yg1, gates, p, ple_norm, wg, wu, final_norm)


def _blockdiag_pack(w):
    nb, bd, _ = w.shape
    per = V7X_MXU_DIM // bd
    w4 = w.reshape(nb // per, per, bd, bd)
    eye = jnp.eye(per, dtype=w.dtype)
    out = jnp.einsum("jpab,pq->jpaqb", w4, eye)
    return out.reshape(nb // per, V7X_MXU_DIM, V7X_MXU_DIM).astype(BF16)


def _tile_map(counts, tm, cap, max_tiles):
    tiles_e = (counts + tm - 1) // tm
    ends = jnp.cumsum(tiles_e)
    starts = ends - tiles_e
    n_tiles = ends[-1]
    step = jnp.minimum(jnp.arange(max_tiles, dtype=jnp.int32), n_tiles - 1)
    expert = jnp.sum((ends[None, :] <= step[:, None]).astype(jnp.int32), axis=1)
    onehot = expert[:, None] == jnp.arange(N_EXPERTS, dtype=jnp.int32)[None, :]
    start = jnp.sum(jnp.where(onehot, starts[None, :], 0), axis=1)
    block = expert * (cap // tm) + (step - start)
    return expert, block, n_tiles[None]


def kernel(x, p, mix_norm, w_in, conv_w, conv_b, lru_wa, lru_ba, lru_wi, lru_bi, lru_lambda, sgu_ln_g, sgu_ln_b, sgu_ws, sgu_bs, w_out, ffn_norm, router_group_w, router_group_b, router_expert_w, router_expert_b, expert_w1, expert_w3, expert_w2, ple_norm, ple_gate_w, ple_up_w, final_norm):
    cfg = _tiles()
    bsz, seq, d = x.shape
    ntok = bsz * seq
    tm = cfg["expert_rows"]
    depth = w_in.shape[0]
    assert depth == 1, "the ple kernel applies the final norm, so it must be the last layer"
    l = 0
    w_router = jnp.zeros((d, ROUTER_ROWS), F32)
    w_router = w_router.at[:, :N_GROUPS].set(router_group_w[l])
    w_router = w_router.at[:, EXPERT_ROW0:EXPERT_ROW0 + N_EXPERTS].set(router_expert_w[l])
    b_router = jnp.zeros((ROUTER_ROWS, 1), F32)
    b_router = b_router.at[:N_GROUPS, 0].set(router_group_b[l])
    b_router = b_router.at[EXPERT_ROW0:EXPERT_ROW0 + N_EXPERTS, 0].set(router_expert_b[l])
    col_scale = jnp.concatenate([jnp.ones((4 * d,), F32), jnp.full((2 * d,), 0.5, F32)])
    ts = cfg["mixer_rows"]
    group = ts // V7X_SUBLANES
    reps = ts // CHUNK
    ws_tile = jnp.tile(sgu_ws[l], (1, reps, reps)).reshape(
        SGU_GROUPS, V7X_SUBLANES, group, V7X_SUBLANES, group)
    ws_tile = jnp.transpose(ws_tile, (0, 2, 1, 4, 3)).reshape(SGU_GROUPS, ts, ts)
    bs_tile = jnp.tile(sgu_bs[l], (1, reps)).reshape(SGU_GROUPS, V7X_SUBLANES, group)
    bs_tile = jnp.transpose(bs_tile, (2, 1, 0)).reshape(ts, SGU_GROUPS)
    x1, hp = _mixer_call(
        x, mix_norm[l][None], (w_in[l] * col_scale[None, :]).astype(BF16), conv_w[l], conv_b[l][None],
        _blockdiag_pack(lru_wa[l]), lru_ba[l][None], _blockdiag_pack(lru_wi[l]), lru_bi[l][None],
        lru_lambda[l][None], sgu_ln_g[l][None], sgu_ln_b[l][None], ws_tile, bs_tile,
        (0.25 * w_out[l]).astype(BF16), ffn_norm[l][None])
    hp = hp.reshape(ntok, d // 2)
    pos, gate, cnt = _router_call(hp, w_router.astype(BF16), b_router)

    cap = ntok
    max_tiles = (ntok * TOP_K) // tm + N_EXPERTS
    tile_expert, tile_block, n_tiles = _tile_map(cnt[:, 0].astype(jnp.int32), tm, cap, max_tiles)
    nw = V7X_SC_CORES * V7X_SC_SUBCORES
    ch = cfg["sc_rows"]
    pos_w = jnp.transpose(pos[:TOP_K].reshape(TOP_K, nw, ntok // (nw * ch), ch), (1, 0, 2, 3))

    hs = _dispatch_call(hp, pos_w, N_EXPERTS * cap)
    ys = _expert_call(tile_expert, tile_block, n_tiles, hs, expert_w1[l], expert_w3[l], expert_w2[l])
    yg0, yg1 = _combine_call(ys, pos_w)
    gates = jnp.transpose(gate[:TOP_K])

    out = _ple_call(x1.reshape(ntok, d), yg0, yg1, gates, p[l].reshape(ntok, -1), ple_norm[l][None],
                    ple_gate_w[l].astype(BF16), ple_up_w[l].astype(BF16), final_norm[None])
    return out.reshape(bsz, seq, d)
```

```python
import functools

import jax
import jax.numpy as jnp
from jax import lax
from jax.experimental import pallas as pl
from jax.experimental.pallas import tpu as pltpu
from jax.experimental.pallas import tpu_sc as plsc

F32 = jnp.float32
BF16 = jnp.bfloat16
U32 = jnp.uint32

LRU_BLOCKS = 16
CONV_WIDTH = 4
LRU_C = 8.0
SGU_GROUPS = 8
CHUNK = 128
N_GROUPS = 4
EXPERTS_PER_GROUP = 8
N_EXPERTS = N_GROUPS * EXPERTS_PER_GROUP
TOP_K = 2
EPS = 1e-6

V7X_MXU_DIM = 256
V7X_SUBLANES = 8
V7X_LANES = 128
V7X_VMEM_BYTES = 64 * 1024 * 1024
V7X_SC_CORES = 2
V7X_SC_SUBCORES = 16

ROUTER_ROWS = V7X_LANES
EXPERT_ROW0 = V7X_SUBLANES


def _tiles():
    return dict(
        mixer_rows=256,
        expert_rows=512,
        ple_rows=512,
        sc_rows=64,
        router_rows=1024,
        mixer_vmem=52 * 1024 * 1024,
        expert_vmem=40 * 1024 * 1024,
        ple_vmem=40 * 1024 * 1024,
        router_vmem=32 * 1024 * 1024,
    )


def _dot(a, b):
    return jnp.dot(a, b, preferred_element_type=F32)


def _sigmoid(x):
    return 0.5 * jnp.tanh(0.5 * x) + 0.5


def _rmsnorm(x, g):
    ms = jnp.mean(x * x, axis=-1, keepdims=True)
    return x * lax.rsqrt(ms + EPS) * g


def _pack_bf16_pair(lo, hi):
    lo_b = lax.bitcast_convert_type(lo.astype(BF16).astype(F32), U32)
    hi_b = lax.bitcast_convert_type(hi.astype(BF16).astype(F32), U32)
    return (hi_b & jnp.uint32(0xFFFF0000)) | lax.shift_right_logical(lo_b, jnp.uint32(16))


def _unpack_bf16_pair(w):
    lo = lax.bitcast_convert_type(lax.shift_left(w, jnp.uint32(16)), F32)
    hi = lax.bitcast_convert_type(w & jnp.uint32(0xFFFF0000), F32)
    return lo, hi


def _const_spec(shape):
    zeros = (0,) * len(shape)
    return pl.BlockSpec(shape, lambda *_: zeros, pipeline_mode=pl.Buffered(1))


def _tile_copies(hbm, buf, sem, b, row0, slot, to_hbm):
    group = buf.shape[1]
    copies = []
    for r in range(V7X_SUBLANES):
        hbm_rows = hbm.at[b, pl.ds(row0 + group * r, group), :]
        vmem_rows = buf.at[slot, :, r, :]
        src, dst = (vmem_rows, hbm_rows) if to_hbm else (hbm_rows, vmem_rows)
        copies.append(pltpu.make_async_copy(src, dst, sem.at[slot]))
    return copies


def _lru_scan(a, u, h0):
    group = a.shape[0]
    acc_a = [a[0]]
    acc_u = [u[0]]
    for g in range(1, group):
        acc_a.append(a[g] * acc_a[-1])
        acc_u.append(a[g] * acc_u[-1] + u[g])
    end_a, end_u = acc_a[-1], acc_u[-1]
    sub = lax.broadcasted_iota(jnp.int32, end_a.shape, 0)
    shift = 1
    while shift < V7X_SUBLANES:
        keep = sub >= shift
        a_sh = pltpu.roll(end_a, shift, axis=0)
        u_sh = pltpu.roll(end_u, shift, axis=0)
        end_u = jnp.where(keep, end_a * u_sh + end_u, end_u)
        end_a = jnp.where(keep, end_a * a_sh, end_a)
        shift *= 2
    h_end = end_a * h0 + end_u
    h_in = jnp.where(sub == 0, h0, pltpu.roll(h_end, 1, axis=0))
    out = [acc_a[g] * h_in + acc_u[g] for g in range(group)]
    return jnp.stack(out, axis=0), h_end[V7X_SUBLANES - 1:V7X_SUBLANES, :]


def _mixer_kernel(x_hbm, mixn_ref, win_ref, convw_ref, convb_ref, wa_ref, ba_ref, wi_ref, bi_ref,
                  lam_ref, lng_ref, lnb_ref, ws_ref, bsp_ref, wout_ref, ffn_ref,
                  x1_hbm, hp_hbm,
                  xbuf, z0_ref, z1_ref, x1buf, hpbuf, xsem, x1sem, hpsem, wsm_ref, ztail_ref, hcar_ref,
                  *, nseq):
    j = pl.program_id(0)
    ntile = pl.num_programs(0) - 1
    _, group, _, d = xbuf.shape
    rows = group * V7X_SUBLANES
    half = d // 2
    ta = jnp.minimum(j, ntile - 1)
    tb = jnp.maximum(j - 1, 0)
    s = lax.rem(tb, nseq)
    slot = lax.rem(tb, 2)

    def fetch(t):
        return _tile_copies(x_hbm, xbuf, xsem, lax.div(t, nseq), lax.rem(t, nseq) * rows,
                            lax.rem(t, 3), to_hbm=False)

    def put(t):
        tb_, ts_, sl = lax.div(t, nseq), lax.rem(t, nseq) * rows, lax.rem(t, 2)
        return (_tile_copies(x1_hbm, x1buf, x1sem, tb_, ts_, sl, to_hbm=True)
                + _tile_copies(hp_hbm, hpbuf, hpsem, tb_, ts_, sl, to_hbm=True))

    @pl.when(j == 0)
    def _():
        for c in fetch(0):
            c.start()
        z1_ref[...] = jnp.zeros_like(z1_ref)
        i_idx = lax.broadcasted_iota(jnp.int32, (rows, rows), 0)
        j_idx = lax.broadcasted_iota(jnp.int32, (rows, rows), 1)
        t_i = group * lax.rem(i_idx, V7X_SUBLANES) + lax.div(i_idx, V7X_SUBLANES)
        t_j = group * lax.rem(j_idx, V7X_SUBLANES) + lax.div(j_idx, V7X_SUBLANES)
        keep = (t_i >= t_j) & (lax.div(t_i, CHUNK) == lax.div(t_j, CHUNK))
        pick_rows = jnp.where(t_i == j_idx, 1.0, 0.0).astype(BF16)
        pick_cols = jnp.where(i_idx == t_j, 1.0, 0.0).astype(BF16)
        reps = rows // CHUNK
        for g in range(SGU_GROUPS):
            w_chunk = ws_ref[g].astype(BF16)
            w_rows = jnp.concatenate([w_chunk] * reps, axis=1)
            w_full = jnp.concatenate([w_rows] * reps, axis=0)
            w_perm = _dot(_dot(pick_rows, w_full).astype(BF16), pick_cols)
            wsm_ref[g] = jnp.where(keep, w_perm, 0.0).astype(BF16)

    @pl.when(j + 1 < ntile)
    def _():
        for c in fetch(j + 1):
            c.start()

    @pl.when(j < ntile)
    def _():
        for c in fetch(j):
            c.wait()

    @pl.when(s == 0)
    def _():
        ztail_ref[...] = jnp.zeros_like(ztail_ref)
        hcar_ref[...] = jnp.zeros_like(hcar_ref)

    def compute(z_w, z_r):
        xa_in = xbuf[lax.rem(ta, 3)].reshape(rows, d)
        h_next = _rmsnorm(xa_in, mixn_ref[...]).astype(BF16)
        pw = d // 2

        def project(k):
            z_w[:, k * pw:(k + 1) * pw] = _dot(h_next, win_ref[:, k * pw:(k + 1) * pw])

        x = xbuf[lax.rem(tb, 3)].reshape(rows, d)

        def sec(k, c0, c1):
            return z_r[:, k * d + c0:k * d + c1]

        def one_plus_tanh_gelu(v):
            c = 0.7978845608028654
            return 1.0 + jnp.tanh(v * (c + (c * 0.044715) * (v * v)))

        cw = 0.5 * convw_ref[...]
        cb_h = 0.5 * convb_ref[...]
        ba_h = 0.5 * ba_ref[...]
        bi_h = 0.5 * bi_ref[...]
        neg_lam = -lam_ref[...]
        softplus = jnp.maximum(neg_lam, 0.0) + jnp.log1p(jnp.exp(-jnp.abs(neg_lam)))
        c_a = (-0.5 * LRU_C) * softplus
        blk = V7X_MXU_DIM
        sub3 = lax.broadcasted_iota(jnp.int32, (CONV_WIDTH - 1, V7X_SUBLANES, blk), 1)
        term_a = []
        for n in range(d // blk):
            project(n)
            c0, c1 = n * blk, (n + 1) * blk
            z3 = sec(0, c0, c1).reshape(group, V7X_SUBLANES, blk)
            tail = z3[group - (CONV_WIDTH - 1):]
            halo = jnp.where(sub3 == 0, pltpu.roll(ztail_ref[:, :, c0:c1], 1, axis=1),
                             pltpu.roll(tail, 1, axis=1))
            ztail_ref[:, :, c0:c1] = tail
            zext = jnp.concatenate([halo, z3], axis=0)
            xa_h = cb_h[:, c0:c1] + cw[CONV_WIDTH - 1:CONV_WIDTH, c0:c1] * z3
            for k in range(1, CONV_WIDTH):
                lo = CONV_WIDTH - 1 - k
                xa_h = xa_h + cw[lo:lo + 1, c0:c1] * zext[lo:lo + group]
            xa2 = xa_h.reshape(rows, blk)
            xa_bf = xa2.astype(BF16)
            th_r = jnp.tanh(_dot(xa_bf, wa_ref[n]) + ba_h[:, c0:c1])
            th_i = jnp.tanh(_dot(xa_bf, wi_ref[n]) + bi_h[:, c0:c1])
            a = jnp.exp(c_a[:, c0:c1] + c_a[:, c0:c1] * th_r)
            u = jnp.sqrt(1.0 - a * a) * ((1.0 + th_i) * xa2)
            hseq, hlast = _lru_scan(a.reshape(group, V7X_SUBLANES, blk),
                                    u.reshape(group, V7X_SUBLANES, blk), hcar_ref[:, c0:c1])
            hcar_ref[:, c0:c1] = hlast
            zg = sec(1, c0, c1)
            term_a.append(((1.0 + jnp.tanh(sec(4, c0, c1))) * one_plus_tanh_gelu(zg))
                          * (zg * hseq.reshape(rows, blk)))

        project(4)
        zv = sec(3, 0, d)
        gv2 = zv * one_plus_tanh_gelu(zv)
        project(5)
        mu = jnp.mean(gv2, axis=-1, keepdims=True)
        xc = gv2 - mu
        var = jnp.mean(xc * xc, axis=-1, keepdims=True)
        v_bf = (xc * lax.rsqrt(var + 4.0 * EPS) * lng_ref[...] + lnb_ref[...]).astype(BF16)
        project(6)
        gdim = d // SGU_GROUPS
        term_b = []
        for g in range(SGU_GROUPS):
            c0, c1 = g * gdim, (g + 1) * gdim
            if g in (1, 3, 5, 6, 7):
                project({1: 7, 3: 8, 5: 9, 6: 10, 7: 11}[g])
            sp = _dot(wsm_ref[g], v_bf[:, c0:c1]) + bsp_ref[:, g:g + 1]
            zu = sec(2, c0, c1)
            term_b.append(((1.0 + jnp.tanh(sec(5, c0, c1))) * one_plus_tanh_gelu(zu)) * (zu * sp))
        merged4 = jnp.concatenate(term_a, axis=1) + jnp.concatenate(term_b, axis=1)

        x1 = x + _dot(merged4.astype(BF16), wout_ref[...])

        hn = _rmsnorm(x1, ffn_ref[...])
        hp = _pack_bf16_pair(hn[:, :half], hn[:, half:])

        @pl.when(j >= 3)
        def _():
            for c in put(tb - 2):
                c.wait()

        x1buf[slot] = x1.reshape(group, V7X_SUBLANES, d)
        hpbuf[slot] = hp.reshape(group, V7X_SUBLANES, half)

        @pl.when(j >= 1)
        def _():
            for c in put(tb):
                c.start()

    @pl.when(lax.rem(j, 2) == 0)
    def _():
        compute(z0_ref, z1_ref)

    @pl.when(lax.rem(j, 2) == 1)
    def _():
        compute(z1_ref, z0_ref)

    @pl.when(j == ntile)
    def _():
        for c in put(tb):
            c.wait()

        @pl.when(ntile >= 2)
        def _():
            for c in put(tb - 1):
                c.wait()


def _mixer_call(x, mix_norm, w_in, conv_w, conv_b, wa_blk, ba, wi_blk, bi, lam, ln_g, ln_b, ws,
                bs_tile, w_out, ffn_norm):
    cfg = _tiles()
    bsz, seq, d = x.shape
    ts = cfg["mixer_rows"]
    group = ts // V7X_SUBLANES
    nseq = seq // ts
    ntile = bsz * nseq
    row1 = (1, d)
    in_specs = [
        pl.BlockSpec(memory_space=pl.ANY),
        _const_spec(row1),
        _const_spec(w_in.shape),
        _const_spec(conv_w.shape), _const_spec(row1),
        _const_spec(wa_blk.shape), _const_spec(row1),
        _const_spec(wi_blk.shape), _const_spec(row1),
        _const_spec(row1),
        _const_spec(row1), _const_spec(row1),
        _const_spec(ws.shape), _const_spec(bs_tile.shape),
        _const_spec(w_out.shape), _const_spec(row1),
    ]
    out_shape = [
        jax.ShapeDtypeStruct((bsz, seq, d), F32),
        jax.ShapeDtypeStruct((bsz, seq, d // 2), U32),
    ]
    out_specs = [
        pl.BlockSpec(memory_space=pl.ANY),
        pl.BlockSpec(memory_space=pl.ANY),
    ]
    scratch = [
        pltpu.VMEM((3, group, V7X_SUBLANES, d), F32),
        pltpu.VMEM((ts, w_in.shape[1]), F32),
        pltpu.VMEM((ts, w_in.shape[1]), F32),
        pltpu.VMEM((2, group, V7X_SUBLANES, d), F32),
        pltpu.VMEM((2, group, V7X_SUBLANES, d // 2), U32),
        pltpu.SemaphoreType.DMA((3,)),
        pltpu.SemaphoreType.DMA((2,)),
        pltpu.SemaphoreType.DMA((2,)),
        pltpu.VMEM((SGU_GROUPS, ts, ts), BF16),
        pltpu.VMEM((CONV_WIDTH - 1, V7X_SUBLANES, d), F32),
        pltpu.VMEM((1, d), F32),
    ]
    return pl.pallas_call(
        functools.partial(_mixer_kernel, nseq=nseq),
        grid=(ntile + 1,),
        in_specs=in_specs,
        out_specs=out_specs,
        out_shape=out_shape,
        scratch_shapes=scratch,
        compiler_params=pltpu.CompilerParams(
            dimension_semantics=("arbitrary",),
            vmem_limit_bytes=cfg["mixer_vmem"]),
        name="mixer",
    )(x, mix_norm, w_in, conv_w, conv_b, wa_blk, ba, wi_blk, bi, lam, ln_g, ln_b, ws, bs_tile,
      w_out, ffn_norm)


def _router_kernel(hp_ref, wr_ref, br_ref, pos_ref, gate_ref, cnt_ref, ccar_ref, *, expert_capacity):
    rows = hp_ref.shape[0]

    @pl.when(pl.program_id(0) == 0)
    def _():
        ccar_ref[...] = jnp.zeros_like(ccar_ref)

    lo, hi = _unpack_bf16_pair(hp_ref[...])
    hn = jnp.concatenate([lo, hi], axis=1)
    logits = _dot(hn.astype(BF16), wr_ref[...])
    lt = jnp.transpose(logits) + br_ref[...]
    sub = lax.broadcasted_iota(jnp.int32, (V7X_SUBLANES, rows), 0)
    subf = sub.astype(F32)
    big = jnp.float32(1e9)

    lg = jnp.where(sub < N_GROUPS, lt[0:V7X_SUBLANES, :], -jnp.inf)
    g_exp = jnp.exp(lg - jnp.max(lg, axis=0, keepdims=True))
    g_prob = g_exp / jnp.sum(g_exp, axis=0, keepdims=True)
    g_top = jnp.max(g_prob, axis=0, keepdims=True)
    g_idx = jnp.min(jnp.where(g_prob == g_top, subf, big), axis=0, keepdims=True)

    e_sel = jnp.zeros((EXPERTS_PER_GROUP, rows), F32)
    for g in range(N_GROUPS):
        r0 = EXPERT_ROW0 + g * EXPERTS_PER_GROUP
        e_sel = jnp.where(g_idx == g, lt[r0:r0 + EXPERTS_PER_GROUP, :], e_sel)
    e_exp = jnp.exp(e_sel - jnp.max(e_sel, axis=0, keepdims=True))
    e_prob = e_exp / jnp.sum(e_exp, axis=0, keepdims=True)
    p1 = jnp.max(e_prob, axis=0, keepdims=True)
    i1 = jnp.min(jnp.where(e_prob == p1, subf, big), axis=0, keepdims=True)
    rest = jnp.where(subf == i1, -1.0, e_prob)
    p2 = jnp.max(rest, axis=0, keepdims=True)
    i2 = jnp.min(jnp.where(rest == p2, subf, big), axis=0, keepdims=True)
    psum = p1 + p2
    gate1 = g_top * (p1 / psum)
    gate2 = g_top * (p2 / psum)
    gid1 = g_idx * EXPERTS_PER_GROUP + i1
    gid2 = g_idx * EXPERTS_PER_GROUP + i2

    eid = lax.broadcasted_iota(jnp.int32, (N_EXPERTS, rows), 0).astype(F32)
    hit1 = eid == gid1
    hit2 = eid == gid2
    cnt = jnp.where(hit1 | hit2, 1.0, 0.0)
    sb = V7X_MXU_DIM
    before = (lax.broadcasted_iota(jnp.int32, (sb, sb), 0)
              < lax.broadcasted_iota(jnp.int32, (sb, sb), 1))
    before = jnp.where(before, 1.0, 0.0).astype(BF16)
    running = ccar_ref[:, 0:1]
    base = []
    for q in range(rows // sb):
        part = cnt[:, q * sb:(q + 1) * sb]
        base.append(running + _dot(part.astype(BF16), before))
        running = running + jnp.sum(part, axis=1, keepdims=True)
    base = jnp.concatenate(base, axis=1)
    rank1 = jnp.sum(jnp.where(hit1, base, 0.0), axis=0, keepdims=True)
    rank2 = jnp.sum(jnp.where(hit2, base, 0.0), axis=0, keepdims=True)
    total = jnp.broadcast_to(running, ccar_ref.shape)
    ccar_ref[...] = total
    cnt_ref[...] = total
    cap = float(expert_capacity)
    zero = jnp.zeros((V7X_SUBLANES - TOP_K, rows), F32)
    pos = jnp.concatenate([gid1 * cap + rank1, gid2 * cap + rank2, zero], axis=0)
    pos_ref[...] = pos.astype(jnp.int32)
    gate_ref[...] = jnp.concatenate([gate1, gate2, zero], axis=0)


def _router_call(hp, w_router, b_router):
    cfg = _tiles()
    ntok, half = hp.shape
    tr = cfg["router_rows"]
    return pl.pallas_call(
        functools.partial(_router_kernel, expert_capacity=ntok),
        grid=(ntok // tr,),
        in_specs=[
            pl.BlockSpec((tr, half), lambda i: (i, 0)),
            _const_spec(w_router.shape),
            _const_spec(b_router.shape),
        ],
        out_specs=[
            pl.BlockSpec((V7X_SUBLANES, tr), lambda i: (0, i)),
            pl.BlockSpec((V7X_SUBLANES, tr), lambda i: (0, i)),
            pl.BlockSpec((N_EXPERTS, V7X_LANES), lambda i: (0, 0)),
        ],
        out_shape=[
            jax.ShapeDtypeStruct((V7X_SUBLANES, ntok), jnp.int32),
            jax.ShapeDtypeStruct((V7X_SUBLANES, ntok), F32),
            jax.ShapeDtypeStruct((N_EXPERTS, V7X_LANES), F32),
        ],
        scratch_shapes=[pltpu.VMEM((N_EXPERTS, V7X_LANES), F32)],
        compiler_params=pltpu.CompilerParams(
            dimension_semantics=("arbitrary",),
            vmem_limit_bytes=cfg["router_vmem"]),
        name="router",
    )(hp, w_router, b_router)


def _expert_kernel(nt_ref, hs_hbm, w1_ref, w3_ref, w2_ref, ys_hbm,
                   hbuf, ybuf, hsem, ysem, w1b_ref, w3b_ref, w2b_ref, *, capacity):
    e = pl.program_id(0)
    nt = nt_ref[e]
    tm = hbuf.shape[1]

    def rows_of(expert, t):
        return pl.ds(expert * capacity + t * tm, tm)

    def load(expert, t, slot):
        return pltpu.make_async_copy(hs_hbm.at[rows_of(expert, t)], hbuf.at[slot], hsem.at[slot])

    def store(t, slot):
        return pltpu.make_async_copy(ybuf.at[slot], ys_hbm.at[rows_of(e, t)], ysem.at[slot])

    @pl.when((e == 0) & (nt > 0))
    def _():
        load(e, 0, 0).start()

    w1b_ref[...] = w1_ref[...].astype(BF16)
    w3b_ref[...] = w3_ref[...].astype(BF16)
    w2b_ref[...] = w2_ref[...].astype(BF16)

    @pl.loop(0, nt)
    def _(t):
        slot = lax.rem(t, 2)

        @pl.when(t + 1 < nt)
        def _():
            load(e, t + 1, 1 - slot).start()

        load(e, t, slot).wait()
        lo, hi = _unpack_bf16_pair(hbuf[slot])
        h = jnp.concatenate([lo, hi], axis=1).astype(BF16)
        a = _dot(h, w1b_ref[...])
        b = _dot(h, w3b_ref[...])
        hid = (a * _sigmoid(a)) * b
        y = _dot(hid.astype(BF16), w2b_ref[...])
        half = y.shape[1] // 2

        @pl.when(t >= 2)
        def _():
            store(t - 2, slot).wait()

        ybuf[slot] = _pack_bf16_pair(y[:, :half], y[:, half:])
        store(t, slot).start()

    nxt = jnp.minimum(e + 1, pl.num_programs(0) - 1)

    @pl.when((e + 1 < pl.num_programs(0)) & (nt_ref[nxt] > 0))
    def _():
        load(nxt, 0, 0).start()

    @pl.when(nt >= 1)
    def _():
        store(nt - 1, lax.rem(nt - 1, 2)).wait()

    @pl.when(nt >= 2)
    def _():
        store(nt - 2, lax.rem(nt, 2)).wait()


def _expert_call(tiles_per_expert, hs, w1, w3, w2, capacity):
    cfg = _tiles()
    tm = cfg["expert_rows"]
    prow, half = hs.shape
    n_exp, d, f = w1.shape

    def w_map(e, nt):
        return (e, 0, 0)

    grid_spec = pltpu.PrefetchScalarGridSpec(
        num_scalar_prefetch=1,
        grid=(n_exp,),
        in_specs=[
            pl.BlockSpec(memory_space=pl.ANY),
            pl.BlockSpec((None, d, f), w_map),
            pl.BlockSpec((None, d, f), w_map),
            pl.BlockSpec((None, f, d), w_map),
        ],
        out_specs=pl.BlockSpec(memory_space=pl.ANY),
        scratch_shapes=[
            pltpu.VMEM((2, tm, half), U32),
            pltpu.VMEM((2, tm, half), U32),
            pltpu.SemaphoreType.DMA((2,)),
            pltpu.SemaphoreType.DMA((2,)),
            pltpu.VMEM((d, f), BF16),
            pltpu.VMEM((d, f), BF16),
            pltpu.VMEM((f, d), BF16),
        ],
    )
    return pl.pallas_call(
        functools.partial(_expert_kernel, capacity=capacity),
        grid_spec=grid_spec,
        out_shape=jax.ShapeDtypeStruct((prow, half), U32),
        compiler_params=pltpu.CompilerParams(
            dimension_semantics=("arbitrary",),
            vmem_limit_bytes=cfg["expert_vmem"]),
        name="experts",
    )(tiles_per_expert, hs, w1, w3, w2)


def _sc_mesh():
    return plsc.VectorSubcoreMesh(core_axis_name="c", subcore_axis_name="s",
                                  num_cores=V7X_SC_CORES, num_subcores=V7X_SC_SUBCORES)


def _sc_worker_id():
    return lax.axis_index("s") * V7X_SC_CORES + lax.axis_index("c")


def _dispatch_call(hp, pos_w, out_rows):
    cfg = _tiles()
    ntok, half = hp.shape
    nw, topk, nch, ch = pos_w.shape
    per_w = nch * ch

    def body(hp_hbm, pos_hbm, hs_hbm, idx_v, buf, sem):
        del sem
        wid = _sc_worker_id()
        pltpu.sync_copy(pos_hbm.at[wid], idx_v)

        @pl.loop(0, nch)
        def _(c):
            pltpu.sync_copy(hp_hbm.at[pl.ds(wid * per_w + c * ch, ch)], buf)
            for k in range(topk):
                pltpu.sync_copy(buf, hs_hbm.at[idx_v.at[k, c]])

    assert nw == V7X_SC_CORES * V7X_SC_SUBCORES and nw * per_w == ntok and ch == cfg["sc_rows"]
    return pl.kernel(
        body,
        out_type=jax.ShapeDtypeStruct((out_rows, half), U32),
        mesh=_sc_mesh(),
        scratch_types=[
            pltpu.VMEM((topk, nch, ch), jnp.int32),
            pltpu.VMEM((ch, half), U32),
            pltpu.SemaphoreType.DMA,
        ],
        name="dispatch",
    )(hp, pos_w)


def _combine_call(ys, pos_w):
    cfg = _tiles()
    _, half = ys.shape
    nw, topk, nch, ch = pos_w.shape
    per_w = nch * ch
    ntok = nw * per_w

    def body(ys_hbm, pos_hbm, *rest):
        outs = rest[:topk]
        idx_v, buf, sem = rest[topk:]
        del sem
        wid = _sc_worker_id()
        pltpu.sync_copy(pos_hbm.at[wid], idx_v)

        @pl.loop(0, nch)
        def _(c):
            for k in range(topk):
                pltpu.sync_copy(ys_hbm.at[idx_v.at[k, c]], buf)
                pltpu.sync_copy(buf, outs[k].at[pl.ds(wid * per_w + c * ch, ch)])

    assert nw == V7X_SC_CORES * V7X_SC_SUBCORES and ch == cfg["sc_rows"]
    return pl.kernel(
        body,
        out_type=[jax.ShapeDtypeStruct((ntok, half), U32)] * topk,
        mesh=_sc_mesh(),
        scratch_types=[
            pltpu.VMEM((topk, nch, ch), jnp.int32),
            pltpu.VMEM((ch, half), U32),
            pltpu.SemaphoreType.DMA,
        ],
        name="combine",
    )(ys, pos_w)


def _ple_kernel(x1_ref, yg0_ref, yg1_ref, gate_ref, p_ref, plen_ref, wg_ref, wu_ref, fin_ref, o_ref):
    lo0, hi0 = _unpack_bf16_pair(yg0_ref[...])
    lo1, hi1 = _unpack_bf16_pair(yg1_ref[...])
    g0 = gate_ref[:, 0:1]
    g1 = gate_ref[:, 1:2]
    moe = g0 * jnp.concatenate([lo0, hi0], axis=1) + g1 * jnp.concatenate([lo1, hi1], axis=1)
    x2 = x1_ref[...] + moe
    r = _rmsnorm(x2, plen_ref[...]).astype(BF16)
    gt = _sigmoid(_dot(r, wg_ref[...]))
    up = _dot(p_ref[...].astype(BF16), wu_ref[...])
    x3 = x2 + gt * up
    o_ref[...] = _rmsnorm(x3, fin_ref[...])


def _ple_call(x1, yg0, yg1, gates, p, ple_norm, wg, wu, final_norm):
    cfg = _tiles()
    ntok, d = x1.shape
    tp = cfg["ple_rows"]
    pdim = p.shape[1]
    return pl.pallas_call(
        _ple_kernel,
        grid=(ntok // tp,),
        in_specs=[
            pl.BlockSpec((tp, d), lambda i: (i, 0)),
            pl.BlockSpec((tp, d // 2), lambda i: (i, 0)),
            pl.BlockSpec((tp, d // 2), lambda i: (i, 0)),
            pl.BlockSpec((tp, TOP_K), lambda i: (i, 0)),
            pl.BlockSpec((tp, pdim), lambda i: (i, 0)),
            _const_spec((1, d)),
            _const_spec(wg.shape),
            _const_spec(wu.shape),
            _const_spec((1, d)),
        ],
        out_specs=pl.BlockSpec((tp, d), lambda i: (i, 0)),
        out_shape=jax.ShapeDtypeStruct((ntok, d), F32),
        compiler_params=pltpu.CompilerParams(
            dimension_semantics=("arbitrary",),
            vmem_limit_bytes=cfg["ple_vmem"]),
        name="ple",
    )(x1, yg0, yg1, gates, p, ple_norm, wg, wu, final_norm)


def _blockdiag_pack(w):
    nb, bd, _ = w.shape
    per = V7X_MXU_DIM // bd
    w4 = w.reshape(nb // per, per, bd, bd)
    eye = jnp.eye(per, dtype=w.dtype)
    out = jnp.einsum("jpab,pq->jpaqb", w4, eye)
    return out.reshape(nb // per, V7X_MXU_DIM, V7X_MXU_DIM).astype(BF16)


def kernel(x, p, mix_norm, w_in, conv_w, conv_b, lru_wa, lru_ba, lru_wi, lru_bi, lru_lambda, sgu_ln_g, sgu_ln_b, sgu_ws, sgu_bs, w_out, ffn_norm, router_group_w, router_group_b, router_expert_w, router_expert_b, expert_w1, expert_w3, expert_w2, ple_norm, ple_gate_w, ple_up_w, final_norm):
    cfg = _tiles()
    bsz, seq, d = x.shape
    ntok = bsz * seq
    tm = cfg["expert_rows"]
    depth = w_in.shape[0]
    assert depth == 1, "the ple kernel applies the final norm, so it must be the last layer"
    l = 0
    w_router = jnp.zeros((d, ROUTER_ROWS), F32)
    w_router = w_router.at[:, :N_GROUPS].set(router_group_w[l])
    w_router = w_router.at[:, EXPERT_ROW0:EXPERT_ROW0 + N_EXPERTS].set(router_expert_w[l])
    b_router = jnp.zeros((ROUTER_ROWS, 1), F32)
    b_router = b_router.at[:N_GROUPS, 0].set(router_group_b[l])
    b_router = b_router.at[EXPERT_ROW0:EXPERT_ROW0 + N_EXPERTS, 0].set(router_expert_b[l])
    col_scale = jnp.concatenate([jnp.ones((4 * d,), F32), jnp.full((2 * d,), 0.5, F32)])
    ts = cfg["mixer_rows"]
    group = ts // V7X_SUBLANES
    bs_tile = jnp.tile(sgu_bs[l], (1, ts // CHUNK)).reshape(SGU_GROUPS, V7X_SUBLANES, group)
    bs_tile = jnp.transpose(bs_tile, (2, 1, 0)).reshape(ts, SGU_GROUPS)
    x1, hp = _mixer_call(
        x, mix_norm[l][None], (w_in[l] * col_scale[None, :]).astype(BF16), conv_w[l], conv_b[l][None],
        _blockdiag_pack(lru_wa[l]), lru_ba[l][None], _blockdiag_pack(lru_wi[l]), lru_bi[l][None],
        lru_lambda[l][None], sgu_ln_g[l][None], sgu_ln_b[l][None], sgu_ws[l], bs_tile,
        (0.25 * w_out[l]).astype(BF16), ffn_norm[l][None])
    hp = hp.reshape(ntok, d // 2)
    pos, gate, cnt = _router_call(hp, w_router.astype(BF16), b_router)

    cap = ntok
    tiles_per_expert = (cnt[:, 0].astype(jnp.int32) + tm - 1) // tm
    nw = V7X_SC_CORES * V7X_SC_SUBCORES
    ch = cfg["sc_rows"]
    pos_w = jnp.transpose(pos[:TOP_K].reshape(TOP_K, nw, ntok // (nw * ch), ch), (1, 0, 2, 3))

    hs = _dispatch_call(hp, pos_w, N_EXPERTS * cap)
    ys = _expert_call(tiles_per_expert, hs, expert_w1[l], expert_w3[l], expert_w2[l], cap)
    yg0, yg1 = _combine_call(ys, pos_w)
    gates = jnp.transpose(gate[:TOP_K])

    out = _ple_call(x1.reshape(ntok, d), yg0, yg1, gates, p[l].reshape(ntok, -1), ple_norm[l][None],
                    ple_gate_w[l].astype(BF16), ple_up_w[l].astype(BF16), final_norm[None])
    return out.reshape(bsz, seq, d)
```

```python
import functools

import jax
import jax.numpy as jnp
from jax import lax
from jax.experimental import pallas as pl
from jax.experimental.pallas import tpu as pltpu
from jax.experimental.pallas import tpu_sc as plsc

F32 = jnp.float32
BF16 = jnp.bfloat16
U32 = jnp.uint32

LRU_BLOCKS = 16
CONV_WIDTH = 4
LRU_C = 8.0
SGU_GROUPS = 8
CHUNK = 128
N_GROUPS = 4
EXPERTS_PER_GROUP = 8
N_EXPERTS = N_GROUPS * EXPERTS_PER_GROUP
TOP_K = 2
EPS = 1e-6

V7X_MXU_DIM = 256
V7X_SUBLANES = 8
V7X_LANES = 128
V7X_VMEM_BYTES = 64 * 1024 * 1024
V7X_SC_CORES = 2
V7X_SC_SUBCORES = 16

EXPERT_LOOKAHEAD = 2
ROUTER_ROWS = V7X_LANES
EXPERT_ROW0 = V7X_SUBLANES


def _tiles():
    return dict(
        mixer_rows=256,
        expert_rows=512,
        ple_rows=512,
        sc_rows=64,
        router_rows=1024,
        mixer_vmem=52 * 1024 * 1024,
        expert_vmem=40 * 1024 * 1024,
        ple_vmem=40 * 1024 * 1024,
        router_vmem=32 * 1024 * 1024,
    )


def _dot(a, b):
    return jnp.dot(a, b, preferred_element_type=F32)


def _sigmoid(x):
    return 0.5 * jnp.tanh(0.5 * x) + 0.5


def _rmsnorm(x, g):
    ms = jnp.mean(x * x, axis=-1, keepdims=True)
    return x * lax.rsqrt(ms + EPS) * g


def _pack_bf16_pair(lo, hi):
    lo_b = lax.bitcast_convert_type(lo.astype(BF16).astype(F32), U32)
    hi_b = lax.bitcast_convert_type(hi.astype(BF16).astype(F32), U32)
    return (hi_b & jnp.uint32(0xFFFF0000)) | lax.shift_right_logical(lo_b, jnp.uint32(16))


def _unpack_bf16_pair(w):
    lo = lax.bitcast_convert_type(lax.shift_left(w, jnp.uint32(16)), F32)
    hi = lax.bitcast_convert_type(w & jnp.uint32(0xFFFF0000), F32)
    return lo, hi


def _const_spec(shape):
    zeros = (0,) * len(shape)
    return pl.BlockSpec(shape, lambda *_: zeros, pipeline_mode=pl.Buffered(1))


def _tile_copies(hbm, buf, sem, b, row0, slot, to_hbm):
    group = buf.shape[1]
    copies = []
    for r in range(V7X_SUBLANES):
        hbm_rows = hbm.at[b, pl.ds(row0 + group * r, group), :]
        vmem_rows = buf.at[slot, :, r, :]
        src, dst = (vmem_rows, hbm_rows) if to_hbm else (hbm_rows, vmem_rows)
        copies.append(pltpu.make_async_copy(src, dst, sem.at[slot]))
    return copies


def _lru_scan(a, u, h0):
    group = a.shape[0]
    acc_a = [a[0]]
    acc_u = [u[0]]
    for g in range(1, group):
        acc_a.append(a[g] * acc_a[-1])
        acc_u.append(a[g] * acc_u[-1] + u[g])
    end_a, end_u = acc_a[-1], acc_u[-1]
    sub = lax.broadcasted_iota(jnp.int32, end_a.shape, 0)
    shift = 1
    while shift < V7X_SUBLANES:
        keep = sub >= shift
        a_sh = pltpu.roll(end_a, shift, axis=0)
        u_sh = pltpu.roll(end_u, shift, axis=0)
        end_u = jnp.where(keep, end_a * u_sh + end_u, end_u)
        end_a = jnp.where(keep, end_a * a_sh, end_a)
        shift *= 2
    h_end = end_a * h0 + end_u
    h_in = jnp.where(sub == 0, h0, pltpu.roll(h_end, 1, axis=0))
    out = [acc_a[g] * h_in + acc_u[g] for g in range(group)]
    return jnp.stack(out, axis=0), h_end[V7X_SUBLANES - 1:V7X_SUBLANES, :]


def _mixer_kernel(x_hbm, mixn_ref, win_ref, convw_ref, convb_ref, wa_ref, ba_ref, wi_ref, bi_ref,
                  lam_ref, lng_ref, lnb_ref, ws_ref, bsp_ref, wout_ref, ffn_ref,
                  x1_hbm, hp_hbm,
                  xbuf, z0_ref, z1_ref, x1buf, hpbuf, xsem, x1sem, hpsem, wsm_ref, ztail_ref, hcar_ref,
                  *, nseq):
    j = pl.program_id(0)
    ntile = pl.num_programs(0) - 1
    _, group, _, d = xbuf.shape
    rows = group * V7X_SUBLANES
    half = d // 2
    ta = jnp.minimum(j, ntile - 1)
    tb = jnp.maximum(j - 1, 0)
    s = lax.rem(tb, nseq)
    slot = lax.rem(tb, 2)

    def fetch(t):
        return _tile_copies(x_hbm, xbuf, xsem, lax.div(t, nseq), lax.rem(t, nseq) * rows,
                            lax.rem(t, 3), to_hbm=False)

    def put(t):
        tb_, ts_, sl = lax.div(t, nseq), lax.rem(t, nseq) * rows, lax.rem(t, 2)
        return (_tile_copies(x1_hbm, x1buf, x1sem, tb_, ts_, sl, to_hbm=True)
                + _tile_copies(hp_hbm, hpbuf, hpsem, tb_, ts_, sl, to_hbm=True))

    @pl.when(j == 0)
    def _():
        for c in fetch(0):
            c.start()
        z1_ref[...] = jnp.zeros_like(z1_ref)
        i_idx = lax.broadcasted_iota(jnp.int32, (rows, rows), 0)
        j_idx = lax.broadcasted_iota(jnp.int32, (rows, rows), 1)
        t_i = group * lax.rem(i_idx, V7X_SUBLANES) + lax.div(i_idx, V7X_SUBLANES)
        t_j = group * lax.rem(j_idx, V7X_SUBLANES) + lax.div(j_idx, V7X_SUBLANES)
        keep = (t_i >= t_j) & (lax.div(t_i, CHUNK) == lax.div(t_j, CHUNK))
        pick_rows = jnp.where(t_i == j_idx, 1.0, 0.0).astype(BF16)
        pick_cols = jnp.where(i_idx == t_j, 1.0, 0.0).astype(BF16)
        reps = rows // CHUNK
        for g in range(SGU_GROUPS):
            w_chunk = ws_ref[g].astype(BF16)
            w_rows = jnp.concatenate([w_chunk] * reps, axis=1)
            w_full = jnp.concatenate([w_rows] * reps, axis=0)
            w_perm = _dot(_dot(pick_rows, w_full).astype(BF16), pick_cols)
            wsm_ref[g] = jnp.where(keep, w_perm, 0.0).astype(BF16)

    @pl.when(j + 1 < ntile)
    def _():
        for c in fetch(j + 1):
            c.start()

    @pl.when(j < ntile)
    def _():
        for c in fetch(j):
            c.wait()

    @pl.when(s == 0)
    def _():
        ztail_ref[...] = jnp.zeros_like(ztail_ref)
        hcar_ref[...] = jnp.zeros_like(hcar_ref)

    def compute(z_w, z_r):
        xa_in = xbuf[lax.rem(ta, 3)].reshape(rows, d)
        h_next = _rmsnorm(xa_in, mixn_ref[...]).astype(BF16)
        pw = d // 2

        def project(k):
            z_w[:, k * pw:(k + 1) * pw] = _dot(h_next, win_ref[:, k * pw:(k + 1) * pw])

        x = xbuf[lax.rem(tb, 3)].reshape(rows, d)

        def sec(k, c0, c1):
            return z_r[:, k * d + c0:k * d + c1]

        def one_plus_tanh_gelu(v):
            c = 0.7978845608028654
            return 1.0 + jnp.tanh(v * (c + (c * 0.044715) * (v * v)))

        cw = 0.5 * convw_ref[...]
        cb_h = 0.5 * convb_ref[...]
        ba_h = 0.5 * ba_ref[...]
        bi_h = 0.5 * bi_ref[...]
        neg_lam = -lam_ref[...]
        softplus = jnp.maximum(neg_lam, 0.0) + jnp.log1p(jnp.exp(-jnp.abs(neg_lam)))
        c_a = (-0.5 * LRU_C) * softplus
        blk = V7X_MXU_DIM
        sub3 = lax.broadcasted_iota(jnp.int32, (CONV_WIDTH - 1, V7X_SUBLANES, blk), 1)
        term_a = []
        for n in range(d // blk):
            project(n)
            c0, c1 = n * blk, (n + 1) * blk
            z3 = sec(0, c0, c1).reshape(group, V7X_SUBLANES, blk)
            tail = z3[group - (CONV_WIDTH - 1):]
            halo = jnp.where(sub3 == 0, pltpu.roll(ztail_ref[:, :, c0:c1], 1, axis=1),
                             pltpu.roll(tail, 1, axis=1))
            ztail_ref[:, :, c0:c1] = tail
            zext = jnp.concatenate([halo, z3], axis=0)
            xa_h = cb_h[:, c0:c1] + cw[CONV_WIDTH - 1:CONV_WIDTH, c0:c1] * z3
            for k in range(1, CONV_WIDTH):
                lo = CONV_WIDTH - 1 - k
                xa_h = xa_h + cw[lo:lo + 1, c0:c1] * zext[lo:lo + group]
            xa2 = xa_h.reshape(rows, blk)
            xa_bf = xa2.astype(BF16)
            th_r = jnp.tanh(_dot(xa_bf, wa_ref[n]) + ba_h[:, c0:c1])
            th_i = jnp.tanh(_dot(xa_bf, wi_ref[n]) + bi_h[:, c0:c1])
            a = jnp.exp(c_a[:, c0:c1] + c_a[:, c0:c1] * th_r)
            u = jnp.sqrt(1.0 - a * a) * ((1.0 + th_i) * xa2)
            hseq, hlast = _lru_scan(a.reshape(group, V7X_SUBLANES, blk),
                                    u.reshape(group, V7X_SUBLANES, blk), hcar_ref[:, c0:c1])
            hcar_ref[:, c0:c1] = hlast
            zg = sec(1, c0, c1)
            term_a.append(((1.0 + jnp.tanh(sec(4, c0, c1))) * one_plus_tanh_gelu(zg))
                          * (zg * hseq.reshape(rows, blk)))

        project(4)
        zv = sec(3, 0, d)
        gv2 = zv * one_plus_tanh_gelu(zv)
        project(5)
        mu = jnp.mean(gv2, axis=-1, keepdims=True)
        xc = gv2 - mu
        var = jnp.mean(xc * xc, axis=-1, keepdims=True)
        v_bf = (xc * lax.rsqrt(var + 4.0 * EPS) * lng_ref[...] + lnb_ref[...]).astype(BF16)
        project(6)
        gdim = d // SGU_GROUPS
        term_b = []
        for g in range(SGU_GROUPS):
            c0, c1 = g * gdim, (g + 1) * gdim
            if g in (1, 3, 5, 6, 7):
                project({1: 7, 3: 8, 5: 9, 6: 10, 7: 11}[g])
            sp = _dot(wsm_ref[g], v_bf[:, c0:c1]) + bsp_ref[:, g:g + 1]
            zu = sec(2, c0, c1)
            term_b.append(((1.0 + jnp.tanh(sec(5, c0, c1))) * one_plus_tanh_gelu(zu)) * (zu * sp))
        merged4 = jnp.concatenate(term_a, axis=1) + jnp.concatenate(term_b, axis=1)

        x1 = x + _dot(merged4.astype(BF16), wout_ref[...])

        hn = _rmsnorm(x1, ffn_ref[...])
        hp = _pack_bf16_pair(hn[:, :half], hn[:, half:])

        @pl.when(j >= 3)
        def _():
            for c in put(tb - 2):
                c.wait()

        x1buf[slot] = x1.reshape(group, V7X_SUBLANES, d)
        hpbuf[slot] = hp.reshape(group, V7X_SUBLANES, half)

        @pl.when(j >= 1)
        def _():
            for c in put(tb):
                c.start()

    @pl.when(lax.rem(j, 2) == 0)
    def _():
        compute(z0_ref, z1_ref)

    @pl.when(lax.rem(j, 2) == 1)
    def _():
        compute(z1_ref, z0_ref)

    @pl.when(j == ntile)
    def _():
        for c in put(tb):
            c.wait()

        @pl.when(ntile >= 2)
        def _():
            for c in put(tb - 1):
                c.wait()


def _mixer_call(x, mix_norm, w_in, conv_w, conv_b, wa_blk, ba, wi_blk, bi, lam, ln_g, ln_b, ws,
                bs_tile, w_out, ffn_norm):
    cfg = _tiles()
    bsz, seq, d = x.shape
    ts = cfg["mixer_rows"]
    group = ts // V7X_SUBLANES
    nseq = seq // ts
    ntile = bsz * nseq
    row1 = (1, d)
    in_specs = [
        pl.BlockSpec(memory_space=pl.ANY),
        _const_spec(row1),
        _const_spec(w_in.shape),
        _const_spec(conv_w.shape), _const_spec(row1),
        _const_spec(wa_blk.shape), _const_spec(row1),
        _const_spec(wi_blk.shape), _const_spec(row1),
        _const_spec(row1),
        _const_spec(row1), _const_spec(row1),
        _const_spec(ws.shape), _const_spec(bs_tile.shape),
        _const_spec(w_out.shape), _const_spec(row1),
    ]
    out_shape = [
        jax.ShapeDtypeStruct((bsz, seq, d), F32),
        jax.ShapeDtypeStruct((bsz, seq, d // 2), U32),
    ]
    out_specs = [
        pl.BlockSpec(memory_space=pl.ANY),
        pl.BlockSpec(memory_space=pl.ANY),
    ]
    scratch = [
        pltpu.VMEM((3, group, V7X_SUBLANES, d), F32),
        pltpu.VMEM((ts, w_in.shape[1]), F32),
        pltpu.VMEM((ts, w_in.shape[1]), F32),
        pltpu.VMEM((2, group, V7X_SUBLANES, d), F32),
        pltpu.VMEM((2, group, V7X_SUBLANES, d // 2), U32),
        pltpu.SemaphoreType.DMA((3,)),
        pltpu.SemaphoreType.DMA((2,)),
        pltpu.SemaphoreType.DMA((2,)),
        pltpu.VMEM((SGU_GROUPS, ts, ts), BF16),
        pltpu.VMEM((CONV_WIDTH - 1, V7X_SUBLANES, d), F32),
        pltpu.VMEM((1, d), F32),
    ]
    return pl.pallas_call(
        functools.partial(_mixer_kernel, nseq=nseq),
        grid=(ntile + 1,),
        in_specs=in_specs,
        out_specs=out_specs,
        out_shape=out_shape,
        scratch_shapes=scratch,
        compiler_params=pltpu.CompilerParams(
            dimension_semantics=("arbitrary",),
            vmem_limit_bytes=cfg["mixer_vmem"]),
        name="mixer",
    )(x, mix_norm, w_in, conv_w, conv_b, wa_blk, ba, wi_blk, bi, lam, ln_g, ln_b, ws, bs_tile,
      w_out, ffn_norm)


def _router_kernel(hp_ref, wr_ref, br_ref, pos_ref, gate_ref, cnt_ref, ccar_ref, *, expert_capacity):
    rows = hp_ref.shape[0]

    @pl.when(pl.program_id(0) == 0)
    def _():
        ccar_ref[...] = jnp.zeros_like(ccar_ref)

    lo, hi = _unpack_bf16_pair(hp_ref[...])
    hn = jnp.concatenate([lo, hi], axis=1)
    logits = _dot(hn.astype(BF16), wr_ref[...])
    lt = jnp.transpose(logits) + br_ref[...]
    sub = lax.broadcasted_iota(jnp.int32, (V7X_SUBLANES, rows), 0)
    subf = sub.astype(F32)
    big = jnp.float32(1e9)

    lg = jnp.where(sub < N_GROUPS, lt[0:V7X_SUBLANES, :], -jnp.inf)
    g_exp = jnp.exp(lg - jnp.max(lg, axis=0, keepdims=True))
    g_prob = g_exp / jnp.sum(g_exp, axis=0, keepdims=True)
    g_top = jnp.max(g_prob, axis=0, keepdims=True)
    g_idx = jnp.min(jnp.where(g_prob == g_top, subf, big), axis=0, keepdims=True)

    e_sel = jnp.zeros((EXPERTS_PER_GROUP, rows), F32)
    for g in range(N_GROUPS):
        r0 = EXPERT_ROW0 + g * EXPERTS_PER_GROUP
        e_sel = jnp.where(g_idx == g, lt[r0:r0 + EXPERTS_PER_GROUP, :], e_sel)
    e_exp = jnp.exp(e_sel - jnp.max(e_sel, axis=0, keepdims=True))
    e_prob = e_exp / jnp.sum(e_exp, axis=0, keepdims=True)
    p1 = jnp.max(e_prob, axis=0, keepdims=True)
    i1 = jnp.min(jnp.where(e_prob == p1, subf, big), axis=0, keepdims=True)
    rest = jnp.where(subf == i1, -1.0, e_prob)
    p2 = jnp.max(rest, axis=0, keepdims=True)
    i2 = jnp.min(jnp.where(rest == p2, subf, big), axis=0, keepdims=True)
    psum = p1 + p2
    gate1 = g_top * (p1 / psum)
    gate2 = g_top * (p2 / psum)
    gid1 = g_idx * EXPERTS_PER_GROUP + i1
    gid2 = g_idx * EXPERTS_PER_GROUP + i2

    eid = lax.broadcasted_iota(jnp.int32, (N_EXPERTS, rows), 0).astype(F32)
    hit1 = eid == gid1
    hit2 = eid == gid2
    cnt = jnp.where(hit1 | hit2, 1.0, 0.0)
    sb = V7X_MXU_DIM
    before = (lax.broadcasted_iota(jnp.int32, (sb, sb), 0)
              < lax.broadcasted_iota(jnp.int32, (sb, sb), 1))
    before = jnp.where(before, 1.0, 0.0).astype(BF16)
    running = ccar_ref[:, 0:1]
    base = []
    for q in range(rows // sb):
        part = cnt[:, q * sb:(q + 1) * sb]
        base.append(running + _dot(part.astype(BF16), before))
        running = running + jnp.sum(part, axis=1, keepdims=True)
    base = jnp.concatenate(base, axis=1)
    rank1 = jnp.sum(jnp.where(hit1, base, 0.0), axis=0, keepdims=True)
    rank2 = jnp.sum(jnp.where(hit2, base, 0.0), axis=0, keepdims=True)
    total = jnp.broadcast_to(running, ccar_ref.shape)
    ccar_ref[...] = total
    cnt_ref[...] = total
    cap = float(expert_capacity)
    zero = jnp.zeros((V7X_SUBLANES - TOP_K, rows), F32)
    pos = jnp.concatenate([gid1 * cap + rank1, gid2 * cap + rank2, zero], axis=0)
    pos_ref[...] = pos.astype(jnp.int32)
    gate_ref[...] = jnp.concatenate([gate1, gate2, zero], axis=0)


def _router_call(hp, w_router, b_router):
    cfg = _tiles()
    ntok, half = hp.shape
    tr = cfg["router_rows"]
    return pl.pallas_call(
        functools.partial(_router_kernel, expert_capacity=ntok),
        grid=(ntok // tr,),
        in_specs=[
            pl.BlockSpec((tr, half), lambda i: (i, 0)),
            _const_spec(w_router.shape),
            _const_spec(b_router.shape),
        ],
        out_specs=[
            pl.BlockSpec((V7X_SUBLANES, tr), lambda i: (0, i)),
            pl.BlockSpec((V7X_SUBLANES, tr), lambda i: (0, i)),
            pl.BlockSpec((N_EXPERTS, V7X_LANES), lambda i: (0, 0)),
        ],
        out_shape=[
            jax.ShapeDtypeStruct((V7X_SUBLANES, ntok), jnp.int32),
            jax.ShapeDtypeStruct((V7X_SUBLANES, ntok), F32),
            jax.ShapeDtypeStruct((N_EXPERTS, V7X_LANES), F32),
        ],
        scratch_shapes=[pltpu.VMEM((N_EXPERTS, V7X_LANES), F32)],
        compiler_params=pltpu.CompilerParams(
            dimension_semantics=("arbitrary",),
            vmem_limit_bytes=cfg["router_vmem"]),
        name="router",
    )(hp, w_router, b_router)


def _expert_kernel(nt_ref, base_ref, hs_hbm, w1_ref, w3_ref, w2_ref, ys_hbm,
                   hbuf, ybuf, hsem, ysem, w1b_ref, w3b_ref, w2b_ref, *, capacity):
    e = pl.program_id(0)
    n_exp = pl.num_programs(0)
    nt = nt_ref[e]
    base = base_ref[e]
    n_in, tm, _ = hbuf.shape
    n_out = ybuf.shape[0]

    def load(expert, t):
        slot = lax.rem(base_ref[expert] + t, n_in)
        rows = pl.ds(expert * capacity + t * tm, tm)
        return pltpu.make_async_copy(hs_hbm.at[rows], hbuf.at[slot], hsem.at[slot])

    def store(t, slot):
        rows = pl.ds(e * capacity + t * tm, tm)
        return pltpu.make_async_copy(ybuf.at[slot], ys_hbm.at[rows], ysem.at[slot])

    def start_first_loads(expert):
        for t0 in range(n_in - 1):
            @pl.when(nt_ref[expert] > t0)
            def _():
                load(expert, t0).start()

    @pl.when(e == 0)
    def _():
        start_first_loads(e)

    w1b_ref[...] = w1_ref[...].astype(BF16)
    w3b_ref[...] = w3_ref[...].astype(BF16)
    w2b_ref[...] = w2_ref[...].astype(BF16)

    @pl.loop(0, nt)
    def _(t):
        @pl.when(t + (n_in - 1) < nt)
        def _():
            load(e, t + (n_in - 1)).start()

        load(e, t).wait()
        lo, hi = _unpack_bf16_pair(hbuf[lax.rem(base + t, n_in)])
        h = jnp.concatenate([lo, hi], axis=1).astype(BF16)
        a = _dot(h, w1b_ref[...])
        b = _dot(h, w3b_ref[...])
        hid = (a * _sigmoid(a)) * b
        y = _dot(hid.astype(BF16), w2b_ref[...])
        half = y.shape[1] // 2
        slot = lax.rem(base + t, n_out)

        @pl.when(base + t >= n_out)
        def _():
            store(t, slot).wait()

        ybuf[slot] = _pack_bf16_pair(y[:, :half], y[:, half:])
        store(t, slot).start()

    @pl.when(e + 1 < n_exp)
    def _():
        start_first_loads(jnp.minimum(e + 1, n_exp - 1))

    @pl.when(e + 1 == n_exp)
    def _():
        total = base + nt
        for back in range(1, n_out + 1):
            @pl.when(total >= back)
            def _():
                store(0, lax.rem(total - back, n_out)).wait()


def _expert_call(tiles_per_expert, hs, w1, w3, w2, capacity):
    cfg = _tiles()
    tm = cfg["expert_rows"]
    prow, half = hs.shape
    n_exp, d, f = w1.shape

    def w_map(e, nt, base):
        return (e, 0, 0)

    grid_spec = pltpu.PrefetchScalarGridSpec(
        num_scalar_prefetch=2,
        grid=(n_exp,),
        in_specs=[
            pl.BlockSpec(memory_space=pl.ANY),
            pl.BlockSpec((None, d, f), w_map),
            pl.BlockSpec((None, d, f), w_map),
            pl.BlockSpec((None, f, d), w_map),
        ],
        out_specs=pl.BlockSpec(memory_space=pl.ANY),
        scratch_shapes=[
            pltpu.VMEM((EXPERT_LOOKAHEAD + 1, tm, half), U32),
            pltpu.VMEM((2, tm, half), U32),
            pltpu.SemaphoreType.DMA((EXPERT_LOOKAHEAD + 1,)),
            pltpu.SemaphoreType.DMA((2,)),
            pltpu.VMEM((d, f), BF16),
            pltpu.VMEM((d, f), BF16),
            pltpu.VMEM((f, d), BF16),
        ],
    )
    return pl.pallas_call(
        functools.partial(_expert_kernel, capacity=capacity),
        grid_spec=grid_spec,
        out_shape=jax.ShapeDtypeStruct((prow, half), U32),
        compiler_params=pltpu.CompilerParams(
            dimension_semantics=("arbitrary",),
            vmem_limit_bytes=cfg["expert_vmem"]),
        name="experts",
    )(tiles_per_expert, jnp.cumsum(tiles_per_expert) - tiles_per_expert, hs, w1, w3, w2)


def _sc_mesh():
    return plsc.VectorSubcoreMesh(core_axis_name="c", subcore_axis_name="s",
                                  num_cores=V7X_SC_CORES, num_subcores=V7X_SC_SUBCORES)


def _sc_worker_id():
    return lax.axis_index("s") * V7X_SC_CORES + lax.axis_index("c")


def _dispatch_call(hp, pos_w, out_rows):
    cfg = _tiles()
    ntok, half = hp.shape
    nw, topk, nch, ch = pos_w.shape
    per_w = nch * ch

    def body(hp_hbm, pos_hbm, hs_hbm, idx_v, buf, sem):
        del sem
        wid = _sc_worker_id()
        pltpu.sync_copy(pos_hbm.at[wid], idx_v)

        @pl.loop(0, nch)
        def _(c):
            pltpu.sync_copy(hp_hbm.at[pl.ds(wid * per_w + c * ch, ch)], buf)
            for k in range(topk):
                pltpu.sync_copy(buf, hs_hbm.at[idx_v.at[k, c]])

    assert nw == V7X_SC_CORES * V7X_SC_SUBCORES and nw * per_w == ntok and ch == cfg["sc_rows"]
    return pl.kernel(
        body,
        out_type=jax.ShapeDtypeStruct((out_rows, half), U32),
        mesh=_sc_mesh(),
        scratch_types=[
            pltpu.VMEM((topk, nch, ch), jnp.int32),
            pltpu.VMEM((ch, half), U32),
            pltpu.SemaphoreType.DMA,
        ],
        name="dispatch",
    )(hp, pos_w)


def _combine_call(ys, pos_w):
    cfg = _tiles()
    _, half = ys.shape
    nw, topk, nch, ch = pos_w.shape
    per_w = nch * ch
    ntok = nw * per_w

    def body(ys_hbm, pos_hbm, *rest):
        outs = rest[:topk]
        idx_v, buf, sem = rest[topk:]
        del sem
        wid = _sc_worker_id()
        pltpu.sync_copy(pos_hbm.at[wid], idx_v)

        @pl.loop(0, nch)
        def _(c):
            for k in range(topk):
                pltpu.sync_copy(ys_hbm.at[idx_v.at[k, c]], buf)
                pltpu.sync_copy(buf, outs[k].at[pl.ds(wid * per_w + c * ch, ch)])

    assert nw == V7X_SC_CORES * V7X_SC_SUBCORES and ch == cfg["sc_rows"]
    return pl.kernel(
        body,
        out_type=[jax.ShapeDtypeStruct((ntok, half), U32)] * topk,
        mesh=_sc_mesh(),
        scratch_types=[
            pltpu.VMEM((topk, nch, ch), jnp.int32),
            pltpu.VMEM((ch, half), U32),
            pltpu.SemaphoreType.DMA,
        ],
        name="combine",
    )(ys, pos_w)


def _ple_kernel(x1_ref, yg0_ref, yg1_ref, gate_ref, p_ref, plen_ref, wg_ref, wu_ref, fin_ref, o_ref):
    lo0, hi0 = _unpack_bf16_pair(yg0_ref[...])
    lo1, hi1 = _unpack_bf16_pair(yg1_ref[...])
    g0 = gate_ref[:, 0:1]
    g1 = gate_ref[:, 1:2]
    moe = g0 * jnp.concatenate([lo0, hi0], axis=1) + g1 * jnp.concatenate([lo1, hi1], axis=1)
    x2 = x1_ref[...] + moe
    r = _rmsnorm(x2, plen_ref[...]).astype(BF16)
    gt = _sigmoid(_dot(r, wg_ref[...]))
    up = _dot(p_ref[...].astype(BF16), wu_ref[...])
    x3 = x2 + gt * up
    o_ref[...] = _rmsnorm(x3, fin_ref[...])


def _ple_call(x1, yg0, yg1, gates, p, ple_norm, wg, wu, final_norm):
    cfg = _tiles()
    ntok, d = x1.shape
    tp = cfg["ple_rows"]
    pdim = p.shape[1]
    return pl.pallas_call(
        _ple_kernel,
        grid=(ntok // tp,),
        in_specs=[
            pl.BlockSpec((tp, d), lambda i: (i, 0)),
            pl.BlockSpec((tp, d // 2), lambda i: (i, 0)),
            pl.BlockSpec((tp, d // 2), lambda i: (i, 0)),
            pl.BlockSpec((tp, TOP_K), lambda i: (i, 0)),
            pl.BlockSpec((tp, pdim), lambda i: (i, 0)),
            _const_spec((1, d)),
            _const_spec(wg.shape),
            _const_spec(wu.shape),
            _const_spec((1, d)),
        ],
        out_specs=pl.BlockSpec((tp, d), lambda i: (i, 0)),
        out_shape=jax.ShapeDtypeStruct((ntok, d), F32),
        compiler_params=pltpu.CompilerParams(
            dimension_semantics=("arbitrary",),
            vmem_limit_bytes=cfg["ple_vmem"]),
        name="ple",
    )(x1, yg0, yg1, gates, p, ple_norm, wg, wu, final_norm)


def _blockdiag_pack(w):
    nb, bd, _ = w.shape
    per = V7X_MXU_DIM // bd
    w4 = w.reshape(nb // per, per, bd, bd)
    eye = jnp.eye(per, dtype=w.dtype)
    out = jnp.einsum("jpab,pq->jpaqb", w4, eye)
    return out.reshape(nb // per, V7X_MXU_DIM, V7X_MXU_DIM).astype(BF16)


def kernel(x, p, mix_norm, w_in, conv_w, conv_b, lru_wa, lru_ba, lru_wi, lru_bi, lru_lambda, sgu_ln_g, sgu_ln_b, sgu_ws, sgu_bs, w_out, ffn_norm, router_group_w, router_group_b, router_expert_w, router_expert_b, expert_w1, expert_w3, expert_w2, ple_norm, ple_gate_w, ple_up_w, final_norm):
    cfg = _tiles()
    bsz, seq, d = x.shape
    ntok = bsz * seq
    tm = cfg["expert_rows"]
    depth = w_in.shape[0]
    assert depth == 1, "the ple kernel applies the final norm, so it must be the last layer"
    l = 0
    w_router = jnp.zeros((d, ROUTER_ROWS), F32)
    w_router = w_router.at[:, :N_GROUPS].set(router_group_w[l])
    w_router = w_router.at[:, EXPERT_ROW0:EXPERT_ROW0 + N_EXPERTS].set(router_expert_w[l])
    b_router = jnp.zeros((ROUTER_ROWS, 1), F32)
    b_router = b_router.at[:N_GROUPS, 0].set(router_group_b[l])
    b_router = b_router.at[EXPERT_ROW0:EXPERT_ROW0 + N_EXPERTS, 0].set(router_expert_b[l])
    col_scale = jnp.concatenate([jnp.ones((4 * d,), F32), jnp.full((2 * d,), 0.5, F32)])
    ts = cfg["mixer_rows"]
    group = ts // V7X_SUBLANES
    bs_tile = jnp.tile(sgu_bs[l], (1, ts // CHUNK)).reshape(SGU_GROUPS, V7X_SUBLANES, group)
    bs_tile = jnp.transpose(bs_tile, (2, 1, 0)).reshape(ts, SGU_GROUPS)
    x1, hp = _mixer_call(
        x, mix_norm[l][None], (w_in[l] * col_scale[None, :]).astype(BF16), conv_w[l], conv_b[l][None],
        _blockdiag_pack(lru_wa[l]), lru_ba[l][None], _blockdiag_pack(lru_wi[l]), lru_bi[l][None],
        lru_lambda[l][None], sgu_ln_g[l][None], sgu_ln_b[l][None], sgu_ws[l], bs_tile,
        (0.25 * w_out[l]).astype(BF16), ffn_norm[l][None])
    hp = hp.reshape(ntok, d // 2)
    pos, gate, cnt = _router_call(hp, w_router.astype(BF16), b_router)

    cap = ntok
    tiles_per_expert = (cnt[:, 0].astype(jnp.int32) + tm - 1) // tm
    nw = V7X_SC_CORES * V7X_SC_SUBCORES
    ch = cfg["sc_rows"]
    pos_w = jnp.transpose(pos[:TOP_K].reshape(TOP_K, nw, ntok // (nw * ch), ch), (1, 0, 2, 3))

    hs = _dispatch_call(hp, pos_w, N_EXPERTS * cap)
    ys = _expert_call(tiles_per_expert, hs, expert_w1[l], expert_w3[l], expert_w2[l], cap)
    yg0, yg1 = _combine_call(ys, pos_w)
    gates = jnp.transpose(gate[:TOP_K])

    out = _ple_call(x1.reshape(ntok, d), yg0, yg1, gates, p[l].reshape(ntok, -1), ple_norm[l][None],
                    ple_gate_w[l].astype(BF16), ple_up_w[l].astype(BF16), final_norm[None])
    return out.reshape(bsz, seq, d)
```

```python
import functools

import jax
import jax.numpy as jnp
from jax import lax
from jax.experimental import pallas as pl
from jax.experimental.pallas import tpu as pltpu
from jax.experimental.pallas import tpu_sc as plsc

F32 = jnp.float32
BF16 = jnp.bfloat16
U32 = jnp.uint32

LRU_BLOCKS = 16
CONV_WIDTH = 4
LRU_C = 8.0
SGU_GROUPS = 8
CHUNK = 128
N_GROUPS = 4
EXPERTS_PER_GROUP = 8
N_EXPERTS = N_GROUPS * EXPERTS_PER_GROUP
TOP_K = 2
EPS = 1e-6

V7X_MXU_DIM = 256
V7X_SUBLANES = 8
V7X_LANES = 128
V7X_VMEM_BYTES = 64 * 1024 * 1024
V7X_SC_CORES = 2
V7X_SC_SUBCORES = 16

EXPERT_LOOKAHEAD = 2
PLE_SUBBLOCKS = 4
ROUTER_ROWS = V7X_LANES
EXPERT_ROW0 = V7X_SUBLANES


def _tiles():
    return dict(
        mixer_rows=256,
        expert_rows=512,
        ple_rows=1024,
        sc_rows=64,
        router_rows=1024,
        mixer_vmem=52 * 1024 * 1024,
        expert_vmem=40 * 1024 * 1024,
        ple_vmem=48 * 1024 * 1024,
        router_vmem=32 * 1024 * 1024,
    )


def _dot(a, b):
    return jnp.dot(a, b, preferred_element_type=F32)


def _sigmoid(x):
    return 0.5 * jnp.tanh(0.5 * x) + 0.5


def _rmsnorm(x, g):
    ms = jnp.mean(x * x, axis=-1, keepdims=True)
    return x * lax.rsqrt(ms + EPS) * g


def _pack_bf16_pair(lo, hi):
    lo_b = lax.bitcast_convert_type(lo.astype(BF16).astype(F32), U32)
    hi_b = lax.bitcast_convert_type(hi.astype(BF16).astype(F32), U32)
    return (hi_b & jnp.uint32(0xFFFF0000)) | lax.shift_right_logical(lo_b, jnp.uint32(16))


def _unpack_bf16_pair(w):
    lo = lax.bitcast_convert_type(lax.shift_left(w, jnp.uint32(16)), F32)
    hi = lax.bitcast_convert_type(w & jnp.uint32(0xFFFF0000), F32)
    return lo, hi


def _const_spec(shape):
    zeros = (0,) * len(shape)
    return pl.BlockSpec(shape, lambda *_: zeros, pipeline_mode=pl.Buffered(1))


def _tile_copies(hbm, buf, sem, b, row0, slot, to_hbm):
    group = buf.shape[1]
    copies = []
    for r in range(V7X_SUBLANES):
        hbm_rows = hbm.at[b, pl.ds(row0 + group * r, group), :]
        vmem_rows = buf.at[slot, :, r, :]
        src, dst = (vmem_rows, hbm_rows) if to_hbm else (hbm_rows, vmem_rows)
        copies.append(pltpu.make_async_copy(src, dst, sem.at[slot]))
    return copies


def _lru_scan(a, u, h0):
    group = a.shape[0]
    acc_a = [a[0]]
    acc_u = [u[0]]
    for g in range(1, group):
        acc_a.append(a[g] * acc_a[-1])
        acc_u.append(a[g] * acc_u[-1] + u[g])
    end_a, end_u = acc_a[-1], acc_u[-1]
    sub = lax.broadcasted_iota(jnp.int32, end_a.shape, 0)
    shift = 1
    while shift < V7X_SUBLANES:
        keep = sub >= shift
        a_sh = pltpu.roll(end_a, shift, axis=0)
        u_sh = pltpu.roll(end_u, shift, axis=0)
        end_u = jnp.where(keep, end_a * u_sh + end_u, end_u)
        end_a = jnp.where(keep, end_a * a_sh, end_a)
        shift *= 2
    h_end = end_a * h0 + end_u
    h_in = jnp.where(sub == 0, h0, pltpu.roll(h_end, 1, axis=0))
    out = [acc_a[g] * h_in + acc_u[g] for g in range(group)]
    return jnp.stack(out, axis=0), h_end[V7X_SUBLANES - 1:V7X_SUBLANES, :]


def _mixer_kernel(x_hbm, mixn_ref, win_ref, convw_ref, convb_ref, wa_ref, ba_ref, wi_ref, bi_ref,
                  lam_ref, lng_ref, lnb_ref, ws_ref, bsp_ref, wout_ref, ffn_ref,
                  x1_hbm, hp_hbm,
                  xbuf, z0_ref, z1_ref, x1buf, hpbuf, xsem, x1sem, hpsem, wsm_ref, ztail_ref, hcar_ref,
                  *, nseq):
    j = pl.program_id(0)
    ntile = pl.num_programs(0) - 1
    _, group, _, d = xbuf.shape
    rows = group * V7X_SUBLANES
    half = d // 2
    ta = jnp.minimum(j, ntile - 1)
    tb = jnp.maximum(j - 1, 0)
    s = lax.rem(tb, nseq)
    slot = lax.rem(tb, 2)

    def fetch(t):
        return _tile_copies(x_hbm, xbuf, xsem, lax.div(t, nseq), lax.rem(t, nseq) * rows,
                            lax.rem(t, 3), to_hbm=False)

    def put(t):
        tb_, ts_, sl = lax.div(t, nseq), lax.rem(t, nseq) * rows, lax.rem(t, 2)
        return (_tile_copies(x1_hbm, x1buf, x1sem, tb_, ts_, sl, to_hbm=True)
                + _tile_copies(hp_hbm, hpbuf, hpsem, tb_, ts_, sl, to_hbm=True))

    @pl.when(j == 0)
    def _():
        for c in fetch(0):
            c.start()
        z1_ref[...] = jnp.zeros_like(z1_ref)
        i_idx = lax.broadcasted_iota(jnp.int32, (rows, rows), 0)
        j_idx = lax.broadcasted_iota(jnp.int32, (rows, rows), 1)
        t_i = group * lax.rem(i_idx, V7X_SUBLANES) + lax.div(i_idx, V7X_SUBLANES)
        t_j = group * lax.rem(j_idx, V7X_SUBLANES) + lax.div(j_idx, V7X_SUBLANES)
        keep = (t_i >= t_j) & (lax.div(t_i, CHUNK) == lax.div(t_j, CHUNK))
        pick_rows = jnp.where(t_i == j_idx, 1.0, 0.0).astype(BF16)
        pick_cols = jnp.where(i_idx == t_j, 1.0, 0.0).astype(BF16)
        reps = rows // CHUNK
        for g in range(SGU_GROUPS):
            w_chunk = ws_ref[g].astype(BF16)
            w_rows = jnp.concatenate([w_chunk] * reps, axis=1)
            w_full = jnp.concatenate([w_rows] * reps, axis=0)
            w_perm = _dot(_dot(pick_rows, w_full).astype(BF16), pick_cols)
            wsm_ref[g] = jnp.where(keep, w_perm, 0.0).astype(BF16)

    @pl.when(j + 1 < ntile)
    def _():
        for c in fetch(j + 1):
            c.start()

    @pl.when(j < ntile)
    def _():
        for c in fetch(j):
            c.wait()

    @pl.when(s == 0)
    def _():
        ztail_ref[...] = jnp.zeros_like(ztail_ref)
        hcar_ref[...] = jnp.zeros_like(hcar_ref)

    def compute(z_w, z_r):
        xa_in = xbuf[lax.rem(ta, 3)].reshape(rows, d)
        h_next = _rmsnorm(xa_in, mixn_ref[...]).astype(BF16)
        pw = d // 2

        def project(k):
            z_w[:, k * pw:(k + 1) * pw] = _dot(h_next, win_ref[:, k * pw:(k + 1) * pw])

        x = xbuf[lax.rem(tb, 3)].reshape(rows, d)

        def sec(k, c0, c1):
            return z_r[:, k * d + c0:k * d + c1]

        def one_plus_tanh_gelu(v):
            c = 0.7978845608028654
            return 1.0 + jnp.tanh(v * (c + (c * 0.044715) * (v * v)))

        cw = 0.5 * convw_ref[...]
        cb_h = 0.5 * convb_ref[...]
        ba_h = 0.5 * ba_ref[...]
        bi_h = 0.5 * bi_ref[...]
        neg_lam = -lam_ref[...]
        softplus = jnp.maximum(neg_lam, 0.0) + jnp.log1p(jnp.exp(-jnp.abs(neg_lam)))
        c_a = (-0.5 * LRU_C) * softplus
        blk = V7X_MXU_DIM
        sub3 = lax.broadcasted_iota(jnp.int32, (CONV_WIDTH - 1, V7X_SUBLANES, blk), 1)
        term_a = []
        for n in range(d // blk):
            project(n)
            c0, c1 = n * blk, (n + 1) * blk
            z3 = sec(0, c0, c1).reshape(group, V7X_SUBLANES, blk)
            tail = z3[group - (CONV_WIDTH - 1):]
            halo = jnp.where(sub3 == 0, pltpu.roll(ztail_ref[:, :, c0:c1], 1, axis=1),
                             pltpu.roll(tail, 1, axis=1))
            ztail_ref[:, :, c0:c1] = tail
            zext = jnp.concatenate([halo, z3], axis=0)
            xa_h = cb_h[:, c0:c1] + cw[CONV_WIDTH - 1:CONV_WIDTH, c0:c1] * z3
            for k in range(1, CONV_WIDTH):
                lo = CONV_WIDTH - 1 - k
                xa_h = xa_h + cw[lo:lo + 1, c0:c1] * zext[lo:lo + group]
            xa2 = xa_h.reshape(rows, blk)
            xa_bf = xa2.astype(BF16)
            th_r = jnp.tanh(_dot(xa_bf, wa_ref[n]) + ba_h[:, c0:c1])
            th_i = jnp.tanh(_dot(xa_bf, wi_ref[n]) + bi_h[:, c0:c1])
            a = jnp.exp(c_a[:, c0:c1] + c_a[:, c0:c1] * th_r)
            u = jnp.sqrt(1.0 - a * a) * ((1.0 + th_i) * xa2)
            hseq, hlast = _lru_scan(a.reshape(group, V7X_SUBLANES, blk),
                                    u.reshape(group, V7X_SUBLANES, blk), hcar_ref[:, c0:c1])
            hcar_ref[:, c0:c1] = hlast
            zg = sec(1, c0, c1)
            term_a.append(((1.0 + jnp.tanh(sec(4, c0, c1))) * one_plus_tanh_gelu(zg))
                          * (zg * hseq.reshape(rows, blk)))

        project(4)
        zv = sec(3, 0, d)
        gv2 = zv * one_plus_tanh_gelu(zv)
        project(5)
        mu = jnp.mean(gv2, axis=-1, keepdims=True)
        xc = gv2 - mu
        var = jnp.mean(xc * xc, axis=-1, keepdims=True)
        v_bf = (xc * lax.rsqrt(var + 4.0 * EPS) * lng_ref[...] + lnb_ref[...]).astype(BF16)
        project(6)
        gdim = d // SGU_GROUPS
        term_b = []
        for g in range(SGU_GROUPS):
            c0, c1 = g * gdim, (g + 1) * gdim
            if g in (1, 3, 5, 6, 7):
                project({1: 7, 3: 8, 5: 9, 6: 10, 7: 11}[g])
            sp = _dot(wsm_ref[g], v_bf[:, c0:c1]) + bsp_ref[:, g:g + 1]
            zu = sec(2, c0, c1)
            term_b.append(((1.0 + jnp.tanh(sec(5, c0, c1))) * one_plus_tanh_gelu(zu)) * (zu * sp))
        merged4 = jnp.concatenate(term_a, axis=1) + jnp.concatenate(term_b, axis=1)

        x1 = x + _dot(merged4.astype(BF16), wout_ref[...])

        hn = _rmsnorm(x1, ffn_ref[...])
        hp = _pack_bf16_pair(hn[:, :half], hn[:, half:])

        @pl.when(j >= 3)
        def _():
            for c in put(tb - 2):
                c.wait()

        x1buf[slot] = x1.reshape(group, V7X_SUBLANES, d)
        hpbuf[slot] = hp.reshape(group, V7X_SUBLANES, half)

        @pl.when(j >= 1)
        def _():
            for c in put(tb):
                c.start()

    @pl.when(lax.rem(j, 2) == 0)
    def _():
        compute(z0_ref, z1_ref)

    @pl.when(lax.rem(j, 2) == 1)
    def _():
        compute(z1_ref, z0_ref)

    @pl.when(j == ntile)
    def _():
        for c in put(tb):
            c.wait()

        @pl.when(ntile >= 2)
        def _():
            for c in put(tb - 1):
                c.wait()


def _mixer_call(x, mix_norm, w_in, conv_w, conv_b, wa_blk, ba, wi_blk, bi, lam, ln_g, ln_b, ws,
                bs_tile, w_out, ffn_norm):
    cfg = _tiles()
    bsz, seq, d = x.shape
    ts = cfg["mixer_rows"]
    group = ts // V7X_SUBLANES
    nseq = seq // ts
    ntile = bsz * nseq
    row1 = (1, d)
    in_specs = [
        pl.BlockSpec(memory_space=pl.ANY),
        _const_spec(row1),
        _const_spec(w_in.shape),
        _const_spec(conv_w.shape), _const_spec(row1),
        _const_spec(wa_blk.shape), _const_spec(row1),
        _const_spec(wi_blk.shape), _const_spec(row1),
        _const_spec(row1),
        _const_spec(row1), _const_spec(row1),
        _const_spec(ws.shape), _const_spec(bs_tile.shape),
        _const_spec(w_out.shape), _const_spec(row1),
    ]
    out_shape = [
        jax.ShapeDtypeStruct((bsz, seq, d), F32),
        jax.ShapeDtypeStruct((bsz, seq, d // 2), U32),
    ]
    out_specs = [
        pl.BlockSpec(memory_space=pl.ANY),
        pl.BlockSpec(memory_space=pl.ANY),
    ]
    scratch = [
        pltpu.VMEM((3, group, V7X_SUBLANES, d), F32),
        pltpu.VMEM((ts, w_in.shape[1]), F32),
        pltpu.VMEM((ts, w_in.shape[1]), F32),
        pltpu.VMEM((2, group, V7X_SUBLANES, d), F32),
        pltpu.VMEM((2, group, V7X_SUBLANES, d // 2), U32),
        pltpu.SemaphoreType.DMA((3,)),
        pltpu.SemaphoreType.DMA((2,)),
        pltpu.SemaphoreType.DMA((2,)),
        pltpu.VMEM((SGU_GROUPS, ts, ts), BF16),
        pltpu.VMEM((CONV_WIDTH - 1, V7X_SUBLANES, d), F32),
        pltpu.VMEM((1, d), F32),
    ]
    return pl.pallas_call(
        functools.partial(_mixer_kernel, nseq=nseq),
        grid=(ntile + 1,),
        in_specs=in_specs,
        out_specs=out_specs,
        out_shape=out_shape,
        scratch_shapes=scratch,
        compiler_params=pltpu.CompilerParams(
            dimension_semantics=("arbitrary",),
            vmem_limit_bytes=cfg["mixer_vmem"]),
        name="mixer",
    )(x, mix_norm, w_in, conv_w, conv_b, wa_blk, ba, wi_blk, bi, lam, ln_g, ln_b, ws, bs_tile,
      w_out, ffn_norm)


def _router_kernel(hp_ref, wr_ref, br_ref, pos_ref, gate_ref, cnt_ref, ccar_ref, *, expert_capacity):
    rows = hp_ref.shape[0]

    @pl.when(pl.program_id(0) == 0)
    def _():
        ccar_ref[...] = jnp.zeros_like(ccar_ref)

    lo, hi = _unpack_bf16_pair(hp_ref[...])
    hn = jnp.concatenate([lo, hi], axis=1)
    logits = _dot(hn.astype(BF16), wr_ref[...])
    lt = jnp.transpose(logits) + br_ref[...]
    sub = lax.broadcasted_iota(jnp.int32, (V7X_SUBLANES, rows), 0)
    subf = sub.astype(F32)
    big = jnp.float32(1e9)

    lg = jnp.where(sub < N_GROUPS, lt[0:V7X_SUBLANES, :], -jnp.inf)
    g_exp = jnp.exp(lg - jnp.max(lg, axis=0, keepdims=True))
    g_prob = g_exp / jnp.sum(g_exp, axis=0, keepdims=True)
    g_top = jnp.max(g_prob, axis=0, keepdims=True)
    g_idx = jnp.min(jnp.where(g_prob == g_top, subf, big), axis=0, keepdims=True)

    e_sel = jnp.zeros((EXPERTS_PER_GROUP, rows), F32)
    for g in range(N_GROUPS):
        r0 = EXPERT_ROW0 + g * EXPERTS_PER_GROUP
        e_sel = jnp.where(g_idx == g, lt[r0:r0 + EXPERTS_PER_GROUP, :], e_sel)
    e_exp = jnp.exp(e_sel - jnp.max(e_sel, axis=0, keepdims=True))
    e_prob = e_exp / jnp.sum(e_exp, axis=0, keepdims=True)
    p1 = jnp.max(e_prob, axis=0, keepdims=True)
    i1 = jnp.min(jnp.where(e_prob == p1, subf, big), axis=0, keepdims=True)
    rest = jnp.where(subf == i1, -1.0, e_prob)
    p2 = jnp.max(rest, axis=0, keepdims=True)
    i2 = jnp.min(jnp.where(rest == p2, subf, big), axis=0, keepdims=True)
    psum = p1 + p2
    gate1 = g_top * (p1 / psum)
    gate2 = g_top * (p2 / psum)
    gid1 = g_idx * EXPERTS_PER_GROUP + i1
    gid2 = g_idx * EXPERTS_PER_GROUP + i2

    eid = lax.broadcasted_iota(jnp.int32, (N_EXPERTS, rows), 0).astype(F32)
    hit1 = eid == gid1
    hit2 = eid == gid2
    cnt = jnp.where(hit1 | hit2, 1.0, 0.0)
    sb = V7X_MXU_DIM
    before = (lax.broadcasted_iota(jnp.int32, (sb, sb), 0)
              < lax.broadcasted_iota(jnp.int32, (sb, sb), 1))
    before = jnp.where(before, 1.0, 0.0).astype(BF16)
    running = ccar_ref[:, 0:1]
    base = []
    for q in range(rows // sb):
        part = cnt[:, q * sb:(q + 1) * sb]
        base.append(running + _dot(part.astype(BF16), before))
        running = running + jnp.sum(part, axis=1, keepdims=True)
    base = jnp.concatenate(base, axis=1)
    rank1 = jnp.sum(jnp.where(hit1, base, 0.0), axis=0, keepdims=True)
    rank2 = jnp.sum(jnp.where(hit2, base, 0.0), axis=0, keepdims=True)
    total = jnp.broadcast_to(running, ccar_ref.shape)
    ccar_ref[...] = total
    cnt_ref[...] = total
    cap = float(expert_capacity)
    zero = jnp.zeros((V7X_SUBLANES - TOP_K, rows), F32)
    pos = jnp.concatenate([gid1 * cap + rank1, gid2 * cap + rank2, zero], axis=0)
    pos_ref[...] = pos.astype(jnp.int32)
    gate_ref[...] = jnp.transpose(jnp.concatenate([gate1, gate2, zero], axis=0))


def _router_call(hp, w_router, b_router):
    cfg = _tiles()
    ntok, half = hp.shape
    tr = cfg["router_rows"]
    return pl.pallas_call(
        functools.partial(_router_kernel, expert_capacity=ntok),
        grid=(ntok // tr,),
        in_specs=[
            pl.BlockSpec((tr, half), lambda i: (i, 0)),
            _const_spec(w_router.shape),
            _const_spec(b_router.shape),
        ],
        out_specs=[
            pl.BlockSpec((V7X_SUBLANES, tr), lambda i: (0, i)),
            pl.BlockSpec((tr, V7X_SUBLANES), lambda i: (i, 0)),
            pl.BlockSpec((N_EXPERTS, V7X_LANES), lambda i: (0, 0)),
        ],
        out_shape=[
            jax.ShapeDtypeStruct((V7X_SUBLANES, ntok), jnp.int32),
            jax.ShapeDtypeStruct((ntok, V7X_SUBLANES), F32),
            jax.ShapeDtypeStruct((N_EXPERTS, V7X_LANES), F32),
        ],
        scratch_shapes=[pltpu.VMEM((N_EXPERTS, V7X_LANES), F32)],
        compiler_params=pltpu.CompilerParams(
            dimension_semantics=("arbitrary",),
            vmem_limit_bytes=cfg["router_vmem"]),
        name="router",
    )(hp, w_router, b_router)


def _expert_kernel(nt_ref, base_ref, hs_hbm, w1_ref, w3_ref, w2_ref, ys_hbm,
                   hbuf, ybuf, hsem, ysem, w1b_ref, w3b_ref, w2b_ref, *, capacity):
    e = pl.program_id(0)
    n_exp = pl.num_programs(0)
    nt = nt_ref[e]
    base = base_ref[e]
    n_in, tm, _ = hbuf.shape
    n_out = ybuf.shape[0]

    def load(expert, t):
        slot = lax.rem(base_ref[expert] + t, n_in)
        rows = pl.ds(expert * capacity + t * tm, tm)
        return pltpu.make_async_copy(hs_hbm.at[rows], hbuf.at[slot], hsem.at[slot])

    def store(t, slot):
        rows = pl.ds(e * capacity + t * tm, tm)
        return pltpu.make_async_copy(ybuf.at[slot], ys_hbm.at[rows], ysem.at[slot])

    def start_first_loads(expert):
        for t0 in range(n_in - 1):
            @pl.when(nt_ref[expert] > t0)
            def _():
                load(expert, t0).start()

    @pl.when(e == 0)
    def _():
        start_first_loads(e)

    w1b_ref[...] = w1_ref[...].astype(BF16)
    w3b_ref[...] = w3_ref[...].astype(BF16)
    w2b_ref[...] = w2_ref[...].astype(BF16)

    @pl.loop(0, nt)
    def _(t):
        @pl.when(t + (n_in - 1) < nt)
        def _():
            load(e, t + (n_in - 1)).start()

        load(e, t).wait()
        lo, hi = _unpack_bf16_pair(hbuf[lax.rem(base + t, n_in)])
        h = jnp.concatenate([lo, hi], axis=1).astype(BF16)
        a = _dot(h, w1b_ref[...])
        b = _dot(h, w3b_ref[...])
        hid = (a * _sigmoid(a)) * b
        y = _dot(hid.astype(BF16), w2b_ref[...])
        half = y.shape[1] // 2
        slot = lax.rem(base + t, n_out)

        @pl.when(base + t >= n_out)
        def _():
            store(t, slot).wait()

        ybuf[slot] = _pack_bf16_pair(y[:, :half], y[:, half:])
        store(t, slot).start()

    @pl.when(e + 1 < n_exp)
    def _():
        start_first_loads(jnp.minimum(e + 1, n_exp - 1))

    @pl.when(e + 1 == n_exp)
    def _():
        total = base + nt
        for back in range(1, n_out + 1):
            @pl.when(total >= back)
            def _():
                store(0, lax.rem(total - back, n_out)).wait()


def _expert_call(tiles_per_expert, hs, w1, w3, w2, capacity):
    cfg = _tiles()
    tm = cfg["expert_rows"]
    prow, half = hs.shape
    n_exp, d, f = w1.shape

    def w_map(e, nt, base):
        return (e, 0, 0)

    grid_spec = pltpu.PrefetchScalarGridSpec(
        num_scalar_prefetch=2,
        grid=(n_exp,),
        in_specs=[
            pl.BlockSpec(memory_space=pl.ANY),
            pl.BlockSpec((None, d, f), w_map),
            pl.BlockSpec((None, d, f), w_map),
            pl.BlockSpec((None, f, d), w_map),
        ],
        out_specs=pl.BlockSpec(memory_space=pl.ANY),
        scratch_shapes=[
            pltpu.VMEM((EXPERT_LOOKAHEAD + 1, tm, half), U32),
            pltpu.VMEM((2, tm, half), U32),
            pltpu.SemaphoreType.DMA((EXPERT_LOOKAHEAD + 1,)),
            pltpu.SemaphoreType.DMA((2,)),
            pltpu.VMEM((d, f), BF16),
            pltpu.VMEM((d, f), BF16),
            pltpu.VMEM((f, d), BF16),
        ],
    )
    return pl.pallas_call(
        functools.partial(_expert_kernel, capacity=capacity),
        grid_spec=grid_spec,
        out_shape=jax.ShapeDtypeStruct((prow, half), U32),
        compiler_params=pltpu.CompilerParams(
            dimension_semantics=("arbitrary",),
            vmem_limit_bytes=cfg["expert_vmem"]),
        name="experts",
    )(tiles_per_expert, jnp.cumsum(tiles_per_expert) - tiles_per_expert, hs, w1, w3, w2)


def _sc_mesh():
    return plsc.VectorSubcoreMesh(core_axis_name="c", subcore_axis_name="s",
                                  num_cores=V7X_SC_CORES, num_subcores=V7X_SC_SUBCORES)


def _sc_worker_id():
    return lax.axis_index("s") * V7X_SC_CORES + lax.axis_index("c")


def _dispatch_call(hp, pos_w, out_rows):
    cfg = _tiles()
    ntok, half = hp.shape
    nw, topk, nch, ch = pos_w.shape
    per_w = nch * ch

    def body(hp_hbm, pos_hbm, hs_hbm, idx_v, buf, rsem, wsem):
        wid = _sc_worker_id()
        pltpu.sync_copy(pos_hbm.at[wid], idx_v)

        def read(c):
            rows = hp_hbm.at[pl.ds(wid * per_w + c * ch, ch)]
            return pltpu.make_async_copy(rows, buf.at[c % 2], rsem.at[c % 2])

        def writes(c):
            return [pltpu.make_async_copy(buf.at[c % 2], hs_hbm.at[idx_v.at[k, c]], wsem.at[c % 2])
                    for k in range(topk)]

        read(0).start()
        for c in range(nch):
            read(c).wait()
            if c >= 1:
                for w in writes(c - 1):
                    w.wait()
            if c + 1 < nch:
                read(c + 1).start()
            for w in writes(c):
                w.start()
        for w in writes(nch - 1):
            w.wait()

    assert nw == V7X_SC_CORES * V7X_SC_SUBCORES and nw * per_w == ntok and ch == cfg["sc_rows"]
    return pl.kernel(
        body,
        out_type=jax.ShapeDtypeStruct((out_rows, half), U32),
        mesh=_sc_mesh(),
        scratch_types=[
            pltpu.VMEM((topk, nch, ch), jnp.int32),
            pltpu.VMEM((2, ch, half), U32),
            pltpu.SemaphoreType.DMA((2,)),
            pltpu.SemaphoreType.DMA((2,)),
        ],
        name="dispatch",
    )(hp, pos_w)


def _combine_call(ys, pos_w):
    cfg = _tiles()
    _, half = ys.shape
    nw, topk, nch, ch = pos_w.shape
    per_w = nch * ch
    ntok = nw * per_w

    def body(ys_hbm, pos_hbm, *rest):
        outs = rest[:topk]
        idx_v, buf, rsem, wsem = rest[topk:]
        wid = _sc_worker_id()
        pltpu.sync_copy(pos_hbm.at[wid], idx_v)
        items = [(c, k) for c in range(nch) for k in range(topk)]

        def read(i):
            c, k = items[i]
            return pltpu.make_async_copy(ys_hbm.at[idx_v.at[k, c]], buf.at[i % 2], rsem.at[i % 2])

        def write(i):
            c, k = items[i]
            rows = outs[k].at[pl.ds(wid * per_w + c * ch, ch)]
            return pltpu.make_async_copy(buf.at[i % 2], rows, wsem.at[i % 2])

        read(0).start()
        for i in range(len(items)):
            read(i).wait()
            if i >= 1:
                write(i - 1).wait()
            if i + 1 < len(items):
                read(i + 1).start()
            write(i).start()
        write(len(items) - 1).wait()

    assert nw == V7X_SC_CORES * V7X_SC_SUBCORES and ch == cfg["sc_rows"]
    return pl.kernel(
        body,
        out_type=[jax.ShapeDtypeStruct((ntok, half), U32)] * topk,
        mesh=_sc_mesh(),
        scratch_types=[
            pltpu.VMEM((topk, nch, ch), jnp.int32),
            pltpu.VMEM((2, ch, half), U32),
            pltpu.SemaphoreType.DMA((2,)),
            pltpu.SemaphoreType.DMA((2,)),
        ],
        name="combine",
    )(ys, pos_w)


def _ple_kernel(x1_ref, yg0_ref, yg1_ref, gate_ref, p_ref, plen_ref, wg_ref, wu_ref, fin_ref, o_ref):
    rows = x1_ref.shape[0]
    sub_rows = rows // PLE_SUBBLOCKS
    for q in range(PLE_SUBBLOCKS):
        rs = pl.ds(q * sub_rows, sub_rows)
        lo0, hi0 = _unpack_bf16_pair(yg0_ref[rs, :])
        lo1, hi1 = _unpack_bf16_pair(yg1_ref[rs, :])
        g0 = gate_ref[rs, 0:1]
        g1 = gate_ref[rs, 1:2]
        moe = g0 * jnp.concatenate([lo0, hi0], axis=1) + g1 * jnp.concatenate([lo1, hi1], axis=1)
        x2 = x1_ref[rs, :] + moe
        r = _rmsnorm(x2, plen_ref[...]).astype(BF16)
        gt = _sigmoid(_dot(r, wg_ref[...]))
        up = _dot(p_ref[rs, :].astype(BF16), wu_ref[...])
        x3 = x2 + gt * up
        o_ref[rs, :] = _rmsnorm(x3, fin_ref[...])


def _ple_call(x1, yg0, yg1, gates, p, ple_norm, wg, wu, final_norm):
    cfg = _tiles()
    ntok, d = x1.shape
    tp = cfg["ple_rows"]
    pdim = p.shape[1]
    return pl.pallas_call(
        _ple_kernel,
        grid=(ntok // tp,),
        in_specs=[
            pl.BlockSpec((tp, d), lambda i: (i, 0)),
            pl.BlockSpec((tp, d // 2), lambda i: (i, 0)),
            pl.BlockSpec((tp, d // 2), lambda i: (i, 0)),
            pl.BlockSpec((tp, V7X_SUBLANES), lambda i: (i, 0)),
            pl.BlockSpec((tp, pdim), lambda i: (i, 0)),
            _const_spec((1, d)),
            _const_spec(wg.shape),
            _const_spec(wu.shape),
            _const_spec((1, d)),
        ],
        out_specs=pl.BlockSpec((tp, d), lambda i: (i, 0)),
        out_shape=jax.ShapeDtypeStruct((ntok, d), F32),
        compiler_params=pltpu.CompilerParams(
            dimension_semantics=("arbitrary",),
            vmem_limit_bytes=cfg["ple_vmem"]),
        name="ple",
    )(x1, yg0, yg1, gates, p, ple_norm, wg, wu, final_norm)


def _blockdiag_pack(w):
    nb, bd, _ = w.shape
    per = V7X_MXU_DIM // bd
    w4 = w.reshape(nb // per, per, bd, bd)
    eye = jnp.eye(per, dtype=w.dtype)
    out = jnp.einsum("jpab,pq->jpaqb", w4, eye)
    return out.reshape(nb // per, V7X_MXU_DIM, V7X_MXU_DIM).astype(BF16)


def kernel(x, p, mix_norm, w_in, conv_w, conv_b, lru_wa, lru_ba, lru_wi, lru_bi, lru_lambda, sgu_ln_g, sgu_ln_b, sgu_ws, sgu_bs, w_out, ffn_norm, router_group_w, router_group_b, router_expert_w, router_expert_b, expert_w1, expert_w3, expert_w2, ple_norm, ple_gate_w, ple_up_w, final_norm):
    cfg = _tiles()
    bsz, seq, d = x.shape
    ntok = bsz * seq
    tm = cfg["expert_rows"]
    depth = w_in.shape[0]
    assert depth == 1, "the ple kernel applies the final norm, so it must be the last layer"
    l = 0
    w_router = jnp.zeros((d, ROUTER_ROWS), F32)
    w_router = w_router.at[:, :N_GROUPS].set(router_group_w[l])
    w_router = w_router.at[:, EXPERT_ROW0:EXPERT_ROW0 + N_EXPERTS].set(router_expert_w[l])
    b_router = jnp.zeros((ROUTER_ROWS, 1), F32)
    b_router = b_router.at[:N_GROUPS, 0].set(router_group_b[l])
    b_router = b_router.at[EXPERT_ROW0:EXPERT_ROW0 + N_EXPERTS, 0].set(router_expert_b[l])
    col_scale = jnp.concatenate([jnp.ones((4 * d,), F32), jnp.full((2 * d,), 0.5, F32)])
    ts = cfg["mixer_rows"]
    group = ts // V7X_SUBLANES
    bs_tile = jnp.tile(sgu_bs[l], (1, ts // CHUNK)).reshape(SGU_GROUPS, V7X_SUBLANES, group)
    bs_tile = jnp.transpose(bs_tile, (2, 1, 0)).reshape(ts, SGU_GROUPS)
    x1, hp = _mixer_call(
        x, mix_norm[l][None], (w_in[l] * col_scale[None, :]).astype(BF16), conv_w[l], conv_b[l][None],
        _blockdiag_pack(lru_wa[l]), lru_ba[l][None], _blockdiag_pack(lru_wi[l]), lru_bi[l][None],
        lru_lambda[l][None], sgu_ln_g[l][None], sgu_ln_b[l][None], sgu_ws[l], bs_tile,
        (0.25 * w_out[l]).astype(BF16), ffn_norm[l][None])
    hp = hp.reshape(ntok, d // 2)
    pos, gate, cnt = _router_call(hp, w_router.astype(BF16), b_router)

    cap = ntok
    tiles_per_expert = (cnt[:, 0].astype(jnp.int32) + tm - 1) // tm
    nw = V7X_SC_CORES * V7X_SC_SUBCORES
    ch = cfg["sc_rows"]
    pos_w = jnp.transpose(pos[:TOP_K].reshape(TOP_K, nw, ntok // (nw * ch), ch), (1, 0, 2, 3))

    hs = _dispatch_call(hp, pos_w, N_EXPERTS * cap)
    ys = _expert_call(tiles_per_expert, hs, expert_w1[l], expert_w3[l], expert_w2[l], cap)
    yg0, yg1 = _combine_call(ys, pos_w)

    out = _ple_call(x1.reshape(ntok, d), yg0, yg1, gate, p[l].reshape(ntok, -1), ple_norm[l][None],
                    ple_gate_w[l].astype(BF16), ple_up_w[l].astype(BF16), final_norm[None])
    return out.reshape(bsz, seq, d)
```

```python
import functools

import jax
import jax.numpy as jnp
from jax import lax
from jax.experimental import pallas as pl
from jax.experimental.pallas import tpu as pltpu
from jax.experimental.pallas import tpu_sc as plsc

F32 = jnp.float32
BF16 = jnp.bfloat16
U32 = jnp.uint32

LRU_BLOCKS = 16
CONV_WIDTH = 4
LRU_C = 8.0
SGU_GROUPS = 8
CHUNK = 128
N_GROUPS = 4
EXPERTS_PER_GROUP = 8
N_EXPERTS = N_GROUPS * EXPERTS_PER_GROUP
TOP_K = 2
EPS = 1e-6

V7X_MXU_DIM = 256
V7X_SUBLANES = 8
V7X_LANES = 128
V7X_VMEM_BYTES = 64 * 1024 * 1024
V7X_SC_CORES = 2
V7X_SC_SUBCORES = 16

EXPERT_LOOKAHEAD = 3
PLE_SUBBLOCKS = 4
ROUTER_ROWS = V7X_LANES
EXPERT_ROW0 = V7X_SUBLANES


def _tiles():
    return dict(
        mixer_rows=256,
        expert_rows=512,
        ple_rows=1024,
        sc_rows=64,
        router_rows=1024,
        mixer_vmem=52 * 1024 * 1024,
        expert_vmem=40 * 1024 * 1024,
        ple_vmem=48 * 1024 * 1024,
        router_vmem=32 * 1024 * 1024,
    )


def _dot(a, b):
    return jnp.dot(a, b, preferred_element_type=F32)


def _sigmoid(x):
    return 0.5 * jnp.tanh(0.5 * x) + 0.5


def _rmsnorm(x, g):
    ms = jnp.mean(x * x, axis=-1, keepdims=True)
    return x * lax.rsqrt(ms + EPS) * g


def _pack_bf16_pair(lo, hi):
    lo_b = lax.bitcast_convert_type(lo.astype(BF16).astype(F32), U32)
    hi_b = lax.bitcast_convert_type(hi.astype(BF16).astype(F32), U32)
    return (hi_b & jnp.uint32(0xFFFF0000)) | lax.shift_right_logical(lo_b, jnp.uint32(16))


def _unpack_bf16_pair(w):
    lo = lax.bitcast_convert_type(lax.shift_left(w, jnp.uint32(16)), F32)
    hi = lax.bitcast_convert_type(w & jnp.uint32(0xFFFF0000), F32)
    return lo, hi


def _const_spec(shape):
    zeros = (0,) * len(shape)
    return pl.BlockSpec(shape, lambda *_: zeros, pipeline_mode=pl.Buffered(1))


def _tile_copies(hbm, buf, sem, b, row0, slot, to_hbm):
    group = buf.shape[1]
    copies = []
    for r in range(V7X_SUBLANES):
        hbm_rows = hbm.at[b, pl.ds(row0 + group * r, group), :]
        vmem_rows = buf.at[slot, :, r, :]
        src, dst = (vmem_rows, hbm_rows) if to_hbm else (hbm_rows, vmem_rows)
        copies.append(pltpu.make_async_copy(src, dst, sem.at[slot]))
    return copies


def _lru_scan(a, u, h0):
    group = a.shape[0]
    acc_a = [a[0]]
    acc_u = [u[0]]
    for g in range(1, group):
        acc_a.append(a[g] * acc_a[-1])
        acc_u.append(a[g] * acc_u[-1] + u[g])
    end_a, end_u = acc_a[-1], acc_u[-1]
    sub = lax.broadcasted_iota(jnp.int32, end_a.shape, 0)
    shift = 1
    while shift < V7X_SUBLANES:
        keep = sub >= shift
        a_sh = pltpu.roll(end_a, shift, axis=0)
        u_sh = pltpu.roll(end_u, shift, axis=0)
        end_u = jnp.where(keep, end_a * u_sh + end_u, end_u)
        end_a = jnp.where(keep, end_a * a_sh, end_a)
        shift *= 2
    h_end = end_a * h0 + end_u
    h_in = jnp.where(sub == 0, h0, pltpu.roll(h_end, 1, axis=0))
    out = [acc_a[g] * h_in + acc_u[g] for g in range(group)]
    return jnp.stack(out, axis=0), h_end[V7X_SUBLANES - 1:V7X_SUBLANES, :]


def _mixer_kernel(x_hbm, mixn_ref, win_ref, convw_ref, convb_ref, wa_ref, ba_ref, wi_ref, bi_ref,
                  lam_ref, lng_ref, lnb_ref, ws_ref, bsp_ref, wout_ref, ffn_ref,
                  x1_hbm, hp_hbm,
                  xbuf, z0_ref, z1_ref, x1buf, hpbuf, xsem, x1sem, hpsem, wsm_ref, ztail_ref, hcar_ref,
                  *, nseq):
    j = pl.program_id(0)
    ntile = pl.num_programs(0) - 1
    _, group, _, d = xbuf.shape
    rows = group * V7X_SUBLANES
    half = d // 2
    ta = jnp.minimum(j, ntile - 1)
    tb = jnp.maximum(j - 1, 0)
    s = lax.rem(tb, nseq)
    slot = lax.rem(tb, 2)

    def fetch(t):
        return _tile_copies(x_hbm, xbuf, xsem, lax.div(t, nseq), lax.rem(t, nseq) * rows,
                            lax.rem(t, 3), to_hbm=False)

    def put(t):
        tb_, ts_, sl = lax.div(t, nseq), lax.rem(t, nseq) * rows, lax.rem(t, 2)
        return (_tile_copies(x1_hbm, x1buf, x1sem, tb_, ts_, sl, to_hbm=True)
                + _tile_copies(hp_hbm, hpbuf, hpsem, tb_, ts_, sl, to_hbm=True))

    @pl.when(j == 0)
    def _():
        for c in fetch(0):
            c.start()
        z1_ref[...] = jnp.zeros_like(z1_ref)
        i_idx = lax.broadcasted_iota(jnp.int32, (rows, rows), 0)
        j_idx = lax.broadcasted_iota(jnp.int32, (rows, rows), 1)
        t_i = group * lax.rem(i_idx, V7X_SUBLANES) + lax.div(i_idx, V7X_SUBLANES)
        t_j = group * lax.rem(j_idx, V7X_SUBLANES) + lax.div(j_idx, V7X_SUBLANES)
        keep = (t_i >= t_j) & (lax.div(t_i, CHUNK) == lax.div(t_j, CHUNK))
        pick_rows = jnp.where(t_i == j_idx, 1.0, 0.0).astype(BF16)
        pick_cols = jnp.where(i_idx == t_j, 1.0, 0.0).astype(BF16)
        reps = rows // CHUNK
        for g in range(SGU_GROUPS):
            w_chunk = ws_ref[g].astype(BF16)
            w_rows = jnp.concatenate([w_chunk] * reps, axis=1)
            w_full = jnp.concatenate([w_rows] * reps, axis=0)
            w_perm = _dot(_dot(pick_rows, w_full).astype(BF16), pick_cols)
            wsm_ref[g] = jnp.where(keep, w_perm, 0.0).astype(BF16)

    @pl.when(j + 1 < ntile)
    def _():
        for c in fetch(j + 1):
            c.start()

    @pl.when(j < ntile)
    def _():
        for c in fetch(j):
            c.wait()

    @pl.when(s == 0)
    def _():
        ztail_ref[...] = jnp.zeros_like(ztail_ref)
        hcar_ref[...] = jnp.zeros_like(hcar_ref)

    def compute(z_w, z_r):
        xa_in = xbuf[lax.rem(ta, 3)].reshape(rows, d)
        h_next = _rmsnorm(xa_in, mixn_ref[...]).astype(BF16)
        pw = d // 2

        def project(k):
            z_w[:, k * pw:(k + 1) * pw] = _dot(h_next, win_ref[:, k * pw:(k + 1) * pw])

        x = xbuf[lax.rem(tb, 3)].reshape(rows, d)

        def sec(k, c0, c1):
            return z_r[:, k * d + c0:k * d + c1]

        def one_plus_tanh_gelu(v):
            c = 0.7978845608028654
            return 1.0 + jnp.tanh(v * (c + (c * 0.044715) * (v * v)))

        cw = 0.5 * convw_ref[...]
        cb_h = 0.5 * convb_ref[...]
        ba_h = 0.5 * ba_ref[...]
        bi_h = 0.5 * bi_ref[...]
        neg_lam = -lam_ref[...]
        softplus = jnp.maximum(neg_lam, 0.0) + jnp.log1p(jnp.exp(-jnp.abs(neg_lam)))
        c_a = (-0.5 * LRU_C) * softplus
        blk = V7X_MXU_DIM
        sub3 = lax.broadcasted_iota(jnp.int32, (CONV_WIDTH - 1, V7X_SUBLANES, blk), 1)
        term_a = []
        for n in range(d // blk):
            project(n)
            c0, c1 = n * blk, (n + 1) * blk
            z3 = sec(0, c0, c1).reshape(group, V7X_SUBLANES, blk)
            tail = z3[group - (CONV_WIDTH - 1):]
            halo = jnp.where(sub3 == 0, pltpu.roll(ztail_ref[:, :, c0:c1], 1, axis=1),
                             pltpu.roll(tail, 1, axis=1))
            ztail_ref[:, :, c0:c1] = tail
            zext = jnp.concatenate([halo, z3], axis=0)
            xa_h = cb_h[:, c0:c1] + cw[CONV_WIDTH - 1:CONV_WIDTH, c0:c1] * z3
            for k in range(1, CONV_WIDTH):
                lo = CONV_WIDTH - 1 - k
                xa_h = xa_h + cw[lo:lo + 1, c0:c1] * zext[lo:lo + group]
            xa2 = xa_h.reshape(rows, blk)
            xa_bf = xa2.astype(BF16)
            th_r = jnp.tanh(_dot(xa_bf, wa_ref[n]) + ba_h[:, c0:c1])
            th_i = jnp.tanh(_dot(xa_bf, wi_ref[n]) + bi_h[:, c0:c1])
            a = jnp.exp(c_a[:, c0:c1] + c_a[:, c0:c1] * th_r)
            u = jnp.sqrt(1.0 - a * a) * ((1.0 + th_i) * xa2)
            hseq, hlast = _lru_scan(a.reshape(group, V7X_SUBLANES, blk),
                                    u.reshape(group, V7X_SUBLANES, blk), hcar_ref[:, c0:c1])
            hcar_ref[:, c0:c1] = hlast
            zg = sec(1, c0, c1)
            term_a.append(((1.0 + jnp.tanh(sec(4, c0, c1))) * one_plus_tanh_gelu(zg))
                          * (zg * hseq.reshape(rows, blk)))

        project(4)
        zv = sec(3, 0, d)
        gv2 = zv * one_plus_tanh_gelu(zv)
        project(5)
        mu = jnp.mean(gv2, axis=-1, keepdims=True)
        xc = gv2 - mu
        var = jnp.mean(xc * xc, axis=-1, keepdims=True)
        v_bf = (xc * lax.rsqrt(var + 4.0 * EPS) * lng_ref[...] + lnb_ref[...]).astype(BF16)
        project(6)
        gdim = d // SGU_GROUPS
        term_b = []
        for g in range(SGU_GROUPS):
            c0, c1 = g * gdim, (g + 1) * gdim
            if g in (1, 3, 5, 6, 7):
                project({1: 7, 3: 8, 5: 9, 6: 10, 7: 11}[g])
            sp = _dot(wsm_ref[g], v_bf[:, c0:c1]) + bsp_ref[:, g:g + 1]
            zu = sec(2, c0, c1)
            term_b.append(((1.0 + jnp.tanh(sec(5, c0, c1))) * one_plus_tanh_gelu(zu)) * (zu * sp))
        merged4 = jnp.concatenate(term_a, axis=1) + jnp.concatenate(term_b, axis=1)

        x1 = x + _dot(merged4.astype(BF16), wout_ref[...])

        hn = _rmsnorm(x1, ffn_ref[...])
        hp = _pack_bf16_pair(hn[:, :half], hn[:, half:])

        @pl.when(j >= 3)
        def _():
            for c in put(tb - 2):
                c.wait()

        x1buf[slot] = x1.reshape(group, V7X_SUBLANES, d)
        hpbuf[slot] = hp.reshape(group, V7X_SUBLANES, half)

        @pl.when(j >= 1)
        def _():
            for c in put(tb):
                c.start()

    @pl.when(lax.rem(j, 2) == 0)
    def _():
        compute(z0_ref, z1_ref)

    @pl.when(lax.rem(j, 2) == 1)
    def _():
        compute(z1_ref, z0_ref)

    @pl.when(j == ntile)
    def _():
        for c in put(tb):
            c.wait()

        @pl.when(ntile >= 2)
        def _():
            for c in put(tb - 1):
                c.wait()


def _mixer_call(x, mix_norm, w_in, conv_w, conv_b, wa_blk, ba, wi_blk, bi, lam, ln_g, ln_b, ws,
                bs_tile, w_out, ffn_norm):
    cfg = _tiles()
    bsz, seq, d = x.shape
    ts = cfg["mixer_rows"]
    group = ts // V7X_SUBLANES
    nseq = seq // ts
    ntile = bsz * nseq
    row1 = (1, d)
    in_specs = [
        pl.BlockSpec(memory_space=pl.ANY),
        _const_spec(row1),
        _const_spec(w_in.shape),
        _const_spec(conv_w.shape), _const_spec(row1),
        _const_spec(wa_blk.shape), _const_spec(row1),
        _const_spec(wi_blk.shape), _const_spec(row1),
        _const_spec(row1),
        _const_spec(row1), _const_spec(row1),
        _const_spec(ws.shape), _const_spec(bs_tile.shape),
        _const_spec(w_out.shape), _const_spec(row1),
    ]
    out_shape = [
        jax.ShapeDtypeStruct((bsz, seq, d), F32),
        jax.ShapeDtypeStruct((bsz, seq, d // 2), U32),
    ]
    out_specs = [
        pl.BlockSpec(memory_space=pl.ANY),
        pl.BlockSpec(memory_space=pl.ANY),
    ]
    scratch = [
        pltpu.VMEM((3, group, V7X_SUBLANES, d), F32),
        pltpu.VMEM((ts, w_in.shape[1]), F32),
        pltpu.VMEM((ts, w_in.shape[1]), F32),
        pltpu.VMEM((2, group, V7X_SUBLANES, d), F32),
        pltpu.VMEM((2, group, V7X_SUBLANES, d // 2), U32),
        pltpu.SemaphoreType.DMA((3,)),
        pltpu.SemaphoreType.DMA((2,)),
        pltpu.SemaphoreType.DMA((2,)),
        pltpu.VMEM((SGU_GROUPS, ts, ts), BF16),
        pltpu.VMEM((CONV_WIDTH - 1, V7X_SUBLANES, d), F32),
        pltpu.VMEM((1, d), F32),
    ]
    return pl.pallas_call(
        functools.partial(_mixer_kernel, nseq=nseq),
        grid=(ntile + 1,),
        in_specs=in_specs,
        out_specs=out_specs,
        out_shape=out_shape,
        scratch_shapes=scratch,
        compiler_params=pltpu.CompilerParams(
            dimension_semantics=("arbitrary",),
            vmem_limit_bytes=cfg["mixer_vmem"]),
        name="mixer",
    )(x, mix_norm, w_in, conv_w, conv_b, wa_blk, ba, wi_blk, bi, lam, ln_g, ln_b, ws, bs_tile,
      w_out, ffn_norm)


def _router_kernel(hp_ref, wr_ref, br_ref, pos_ref, gate_ref, cnt_ref, ccar_ref, *, expert_capacity):
    rows = hp_ref.shape[0]

    @pl.when(pl.program_id(0) == 0)
    def _():
        ccar_ref[...] = jnp.zeros_like(ccar_ref)

    lo, hi = _unpack_bf16_pair(hp_ref[...])
    hn = jnp.concatenate([lo, hi], axis=1)
    logits = _dot(hn.astype(BF16), wr_ref[...])
    lt = jnp.transpose(logits) + br_ref[...]
    sub = lax.broadcasted_iota(jnp.int32, (V7X_SUBLANES, rows), 0)
    subf = sub.astype(F32)
    big = jnp.float32(1e9)

    lg = jnp.where(sub < N_GROUPS, lt[0:V7X_SUBLANES, :], -jnp.inf)
    g_exp = jnp.exp(lg - jnp.max(lg, axis=0, keepdims=True))
    g_prob = g_exp / jnp.sum(g_exp, axis=0, keepdims=True)
    g_top = jnp.max(g_prob, axis=0, keepdims=True)
    g_idx = jnp.min(jnp.where(g_prob == g_top, subf, big), axis=0, keepdims=True)

    e_sel = jnp.zeros((EXPERTS_PER_GROUP, rows), F32)
    for g in range(N_GROUPS):
        r0 = EXPERT_ROW0 + g * EXPERTS_PER_GROUP
        e_sel = jnp.where(g_idx == g, lt[r0:r0 + EXPERTS_PER_GROUP, :], e_sel)
    e_exp = jnp.exp(e_sel - jnp.max(e_sel, axis=0, keepdims=True))
    e_prob = e_exp / jnp.sum(e_exp, axis=0, keepdims=True)
    p1 = jnp.max(e_prob, axis=0, keepdims=True)
    i1 = jnp.min(jnp.where(e_prob == p1, subf, big), axis=0, keepdims=True)
    rest = jnp.where(subf == i1, -1.0, e_prob)
    p2 = jnp.max(rest, axis=0, keepdims=True)
    i2 = jnp.min(jnp.where(rest == p2, subf, big), axis=0, keepdims=True)
    psum = p1 + p2
    gate1 = g_top * (p1 / psum)
    gate2 = g_top * (p2 / psum)
    gid1 = g_idx * EXPERTS_PER_GROUP + i1
    gid2 = g_idx * EXPERTS_PER_GROUP + i2

    eid = lax.broadcasted_iota(jnp.int32, (N_EXPERTS, rows), 0).astype(F32)
    hit1 = eid == gid1
    hit2 = eid == gid2
    cnt = jnp.where(hit1 | hit2, 1.0, 0.0)
    sb = V7X_MXU_DIM
    before = (lax.broadcasted_iota(jnp.int32, (sb, sb), 0)
              < lax.broadcasted_iota(jnp.int32, (sb, sb), 1))
    before = jnp.where(before, 1.0, 0.0).astype(BF16)
    running = ccar_ref[:, 0:1]
    base = []
    for q in range(rows // sb):
        part = cnt[:, q * sb:(q + 1) * sb]
        base.append(running + _dot(part.astype(BF16), before))
        running = running + jnp.sum(part, axis=1, keepdims=True)
    base = jnp.concatenate(base, axis=1)
    rank1 = jnp.sum(jnp.where(hit1, base, 0.0), axis=0, keepdims=True)
    rank2 = jnp.sum(jnp.where(hit2, base, 0.0), axis=0, keepdims=True)
    total = jnp.broadcast_to(running, ccar_ref.shape)
    ccar_ref[...] = total
    cnt_ref[...] = total
    cap = float(expert_capacity)
    zero = jnp.zeros((V7X_SUBLANES - TOP_K, rows), F32)
    pos = jnp.concatenate([gid1 * cap + rank1, gid2 * cap + rank2, zero], axis=0)
    pos_ref[...] = pos.astype(jnp.int32)
    gate_ref[...] = jnp.transpose(jnp.concatenate([gate1, gate2, zero], axis=0))


def _router_call(hp, w_router, b_router):
    cfg = _tiles()
    ntok, half = hp.shape
    tr = cfg["router_rows"]
    return pl.pallas_call(
        functools.partial(_router_kernel, expert_capacity=ntok),
        grid=(ntok // tr,),
        in_specs=[
            pl.BlockSpec((tr, half), lambda i: (i, 0)),
            _const_spec(w_router.shape),
            _const_spec(b_router.shape),
        ],
        out_specs=[
            pl.BlockSpec((V7X_SUBLANES, tr), lambda i: (0, i)),
            pl.BlockSpec((tr, V7X_SUBLANES), lambda i: (i, 0)),
            pl.BlockSpec((N_EXPERTS, V7X_LANES), lambda i: (0, 0)),
        ],
        out_shape=[
            jax.ShapeDtypeStruct((V7X_SUBLANES, ntok), jnp.int32),
            jax.ShapeDtypeStruct((ntok, V7X_SUBLANES), F32),
            jax.ShapeDtypeStruct((N_EXPERTS, V7X_LANES), F32),
        ],
        scratch_shapes=[pltpu.VMEM((N_EXPERTS, V7X_LANES), F32)],
        compiler_params=pltpu.CompilerParams(
            dimension_semantics=("arbitrary",),
            vmem_limit_bytes=cfg["router_vmem"]),
        name="router",
    )(hp, w_router, b_router)


def _expert_kernel(nt_ref, base_ref, texp_ref, tloc_ref, hs_hbm, w1_ref, w3_ref, w2_ref, ys_hbm,
                   hbuf, ybuf, hsem, ysem, w1b_ref, w3b_ref, w2b_ref, *, capacity):
    e = pl.program_id(0)
    n_exp = pl.num_programs(0)
    nt = nt_ref[e]
    base = base_ref[e]
    total = base_ref[n_exp - 1] + nt_ref[n_exp - 1]
    n_in, tm, _ = hbuf.shape
    n_out = ybuf.shape[0]
    ahead = n_in - 1

    def load(g):
        slot = lax.rem(g, n_in)
        rows = pl.ds(texp_ref[g] * capacity + tloc_ref[g] * tm, tm)
        return pltpu.make_async_copy(hs_hbm.at[rows], hbuf.at[slot], hsem.at[slot])

    def store(t, slot):
        rows = pl.ds(e * capacity + t * tm, tm)
        return pltpu.make_async_copy(ybuf.at[slot], ys_hbm.at[rows], ysem.at[slot])

    @pl.when(e == 0)
    def _():
        for g0 in range(ahead):
            @pl.when(g0 < total)
            def _():
                load(g0).start()

    w1b_ref[...] = w1_ref[...].astype(BF16)
    w3b_ref[...] = w3_ref[...].astype(BF16)
    w2b_ref[...] = w2_ref[...].astype(BF16)

    @pl.loop(0, nt)
    def _(t):
        g = base + t

        @pl.when(g + ahead < total)
        def _():
            load(g + ahead).start()

        load(g).wait()
        lo, hi = _unpack_bf16_pair(hbuf[lax.rem(g, n_in)])
        h = jnp.concatenate([lo, hi], axis=1).astype(BF16)
        a = _dot(h, w1b_ref[...])
        b = _dot(h, w3b_ref[...])
        hid = (a * _sigmoid(a)) * b
        y = _dot(hid.astype(BF16), w2b_ref[...])
        half = y.shape[1] // 2
        slot = lax.rem(g, n_out)

        @pl.when(g >= n_out)
        def _():
            store(t, slot).wait()

        ybuf[slot] = _pack_bf16_pair(y[:, :half], y[:, half:])
        store(t, slot).start()

    @pl.when(e + 1 == n_exp)
    def _():
        for back in range(1, n_out + 1):
            @pl.when(total >= back)
            def _():
                store(0, lax.rem(total - back, n_out)).wait()


def _expert_call(tiles_per_expert, hs, w1, w3, w2, capacity):
    cfg = _tiles()
    tm = cfg["expert_rows"]
    prow, half = hs.shape
    n_exp, d, f = w1.shape
    ends = jnp.cumsum(tiles_per_expert)
    base = ends - tiles_per_expert
    g = jnp.arange(capacity * TOP_K // tm + n_exp, dtype=jnp.int32)
    texp = jnp.minimum(jnp.sum((ends[None, :] <= g[:, None]).astype(jnp.int32), axis=1), n_exp - 1)
    onehot = texp[:, None] == jnp.arange(n_exp, dtype=jnp.int32)[None, :]
    tloc = g - jnp.sum(jnp.where(onehot, base[None, :], 0), axis=1)

    def w_map(e, *_):
        return (e, 0, 0)

    grid_spec = pltpu.PrefetchScalarGridSpec(
        num_scalar_prefetch=4,
        grid=(n_exp,),
        in_specs=[
            pl.BlockSpec(memory_space=pl.ANY),
            pl.BlockSpec((None, d, f), w_map),
            pl.BlockSpec((None, d, f), w_map),
            pl.BlockSpec((None, f, d), w_map),
        ],
        out_specs=pl.BlockSpec(memory_space=pl.ANY),
        scratch_shapes=[
            pltpu.VMEM((EXPERT_LOOKAHEAD + 1, tm, half), U32),
            pltpu.VMEM((2, tm, half), U32),
            pltpu.SemaphoreType.DMA((EXPERT_LOOKAHEAD + 1,)),
            pltpu.SemaphoreType.DMA((2,)),
            pltpu.VMEM((d, f), BF16),
            pltpu.VMEM((d, f), BF16),
            pltpu.VMEM((f, d), BF16),
        ],
    )
    return pl.pallas_call(
        functools.partial(_expert_kernel, capacity=capacity),
        grid_spec=grid_spec,
        out_shape=jax.ShapeDtypeStruct((prow, half), U32),
        compiler_params=pltpu.CompilerParams(
            dimension_semantics=("arbitrary",),
            vmem_limit_bytes=cfg["expert_vmem"]),
        name="experts",
    )(tiles_per_expert, base, texp, tloc, hs, w1, w3, w2)


def _sc_mesh():
    return plsc.VectorSubcoreMesh(core_axis_name="c", subcore_axis_name="s",
                                  num_cores=V7X_SC_CORES, num_subcores=V7X_SC_SUBCORES)


def _sc_worker_id():
    return lax.axis_index("s") * V7X_SC_CORES + lax.axis_index("c")


def _dispatch_call(hp, pos_w, out_rows):
    cfg = _tiles()
    ntok, half = hp.shape
    nw, topk, nch, ch = pos_w.shape
    per_w = nch * ch

    def body(hp_hbm, pos_hbm, hs_hbm, idx_v, buf, rsem, wsem):
        wid = _sc_worker_id()
        pltpu.sync_copy(pos_hbm.at[wid], idx_v)

        def read(c):
            rows = hp_hbm.at[pl.ds(wid * per_w + c * ch, ch)]
            return pltpu.make_async_copy(rows, buf.at[c % 2], rsem.at[c % 2])

        def writes(c):
            return [pltpu.make_async_copy(buf.at[c % 2], hs_hbm.at[idx_v.at[k, c]], wsem.at[c % 2])
                    for k in range(topk)]

        read(0).start()
        for c in range(nch):
            read(c).wait()
            if c >= 1:
                for w in writes(c - 1):
                    w.wait()
            if c + 1 < nch:
                read(c + 1).start()
            for w in writes(c):
                w.start()
        for w in writes(nch - 1):
            w.wait()

    assert nw == V7X_SC_CORES * V7X_SC_SUBCORES and nw * per_w == ntok and ch == cfg["sc_rows"]
    return pl.kernel(
        body,
        out_type=jax.ShapeDtypeStruct((out_rows, half), U32),
        mesh=_sc_mesh(),
        scratch_types=[
            pltpu.VMEM((topk, nch, ch), jnp.int32),
            pltpu.VMEM((2, ch, half), U32),
            pltpu.SemaphoreType.DMA((2,)),
            pltpu.SemaphoreType.DMA((2,)),
        ],
        name="dispatch",
    )(hp, pos_w)


def _combine_call(ys, pos_w):
    cfg = _tiles()
    _, half = ys.shape
    nw, topk, nch, ch = pos_w.shape
    per_w = nch * ch
    ntok = nw * per_w

    def body(ys_hbm, pos_hbm, *rest):
        outs = rest[:topk]
        idx_v, buf, rsem, wsem = rest[topk:]
        wid = _sc_worker_id()
        pltpu.sync_copy(pos_hbm.at[wid], idx_v)
        items = [(c, k) for c in range(nch) for k in range(topk)]

        def read(i):
            c, k = items[i]
            return pltpu.make_async_copy(ys_hbm.at[idx_v.at[k, c]], buf.at[i % 2], rsem.at[i % 2])

        def write(i):
            c, k = items[i]
            rows = outs[k].at[pl.ds(wid * per_w + c * ch, ch)]
            return pltpu.make_async_copy(buf.at[i % 2], rows, wsem.at[i % 2])

        read(0).start()
        for i in range(len(items)):
            read(i).wait()
            if i >= 1:
                write(i - 1).wait()
            if i + 1 < len(items):
                read(i + 1).start()
            write(i).start()
        write(len(items) - 1).wait()

    assert nw == V7X_SC_CORES * V7X_SC_SUBCORES and ch == cfg["sc_rows"]
    return pl.kernel(
        body,
        out_type=[jax.ShapeDtypeStruct((ntok, half), U32)] * topk,
        mesh=_sc_mesh(),
        scratch_types=[
            pltpu.VMEM((topk, nch, ch), jnp.int32),
            pltpu.VMEM((2, ch, half), U32),
            pltpu.SemaphoreType.DMA((2,)),
            pltpu.SemaphoreType.DMA((2,)),
        ],
        name="combine",
    )(ys, pos_w)


def _ple_kernel(x1_ref, yg0_ref, yg1_ref, gate_ref, p_ref, plen_ref, wg_ref, wu_ref, fin_ref, o_ref):
    rows = x1_ref.shape[0]
    sub_rows = rows // PLE_SUBBLOCKS
    for q in range(PLE_SUBBLOCKS):
        rs = pl.ds(q * sub_rows, sub_rows)
        lo0, hi0 = _unpack_bf16_pair(yg0_ref[rs, :])
        lo1, hi1 = _unpack_bf16_pair(yg1_ref[rs, :])
        g0 = gate_ref[rs, 0:1]
        g1 = gate_ref[rs, 1:2]
        moe = g0 * jnp.concatenate([lo0, hi0], axis=1) + g1 * jnp.concatenate([lo1, hi1], axis=1)
        x2 = x1_ref[rs, :] + moe
        r = _rmsnorm(x2, plen_ref[...]).astype(BF16)
        gt = _sigmoid(_dot(r, wg_ref[...]))
        up = _dot(p_ref[rs, :].astype(BF16), wu_ref[...])
        x3 = x2 + gt * up
        o_ref[rs, :] = _rmsnorm(x3, fin_ref[...])


def _ple_call(x1, yg0, yg1, gates, p, ple_norm, wg, wu, final_norm):
    cfg = _tiles()
    ntok, d = x1.shape
    tp = cfg["ple_rows"]
    pdim = p.shape[1]
    return pl.pallas_call(
        _ple_kernel,
        grid=(ntok // tp,),
        in_specs=[
            pl.BlockSpec((tp, d), lambda i: (i, 0)),
            pl.BlockSpec((tp, d // 2), lambda i: (i, 0)),
            pl.BlockSpec((tp, d // 2), lambda i: (i, 0)),
            pl.BlockSpec((tp, V7X_SUBLANES), lambda i: (i, 0)),
            pl.BlockSpec((tp, pdim), lambda i: (i, 0)),
            _const_spec((1, d)),
            _const_spec(wg.shape),
            _const_spec(wu.shape),
            _const_spec((1, d)),
        ],
        out_specs=pl.BlockSpec((tp, d), lambda i: (i, 0)),
        out_shape=jax.ShapeDtypeStruct((ntok, d), F32),
        compiler_params=pltpu.CompilerParams(
            dimension_semantics=("arbitrary",),
            vmem_limit_bytes=cfg["ple_vmem"]),
        name="ple",
    )(x1, yg0, yg1, gates, p, ple_norm, wg, wu, final_norm)


def _blockdiag_pack(w):
    nb, bd, _ = w.shape
    per = V7X_MXU_DIM // bd
    w4 = w.reshape(nb // per, per, bd, bd)
    eye = jnp.eye(per, dtype=w.dtype)
    out = jnp.einsum("jpab,pq->jpaqb", w4, eye)
    return out.reshape(nb // per, V7X_MXU_DIM, V7X_MXU_DIM).astype(BF16)


def kernel(x, p, mix_norm, w_in, conv_w, conv_b, lru_wa, lru_ba, lru_wi, lru_bi, lru_lambda, sgu_ln_g, sgu_ln_b, sgu_ws, sgu_bs, w_out, ffn_norm, router_group_w, router_group_b, router_expert_w, router_expert_b, expert_w1, expert_w3, expert_w2, ple_norm, ple_gate_w, ple_up_w, final_norm):
    cfg = _tiles()
    bsz, seq, d = x.shape
    ntok = bsz * seq
    tm = cfg["expert_rows"]
    depth = w_in.shape[0]
    assert depth == 1, "the ple kernel applies the final norm, so it must be the last layer"
    l = 0
    w_router = jnp.zeros((d, ROUTER_ROWS), F32)
    w_router = w_router.at[:, :N_GROUPS].set(router_group_w[l])
    w_router = w_router.at[:, EXPERT_ROW0:EXPERT_ROW0 + N_EXPERTS].set(router_expert_w[l])
    b_router = jnp.zeros((ROUTER_ROWS, 1), F32)
    b_router = b_router.at[:N_GROUPS, 0].set(router_group_b[l])
    b_router = b_router.at[EXPERT_ROW0:EXPERT_ROW0 + N_EXPERTS, 0].set(router_expert_b[l])
    col_scale = jnp.concatenate([jnp.ones((4 * d,), F32), jnp.full((2 * d,), 0.5, F32)])
    ts = cfg["mixer_rows"]
    group = ts // V7X_SUBLANES
    bs_tile = jnp.tile(sgu_bs[l], (1, ts // CHUNK)).reshape(SGU_GROUPS, V7X_SUBLANES, group)
    bs_tile = jnp.transpose(bs_tile, (2, 1, 0)).reshape(ts, SGU_GROUPS)
    x1, hp = _mixer_call(
        x, mix_norm[l][None], (w_in[l] * col_scale[None, :]).astype(BF16), conv_w[l], conv_b[l][None],
        _blockdiag_pack(lru_wa[l]), lru_ba[l][None], _blockdiag_pack(lru_wi[l]), lru_bi[l][None],
        lru_lambda[l][None], sgu_ln_g[l][None], sgu_ln_b[l][None], sgu_ws[l], bs_tile,
        (0.25 * w_out[l]).astype(BF16), ffn_norm[l][None])
    hp = hp.reshape(ntok, d // 2)
    pos, gate, cnt = _router_call(hp, w_router.astype(BF16), b_router)

    cap = ntok
    tiles_per_expert = (cnt[:, 0].astype(jnp.int32) + tm - 1) // tm
    nw = V7X_SC_CORES * V7X_SC_SUBCORES
    ch = cfg["sc_rows"]
    pos_w = jnp.transpose(pos[:TOP_K].reshape(TOP_K, nw, ntok // (nw * ch), ch), (1, 0, 2, 3))

    hs = _dispatch_call(hp, pos_w, N_EXPERTS * cap)
    ys = _expert_call(tiles_per_expert, hs, expert_w1[l], expert_w3[l], expert_w2[l], cap)
    yg0, yg1 = _combine_call(ys, pos_w)

    out = _ple_call(x1.reshape(ntok, d), yg0, yg1, gate, p[l].reshape(ntok, -1), ple_norm[l][None],
                    ple_gate_w[l].astype(BF16), ple_up_w[l].astype(BF16), final_norm[None])
    return out.reshape(bsz, seq, d)
```

```python
import functools

import jax
import jax.numpy as jnp
from jax import lax
from jax.experimental import pallas as pl
from jax.experimental.pallas import tpu as pltpu
from jax.experimental.pallas import tpu_sc as plsc

F32 = jnp.float32
BF16 = jnp.bfloat16
U32 = jnp.uint32

LRU_BLOCKS = 16
CONV_WIDTH = 4
LRU_C = 8.0
SGU_GROUPS = 8
CHUNK = 128
N_GROUPS = 4
EXPERTS_PER_GROUP = 8
N_EXPERTS = N_GROUPS * EXPERTS_PER_GROUP
TOP_K = 2
EPS = 1e-6

V7X_MXU_DIM = 256
V7X_SUBLANES = 8
V7X_LANES = 128
V7X_VMEM_BYTES = 64 * 1024 * 1024
V7X_SC_CORES = 2
V7X_SC_SUBCORES = 16

EXPERT_LOOKAHEAD = 3
PLE_SUBBLOCKS = 4
ROUTER_ROWS = V7X_LANES
EXPERT_ROW0 = V7X_SUBLANES


def _tiles():
    return dict(
        mixer_rows=256,
        expert_rows=512,
        ple_rows=1024,
        ple_parts=2,
        sc_rows=64,
        router_rows=1024,
        mixer_vmem=52 * 1024 * 1024,
        expert_vmem=40 * 1024 * 1024,
        ple_vmem=48 * 1024 * 1024,
        router_vmem=32 * 1024 * 1024,
    )


def _dot(a, b):
    return jnp.dot(a, b, preferred_element_type=F32)


def _sigmoid(x):
    return 0.5 * jnp.tanh(0.5 * x) + 0.5


def _rmsnorm(x, g):
    ms = jnp.mean(x * x, axis=-1, keepdims=True)
    return x * lax.rsqrt(ms + EPS) * g


def _pack_bf16_pair(lo, hi):
    lo_b = lax.bitcast_convert_type(lo.astype(BF16).astype(F32), U32)
    hi_b = lax.bitcast_convert_type(hi.astype(BF16).astype(F32), U32)
    return (hi_b & jnp.uint32(0xFFFF0000)) | lax.shift_right_logical(lo_b, jnp.uint32(16))


def _unpack_bf16_pair(w):
    lo = lax.bitcast_convert_type(lax.shift_left(w, jnp.uint32(16)), F32)
    hi = lax.bitcast_convert_type(w & jnp.uint32(0xFFFF0000), F32)
    return lo, hi


def _const_spec(shape):
    zeros = (0,) * len(shape)
    return pl.BlockSpec(shape, lambda *_: zeros, pipeline_mode=pl.Buffered(1))


def _tile_copies(hbm, buf, sem, b, row0, slot, to_hbm):
    group = buf.shape[1]
    copies = []
    for r in range(V7X_SUBLANES):
        hbm_rows = hbm.at[b, pl.ds(row0 + group * r, group), :]
        vmem_rows = buf.at[slot, :, r, :]
        src, dst = (vmem_rows, hbm_rows) if to_hbm else (hbm_rows, vmem_rows)
        copies.append(pltpu.make_async_copy(src, dst, sem.at[slot]))
    return copies


def _lru_scan(a, u, h0):
    group = a.shape[0]
    acc_a = [a[0]]
    acc_u = [u[0]]
    for g in range(1, group):
        acc_a.append(a[g] * acc_a[-1])
        acc_u.append(a[g] * acc_u[-1] + u[g])
    end_a, end_u = acc_a[-1], acc_u[-1]
    sub = lax.broadcasted_iota(jnp.int32, end_a.shape, 0)
    shift = 1
    while shift < V7X_SUBLANES:
        keep = sub >= shift
        a_sh = pltpu.roll(end_a, shift, axis=0)
        u_sh = pltpu.roll(end_u, shift, axis=0)
        end_u = jnp.where(keep, end_a * u_sh + end_u, end_u)
        end_a = jnp.where(keep, end_a * a_sh, end_a)
        shift *= 2
    h_end = end_a * h0 + end_u
    h_in = jnp.where(sub == 0, h0, pltpu.roll(h_end, 1, axis=0))
    out = [acc_a[g] * h_in + acc_u[g] for g in range(group)]
    return jnp.stack(out, axis=0), h_end[V7X_SUBLANES - 1:V7X_SUBLANES, :]


def _mixer_kernel(x_hbm, mixn_ref, win_ref, convw_ref, convb_ref, wa_ref, ba_ref, wi_ref, bi_ref,
                  lam_ref, lng_ref, lnb_ref, ws_ref, bsp_ref, wout_ref, ffn_ref,
                  x1_hbm, hp_hbm,
                  xbuf, z0_ref, z1_ref, x1buf, hpbuf, xsem, x1sem, hpsem, wsm_ref, ztail_ref, hcar_ref,
                  *, nseq):
    j = pl.program_id(0)
    ntile = pl.num_programs(0) - 1
    _, group, _, d = xbuf.shape
    rows = group * V7X_SUBLANES
    half = d // 2
    ta = jnp.minimum(j, ntile - 1)
    tb = jnp.maximum(j - 1, 0)
    s = lax.rem(tb, nseq)
    slot = lax.rem(tb, 2)

    def fetch(t):
        return _tile_copies(x_hbm, xbuf, xsem, lax.div(t, nseq), lax.rem(t, nseq) * rows,
                            lax.rem(t, 3), to_hbm=False)

    def put(t):
        tb_, ts_, sl = lax.div(t, nseq), lax.rem(t, nseq) * rows, lax.rem(t, 2)
        return (_tile_copies(x1_hbm, x1buf, x1sem, tb_, ts_, sl, to_hbm=True)
                + _tile_copies(hp_hbm, hpbuf, hpsem, tb_, ts_, sl, to_hbm=True))

    @pl.when(j == 0)
    def _():
        for c in fetch(0):
            c.start()
        z1_ref[...] = jnp.zeros_like(z1_ref)
        i_idx = lax.broadcasted_iota(jnp.int32, (rows, rows), 0)
        j_idx = lax.broadcasted_iota(jnp.int32, (rows, rows), 1)
        t_i = group * lax.rem(i_idx, V7X_SUBLANES) + lax.div(i_idx, V7X_SUBLANES)
        t_j = group * lax.rem(j_idx, V7X_SUBLANES) + lax.div(j_idx, V7X_SUBLANES)
        keep = (t_i >= t_j) & (lax.div(t_i, CHUNK) == lax.div(t_j, CHUNK))
        pick_rows = jnp.where(t_i == j_idx, 1.0, 0.0).astype(BF16)
        pick_cols = jnp.where(i_idx == t_j, 1.0, 0.0).astype(BF16)
        reps = rows // CHUNK
        for g in range(SGU_GROUPS):
            w_chunk = ws_ref[g].astype(BF16)
            w_rows = jnp.concatenate([w_chunk] * reps, axis=1)
            w_full = jnp.concatenate([w_rows] * reps, axis=0)
            w_perm = _dot(_dot(pick_rows, w_full).astype(BF16), pick_cols)
            wsm_ref[g] = jnp.where(keep, w_perm, 0.0).astype(BF16)

    @pl.when(j + 1 < ntile)
    def _():
        for c in fetch(j + 1):
            c.start()

    @pl.when(j < ntile)
    def _():
        for c in fetch(j):
            c.wait()

    @pl.when(s == 0)
    def _():
        ztail_ref[...] = jnp.zeros_like(ztail_ref)
        hcar_ref[...] = jnp.zeros_like(hcar_ref)

    def compute(z_w, z_r):
        xa_in = xbuf[lax.rem(ta, 3)].reshape(rows, d)
        h_next = _rmsnorm(xa_in, mixn_ref[...]).astype(BF16)
        pw = d // 2

        def project(k):
            z_w[:, k * pw:(k + 1) * pw] = _dot(h_next, win_ref[:, k * pw:(k + 1) * pw])

        x = xbuf[lax.rem(tb, 3)].reshape(rows, d)

        def sec(k, c0, c1):
            return z_r[:, k * d + c0:k * d + c1]

        def one_plus_tanh_gelu(v):
            c = 0.7978845608028654
            return 1.0 + jnp.tanh(v * (c + (c * 0.044715) * (v * v)))

        cw = 0.5 * convw_ref[...]
        cb_h = 0.5 * convb_ref[...]
        ba_h = 0.5 * ba_ref[...]
        bi_h = 0.5 * bi_ref[...]
        neg_lam = -lam_ref[...]
        softplus = jnp.maximum(neg_lam, 0.0) + jnp.log1p(jnp.exp(-jnp.abs(neg_lam)))
        c_a = (-0.5 * LRU_C) * softplus
        blk = V7X_MXU_DIM
        sub3 = lax.broadcasted_iota(jnp.int32, (CONV_WIDTH - 1, V7X_SUBLANES, blk), 1)
        term_a = []
        for n in range(d // blk):
            project(n)
            c0, c1 = n * blk, (n + 1) * blk
            z3 = sec(0, c0, c1).reshape(group, V7X_SUBLANES, blk)
            tail = z3[group - (CONV_WIDTH - 1):]
            halo = jnp.where(sub3 == 0, pltpu.roll(ztail_ref[:, :, c0:c1], 1, axis=1),
                             pltpu.roll(tail, 1, axis=1))
            ztail_ref[:, :, c0:c1] = tail
            zext = jnp.concatenate([halo, z3], axis=0)
            xa_h = cb_h[:, c0:c1] + cw[CONV_WIDTH - 1:CONV_WIDTH, c0:c1] * z3
            for k in range(1, CONV_WIDTH):
                lo = CONV_WIDTH - 1 - k
                xa_h = xa_h + cw[lo:lo + 1, c0:c1] * zext[lo:lo + group]
            xa2 = xa_h.reshape(rows, blk)
            xa_bf = xa2.astype(BF16)
            th_r = jnp.tanh(_dot(xa_bf, wa_ref[n]) + ba_h[:, c0:c1])
            th_i = jnp.tanh(_dot(xa_bf, wi_ref[n]) + bi_h[:, c0:c1])
            a = jnp.exp(c_a[:, c0:c1] + c_a[:, c0:c1] * th_r)
            u = jnp.sqrt(1.0 - a * a) * ((1.0 + th_i) * xa2)
            hseq, hlast = _lru_scan(a.reshape(group, V7X_SUBLANES, blk),
                                    u.reshape(group, V7X_SUBLANES, blk), hcar_ref[:, c0:c1])
            hcar_ref[:, c0:c1] = hlast
            zg = sec(1, c0, c1)
            term_a.append(((1.0 + jnp.tanh(sec(4, c0, c1))) * one_plus_tanh_gelu(zg))
                          * (zg * hseq.reshape(rows, blk)))

        project(4)
        zv = sec(3, 0, d)
        gv2 = zv * one_plus_tanh_gelu(zv)
        project(5)
        mu = jnp.mean(gv2, axis=-1, keepdims=True)
        xc = gv2 - mu
        var = jnp.mean(xc * xc, axis=-1, keepdims=True)
        v_bf = (xc * lax.rsqrt(var + 4.0 * EPS) * lng_ref[...] + lnb_ref[...]).astype(BF16)
        project(6)
        gdim = d // SGU_GROUPS
        term_b = []
        for g in range(SGU_GROUPS):
            c0, c1 = g * gdim, (g + 1) * gdim
            if g in (1, 3, 5, 6, 7):
                project({1: 7, 3: 8, 5: 9, 6: 10, 7: 11}[g])
            sp = _dot(wsm_ref[g], v_bf[:, c0:c1]) + bsp_ref[:, g:g + 1]
            zu = sec(2, c0, c1)
            term_b.append(((1.0 + jnp.tanh(sec(5, c0, c1))) * one_plus_tanh_gelu(zu)) * (zu * sp))
        merged4 = jnp.concatenate(term_a, axis=1) + jnp.concatenate(term_b, axis=1)

        x1 = x + _dot(merged4.astype(BF16), wout_ref[...])

        hn = _rmsnorm(x1, ffn_ref[...])
        hp = _pack_bf16_pair(hn[:, :half], hn[:, half:])

        @pl.when(j >= 3)
        def _():
            for c in put(tb - 2):
                c.wait()

        x1buf[slot] = x1.reshape(group, V7X_SUBLANES, d)
        hpbuf[slot] = hp.reshape(group, V7X_SUBLANES, half)

        @pl.when(j >= 1)
        def _():
            for c in put(tb):
                c.start()

    @pl.when(lax.rem(j, 2) == 0)
    def _():
        compute(z0_ref, z1_ref)

    @pl.when(lax.rem(j, 2) == 1)
    def _():
        compute(z1_ref, z0_ref)

    @pl.when(j == ntile)
    def _():
        for c in put(tb):
            c.wait()

        @pl.when(ntile >= 2)
        def _():
            for c in put(tb - 1):
                c.wait()


def _mixer_call(x, mix_norm, w_in, conv_w, conv_b, wa_blk, ba, wi_blk, bi, lam, ln_g, ln_b, ws,
                bs_tile, w_out, ffn_norm):
    cfg = _tiles()
    bsz, seq, d = x.shape
    ts = cfg["mixer_rows"]
    group = ts // V7X_SUBLANES
    nseq = seq // ts
    ntile = bsz * nseq
    row1 = (1, d)
    in_specs = [
        pl.BlockSpec(memory_space=pl.ANY),
        _const_spec(row1),
        _const_spec(w_in.shape),
        _const_spec(conv_w.shape), _const_spec(row1),
        _const_spec(wa_blk.shape), _const_spec(row1),
        _const_spec(wi_blk.shape), _const_spec(row1),
        _const_spec(row1),
        _const_spec(row1), _const_spec(row1),
        _const_spec(ws.shape), _const_spec(bs_tile.shape),
        _const_spec(w_out.shape), _const_spec(row1),
    ]
    out_shape = [
        jax.ShapeDtypeStruct((bsz, seq, d), F32),
        jax.ShapeDtypeStruct((bsz, seq, d // 2), U32),
    ]
    out_specs = [
        pl.BlockSpec(memory_space=pl.ANY),
        pl.BlockSpec(memory_space=pl.ANY),
    ]
    scratch = [
        pltpu.VMEM((3, group, V7X_SUBLANES, d), F32),
        pltpu.VMEM((ts, w_in.shape[1]), F32),
        pltpu.VMEM((ts, w_in.shape[1]), F32),
        pltpu.VMEM((2, group, V7X_SUBLANES, d), F32),
        pltpu.VMEM((2, group, V7X_SUBLANES, d // 2), U32),
        pltpu.SemaphoreType.DMA((3,)),
        pltpu.SemaphoreType.DMA((2,)),
        pltpu.SemaphoreType.DMA((2,)),
        pltpu.VMEM((SGU_GROUPS, ts, ts), BF16),
        pltpu.VMEM((CONV_WIDTH - 1, V7X_SUBLANES, d), F32),
        pltpu.VMEM((1, d), F32),
    ]
    return pl.pallas_call(
        functools.partial(_mixer_kernel, nseq=nseq),
        grid=(ntile + 1,),
        in_specs=in_specs,
        out_specs=out_specs,
        out_shape=out_shape,
        scratch_shapes=scratch,
        compiler_params=pltpu.CompilerParams(
            dimension_semantics=("arbitrary",),
            vmem_limit_bytes=cfg["mixer_vmem"]),
        name="mixer",
    )(x, mix_norm, w_in, conv_w, conv_b, wa_blk, ba, wi_blk, bi, lam, ln_g, ln_b, ws, bs_tile,
      w_out, ffn_norm)


def _router_kernel(hp_ref, wr_ref, br_ref, pos_ref, gate_ref, cnt_ref, ccar_ref, *, expert_capacity):
    rows = hp_ref.shape[0]

    @pl.when(pl.program_id(0) == 0)
    def _():
        ccar_ref[...] = jnp.zeros_like(ccar_ref)

    lo, hi = _unpack_bf16_pair(hp_ref[...])
    hn = jnp.concatenate([lo, hi], axis=1)
    logits = _dot(hn.astype(BF16), wr_ref[...])
    lt = jnp.transpose(logits) + br_ref[...]
    sub = lax.broadcasted_iota(jnp.int32, (V7X_SUBLANES, rows), 0)
    subf = sub.astype(F32)
    big = jnp.float32(1e9)

    lg = jnp.where(sub < N_GROUPS, lt[0:V7X_SUBLANES, :], -jnp.inf)
    g_exp = jnp.exp(lg - jnp.max(lg, axis=0, keepdims=True))
    g_prob = g_exp / jnp.sum(g_exp, axis=0, keepdims=True)
    g_top = jnp.max(g_prob, axis=0, keepdims=True)
    g_idx = jnp.min(jnp.where(g_prob == g_top, subf, big), axis=0, keepdims=True)

    e_sel = jnp.zeros((EXPERTS_PER_GROUP, rows), F32)
    for g in range(N_GROUPS):
        r0 = EXPERT_ROW0 + g * EXPERTS_PER_GROUP
        e_sel = jnp.where(g_idx == g, lt[r0:r0 + EXPERTS_PER_GROUP, :], e_sel)
    e_exp = jnp.exp(e_sel - jnp.max(e_sel, axis=0, keepdims=True))
    e_prob = e_exp / jnp.sum(e_exp, axis=0, keepdims=True)
    p1 = jnp.max(e_prob, axis=0, keepdims=True)
    i1 = jnp.min(jnp.where(e_prob == p1, subf, big), axis=0, keepdims=True)
    rest = jnp.where(subf == i1, -1.0, e_prob)
    p2 = jnp.max(rest, axis=0, keepdims=True)
    i2 = jnp.min(jnp.where(rest == p2, subf, big), axis=0, keepdims=True)
    psum = p1 + p2
    gate1 = g_top * (p1 / psum)
    gate2 = g_top * (p2 / psum)
    gid1 = g_idx * EXPERTS_PER_GROUP + i1
    gid2 = g_idx * EXPERTS_PER_GROUP + i2

    eid = lax.broadcasted_iota(jnp.int32, (N_EXPERTS, rows), 0).astype(F32)
    hit1 = eid == gid1
    hit2 = eid == gid2
    cnt = jnp.where(hit1 | hit2, 1.0, 0.0)
    sb = V7X_MXU_DIM
    before = (lax.broadcasted_iota(jnp.int32, (sb, sb), 0)
              < lax.broadcasted_iota(jnp.int32, (sb, sb), 1))
    before = jnp.where(before, 1.0, 0.0).astype(BF16)
    running = ccar_ref[:, 0:1]
    base = []
    for q in range(rows // sb):
        part = cnt[:, q * sb:(q + 1) * sb]
        base.append(running + _dot(part.astype(BF16), before))
        running = running + jnp.sum(part, axis=1, keepdims=True)
    base = jnp.concatenate(base, axis=1)
    rank1 = jnp.sum(jnp.where(hit1, base, 0.0), axis=0, keepdims=True)
    rank2 = jnp.sum(jnp.where(hit2, base, 0.0), axis=0, keepdims=True)
    total = jnp.broadcast_to(running, ccar_ref.shape)
    ccar_ref[...] = total
    cnt_ref[...] = total
    cap = float(expert_capacity)
    zero = jnp.zeros((V7X_SUBLANES - TOP_K, rows), F32)
    pos = jnp.concatenate([gid1 * cap + rank1, gid2 * cap + rank2, zero], axis=0)
    pos_ref[...] = pos.astype(jnp.int32)
    gate_ref[...] = jnp.transpose(jnp.concatenate([gate1, gate2, zero], axis=0))


def _router_call(hp, w_router, b_router):
    cfg = _tiles()
    ntok, half = hp.shape
    tr = cfg["router_rows"]
    return pl.pallas_call(
        functools.partial(_router_kernel, expert_capacity=ntok),
        grid=(ntok // tr,),
        in_specs=[
            pl.BlockSpec((tr, half), lambda i: (i, 0)),
            _const_spec(w_router.shape),
            _const_spec(b_router.shape),
        ],
        out_specs=[
            pl.BlockSpec((V7X_SUBLANES, tr), lambda i: (0, i)),
            pl.BlockSpec((tr, V7X_SUBLANES), lambda i: (i, 0)),
            pl.BlockSpec((N_EXPERTS, V7X_LANES), lambda i: (0, 0)),
        ],
        out_shape=[
            jax.ShapeDtypeStruct((V7X_SUBLANES, ntok), jnp.int32),
            jax.ShapeDtypeStruct((ntok, V7X_SUBLANES), F32),
            jax.ShapeDtypeStruct((N_EXPERTS, V7X_LANES), F32),
        ],
        scratch_shapes=[pltpu.VMEM((N_EXPERTS, V7X_LANES), F32)],
        compiler_params=pltpu.CompilerParams(
            dimension_semantics=("arbitrary",),
            vmem_limit_bytes=cfg["router_vmem"]),
        name="router",
    )(hp, w_router, b_router)


def _expert_kernel(nt_ref, base_ref, texp_ref, tloc_ref, hs_hbm, w1_ref, w3_ref, w2_ref, ys_hbm,
                   hbuf, ybuf, hsem, ysem, w1b_ref, w3b_ref, w2b_ref, *, capacity):
    e = pl.program_id(0)
    n_exp = pl.num_programs(0)
    nt = nt_ref[e]
    base = base_ref[e]
    total = base_ref[n_exp - 1] + nt_ref[n_exp - 1]
    n_in, tm, _ = hbuf.shape
    n_out = ybuf.shape[0]
    ahead = n_in - 1

    def load(g):
        slot = lax.rem(g, n_in)
        rows = pl.ds(texp_ref[g] * capacity + tloc_ref[g] * tm, tm)
        return pltpu.make_async_copy(hs_hbm.at[rows], hbuf.at[slot], hsem.at[slot])

    def store(t, slot):
        rows = pl.ds(e * capacity + t * tm, tm)
        return pltpu.make_async_copy(ybuf.at[slot], ys_hbm.at[rows], ysem.at[slot])

    @pl.when(e == 0)
    def _():
        for g0 in range(ahead):
            @pl.when(g0 < total)
            def _():
                load(g0).start()

    w1b_ref[...] = w1_ref[...].astype(BF16)
    w3b_ref[...] = w3_ref[...].astype(BF16)
    w2b_ref[...] = w2_ref[...].astype(BF16)

    @pl.loop(0, nt)
    def _(t):
        g = base + t

        @pl.when(g + ahead < total)
        def _():
            load(g + ahead).start()

        load(g).wait()
        lo, hi = _unpack_bf16_pair(hbuf[lax.rem(g, n_in)])
        h = jnp.concatenate([lo, hi], axis=1).astype(BF16)
        a = _dot(h, w1b_ref[...])
        b = _dot(h, w3b_ref[...])
        hid = (a * _sigmoid(a)) * b
        y = _dot(hid.astype(BF16), w2b_ref[...])
        half = y.shape[1] // 2
        slot = lax.rem(g, n_out)

        @pl.when(g >= n_out)
        def _():
            store(t, slot).wait()

        ybuf[slot] = _pack_bf16_pair(y[:, :half], y[:, half:])
        store(t, slot).start()

    @pl.when(e + 1 == n_exp)
    def _():
        for back in range(1, n_out + 1):
            @pl.when(total >= back)
            def _():
                store(0, lax.rem(total - back, n_out)).wait()


def _expert_call(tiles_per_expert, hs, w1, w3, w2, capacity):
    cfg = _tiles()
    tm = cfg["expert_rows"]
    prow, half = hs.shape
    n_exp, d, f = w1.shape
    ends = jnp.cumsum(tiles_per_expert)
    base = ends - tiles_per_expert
    g = jnp.arange(capacity * TOP_K // tm + n_exp, dtype=jnp.int32)
    texp = jnp.minimum(jnp.sum((ends[None, :] <= g[:, None]).astype(jnp.int32), axis=1), n_exp - 1)
    onehot = texp[:, None] == jnp.arange(n_exp, dtype=jnp.int32)[None, :]
    tloc = g - jnp.sum(jnp.where(onehot, base[None, :], 0), axis=1)

    def w_map(e, *_):
        return (e, 0, 0)

    grid_spec = pltpu.PrefetchScalarGridSpec(
        num_scalar_prefetch=4,
        grid=(n_exp,),
        in_specs=[
            pl.BlockSpec(memory_space=pl.ANY),
            pl.BlockSpec((None, d, f), w_map),
            pl.BlockSpec((None, d, f), w_map),
            pl.BlockSpec((None, f, d), w_map),
        ],
        out_specs=pl.BlockSpec(memory_space=pl.ANY),
        scratch_shapes=[
            pltpu.VMEM((EXPERT_LOOKAHEAD + 1, tm, half), U32),
            pltpu.VMEM((2, tm, half), U32),
            pltpu.SemaphoreType.DMA((EXPERT_LOOKAHEAD + 1,)),
            pltpu.SemaphoreType.DMA((2,)),
            pltpu.VMEM((d, f), BF16),
            pltpu.VMEM((d, f), BF16),
            pltpu.VMEM((f, d), BF16),
        ],
    )
    return pl.pallas_call(
        functools.partial(_expert_kernel, capacity=capacity),
        grid_spec=grid_spec,
        out_shape=jax.ShapeDtypeStruct((prow, half), U32),
        compiler_params=pltpu.CompilerParams(
            dimension_semantics=("arbitrary",),
            vmem_limit_bytes=cfg["expert_vmem"]),
        name="experts",
    )(tiles_per_expert, base, texp, tloc, hs, w1, w3, w2)


def _sc_mesh():
    return plsc.VectorSubcoreMesh(core_axis_name="c", subcore_axis_name="s",
                                  num_cores=V7X_SC_CORES, num_subcores=V7X_SC_SUBCORES)


def _sc_worker_id():
    return lax.axis_index("s") * V7X_SC_CORES + lax.axis_index("c")


def _dispatch_call(hp, pos_w, out_rows):
    cfg = _tiles()
    ntok, half = hp.shape
    nw, topk, nch, ch = pos_w.shape
    per_w = nch * ch

    def body(hp_hbm, pos_hbm, hs_hbm, idx_v, buf, rsem, wsem):
        wid = _sc_worker_id()
        pltpu.sync_copy(pos_hbm.at[wid], idx_v)

        def read(c):
            rows = hp_hbm.at[pl.ds(wid * per_w + c * ch, ch)]
            return pltpu.make_async_copy(rows, buf.at[c % 2], rsem.at[c % 2])

        def writes(c):
            return [pltpu.make_async_copy(buf.at[c % 2], hs_hbm.at[idx_v.at[k, c]], wsem.at[c % 2])
                    for k in range(topk)]

        read(0).start()
        for c in range(nch):
            read(c).wait()
            if c >= 1:
                for w in writes(c - 1):
                    w.wait()
            if c + 1 < nch:
                read(c + 1).start()
            for w in writes(c):
                w.start()
        for w in writes(nch - 1):
            w.wait()

    assert nw == V7X_SC_CORES * V7X_SC_SUBCORES and nw * per_w == ntok and ch == cfg["sc_rows"]
    return pl.kernel(
        body,
        out_type=jax.ShapeDtypeStruct((out_rows, half), U32),
        mesh=_sc_mesh(),
        scratch_types=[
            pltpu.VMEM((topk, nch, ch), jnp.int32),
            pltpu.VMEM((2, ch, half), U32),
            pltpu.SemaphoreType.DMA((2,)),
            pltpu.SemaphoreType.DMA((2,)),
        ],
        name="dispatch",
    )(hp, pos_w)


def _combine_call(ys, pos_w):
    cfg = _tiles()
    _, half = ys.shape
    nw, topk, nch, ch = pos_w.shape
    per_w = nch * ch
    ntok = nw * per_w

    def body(ys_hbm, pos_hbm, *rest):
        outs = rest[:topk]
        idx_v, buf, rsem, wsem = rest[topk:]
        wid = _sc_worker_id()
        pltpu.sync_copy(pos_hbm.at[wid], idx_v)
        items = [(c, k) for c in range(nch) for k in range(topk)]

        def read(i):
            c, k = items[i]
            return pltpu.make_async_copy(ys_hbm.at[idx_v.at[k, c]], buf.at[i % 2], rsem.at[i % 2])

        def write(i):
            c, k = items[i]
            rows = outs[k].at[pl.ds(wid * per_w + c * ch, ch)]
            return pltpu.make_async_copy(buf.at[i % 2], rows, wsem.at[i % 2])

        read(0).start()
        for i in range(len(items)):
            read(i).wait()
            if i >= 1:
                write(i - 1).wait()
            if i + 1 < len(items):
                read(i + 1).start()
            write(i).start()
        write(len(items) - 1).wait()

    assert nw == V7X_SC_CORES * V7X_SC_SUBCORES and ch == cfg["sc_rows"]
    return pl.kernel(
        body,
        out_type=[jax.ShapeDtypeStruct((ntok, half), U32)] * topk,
        mesh=_sc_mesh(),
        scratch_types=[
            pltpu.VMEM((topk, nch, ch), jnp.int32),
            pltpu.VMEM((2, ch, half), U32),
            pltpu.SemaphoreType.DMA((2,)),
            pltpu.SemaphoreType.DMA((2,)),
        ],
        name="combine",
    )(ys, pos_w)


def _ple_kernel(x1_ref, yg0_ref, yg1_ref, gate_ref, p_ref, plen_ref, wg_ref, wu_ref, fin_ref, o_ref):
    rows = x1_ref.shape[0]
    sub_rows = rows // PLE_SUBBLOCKS
    for q in range(PLE_SUBBLOCKS):
        rs = pl.ds(q * sub_rows, sub_rows)
        lo0, hi0 = _unpack_bf16_pair(yg0_ref[rs, :])
        lo1, hi1 = _unpack_bf16_pair(yg1_ref[rs, :])
        g0 = gate_ref[rs, 0:1]
        g1 = gate_ref[rs, 1:2]
        moe = g0 * jnp.concatenate([lo0, hi0], axis=1) + g1 * jnp.concatenate([lo1, hi1], axis=1)
        x2 = x1_ref[rs, :] + moe
        r = _rmsnorm(x2, plen_ref[...]).astype(BF16)
        gt = _sigmoid(_dot(r, wg_ref[...]))
        up = _dot(p_ref[rs, :].astype(BF16), wu_ref[...])
        x3 = x2 + gt * up
        o_ref[rs, :] = _rmsnorm(x3, fin_ref[...])


def _ple_call(x1, yg0, yg1, gates, p, ple_norm, wg, wu, final_norm, part, prev_out):
    cfg = _tiles()
    ntok, d = x1.shape
    tp = cfg["ple_rows"]
    pdim = p.shape[1]
    steps = yg0.shape[0] // tp
    off = part * steps
    in_specs = [
        pl.BlockSpec((tp, d), lambda i: (off + i, 0)),
        pl.BlockSpec((tp, d // 2), lambda i: (i, 0)),
        pl.BlockSpec((tp, d // 2), lambda i: (i, 0)),
        pl.BlockSpec((tp, V7X_SUBLANES), lambda i: (off + i, 0)),
        pl.BlockSpec((tp, pdim), lambda i: (off + i, 0)),
        _const_spec((1, d)),
        _const_spec(wg.shape),
        _const_spec(wu.shape),
        _const_spec((1, d)),
    ]
    args = [x1, yg0, yg1, gates, p, ple_norm, wg, wu, final_norm]
    kernel_fn = _ple_kernel
    aliases = {}
    if prev_out is not None:
        in_specs.append(pl.BlockSpec(memory_space=pl.ANY))
        args.append(prev_out)
        aliases = {len(args) - 1: 0}

        def kernel_fn(*refs):
            _ple_kernel(*refs[:len(args) - 1], refs[-1])

    return pl.pallas_call(
        kernel_fn,
        grid=(steps,),
        in_specs=in_specs,
        out_specs=pl.BlockSpec((tp, d), lambda i: (off + i, 0)),
        out_shape=jax.ShapeDtypeStruct((ntok, d), F32),
        input_output_aliases=aliases,
        compiler_params=pltpu.CompilerParams(
            dimension_semantics=("arbitrary",),
            vmem_limit_bytes=cfg["ple_vmem"]),
        name="ple",
    )(*args)


def _blockdiag_pack(w):
    nb, bd, _ = w.shape
    per = V7X_MXU_DIM // bd
    w4 = w.reshape(nb // per, per, bd, bd)
    eye = jnp.eye(per, dtype=w.dtype)
    out = jnp.einsum("jpab,pq->jpaqb", w4, eye)
    return out.reshape(nb // per, V7X_MXU_DIM, V7X_MXU_DIM).astype(BF16)


def kernel(x, p, mix_norm, w_in, conv_w, conv_b, lru_wa, lru_ba, lru_wi, lru_bi, lru_lambda, sgu_ln_g, sgu_ln_b, sgu_ws, sgu_bs, w_out, ffn_norm, router_group_w, router_group_b, router_expert_w, router_expert_b, expert_w1, expert_w3, expert_w2, ple_norm, ple_gate_w, ple_up_w, final_norm):
    cfg = _tiles()
    bsz, seq, d = x.shape
    ntok = bsz * seq
    tm = cfg["expert_rows"]
    depth = w_in.shape[0]
    assert depth == 1, "the ple kernel applies the final norm, so it must be the last layer"
    l = 0
    w_router = jnp.zeros((d, ROUTER_ROWS), F32)
    w_router = w_router.at[:, :N_GROUPS].set(router_group_w[l])
    w_router = w_router.at[:, EXPERT_ROW0:EXPERT_ROW0 + N_EXPERTS].set(router_expert_w[l])
    b_router = jnp.zeros((ROUTER_ROWS, 1), F32)
    b_router = b_router.at[:N_GROUPS, 0].set(router_group_b[l])
    b_router = b_router.at[EXPERT_ROW0:EXPERT_ROW0 + N_EXPERTS, 0].set(router_expert_b[l])
    col_scale = jnp.concatenate([jnp.ones((4 * d,), F32), jnp.full((2 * d,), 0.5, F32)])
    ts = cfg["mixer_rows"]
    group = ts // V7X_SUBLANES
    bs_tile = jnp.tile(sgu_bs[l], (1, ts // CHUNK)).reshape(SGU_GROUPS, V7X_SUBLANES, group)
    bs_tile = jnp.transpose(bs_tile, (2, 1, 0)).reshape(ts, SGU_GROUPS)
    x1, hp = _mixer_call(
        x, mix_norm[l][None], (w_in[l] * col_scale[None, :]).astype(BF16), conv_w[l], conv_b[l][None],
        _blockdiag_pack(lru_wa[l]), lru_ba[l][None], _blockdiag_pack(lru_wi[l]), lru_bi[l][None],
        lru_lambda[l][None], sgu_ln_g[l][None], sgu_ln_b[l][None], sgu_ws[l], bs_tile,
        (0.25 * w_out[l]).astype(BF16), ffn_norm[l][None])
    hp = hp.reshape(ntok, d // 2)
    pos, gate, cnt = _router_call(hp, w_router.astype(BF16), b_router)

    cap = ntok
    tiles_per_expert = (cnt[:, 0].astype(jnp.int32) + tm - 1) // tm
    nw = V7X_SC_CORES * V7X_SC_SUBCORES
    ch = cfg["sc_rows"]
    pos_w = jnp.transpose(pos[:TOP_K].reshape(TOP_K, nw, ntok // (nw * ch), ch), (1, 0, 2, 3))

    hs = _dispatch_call(hp, pos_w, N_EXPERTS * cap)
    ys = _expert_call(tiles_per_expert, hs, expert_w1[l], expert_w3[l], expert_w2[l], cap)
    nparts = cfg["ple_parts"]
    part_tok = ntok // nparts
    out = None
    for part in range(nparts):
        seg = pos[:TOP_K, part * part_tok:(part + 1) * part_tok]
        seg_w = jnp.transpose(seg.reshape(TOP_K, nw, part_tok // (nw * ch), ch), (1, 0, 2, 3))
        yg0, yg1 = _combine_call(ys, seg_w)
        out = _ple_call(x1.reshape(ntok, d), yg0, yg1, gate, p[l].reshape(ntok, -1), ple_norm[l][None],
                        ple_gate_w[l].astype(BF16), ple_up_w[l].astype(BF16), final_norm[None],
                        part, out)
    return out.reshape(bsz, seq, d)
```

```python
import functools

import jax
import jax.numpy as jnp
from jax import lax
from jax.experimental import pallas as pl
from jax.experimental.pallas import tpu as pltpu
from jax.experimental.pallas import tpu_sc as plsc

F32 = jnp.float32
BF16 = jnp.bfloat16
U32 = jnp.uint32

LRU_BLOCKS = 16
CONV_WIDTH = 4
LRU_C = 8.0
SGU_GROUPS = 8
CHUNK = 128
N_GROUPS = 4
EXPERTS_PER_GROUP = 8
N_EXPERTS = N_GROUPS * EXPERTS_PER_GROUP
TOP_K = 2
EPS = 1e-6

V7X_MXU_DIM = 256
V7X_SUBLANES = 8
V7X_LANES = 128
V7X_VMEM_BYTES = 64 * 1024 * 1024
V7X_SC_CORES = 2
V7X_SC_SUBCORES = 16

EXPERT_LOOKAHEAD = 3
PLE_SUBBLOCKS = 4
ROUTER_ROWS = V7X_LANES
EXPERT_ROW0 = V7X_SUBLANES


def _tiles():
    return dict(
        mixer_rows=256,
        expert_rows=512,
        ple_rows=1024,
        sc_rows=64,
        router_rows=1024,
        mixer_vmem=52 * 1024 * 1024,
        expert_vmem=40 * 1024 * 1024,
        ple_vmem=48 * 1024 * 1024,
        router_vmem=32 * 1024 * 1024,
    )


def _dot(a, b):
    return jnp.dot(a, b, preferred_element_type=F32)


def _sigmoid(x):
    return 0.5 * jnp.tanh(0.5 * x) + 0.5


def _rmsnorm(x, g):
    ms = jnp.mean(x * x, axis=-1, keepdims=True)
    return x * lax.rsqrt(ms + EPS) * g


def _pack_bf16_pair(lo, hi):
    lo_b = lax.bitcast_convert_type(lo.astype(BF16).astype(F32), U32)
    hi_b = lax.bitcast_convert_type(hi.astype(BF16).astype(F32), U32)
    return (hi_b & jnp.uint32(0xFFFF0000)) | lax.shift_right_logical(lo_b, jnp.uint32(16))


def _unpack_bf16_pair(w):
    lo = lax.bitcast_convert_type(lax.shift_left(w, jnp.uint32(16)), F32)
    hi = lax.bitcast_convert_type(w & jnp.uint32(0xFFFF0000), F32)
    return lo, hi


def _const_spec(shape):
    zeros = (0,) * len(shape)
    return pl.BlockSpec(shape, lambda *_: zeros, pipeline_mode=pl.Buffered(1))


def _tile_copies(hbm, buf, sem, b, row0, slot, to_hbm):
    group = buf.shape[1]
    copies = []
    for r in range(V7X_SUBLANES):
        hbm_rows = hbm.at[b, pl.ds(row0 + group * r, group), :]
        vmem_rows = buf.at[slot, :, r, :]
        src, dst = (vmem_rows, hbm_rows) if to_hbm else (hbm_rows, vmem_rows)
        copies.append(pltpu.make_async_copy(src, dst, sem.at[slot]))
    return copies


def _lru_scan(a, u, h0):
    group = a.shape[0]
    acc_a = [a[0]]
    acc_u = [u[0]]
    for g in range(1, group):
        acc_a.append(a[g] * acc_a[-1])
        acc_u.append(a[g] * acc_u[-1] + u[g])
    end_a, end_u = acc_a[-1], acc_u[-1]
    sub = lax.broadcasted_iota(jnp.int32, end_a.shape, 0)
    shift = 1
    while shift < V7X_SUBLANES:
        keep = sub >= shift
        a_sh = pltpu.roll(end_a, shift, axis=0)
        u_sh = pltpu.roll(end_u, shift, axis=0)
        end_u = jnp.where(keep, end_a * u_sh + end_u, end_u)
        end_a = jnp.where(keep, end_a * a_sh, end_a)
        shift *= 2
    h_end = end_a * h0 + end_u
    h_in = jnp.where(sub == 0, h0, pltpu.roll(h_end, 1, axis=0))
    out = [acc_a[g] * h_in + acc_u[g] for g in range(group)]
    return jnp.stack(out, axis=0), h_end[V7X_SUBLANES - 1:V7X_SUBLANES, :]


def _mixer_kernel(x_hbm, mixn_ref, win_hbm, convw_ref, convb_ref, wa_ref, ba_ref, wi_ref, bi_ref,
                  lam_ref, lng_ref, lnb_ref, ws_ref, bsp_ref, wout_hbm, ffn_ref,
                  x1_hbm, hp_hbm,
                  xbuf, z0_ref, z1_ref, x1buf, hpbuf, xsem, x1sem, hpsem, wsm_ref, ztail_ref, hcar_ref,
                  win_ref, wout_ref, wsem,
                  *, nseq):
    j = pl.program_id(0)
    ntile = pl.num_programs(0) - 1
    _, group, _, d = xbuf.shape
    rows = group * V7X_SUBLANES
    half = d // 2
    ta = jnp.minimum(j, ntile - 1)
    tb = jnp.maximum(j - 1, 0)
    s = lax.rem(tb, nseq)
    slot = lax.rem(tb, 2)

    def fetch(t):
        return _tile_copies(x_hbm, xbuf, xsem, lax.div(t, nseq), lax.rem(t, nseq) * rows,
                            lax.rem(t, 3), to_hbm=False)

    def put(t):
        tb_, ts_, sl = lax.div(t, nseq), lax.rem(t, nseq) * rows, lax.rem(t, 2)
        return (_tile_copies(x1_hbm, x1buf, x1sem, tb_, ts_, sl, to_hbm=True)
                + _tile_copies(hp_hbm, hpbuf, hpsem, tb_, ts_, sl, to_hbm=True))

    @pl.when(j == 0)
    def _():
        for c in fetch(0):
            c.start()
        stage = (z0_ref, z1_ref)
        n_in_chunks = win_hbm.shape[0] // rows

        def win_copy(c):
            return pltpu.make_async_copy(win_hbm.at[pl.ds(c * rows, rows), :], stage[c % 2], wsem.at[c % 2])

        win_copy(0).start()
        for c in range(n_in_chunks):
            if c + 1 < n_in_chunks:
                win_copy(c + 1).start()
            win_copy(c).wait()
            plain = 4 * d
            win_ref[c * rows:(c + 1) * rows, :plain] = stage[c % 2][:, :plain].astype(BF16)
            win_ref[c * rows:(c + 1) * rows, plain:] = (0.5 * stage[c % 2][:, plain:]).astype(BF16)
        n_out_chunks = wout_hbm.shape[0] // rows
        out_copies = [pltpu.make_async_copy(wout_hbm.at[pl.ds(c * rows, rows), :],
                                            z0_ref.at[:, c * d:(c + 1) * d], wsem.at[0])
                      for c in range(n_out_chunks)]
        for cp in out_copies:
            cp.start()
        for cp in out_copies:
            cp.wait()
        for c in range(n_out_chunks):
            wout_ref[c * rows:(c + 1) * rows, :] = (0.25 * z0_ref[:, c * d:(c + 1) * d]).astype(BF16)
        z1_ref[...] = jnp.zeros_like(z1_ref)
        i_idx = lax.broadcasted_iota(jnp.int32, (rows, rows), 0)
        j_idx = lax.broadcasted_iota(jnp.int32, (rows, rows), 1)
        t_i = group * lax.rem(i_idx, V7X_SUBLANES) + lax.div(i_idx, V7X_SUBLANES)
        t_j = group * lax.rem(j_idx, V7X_SUBLANES) + lax.div(j_idx, V7X_SUBLANES)
        keep = (t_i >= t_j) & (lax.div(t_i, CHUNK) == lax.div(t_j, CHUNK))
        pick_rows = jnp.where(t_i == j_idx, 1.0, 0.0).astype(BF16)
        pick_cols = jnp.where(i_idx == t_j, 1.0, 0.0).astype(BF16)
        reps = rows // CHUNK
        for g in range(SGU_GROUPS):
            w_chunk = ws_ref[g].astype(BF16)
            w_rows = jnp.concatenate([w_chunk] * reps, axis=1)
            w_full = jnp.concatenate([w_rows] * reps, axis=0)
            w_perm = _dot(_dot(pick_rows, w_full).astype(BF16), pick_cols)
            wsm_ref[g] = jnp.where(keep, w_perm, 0.0).astype(BF16)

    @pl.when(j + 1 < ntile)
    def _():
        for c in fetch(j + 1):
            c.start()

    @pl.when(j < ntile)
    def _():
        for c in fetch(j):
            c.wait()

    @pl.when(s == 0)
    def _():
        ztail_ref[...] = jnp.zeros_like(ztail_ref)
        hcar_ref[...] = jnp.zeros_like(hcar_ref)

    def compute(z_w, z_r):
        xa_in = xbuf[lax.rem(ta, 3)].reshape(rows, d)
        h_next = _rmsnorm(xa_in, mixn_ref[...]).astype(BF16)
        pw = d // 2

        def project(k):
            z_w[:, k * pw:(k + 1) * pw] = _dot(h_next, win_ref[:, k * pw:(k + 1) * pw])

        x = xbuf[lax.rem(tb, 3)].reshape(rows, d)

        def sec(k, c0, c1):
            return z_r[:, k * d + c0:k * d + c1]

        def one_plus_tanh_gelu(v):
            c = 0.7978845608028654
            return 1.0 + jnp.tanh(v * (c + (c * 0.044715) * (v * v)))

        cw = 0.5 * convw_ref[...]
        cb_h = 0.5 * convb_ref[...]
        ba_h = 0.5 * ba_ref[...]
        bi_h = 0.5 * bi_ref[...]
        neg_lam = -lam_ref[...]
        softplus = jnp.maximum(neg_lam, 0.0) + jnp.log1p(jnp.exp(-jnp.abs(neg_lam)))
        c_a = (-0.5 * LRU_C) * softplus
        blk = V7X_MXU_DIM
        sub3 = lax.broadcasted_iota(jnp.int32, (CONV_WIDTH - 1, V7X_SUBLANES, blk), 1)
        term_a = []
        for n in range(d // blk):
            project(n)
            c0, c1 = n * blk, (n + 1) * blk
            z3 = sec(0, c0, c1).reshape(group, V7X_SUBLANES, blk)
            tail = z3[group - (CONV_WIDTH - 1):]
            halo = jnp.where(sub3 == 0, pltpu.roll(ztail_ref[:, :, c0:c1], 1, axis=1),
                             pltpu.roll(tail, 1, axis=1))
            ztail_ref[:, :, c0:c1] = tail
            zext = jnp.concatenate([halo, z3], axis=0)
            xa_h = cb_h[:, c0:c1] + cw[CONV_WIDTH - 1:CONV_WIDTH, c0:c1] * z3
            for k in range(1, CONV_WIDTH):
                lo = CONV_WIDTH - 1 - k
                xa_h = xa_h + cw[lo:lo + 1, c0:c1] * zext[lo:lo + group]
            xa2 = xa_h.reshape(rows, blk)
            xa_bf = xa2.astype(BF16)
            th_r = jnp.tanh(_dot(xa_bf, wa_ref[n]) + ba_h[:, c0:c1])
            th_i = jnp.tanh(_dot(xa_bf, wi_ref[n]) + bi_h[:, c0:c1])
            a = jnp.exp(c_a[:, c0:c1] + c_a[:, c0:c1] * th_r)
            u = jnp.sqrt(1.0 - a * a) * ((1.0 + th_i) * xa2)
            hseq, hlast = _lru_scan(a.reshape(group, V7X_SUBLANES, blk),
                                    u.reshape(group, V7X_SUBLANES, blk), hcar_ref[:, c0:c1])
            hcar_ref[:, c0:c1] = hlast
            zg = sec(1, c0, c1)
            term_a.append(((1.0 + jnp.tanh(sec(4, c0, c1))) * one_plus_tanh_gelu(zg))
                          * (zg * hseq.reshape(rows, blk)))

        project(4)
        zv = sec(3, 0, d)
        gv2 = zv * one_plus_tanh_gelu(zv)
        project(5)
        mu = jnp.mean(gv2, axis=-1, keepdims=True)
        xc = gv2 - mu
        var = jnp.mean(xc * xc, axis=-1, keepdims=True)
        v_bf = (xc * lax.rsqrt(var + 4.0 * EPS) * lng_ref[...] + lnb_ref[...]).astype(BF16)
        project(6)
        gdim = d // SGU_GROUPS
        term_b = []
        for g in range(SGU_GROUPS):
            c0, c1 = g * gdim, (g + 1) * gdim
            if g in (1, 3, 5, 6, 7):
                project({1: 7, 3: 8, 5: 9, 6: 10, 7: 11}[g])
            sp = _dot(wsm_ref[g], v_bf[:, c0:c1]) + bsp_ref[:, g:g + 1]
            zu = sec(2, c0, c1)
            term_b.append(((1.0 + jnp.tanh(sec(5, c0, c1))) * one_plus_tanh_gelu(zu)) * (zu * sp))
        merged4 = jnp.concatenate(term_a, axis=1) + jnp.concatenate(term_b, axis=1)

        x1 = x + _dot(merged4.astype(BF16), wout_ref[...])

        hn = _rmsnorm(x1, ffn_ref[...])
        hp = _pack_bf16_pair(hn[:, :half], hn[:, half:])

        @pl.when(j >= 3)
        def _():
            for c in put(tb - 2):
                c.wait()

        x1buf[slot] = x1.reshape(group, V7X_SUBLANES, d)
        hpbuf[slot] = hp.reshape(group, V7X_SUBLANES, half)

        @pl.when(j >= 1)
        def _():
            for c in put(tb):
                c.start()

    @pl.when(lax.rem(j, 2) == 0)
    def _():
        compute(z0_ref, z1_ref)

    @pl.when(lax.rem(j, 2) == 1)
    def _():
        compute(z1_ref, z0_ref)

    @pl.when(j == ntile)
    def _():
        for c in put(tb):
            c.wait()

        @pl.when(ntile >= 2)
        def _():
            for c in put(tb - 1):
                c.wait()


def _mixer_call(x, mix_norm, w_in, conv_w, conv_b, wa_blk, ba, wi_blk, bi, lam, ln_g, ln_b, ws,
                bs_tile, w_out, ffn_norm):
    cfg = _tiles()
    bsz, seq, d = x.shape
    ts = cfg["mixer_rows"]
    group = ts // V7X_SUBLANES
    nseq = seq // ts
    ntile = bsz * nseq
    row1 = (1, d)
    in_specs = [
        pl.BlockSpec(memory_space=pl.ANY),
        _const_spec(row1),
        pl.BlockSpec(memory_space=pl.ANY),
        _const_spec(conv_w.shape), _const_spec(row1),
        _const_spec(wa_blk.shape), _const_spec(row1),
        _const_spec(wi_blk.shape), _const_spec(row1),
        _const_spec(row1),
        _const_spec(row1), _const_spec(row1),
        _const_spec(ws.shape), _const_spec(bs_tile.shape),
        pl.BlockSpec(memory_space=pl.ANY), _const_spec(row1),
    ]
    out_shape = [
        jax.ShapeDtypeStruct((bsz, seq, d), F32),
        jax.ShapeDtypeStruct((bsz, seq, d // 2), U32),
    ]
    out_specs = [
        pl.BlockSpec(memory_space=pl.ANY),
        pl.BlockSpec(memory_space=pl.ANY),
    ]
    scratch = [
        pltpu.VMEM((3, group, V7X_SUBLANES, d), F32),
        pltpu.VMEM((ts, w_in.shape[1]), F32),
        pltpu.VMEM((ts, w_in.shape[1]), F32),
        pltpu.VMEM((2, group, V7X_SUBLANES, d), F32),
        pltpu.VMEM((2, group, V7X_SUBLANES, d // 2), U32),
        pltpu.SemaphoreType.DMA((3,)),
        pltpu.SemaphoreType.DMA((2,)),
        pltpu.SemaphoreType.DMA((2,)),
        pltpu.VMEM((SGU_GROUPS, ts, ts), BF16),
        pltpu.VMEM((CONV_WIDTH - 1, V7X_SUBLANES, d), F32),
        pltpu.VMEM((1, d), F32),
        pltpu.VMEM(w_in.shape, BF16),
        pltpu.VMEM(w_out.shape, BF16),
        pltpu.SemaphoreType.DMA((2,)),
    ]
    return pl.pallas_call(
        functools.partial(_mixer_kernel, nseq=nseq),
        grid=(ntile + 1,),
        in_specs=in_specs,
        out_specs=out_specs,
        out_shape=out_shape,
        scratch_shapes=scratch,
        compiler_params=pltpu.CompilerParams(
            dimension_semantics=("arbitrary",),
            vmem_limit_bytes=cfg["mixer_vmem"]),
        name="mixer",
    )(x, mix_norm, w_in, conv_w, conv_b, wa_blk, ba, wi_blk, bi, lam, ln_g, ln_b, ws, bs_tile,
      w_out, ffn_norm)


def _router_kernel(hp_ref, wg_ref, we_ref, br_ref, pos_ref, gate_ref, cnt_ref, ccar_ref, wr_ref,
                   *, expert_capacity):
    rows = hp_ref.shape[0]

    @pl.when(pl.program_id(0) == 0)
    def _():
        ccar_ref[...] = jnp.zeros_like(ccar_ref)
        wr_ref[...] = jnp.zeros_like(wr_ref)
        wr_ref[:, 0:N_GROUPS] = wg_ref[...].astype(BF16)
        wr_ref[:, EXPERT_ROW0:EXPERT_ROW0 + N_EXPERTS] = we_ref[...].astype(BF16)

    lo, hi = _unpack_bf16_pair(hp_ref[...])
    hn = jnp.concatenate([lo, hi], axis=1)
    logits = _dot(hn.astype(BF16), wr_ref[...])
    lt = jnp.transpose(logits) + br_ref[...]
    sub = lax.broadcasted_iota(jnp.int32, (V7X_SUBLANES, rows), 0)
    subf = sub.astype(F32)
    big = jnp.float32(1e9)

    lg = jnp.where(sub < N_GROUPS, lt[0:V7X_SUBLANES, :], -jnp.inf)
    g_exp = jnp.exp(lg - jnp.max(lg, axis=0, keepdims=True))
    g_prob = g_exp / jnp.sum(g_exp, axis=0, keepdims=True)
    g_top = jnp.max(g_prob, axis=0, keepdims=True)
    g_idx = jnp.min(jnp.where(g_prob == g_top, subf, big), axis=0, keepdims=True)

    e_sel = jnp.zeros((EXPERTS_PER_GROUP, rows), F32)
    for g in range(N_GROUPS):
        r0 = EXPERT_ROW0 + g * EXPERTS_PER_GROUP
        e_sel = jnp.where(g_idx == g, lt[r0:r0 + EXPERTS_PER_GROUP, :], e_sel)
    e_exp = jnp.exp(e_sel - jnp.max(e_sel, axis=0, keepdims=True))
    e_prob = e_exp / jnp.sum(e_exp, axis=0, keepdims=True)
    p1 = jnp.max(e_prob, axis=0, keepdims=True)
    i1 = jnp.min(jnp.where(e_prob == p1, subf, big), axis=0, keepdims=True)
    rest = jnp.where(subf == i1, -1.0, e_prob)
    p2 = jnp.max(rest, axis=0, keepdims=True)
    i2 = jnp.min(jnp.where(rest == p2, subf, big), axis=0, keepdims=True)
    psum = p1 + p2
    gate1 = g_top * (p1 / psum)
    gate2 = g_top * (p2 / psum)
    gid1 = g_idx * EXPERTS_PER_GROUP + i1
    gid2 = g_idx * EXPERTS_PER_GROUP + i2

    eid = lax.broadcasted_iota(jnp.int32, (N_EXPERTS, rows), 0).astype(F32)
    hit1 = eid == gid1
    hit2 = eid == gid2
    cnt = jnp.where(hit1 | hit2, 1.0, 0.0)
    sb = V7X_MXU_DIM
    before = (lax.broadcasted_iota(jnp.int32, (sb, sb), 0)
              < lax.broadcasted_iota(jnp.int32, (sb, sb), 1))
    before = jnp.where(before, 1.0, 0.0).astype(BF16)
    running = ccar_ref[:, 0:1]
    base = []
    for q in range(rows // sb):
        part = cnt[:, q * sb:(q + 1) * sb]
        base.append(running + _dot(part.astype(BF16), before))
        running = running + jnp.sum(part, axis=1, keepdims=True)
    base = jnp.concatenate(base, axis=1)
    rank1 = jnp.sum(jnp.where(hit1, base, 0.0), axis=0, keepdims=True)
    rank2 = jnp.sum(jnp.where(hit2, base, 0.0), axis=0, keepdims=True)
    total = jnp.broadcast_to(running, ccar_ref.shape)
    ccar_ref[...] = total
    cnt_ref[...] = total
    cap = float(expert_capacity)
    zero = jnp.zeros((V7X_SUBLANES - TOP_K, rows), F32)
    pos = jnp.concatenate([gid1 * cap + rank1, gid2 * cap + rank2, zero], axis=0)
    pos_ref[...] = pos.astype(jnp.int32)
    gate_ref[...] = jnp.transpose(jnp.concatenate([gate1, gate2, zero], axis=0))


def _router_call(hp, w_group, w_expert, b_router):
    cfg = _tiles()
    ntok, half = hp.shape
    tr = cfg["router_rows"]
    return pl.pallas_call(
        functools.partial(_router_kernel, expert_capacity=ntok),
        grid=(ntok // tr,),
        in_specs=[
            pl.BlockSpec((tr, half), lambda i: (i, 0)),
            _const_spec(w_group.shape),
            _const_spec(w_expert.shape),
            _const_spec(b_router.shape),
        ],
        out_specs=[
            pl.BlockSpec((V7X_SUBLANES, tr), lambda i: (0, i)),
            pl.BlockSpec((tr, V7X_SUBLANES), lambda i: (i, 0)),
            pl.BlockSpec((N_EXPERTS, V7X_LANES), lambda i: (0, 0)),
        ],
        out_shape=[
            jax.ShapeDtypeStruct((V7X_SUBLANES, ntok), jnp.int32),
            jax.ShapeDtypeStruct((ntok, V7X_SUBLANES), F32),
            jax.ShapeDtypeStruct((N_EXPERTS, V7X_LANES), F32),
        ],
        scratch_shapes=[
            pltpu.VMEM((N_EXPERTS, V7X_LANES), F32),
            pltpu.VMEM((w_group.shape[0], ROUTER_ROWS), BF16),
        ],
        compiler_params=pltpu.CompilerParams(
            dimension_semantics=("arbitrary",),
            vmem_limit_bytes=cfg["router_vmem"]),
        name="router",
    )(hp, w_group, w_expert, b_router)


def _expert_kernel(nt_ref, base_ref, texp_ref, tloc_ref, hs_hbm, w1_ref, w3_ref, w2_ref, ys_hbm,
                   hbuf, ybuf, hsem, ysem, w1b_ref, w3b_ref, w2b_ref, *, capacity):
    e = pl.program_id(0)
    n_exp = pl.num_programs(0)
    nt = nt_ref[e]
    base = base_ref[e]
    total = base_ref[n_exp - 1] + nt_ref[n_exp - 1]
    n_in, tm, _ = hbuf.shape
    n_out = ybuf.shape[0]
    ahead = n_in - 1

    def load(g):
        slot = lax.rem(g, n_in)
        rows = pl.ds(texp_ref[g] * capacity + tloc_ref[g] * tm, tm)
        return pltpu.make_async_copy(hs_hbm.at[rows], hbuf.at[slot], hsem.at[slot])

    def store(t, slot):
        rows = pl.ds(e * capacity + t * tm, tm)
        return pltpu.make_async_copy(ybuf.at[slot], ys_hbm.at[rows], ysem.at[slot])

    @pl.when(e == 0)
    def _():
        for g0 in range(ahead):
            @pl.when(g0 < total)
            def _():
                load(g0).start()

    w1b_ref[...] = w1_ref[...].astype(BF16)
    w3b_ref[...] = w3_ref[...].astype(BF16)
    w2b_ref[...] = w2_ref[...].astype(BF16)

    @pl.loop(0, nt)
    def _(t):
        g = base + t

        @pl.when(g + ahead < total)
        def _():
            load(g + ahead).start()

        load(g).wait()
        lo, hi = _unpack_bf16_pair(hbuf[lax.rem(g, n_in)])
        h = jnp.concatenate([lo, hi], axis=1).astype(BF16)
        a = _dot(h, w1b_ref[...])
        b = _dot(h, w3b_ref[...])
        hid = (a * _sigmoid(a)) * b
        y = _dot(hid.astype(BF16), w2b_ref[...])
        half = y.shape[1] // 2
        slot = lax.rem(g, n_out)

        @pl.when(g >= n_out)
        def _():
            store(t, slot).wait()

        ybuf[slot] = _pack_bf16_pair(y[:, :half], y[:, half:])
        store(t, slot).start()

    @pl.when(e + 1 == n_exp)
    def _():
        for back in range(1, n_out + 1):
            @pl.when(total >= back)
            def _():
                store(0, lax.rem(total - back, n_out)).wait()


def _expert_call(tiles_per_expert, hs, w1, w3, w2, capacity):
    cfg = _tiles()
    tm = cfg["expert_rows"]
    prow, half = hs.shape
    n_exp, d, f = w1.shape
    ends = jnp.cumsum(tiles_per_expert)
    base = ends - tiles_per_expert
    g = jnp.arange(capacity * TOP_K // tm + n_exp, dtype=jnp.int32)
    texp = jnp.minimum(jnp.sum((ends[None, :] <= g[:, None]).astype(jnp.int32), axis=1), n_exp - 1)
    onehot = texp[:, None] == jnp.arange(n_exp, dtype=jnp.int32)[None, :]
    tloc = g - jnp.sum(jnp.where(onehot, base[None, :], 0), axis=1)

    def w_map(e, *_):
        return (e, 0, 0)

    grid_spec = pltpu.PrefetchScalarGridSpec(
        num_scalar_prefetch=4,
        grid=(n_exp,),
        in_specs=[
            pl.BlockSpec(memory_space=pl.ANY),
            pl.BlockSpec((None, d, f), w_map),
            pl.BlockSpec((None, d, f), w_map),
            pl.BlockSpec((None, f, d), w_map),
        ],
        out_specs=pl.BlockSpec(memory_space=pl.ANY),
        scratch_shapes=[
            pltpu.VMEM((EXPERT_LOOKAHEAD + 1, tm, half), U32),
            pltpu.VMEM((2, tm, half), U32),
            pltpu.SemaphoreType.DMA((EXPERT_LOOKAHEAD + 1,)),
            pltpu.SemaphoreType.DMA((2,)),
            pltpu.VMEM((d, f), BF16),
            pltpu.VMEM((d, f), BF16),
            pltpu.VMEM((f, d), BF16),
        ],
    )
    return pl.pallas_call(
        functools.partial(_expert_kernel, capacity=capacity),
        grid_spec=grid_spec,
        out_shape=jax.ShapeDtypeStruct((prow, half), U32),
        compiler_params=pltpu.CompilerParams(
            dimension_semantics=("arbitrary",),
            vmem_limit_bytes=cfg["expert_vmem"]),
        name="experts",
    )(tiles_per_expert, base, texp, tloc, hs, w1, w3, w2)


def _sc_mesh():
    return plsc.VectorSubcoreMesh(core_axis_name="c", subcore_axis_name="s",
                                  num_cores=V7X_SC_CORES, num_subcores=V7X_SC_SUBCORES)


def _sc_worker_id():
    return lax.axis_index("s") * V7X_SC_CORES + lax.axis_index("c")


def _dispatch_call(hp, pos_w, out_rows):
    cfg = _tiles()
    ntok, half = hp.shape
    nw, topk, nch, ch = pos_w.shape
    per_w = nch * ch

    def body(hp_hbm, pos_hbm, hs_hbm, idx_v, buf, rsem, wsem):
        wid = _sc_worker_id()
        pltpu.sync_copy(pos_hbm.at[wid], idx_v)

        def read(c):
            rows = hp_hbm.at[pl.ds(wid * per_w + c * ch, ch)]
            return pltpu.make_async_copy(rows, buf.at[c % 2], rsem.at[c % 2])

        def writes(c):
            return [pltpu.make_async_copy(buf.at[c % 2], hs_hbm.at[idx_v.at[k, c]], wsem.at[c % 2])
                    for k in range(topk)]

        read(0).start()
        for c in range(nch):
            read(c).wait()
            if c >= 1:
                for w in writes(c - 1):
                    w.wait()
            if c + 1 < nch:
                read(c + 1).start()
            for w in writes(c):
                w.start()
        for w in writes(nch - 1):
            w.wait()

    assert nw == V7X_SC_CORES * V7X_SC_SUBCORES and nw * per_w == ntok and ch == cfg["sc_rows"]
    return pl.kernel(
        body,
        out_type=jax.ShapeDtypeStruct((out_rows, half), U32),
        mesh=_sc_mesh(),
        scratch_types=[
            pltpu.VMEM((topk, nch, ch), jnp.int32),
            pltpu.VMEM((2, ch, half), U32),
            pltpu.SemaphoreType.DMA((2,)),
            pltpu.SemaphoreType.DMA((2,)),
        ],
        name="dispatch",
    )(hp, pos_w)


def _combine_call(ys, pos_w):
    cfg = _tiles()
    _, half = ys.shape
    nw, topk, nch, ch = pos_w.shape
    per_w = nch * ch
    ntok = nw * per_w

    def body(ys_hbm, pos_hbm, *rest):
        outs = rest[:topk]
        idx_v, buf, rsem, wsem = rest[topk:]
        wid = _sc_worker_id()
        pltpu.sync_copy(pos_hbm.at[wid], idx_v)
        items = [(c, k) for c in range(nch) for k in range(topk)]

        def read(i):
            c, k = items[i]
            return pltpu.make_async_copy(ys_hbm.at[idx_v.at[k, c]], buf.at[i % 2], rsem.at[i % 2])

        def write(i):
            c, k = items[i]
            rows = outs[k].at[pl.ds(wid * per_w + c * ch, ch)]
            return pltpu.make_async_copy(buf.at[i % 2], rows, wsem.at[i % 2])

        read(0).start()
        for i in range(len(items)):
            read(i).wait()
            if i >= 1:
                write(i - 1).wait()
            if i + 1 < len(items):
                read(i + 1).start()
            write(i).start()
        write(len(items) - 1).wait()

    assert nw == V7X_SC_CORES * V7X_SC_SUBCORES and ch == cfg["sc_rows"]
    return pl.kernel(
        body,
        out_type=[jax.ShapeDtypeStruct((ntok, half), U32)] * topk,
        mesh=_sc_mesh(),
        scratch_types=[
            pltpu.VMEM((topk, nch, ch), jnp.int32),
            pltpu.VMEM((2, ch, half), U32),
            pltpu.SemaphoreType.DMA((2,)),
            pltpu.SemaphoreType.DMA((2,)),
        ],
        name="combine",
    )(ys, pos_w)


def _ple_kernel(x1_ref, yg0_ref, yg1_ref, gate_ref, p_ref, plen_ref, wg32_ref, wu32_ref, fin_ref, o_ref,
                wg_ref, wu_ref):
    @pl.when(pl.program_id(0) == 0)
    def _():
        wg_ref[...] = wg32_ref[...].astype(BF16)
        wu_ref[...] = wu32_ref[...].astype(BF16)

    rows = x1_ref.shape[0]
    sub_rows = rows // PLE_SUBBLOCKS
    for q in range(PLE_SUBBLOCKS):
        rs = pl.ds(q * sub_rows, sub_rows)
        lo0, hi0 = _unpack_bf16_pair(yg0_ref[rs, :])
        lo1, hi1 = _unpack_bf16_pair(yg1_ref[rs, :])
        g0 = gate_ref[rs, 0:1]
        g1 = gate_ref[rs, 1:2]
        moe = g0 * jnp.concatenate([lo0, hi0], axis=1) + g1 * jnp.concatenate([lo1, hi1], axis=1)
        x2 = x1_ref[rs, :] + moe
        r = _rmsnorm(x2, plen_ref[...]).astype(BF16)
        gt = _sigmoid(_dot(r, wg_ref[...]))
        up = _dot(p_ref[rs, :].astype(BF16), wu_ref[...])
        x3 = x2 + gt * up
        o_ref[rs, :] = _rmsnorm(x3, fin_ref[...])


def _ple_call(x1, yg0, yg1, gates, p, ple_norm, wg, wu, final_norm):
    cfg = _tiles()
    ntok, d = x1.shape
    tp = cfg["ple_rows"]
    pdim = p.shape[1]
    return pl.pallas_call(
        _ple_kernel,
        grid=(ntok // tp,),
        in_specs=[
            pl.BlockSpec((tp, d), lambda i: (i, 0)),
            pl.BlockSpec((tp, d // 2), lambda i: (i, 0)),
            pl.BlockSpec((tp, d // 2), lambda i: (i, 0)),
            pl.BlockSpec((tp, V7X_SUBLANES), lambda i: (i, 0)),
            pl.BlockSpec((tp, pdim), lambda i: (i, 0)),
            _const_spec((1, d)),
            _const_spec(wg.shape),
            _const_spec(wu.shape),
            _const_spec((1, d)),
        ],
        out_specs=pl.BlockSpec((tp, d), lambda i: (i, 0)),
        out_shape=jax.ShapeDtypeStruct((ntok, d), F32),
        scratch_shapes=[pltpu.VMEM(wg.shape, BF16), pltpu.VMEM(wu.shape, BF16)],
        compiler_params=pltpu.CompilerParams(
            dimension_semantics=("arbitrary",),
            vmem_limit_bytes=cfg["ple_vmem"]),
        name="ple",
    )(x1, yg0, yg1, gates, p, ple_norm, wg, wu, final_norm)


def _blockdiag_pack(w):
    nb, bd, _ = w.shape
    per = V7X_MXU_DIM // bd
    w4 = w.reshape(nb // per, per, bd, bd)
    eye = jnp.eye(per, dtype=w.dtype)
    out = jnp.einsum("jpab,pq->jpaqb", w4, eye)
    return out.reshape(nb // per, V7X_MXU_DIM, V7X_MXU_DIM).astype(BF16)


def kernel(x, p, mix_norm, w_in, conv_w, conv_b, lru_wa, lru_ba, lru_wi, lru_bi, lru_lambda, sgu_ln_g, sgu_ln_b, sgu_ws, sgu_bs, w_out, ffn_norm, router_group_w, router_group_b, router_expert_w, router_expert_b, expert_w1, expert_w3, expert_w2, ple_norm, ple_gate_w, ple_up_w, final_norm):
    cfg = _tiles()
    bsz, seq, d = x.shape
    ntok = bsz * seq
    tm = cfg["expert_rows"]
    depth = w_in.shape[0]
    assert depth == 1, "the ple kernel applies the final norm, so it must be the last layer"
    l = 0
    b_router = jnp.concatenate([
        router_group_b[l], jnp.zeros((EXPERT_ROW0 - N_GROUPS,), F32), router_expert_b[l],
        jnp.zeros((ROUTER_ROWS - EXPERT_ROW0 - N_EXPERTS,), F32)])[:, None]
    ts = cfg["mixer_rows"]
    group = ts // V7X_SUBLANES
    bs_tile = jnp.tile(sgu_bs[l], (1, ts // CHUNK)).reshape(SGU_GROUPS, V7X_SUBLANES, group)
    bs_tile = jnp.transpose(bs_tile, (2, 1, 0)).reshape(ts, SGU_GROUPS)
    x1, hp = _mixer_call(
        x, mix_norm[l][None], w_in[l], conv_w[l], conv_b[l][None],
        _blockdiag_pack(lru_wa[l]), lru_ba[l][None], _blockdiag_pack(lru_wi[l]), lru_bi[l][None],
        lru_lambda[l][None], sgu_ln_g[l][None], sgu_ln_b[l][None], sgu_ws[l], bs_tile,
        w_out[l], ffn_norm[l][None])
    hp = hp.reshape(ntok, d // 2)
    pos, gate, cnt = _router_call(hp, router_group_w[l], router_expert_w[l], b_router)

    cap = ntok
    tiles_per_expert = (cnt[:, 0].astype(jnp.int32) + tm - 1) // tm
    nw = V7X_SC_CORES * V7X_SC_SUBCORES
    ch = cfg["sc_rows"]
    pos_w = jnp.transpose(pos[:TOP_K].reshape(TOP_K, nw, ntok // (nw * ch), ch), (1, 0, 2, 3))

    hs = _dispatch_call(hp, pos_w, N_EXPERTS * cap)
    ys = _expert_call(tiles_per_expert, hs, expert_w1[l], expert_w3[l], expert_w2[l], cap)
    yg0, yg1 = _combine_call(ys, pos_w)

    out = _ple_call(x1.reshape(ntok, d), yg0, yg1, gate, p[l].reshape(ntok, -1), ple_norm[l][None],
                    ple_gate_w[l], ple_up_w[l], final_norm[None])
    return out.reshape(bsz, seq, d)
```

```python
import functools

import jax
import jax.numpy as jnp
from jax import lax
from jax.experimental import pallas as pl
from jax.experimental.pallas import tpu as pltpu
from jax.experimental.pallas import tpu_sc as plsc

F32 = jnp.float32
BF16 = jnp.bfloat16
U32 = jnp.uint32

LRU_BLOCKS = 16
CONV_WIDTH = 4
LRU_C = 8.0
SGU_GROUPS = 8
CHUNK = 128
N_GROUPS = 4
EXPERTS_PER_GROUP = 8
N_EXPERTS = N_GROUPS * EXPERTS_PER_GROUP
TOP_K = 2
EPS = 1e-6

V7X_MXU_DIM = 256
V7X_SUBLANES = 8
V7X_LANES = 128
V7X_VMEM_BYTES = 64 * 1024 * 1024
V7X_SC_CORES = 2
V7X_SC_SUBCORES = 16

EXPERT_LOOKAHEAD = 3
PLE_SUBBLOCKS = 4
ROUTER_ROWS = V7X_LANES
EXPERT_ROW0 = V7X_SUBLANES


def _tiles():
    return dict(
        mixer_rows=256,
        expert_rows=512,
        ple_rows=1024,
        sc_rows=64,
        router_rows=2048,
        mixer_vmem=52 * 1024 * 1024,
        expert_vmem=40 * 1024 * 1024,
        ple_vmem=48 * 1024 * 1024,
        router_vmem=32 * 1024 * 1024,
    )


def _dot(a, b):
    return jnp.dot(a, b, preferred_element_type=F32)


def _sigmoid(x):
    return 0.5 * jnp.tanh(0.5 * x) + 0.5


def _rmsnorm(x, g):
    ms = jnp.mean(x * x, axis=-1, keepdims=True)
    return x * lax.rsqrt(ms + EPS) * g


def _pack_bf16_pair(lo, hi):
    lo_b = lax.bitcast_convert_type(lo.astype(BF16).astype(F32), U32)
    hi_b = lax.bitcast_convert_type(hi.astype(BF16).astype(F32), U32)
    return (hi_b & jnp.uint32(0xFFFF0000)) | lax.shift_right_logical(lo_b, jnp.uint32(16))


def _unpack_bf16_pair(w):
    lo = lax.bitcast_convert_type(lax.shift_left(w, jnp.uint32(16)), F32)
    hi = lax.bitcast_convert_type(w & jnp.uint32(0xFFFF0000), F32)
    return lo, hi


def _const_spec(shape):
    zeros = (0,) * len(shape)
    return pl.BlockSpec(shape, lambda *_: zeros, pipeline_mode=pl.Buffered(1))


def _tile_copies(hbm, buf, sem, b, row0, slot, to_hbm):
    group = buf.shape[1]
    copies = []
    for r in range(V7X_SUBLANES):
        hbm_rows = hbm.at[b, pl.ds(row0 + group * r, group), :]
        vmem_rows = buf.at[slot, :, r, :]
        src, dst = (vmem_rows, hbm_rows) if to_hbm else (hbm_rows, vmem_rows)
        copies.append(pltpu.make_async_copy(src, dst, sem.at[slot]))
    return copies


def _lru_scan(a, u, h0):
    group = a.shape[0]
    acc_a = [a[0]]
    acc_u = [u[0]]
    for g in range(1, group):
        acc_a.append(a[g] * acc_a[-1])
        acc_u.append(a[g] * acc_u[-1] + u[g])
    end_a, end_u = acc_a[-1], acc_u[-1]
    sub = lax.broadcasted_iota(jnp.int32, end_a.shape, 0)
    shift = 1
    while shift < V7X_SUBLANES:
        keep = sub >= shift
        a_sh = pltpu.roll(end_a, shift, axis=0)
        u_sh = pltpu.roll(end_u, shift, axis=0)
        end_u = jnp.where(keep, end_a * u_sh + end_u, end_u)
        end_a = jnp.where(keep, end_a * a_sh, end_a)
        shift *= 2
    h_end = end_a * h0 + end_u
    h_in = jnp.where(sub == 0, h0, pltpu.roll(h_end, 1, axis=0))
    out = [acc_a[g] * h_in + acc_u[g] for g in range(group)]
    return jnp.stack(out, axis=0), h_end[V7X_SUBLANES - 1:V7X_SUBLANES, :]


def _mixer_kernel(x_hbm, mixn_ref, win_hbm, convw_ref, convb_ref, wa_ref, ba_ref, wi_ref, bi_ref,
                  lam_ref, lng_ref, lnb_ref, ws_ref, bsp_ref, wout_hbm, ffn_ref,
                  x1_hbm, hp_hbm,
                  xbuf, z0_ref, z1_ref, x1buf, hpbuf, xsem, x1sem, hpsem, wsm_ref, ztail_ref, hcar_ref,
                  win_ref, wout_ref, wsem,
                  *, nseq):
    j = pl.program_id(0)
    ntile = pl.num_programs(0) - 1
    _, group, _, d = xbuf.shape
    rows = group * V7X_SUBLANES
    half = d // 2
    ta = jnp.minimum(j, ntile - 1)
    tb = jnp.maximum(j - 1, 0)
    s = lax.rem(tb, nseq)
    slot = lax.rem(tb, 2)

    def fetch(t):
        return _tile_copies(x_hbm, xbuf, xsem, lax.div(t, nseq), lax.rem(t, nseq) * rows,
                            lax.rem(t, 3), to_hbm=False)

    def put(t):
        tb_, ts_, sl = lax.div(t, nseq), lax.rem(t, nseq) * rows, lax.rem(t, 2)
        return (_tile_copies(x1_hbm, x1buf, x1sem, tb_, ts_, sl, to_hbm=True)
                + _tile_copies(hp_hbm, hpbuf, hpsem, tb_, ts_, sl, to_hbm=True))

    @pl.when(j == 0)
    def _():
        for c in fetch(0):
            c.start()
        stage = (z0_ref, z1_ref)
        n_in_chunks = win_hbm.shape[0] // rows

        def win_copy(c):
            return pltpu.make_async_copy(win_hbm.at[pl.ds(c * rows, rows), :], stage[c % 2], wsem.at[c % 2])

        win_copy(0).start()
        for c in range(n_in_chunks):
            if c + 1 < n_in_chunks:
                win_copy(c + 1).start()
            win_copy(c).wait()
            plain = 4 * d
            win_ref[c * rows:(c + 1) * rows, :plain] = stage[c % 2][:, :plain].astype(BF16)
            win_ref[c * rows:(c + 1) * rows, plain:] = (0.5 * stage[c % 2][:, plain:]).astype(BF16)
        n_out_chunks = wout_hbm.shape[0] // rows
        out_copies = [pltpu.make_async_copy(wout_hbm.at[pl.ds(c * rows, rows), :],
                                            z0_ref.at[:, c * d:(c + 1) * d], wsem.at[0])
                      for c in range(n_out_chunks)]
        for cp in out_copies:
            cp.start()
        for cp in out_copies:
            cp.wait()
        for c in range(n_out_chunks):
            wout_ref[c * rows:(c + 1) * rows, :] = (0.25 * z0_ref[:, c * d:(c + 1) * d]).astype(BF16)
        z1_ref[...] = jnp.zeros_like(z1_ref)
        i_idx = lax.broadcasted_iota(jnp.int32, (rows, rows), 0)
        j_idx = lax.broadcasted_iota(jnp.int32, (rows, rows), 1)
        t_i = group * lax.rem(i_idx, V7X_SUBLANES) + lax.div(i_idx, V7X_SUBLANES)
        t_j = group * lax.rem(j_idx, V7X_SUBLANES) + lax.div(j_idx, V7X_SUBLANES)
        keep = (t_i >= t_j) & (lax.div(t_i, CHUNK) == lax.div(t_j, CHUNK))
        pick_rows = jnp.where(t_i == j_idx, 1.0, 0.0).astype(BF16)
        pick_cols = jnp.where(i_idx == t_j, 1.0, 0.0).astype(BF16)
        reps = rows // CHUNK
        for g in range(SGU_GROUPS):
            w_chunk = ws_ref[g].astype(BF16)
            w_rows = jnp.concatenate([w_chunk] * reps, axis=1)
            w_full = jnp.concatenate([w_rows] * reps, axis=0)
            w_perm = _dot(_dot(pick_rows, w_full).astype(BF16), pick_cols)
            wsm_ref[g] = jnp.where(keep, w_perm, 0.0).astype(BF16)

    @pl.when(j + 1 < ntile)
    def _():
        for c in fetch(j + 1):
            c.start()

    @pl.when(j < ntile)
    def _():
        for c in fetch(j):
            c.wait()

    @pl.when(s == 0)
    def _():
        ztail_ref[...] = jnp.zeros_like(ztail_ref)
        hcar_ref[...] = jnp.zeros_like(hcar_ref)

    def compute(z_w, z_r):
        xa_in = xbuf[lax.rem(ta, 3)].reshape(rows, d)
        h_next = _rmsnorm(xa_in, mixn_ref[...]).astype(BF16)
        pw = d // 2

        def project(k):
            z_w[:, k * pw:(k + 1) * pw] = _dot(h_next, win_ref[:, k * pw:(k + 1) * pw])

        x = xbuf[lax.rem(tb, 3)].reshape(rows, d)

        def sec(k, c0, c1):
            return z_r[:, k * d + c0:k * d + c1]

        def one_plus_tanh_gelu(v):
            c = 0.7978845608028654
            return 1.0 + jnp.tanh(v * (c + (c * 0.044715) * (v * v)))

        cw = 0.5 * convw_ref[...]
        cb_h = 0.5 * convb_ref[...]
        ba_h = 0.5 * ba_ref[...]
        bi_h = 0.5 * bi_ref[...]
        neg_lam = -lam_ref[...]
        softplus = jnp.maximum(neg_lam, 0.0) + jnp.log1p(jnp.exp(-jnp.abs(neg_lam)))
        c_a = (-0.5 * LRU_C) * softplus
        blk = V7X_MXU_DIM
        sub3 = lax.broadcasted_iota(jnp.int32, (CONV_WIDTH - 1, V7X_SUBLANES, blk), 1)
        term_a = []
        for n in range(d // blk):
            project(n)
            c0, c1 = n * blk, (n + 1) * blk
            z3 = sec(0, c0, c1).reshape(group, V7X_SUBLANES, blk)
            tail = z3[group - (CONV_WIDTH - 1):]
            halo = jnp.where(sub3 == 0, pltpu.roll(ztail_ref[:, :, c0:c1], 1, axis=1),
                             pltpu.roll(tail, 1, axis=1))
            ztail_ref[:, :, c0:c1] = tail
            zext = jnp.concatenate([halo, z3], axis=0)
            xa_h = cb_h[:, c0:c1] + cw[CONV_WIDTH - 1:CONV_WIDTH, c0:c1] * z3
            for k in range(1, CONV_WIDTH):
                lo = CONV_WIDTH - 1 - k
                xa_h = xa_h + cw[lo:lo + 1, c0:c1] * zext[lo:lo + group]
            xa2 = xa_h.reshape(rows, blk)
            xa_bf = xa2.astype(BF16)
            th_r = jnp.tanh(_dot(xa_bf, wa_ref[n]) + ba_h[:, c0:c1])
            th_i = jnp.tanh(_dot(xa_bf, wi_ref[n]) + bi_h[:, c0:c1])
            a = jnp.exp(c_a[:, c0:c1] + c_a[:, c0:c1] * th_r)
            u = jnp.sqrt(1.0 - a * a) * ((1.0 + th_i) * xa2)
            hseq, hlast = _lru_scan(a.reshape(group, V7X_SUBLANES, blk),
                                    u.reshape(group, V7X_SUBLANES, blk), hcar_ref[:, c0:c1])
            hcar_ref[:, c0:c1] = hlast
            zg = sec(1, c0, c1)
            term_a.append(((1.0 + jnp.tanh(sec(4, c0, c1))) * one_plus_tanh_gelu(zg))
                          * (zg * hseq.reshape(rows, blk)))

        project(4)
        zv = sec(3, 0, d)
        gv2 = zv * one_plus_tanh_gelu(zv)
        project(5)
        mu = jnp.mean(gv2, axis=-1, keepdims=True)
        xc = gv2 - mu
        var = jnp.mean(xc * xc, axis=-1, keepdims=True)
        v_bf = (xc * lax.rsqrt(var + 4.0 * EPS) * lng_ref[...] + lnb_ref[...]).astype(BF16)
        project(6)
        gdim = d // SGU_GROUPS
        term_b = []
        for g in range(SGU_GROUPS):
            c0, c1 = g * gdim, (g + 1) * gdim
            if g in (1, 3, 5):
                project({1: 7, 3: 8, 5: 9}[g])
            sp = _dot(wsm_ref[g], v_bf[:, c0:c1]) + bsp_ref[:, g:g + 1]
            zu = sec(2, c0, c1)
            term_b.append(((1.0 + jnp.tanh(sec(5, c0, c1))) * one_plus_tanh_gelu(zu)) * (zu * sp))
        merged4 = jnp.concatenate(term_a, axis=1) + jnp.concatenate(term_b, axis=1)

        x1 = x + _dot(merged4.astype(BF16), wout_ref[...])
        project(10)
        project(11)

        hn = _rmsnorm(x1, ffn_ref[...])
        hp = _pack_bf16_pair(hn[:, :half], hn[:, half:])

        @pl.when(j >= 3)
        def _():
            for c in put(tb - 2):
                c.wait()

        x1buf[slot] = x1.reshape(group, V7X_SUBLANES, d)
        hpbuf[slot] = hp.reshape(group, V7X_SUBLANES, half)

        @pl.when(j >= 1)
        def _():
            for c in put(tb):
                c.start()

    @pl.when(lax.rem(j, 2) == 0)
    def _():
        compute(z0_ref, z1_ref)

    @pl.when(lax.rem(j, 2) == 1)
    def _():
        compute(z1_ref, z0_ref)

    @pl.when(j == ntile)
    def _():
        for c in put(tb):
            c.wait()

        @pl.when(ntile >= 2)
        def _():
            for c in put(tb - 1):
                c.wait()


def _mixer_call(x, mix_norm, w_in, conv_w, conv_b, wa_blk, ba, wi_blk, bi, lam, ln_g, ln_b, ws,
                bs_tile, w_out, ffn_norm):
    cfg = _tiles()
    bsz, seq, d = x.shape
    ts = cfg["mixer_rows"]
    group = ts // V7X_SUBLANES
    nseq = seq // ts
    ntile = bsz * nseq
    row1 = (1, d)
    in_specs = [
        pl.BlockSpec(memory_space=pl.ANY),
        _const_spec(row1),
        pl.BlockSpec(memory_space=pl.ANY),
        _const_spec(conv_w.shape), _const_spec(row1),
        _const_spec(wa_blk.shape), _const_spec(row1),
        _const_spec(wi_blk.shape), _const_spec(row1),
        _const_spec(row1),
        _const_spec(row1), _const_spec(row1),
        _const_spec(ws.shape), _const_spec(bs_tile.shape),
        pl.BlockSpec(memory_space=pl.ANY), _const_spec(row1),
    ]
    out_shape = [
        jax.ShapeDtypeStruct((bsz, seq, d), F32),
        jax.ShapeDtypeStruct((bsz, seq, d // 2), U32),
    ]
    out_specs = [
        pl.BlockSpec(memory_space=pl.ANY),
        pl.BlockSpec(memory_space=pl.ANY),
    ]
    scratch = [
        pltpu.VMEM((3, group, V7X_SUBLANES, d), F32),
        pltpu.VMEM((ts, w_in.shape[1]), F32),
        pltpu.VMEM((ts, w_in.shape[1]), F32),
        pltpu.VMEM((2, group, V7X_SUBLANES, d), F32),
        pltpu.VMEM((2, group, V7X_SUBLANES, d // 2), U32),
        pltpu.SemaphoreType.DMA((3,)),
        pltpu.SemaphoreType.DMA((2,)),
        pltpu.SemaphoreType.DMA((2,)),
        pltpu.VMEM((SGU_GROUPS, ts, ts), BF16),
        pltpu.VMEM((CONV_WIDTH - 1, V7X_SUBLANES, d), F32),
        pltpu.VMEM((1, d), F32),
        pltpu.VMEM(w_in.shape, BF16),
        pltpu.VMEM(w_out.shape, BF16),
        pltpu.SemaphoreType.DMA((2,)),
    ]
    return pl.pallas_call(
        functools.partial(_mixer_kernel, nseq=nseq),
        grid=(ntile + 1,),
        in_specs=in_specs,
        out_specs=out_specs,
        out_shape=out_shape,
        scratch_shapes=scratch,
        compiler_params=pltpu.CompilerParams(
            dimension_semantics=("arbitrary",),
            vmem_limit_bytes=cfg["mixer_vmem"]),
        name="mixer",
    )(x, mix_norm, w_in, conv_w, conv_b, wa_blk, ba, wi_blk, bi, lam, ln_g, ln_b, ws, bs_tile,
      w_out, ffn_norm)


def _router_kernel(hp_ref, wg_ref, we_ref, br_ref, pos_ref, gate_ref, cnt_ref, ccar_ref, wr_ref,
                   *, expert_capacity):
    rows = hp_ref.shape[0]

    @pl.when(pl.program_id(0) == 0)
    def _():
        ccar_ref[...] = jnp.zeros_like(ccar_ref)
        wr_ref[...] = jnp.zeros_like(wr_ref)
        wr_ref[:, 0:N_GROUPS] = wg_ref[...].astype(BF16)
        wr_ref[:, EXPERT_ROW0:EXPERT_ROW0 + N_EXPERTS] = we_ref[...].astype(BF16)

    lo, hi = _unpack_bf16_pair(hp_ref[...])
    hn = jnp.concatenate([lo, hi], axis=1)
    logits = _dot(hn.astype(BF16), wr_ref[...])
    lt = jnp.transpose(logits) + br_ref[...]
    sub = lax.broadcasted_iota(jnp.int32, (V7X_SUBLANES, rows), 0)
    subf = sub.astype(F32)
    big = jnp.float32(1e9)

    lg = jnp.where(sub < N_GROUPS, lt[0:V7X_SUBLANES, :], -jnp.inf)
    g_exp = jnp.exp(lg - jnp.max(lg, axis=0, keepdims=True))
    g_prob = g_exp / jnp.sum(g_exp, axis=0, keepdims=True)
    g_top = jnp.max(g_prob, axis=0, keepdims=True)
    g_idx = jnp.min(jnp.where(g_prob == g_top, subf, big), axis=0, keepdims=True)

    e_sel = jnp.zeros((EXPERTS_PER_GROUP, rows), F32)
    for g in range(N_GROUPS):
        r0 = EXPERT_ROW0 + g * EXPERTS_PER_GROUP
        e_sel = jnp.where(g_idx == g, lt[r0:r0 + EXPERTS_PER_GROUP, :], e_sel)
    e_exp = jnp.exp(e_sel - jnp.max(e_sel, axis=0, keepdims=True))
    e_prob = e_exp / jnp.sum(e_exp, axis=0, keepdims=True)
    p1 = jnp.max(e_prob, axis=0, keepdims=True)
    i1 = jnp.min(jnp.where(e_prob == p1, subf, big), axis=0, keepdims=True)
    rest = jnp.where(subf == i1, -1.0, e_prob)
    p2 = jnp.max(rest, axis=0, keepdims=True)
    i2 = jnp.min(jnp.where(rest == p2, subf, big), axis=0, keepdims=True)
    psum = p1 + p2
    gate1 = g_top * (p1 / psum)
    gate2 = g_top * (p2 / psum)
    gid1 = g_idx * EXPERTS_PER_GROUP + i1
    gid2 = g_idx * EXPERTS_PER_GROUP + i2

    eid = lax.broadcasted_iota(jnp.int32, (N_EXPERTS, rows), 0).astype(F32)
    hit1 = eid == gid1
    hit2 = eid == gid2
    cnt = jnp.where(hit1 | hit2, 1.0, 0.0)
    sb = V7X_MXU_DIM
    before = (lax.broadcasted_iota(jnp.int32, (sb, sb), 0)
              < lax.broadcasted_iota(jnp.int32, (sb, sb), 1))
    before = jnp.where(before, 1.0, 0.0).astype(BF16)
    running = ccar_ref[:, 0:1]
    base = []
    for q in range(rows // sb):
        part = cnt[:, q * sb:(q + 1) * sb]
        base.append(running + _dot(part.astype(BF16), before))
        running = running + jnp.sum(part, axis=1, keepdims=True)
    base = jnp.concatenate(base, axis=1)
    rank1 = jnp.sum(jnp.where(hit1, base, 0.0), axis=0, keepdims=True)
    rank2 = jnp.sum(jnp.where(hit2, base, 0.0), axis=0, keepdims=True)
    total = jnp.broadcast_to(running, ccar_ref.shape)
    ccar_ref[...] = total
    cnt_ref[...] = total
    cap = float(expert_capacity)
    zero = jnp.zeros((V7X_SUBLANES - TOP_K, rows), F32)
    pos = jnp.concatenate([gid1 * cap + rank1, gid2 * cap + rank2, zero], axis=0)
    pos_ref[...] = pos.astype(jnp.int32)
    gate_ref[...] = jnp.transpose(jnp.concatenate([gate1, gate2, zero], axis=0))


def _router_call(hp, w_group, w_expert, b_router):
    cfg = _tiles()
    ntok, half = hp.shape
    tr = cfg["router_rows"]
    return pl.pallas_call(
        functools.partial(_router_kernel, expert_capacity=ntok),
        grid=(ntok // tr,),
        in_specs=[
            pl.BlockSpec((tr, half), lambda i: (i, 0)),
            _const_spec(w_group.shape),
            _const_spec(w_expert.shape),
            _const_spec(b_router.shape),
        ],
        out_specs=[
            pl.BlockSpec((V7X_SUBLANES, tr), lambda i: (0, i)),
            pl.BlockSpec((tr, V7X_SUBLANES), lambda i: (i, 0)),
            pl.BlockSpec((N_EXPERTS, V7X_LANES), lambda i: (0, 0)),
        ],
        out_shape=[
            jax.ShapeDtypeStruct((V7X_SUBLANES, ntok), jnp.int32),
            jax.ShapeDtypeStruct((ntok, V7X_SUBLANES), F32),
            jax.ShapeDtypeStruct((N_EXPERTS, V7X_LANES), F32),
        ],
        scratch_shapes=[
            pltpu.VMEM((N_EXPERTS, V7X_LANES), F32),
            pltpu.VMEM((w_group.shape[0], ROUTER_ROWS), BF16),
        ],
        compiler_params=pltpu.CompilerParams(
            dimension_semantics=("arbitrary",),
            vmem_limit_bytes=cfg["router_vmem"]),
        name="router",
    )(hp, w_group, w_expert, b_router)


def _expert_kernel(nt_ref, base_ref, texp_ref, tloc_ref, hs_hbm, w1_ref, w3_ref, w2_ref, ys_hbm,
                   hbuf, ybuf, hsem, ysem, w1b_ref, w3b_ref, w2b_ref, *, capacity):
    e = pl.program_id(0)
    n_exp = pl.num_programs(0)
    nt = nt_ref[e]
    base = base_ref[e]
    total = base_ref[n_exp - 1] + nt_ref[n_exp - 1]
    n_in, tm, _ = hbuf.shape
    n_out = ybuf.shape[0]
    ahead = n_in - 1

    def load(g):
        slot = lax.rem(g, n_in)
        rows = pl.ds(texp_ref[g] * capacity + tloc_ref[g] * tm, tm)
        return pltpu.make_async_copy(hs_hbm.at[rows], hbuf.at[slot], hsem.at[slot])

    def store(t, slot):
        rows = pl.ds(e * capacity + t * tm, tm)
        return pltpu.make_async_copy(ybuf.at[slot], ys_hbm.at[rows], ysem.at[slot])

    @pl.when(e == 0)
    def _():
        for g0 in range(ahead):
            @pl.when(g0 < total)
            def _():
                load(g0).start()

    w1b_ref[...] = w1_ref[...].astype(BF16)
    w3b_ref[...] = w3_ref[...].astype(BF16)
    w2b_ref[...] = w2_ref[...].astype(BF16)

    @pl.loop(0, nt)
    def _(t):
        g = base + t

        @pl.when(g + ahead < total)
        def _():
            load(g + ahead).start()

        load(g).wait()
        lo, hi = _unpack_bf16_pair(hbuf[lax.rem(g, n_in)])
        h = jnp.concatenate([lo, hi], axis=1).astype(BF16)
        a = _dot(h, w1b_ref[...])
        b = _dot(h, w3b_ref[...])
        hid = (a * _sigmoid(a)) * b
        y = _dot(hid.astype(BF16), w2b_ref[...])
        half = y.shape[1] // 2
        slot = lax.rem(g, n_out)

        @pl.when(g >= n_out)
        def _():
            store(t, slot).wait()

        ybuf[slot] = _pack_bf16_pair(y[:, :half], y[:, half:])
        store(t, slot).start()

    @pl.when(e + 1 == n_exp)
    def _():
        for back in range(1, n_out + 1):
            @pl.when(total >= back)
            def _():
                store(0, lax.rem(total - back, n_out)).wait()


def _expert_call(tiles_per_expert, hs, w1, w3, w2, capacity):
    cfg = _tiles()
    tm = cfg["expert_rows"]
    prow, half = hs.shape
    n_exp, d, f = w1.shape
    ends = jnp.cumsum(tiles_per_expert)
    base = ends - tiles_per_expert
    g = jnp.arange(capacity * TOP_K // tm + n_exp, dtype=jnp.int32)
    texp = jnp.minimum(jnp.sum((ends[None, :] <= g[:, None]).astype(jnp.int32), axis=1), n_exp - 1)
    onehot = texp[:, None] == jnp.arange(n_exp, dtype=jnp.int32)[None, :]
    tloc = g - jnp.sum(jnp.where(onehot, base[None, :], 0), axis=1)

    def w_map(e, *_):
        return (e, 0, 0)

    grid_spec = pltpu.PrefetchScalarGridSpec(
        num_scalar_prefetch=4,
        grid=(n_exp,),
        in_specs=[
            pl.BlockSpec(memory_space=pl.ANY),
            pl.BlockSpec((None, d, f), w_map),
            pl.BlockSpec((None, d, f), w_map),
            pl.BlockSpec((None, f, d), w_map),
        ],
        out_specs=pl.BlockSpec(memory_space=pl.ANY),
        scratch_shapes=[
            pltpu.VMEM((EXPERT_LOOKAHEAD + 1, tm, half), U32),
            pltpu.VMEM((2, tm, half), U32),
            pltpu.SemaphoreType.DMA((EXPERT_LOOKAHEAD + 1,)),
            pltpu.SemaphoreType.DMA((2,)),
            pltpu.VMEM((d, f), BF16),
            pltpu.VMEM((d, f), BF16),
            pltpu.VMEM((f, d), BF16),
        ],
    )
    return pl.pallas_call(
        functools.partial(_expert_kernel, capacity=capacity),
        grid_spec=grid_spec,
        out_shape=jax.ShapeDtypeStruct((prow, half), U32),
        compiler_params=pltpu.CompilerParams(
            dimension_semantics=("arbitrary",),
            vmem_limit_bytes=cfg["expert_vmem"]),
        name="experts",
    )(tiles_per_expert, base, texp, tloc, hs, w1, w3, w2)


def _sc_mesh():
    return plsc.VectorSubcoreMesh(core_axis_name="c", subcore_axis_name="s",
                                  num_cores=V7X_SC_CORES, num_subcores=V7X_SC_SUBCORES)


def _sc_worker_id():
    return lax.axis_index("s") * V7X_SC_CORES + lax.axis_index("c")


def _dispatch_call(hp, pos_w, out_rows):
    cfg = _tiles()
    ntok, half = hp.shape
    nw, topk, nch, ch = pos_w.shape
    per_w = nch * ch

    def body(hp_hbm, pos_hbm, hs_hbm, idx_v, buf, rsem, wsem):
        wid = _sc_worker_id()
        pltpu.sync_copy(pos_hbm.at[wid], idx_v)

        def read(c):
            rows = hp_hbm.at[pl.ds(wid * per_w + c * ch, ch)]
            return pltpu.make_async_copy(rows, buf.at[c % 2], rsem.at[c % 2])

        def writes(c):
            return [pltpu.make_async_copy(buf.at[c % 2], hs_hbm.at[idx_v.at[k, c]], wsem.at[c % 2])
                    for k in range(topk)]

        read(0).start()
        for c in range(nch):
            read(c).wait()
            if c >= 1:
                for w in writes(c - 1):
                    w.wait()
            if c + 1 < nch:
                read(c + 1).start()
            for w in writes(c):
                w.start()
        for w in writes(nch - 1):
            w.wait()

    assert nw == V7X_SC_CORES * V7X_SC_SUBCORES and nw * per_w == ntok and ch == cfg["sc_rows"]
    return pl.kernel(
        body,
        out_type=jax.ShapeDtypeStruct((out_rows, half), U32),
        mesh=_sc_mesh(),
        scratch_types=[
            pltpu.VMEM((topk, nch, ch), jnp.int32),
            pltpu.VMEM((2, ch, half), U32),
            pltpu.SemaphoreType.DMA((2,)),
            pltpu.SemaphoreType.DMA((2,)),
        ],
        name="dispatch",
    )(hp, pos_w)


def _combine_call(ys, pos_w):
    cfg = _tiles()
    _, half = ys.shape
    nw, topk, nch, ch = pos_w.shape
    per_w = nch * ch
    ntok = nw * per_w

    def body(ys_hbm, pos_hbm, *rest):
        outs = rest[:topk]
        idx_v, buf, rsem, wsem = rest[topk:]
        wid = _sc_worker_id()
        pltpu.sync_copy(pos_hbm.at[wid], idx_v)
        items = [(c, k) for c in range(nch) for k in range(topk)]

        def read(i):
            c, k = items[i]
            return pltpu.make_async_copy(ys_hbm.at[idx_v.at[k, c]], buf.at[i % 2], rsem.at[i % 2])

        def write(i):
            c, k = items[i]
            rows = outs[k].at[pl.ds(wid * per_w + c * ch, ch)]
            return pltpu.make_async_copy(buf.at[i % 2], rows, wsem.at[i % 2])

        read(0).start()
        for i in range(len(items)):
            read(i).wait()
            if i >= 1:
                write(i - 1).wait()
            if i + 1 < len(items):
                read(i + 1).start()
            write(i).start()
        write(len(items) - 1).wait()

    assert nw == V7X_SC_CORES * V7X_SC_SUBCORES and ch == cfg["sc_rows"]
    return pl.kernel(
        body,
        out_type=[jax.ShapeDtypeStruct((ntok, half), U32)] * topk,
        mesh=_sc_mesh(),
        scratch_types=[
            pltpu.VMEM((topk, nch, ch), jnp.int32),
            pltpu.VMEM((2, ch, half), U32),
            pltpu.SemaphoreType.DMA((2,)),
            pltpu.SemaphoreType.DMA((2,)),
        ],
        name="combine",
    )(ys, pos_w)


def _ple_kernel(x1_ref, yg0_ref, yg1_ref, gate_ref, p_ref, plen_ref, wg32_ref, wu32_ref, fin_ref, o_ref,
                wg_ref, wu_ref):
    @pl.when(pl.program_id(0) == 0)
    def _():
        wg_ref[...] = wg32_ref[...].astype(BF16)
        wu_ref[...] = wu32_ref[...].astype(BF16)

    rows = x1_ref.shape[0]
    sub_rows = rows // PLE_SUBBLOCKS
    for q in range(PLE_SUBBLOCKS):
        rs = pl.ds(q * sub_rows, sub_rows)
        lo0, hi0 = _unpack_bf16_pair(yg0_ref[rs, :])
        lo1, hi1 = _unpack_bf16_pair(yg1_ref[rs, :])
        g0 = gate_ref[rs, 0:1]
        g1 = gate_ref[rs, 1:2]
        moe = g0 * jnp.concatenate([lo0, hi0], axis=1) + g1 * jnp.concatenate([lo1, hi1], axis=1)
        x2 = x1_ref[rs, :] + moe
        r = _rmsnorm(x2, plen_ref[...]).astype(BF16)
        gt = _sigmoid(_dot(r, wg_ref[...]))
        up = _dot(p_ref[rs, :].astype(BF16), wu_ref[...])
        x3 = x2 + gt * up
        o_ref[rs, :] = _rmsnorm(x3, fin_ref[...])


def _ple_call(x1, yg0, yg1, gates, p, ple_norm, wg, wu, final_norm):
    cfg = _tiles()
    ntok, d = x1.shape
    tp = cfg["ple_rows"]
    pdim = p.shape[1]
    return pl.pallas_call(
        _ple_kernel,
        grid=(ntok // tp,),
        in_specs=[
            pl.BlockSpec((tp, d), lambda i: (i, 0)),
            pl.BlockSpec((tp, d // 2), lambda i: (i, 0)),
            pl.BlockSpec((tp, d // 2), lambda i: (i, 0)),
            pl.BlockSpec((tp, V7X_SUBLANES), lambda i: (i, 0)),
            pl.BlockSpec((tp, pdim), lambda i: (i, 0)),
            _const_spec((1, d)),
            _const_spec(wg.shape),
            _const_spec(wu.shape),
            _const_spec((1, d)),
        ],
        out_specs=pl.BlockSpec((tp, d), lambda i: (i, 0)),
        out_shape=jax.ShapeDtypeStruct((ntok, d), F32),
        scratch_shapes=[pltpu.VMEM(wg.shape, BF16), pltpu.VMEM(wu.shape, BF16)],
        compiler_params=pltpu.CompilerParams(
            dimension_semantics=("arbitrary",),
            vmem_limit_bytes=cfg["ple_vmem"]),
        name="ple",
    )(x1, yg0, yg1, gates, p, ple_norm, wg, wu, final_norm)


def _blockdiag_pack(w):
    nb, bd, _ = w.shape
    per = V7X_MXU_DIM // bd
    w4 = w.reshape(nb // per, per, bd, bd)
    eye = jnp.eye(per, dtype=w.dtype)
    out = jnp.einsum("jpab,pq->jpaqb", w4, eye)
    return out.reshape(nb // per, V7X_MXU_DIM, V7X_MXU_DIM).astype(BF16)


def kernel(x, p, mix_norm, w_in, conv_w, conv_b, lru_wa, lru_ba, lru_wi, lru_bi, lru_lambda, sgu_ln_g, sgu_ln_b, sgu_ws, sgu_bs, w_out, ffn_norm, router_group_w, router_group_b, router_expert_w, router_expert_b, expert_w1, expert_w3, expert_w2, ple_norm, ple_gate_w, ple_up_w, final_norm):
    cfg = _tiles()
    bsz, seq, d = x.shape
    ntok = bsz * seq
    tm = cfg["expert_rows"]
    depth = w_in.shape[0]
    assert depth == 1, "the ple kernel applies the final norm, so it must be the last layer"
    l = 0
    b_router = jnp.concatenate([
        router_group_b[l], jnp.zeros((EXPERT_ROW0 - N_GROUPS,), F32), router_expert_b[l],
        jnp.zeros((ROUTER_ROWS - EXPERT_ROW0 - N_EXPERTS,), F32)])[:, None]
    ts = cfg["mixer_rows"]
    group = ts // V7X_SUBLANES
    bs_tile = jnp.tile(sgu_bs[l], (1, ts // CHUNK)).reshape(SGU_GROUPS, V7X_SUBLANES, group)
    bs_tile = jnp.transpose(bs_tile, (2, 1, 0)).reshape(ts, SGU_GROUPS)
    x1, hp = _mixer_call(
        x, mix_norm[l][None], w_in[l], conv_w[l], conv_b[l][None],
        _blockdiag_pack(lru_wa[l]), lru_ba[l][None], _blockdiag_pack(lru_wi[l]), lru_bi[l][None],
        lru_lambda[l][None], sgu_ln_g[l][None], sgu_ln_b[l][None], sgu_ws[l], bs_tile,
        w_out[l], ffn_norm[l][None])
    hp = hp.reshape(ntok, d // 2)
    pos, gate, cnt = _router_call(hp, router_group_w[l], router_expert_w[l], b_router)

    cap = ntok
    tiles_per_expert = (cnt[:, 0].astype(jnp.int32) + tm - 1) // tm
    nw = V7X_SC_CORES * V7X_SC_SUBCORES
    ch = cfg["sc_rows"]
    pos_w = jnp.transpose(pos[:TOP_K].reshape(TOP_K, nw, ntok // (nw * ch), ch), (1, 0, 2, 3))

    hs = _dispatch_call(hp, pos_w, N_EXPERTS * cap)
    ys = _expert_call(tiles_per_expert, hs, expert_w1[l], expert_w3[l], expert_w2[l], cap)
    yg0, yg1 = _combine_call(ys, pos_w)

    out = _ple_call(x1.reshape(ntok, d), yg0, yg1, gate, p[l].reshape(ntok, -1), ple_norm[l][None],
                    ple_gate_w[l], ple_up_w[l], final_norm[None])
    return out.reshape(bsz, seq, d)
```

```python
import functools

import jax
import jax.numpy as jnp
from jax import lax
from jax.experimental import pallas as pl
from jax.experimental.pallas import tpu as pltpu
from jax.experimental.pallas import tpu_sc as plsc

F32 = jnp.float32
BF16 = jnp.bfloat16
U32 = jnp.uint32

LRU_BLOCKS = 16
CONV_WIDTH = 4
LRU_C = 8.0
SGU_GROUPS = 8
CHUNK = 128
N_GROUPS = 4
EXPERTS_PER_GROUP = 8
N_EXPERTS = N_GROUPS * EXPERTS_PER_GROUP
TOP_K = 2
EPS = 1e-6

V7X_MXU_DIM = 256
V7X_SUBLANES = 8
V7X_LANES = 128
V7X_VMEM_BYTES = 64 * 1024 * 1024
V7X_SC_CORES = 2
V7X_SC_SUBCORES = 16

EXPERT_LOOKAHEAD = 3
PLE_SUBBLOCKS = 4
ROUTER_ROWS = V7X_LANES
EXPERT_ROW0 = V7X_SUBLANES


def _tiles():
    return dict(
        mixer_rows=256,
        expert_rows=512,
        ple_rows=1024,
        sc_rows=128,
        router_rows=2048,
        mixer_vmem=52 * 1024 * 1024,
        expert_vmem=40 * 1024 * 1024,
        ple_vmem=48 * 1024 * 1024,
        router_vmem=32 * 1024 * 1024,
    )


def _dot(a, b):
    return jnp.dot(a, b, preferred_element_type=F32)


def _sigmoid(x):
    return 0.5 * jnp.tanh(0.5 * x) + 0.5


def _rmsnorm(x, g):
    ms = jnp.mean(x * x, axis=-1, keepdims=True)
    return x * lax.rsqrt(ms + EPS) * g


def _pack_bf16_pair(lo, hi):
    lo_b = lax.bitcast_convert_type(lo.astype(BF16).astype(F32), U32)
    hi_b = lax.bitcast_convert_type(hi.astype(BF16).astype(F32), U32)
    return (hi_b & jnp.uint32(0xFFFF0000)) | lax.shift_right_logical(lo_b, jnp.uint32(16))


def _unpack_bf16_pair(w):
    lo = lax.bitcast_convert_type(lax.shift_left(w, jnp.uint32(16)), F32)
    hi = lax.bitcast_convert_type(w & jnp.uint32(0xFFFF0000), F32)
    return lo, hi


def _const_spec(shape):
    zeros = (0,) * len(shape)
    return pl.BlockSpec(shape, lambda *_: zeros, pipeline_mode=pl.Buffered(1))


def _tile_copies(hbm, buf, sem, b, row0, slot, to_hbm):
    group = buf.shape[1]
    copies = []
    for r in range(V7X_SUBLANES):
        hbm_rows = hbm.at[b, pl.ds(row0 + group * r, group), :]
        vmem_rows = buf.at[slot, :, r, :]
        src, dst = (vmem_rows, hbm_rows) if to_hbm else (hbm_rows, vmem_rows)
        copies.append(pltpu.make_async_copy(src, dst, sem.at[slot]))
    return copies


def _lru_scan(a, u, h0):
    group = a.shape[0]
    acc_a = [a[0]]
    acc_u = [u[0]]
    for g in range(1, group):
        acc_a.append(a[g] * acc_a[-1])
        acc_u.append(a[g] * acc_u[-1] + u[g])
    end_a, end_u = acc_a[-1], acc_u[-1]
    sub = lax.broadcasted_iota(jnp.int32, end_a.shape, 0)
    shift = 1
    while shift < V7X_SUBLANES:
        keep = sub >= shift
        a_sh = pltpu.roll(end_a, shift, axis=0)
        u_sh = pltpu.roll(end_u, shift, axis=0)
        end_u = jnp.where(keep, end_a * u_sh + end_u, end_u)
        end_a = jnp.where(keep, end_a * a_sh, end_a)
        shift *= 2
    h_end = end_a * h0 + end_u
    h_in = jnp.where(sub == 0, h0, pltpu.roll(h_end, 1, axis=0))
    out = [acc_a[g] * h_in + acc_u[g] for g in range(group)]
    return jnp.stack(out, axis=0), h_end[V7X_SUBLANES - 1:V7X_SUBLANES, :]


def _mixer_kernel(x_hbm, mixn_ref, win_hbm, convw_ref, convb_ref, wa_ref, ba_ref, wi_ref, bi_ref,
                  lam_ref, lng_ref, lnb_ref, ws_ref, bsp_ref, wout_hbm, ffn_ref,
                  x1_hbm, hp_hbm,
                  xbuf, z0_ref, z1_ref, x1buf, hpbuf, xsem, x1sem, hpsem, wsm_ref, ztail_ref, hcar_ref,
                  win_ref, wout_ref, wsem,
                  *, nseq):
    j = pl.program_id(0)
    ntile = pl.num_programs(0) - 1
    _, group, _, d = xbuf.shape
    rows = group * V7X_SUBLANES
    half = d // 2
    ta = jnp.minimum(j, ntile - 1)
    tb = jnp.maximum(j - 1, 0)
    s = lax.rem(tb, nseq)
    slot = lax.rem(tb, 2)

    def fetch(t):
        return _tile_copies(x_hbm, xbuf, xsem, lax.div(t, nseq), lax.rem(t, nseq) * rows,
                            lax.rem(t, 3), to_hbm=False)

    def put(t):
        tb_, ts_, sl = lax.div(t, nseq), lax.rem(t, nseq) * rows, lax.rem(t, 2)
        return (_tile_copies(x1_hbm, x1buf, x1sem, tb_, ts_, sl, to_hbm=True)
                + _tile_copies(hp_hbm, hpbuf, hpsem, tb_, ts_, sl, to_hbm=True))

    @pl.when(j == 0)
    def _():
        for c in fetch(0):
            c.start()
        stage = (z0_ref, z1_ref)
        n_in_chunks = win_hbm.shape[0] // rows

        def win_copy(c):
            return pltpu.make_async_copy(win_hbm.at[pl.ds(c * rows, rows), :], stage[c % 2], wsem.at[c % 2])

        win_copy(0).start()
        for c in range(n_in_chunks):
            if c + 1 < n_in_chunks:
                win_copy(c + 1).start()
            win_copy(c).wait()
            plain = 4 * d
            win_ref[c * rows:(c + 1) * rows, :plain] = stage[c % 2][:, :plain].astype(BF16)
            win_ref[c * rows:(c + 1) * rows, plain:] = (0.5 * stage[c % 2][:, plain:]).astype(BF16)
        n_out_chunks = wout_hbm.shape[0] // rows
        out_copies = [pltpu.make_async_copy(wout_hbm.at[pl.ds(c * rows, rows), :],
                                            z0_ref.at[:, c * d:(c + 1) * d], wsem.at[0])
                      for c in range(n_out_chunks)]
        for cp in out_copies:
            cp.start()
        for cp in out_copies:
            cp.wait()
        for c in range(n_out_chunks):
            wout_ref[c * rows:(c + 1) * rows, :] = (0.25 * z0_ref[:, c * d:(c + 1) * d]).astype(BF16)
        z1_ref[...] = jnp.zeros_like(z1_ref)
        i_idx = lax.broadcasted_iota(jnp.int32, (rows, rows), 0)
        j_idx = lax.broadcasted_iota(jnp.int32, (rows, rows), 1)
        t_i = group * lax.rem(i_idx, V7X_SUBLANES) + lax.div(i_idx, V7X_SUBLANES)
        t_j = group * lax.rem(j_idx, V7X_SUBLANES) + lax.div(j_idx, V7X_SUBLANES)
        keep = (t_i >= t_j) & (lax.div(t_i, CHUNK) == lax.div(t_j, CHUNK))
        pick_rows = jnp.where(t_i == j_idx, 1.0, 0.0).astype(BF16)
        pick_cols = jnp.where(i_idx == t_j, 1.0, 0.0).astype(BF16)
        reps = rows // CHUNK
        for g in range(SGU_GROUPS):
            w_chunk = ws_ref[g].astype(BF16)
            w_rows = jnp.concatenate([w_chunk] * reps, axis=1)
            w_full = jnp.concatenate([w_rows] * reps, axis=0)
            w_perm = _dot(_dot(pick_rows, w_full).astype(BF16), pick_cols)
            wsm_ref[g] = jnp.where(keep, w_perm, 0.0).astype(BF16)

    @pl.when(j + 1 < ntile)
    def _():
        for c in fetch(j + 1):
            c.start()

    @pl.when(j < ntile)
    def _():
        for c in fetch(j):
            c.wait()

    @pl.when(s == 0)
    def _():
        ztail_ref[...] = jnp.zeros_like(ztail_ref)
        hcar_ref[...] = jnp.zeros_like(hcar_ref)

    def compute(z_w, z_r):
        xa_in = xbuf[lax.rem(ta, 3)].reshape(rows, d)
        h_next = _rmsnorm(xa_in, mixn_ref[...]).astype(BF16)
        pw = d // 2

        def project(k):
            z_w[:, k * pw:(k + 1) * pw] = _dot(h_next, win_ref[:, k * pw:(k + 1) * pw])

        x = xbuf[lax.rem(tb, 3)].reshape(rows, d)

        def sec(k, c0, c1):
            return z_r[:, k * d + c0:k * d + c1]

        def one_plus_tanh_gelu(v):
            c = 0.7978845608028654
            return 1.0 + jnp.tanh(v * (c + (c * 0.044715) * (v * v)))

        cw = 0.5 * convw_ref[...]
        cb_h = 0.5 * convb_ref[...]
        ba_h = 0.5 * ba_ref[...]
        bi_h = 0.5 * bi_ref[...]
        neg_lam = -lam_ref[...]
        softplus = jnp.maximum(neg_lam, 0.0) + jnp.log1p(jnp.exp(-jnp.abs(neg_lam)))
        c_a = (-0.5 * LRU_C) * softplus
        blk = V7X_MXU_DIM
        sub3 = lax.broadcasted_iota(jnp.int32, (CONV_WIDTH - 1, V7X_SUBLANES, blk), 1)
        term_a = []
        for n in range(d // blk):
            project(n)
            c0, c1 = n * blk, (n + 1) * blk
            z3 = sec(0, c0, c1).reshape(group, V7X_SUBLANES, blk)
            tail = z3[group - (CONV_WIDTH - 1):]
            halo = jnp.where(sub3 == 0, pltpu.roll(ztail_ref[:, :, c0:c1], 1, axis=1),
                             pltpu.roll(tail, 1, axis=1))
            ztail_ref[:, :, c0:c1] = tail
            zext = jnp.concatenate([halo, z3], axis=0)
            xa_h = cb_h[:, c0:c1] + cw[CONV_WIDTH - 1:CONV_WIDTH, c0:c1] * z3
            for k in range(1, CONV_WIDTH):
                lo = CONV_WIDTH - 1 - k
                xa_h = xa_h + cw[lo:lo + 1, c0:c1] * zext[lo:lo + group]
            xa2 = xa_h.reshape(rows, blk)
            xa_bf = xa2.astype(BF16)
            th_r = jnp.tanh(_dot(xa_bf, wa_ref[n]) + ba_h[:, c0:c1])
            th_i = jnp.tanh(_dot(xa_bf, wi_ref[n]) + bi_h[:, c0:c1])
            a = jnp.exp(c_a[:, c0:c1] + c_a[:, c0:c1] * th_r)
            u = jnp.sqrt(1.0 - a * a) * ((1.0 + th_i) * xa2)
            hseq, hlast = _lru_scan(a.reshape(group, V7X_SUBLANES, blk),
                                    u.reshape(group, V7X_SUBLANES, blk), hcar_ref[:, c0:c1])
            hcar_ref[:, c0:c1] = hlast
            zg = sec(1, c0, c1)
            term_a.append(((1.0 + jnp.tanh(sec(4, c0, c1))) * one_plus_tanh_gelu(zg))
                          * (zg * hseq.reshape(rows, blk)))

        project(4)
        zv = sec(3, 0, d)
        gv2 = zv * one_plus_tanh_gelu(zv)
        project(5)
        mu = jnp.mean(gv2, axis=-1, keepdims=True)
        xc = gv2 - mu
        var = jnp.mean(xc * xc, axis=-1, keepdims=True)
        v_bf = (xc * lax.rsqrt(var + 4.0 * EPS) * lng_ref[...] + lnb_ref[...]).astype(BF16)
        project(6)
        gdim = d // SGU_GROUPS
        term_b = []
        for g in range(SGU_GROUPS):
            c0, c1 = g * gdim, (g + 1) * gdim
            if g in (1, 3, 5, 6, 7):
                project({1: 7, 3: 8, 5: 9, 6: 10, 7: 11}[g])
            sp = _dot(wsm_ref[g], v_bf[:, c0:c1]) + bsp_ref[:, g:g + 1]
            zu = sec(2, c0, c1)
            term_b.append(((1.0 + jnp.tanh(sec(5, c0, c1))) * one_plus_tanh_gelu(zu)) * (zu * sp))
        merged4 = jnp.concatenate(term_a, axis=1) + jnp.concatenate(term_b, axis=1)

        x1 = x + _dot(merged4.astype(BF16), wout_ref[...])

        hn = _rmsnorm(x1, ffn_ref[...])
        hp = _pack_bf16_pair(hn[:, :half], hn[:, half:])

        @pl.when(j >= 3)
        def _():
            for c in put(tb - 2):
                c.wait()

        x1buf[slot] = x1.reshape(group, V7X_SUBLANES, d)
        hpbuf[slot] = hp.reshape(group, V7X_SUBLANES, half)

        @pl.when(j >= 1)
        def _():
            for c in put(tb):
                c.start()

    @pl.when(lax.rem(j, 2) == 0)
    def _():
        compute(z0_ref, z1_ref)

    @pl.when(lax.rem(j, 2) == 1)
    def _():
        compute(z1_ref, z0_ref)

    @pl.when(j == ntile)
    def _():
        for c in put(tb):
            c.wait()

        @pl.when(ntile >= 2)
        def _():
            for c in put(tb - 1):
                c.wait()


def _mixer_call(x, mix_norm, w_in, conv_w, conv_b, wa_blk, ba, wi_blk, bi, lam, ln_g, ln_b, ws,
                bs_tile, w_out, ffn_norm):
    cfg = _tiles()
    bsz, seq, d = x.shape
    ts = cfg["mixer_rows"]
    group = ts // V7X_SUBLANES
    nseq = seq // ts
    ntile = bsz * nseq
    row1 = (1, d)
    in_specs = [
        pl.BlockSpec(memory_space=pl.ANY),
        _const_spec(row1),
        pl.BlockSpec(memory_space=pl.ANY),
        _const_spec(conv_w.shape), _const_spec(row1),
        _const_spec(wa_blk.shape), _const_spec(row1),
        _const_spec(wi_blk.shape), _const_spec(row1),
        _const_spec(row1),
        _const_spec(row1), _const_spec(row1),
        _const_spec(ws.shape), _const_spec(bs_tile.shape),
        pl.BlockSpec(memory_space=pl.ANY), _const_spec(row1),
    ]
    out_shape = [
        jax.ShapeDtypeStruct((bsz, seq, d), F32),
        jax.ShapeDtypeStruct((bsz, seq, d // 2), U32),
    ]
    out_specs = [
        pl.BlockSpec(memory_space=pl.ANY),
        pl.BlockSpec(memory_space=pl.ANY),
    ]
    scratch = [
        pltpu.VMEM((3, group, V7X_SUBLANES, d), F32),
        pltpu.VMEM((ts, w_in.shape[1]), F32),
        pltpu.VMEM((ts, w_in.shape[1]), F32),
        pltpu.VMEM((2, group, V7X_SUBLANES, d), F32),
        pltpu.VMEM((2, group, V7X_SUBLANES, d // 2), U32),
        pltpu.SemaphoreType.DMA((3,)),
        pltpu.SemaphoreType.DMA((2,)),
        pltpu.SemaphoreType.DMA((2,)),
        pltpu.VMEM((SGU_GROUPS, ts, ts), BF16),
        pltpu.VMEM((CONV_WIDTH - 1, V7X_SUBLANES, d), F32),
        pltpu.VMEM((1, d), F32),
        pltpu.VMEM(w_in.shape, BF16),
        pltpu.VMEM(w_out.shape, BF16),
        pltpu.SemaphoreType.DMA((2,)),
    ]
    return pl.pallas_call(
        functools.partial(_mixer_kernel, nseq=nseq),
        grid=(ntile + 1,),
        in_specs=in_specs,
        out_specs=out_specs,
        out_shape=out_shape,
        scratch_shapes=scratch,
        compiler_params=pltpu.CompilerParams(
            dimension_semantics=("arbitrary",),
            vmem_limit_bytes=cfg["mixer_vmem"]),
        name="mixer",
    )(x, mix_norm, w_in, conv_w, conv_b, wa_blk, ba, wi_blk, bi, lam, ln_g, ln_b, ws, bs_tile,
      w_out, ffn_norm)


def _router_kernel(hp_ref, wg_ref, we_ref, br_ref, pos_ref, gate_ref, cnt_ref, ccar_ref, wr_ref,
                   *, expert_capacity):
    rows = hp_ref.shape[0]

    @pl.when(pl.program_id(0) == 0)
    def _():
        ccar_ref[...] = jnp.zeros_like(ccar_ref)
        wr_ref[...] = jnp.zeros_like(wr_ref)
        wr_ref[:, 0:N_GROUPS] = wg_ref[...].astype(BF16)
        wr_ref[:, EXPERT_ROW0:EXPERT_ROW0 + N_EXPERTS] = we_ref[...].astype(BF16)

    lo, hi = _unpack_bf16_pair(hp_ref[...])
    hn = jnp.concatenate([lo, hi], axis=1)
    logits = _dot(hn.astype(BF16), wr_ref[...])
    lt = jnp.transpose(logits) + br_ref[...]
    sub = lax.broadcasted_iota(jnp.int32, (V7X_SUBLANES, rows), 0)
    subf = sub.astype(F32)
    big = jnp.float32(1e9)

    lg = jnp.where(sub < N_GROUPS, lt[0:V7X_SUBLANES, :], -jnp.inf)
    g_exp = jnp.exp(lg - jnp.max(lg, axis=0, keepdims=True))
    g_prob = g_exp / jnp.sum(g_exp, axis=0, keepdims=True)
    g_top = jnp.max(g_prob, axis=0, keepdims=True)
    g_idx = jnp.min(jnp.where(g_prob == g_top, subf, big), axis=0, keepdims=True)

    e_sel = jnp.zeros((EXPERTS_PER_GROUP, rows), F32)
    for g in range(N_GROUPS):
        r0 = EXPERT_ROW0 + g * EXPERTS_PER_GROUP
        e_sel = jnp.where(g_idx == g, lt[r0:r0 + EXPERTS_PER_GROUP, :], e_sel)
    e_exp = jnp.exp(e_sel - jnp.max(e_sel, axis=0, keepdims=True))
    e_prob = e_exp / jnp.sum(e_exp, axis=0, keepdims=True)
    p1 = jnp.max(e_prob, axis=0, keepdims=True)
    i1 = jnp.min(jnp.where(e_prob == p1, subf, big), axis=0, keepdims=True)
    rest = jnp.where(subf == i1, -1.0, e_prob)
    p2 = jnp.max(rest, axis=0, keepdims=True)
    i2 = jnp.min(jnp.where(rest == p2, subf, big), axis=0, keepdims=True)
    psum = p1 + p2
    gate1 = g_top * (p1 / psum)
    gate2 = g_top * (p2 / psum)
    gid1 = g_idx * EXPERTS_PER_GROUP + i1
    gid2 = g_idx * EXPERTS_PER_GROUP + i2

    eid = lax.broadcasted_iota(jnp.int32, (N_EXPERTS, rows), 0).astype(F32)
    hit1 = eid == gid1
    hit2 = eid == gid2
    cnt = jnp.where(hit1 | hit2, 1.0, 0.0)
    sb = V7X_MXU_DIM
    before = (lax.broadcasted_iota(jnp.int32, (sb, sb), 0)
              < lax.broadcasted_iota(jnp.int32, (sb, sb), 1))
    before = jnp.where(before, 1.0, 0.0).astype(BF16)
    running = ccar_ref[:, 0:1]
    base = []
    for q in range(rows // sb):
        part = cnt[:, q * sb:(q + 1) * sb]
        base.append(running + _dot(part.astype(BF16), before))
        running = running + jnp.sum(part, axis=1, keepdims=True)
    base = jnp.concatenate(base, axis=1)
    rank1 = jnp.sum(jnp.where(hit1, base, 0.0), axis=0, keepdims=True)
    rank2 = jnp.sum(jnp.where(hit2, base, 0.0), axis=0, keepdims=True)
    total = jnp.broadcast_to(running, ccar_ref.shape)
    ccar_ref[...] = total
    cnt_ref[...] = total
    cap = float(expert_capacity)
    zero = jnp.zeros((V7X_SUBLANES - TOP_K, rows), F32)
    pos = jnp.concatenate([gid1 * cap + rank1, gid2 * cap + rank2, zero], axis=0)
    pos_ref[...] = pos.astype(jnp.int32)
    gate_ref[...] = jnp.transpose(jnp.concatenate([gate1, gate2, zero], axis=0))


def _router_call(hp, w_group, w_expert, b_router):
    cfg = _tiles()
    ntok, half = hp.shape
    tr = cfg["router_rows"]
    return pl.pallas_call(
        functools.partial(_router_kernel, expert_capacity=ntok),
        grid=(ntok // tr,),
        in_specs=[
            pl.BlockSpec((tr, half), lambda i: (i, 0)),
            _const_spec(w_group.shape),
            _const_spec(w_expert.shape),
            _const_spec(b_router.shape),
        ],
        out_specs=[
            pl.BlockSpec((V7X_SUBLANES, tr), lambda i: (0, i)),
            pl.BlockSpec((tr, V7X_SUBLANES), lambda i: (i, 0)),
            pl.BlockSpec((N_EXPERTS, V7X_LANES), lambda i: (0, 0)),
        ],
        out_shape=[
            jax.ShapeDtypeStruct((V7X_SUBLANES, ntok), jnp.int32),
            jax.ShapeDtypeStruct((ntok, V7X_SUBLANES), F32),
            jax.ShapeDtypeStruct((N_EXPERTS, V7X_LANES), F32),
        ],
        scratch_shapes=[
            pltpu.VMEM((N_EXPERTS, V7X_LANES), F32),
            pltpu.VMEM((w_group.shape[0], ROUTER_ROWS), BF16),
        ],
        compiler_params=pltpu.CompilerParams(
            dimension_semantics=("arbitrary",),
            vmem_limit_bytes=cfg["router_vmem"]),
        name="router",
    )(hp, w_group, w_expert, b_router)


def _expert_kernel(nt_ref, base_ref, texp_ref, tloc_ref, hs_hbm, w1_ref, w3_ref, w2_ref, ys_hbm,
                   hbuf, ybuf, hsem, ysem, w1b_ref, w3b_ref, w2b_ref, *, capacity):
    e = pl.program_id(0)
    n_exp = pl.num_programs(0)
    nt = nt_ref[e]
    base = base_ref[e]
    total = base_ref[n_exp - 1] + nt_ref[n_exp - 1]
    n_in, tm, _ = hbuf.shape
    n_out = ybuf.shape[0]
    ahead = n_in - 1

    def load(g):
        slot = lax.rem(g, n_in)
        rows = pl.ds(texp_ref[g] * capacity + tloc_ref[g] * tm, tm)
        return pltpu.make_async_copy(hs_hbm.at[rows], hbuf.at[slot], hsem.at[slot])

    def store(t, slot):
        rows = pl.ds(e * capacity + t * tm, tm)
        return pltpu.make_async_copy(ybuf.at[slot], ys_hbm.at[rows], ysem.at[slot])

    @pl.when(e == 0)
    def _():
        for g0 in range(ahead):
            @pl.when(g0 < total)
            def _():
                load(g0).start()

    w1b_ref[...] = w1_ref[...].astype(BF16)
    w3b_ref[...] = w3_ref[...].astype(BF16)
    w2b_ref[...] = w2_ref[...].astype(BF16)

    @pl.loop(0, nt)
    def _(t):
        g = base + t

        @pl.when(g + ahead < total)
        def _():
            load(g + ahead).start()

        load(g).wait()
        lo, hi = _unpack_bf16_pair(hbuf[lax.rem(g, n_in)])
        h = jnp.concatenate([lo, hi], axis=1).astype(BF16)
        a = _dot(h, w1b_ref[...])
        b = _dot(h, w3b_ref[...])
        hid = (a * _sigmoid(a)) * b
        y = _dot(hid.astype(BF16), w2b_ref[...])
        half = y.shape[1] // 2
        slot = lax.rem(g, n_out)

        @pl.when(g >= n_out)
        def _():
            store(t, slot).wait()

        ybuf[slot] = _pack_bf16_pair(y[:, :half], y[:, half:])
        store(t, slot).start()

    @pl.when(e + 1 == n_exp)
    def _():
        for back in range(1, n_out + 1):
            @pl.when(total >= back)
            def _():
                store(0, lax.rem(total - back, n_out)).wait()


def _expert_call(tiles_per_expert, hs, w1, w3, w2, capacity):
    cfg = _tiles()
    tm = cfg["expert_rows"]
    prow, half = hs.shape
    n_exp, d, f = w1.shape
    ends = jnp.cumsum(tiles_per_expert)
    base = ends - tiles_per_expert
    g = jnp.arange(capacity * TOP_K // tm + n_exp, dtype=jnp.int32)
    texp = jnp.minimum(jnp.sum((ends[None, :] <= g[:, None]).astype(jnp.int32), axis=1), n_exp - 1)
    onehot = texp[:, None] == jnp.arange(n_exp, dtype=jnp.int32)[None, :]
    tloc = g - jnp.sum(jnp.where(onehot, base[None, :], 0), axis=1)

    def w_map(e, *_):
        return (e, 0, 0)

    grid_spec = pltpu.PrefetchScalarGridSpec(
        num_scalar_prefetch=4,
        grid=(n_exp,),
        in_specs=[
            pl.BlockSpec(memory_space=pl.ANY),
            pl.BlockSpec((None, d, f), w_map),
            pl.BlockSpec((None, d, f), w_map),
            pl.BlockSpec((None, f, d), w_map),
        ],
        out_specs=pl.BlockSpec(memory_space=pl.ANY),
        scratch_shapes=[
            pltpu.VMEM((EXPERT_LOOKAHEAD + 1, tm, half), U32),
            pltpu.VMEM((2, tm, half), U32),
            pltpu.SemaphoreType.DMA((EXPERT_LOOKAHEAD + 1,)),
            pltpu.SemaphoreType.DMA((2,)),
            pltpu.VMEM((d, f), BF16),
            pltpu.VMEM((d, f), BF16),
            pltpu.VMEM((f, d), BF16),
        ],
    )
    return pl.pallas_call(
        functools.partial(_expert_kernel, capacity=capacity),
        grid_spec=grid_spec,
        out_shape=jax.ShapeDtypeStruct((prow, half), U32),
        compiler_params=pltpu.CompilerParams(
            dimension_semantics=("arbitrary",),
            vmem_limit_bytes=cfg["expert_vmem"]),
        name="experts",
    )(tiles_per_expert, base, texp, tloc, hs, w1, w3, w2)


def _sc_mesh():
    return plsc.VectorSubcoreMesh(core_axis_name="c", subcore_axis_name="s",
                                  num_cores=V7X_SC_CORES, num_subcores=V7X_SC_SUBCORES)


def _sc_worker_id():
    return lax.axis_index("s") * V7X_SC_CORES + lax.axis_index("c")


def _dispatch_call(hp, pos_w, out_rows):
    cfg = _tiles()
    ntok, half = hp.shape
    nw, topk, nch, ch = pos_w.shape
    per_w = nch * ch

    def body(hp_hbm, pos_hbm, hs_hbm, idx_v, buf, wsem):
        wid = _sc_worker_id()
        pltpu.sync_copy(pos_hbm.at[wid], idx_v)

        for c in range(nch):
            pltpu.sync_copy(hp_hbm.at[pl.ds(wid * per_w + c * ch, ch)], buf)
            writes = [pltpu.make_async_copy(buf, hs_hbm.at[idx_v.at[k, c]], wsem.at[k]) for k in range(topk)]
            for w in writes:
                w.start()
            for w in writes:
                w.wait()

    assert nw == V7X_SC_CORES * V7X_SC_SUBCORES and nw * per_w == ntok and ch == cfg["sc_rows"]
    return pl.kernel(
        body,
        out_type=jax.ShapeDtypeStruct((out_rows, half), U32),
        mesh=_sc_mesh(),
        scratch_types=[
            pltpu.VMEM((topk, nch, ch), jnp.int32),
            pltpu.VMEM((ch, half), U32),
            pltpu.SemaphoreType.DMA((topk,)),
        ],
        name="dispatch",
    )(hp, pos_w)


def _combine_call(ys, pos_w):
    cfg = _tiles()
    _, half = ys.shape
    nw, topk, nch, ch = pos_w.shape
    per_w = nch * ch
    ntok = nw * per_w

    def body(ys_hbm, pos_hbm, *rest):
        outs = rest[:topk]
        idx_v, buf = rest[topk:]
        wid = _sc_worker_id()
        pltpu.sync_copy(pos_hbm.at[wid], idx_v)
        for c in range(nch):
            for k in range(topk):
                pltpu.sync_copy(ys_hbm.at[idx_v.at[k, c]], buf)
                pltpu.sync_copy(buf, outs[k].at[pl.ds(wid * per_w + c * ch, ch)])

    assert nw == V7X_SC_CORES * V7X_SC_SUBCORES and ch == cfg["sc_rows"]
    return pl.kernel(
        body,
        out_type=[jax.ShapeDtypeStruct((ntok, half), U32)] * topk,
        mesh=_sc_mesh(),
        scratch_types=[
            pltpu.VMEM((topk, nch, ch), jnp.int32),
            pltpu.VMEM((ch, half), U32),
        ],
        name="combine",
    )(ys, pos_w)


def _ple_kernel(x1_ref, yg0_ref, yg1_ref, gate_ref, p_ref, plen_ref, wg32_ref, wu32_ref, fin_ref, o_ref,
                wg_ref, wu_ref):
    @pl.when(pl.program_id(0) == 0)
    def _():
        wg_ref[...] = wg32_ref[...].astype(BF16)
        wu_ref[...] = wu32_ref[...].astype(BF16)

    rows = x1_ref.shape[0]
    sub_rows = rows // PLE_SUBBLOCKS
    for q in range(PLE_SUBBLOCKS):
        rs = pl.ds(q * sub_rows, sub_rows)
        lo0, hi0 = _unpack_bf16_pair(yg0_ref[rs, :])
        lo1, hi1 = _unpack_bf16_pair(yg1_ref[rs, :])
        g0 = gate_ref[rs, 0:1]
        g1 = gate_ref[rs, 1:2]
        moe = g0 * jnp.concatenate([lo0, hi0], axis=1) + g1 * jnp.concatenate([lo1, hi1], axis=1)
        x2 = x1_ref[rs, :] + moe
        r = _rmsnorm(x2, plen_ref[...]).astype(BF16)
        gt = _sigmoid(_dot(r, wg_ref[...]))
        up = _dot(p_ref[rs, :].astype(BF16), wu_ref[...])
        x3 = x2 + gt * up
        o_ref[rs, :] = _rmsnorm(x3, fin_ref[...])


def _ple_call(x1, yg0, yg1, gates, p, ple_norm, wg, wu, final_norm):
    cfg = _tiles()
    ntok, d = x1.shape
    tp = cfg["ple_rows"]
    pdim = p.shape[1]
    return pl.pallas_call(
        _ple_kernel,
        grid=(ntok // tp,),
        in_specs=[
            pl.BlockSpec((tp, d), lambda i: (i, 0)),
            pl.BlockSpec((tp, d // 2), lambda i: (i, 0)),
            pl.BlockSpec((tp, d // 2), lambda i: (i, 0)),
            pl.BlockSpec((tp, V7X_SUBLANES), lambda i: (i, 0)),
            pl.BlockSpec((tp, pdim), lambda i: (i, 0)),
            _const_spec((1, d)),
            _const_spec(wg.shape),
            _const_spec(wu.shape),
            _const_spec((1, d)),
        ],
        out_specs=pl.BlockSpec((tp, d), lambda i: (i, 0)),
        out_shape=jax.ShapeDtypeStruct((ntok, d), F32),
        scratch_shapes=[pltpu.VMEM(wg.shape, BF16), pltpu.VMEM(wu.shape, BF16)],
        compiler_params=pltpu.CompilerParams(
            dimension_semantics=("arbitrary",),
            vmem_limit_bytes=cfg["ple_vmem"]),
        name="ple",
    )(x1, yg0, yg1, gates, p, ple_norm, wg, wu, final_norm)


def _blockdiag_pack(w):
    nb, bd, _ = w.shape
    per = V7X_MXU_DIM // bd
    w4 = w.reshape(nb // per, per, bd, bd)
    eye = jnp.eye(per, dtype=w.dtype)
    out = jnp.einsum("jpab,pq->jpaqb", w4, eye)
    return out.reshape(nb // per, V7X_MXU_DIM, V7X_MXU_DIM).astype(BF16)


def kernel(x, p, mix_norm, w_in, conv_w, conv_b, lru_wa, lru_ba, lru_wi, lru_bi, lru_lambda, sgu_ln_g, sgu_ln_b, sgu_ws, sgu_bs, w_out, ffn_norm, router_group_w, router_group_b, router_expert_w, router_expert_b, expert_w1, expert_w3, expert_w2, ple_norm, ple_gate_w, ple_up_w, final_norm):
    cfg = _tiles()
    bsz, seq, d = x.shape
    ntok = bsz * seq
    tm = cfg["expert_rows"]
    depth = w_in.shape[0]
    assert depth == 1, "the ple kernel applies the final norm, so it must be the last layer"
    l = 0
    b_router = jnp.concatenate([
        router_group_b[l], jnp.zeros((EXPERT_ROW0 - N_GROUPS,), F32), router_expert_b[l],
        jnp.zeros((ROUTER_ROWS - EXPERT_ROW0 - N_EXPERTS,), F32)])[:, None]
    ts = cfg["mixer_rows"]
    group = ts // V7X_SUBLANES
    bs_tile = jnp.tile(sgu_bs[l], (1, ts // CHUNK)).reshape(SGU_GROUPS, V7X_SUBLANES, group)
    bs_tile = jnp.transpose(bs_tile, (2, 1, 0)).reshape(ts, SGU_GROUPS)
    x1, hp = _mixer_call(
        x, mix_norm[l][None], w_in[l], conv_w[l], conv_b[l][None],
        _blockdiag_pack(lru_wa[l]), lru_ba[l][None], _blockdiag_pack(lru_wi[l]), lru_bi[l][None],
        lru_lambda[l][None], sgu_ln_g[l][None], sgu_ln_b[l][None], sgu_ws[l], bs_tile,
        w_out[l], ffn_norm[l][None])
    hp = hp.reshape(ntok, d // 2)
    pos, gate, cnt = _router_call(hp, router_group_w[l], router_expert_w[l], b_router)

    cap = ntok
    tiles_per_expert = (cnt[:, 0].astype(jnp.int32) + tm - 1) // tm
    nw = V7X_SC_CORES * V7X_SC_SUBCORES
    ch = cfg["sc_rows"]
    pos_w = jnp.transpose(pos[:TOP_K].reshape(TOP_K, nw, ntok // (nw * ch), ch), (1, 0, 2, 3))

    hs = _dispatch_call(hp, pos_w, N_EXPERTS * cap)
    ys = _expert_call(tiles_per_expert, hs, expert_w1[l], expert_w3[l], expert_w2[l], cap)
    yg0, yg1 = _combine_call(ys, pos_w)

    out = _ple_call(x1.reshape(ntok, d), yg0, yg1, gate, p[l].reshape(ntok, -1), ple_norm[l][None],
                    ple_gate_w[l], ple_up_w[l], final_norm[None])
    return out.reshape(bsz, seq, d)
```

```python
import functools

import jax
import jax.numpy as jnp
from jax import lax
from jax.experimental import pallas as pl
from jax.experimental.pallas import tpu as pltpu
from jax.experimental.pallas import tpu_sc as plsc

F32 = jnp.float32
BF16 = jnp.bfloat16
U32 = jnp.uint32

LRU_BLOCKS = 16
CONV_WIDTH = 4
LRU_C = 8.0
SGU_GROUPS = 8
CHUNK = 128
N_GROUPS = 4
EXPERTS_PER_GROUP = 8
N_EXPERTS = N_GROUPS * EXPERTS_PER_GROUP
TOP_K = 2
EPS = 1e-6

V7X_MXU_DIM = 256
V7X_SUBLANES = 8
V7X_LANES = 128
V7X_VMEM_BYTES = 64 * 1024 * 1024
V7X_SC_CORES = 2
V7X_SC_SUBCORES = 16

EXPERT_LOOKAHEAD = 3
PLE_SUBBLOCKS = 4
ROUTER_ROWS = V7X_LANES
EXPERT_ROW0 = V7X_SUBLANES


def _tiles():
    return dict(
        mixer_rows=256,
        expert_rows=512,
        ple_rows=1024,
        sc_rows=128,
        router_rows=2048,
        mixer_vmem=52 * 1024 * 1024,
        expert_vmem=40 * 1024 * 1024,
        ple_vmem=48 * 1024 * 1024,
        router_vmem=32 * 1024 * 1024,
    )


def _dot(a, b):
    return jnp.dot(a, b, preferred_element_type=F32)


def _sigmoid(x):
    return 0.5 * jnp.tanh(0.5 * x) + 0.5


def _rmsnorm(x, g):
    ms = jnp.mean(x * x, axis=-1, keepdims=True)
    return x * lax.rsqrt(ms + EPS) * g


def _pack_bf16_pair(lo, hi):
    lo_b = lax.bitcast_convert_type(lo.astype(BF16).astype(F32), U32)
    hi_b = lax.bitcast_convert_type(hi.astype(BF16).astype(F32), U32)
    return (hi_b & jnp.uint32(0xFFFF0000)) | lax.shift_right_logical(lo_b, jnp.uint32(16))


def _unpack_bf16_pair(w):
    lo = lax.bitcast_convert_type(lax.shift_left(w, jnp.uint32(16)), F32)
    hi = lax.bitcast_convert_type(w & jnp.uint32(0xFFFF0000), F32)
    return lo, hi


def _const_spec(shape):
    zeros = (0,) * len(shape)
    return pl.BlockSpec(shape, lambda *_: zeros, pipeline_mode=pl.Buffered(1))


def _tile_copies(hbm, buf, sem, b, row0, slot, to_hbm):
    group = buf.shape[1]
    copies = []
    for r in range(V7X_SUBLANES):
        hbm_rows = hbm.at[b, pl.ds(row0 + group * r, group), :]
        vmem_rows = buf.at[slot, :, r, :]
        src, dst = (vmem_rows, hbm_rows) if to_hbm else (hbm_rows, vmem_rows)
        copies.append(pltpu.make_async_copy(src, dst, sem.at[slot]))
    return copies


def _lru_scan(a, u, h0):
    group = a.shape[0]
    acc_a = [a[0]]
    acc_u = [u[0]]
    for g in range(1, group):
        acc_a.append(a[g] * acc_a[-1])
        acc_u.append(a[g] * acc_u[-1] + u[g])
    end_a, end_u = acc_a[-1], acc_u[-1]
    sub = lax.broadcasted_iota(jnp.int32, end_a.shape, 0)
    shift = 1
    while shift < V7X_SUBLANES:
        keep = sub >= shift
        a_sh = pltpu.roll(end_a, shift, axis=0)
        u_sh = pltpu.roll(end_u, shift, axis=0)
        end_u = jnp.where(keep, end_a * u_sh + end_u, end_u)
        end_a = jnp.where(keep, end_a * a_sh, end_a)
        shift *= 2
    h_end = end_a * h0 + end_u
    h_in = jnp.where(sub == 0, h0, pltpu.roll(h_end, 1, axis=0))
    out = [acc_a[g] * h_in + acc_u[g] for g in range(group)]
    return jnp.stack(out, axis=0), h_end[V7X_SUBLANES - 1:V7X_SUBLANES, :]


def _mixer_kernel(x_hbm, mixn_ref, win_hbm, convw_ref, convb_ref, wa_ref, ba_ref, wi_ref, bi_ref,
                  lam_ref, lng_ref, lnb_ref, ws_ref, bsp_ref, wout_hbm, ffn_ref,
                  x1_hbm, hp_hbm,
                  xbuf, z0_ref, z1_ref, x1buf, hpbuf, xsem, x1sem, hpsem, wsm_ref, ztail_ref, hcar_ref,
                  win_ref, wout_ref, wsem,
                  *, nseq):
    j = pl.program_id(0)
    ntile = pl.num_programs(0) - 1
    _, group, _, d = xbuf.shape
    rows = group * V7X_SUBLANES
    half = d // 2
    ta = jnp.minimum(j, ntile - 1)
    tb = jnp.maximum(j - 1, 0)
    s = lax.rem(tb, nseq)
    slot = lax.rem(tb, 2)

    def fetch(t):
        return _tile_copies(x_hbm, xbuf, xsem, lax.div(t, nseq), lax.rem(t, nseq) * rows,
                            lax.rem(t, 3), to_hbm=False)

    def put(t):
        tb_, ts_, sl = lax.div(t, nseq), lax.rem(t, nseq) * rows, lax.rem(t, 2)
        return (_tile_copies(x1_hbm, x1buf, x1sem, tb_, ts_, sl, to_hbm=True)
                + _tile_copies(hp_hbm, hpbuf, hpsem, tb_, ts_, sl, to_hbm=True))

    @pl.when(j == 0)
    def _():
        for c in fetch(0):
            c.start()
        stage = (z0_ref, z1_ref)
        n_in_chunks = win_hbm.shape[0] // rows

        def win_copy(c):
            return pltpu.make_async_copy(win_hbm.at[pl.ds(c * rows, rows), :], stage[c % 2], wsem.at[c % 2])

        win_copy(0).start()
        for c in range(n_in_chunks):
            if c + 1 < n_in_chunks:
                win_copy(c + 1).start()
            win_copy(c).wait()
            plain = 4 * d
            win_ref[c * rows:(c + 1) * rows, :plain] = stage[c % 2][:, :plain].astype(BF16)
            win_ref[c * rows:(c + 1) * rows, plain:] = (0.5 * stage[c % 2][:, plain:]).astype(BF16)
        n_out_chunks = wout_hbm.shape[0] // rows
        out_copies = [pltpu.make_async_copy(wout_hbm.at[pl.ds(c * rows, rows), :],
                                            z0_ref.at[:, c * d:(c + 1) * d], wsem.at[0])
                      for c in range(n_out_chunks)]
        for cp in out_copies:
            cp.start()
        for cp in out_copies:
            cp.wait()
        for c in range(n_out_chunks):
            wout_ref[c * rows:(c + 1) * rows, :] = (0.25 * z0_ref[:, c * d:(c + 1) * d]).astype(BF16)
        z1_ref[...] = jnp.zeros_like(z1_ref)
        i_idx = lax.broadcasted_iota(jnp.int32, (rows, rows), 0)
        j_idx = lax.broadcasted_iota(jnp.int32, (rows, rows), 1)
        t_i = group * lax.rem(i_idx, V7X_SUBLANES) + lax.div(i_idx, V7X_SUBLANES)
        t_j = group * lax.rem(j_idx, V7X_SUBLANES) + lax.div(j_idx, V7X_SUBLANES)
        keep = (t_i >= t_j) & (lax.div(t_i, CHUNK) == lax.div(t_j, CHUNK))
        pick_rows = jnp.where(t_i == j_idx, 1.0, 0.0).astype(BF16)
        pick_cols = jnp.where(i_idx == t_j, 1.0, 0.0).astype(BF16)
        reps = rows // CHUNK
        for g in range(SGU_GROUPS):
            w_chunk = ws_ref[g].astype(BF16)
            w_rows = jnp.concatenate([w_chunk] * reps, axis=1)
            w_full = jnp.concatenate([w_rows] * reps, axis=0)
            w_perm = _dot(_dot(pick_rows, w_full).astype(BF16), pick_cols)
            wsm_ref[g] = jnp.where(keep, w_perm, 0.0).astype(BF16)

    @pl.when(j + 1 < ntile)
    def _():
        for c in fetch(j + 1):
            c.start()

    @pl.when(j < ntile)
    def _():
        for c in fetch(j):
            c.wait()

    @pl.when(s == 0)
    def _():
        ztail_ref[...] = jnp.zeros_like(ztail_ref)
        hcar_ref[...] = jnp.zeros_like(hcar_ref)

    def compute(z_w, z_r):
        xa_in = xbuf[lax.rem(ta, 3)].reshape(rows, d)
        h_next = _rmsnorm(xa_in, mixn_ref[...]).astype(BF16)
        pw = d // 2

        def project(k):
            z_w[:, k * pw:(k + 1) * pw] = _dot(h_next, win_ref[:, k * pw:(k + 1) * pw])

        x = xbuf[lax.rem(tb, 3)].reshape(rows, d)

        def sec(k, c0, c1):
            return z_r[:, k * d + c0:k * d + c1]

        def one_plus_tanh_gelu(v):
            c = 0.7978845608028654
            return 1.0 + jnp.tanh(v * (c + (c * 0.044715) * (v * v)))

        cw = 0.5 * convw_ref[...]
        cb_h = 0.5 * convb_ref[...]
        ba_h = 0.5 * ba_ref[...]
        bi_h = 0.5 * bi_ref[...]
        neg_lam = -lam_ref[...]
        softplus = jnp.maximum(neg_lam, 0.0) + jnp.log1p(jnp.exp(-jnp.abs(neg_lam)))
        c_a = (-0.5 * LRU_C) * softplus
        blk = V7X_MXU_DIM
        sub3 = lax.broadcasted_iota(jnp.int32, (CONV_WIDTH - 1, V7X_SUBLANES, blk), 1)
        term_a = []
        for n in range(d // blk):
            project(n)
            c0, c1 = n * blk, (n + 1) * blk
            z3 = sec(0, c0, c1).reshape(group, V7X_SUBLANES, blk)
            tail = z3[group - (CONV_WIDTH - 1):]
            halo = jnp.where(sub3 == 0, pltpu.roll(ztail_ref[:, :, c0:c1], 1, axis=1),
                             pltpu.roll(tail, 1, axis=1))
            ztail_ref[:, :, c0:c1] = tail
            zext = jnp.concatenate([halo, z3], axis=0)
            xa_h = cb_h[:, c0:c1] + cw[CONV_WIDTH - 1:CONV_WIDTH, c0:c1] * z3
            for k in range(1, CONV_WIDTH):
                lo = CONV_WIDTH - 1 - k
                xa_h = xa_h + cw[lo:lo + 1, c0:c1] * zext[lo:lo + group]
            xa2 = xa_h.reshape(rows, blk)
            xa_bf = xa2.astype(BF16)
            th_r = jnp.tanh(_dot(xa_bf, wa_ref[n]) + ba_h[:, c0:c1])
            th_i = jnp.tanh(_dot(xa_bf, wi_ref[n]) + bi_h[:, c0:c1])
            a = jnp.exp(c_a[:, c0:c1] + c_a[:, c0:c1] * th_r)
            u = jnp.sqrt(1.0 - a * a) * ((1.0 + th_i) * xa2)
            hseq, hlast = _lru_scan(a.reshape(group, V7X_SUBLANES, blk),
                                    u.reshape(group, V7X_SUBLANES, blk), hcar_ref[:, c0:c1])
            hcar_ref[:, c0:c1] = hlast
            zg = sec(1, c0, c1)
            term_a.append(((1.0 + jnp.tanh(sec(4, c0, c1))) * one_plus_tanh_gelu(zg))
                          * (zg * hseq.reshape(rows, blk)))

        project(4)
        zv = sec(3, 0, d)
        gv2 = zv * one_plus_tanh_gelu(zv)
        project(5)
        mu = jnp.mean(gv2, axis=-1, keepdims=True)
        xc = gv2 - mu
        var = jnp.mean(xc * xc, axis=-1, keepdims=True)
        v_bf = (xc * lax.rsqrt(var + 4.0 * EPS) * lng_ref[...] + lnb_ref[...]).astype(BF16)
        project(6)
        gdim = d // SGU_GROUPS
        term_b = []
        for g in range(SGU_GROUPS):
            c0, c1 = g * gdim, (g + 1) * gdim
            if g in (1, 3, 5, 6, 7):
                project({1: 7, 3: 8, 5: 9, 6: 10, 7: 11}[g])
            sp = _dot(wsm_ref[g], v_bf[:, c0:c1]) + bsp_ref[:, g:g + 1]
            zu = sec(2, c0, c1)
            term_b.append(((1.0 + jnp.tanh(sec(5, c0, c1))) * one_plus_tanh_gelu(zu)) * (zu * sp))
        merged4 = jnp.concatenate(term_a, axis=1) + jnp.concatenate(term_b, axis=1)

        x1 = x + _dot(merged4.astype(BF16), wout_ref[...])

        hn = _rmsnorm(x1, ffn_ref[...])
        hp = _pack_bf16_pair(hn[:, :half], hn[:, half:])

        @pl.when(j >= 3)
        def _():
            for c in put(tb - 2):
                c.wait()

        x1buf[slot] = x1.reshape(group, V7X_SUBLANES, d)
        hpbuf[slot] = hp.reshape(group, V7X_SUBLANES, half)

        @pl.when(j >= 1)
        def _():
            for c in put(tb):
                c.start()

    @pl.when(lax.rem(j, 2) == 0)
    def _():
        compute(z0_ref, z1_ref)

    @pl.when(lax.rem(j, 2) == 1)
    def _():
        compute(z1_ref, z0_ref)

    @pl.when(j == ntile)
    def _():
        for c in put(tb):
            c.wait()

        @pl.when(ntile >= 2)
        def _():
            for c in put(tb - 1):
                c.wait()


def _mixer_call(x, mix_norm, w_in, conv_w, conv_b, wa_blk, ba, wi_blk, bi, lam, ln_g, ln_b, ws,
                bs_tile, w_out, ffn_norm):
    cfg = _tiles()
    bsz, seq, d = x.shape
    ts = cfg["mixer_rows"]
    group = ts // V7X_SUBLANES
    nseq = seq // ts
    ntile = bsz * nseq
    row1 = (1, d)
    in_specs = [
        pl.BlockSpec(memory_space=pl.ANY),
        _const_spec(row1),
        pl.BlockSpec(memory_space=pl.ANY),
        _const_spec(conv_w.shape), _const_spec(row1),
        _const_spec(wa_blk.shape), _const_spec(row1),
        _const_spec(wi_blk.shape), _const_spec(row1),
        _const_spec(row1),
        _const_spec(row1), _const_spec(row1),
        _const_spec(ws.shape), _const_spec(bs_tile.shape),
        pl.BlockSpec(memory_space=pl.ANY), _const_spec(row1),
    ]
    out_shape = [
        jax.ShapeDtypeStruct((bsz, seq, d), F32),
        jax.ShapeDtypeStruct((bsz, seq, d // 2), U32),
    ]
    out_specs = [
        pl.BlockSpec(memory_space=pl.ANY),
        pl.BlockSpec(memory_space=pl.ANY),
    ]
    scratch = [
        pltpu.VMEM((3, group, V7X_SUBLANES, d), F32),
        pltpu.VMEM((ts, w_in.shape[1]), F32),
        pltpu.VMEM((ts, w_in.shape[1]), F32),
        pltpu.VMEM((2, group, V7X_SUBLANES, d), F32),
        pltpu.VMEM((2, group, V7X_SUBLANES, d // 2), U32),
        pltpu.SemaphoreType.DMA((3,)),
        pltpu.SemaphoreType.DMA((2,)),
        pltpu.SemaphoreType.DMA((2,)),
        pltpu.VMEM((SGU_GROUPS, ts, ts), BF16),
        pltpu.VMEM((CONV_WIDTH - 1, V7X_SUBLANES, d), F32),
        pltpu.VMEM((1, d), F32),
        pltpu.VMEM(w_in.shape, BF16),
        pltpu.VMEM(w_out.shape, BF16),
        pltpu.SemaphoreType.DMA((2,)),
    ]
    return pl.pallas_call(
        functools.partial(_mixer_kernel, nseq=nseq),
        grid=(ntile + 1,),
        in_specs=in_specs,
        out_specs=out_specs,
        out_shape=out_shape,
        scratch_shapes=scratch,
        compiler_params=pltpu.CompilerParams(
            dimension_semantics=("arbitrary",),
            vmem_limit_bytes=cfg["mixer_vmem"]),
        name="mixer",
    )(x, mix_norm, w_in, conv_w, conv_b, wa_blk, ba, wi_blk, bi, lam, ln_g, ln_b, ws, bs_tile,
      w_out, ffn_norm)


def _router_kernel(hp_ref, wg_ref, we_ref, br_ref, pos_ref, gate_ref, cnt_ref, ccar_ref, wr_ref,
                   *, expert_capacity):
    rows = hp_ref.shape[0]

    @pl.when(pl.program_id(0) == 0)
    def _():
        ccar_ref[...] = jnp.zeros_like(ccar_ref)
        wr_ref[...] = jnp.zeros_like(wr_ref)
        wr_ref[:, 0:N_GROUPS] = wg_ref[...].astype(BF16)
        wr_ref[:, EXPERT_ROW0:EXPERT_ROW0 + N_EXPERTS] = we_ref[...].astype(BF16)

    lo, hi = _unpack_bf16_pair(hp_ref[...])
    hn = jnp.concatenate([lo, hi], axis=1)
    logits = _dot(hn.astype(BF16), wr_ref[...])
    lt = jnp.transpose(logits) + br_ref[...]
    sub = lax.broadcasted_iota(jnp.int32, (V7X_SUBLANES, rows), 0)
    subf = sub.astype(F32)
    big = jnp.float32(1e9)

    lg = jnp.where(sub < N_GROUPS, lt[0:V7X_SUBLANES, :], -jnp.inf)
    g_exp = jnp.exp(lg - jnp.max(lg, axis=0, keepdims=True))
    g_prob = g_exp / jnp.sum(g_exp, axis=0, keepdims=True)
    g_top = jnp.max(g_prob, axis=0, keepdims=True)
    g_idx = jnp.min(jnp.where(g_prob == g_top, subf, big), axis=0, keepdims=True)

    e_sel = jnp.zeros((EXPERTS_PER_GROUP, rows), F32)
    for g in range(N_GROUPS):
        r0 = EXPERT_ROW0 + g * EXPERTS_PER_GROUP
        e_sel = jnp.where(g_idx == g, lt[r0:r0 + EXPERTS_PER_GROUP, :], e_sel)
    e_exp = jnp.exp(e_sel - jnp.max(e_sel, axis=0, keepdims=True))
    e_prob = e_exp / jnp.sum(e_exp, axis=0, keepdims=True)
    p1 = jnp.max(e_prob, axis=0, keepdims=True)
    i1 = jnp.min(jnp.where(e_prob == p1, subf, big), axis=0, keepdims=True)
    rest = jnp.where(subf == i1, -1.0, e_prob)
    p2 = jnp.max(rest, axis=0, keepdims=True)
    i2 = jnp.min(jnp.where(rest == p2, subf, big), axis=0, keepdims=True)
    psum = p1 + p2
    gate1 = g_top * (p1 / psum)
    gate2 = g_top * (p2 / psum)
    gid1 = g_idx * EXPERTS_PER_GROUP + i1
    gid2 = g_idx * EXPERTS_PER_GROUP + i2

    eid = lax.broadcasted_iota(jnp.int32, (N_EXPERTS, rows), 0).astype(F32)
    hit1 = eid == gid1
    hit2 = eid == gid2
    cnt = jnp.where(hit1 | hit2, 1.0, 0.0)
    sb = V7X_MXU_DIM
    before = (lax.broadcasted_iota(jnp.int32, (sb, sb), 0)
              < lax.broadcasted_iota(jnp.int32, (sb, sb), 1))
    before = jnp.where(before, 1.0, 0.0).astype(BF16)
    running = ccar_ref[:, 0:1]
    base = []
    for q in range(rows // sb):
        part = cnt[:, q * sb:(q + 1) * sb]
        base.append(running + _dot(part.astype(BF16), before))
        running = running + jnp.sum(part, axis=1, keepdims=True)
    base = jnp.concatenate(base, axis=1)
    rank1 = jnp.sum(jnp.where(hit1, base, 0.0), axis=0, keepdims=True)
    rank2 = jnp.sum(jnp.where(hit2, base, 0.0), axis=0, keepdims=True)
    total = jnp.broadcast_to(running, ccar_ref.shape)
    ccar_ref[...] = total
    cnt_ref[...] = total
    cap = float(expert_capacity)
    zero = jnp.zeros((V7X_SUBLANES - TOP_K, rows), F32)
    pos = jnp.concatenate([gid1 * cap + rank1, gid2 * cap + rank2, zero], axis=0)
    pos_ref[...] = pos.astype(jnp.int32)
    gate_ref[...] = jnp.transpose(jnp.concatenate([gate1, gate2, zero], axis=0))


def _router_call(hp, w_group, w_expert, b_router):
    cfg = _tiles()
    ntok, half = hp.shape
    tr = cfg["router_rows"]
    return pl.pallas_call(
        functools.partial(_router_kernel, expert_capacity=ntok),
        grid=(ntok // tr,),
        in_specs=[
            pl.BlockSpec((tr, half), lambda i: (i, 0)),
            _const_spec(w_group.shape),
            _const_spec(w_expert.shape),
            _const_spec(b_router.shape),
        ],
        out_specs=[
            pl.BlockSpec((V7X_SUBLANES, tr), lambda i: (0, i)),
            pl.BlockSpec((tr, V7X_SUBLANES), lambda i: (i, 0)),
            pl.BlockSpec((N_EXPERTS, V7X_LANES), lambda i: (0, 0)),
        ],
        out_shape=[
            jax.ShapeDtypeStruct((V7X_SUBLANES, ntok), jnp.int32),
            jax.ShapeDtypeStruct((ntok, V7X_SUBLANES), F32),
            jax.ShapeDtypeStruct((N_EXPERTS, V7X_LANES), F32),
        ],
        scratch_shapes=[
            pltpu.VMEM((N_EXPERTS, V7X_LANES), F32),
            pltpu.VMEM((w_group.shape[0], ROUTER_ROWS), BF16),
        ],
        compiler_params=pltpu.CompilerParams(
            dimension_semantics=("arbitrary",),
            vmem_limit_bytes=cfg["router_vmem"]),
        name="router",
    )(hp, w_group, w_expert, b_router)


def _expert_kernel(nt_ref, base_ref, texp_ref, tloc_ref, hs_hbm, w1_ref, w3_ref, w2_ref, ys_hbm,
                   hbuf, ybuf, hsem, ysem, w1b_ref, w3b_ref, w2b_ref, *, capacity):
    e = pl.program_id(0)
    n_exp = pl.num_programs(0)
    nt = nt_ref[e]
    base = base_ref[e]
    total = base_ref[n_exp - 1] + nt_ref[n_exp - 1]
    n_in, tm, _ = hbuf.shape
    n_out = ybuf.shape[0]
    ahead = n_in - 1

    def load(g):
        slot = lax.rem(g, n_in)
        rows = pl.ds(texp_ref[g] * capacity + tloc_ref[g] * tm, tm)
        return pltpu.make_async_copy(hs_hbm.at[rows], hbuf.at[slot], hsem.at[slot])

    def store(t, slot):
        rows = pl.ds(e * capacity + t * tm, tm)
        return pltpu.make_async_copy(ybuf.at[slot], ys_hbm.at[rows], ysem.at[slot])

    @pl.when(e == 0)
    def _():
        for g0 in range(ahead):
            @pl.when(g0 < total)
            def _():
                load(g0).start()

    w1b_ref[...] = w1_ref[...].astype(BF16)
    w3b_ref[...] = w3_ref[...].astype(BF16)
    w2b_ref[...] = w2_ref[...].astype(BF16)

    @pl.loop(0, nt)
    def _(t):
        g = base + t

        @pl.when(g + ahead < total)
        def _():
            load(g + ahead).start()

        load(g).wait()
        lo, hi = _unpack_bf16_pair(hbuf[lax.rem(g, n_in)])
        h = jnp.concatenate([lo, hi], axis=1).astype(BF16)
        a = _dot(h, w1b_ref[...])
        b = _dot(h, w3b_ref[...])
        hid = (a * _sigmoid(a)) * b
        y = _dot(hid.astype(BF16), w2b_ref[...])
        half = y.shape[1] // 2
        slot = lax.rem(g, n_out)

        @pl.when(g >= n_out)
        def _():
            store(t, slot).wait()

        ybuf[slot] = _pack_bf16_pair(y[:, :half], y[:, half:])
        store(t, slot).start()

    @pl.when(e + 1 == n_exp)
    def _():
        for back in range(1, n_out + 1):
            @pl.when(total >= back)
            def _():
                store(0, lax.rem(total - back, n_out)).wait()


def _expert_call(tiles_per_expert, hs, w1, w3, w2, capacity):
    cfg = _tiles()
    tm = cfg["expert_rows"]
    prow, half = hs.shape
    n_exp, d, f = w1.shape
    ends = jnp.cumsum(tiles_per_expert)
    base = ends - tiles_per_expert
    g = jnp.arange(capacity * TOP_K // tm + n_exp, dtype=jnp.int32)
    texp = jnp.minimum(jnp.sum((ends[None, :] <= g[:, None]).astype(jnp.int32), axis=1), n_exp - 1)
    onehot = texp[:, None] == jnp.arange(n_exp, dtype=jnp.int32)[None, :]
    tloc = g - jnp.sum(jnp.where(onehot, base[None, :], 0), axis=1)

    def w_map(e, *_):
        return (e, 0, 0)

    grid_spec = pltpu.PrefetchScalarGridSpec(
        num_scalar_prefetch=4,
        grid=(n_exp,),
        in_specs=[
            pl.BlockSpec(memory_space=pl.ANY),
            pl.BlockSpec((None, d, f), w_map),
            pl.BlockSpec((None, d, f), w_map),
            pl.BlockSpec((None, f, d), w_map),
        ],
        out_specs=pl.BlockSpec(memory_space=pl.ANY),
        scratch_shapes=[
            pltpu.VMEM((EXPERT_LOOKAHEAD + 1, tm, half), U32),
            pltpu.VMEM((2, tm, half), U32),
            pltpu.SemaphoreType.DMA((EXPERT_LOOKAHEAD + 1,)),
            pltpu.SemaphoreType.DMA((2,)),
            pltpu.VMEM((d, f), BF16),
            pltpu.VMEM((d, f), BF16),
            pltpu.VMEM((f, d), BF16),
        ],
    )
    return pl.pallas_call(
        functools.partial(_expert_kernel, capacity=capacity),
        grid_spec=grid_spec,
        out_shape=jax.ShapeDtypeStruct((prow, half), U32),
        compiler_params=pltpu.CompilerParams(
            dimension_semantics=("arbitrary",),
            vmem_limit_bytes=cfg["expert_vmem"]),
        name="experts",
    )(tiles_per_expert, base, texp, tloc, hs, w1, w3, w2)


def _sc_mesh():
    return plsc.VectorSubcoreMesh(core_axis_name="c", subcore_axis_name="s",
                                  num_cores=V7X_SC_CORES, num_subcores=V7X_SC_SUBCORES)


def _sc_worker_id():
    return lax.axis_index("s") * V7X_SC_CORES + lax.axis_index("c")


def _dispatch_call(hp, pos_w, out_rows):
    cfg = _tiles()
    ntok, half = hp.shape
    nw, topk, nch, ch = pos_w.shape
    per_w = nch * ch

    def body(hp_hbm, pos_hbm, hs_hbm, idx_v, buf, wsem):
        wid = _sc_worker_id()
        pltpu.sync_copy(pos_hbm.at[wid], idx_v)

        for c in range(nch):
            pltpu.sync_copy(hp_hbm.at[pl.ds(wid * per_w + c * ch, ch)], buf)
            writes = [pltpu.make_async_copy(buf, hs_hbm.at[idx_v.at[k, c]], wsem.at[k]) for k in range(topk)]
            for w in writes:
                w.start()
            for w in writes:
                w.wait()

    assert nw == V7X_SC_CORES * V7X_SC_SUBCORES and nw * per_w == ntok and ch == cfg["sc_rows"]
    return pl.kernel(
        body,
        out_type=jax.ShapeDtypeStruct((out_rows, half), U32),
        mesh=_sc_mesh(),
        scratch_types=[
            pltpu.VMEM((topk, nch, ch), jnp.int32),
            pltpu.VMEM((ch, half), U32),
            pltpu.SemaphoreType.DMA((topk,)),
        ],
        name="dispatch",
    )(hp, pos_w)


def _combine_call(ys, pos_w):
    cfg = _tiles()
    _, half = ys.shape
    nw, topk, nch, ch = pos_w.shape
    per_w = nch * ch
    ntok = nw * per_w

    def body(ys_hbm, pos_hbm, *rest):
        outs = rest[:topk]
        idx_v, buf = rest[topk:]
        wid = _sc_worker_id()
        pltpu.sync_copy(pos_hbm.at[wid], idx_v)
        for c in range(nch):
            for k in range(topk):
                pltpu.sync_copy(ys_hbm.at[idx_v.at[k, c]], buf)
                pltpu.sync_copy(buf, outs[k].at[pl.ds(wid * per_w + c * ch, ch)])

    assert nw == V7X_SC_CORES * V7X_SC_SUBCORES and ch == cfg["sc_rows"]
    return pl.kernel(
        body,
        out_type=[jax.ShapeDtypeStruct((ntok, half), U32)] * topk,
        mesh=_sc_mesh(),
        scratch_types=[
            pltpu.VMEM((topk, nch, ch), jnp.int32),
            pltpu.VMEM((ch, half), U32),
        ],
        name="combine",
    )(ys, pos_w)


def _ple_kernel(x1_ref, yg0_ref, yg1_ref, gate_ref, p_ref, plen_ref, wg32_ref, wu32_ref, fin_ref, o_ref,
                wg_ref, wu_ref):
    @pl.when(pl.program_id(0) == 0)
    def _():
        wg_ref[...] = wg32_ref[...].astype(BF16)
        wu_ref[...] = wu32_ref[...].astype(BF16)

    rows = x1_ref.shape[0]
    sub_rows = rows // PLE_SUBBLOCKS
    for q in range(PLE_SUBBLOCKS):
        rs = pl.ds(q * sub_rows, sub_rows)
        lo0, hi0 = _unpack_bf16_pair(yg0_ref[rs, :])
        lo1, hi1 = _unpack_bf16_pair(yg1_ref[rs, :])
        g0 = gate_ref[rs, 0:1]
        g1 = gate_ref[rs, 1:2]
        moe = g0 * jnp.concatenate([lo0, hi0], axis=1) + g1 * jnp.concatenate([lo1, hi1], axis=1)
        x2 = x1_ref[rs, :] + moe
        r = _rmsnorm(x2, plen_ref[...]).astype(BF16)
        gt = _sigmoid(_dot(r, wg_ref[...]))
        up = _dot(p_ref[rs, :].astype(BF16), wu_ref[...])
        x3 = x2 + gt * up
        o_ref[rs, :] = _rmsnorm(x3, fin_ref[...])


def _ple_call(x1, yg0, yg1, gates, p, ple_norm, wg, wu, final_norm):
    cfg = _tiles()
    ntok, d = x1.shape
    tp = cfg["ple_rows"]
    pdim = p.shape[1]
    return pl.pallas_call(
        _ple_kernel,
        grid=(ntok // tp,),
        in_specs=[
            pl.BlockSpec((tp, d), lambda i: (i, 0)),
            pl.BlockSpec((tp, d // 2), lambda i: (i, 0)),
            pl.BlockSpec((tp, d // 2), lambda i: (i, 0)),
            pl.BlockSpec((tp, V7X_SUBLANES), lambda i: (i, 0)),
            pl.BlockSpec((tp, pdim), lambda i: (i, 0)),
            _const_spec((1, d)),
            _const_spec(wg.shape),
            _const_spec(wu.shape),
            _const_spec((1, d)),
        ],
        out_specs=pl.BlockSpec((tp, d), lambda i: (i, 0)),
        out_shape=jax.ShapeDtypeStruct((ntok, d), F32),
        scratch_shapes=[pltpu.VMEM(wg.shape, BF16), pltpu.VMEM(wu.shape, BF16)],
        compiler_params=pltpu.CompilerParams(
            dimension_semantics=("arbitrary",),
            vmem_limit_bytes=cfg["ple_vmem"]),
        name="ple",
    )(x1, yg0, yg1, gates, p, ple_norm, wg, wu, final_norm)


def _blockdiag_pack(w):
    nb, bd, _ = w.shape
    per = V7X_MXU_DIM // bd
    w4 = w.reshape(nb // per, per, bd, bd)
    eye = jnp.eye(per, dtype=w.dtype)
    out = jnp.einsum("jpab,pq->jpaqb", w4, eye)
    return out.reshape(nb // per, V7X_MXU_DIM, V7X_MXU_DIM).astype(BF16)


def kernel(x, p, mix_norm, w_in, conv_w, conv_b, lru_wa, lru_ba, lru_wi, lru_bi, lru_lambda, sgu_ln_g, sgu_ln_b, sgu_ws, sgu_bs, w_out, ffn_norm, router_group_w, router_group_b, router_expert_w, router_expert_b, expert_w1, expert_w3, expert_w2, ple_norm, ple_gate_w, ple_up_w, final_norm):
    cfg = _tiles()
    bsz, seq, d = x.shape
    ntok = bsz * seq
    tm = cfg["expert_rows"]
    depth = w_in.shape[0]
    assert depth == 1, "the ple kernel applies the final norm, so it must be the last layer"
    l = 0
    nw_rows = V7X_SC_CORES * V7X_SC_SUBCORES * cfg["sc_rows"]
    assert cfg["mixer_rows"] % CHUNK == 0 and seq % cfg["mixer_rows"] == 0
    assert ntok % cfg["router_rows"] == 0 and ntok % cfg["ple_rows"] == 0 and ntok % nw_rows == 0
    assert ntok % tm == 0 and lru_wa.shape[1:] == (LRU_BLOCKS, d // LRU_BLOCKS, d // LRU_BLOCKS)
    assert max(cfg[k] for k in cfg if k.endswith("_vmem")) < V7X_VMEM_BYTES
    b_router = jnp.concatenate([
        router_group_b[l], jnp.zeros((EXPERT_ROW0 - N_GROUPS,), F32), router_expert_b[l],
        jnp.zeros((ROUTER_ROWS - EXPERT_ROW0 - N_EXPERTS,), F32)])[:, None]
    ts = cfg["mixer_rows"]
    group = ts // V7X_SUBLANES
    bs_tile = jnp.tile(sgu_bs[l], (1, ts // CHUNK)).reshape(SGU_GROUPS, V7X_SUBLANES, group)
    bs_tile = jnp.transpose(bs_tile, (2, 1, 0)).reshape(ts, SGU_GROUPS)
    x1, hp = _mixer_call(
        x, mix_norm[l][None], w_in[l], conv_w[l], conv_b[l][None],
        _blockdiag_pack(lru_wa[l]), lru_ba[l][None], _blockdiag_pack(lru_wi[l]), lru_bi[l][None],
        lru_lambda[l][None], sgu_ln_g[l][None], sgu_ln_b[l][None], sgu_ws[l], bs_tile,
        w_out[l], ffn_norm[l][None])
    hp = hp.reshape(ntok, d // 2)
    pos, gate, cnt = _router_call(hp, router_group_w[l], router_expert_w[l], b_router)

    cap = ntok
    tiles_per_expert = (cnt[:, 0].astype(jnp.int32) + tm - 1) // tm
    nw = V7X_SC_CORES * V7X_SC_SUBCORES
    ch = cfg["sc_rows"]
    pos_w = jnp.transpose(pos[:TOP_K].reshape(TOP_K, nw, ntok // (nw * ch), ch), (1, 0, 2, 3))

    hs = _dispatch_call(hp, pos_w, N_EXPERTS * cap)
    ys = _expert_call(tiles_per_expert, hs, expert_w1[l], expert_w3[l], expert_w2[l], cap)
    yg0, yg1 = _combine_call(ys, pos_w)

    out = _ple_call(x1.reshape(ntok, d), yg0, yg1, gate, p[l].reshape(ntok, -1), ple_norm[l][None],
                    ple_gate_w[l], ple_up_w[l], final_norm[None])
    return out.reshape(bsz, seq, d)
```

```python
import functools

import jax
import jax.numpy as jnp
from jax import lax
from jax.experimental import pallas as pl
from jax.experimental.pallas import tpu as pltpu
from jax.experimental.pallas import tpu_sc as plsc

F32 = jnp.float32
BF16 = jnp.bfloat16
U32 = jnp.uint32

LRU_BLOCKS = 16
CONV_WIDTH = 4
LRU_C = 8.0
SGU_GROUPS = 8
CHUNK = 128
N_GROUPS = 4
EXPERTS_PER_GROUP = 8
N_EXPERTS = N_GROUPS * EXPERTS_PER_GROUP
TOP_K = 2
EPS = 1e-6

V7X_MXU_DIM = 256
V7X_SUBLANES = 8
V7X_LANES = 128
V7X_VMEM_BYTES = 64 * 1024 * 1024
V7X_SC_CORES = 2
V7X_SC_SUBCORES = 16

EXPERT_LOOKAHEAD = 3
EXPERT_SUBBLOCKS = 2
PLE_SUBBLOCKS = 4
ROUTER_ROWS = V7X_LANES
EXPERT_ROW0 = V7X_SUBLANES


def _tiles():
    return dict(
        mixer_rows=256,
        expert_rows=512,
        ple_rows=1024,
        sc_rows=128,
        router_rows=2048,
        mixer_vmem=52 * 1024 * 1024,
        expert_vmem=40 * 1024 * 1024,
        ple_vmem=48 * 1024 * 1024,
        router_vmem=32 * 1024 * 1024,
    )


def _dot(a, b):
    return jnp.dot(a, b, preferred_element_type=F32)


def _sigmoid(x):
    return 0.5 * jnp.tanh(0.5 * x) + 0.5


def _rmsnorm(x, g):
    ms = jnp.mean(x * x, axis=-1, keepdims=True)
    return x * lax.rsqrt(ms + EPS) * g


def _pack_bf16_pair(lo, hi):
    lo_b = lax.bitcast_convert_type(lo.astype(BF16).astype(F32), U32)
    hi_b = lax.bitcast_convert_type(hi.astype(BF16).astype(F32), U32)
    return (hi_b & jnp.uint32(0xFFFF0000)) | lax.shift_right_logical(lo_b, jnp.uint32(16))


def _unpack_bf16_pair(w):
    lo = lax.bitcast_convert_type(lax.shift_left(w, jnp.uint32(16)), F32)
    hi = lax.bitcast_convert_type(w & jnp.uint32(0xFFFF0000), F32)
    return lo, hi


def _const_spec(shape):
    zeros = (0,) * len(shape)
    return pl.BlockSpec(shape, lambda *_: zeros, pipeline_mode=pl.Buffered(1))


def _tile_copies(hbm, buf, sem, b, row0, slot, to_hbm):
    group = buf.shape[1]
    copies = []
    for r in range(V7X_SUBLANES):
        hbm_rows = hbm.at[b, pl.ds(row0 + group * r, group), :]
        vmem_rows = buf.at[slot, :, r, :]
        src, dst = (vmem_rows, hbm_rows) if to_hbm else (hbm_rows, vmem_rows)
        copies.append(pltpu.make_async_copy(src, dst, sem.at[slot]))
    return copies


def _lru_scan(a, u, h0):
    group = a.shape[0]
    acc_a = [a[0]]
    acc_u = [u[0]]
    for g in range(1, group):
        acc_a.append(a[g] * acc_a[-1])
        acc_u.append(a[g] * acc_u[-1] + u[g])
    end_a, end_u = acc_a[-1], acc_u[-1]
    sub = lax.broadcasted_iota(jnp.int32, end_a.shape, 0)
    shift = 1
    while shift < V7X_SUBLANES:
        keep = sub >= shift
        a_sh = pltpu.roll(end_a, shift, axis=0)
        u_sh = pltpu.roll(end_u, shift, axis=0)
        end_u = jnp.where(keep, end_a * u_sh + end_u, end_u)
        end_a = jnp.where(keep, end_a * a_sh, end_a)
        shift *= 2
    h_end = end_a * h0 + end_u
    h_in = jnp.where(sub == 0, h0, pltpu.roll(h_end, 1, axis=0))
    out = [acc_a[g] * h_in + acc_u[g] for g in range(group)]
    return jnp.stack(out, axis=0), h_end[V7X_SUBLANES - 1:V7X_SUBLANES, :]


def _mixer_kernel(x_hbm, mixn_ref, win_hbm, convw_ref, convb_ref, wa_ref, ba_ref, wi_ref, bi_ref,
                  lam_ref, lng_ref, lnb_ref, ws_ref, bsp_ref, wout_hbm, ffn_ref,
                  x1_hbm, hp_hbm,
                  xbuf, z0_ref, z1_ref, x1buf, hpbuf, xsem, x1sem, hpsem, wsm_ref, ztail_ref, hcar_ref,
                  win_ref, wout_ref, wsem,
                  *, nseq):
    j = pl.program_id(0)
    ntile = pl.num_programs(0) - 1
    _, group, _, d = xbuf.shape
    rows = group * V7X_SUBLANES
    half = d // 2
    ta = jnp.minimum(j, ntile - 1)
    tb = jnp.maximum(j - 1, 0)
    s = lax.rem(tb, nseq)
    slot = lax.rem(tb, 2)

    def fetch(t):
        return _tile_copies(x_hbm, xbuf, xsem, lax.div(t, nseq), lax.rem(t, nseq) * rows,
                            lax.rem(t, 3), to_hbm=False)

    def put(t):
        tb_, ts_, sl = lax.div(t, nseq), lax.rem(t, nseq) * rows, lax.rem(t, 2)
        return (_tile_copies(x1_hbm, x1buf, x1sem, tb_, ts_, sl, to_hbm=True)
                + _tile_copies(hp_hbm, hpbuf, hpsem, tb_, ts_, sl, to_hbm=True))

    @pl.when(j == 0)
    def _():
        for c in fetch(0):
            c.start()
        stage = (z0_ref, z1_ref)
        n_in_chunks = win_hbm.shape[0] // rows

        def win_copy(c):
            return pltpu.make_async_copy(win_hbm.at[pl.ds(c * rows, rows), :], stage[c % 2], wsem.at[c % 2])

        win_copy(0).start()
        for c in range(n_in_chunks):
            if c + 1 < n_in_chunks:
                win_copy(c + 1).start()
            win_copy(c).wait()
            plain = 4 * d
            win_ref[c * rows:(c + 1) * rows, :plain] = stage[c % 2][:, :plain].astype(BF16)
            win_ref[c * rows:(c + 1) * rows, plain:] = (0.5 * stage[c % 2][:, plain:]).astype(BF16)
        n_out_chunks = wout_hbm.shape[0] // rows
        out_copies = [pltpu.make_async_copy(wout_hbm.at[pl.ds(c * rows, rows), :],
                                            z0_ref.at[:, c * d:(c + 1) * d], wsem.at[0])
                      for c in range(n_out_chunks)]
        for cp in out_copies:
            cp.start()
        for cp in out_copies:
            cp.wait()
        for c in range(n_out_chunks):
            wout_ref[c * rows:(c + 1) * rows, :] = (0.25 * z0_ref[:, c * d:(c + 1) * d]).astype(BF16)
        z1_ref[...] = jnp.zeros_like(z1_ref)
        i_idx = lax.broadcasted_iota(jnp.int32, (rows, rows), 0)
        j_idx = lax.broadcasted_iota(jnp.int32, (rows, rows), 1)
        t_i = group * lax.rem(i_idx, V7X_SUBLANES) + lax.div(i_idx, V7X_SUBLANES)
        t_j = group * lax.rem(j_idx, V7X_SUBLANES) + lax.div(j_idx, V7X_SUBLANES)
        keep = (t_i >= t_j) & (lax.div(t_i, CHUNK) == lax.div(t_j, CHUNK))
        pick_rows = jnp.where(t_i == j_idx, 1.0, 0.0).astype(BF16)
        pick_cols = jnp.where(i_idx == t_j, 1.0, 0.0).astype(BF16)
        reps = rows // CHUNK
        for g in range(SGU_GROUPS):
            w_chunk = ws_ref[g].astype(BF16)
            w_rows = jnp.concatenate([w_chunk] * reps, axis=1)
            w_full = jnp.concatenate([w_rows] * reps, axis=0)
            w_perm = _dot(_dot(pick_rows, w_full).astype(BF16), pick_cols)
            wsm_ref[g] = jnp.where(keep, w_perm, 0.0).astype(BF16)

    @pl.when(j + 1 < ntile)
    def _():
        for c in fetch(j + 1):
            c.start()

    @pl.when(j < ntile)
    def _():
        for c in fetch(j):
            c.wait()

    @pl.when(s == 0)
    def _():
        ztail_ref[...] = jnp.zeros_like(ztail_ref)
        hcar_ref[...] = jnp.zeros_like(hcar_ref)

    def compute(z_w, z_r):
        xa_in = xbuf[lax.rem(ta, 3)].reshape(rows, d)
        h_next = _rmsnorm(xa_in, mixn_ref[...]).astype(BF16)
        pw = d // 2

        def project(k):
            z_w[:, k * pw:(k + 1) * pw] = _dot(h_next, win_ref[:, k * pw:(k + 1) * pw])

        x = xbuf[lax.rem(tb, 3)].reshape(rows, d)

        def sec(k, c0, c1):
            return z_r[:, k * d + c0:k * d + c1]

        def one_plus_tanh_gelu(v):
            c = 0.7978845608028654
            return 1.0 + jnp.tanh(v * (c + (c * 0.044715) * (v * v)))

        cw = 0.5 * convw_ref[...]
        cb_h = 0.5 * convb_ref[...]
        ba_h = 0.5 * ba_ref[...]
        bi_h = 0.5 * bi_ref[...]
        neg_lam = -lam_ref[...]
        softplus = jnp.maximum(neg_lam, 0.0) + jnp.log1p(jnp.exp(-jnp.abs(neg_lam)))
        c_a = (-0.5 * LRU_C) * softplus
        blk = V7X_MXU_DIM
        sub3 = lax.broadcasted_iota(jnp.int32, (CONV_WIDTH - 1, V7X_SUBLANES, blk), 1)
        term_a = []
        for n in range(d // blk):
            project(n)
            c0, c1 = n * blk, (n + 1) * blk
            z3 = sec(0, c0, c1).reshape(group, V7X_SUBLANES, blk)
            tail = z3[group - (CONV_WIDTH - 1):]
            halo = jnp.where(sub3 == 0, pltpu.roll(ztail_ref[:, :, c0:c1], 1, axis=1),
                             pltpu.roll(tail, 1, axis=1))
            ztail_ref[:, :, c0:c1] = tail
            zext = jnp.concatenate([halo, z3], axis=0)
            xa_h = cb_h[:, c0:c1] + cw[CONV_WIDTH - 1:CONV_WIDTH, c0:c1] * z3
            for k in range(1, CONV_WIDTH):
                lo = CONV_WIDTH - 1 - k
                xa_h = xa_h + cw[lo:lo + 1, c0:c1] * zext[lo:lo + group]
            xa2 = xa_h.reshape(rows, blk)
            xa_bf = xa2.astype(BF16)
            th_r = jnp.tanh(_dot(xa_bf, wa_ref[n]) + ba_h[:, c0:c1])
            th_i = jnp.tanh(_dot(xa_bf, wi_ref[n]) + bi_h[:, c0:c1])
            a = jnp.exp(c_a[:, c0:c1] + c_a[:, c0:c1] * th_r)
            u = jnp.sqrt(1.0 - a * a) * ((1.0 + th_i) * xa2)
            hseq, hlast = _lru_scan(a.reshape(group, V7X_SUBLANES, blk),
                                    u.reshape(group, V7X_SUBLANES, blk), hcar_ref[:, c0:c1])
            hcar_ref[:, c0:c1] = hlast
            zg = sec(1, c0, c1)
            term_a.append(((1.0 + jnp.tanh(sec(4, c0, c1))) * one_plus_tanh_gelu(zg))
                          * (zg * hseq.reshape(rows, blk)))

        project(4)
        zv = sec(3, 0, d)
        gv2 = zv * one_plus_tanh_gelu(zv)
        project(5)
        mu = jnp.mean(gv2, axis=-1, keepdims=True)
        xc = gv2 - mu
        var = jnp.mean(xc * xc, axis=-1, keepdims=True)
        v_bf = (xc * lax.rsqrt(var + 4.0 * EPS) * lng_ref[...] + lnb_ref[...]).astype(BF16)
        project(6)
        gdim = d // SGU_GROUPS
        term_b = []
        for g in range(SGU_GROUPS):
            c0, c1 = g * gdim, (g + 1) * gdim
            if g in (1, 3, 5, 6, 7):
                project({1: 7, 3: 8, 5: 9, 6: 10, 7: 11}[g])
            sp = _dot(wsm_ref[g], v_bf[:, c0:c1]) + bsp_ref[:, g:g + 1]
            zu = sec(2, c0, c1)
            term_b.append(((1.0 + jnp.tanh(sec(5, c0, c1))) * one_plus_tanh_gelu(zu)) * (zu * sp))
        merged4 = jnp.concatenate(term_a, axis=1) + jnp.concatenate(term_b, axis=1)

        x1 = x + _dot(merged4.astype(BF16), wout_ref[...])

        hn = _rmsnorm(x1, ffn_ref[...])
        hp = _pack_bf16_pair(hn[:, :half], hn[:, half:])

        @pl.when(j >= 3)
        def _():
            for c in put(tb - 2):
                c.wait()

        x1buf[slot] = x1.reshape(group, V7X_SUBLANES, d)
        hpbuf[slot] = hp.reshape(group, V7X_SUBLANES, half)

        @pl.when(j >= 1)
        def _():
            for c in put(tb):
                c.start()

    @pl.when(lax.rem(j, 2) == 0)
    def _():
        compute(z0_ref, z1_ref)

    @pl.when(lax.rem(j, 2) == 1)
    def _():
        compute(z1_ref, z0_ref)

    @pl.when(j == ntile)
    def _():
        for c in put(tb):
            c.wait()

        @pl.when(ntile >= 2)
        def _():
            for c in put(tb - 1):
                c.wait()


def _mixer_call(x, mix_norm, w_in, conv_w, conv_b, wa_blk, ba, wi_blk, bi, lam, ln_g, ln_b, ws,
                bs_tile, w_out, ffn_norm):
    cfg = _tiles()
    bsz, seq, d = x.shape
    ts = cfg["mixer_rows"]
    group = ts // V7X_SUBLANES
    nseq = seq // ts
    ntile = bsz * nseq
    row1 = (1, d)
    in_specs = [
        pl.BlockSpec(memory_space=pl.ANY),
        _const_spec(row1),
        pl.BlockSpec(memory_space=pl.ANY),
        _const_spec(conv_w.shape), _const_spec(row1),
        _const_spec(wa_blk.shape), _const_spec(row1),
        _const_spec(wi_blk.shape), _const_spec(row1),
        _const_spec(row1),
        _const_spec(row1), _const_spec(row1),
        _const_spec(ws.shape), _const_spec(bs_tile.shape),
        pl.BlockSpec(memory_space=pl.ANY), _const_spec(row1),
    ]
    out_shape = [
        jax.ShapeDtypeStruct((bsz, seq, d), F32),
        jax.ShapeDtypeStruct((bsz, seq, d // 2), U32),
    ]
    out_specs = [
        pl.BlockSpec(memory_space=pl.ANY),
        pl.BlockSpec(memory_space=pl.ANY),
    ]
    scratch = [
        pltpu.VMEM((3, group, V7X_SUBLANES, d), F32),
        pltpu.VMEM((ts, w_in.shape[1]), F32),
        pltpu.VMEM((ts, w_in.shape[1]), F32),
        pltpu.VMEM((2, group, V7X_SUBLANES, d), F32),
        pltpu.VMEM((2, group, V7X_SUBLANES, d // 2), U32),
        pltpu.SemaphoreType.DMA((3,)),
        pltpu.SemaphoreType.DMA((2,)),
        pltpu.SemaphoreType.DMA((2,)),
        pltpu.VMEM((SGU_GROUPS, ts, ts), BF16),
        pltpu.VMEM((CONV_WIDTH - 1, V7X_SUBLANES, d), F32),
        pltpu.VMEM((1, d), F32),
        pltpu.VMEM(w_in.shape, BF16),
        pltpu.VMEM(w_out.shape, BF16),
        pltpu.SemaphoreType.DMA((2,)),
    ]
    return pl.pallas_call(
        functools.partial(_mixer_kernel, nseq=nseq),
        grid=(ntile + 1,),
        in_specs=in_specs,
        out_specs=out_specs,
        out_shape=out_shape,
        scratch_shapes=scratch,
        compiler_params=pltpu.CompilerParams(
            dimension_semantics=("arbitrary",),
            vmem_limit_bytes=cfg["mixer_vmem"]),
        name="mixer",
    )(x, mix_norm, w_in, conv_w, conv_b, wa_blk, ba, wi_blk, bi, lam, ln_g, ln_b, ws, bs_tile,
      w_out, ffn_norm)


def _router_kernel(hp_ref, wg_ref, we_ref, br_ref, pos_ref, gate_ref, cnt_ref, ccar_ref, wr_ref,
                   *, expert_capacity):
    rows = hp_ref.shape[0]

    @pl.when(pl.program_id(0) == 0)
    def _():
        ccar_ref[...] = jnp.zeros_like(ccar_ref)
        wr_ref[...] = jnp.zeros_like(wr_ref)
        wr_ref[:, 0:N_GROUPS] = wg_ref[...].astype(BF16)
        wr_ref[:, EXPERT_ROW0:EXPERT_ROW0 + N_EXPERTS] = we_ref[...].astype(BF16)

    lo, hi = _unpack_bf16_pair(hp_ref[...])
    hn = jnp.concatenate([lo, hi], axis=1)
    logits = _dot(hn.astype(BF16), wr_ref[...])
    lt = jnp.transpose(logits) + br_ref[...]
    sub = lax.broadcasted_iota(jnp.int32, (V7X_SUBLANES, rows), 0)
    subf = sub.astype(F32)
    big = jnp.float32(1e9)

    lg = jnp.where(sub < N_GROUPS, lt[0:V7X_SUBLANES, :], -jnp.inf)
    g_exp = jnp.exp(lg - jnp.max(lg, axis=0, keepdims=True))
    g_prob = g_exp / jnp.sum(g_exp, axis=0, keepdims=True)
    g_top = jnp.max(g_prob, axis=0, keepdims=True)
    g_idx = jnp.min(jnp.where(g_prob == g_top, subf, big), axis=0, keepdims=True)

    e_sel = jnp.zeros((EXPERTS_PER_GROUP, rows), F32)
    for g in range(N_GROUPS):
        r0 = EXPERT_ROW0 + g * EXPERTS_PER_GROUP
        e_sel = jnp.where(g_idx == g, lt[r0:r0 + EXPERTS_PER_GROUP, :], e_sel)
    e_exp = jnp.exp(e_sel - jnp.max(e_sel, axis=0, keepdims=True))
    e_prob = e_exp / jnp.sum(e_exp, axis=0, keepdims=True)
    p1 = jnp.max(e_prob, axis=0, keepdims=True)
    i1 = jnp.min(jnp.where(e_prob == p1, subf, big), axis=0, keepdims=True)
    rest = jnp.where(subf == i1, -1.0, e_prob)
    p2 = jnp.max(rest, axis=0, keepdims=True)
    i2 = jnp.min(jnp.where(rest == p2, subf, big), axis=0, keepdims=True)
    psum = p1 + p2
    gate1 = g_top * (p1 / psum)
    gate2 = g_top * (p2 / psum)
    gid1 = g_idx * EXPERTS_PER_GROUP + i1
    gid2 = g_idx * EXPERTS_PER_GROUP + i2

    eid = lax.broadcasted_iota(jnp.int32, (N_EXPERTS, rows), 0).astype(F32)
    hit1 = eid == gid1
    hit2 = eid == gid2
    cnt = jnp.where(hit1 | hit2, 1.0, 0.0)
    sb = V7X_MXU_DIM
    before = (lax.broadcasted_iota(jnp.int32, (sb, sb), 0)
              < lax.broadcasted_iota(jnp.int32, (sb, sb), 1))
    before = jnp.where(before, 1.0, 0.0).astype(BF16)
    running = ccar_ref[:, 0:1]
    base = []
    for q in range(rows // sb):
        part = cnt[:, q * sb:(q + 1) * sb]
        base.append(running + _dot(part.astype(BF16), before))
        running = running + jnp.sum(part, axis=1, keepdims=True)
    base = jnp.concatenate(base, axis=1)
    rank1 = jnp.sum(jnp.where(hit1, base, 0.0), axis=0, keepdims=True)
    rank2 = jnp.sum(jnp.where(hit2, base, 0.0), axis=0, keepdims=True)
    total = jnp.broadcast_to(running, ccar_ref.shape)
    ccar_ref[...] = total
    cnt_ref[...] = total
    cap = float(expert_capacity)
    zero = jnp.zeros((V7X_SUBLANES - TOP_K, rows), F32)
    pos = jnp.concatenate([gid1 * cap + rank1, gid2 * cap + rank2, zero], axis=0)
    pos_ref[...] = pos.astype(jnp.int32)
    gate_ref[...] = jnp.transpose(jnp.concatenate([gate1, gate2, zero], axis=0))


def _router_call(hp, w_group, w_expert, b_router):
    cfg = _tiles()
    ntok, half = hp.shape
    tr = cfg["router_rows"]
    return pl.pallas_call(
        functools.partial(_router_kernel, expert_capacity=ntok),
        grid=(ntok // tr,),
        in_specs=[
            pl.BlockSpec((tr, half), lambda i: (i, 0)),
            _const_spec(w_group.shape),
            _const_spec(w_expert.shape),
            _const_spec(b_router.shape),
        ],
        out_specs=[
            pl.BlockSpec((V7X_SUBLANES, tr), lambda i: (0, i)),
            pl.BlockSpec((tr, V7X_SUBLANES), lambda i: (i, 0)),
            pl.BlockSpec((N_EXPERTS, V7X_LANES), lambda i: (0, 0)),
        ],
        out_shape=[
            jax.ShapeDtypeStruct((V7X_SUBLANES, ntok), jnp.int32),
            jax.ShapeDtypeStruct((ntok, V7X_SUBLANES), F32),
            jax.ShapeDtypeStruct((N_EXPERTS, V7X_LANES), F32),
        ],
        scratch_shapes=[
            pltpu.VMEM((N_EXPERTS, V7X_LANES), F32),
            pltpu.VMEM((w_group.shape[0], ROUTER_ROWS), BF16),
        ],
        compiler_params=pltpu.CompilerParams(
            dimension_semantics=("arbitrary",),
            vmem_limit_bytes=cfg["router_vmem"]),
        name="router",
    )(hp, w_group, w_expert, b_router)


def _expert_kernel(nt_ref, base_ref, texp_ref, tloc_ref, hs_hbm, w1_ref, w3_ref, w2_ref, ys_hbm,
                   hbuf, ybuf, hsem, ysem, w1b_ref, w3b_ref, w2b_ref, *, capacity):
    e = pl.program_id(0)
    n_exp = pl.num_programs(0)
    nt = nt_ref[e]
    base = base_ref[e]
    total = base_ref[n_exp - 1] + nt_ref[n_exp - 1]
    n_in, tm, _ = hbuf.shape
    n_out = ybuf.shape[0]
    ahead = n_in - 1

    def load(g):
        slot = lax.rem(g, n_in)
        rows = pl.ds(texp_ref[g] * capacity + tloc_ref[g] * tm, tm)
        return pltpu.make_async_copy(hs_hbm.at[rows], hbuf.at[slot], hsem.at[slot])

    def store(t, slot):
        rows = pl.ds(e * capacity + t * tm, tm)
        return pltpu.make_async_copy(ybuf.at[slot], ys_hbm.at[rows], ysem.at[slot])

    @pl.when(e == 0)
    def _():
        for g0 in range(ahead):
            @pl.when(g0 < total)
            def _():
                load(g0).start()

    w1b_ref[...] = w1_ref[...].astype(BF16)
    w3b_ref[...] = w3_ref[...].astype(BF16)
    w2b_ref[...] = w2_ref[...].astype(BF16)

    @pl.loop(0, nt)
    def _(t):
        g = base + t

        @pl.when(g + ahead < total)
        def _():
            load(g + ahead).start()

        load(g).wait()
        slot = lax.rem(g, n_out)

        @pl.when(g >= n_out)
        def _():
            store(t, slot).wait()

        sub_rows = tm // EXPERT_SUBBLOCKS
        blocks = [pl.ds(q * sub_rows, sub_rows) for q in range(EXPERT_SUBBLOCKS)]
        rows_in = []
        for rs in blocks:
            lo, hi = _unpack_bf16_pair(hbuf[lax.rem(g, n_in), rs, :])
            rows_in.append(jnp.concatenate([lo, hi], axis=1).astype(BF16))
        up = [(_dot(h, w1b_ref[...]), _dot(h, w3b_ref[...])) for h in rows_in]
        down = []
        for a, b in up:
            hid = (a * _sigmoid(a)) * b
            down.append(_dot(hid.astype(BF16), w2b_ref[...]))
        for rs, y in zip(blocks, down):
            half = y.shape[1] // 2
            ybuf[slot, rs, :] = _pack_bf16_pair(y[:, :half], y[:, half:])
        store(t, slot).start()

    @pl.when(e + 1 == n_exp)
    def _():
        for back in range(1, n_out + 1):
            @pl.when(total >= back)
            def _():
                store(0, lax.rem(total - back, n_out)).wait()


def _expert_call(tiles_per_expert, hs, w1, w3, w2, capacity):
    cfg = _tiles()
    tm = cfg["expert_rows"]
    prow, half = hs.shape
    n_exp, d, f = w1.shape
    ends = jnp.cumsum(tiles_per_expert)
    base = ends - tiles_per_expert
    g = jnp.arange(capacity * TOP_K // tm + n_exp, dtype=jnp.int32)
    texp = jnp.minimum(jnp.sum((ends[None, :] <= g[:, None]).astype(jnp.int32), axis=1), n_exp - 1)
    onehot = texp[:, None] == jnp.arange(n_exp, dtype=jnp.int32)[None, :]
    tloc = g - jnp.sum(jnp.where(onehot, base[None, :], 0), axis=1)

    def w_map(e, *_):
        return (e, 0, 0)

    grid_spec = pltpu.PrefetchScalarGridSpec(
        num_scalar_prefetch=4,
        grid=(n_exp,),
        in_specs=[
            pl.BlockSpec(memory_space=pl.ANY),
            pl.BlockSpec((None, d, f), w_map),
            pl.BlockSpec((None, d, f), w_map),
            pl.BlockSpec((None, f, d), w_map),
        ],
        out_specs=pl.BlockSpec(memory_space=pl.ANY),
        scratch_shapes=[
            pltpu.VMEM((EXPERT_LOOKAHEAD + 1, tm, half), U32),
            pltpu.VMEM((2, tm, half), U32),
            pltpu.SemaphoreType.DMA((EXPERT_LOOKAHEAD + 1,)),
            pltpu.SemaphoreType.DMA((2,)),
            pltpu.VMEM((d, f), BF16),
            pltpu.VMEM((d, f), BF16),
            pltpu.VMEM((f, d), BF16),
        ],
    )
    return pl.pallas_call(
        functools.partial(_expert_kernel, capacity=capacity),
        grid_spec=grid_spec,
        out_shape=jax.ShapeDtypeStruct((prow, half), U32),
        compiler_params=pltpu.CompilerParams(
            dimension_semantics=("arbitrary",),
            vmem_limit_bytes=cfg["expert_vmem"]),
        name="experts",
    )(tiles_per_expert, base, texp, tloc, hs, w1, w3, w2)


def _sc_mesh():
    return plsc.VectorSubcoreMesh(core_axis_name="c", subcore_axis_name="s",
                                  num_cores=V7X_SC_CORES, num_subcores=V7X_SC_SUBCORES)


def _sc_worker_id():
    return lax.axis_index("s") * V7X_SC_CORES + lax.axis_index("c")


def _dispatch_call(hp, pos_w, out_rows):
    cfg = _tiles()
    ntok, half = hp.shape
    nw, topk, nch, ch = pos_w.shape
    per_w = nch * ch

    def body(hp_hbm, pos_hbm, hs_hbm, idx_v, buf, wsem):
        wid = _sc_worker_id()
        pltpu.sync_copy(pos_hbm.at[wid], idx_v)

        for c in range(nch):
            pltpu.sync_copy(hp_hbm.at[pl.ds(wid * per_w + c * ch, ch)], buf)
            writes = [pltpu.make_async_copy(buf, hs_hbm.at[idx_v.at[k, c]], wsem.at[k]) for k in range(topk)]
            for w in writes:
                w.start()
            for w in writes:
                w.wait()

    assert nw == V7X_SC_CORES * V7X_SC_SUBCORES and nw * per_w == ntok and ch == cfg["sc_rows"]
    return pl.kernel(
        body,
        out_type=jax.ShapeDtypeStruct((out_rows, half), U32),
        mesh=_sc_mesh(),
        scratch_types=[
            pltpu.VMEM((topk, nch, ch), jnp.int32),
            pltpu.VMEM((ch, half), U32),
            pltpu.SemaphoreType.DMA((topk,)),
        ],
        name="dispatch",
    )(hp, pos_w)


def _combine_call(ys, pos_w):
    cfg = _tiles()
    _, half = ys.shape
    nw, topk, nch, ch = pos_w.shape
    per_w = nch * ch
    ntok = nw * per_w

    def body(ys_hbm, pos_hbm, *rest):
        outs = rest[:topk]
        idx_v, buf = rest[topk:]
        wid = _sc_worker_id()
        pltpu.sync_copy(pos_hbm.at[wid], idx_v)
        for c in range(nch):
            for k in range(topk):
                pltpu.sync_copy(ys_hbm.at[idx_v.at[k, c]], buf)
                pltpu.sync_copy(buf, outs[k].at[pl.ds(wid * per_w + c * ch, ch)])

    assert nw == V7X_SC_CORES * V7X_SC_SUBCORES and ch == cfg["sc_rows"]
    return pl.kernel(
        body,
        out_type=[jax.ShapeDtypeStruct((ntok, half), U32)] * topk,
        mesh=_sc_mesh(),
        scratch_types=[
            pltpu.VMEM((topk, nch, ch), jnp.int32),
            pltpu.VMEM((ch, half), U32),
        ],
        name="combine",
    )(ys, pos_w)


def _ple_kernel(x1_ref, yg0_ref, yg1_ref, gate_ref, p_ref, plen_ref, wg32_ref, wu32_ref, fin_ref, o_ref,
                wg_ref, wu_ref):
    @pl.when(pl.program_id(0) == 0)
    def _():
        wg_ref[...] = wg32_ref[...].astype(BF16)
        wu_ref[...] = wu32_ref[...].astype(BF16)

    rows = x1_ref.shape[0]
    sub_rows = rows // PLE_SUBBLOCKS
    for q in range(PLE_SUBBLOCKS):
        rs = pl.ds(q * sub_rows, sub_rows)
        lo0, hi0 = _unpack_bf16_pair(yg0_ref[rs, :])
        lo1, hi1 = _unpack_bf16_pair(yg1_ref[rs, :])
        g0 = gate_ref[rs, 0:1]
        g1 = gate_ref[rs, 1:2]
        moe = g0 * jnp.concatenate([lo0, hi0], axis=1) + g1 * jnp.concatenate([lo1, hi1], axis=1)
        x2 = x1_ref[rs, :] + moe
        r = _rmsnorm(x2, plen_ref[...]).astype(BF16)
        gt = _sigmoid(_dot(r, wg_ref[...]))
        up = _dot(p_ref[rs, :].astype(BF16), wu_ref[...])
        x3 = x2 + gt * up
        o_ref[rs, :] = _rmsnorm(x3, fin_ref[...])


def _ple_call(x1, yg0, yg1, gates, p, ple_norm, wg, wu, final_norm):
    cfg = _tiles()
    ntok, d = x1.shape
    tp = cfg["ple_rows"]
    pdim = p.shape[1]
    return pl.pallas_call(
        _ple_kernel,
        grid=(ntok // tp,),
        in_specs=[
            pl.BlockSpec((tp, d), lambda i: (i, 0)),
            pl.BlockSpec((tp, d // 2), lambda i: (i, 0)),
            pl.BlockSpec((tp, d // 2), lambda i: (i, 0)),
            pl.BlockSpec((tp, V7X_SUBLANES), lambda i: (i, 0)),
            pl.BlockSpec((tp, pdim), lambda i: (i, 0)),
            _const_spec((1, d)),
            _const_spec(wg.shape),
            _const_spec(wu.shape),
            _const_spec((1, d)),
        ],
        out_specs=pl.BlockSpec((tp, d), lambda i: (i, 0)),
        out_shape=jax.ShapeDtypeStruct((ntok, d), F32),
        scratch_shapes=[pltpu.VMEM(wg.shape, BF16), pltpu.VMEM(wu.shape, BF16)],
        compiler_params=pltpu.CompilerParams(
            dimension_semantics=("arbitrary",),
            vmem_limit_bytes=cfg["ple_vmem"]),
        name="ple",
    )(x1, yg0, yg1, gates, p, ple_norm, wg, wu, final_norm)


def _blockdiag_pack(w):
    nb, bd, _ = w.shape
    per = V7X_MXU_DIM // bd
    w4 = w.reshape(nb // per, per, bd, bd)
    eye = jnp.eye(per, dtype=w.dtype)
    out = jnp.einsum("jpab,pq->jpaqb", w4, eye)
    return out.reshape(nb // per, V7X_MXU_DIM, V7X_MXU_DIM).astype(BF16)


def kernel(x, p, mix_norm, w_in, conv_w, conv_b, lru_wa, lru_ba, lru_wi, lru_bi, lru_lambda, sgu_ln_g, sgu_ln_b, sgu_ws, sgu_bs, w_out, ffn_norm, router_group_w, router_group_b, router_expert_w, router_expert_b, expert_w1, expert_w3, expert_w2, ple_norm, ple_gate_w, ple_up_w, final_norm):
    cfg = _tiles()
    bsz, seq, d = x.shape
    ntok = bsz * seq
    tm = cfg["expert_rows"]
    depth = w_in.shape[0]
    assert depth == 1, "the ple kernel applies the final norm, so it must be the last layer"
    l = 0
    nw_rows = V7X_SC_CORES * V7X_SC_SUBCORES * cfg["sc_rows"]
    assert cfg["mixer_rows"] % CHUNK == 0 and seq % cfg["mixer_rows"] == 0
    assert ntok % cfg["router_rows"] == 0 and ntok % cfg["ple_rows"] == 0 and ntok % nw_rows == 0
    assert ntok % tm == 0 and lru_wa.shape[1:] == (LRU_BLOCKS, d // LRU_BLOCKS, d // LRU_BLOCKS)
    assert max(cfg[k] for k in cfg if k.endswith("_vmem")) < V7X_VMEM_BYTES
    b_router = jnp.concatenate([
        router_group_b[l], jnp.zeros((EXPERT_ROW0 - N_GROUPS,), F32), router_expert_b[l],
        jnp.zeros((ROUTER_ROWS - EXPERT_ROW0 - N_EXPERTS,), F32)])[:, None]
    ts = cfg["mixer_rows"]
    group = ts // V7X_SUBLANES
    bs_tile = jnp.tile(sgu_bs[l], (1, ts // CHUNK)).reshape(SGU_GROUPS, V7X_SUBLANES, group)
    bs_tile = jnp.transpose(bs_tile, (2, 1, 0)).reshape(ts, SGU_GROUPS)
    x1, hp = _mixer_call(
        x, mix_norm[l][None], w_in[l], conv_w[l], conv_b[l][None],
        _blockdiag_pack(lru_wa[l]), lru_ba[l][None], _blockdiag_pack(lru_wi[l]), lru_bi[l][None],
        lru_lambda[l][None], sgu_ln_g[l][None], sgu_ln_b[l][None], sgu_ws[l], bs_tile,
        w_out[l], ffn_norm[l][None])
    hp = hp.reshape(ntok, d // 2)
    pos, gate, cnt = _router_call(hp, router_group_w[l], router_expert_w[l], b_router)

    cap = ntok
    tiles_per_expert = (cnt[:, 0].astype(jnp.int32) + tm - 1) // tm
    nw = V7X_SC_CORES * V7X_SC_SUBCORES
    ch = cfg["sc_rows"]
    pos_w = jnp.transpose(pos[:TOP_K].reshape(TOP_K, nw, ntok // (nw * ch), ch), (1, 0, 2, 3))

    hs = _dispatch_call(hp, pos_w, N_EXPERTS * cap)
    ys = _expert_call(tiles_per_expert, hs, expert_w1[l], expert_w3[l], expert_w2[l], cap)
    yg0, yg1 = _combine_call(ys, pos_w)

    out = _ple_call(x1.reshape(ntok, d), yg0, yg1, gate, p[l].reshape(ntok, -1), ple_norm[l][None],
                    ple_gate_w[l], ple_up_w[l], final_norm[None])
    return out.reshape(bsz, seq, d)
```

```python
import functools

import jax
import jax.numpy as jnp
from jax import lax
from jax.experimental import pallas as pl
from jax.experimental.pallas import tpu as pltpu
from jax.experimental.pallas import tpu_sc as plsc

F32 = jnp.float32
BF16 = jnp.bfloat16
U32 = jnp.uint32

LRU_BLOCKS = 16
CONV_WIDTH = 4
LRU_C = 8.0
SGU_GROUPS = 8
CHUNK = 128
N_GROUPS = 4
EXPERTS_PER_GROUP = 8
N_EXPERTS = N_GROUPS * EXPERTS_PER_GROUP
TOP_K = 2
EPS = 1e-6

V7X_MXU_DIM = 256
V7X_SUBLANES = 8
V7X_LANES = 128
V7X_VMEM_BYTES = 64 * 1024 * 1024
V7X_SC_CORES = 2
V7X_SC_SUBCORES = 16

EXPERT_LOOKAHEAD = 3
ROUTER_SUBBLOCKS = 8
EXPERT_SUBBLOCKS = 2
PLE_SUBBLOCKS = 4
ROUTER_ROWS = V7X_LANES
EXPERT_ROW0 = V7X_SUBLANES


def _tiles():
    return dict(
        mixer_rows=256,
        expert_rows=512,
        ple_rows=1024,
        sc_rows=128,
        router_rows=4096,
        mixer_vmem=52 * 1024 * 1024,
        expert_vmem=40 * 1024 * 1024,
        ple_vmem=48 * 1024 * 1024,
        router_vmem=40 * 1024 * 1024,
    )


def _dot(a, b):
    return jnp.dot(a, b, preferred_element_type=F32)


def _sigmoid(x):
    return 0.5 * jnp.tanh(0.5 * x) + 0.5


def _rmsnorm(x, g):
    ms = jnp.mean(x * x, axis=-1, keepdims=True)
    return x * lax.rsqrt(ms + EPS) * g


def _pack_bf16_pair(lo, hi):
    lo_b = lax.bitcast_convert_type(lo.astype(BF16).astype(F32), U32)
    hi_b = lax.bitcast_convert_type(hi.astype(BF16).astype(F32), U32)
    return (hi_b & jnp.uint32(0xFFFF0000)) | lax.shift_right_logical(lo_b, jnp.uint32(16))


def _unpack_bf16_pair(w):
    lo = lax.bitcast_convert_type(lax.shift_left(w, jnp.uint32(16)), F32)
    hi = lax.bitcast_convert_type(w & jnp.uint32(0xFFFF0000), F32)
    return lo, hi


def _const_spec(shape):
    zeros = (0,) * len(shape)
    return pl.BlockSpec(shape, lambda *_: zeros, pipeline_mode=pl.Buffered(1))


def _tile_copies(hbm, buf, sem, b, row0, slot, to_hbm):
    group = buf.shape[1]
    copies = []
    for r in range(V7X_SUBLANES):
        hbm_rows = hbm.at[b, pl.ds(row0 + group * r, group), :]
        vmem_rows = buf.at[slot, :, r, :]
        src, dst = (vmem_rows, hbm_rows) if to_hbm else (hbm_rows, vmem_rows)
        copies.append(pltpu.make_async_copy(src, dst, sem.at[slot]))
    return copies


def _lru_scan(a, u, h0):
    group = a.shape[0]
    acc_a = [a[0]]
    acc_u = [u[0]]
    for g in range(1, group):
        acc_a.append(a[g] * acc_a[-1])
        acc_u.append(a[g] * acc_u[-1] + u[g])
    end_a, end_u = acc_a[-1], acc_u[-1]
    sub = lax.broadcasted_iota(jnp.int32, end_a.shape, 0)
    shift = 1
    while shift < V7X_SUBLANES:
        keep = sub >= shift
        a_sh = pltpu.roll(end_a, shift, axis=0)
        u_sh = pltpu.roll(end_u, shift, axis=0)
        end_u = jnp.where(keep, end_a * u_sh + end_u, end_u)
        end_a = jnp.where(keep, end_a * a_sh, end_a)
        shift *= 2
    h_end = end_a * h0 + end_u
    h_in = jnp.where(sub == 0, h0, pltpu.roll(h_end, 1, axis=0))
    out = [acc_a[g] * h_in + acc_u[g] for g in range(group)]
    return jnp.stack(out, axis=0), h_end[V7X_SUBLANES - 1:V7X_SUBLANES, :]


def _mixer_kernel(x_hbm, mixn_ref, win_hbm, convw_ref, convb_ref, wa_ref, ba_ref, wi_ref, bi_ref,
                  lam_ref, lng_ref, lnb_ref, ws_ref, bsp_ref, wout_hbm, ffn_ref,
                  x1_hbm, hp_hbm,
                  xbuf, z0_ref, z1_ref, x1buf, hpbuf, xsem, x1sem, hpsem, wsm_ref, ztail_ref, hcar_ref,
                  win_ref, wout_ref, wsem,
                  *, nseq):
    j = pl.program_id(0)
    ntile = pl.num_programs(0) - 1
    _, group, _, d = xbuf.shape
    rows = group * V7X_SUBLANES
    half = d // 2
    ta = jnp.minimum(j, ntile - 1)
    tb = jnp.maximum(j - 1, 0)
    s = lax.rem(tb, nseq)
    slot = lax.rem(tb, 2)

    def fetch(t):
        return _tile_copies(x_hbm, xbuf, xsem, lax.div(t, nseq), lax.rem(t, nseq) * rows,
                            lax.rem(t, 3), to_hbm=False)

    def put(t):
        tb_, ts_, sl = lax.div(t, nseq), lax.rem(t, nseq) * rows, lax.rem(t, 2)
        return (_tile_copies(x1_hbm, x1buf, x1sem, tb_, ts_, sl, to_hbm=True)
                + _tile_copies(hp_hbm, hpbuf, hpsem, tb_, ts_, sl, to_hbm=True))

    @pl.when(j == 0)
    def _():
        for c in fetch(0):
            c.start()
        stage = (z0_ref, z1_ref)
        n_in_chunks = win_hbm.shape[0] // rows

        def win_copy(c):
            return pltpu.make_async_copy(win_hbm.at[pl.ds(c * rows, rows), :], stage[c % 2], wsem.at[c % 2])

        win_copy(0).start()
        for c in range(n_in_chunks):
            if c + 1 < n_in_chunks:
                win_copy(c + 1).start()
            win_copy(c).wait()
            plain = 4 * d
            win_ref[c * rows:(c + 1) * rows, :plain] = stage[c % 2][:, :plain].astype(BF16)
            win_ref[c * rows:(c + 1) * rows, plain:] = (0.5 * stage[c % 2][:, plain:]).astype(BF16)
        n_out_chunks = wout_hbm.shape[0] // rows
        out_copies = [pltpu.make_async_copy(wout_hbm.at[pl.ds(c * rows, rows), :],
                                            z0_ref.at[:, c * d:(c + 1) * d], wsem.at[0])
                      for c in range(n_out_chunks)]
        for cp in out_copies:
            cp.start()
        for cp in out_copies:
            cp.wait()
        for c in range(n_out_chunks):
            wout_ref[c * rows:(c + 1) * rows, :] = (0.25 * z0_ref[:, c * d:(c + 1) * d]).astype(BF16)
        z1_ref[...] = jnp.zeros_like(z1_ref)
        i_idx = lax.broadcasted_iota(jnp.int32, (rows, rows), 0)
        j_idx = lax.broadcasted_iota(jnp.int32, (rows, rows), 1)
        t_i = group * lax.rem(i_idx, V7X_SUBLANES) + lax.div(i_idx, V7X_SUBLANES)
        t_j = group * lax.rem(j_idx, V7X_SUBLANES) + lax.div(j_idx, V7X_SUBLANES)
        keep = (t_i >= t_j) & (lax.div(t_i, CHUNK) == lax.div(t_j, CHUNK))
        pick_rows = jnp.where(t_i == j_idx, 1.0, 0.0).astype(BF16)
        pick_cols = jnp.where(i_idx == t_j, 1.0, 0.0).astype(BF16)
        reps = rows // CHUNK
        for g in range(SGU_GROUPS):
            w_chunk = ws_ref[g].astype(BF16)
            w_rows = jnp.concatenate([w_chunk] * reps, axis=1)
            w_full = jnp.concatenate([w_rows] * reps, axis=0)
            w_perm = _dot(_dot(pick_rows, w_full).astype(BF16), pick_cols)
            wsm_ref[g] = jnp.where(keep, w_perm, 0.0).astype(BF16)

    @pl.when(j + 1 < ntile)
    def _():
        for c in fetch(j + 1):
            c.start()

    @pl.when(j < ntile)
    def _():
        for c in fetch(j):
            c.wait()

    @pl.when(s == 0)
    def _():
        ztail_ref[...] = jnp.zeros_like(ztail_ref)
        hcar_ref[...] = jnp.zeros_like(hcar_ref)

    def compute(z_w, z_r):
        xa_in = xbuf[lax.rem(ta, 3)].reshape(rows, d)
        h_next = _rmsnorm(xa_in, mixn_ref[...]).astype(BF16)
        pw = d // 2

        def project(k):
            z_w[:, k * pw:(k + 1) * pw] = _dot(h_next, win_ref[:, k * pw:(k + 1) * pw])

        x = xbuf[lax.rem(tb, 3)].reshape(rows, d)

        def sec(k, c0, c1):
            return z_r[:, k * d + c0:k * d + c1]

        def one_plus_tanh_gelu(v):
            c = 0.7978845608028654
            return 1.0 + jnp.tanh(v * (c + (c * 0.044715) * (v * v)))

        cw = 0.5 * convw_ref[...]
        cb_h = 0.5 * convb_ref[...]
        ba_h = 0.5 * ba_ref[...]
        bi_h = 0.5 * bi_ref[...]
        neg_lam = -lam_ref[...]
        softplus = jnp.maximum(neg_lam, 0.0) + jnp.log1p(jnp.exp(-jnp.abs(neg_lam)))
        c_a = (-0.5 * LRU_C) * softplus
        blk = V7X_MXU_DIM
        sub3 = lax.broadcasted_iota(jnp.int32, (CONV_WIDTH - 1, V7X_SUBLANES, blk), 1)
        term_a = []
        for n in range(d // blk):
            project(n)
            c0, c1 = n * blk, (n + 1) * blk
            z3 = sec(0, c0, c1).reshape(group, V7X_SUBLANES, blk)
            tail = z3[group - (CONV_WIDTH - 1):]
            halo = jnp.where(sub3 == 0, pltpu.roll(ztail_ref[:, :, c0:c1], 1, axis=1),
                             pltpu.roll(tail, 1, axis=1))
            ztail_ref[:, :, c0:c1] = tail
            zext = jnp.concatenate([halo, z3], axis=0)
            xa_h = cb_h[:, c0:c1] + cw[CONV_WIDTH - 1:CONV_WIDTH, c0:c1] * z3
            for k in range(1, CONV_WIDTH):
                lo = CONV_WIDTH - 1 - k
                xa_h = xa_h + cw[lo:lo + 1, c0:c1] * zext[lo:lo + group]
            xa2 = xa_h.reshape(rows, blk)
            xa_bf = xa2.astype(BF16)
            th_r = jnp.tanh(_dot(xa_bf, wa_ref[n]) + ba_h[:, c0:c1])
            th_i = jnp.tanh(_dot(xa_bf, wi_ref[n]) + bi_h[:, c0:c1])
            a = jnp.exp(c_a[:, c0:c1] + c_a[:, c0:c1] * th_r)
            u = jnp.sqrt(1.0 - a * a) * ((1.0 + th_i) * xa2)
            hseq, hlast = _lru_scan(a.reshape(group, V7X_SUBLANES, blk),
                                    u.reshape(group, V7X_SUBLANES, blk), hcar_ref[:, c0:c1])
            hcar_ref[:, c0:c1] = hlast
            zg = sec(1, c0, c1)
            term_a.append(((1.0 + jnp.tanh(sec(4, c0, c1))) * one_plus_tanh_gelu(zg))
                          * (zg * hseq.reshape(rows, blk)))

        project(4)
        zv = sec(3, 0, d)
        gv2 = zv * one_plus_tanh_gelu(zv)
        project(5)
        mu = jnp.mean(gv2, axis=-1, keepdims=True)
        xc = gv2 - mu
        var = jnp.mean(xc * xc, axis=-1, keepdims=True)
        v_bf = (xc * lax.rsqrt(var + 4.0 * EPS) * lng_ref[...] + lnb_ref[...]).astype(BF16)
        project(6)
        gdim = d // SGU_GROUPS
        term_b = []
        for g in range(SGU_GROUPS):
            c0, c1 = g * gdim, (g + 1) * gdim
            if g in (1, 3, 5, 6, 7):
                project({1: 7, 3: 8, 5: 9, 6: 10, 7: 11}[g])
            sp = _dot(wsm_ref[g], v_bf[:, c0:c1]) + bsp_ref[:, g:g + 1]
            zu = sec(2, c0, c1)
            term_b.append(((1.0 + jnp.tanh(sec(5, c0, c1))) * one_plus_tanh_gelu(zu)) * (zu * sp))
        merged4 = jnp.concatenate(term_a, axis=1) + jnp.concatenate(term_b, axis=1)

        x1 = x + _dot(merged4.astype(BF16), wout_ref[...])

        hn = _rmsnorm(x1, ffn_ref[...])
        hp = _pack_bf16_pair(hn[:, :half], hn[:, half:])

        @pl.when(j >= 3)
        def _():
            for c in put(tb - 2):
                c.wait()

        x1buf[slot] = x1.reshape(group, V7X_SUBLANES, d)
        hpbuf[slot] = hp.reshape(group, V7X_SUBLANES, half)

        @pl.when(j >= 1)
        def _():
            for c in put(tb):
                c.start()

    @pl.when(lax.rem(j, 2) == 0)
    def _():
        compute(z0_ref, z1_ref)

    @pl.when(lax.rem(j, 2) == 1)
    def _():
        compute(z1_ref, z0_ref)

    @pl.when(j == ntile)
    def _():
        for c in put(tb):
            c.wait()

        @pl.when(ntile >= 2)
        def _():
            for c in put(tb - 1):
                c.wait()


def _mixer_call(x, mix_norm, w_in, conv_w, conv_b, wa_blk, ba, wi_blk, bi, lam, ln_g, ln_b, ws,
                bs_tile, w_out, ffn_norm):
    cfg = _tiles()
    bsz, seq, d = x.shape
    ts = cfg["mixer_rows"]
    group = ts // V7X_SUBLANES
    nseq = seq // ts
    ntile = bsz * nseq
    row1 = (1, d)
    in_specs = [
        pl.BlockSpec(memory_space=pl.ANY),
        _const_spec(row1),
        pl.BlockSpec(memory_space=pl.ANY),
        _const_spec(conv_w.shape), _const_spec(row1),
        _const_spec(wa_blk.shape), _const_spec(row1),
        _const_spec(wi_blk.shape), _const_spec(row1),
        _const_spec(row1),
        _const_spec(row1), _const_spec(row1),
        _const_spec(ws.shape), _const_spec(bs_tile.shape),
        pl.BlockSpec(memory_space=pl.ANY), _const_spec(row1),
    ]
    out_shape = [
        jax.ShapeDtypeStruct((bsz, seq, d), F32),
        jax.ShapeDtypeStruct((bsz, seq, d // 2), U32),
    ]
    out_specs = [
        pl.BlockSpec(memory_space=pl.ANY),
        pl.BlockSpec(memory_space=pl.ANY),
    ]
    scratch = [
        pltpu.VMEM((3, group, V7X_SUBLANES, d), F32),
        pltpu.VMEM((ts, w_in.shape[1]), F32),
        pltpu.VMEM((ts, w_in.shape[1]), F32),
        pltpu.VMEM((2, group, V7X_SUBLANES, d), F32),
        pltpu.VMEM((2, group, V7X_SUBLANES, d // 2), U32),
        pltpu.SemaphoreType.DMA((3,)),
        pltpu.SemaphoreType.DMA((2,)),
        pltpu.SemaphoreType.DMA((2,)),
        pltpu.VMEM((SGU_GROUPS, ts, ts), BF16),
        pltpu.VMEM((CONV_WIDTH - 1, V7X_SUBLANES, d), F32),
        pltpu.VMEM((1, d), F32),
        pltpu.VMEM(w_in.shape, BF16),
        pltpu.VMEM(w_out.shape, BF16),
        pltpu.SemaphoreType.DMA((2,)),
    ]
    return pl.pallas_call(
        functools.partial(_mixer_kernel, nseq=nseq),
        grid=(ntile + 1,),
        in_specs=in_specs,
        out_specs=out_specs,
        out_shape=out_shape,
        scratch_shapes=scratch,
        compiler_params=pltpu.CompilerParams(
            dimension_semantics=("arbitrary",),
            vmem_limit_bytes=cfg["mixer_vmem"]),
        name="mixer",
    )(x, mix_norm, w_in, conv_w, conv_b, wa_blk, ba, wi_blk, bi, lam, ln_g, ln_b, ws, bs_tile,
      w_out, ffn_norm)


def _router_kernel(hp_ref, wg_ref, we_ref, br_ref, pos_ref, gate_ref, cnt_ref, ccar_ref, wr_ref,
                   *, expert_capacity):
    rows = hp_ref.shape[0]

    @pl.when(pl.program_id(0) == 0)
    def _():
        ccar_ref[...] = jnp.zeros_like(ccar_ref)
        wr_ref[...] = jnp.zeros_like(wr_ref)
        wr_ref[:, 0:N_GROUPS] = wg_ref[...].astype(BF16)
        wr_ref[:, EXPERT_ROW0:EXPERT_ROW0 + N_EXPERTS] = we_ref[...].astype(BF16)

    sub_rows = rows // ROUTER_SUBBLOCKS
    lts = []
    for q in range(ROUTER_SUBBLOCKS):
        lo, hi = _unpack_bf16_pair(hp_ref[q * sub_rows:(q + 1) * sub_rows, :])
        hn = jnp.concatenate([lo, hi], axis=1)
        logits = _dot(hn.astype(BF16), wr_ref[...])
        lts.append(jnp.transpose(logits) + br_ref[...])
    sub = lax.broadcasted_iota(jnp.int32, (V7X_SUBLANES, sub_rows), 0)
    subf = sub.astype(F32)
    big = jnp.float32(1e9)
    eid = lax.broadcasted_iota(jnp.int32, (N_EXPERTS, sub_rows), 0).astype(F32)
    sb = V7X_MXU_DIM
    before = (lax.broadcasted_iota(jnp.int32, (sb, sb), 0)
              < lax.broadcasted_iota(jnp.int32, (sb, sb), 1))
    before = jnp.where(before, 1.0, 0.0).astype(BF16)
    cap = float(expert_capacity)
    zero = jnp.zeros((V7X_SUBLANES - TOP_K, sub_rows), F32)
    running = ccar_ref[:, 0:1]
    for q, lt in enumerate(lts):
        lg = jnp.where(sub < N_GROUPS, lt[0:V7X_SUBLANES, :], -jnp.inf)
        g_exp = jnp.exp(lg - jnp.max(lg, axis=0, keepdims=True))
        g_prob = g_exp / jnp.sum(g_exp, axis=0, keepdims=True)
        g_top = jnp.max(g_prob, axis=0, keepdims=True)
        g_idx = jnp.min(jnp.where(g_prob == g_top, subf, big), axis=0, keepdims=True)

        e_sel = jnp.zeros((EXPERTS_PER_GROUP, sub_rows), F32)
        for g in range(N_GROUPS):
            r0 = EXPERT_ROW0 + g * EXPERTS_PER_GROUP
            e_sel = jnp.where(g_idx == g, lt[r0:r0 + EXPERTS_PER_GROUP, :], e_sel)
        e_exp = jnp.exp(e_sel - jnp.max(e_sel, axis=0, keepdims=True))
        e_prob = e_exp / jnp.sum(e_exp, axis=0, keepdims=True)
        p1 = jnp.max(e_prob, axis=0, keepdims=True)
        i1 = jnp.min(jnp.where(e_prob == p1, subf, big), axis=0, keepdims=True)
        rest = jnp.where(subf == i1, -1.0, e_prob)
        p2 = jnp.max(rest, axis=0, keepdims=True)
        i2 = jnp.min(jnp.where(rest == p2, subf, big), axis=0, keepdims=True)
        psum = p1 + p2
        gate1 = g_top * (p1 / psum)
        gate2 = g_top * (p2 / psum)
        gid1 = g_idx * EXPERTS_PER_GROUP + i1
        gid2 = g_idx * EXPERTS_PER_GROUP + i2

        hit1 = eid == gid1
        hit2 = eid == gid2
        cnt = jnp.where(hit1 | hit2, 1.0, 0.0)
        base = []
        for c in range(sub_rows // sb):
            part = cnt[:, c * sb:(c + 1) * sb]
            base.append(running + _dot(part.astype(BF16), before))
            running = running + jnp.sum(part, axis=1, keepdims=True)
        base = jnp.concatenate(base, axis=1)
        rank1 = jnp.sum(jnp.where(hit1, base, 0.0), axis=0, keepdims=True)
        rank2 = jnp.sum(jnp.where(hit2, base, 0.0), axis=0, keepdims=True)
        pos = jnp.concatenate([gid1 * cap + rank1, gid2 * cap + rank2, zero], axis=0)
        pos_ref[:, q * sub_rows:(q + 1) * sub_rows] = pos.astype(jnp.int32)
        gate_ref[q * sub_rows:(q + 1) * sub_rows, :] = jnp.transpose(
            jnp.concatenate([gate1, gate2, zero], axis=0))
    total = jnp.broadcast_to(running, ccar_ref.shape)
    ccar_ref[...] = total
    cnt_ref[...] = total


def _router_call(hp, w_group, w_expert, b_router):
    cfg = _tiles()
    ntok, half = hp.shape
    tr = cfg["router_rows"]
    return pl.pallas_call(
        functools.partial(_router_kernel, expert_capacity=ntok),
        grid=(ntok // tr,),
        in_specs=[
            pl.BlockSpec((tr, half), lambda i: (i, 0)),
            _const_spec(w_group.shape),
            _const_spec(w_expert.shape),
            _const_spec(b_router.shape),
        ],
        out_specs=[
            pl.BlockSpec((V7X_SUBLANES, tr), lambda i: (0, i)),
            pl.BlockSpec((tr, V7X_SUBLANES), lambda i: (i, 0)),
            pl.BlockSpec((N_EXPERTS, V7X_LANES), lambda i: (0, 0)),
        ],
        out_shape=[
            jax.ShapeDtypeStruct((V7X_SUBLANES, ntok), jnp.int32),
            jax.ShapeDtypeStruct((ntok, V7X_SUBLANES), F32),
            jax.ShapeDtypeStruct((N_EXPERTS, V7X_LANES), F32),
        ],
        scratch_shapes=[
            pltpu.VMEM((N_EXPERTS, V7X_LANES), F32),
            pltpu.VMEM((w_group.shape[0], ROUTER_ROWS), BF16),
        ],
        compiler_params=pltpu.CompilerParams(
            dimension_semantics=("arbitrary",),
            vmem_limit_bytes=cfg["router_vmem"]),
        name="router",
    )(hp, w_group, w_expert, b_router)


def _expert_kernel(nt_ref, base_ref, texp_ref, tloc_ref, hs_hbm, w1_ref, w3_ref, w2_ref, ys_hbm,
                   hbuf, ybuf, hsem, ysem, w1b_ref, w3b_ref, w2b_ref, *, capacity):
    e = pl.program_id(0)
    n_exp = pl.num_programs(0)
    nt = nt_ref[e]
    base = base_ref[e]
    total = base_ref[n_exp - 1] + nt_ref[n_exp - 1]
    n_in, tm, _ = hbuf.shape
    n_out = ybuf.shape[0]
    ahead = n_in - 1

    def load(g):
        slot = lax.rem(g, n_in)
        rows = pl.ds(texp_ref[g] * capacity + tloc_ref[g] * tm, tm)
        return pltpu.make_async_copy(hs_hbm.at[rows], hbuf.at[slot], hsem.at[slot])

    def store(t, slot):
        rows = pl.ds(e * capacity + t * tm, tm)
        return pltpu.make_async_copy(ybuf.at[slot], ys_hbm.at[rows], ysem.at[slot])

    @pl.when(e == 0)
    def _():
        for g0 in range(ahead):
            @pl.when(g0 < total)
            def _():
                load(g0).start()

    w1b_ref[...] = w1_ref[...].astype(BF16)
    w3b_ref[...] = w3_ref[...].astype(BF16)
    w2b_ref[...] = w2_ref[...].astype(BF16)

    @pl.loop(0, nt)
    def _(t):
        g = base + t

        @pl.when(g + ahead < total)
        def _():
            load(g + ahead).start()

        load(g).wait()
        slot = lax.rem(g, n_out)

        @pl.when(g >= n_out)
        def _():
            store(t, slot).wait()

        sub_rows = tm // EXPERT_SUBBLOCKS
        blocks = [pl.ds(q * sub_rows, sub_rows) for q in range(EXPERT_SUBBLOCKS)]
        rows_in = []
        for rs in blocks:
            lo, hi = _unpack_bf16_pair(hbuf[lax.rem(g, n_in), rs, :])
            rows_in.append(jnp.concatenate([lo, hi], axis=1).astype(BF16))
        up = [(_dot(h, w1b_ref[...]), _dot(h, w3b_ref[...])) for h in rows_in]
        down = []
        for a, b in up:
            hid = (a * _sigmoid(a)) * b
            down.append(_dot(hid.astype(BF16), w2b_ref[...]))
        for rs, y in zip(blocks, down):
            half = y.shape[1] // 2
            ybuf[slot, rs, :] = _pack_bf16_pair(y[:, :half], y[:, half:])
        store(t, slot).start()

    @pl.when(e + 1 == n_exp)
    def _():
        for back in range(1, n_out + 1):
            @pl.when(total >= back)
            def _():
                store(0, lax.rem(total - back, n_out)).wait()


def _expert_call(tiles_per_expert, hs, w1, w3, w2, capacity):
    cfg = _tiles()
    tm = cfg["expert_rows"]
    prow, half = hs.shape
    n_exp, d, f = w1.shape
    ends = jnp.cumsum(tiles_per_expert)
    base = ends - tiles_per_expert
    g = jnp.arange(capacity * TOP_K // tm + n_exp, dtype=jnp.int32)
    texp = jnp.minimum(jnp.sum((ends[None, :] <= g[:, None]).astype(jnp.int32), axis=1), n_exp - 1)
    onehot = texp[:, None] == jnp.arange(n_exp, dtype=jnp.int32)[None, :]
    tloc = g - jnp.sum(jnp.where(onehot, base[None, :], 0), axis=1)

    def w_map(e, *_):
        return (e, 0, 0)

    grid_spec = pltpu.PrefetchScalarGridSpec(
        num_scalar_prefetch=4,
        grid=(n_exp,),
        in_specs=[
            pl.BlockSpec(memory_space=pl.ANY),
            pl.BlockSpec((None, d, f), w_map),
            pl.BlockSpec((None, d, f), w_map),
            pl.BlockSpec((None, f, d), w_map),
        ],
        out_specs=pl.BlockSpec(memory_space=pl.ANY),
        scratch_shapes=[
            pltpu.VMEM((EXPERT_LOOKAHEAD + 1, tm, half), U32),
            pltpu.VMEM((2, tm, half), U32),
            pltpu.SemaphoreType.DMA((EXPERT_LOOKAHEAD + 1,)),
            pltpu.SemaphoreType.DMA((2,)),
            pltpu.VMEM((d, f), BF16),
            pltpu.VMEM((d, f), BF16),
            pltpu.VMEM((f, d), BF16),
        ],
    )
    return pl.pallas_call(
        functools.partial(_expert_kernel, capacity=capacity),
        grid_spec=grid_spec,
        out_shape=jax.ShapeDtypeStruct((prow, half), U32),
        compiler_params=pltpu.CompilerParams(
            dimension_semantics=("arbitrary",),
            vmem_limit_bytes=cfg["expert_vmem"]),
        name="experts",
    )(tiles_per_expert, base, texp, tloc, hs, w1, w3, w2)


def _sc_mesh():
    return plsc.VectorSubcoreMesh(core_axis_name="c", subcore_axis_name="s",
                                  num_cores=V7X_SC_CORES, num_subcores=V7X_SC_SUBCORES)


def _sc_worker_id():
    return lax.axis_index("s") * V7X_SC_CORES + lax.axis_index("c")


def _dispatch_call(hp, pos_w, out_rows):
    cfg = _tiles()
    ntok, half = hp.shape
    nw, topk, nch, ch = pos_w.shape
    per_w = nch * ch

    def body(hp_hbm, pos_hbm, hs_hbm, idx_v, buf, wsem):
        wid = _sc_worker_id()
        pltpu.sync_copy(pos_hbm.at[wid], idx_v)

        for c in range(nch):
            pltpu.sync_copy(hp_hbm.at[pl.ds(wid * per_w + c * ch, ch)], buf)
            writes = [pltpu.make_async_copy(buf, hs_hbm.at[idx_v.at[k, c]], wsem.at[k]) for k in range(topk)]
            for w in writes:
                w.start()
            for w in writes:
                w.wait()

    assert nw == V7X_SC_CORES * V7X_SC_SUBCORES and nw * per_w == ntok and ch == cfg["sc_rows"]
    return pl.kernel(
        body,
        out_type=jax.ShapeDtypeStruct((out_rows, half), U32),
        mesh=_sc_mesh(),
        scratch_types=[
            pltpu.VMEM((topk, nch, ch), jnp.int32),
            pltpu.VMEM((ch, half), U32),
            pltpu.SemaphoreType.DMA((topk,)),
        ],
        name="dispatch",
    )(hp, pos_w)


def _combine_call(ys, pos_w):
    cfg = _tiles()
    _, half = ys.shape
    nw, topk, nch, ch = pos_w.shape
    per_w = nch * ch
    ntok = nw * per_w

    def body(ys_hbm, pos_hbm, *rest):
        outs = rest[:topk]
        idx_v, buf = rest[topk:]
        wid = _sc_worker_id()
        pltpu.sync_copy(pos_hbm.at[wid], idx_v)
        for c in range(nch):
            for k in range(topk):
                pltpu.sync_copy(ys_hbm.at[idx_v.at[k, c]], buf)
                pltpu.sync_copy(buf, outs[k].at[pl.ds(wid * per_w + c * ch, ch)])

    assert nw == V7X_SC_CORES * V7X_SC_SUBCORES and ch == cfg["sc_rows"]
    return pl.kernel(
        body,
        out_type=[jax.ShapeDtypeStruct((ntok, half), U32)] * topk,
        mesh=_sc_mesh(),
        scratch_types=[
            pltpu.VMEM((topk, nch, ch), jnp.int32),
            pltpu.VMEM((ch, half), U32),
        ],
        name="combine",
    )(ys, pos_w)


def _ple_kernel(x1_ref, yg0_ref, yg1_ref, gate_ref, p_ref, plen_ref, wg32_ref, wu32_ref, fin_ref, o_ref,
                wg_ref, wu_ref):
    @pl.when(pl.program_id(0) == 0)
    def _():
        wg_ref[...] = wg32_ref[...].astype(BF16)
        wu_ref[...] = wu32_ref[...].astype(BF16)

    rows = x1_ref.shape[0]
    sub_rows = rows // PLE_SUBBLOCKS
    for q in range(PLE_SUBBLOCKS):
        rs = pl.ds(q * sub_rows, sub_rows)
        lo0, hi0 = _unpack_bf16_pair(yg0_ref[rs, :])
        lo1, hi1 = _unpack_bf16_pair(yg1_ref[rs, :])
        g0 = gate_ref[rs, 0:1]
        g1 = gate_ref[rs, 1:2]
        moe = g0 * jnp.concatenate([lo0, hi0], axis=1) + g1 * jnp.concatenate([lo1, hi1], axis=1)
        x2 = x1_ref[rs, :] + moe
        r = _rmsnorm(x2, plen_ref[...]).astype(BF16)
        gt = _sigmoid(_dot(r, wg_ref[...]))
        up = _dot(p_ref[rs, :].astype(BF16), wu_ref[...])
        x3 = x2 + gt * up
        o_ref[rs, :] = _rmsnorm(x3, fin_ref[...])


def _ple_call(x1, yg0, yg1, gates, p, ple_norm, wg, wu, final_norm):
    cfg = _tiles()
    ntok, d = x1.shape
    tp = cfg["ple_rows"]
    pdim = p.shape[1]
    return pl.pallas_call(
        _ple_kernel,
        grid=(ntok // tp,),
        in_specs=[
            pl.BlockSpec((tp, d), lambda i: (i, 0)),
            pl.BlockSpec((tp, d // 2), lambda i: (i, 0)),
            pl.BlockSpec((tp, d // 2), lambda i: (i, 0)),
            pl.BlockSpec((tp, V7X_SUBLANES), lambda i: (i, 0)),
            pl.BlockSpec((tp, pdim), lambda i: (i, 0)),
            _const_spec((1, d)),
            _const_spec(wg.shape),
            _const_spec(wu.shape),
            _const_spec((1, d)),
        ],
        out_specs=pl.BlockSpec((tp, d), lambda i: (i, 0)),
        out_shape=jax.ShapeDtypeStruct((ntok, d), F32),
        scratch_shapes=[pltpu.VMEM(wg.shape, BF16), pltpu.VMEM(wu.shape, BF16)],
        compiler_params=pltpu.CompilerParams(
            dimension_semantics=("arbitrary",),
            vmem_limit_bytes=cfg["ple_vmem"]),
        name="ple",
    )(x1, yg0, yg1, gates, p, ple_norm, wg, wu, final_norm)


def _blockdiag_pack(w):
    nb, bd, _ = w.shape
    per = V7X_MXU_DIM // bd
    w4 = w.reshape(nb // per, per, bd, bd)
    eye = jnp.eye(per, dtype=w.dtype)
    out = jnp.einsum("jpab,pq->jpaqb", w4, eye)
    return out.reshape(nb // per, V7X_MXU_DIM, V7X_MXU_DIM).astype(BF16)


def kernel(x, p, mix_norm, w_in, conv_w, conv_b, lru_wa, lru_ba, lru_wi, lru_bi, lru_lambda, sgu_ln_g, sgu_ln_b, sgu_ws, sgu_bs, w_out, ffn_norm, router_group_w, router_group_b, router_expert_w, router_expert_b, expert_w1, expert_w3, expert_w2, ple_norm, ple_gate_w, ple_up_w, final_norm):
    cfg = _tiles()
    bsz, seq, d = x.shape
    ntok = bsz * seq
    tm = cfg["expert_rows"]
    depth = w_in.shape[0]
    assert depth == 1, "the ple kernel applies the final norm, so it must be the last layer"
    l = 0
    nw_rows = V7X_SC_CORES * V7X_SC_SUBCORES * cfg["sc_rows"]
    assert cfg["mixer_rows"] % CHUNK == 0 and seq % cfg["mixer_rows"] == 0
    assert ntok % cfg["router_rows"] == 0 and ntok % cfg["ple_rows"] == 0 and ntok % nw_rows == 0
    assert ntok % tm == 0 and lru_wa.shape[1:] == (LRU_BLOCKS, d // LRU_BLOCKS, d // LRU_BLOCKS)
    assert max(cfg[k] for k in cfg if k.endswith("_vmem")) < V7X_VMEM_BYTES
    b_router = jnp.concatenate([
        router_group_b[l], jnp.zeros((EXPERT_ROW0 - N_GROUPS,), F32), router_expert_b[l],
        jnp.zeros((ROUTER_ROWS - EXPERT_ROW0 - N_EXPERTS,), F32)])[:, None]
    ts = cfg["mixer_rows"]
    group = ts // V7X_SUBLANES
    bs_tile = jnp.tile(sgu_bs[l], (1, ts // CHUNK)).reshape(SGU_GROUPS, V7X_SUBLANES, group)
    bs_tile = jnp.transpose(bs_tile, (2, 1, 0)).reshape(ts, SGU_GROUPS)
    x1, hp = _mixer_call(
        x, mix_norm[l][None], w_in[l], conv_w[l], conv_b[l][None],
        _blockdiag_pack(lru_wa[l]), lru_ba[l][None], _blockdiag_pack(lru_wi[l]), lru_bi[l][None],
        lru_lambda[l][None], sgu_ln_g[l][None], sgu_ln_b[l][None], sgu_ws[l], bs_tile,
        w_out[l], ffn_norm[l][None])
    hp = hp.reshape(ntok, d // 2)
    pos, gate, cnt = _router_call(hp, router_group_w[l], router_expert_w[l], b_router)

    cap = ntok
    tiles_per_expert = (cnt[:, 0].astype(jnp.int32) + tm - 1) // tm
    nw = V7X_SC_CORES * V7X_SC_SUBCORES
    ch = cfg["sc_rows"]
    pos_w = jnp.transpose(pos[:TOP_K].reshape(TOP_K, nw, ntok // (nw * ch), ch), (1, 0, 2, 3))

    hs = _dispatch_call(hp, pos_w, N_EXPERTS * cap)
    ys = _expert_call(tiles_per_expert, hs, expert_w1[l], expert_w3[l], expert_w2[l], cap)
    yg0, yg1 = _combine_call(ys, pos_w)

    out = _ple_call(x1.reshape(ntok, d), yg0, yg1, gate, p[l].reshape(ntok, -1), ple_norm[l][None],
                    ple_gate_w[l], ple_up_w[l], final_norm[None])
    return out.reshape(bsz, seq, d)
```

```python
import functools

import jax
import jax.numpy as jnp
from jax import lax
from jax.experimental import pallas as pl
from jax.experimental.pallas import tpu as pltpu
from jax.experimental.pallas import tpu_sc as plsc

F32 = jnp.float32
BF16 = jnp.bfloat16
U32 = jnp.uint32

LRU_BLOCKS = 16
CONV_WIDTH = 4
LRU_C = 8.0
SGU_GROUPS = 8
CHUNK = 128
N_GROUPS = 4
EXPERTS_PER_GROUP = 8
N_EXPERTS = N_GROUPS * EXPERTS_PER_GROUP
TOP_K = 2
EPS = 1e-6

V7X_MXU_DIM = 256
V7X_SUBLANES = 8
V7X_LANES = 128
V7X_VMEM_BYTES = 64 * 1024 * 1024
V7X_SC_CORES = 2
V7X_SC_SUBCORES = 16

EXPERT_LOOKAHEAD = 3
ROUTER_SUBBLOCKS = 8
EXPERT_SUBBLOCKS = 2
PLE_SUBBLOCKS = 4
ROUTER_ROWS = V7X_LANES
EXPERT_ROW0 = V7X_SUBLANES


def _tiles():
    return dict(
        mixer_rows=256,
        expert_rows=512,
        ple_rows=1024,
        sc_rows=128,
        router_rows=4096,
        mixer_vmem=52 * 1024 * 1024,
        expert_vmem=40 * 1024 * 1024,
        ple_vmem=48 * 1024 * 1024,
        router_vmem=40 * 1024 * 1024,
    )


def _dot(a, b):
    return jnp.dot(a, b, preferred_element_type=F32)


def _sigmoid(x):
    return 0.5 * jnp.tanh(0.5 * x) + 0.5


def _rmsnorm(x, g):
    ms = jnp.mean(x * x, axis=-1, keepdims=True)
    return x * lax.rsqrt(ms + EPS) * g


def _pack_bf16_pair(lo, hi):
    lo_b = lax.bitcast_convert_type(lo.astype(BF16).astype(F32), U32)
    hi_b = lax.bitcast_convert_type(hi.astype(BF16).astype(F32), U32)
    return (hi_b & jnp.uint32(0xFFFF0000)) | lax.shift_right_logical(lo_b, jnp.uint32(16))


def _unpack_bf16_pair(w):
    lo = lax.bitcast_convert_type(lax.shift_left(w, jnp.uint32(16)), F32)
    hi = lax.bitcast_convert_type(w & jnp.uint32(0xFFFF0000), F32)
    return lo, hi


def _const_spec(shape):
    zeros = (0,) * len(shape)
    return pl.BlockSpec(shape, lambda *_: zeros, pipeline_mode=pl.Buffered(1))


def _tile_copies(hbm, buf, sem, b, row0, slot, to_hbm):
    group = buf.shape[1]
    copies = []
    for r in range(V7X_SUBLANES):
        hbm_rows = hbm.at[b, pl.ds(row0 + group * r, group), :]
        vmem_rows = buf.at[slot, :, r, :]
        src, dst = (vmem_rows, hbm_rows) if to_hbm else (hbm_rows, vmem_rows)
        copies.append(pltpu.make_async_copy(src, dst, sem.at[slot]))
    return copies


def _lru_scan(a, u, h0):
    group = a.shape[0]
    acc_a = [a[0]]
    acc_u = [u[0]]
    for g in range(1, group):
        acc_a.append(a[g] * acc_a[-1])
        acc_u.append(a[g] * acc_u[-1] + u[g])
    end_a, end_u = acc_a[-1], acc_u[-1]
    sub = lax.broadcasted_iota(jnp.int32, end_a.shape, 0)
    shift = 1
    while shift < V7X_SUBLANES:
        keep = sub >= shift
        a_sh = pltpu.roll(end_a, shift, axis=0)
        u_sh = pltpu.roll(end_u, shift, axis=0)
        end_u = jnp.where(keep, end_a * u_sh + end_u, end_u)
        end_a = jnp.where(keep, end_a * a_sh, end_a)
        shift *= 2
    h_end = end_a * h0 + end_u
    h_in = jnp.where(sub == 0, h0, pltpu.roll(h_end, 1, axis=0))
    out = [acc_a[g] * h_in + acc_u[g] for g in range(group)]
    return jnp.stack(out, axis=0), h_end[V7X_SUBLANES - 1:V7X_SUBLANES, :]


def _mixer_kernel(x_hbm, mixn_ref, win_hbm, convw_ref, convb_ref, wa_ref, ba_ref, wi_ref, bi_ref,
                  lam_ref, lng_ref, lnb_ref, ws_ref, bsp_ref, wout_hbm, ffn_ref,
                  x1_hbm, hp_hbm,
                  xbuf, z0_ref, z1_ref, x1buf, hpbuf, xsem, x1sem, hpsem, wsm_ref, ztail_ref, hcar_ref,
                  win_ref, wout_ref, wsem,
                  *, nseq):
    j = pl.program_id(0)
    ntile = pl.num_programs(0) - 1
    _, group, _, d = xbuf.shape
    rows = group * V7X_SUBLANES
    half = d // 2
    ta = jnp.minimum(j, ntile - 1)
    tb = jnp.maximum(j - 1, 0)
    s = lax.rem(tb, nseq)
    slot = lax.rem(tb, 2)

    def fetch(t):
        return _tile_copies(x_hbm, xbuf, xsem, lax.div(t, nseq), lax.rem(t, nseq) * rows,
                            lax.rem(t, 3), to_hbm=False)

    def put(t):
        tb_, ts_, sl = lax.div(t, nseq), lax.rem(t, nseq) * rows, lax.rem(t, 2)
        return (_tile_copies(x1_hbm, x1buf, x1sem, tb_, ts_, sl, to_hbm=True)
                + _tile_copies(hp_hbm, hpbuf, hpsem, tb_, ts_, sl, to_hbm=True))

    @pl.when(j == 0)
    def _():
        for c in fetch(0):
            c.start()
        stage = (z0_ref, z1_ref)
        n_in_chunks = win_hbm.shape[0] // rows

        def win_copy(c):
            return pltpu.make_async_copy(win_hbm.at[pl.ds(c * rows, rows), :], stage[c % 2], wsem.at[c % 2])

        win_copy(0).start()
        for c in range(n_in_chunks):
            if c + 1 < n_in_chunks:
                win_copy(c + 1).start()
            win_copy(c).wait()
            plain = 4 * d
            win_ref[c * rows:(c + 1) * rows, :plain] = stage[c % 2][:, :plain].astype(BF16)
            win_ref[c * rows:(c + 1) * rows, plain:] = (0.5 * stage[c % 2][:, plain:]).astype(BF16)
        n_out_chunks = wout_hbm.shape[0] // rows
        out_copies = [pltpu.make_async_copy(wout_hbm.at[pl.ds(c * rows, rows), :],
                                            z0_ref.at[:, c * d:(c + 1) * d], wsem.at[0])
                      for c in range(n_out_chunks)]
        for cp in out_copies:
            cp.start()
        for cp in out_copies:
            cp.wait()
        for c in range(n_out_chunks):
            wout_ref[c * rows:(c + 1) * rows, :] = (0.25 * z0_ref[:, c * d:(c + 1) * d]).astype(BF16)
        z1_ref[...] = jnp.zeros_like(z1_ref)
        i_idx = lax.broadcasted_iota(jnp.int32, (rows, rows), 0)
        j_idx = lax.broadcasted_iota(jnp.int32, (rows, rows), 1)
        t_i = group * lax.rem(i_idx, V7X_SUBLANES) + lax.div(i_idx, V7X_SUBLANES)
        t_j = group * lax.rem(j_idx, V7X_SUBLANES) + lax.div(j_idx, V7X_SUBLANES)
        keep = (t_i >= t_j) & (lax.div(t_i, CHUNK) == lax.div(t_j, CHUNK))
        pick_rows = jnp.where(t_i == j_idx, 1.0, 0.0).astype(BF16)
        pick_cols = jnp.where(i_idx == t_j, 1.0, 0.0).astype(BF16)
        reps = rows // CHUNK
        for g in range(SGU_GROUPS):
            w_chunk = ws_ref[g].astype(BF16)
            w_rows = jnp.concatenate([w_chunk] * reps, axis=1)
            w_full = jnp.concatenate([w_rows] * reps, axis=0)
            w_perm = _dot(_dot(pick_rows, w_full).astype(BF16), pick_cols)
            wsm_ref[g] = jnp.where(keep, w_perm, 0.0).astype(BF16)

    @pl.when(j + 1 < ntile)
    def _():
        for c in fetch(j + 1):
            c.start()

    @pl.when(j < ntile)
    def _():
        for c in fetch(j):
            c.wait()

    @pl.when(s == 0)
    def _():
        ztail_ref[...] = jnp.zeros_like(ztail_ref)
        hcar_ref[...] = jnp.zeros_like(hcar_ref)

    def compute(z_w, z_r):
        xa_in = xbuf[lax.rem(ta, 3)].reshape(rows, d)
        h_next = _rmsnorm(xa_in, mixn_ref[...]).astype(BF16)
        pw = d // 2

        def project(k):
            z_w[:, k * pw:(k + 1) * pw] = _dot(h_next, win_ref[:, k * pw:(k + 1) * pw])

        x = xbuf[lax.rem(tb, 3)].reshape(rows, d)

        def sec(k, c0, c1):
            return z_r[:, k * d + c0:k * d + c1]

        def one_plus_tanh_gelu(v):
            c = 0.7978845608028654
            return 1.0 + jnp.tanh(v * (c + (c * 0.044715) * (v * v)))

        cw = 0.5 * convw_ref[...]
        cb_h = 0.5 * convb_ref[...]
        ba_h = 0.5 * ba_ref[...]
        bi_h = 0.5 * bi_ref[...]
        neg_lam = -lam_ref[...]
        softplus = jnp.maximum(neg_lam, 0.0) + jnp.log1p(jnp.exp(-jnp.abs(neg_lam)))
        c_a = (-0.5 * LRU_C) * softplus
        blk = V7X_MXU_DIM
        sub3 = lax.broadcasted_iota(jnp.int32, (CONV_WIDTH - 1, V7X_SUBLANES, blk), 1)
        term_a = []
        for n in range(d // blk):
            project(n)
            c0, c1 = n * blk, (n + 1) * blk
            z3 = sec(0, c0, c1).reshape(group, V7X_SUBLANES, blk)
            tail = z3[group - (CONV_WIDTH - 1):]
            halo = jnp.where(sub3 == 0, pltpu.roll(ztail_ref[:, :, c0:c1], 1, axis=1),
                             pltpu.roll(tail, 1, axis=1))
            ztail_ref[:, :, c0:c1] = tail
            zext = jnp.concatenate([halo, z3], axis=0)
            xa_h = cb_h[:, c0:c1] + cw[CONV_WIDTH - 1:CONV_WIDTH, c0:c1] * z3
            for k in range(1, CONV_WIDTH):
                lo = CONV_WIDTH - 1 - k
                xa_h = xa_h + cw[lo:lo + 1, c0:c1] * zext[lo:lo + group]
            xa2 = xa_h.reshape(rows, blk)
            xa_bf = xa2.astype(BF16)
            th_r = jnp.tanh(_dot(xa_bf, wa_ref[n]) + ba_h[:, c0:c1])
            th_i = jnp.tanh(_dot(xa_bf, wi_ref[n]) + bi_h[:, c0:c1])
            a = jnp.exp(c_a[:, c0:c1] + c_a[:, c0:c1] * th_r)
            u = jnp.sqrt(1.0 - a * a) * ((1.0 + th_i) * xa2)
            hseq, hlast = _lru_scan(a.reshape(group, V7X_SUBLANES, blk),
                                    u.reshape(group, V7X_SUBLANES, blk), hcar_ref[:, c0:c1])
            hcar_ref[:, c0:c1] = hlast
            zg = sec(1, c0, c1)
            term_a.append(((1.0 + jnp.tanh(sec(4, c0, c1))) * one_plus_tanh_gelu(zg))
                          * (zg * hseq.reshape(rows, blk)))

        project(4)
        zv = sec(3, 0, d)
        gv2 = zv * one_plus_tanh_gelu(zv)
        project(5)
        mu = jnp.mean(gv2, axis=-1, keepdims=True)
        xc = gv2 - mu
        var = jnp.mean(xc * xc, axis=-1, keepdims=True)
        v_bf = (xc * lax.rsqrt(var + 4.0 * EPS) * lng_ref[...] + lnb_ref[...]).astype(BF16)
        project(6)
        gdim = d // SGU_GROUPS
        term_b = []
        for g in range(SGU_GROUPS):
            c0, c1 = g * gdim, (g + 1) * gdim
            if g in (1, 3, 5, 6, 7):
                project({1: 7, 3: 8, 5: 9, 6: 10, 7: 11}[g])
            sp = _dot(wsm_ref[g], v_bf[:, c0:c1]) + bsp_ref[:, g:g + 1]
            zu = sec(2, c0, c1)
            term_b.append(((1.0 + jnp.tanh(sec(5, c0, c1))) * one_plus_tanh_gelu(zu)) * (zu * sp))
        merged4 = jnp.concatenate(term_a, axis=1) + jnp.concatenate(term_b, axis=1)

        x1 = x + _dot(merged4.astype(BF16), wout_ref[...])

        hn = _rmsnorm(x1, ffn_ref[...])
        hp = _pack_bf16_pair(hn[:, :half], hn[:, half:])

        @pl.when(j >= 3)
        def _():
            for c in put(tb - 2):
                c.wait()

        x1buf[slot] = x1.reshape(group, V7X_SUBLANES, d)
        hpbuf[slot] = hp.reshape(group, V7X_SUBLANES, half)

        @pl.when(j >= 1)
        def _():
            for c in put(tb):
                c.start()

    @pl.when(lax.rem(j, 2) == 0)
    def _():
        compute(z0_ref, z1_ref)

    @pl.when(lax.rem(j, 2) == 1)
    def _():
        compute(z1_ref, z0_ref)

    @pl.when(j == ntile)
    def _():
        for c in put(tb):
            c.wait()

        @pl.when(ntile >= 2)
        def _():
            for c in put(tb - 1):
                c.wait()


def _mixer_call(x, mix_norm, w_in, conv_w, conv_b, wa_blk, ba, wi_blk, bi, lam, ln_g, ln_b, ws,
                bs_tile, w_out, ffn_norm):
    cfg = _tiles()
    bsz, seq, d = x.shape
    ts = cfg["mixer_rows"]
    group = ts // V7X_SUBLANES
    nseq = seq // ts
    ntile = bsz * nseq
    row1 = (1, d)
    in_specs = [
        pl.BlockSpec(memory_space=pl.ANY),
        _const_spec(row1),
        pl.BlockSpec(memory_space=pl.ANY),
        _const_spec(conv_w.shape), _const_spec(row1),
        _const_spec(wa_blk.shape), _const_spec(row1),
        _const_spec(wi_blk.shape), _const_spec(row1),
        _const_spec(row1),
        _const_spec(row1), _const_spec(row1),
        _const_spec(ws.shape), _const_spec(bs_tile.shape),
        pl.BlockSpec(memory_space=pl.ANY), _const_spec(row1),
    ]
    out_shape = [
        jax.ShapeDtypeStruct((bsz, seq, d), F32),
        jax.ShapeDtypeStruct((bsz, seq, d // 2), U32),
    ]
    out_specs = [
        pl.BlockSpec(memory_space=pl.ANY),
        pl.BlockSpec(memory_space=pl.ANY),
    ]
    scratch = [
        pltpu.VMEM((3, group, V7X_SUBLANES, d), F32),
        pltpu.VMEM((ts, w_in.shape[1]), F32),
        pltpu.VMEM((ts, w_in.shape[1]), F32),
        pltpu.VMEM((2, group, V7X_SUBLANES, d), F32),
        pltpu.VMEM((2, group, V7X_SUBLANES, d // 2), U32),
        pltpu.SemaphoreType.DMA((3,)),
        pltpu.SemaphoreType.DMA((2,)),
        pltpu.SemaphoreType.DMA((2,)),
        pltpu.VMEM((SGU_GROUPS, ts, ts), BF16),
        pltpu.VMEM((CONV_WIDTH - 1, V7X_SUBLANES, d), F32),
        pltpu.VMEM((1, d), F32),
        pltpu.VMEM(w_in.shape, BF16),
        pltpu.VMEM(w_out.shape, BF16),
        pltpu.SemaphoreType.DMA((2,)),
    ]
    return pl.pallas_call(
        functools.partial(_mixer_kernel, nseq=nseq),
        grid=(ntile + 1,),
        in_specs=in_specs,
        out_specs=out_specs,
        out_shape=out_shape,
        scratch_shapes=scratch,
        compiler_params=pltpu.CompilerParams(
            dimension_semantics=("arbitrary",),
            vmem_limit_bytes=cfg["mixer_vmem"]),
        name="mixer",
    )(x, mix_norm, w_in, conv_w, conv_b, wa_blk, ba, wi_blk, bi, lam, ln_g, ln_b, ws, bs_tile,
      w_out, ffn_norm)


def _router_kernel(hp_ref, wg_ref, we_ref, br_ref, pos_ref, gate_ref, cnt_ref, ccar_ref, wr_ref,
                   *, expert_capacity):
    rows = hp_ref.shape[0]

    @pl.when(pl.program_id(0) == 0)
    def _():
        ccar_ref[...] = jnp.zeros_like(ccar_ref)
        wr_ref[...] = jnp.zeros_like(wr_ref)
        wr_ref[:, 0:N_GROUPS] = wg_ref[...].astype(BF16)
        wr_ref[:, EXPERT_ROW0:EXPERT_ROW0 + N_EXPERTS] = we_ref[...].astype(BF16)

    sub_rows = rows // ROUTER_SUBBLOCKS
    lts = []
    for q in range(ROUTER_SUBBLOCKS):
        lo, hi = _unpack_bf16_pair(hp_ref[q * sub_rows:(q + 1) * sub_rows, :])
        hn = jnp.concatenate([lo, hi], axis=1)
        logits = _dot(hn.astype(BF16), wr_ref[...])
        lts.append(jnp.transpose(logits) + br_ref[...])
    sub = lax.broadcasted_iota(jnp.int32, (V7X_SUBLANES, sub_rows), 0)
    subf = sub.astype(F32)
    big = jnp.float32(1e9)
    eid = lax.broadcasted_iota(jnp.int32, (N_EXPERTS, sub_rows), 0).astype(F32)
    sb = V7X_MXU_DIM
    before = (lax.broadcasted_iota(jnp.int32, (sb, sb), 0)
              < lax.broadcasted_iota(jnp.int32, (sb, sb), 1))
    before = jnp.where(before, 1.0, 0.0).astype(BF16)
    cap = float(expert_capacity)
    zero = jnp.zeros((V7X_SUBLANES - TOP_K, sub_rows), F32)
    running = ccar_ref[:, 0:1]
    for q, lt in enumerate(lts):
        lg = jnp.where(sub < N_GROUPS, lt[0:V7X_SUBLANES, :], -jnp.inf)
        g_exp = jnp.exp(lg - jnp.max(lg, axis=0, keepdims=True))
        g_prob = g_exp / jnp.sum(g_exp, axis=0, keepdims=True)
        g_top = jnp.max(g_prob, axis=0, keepdims=True)
        g_idx = jnp.min(jnp.where(g_prob == g_top, subf, big), axis=0, keepdims=True)

        e_sel = jnp.zeros((EXPERTS_PER_GROUP, sub_rows), F32)
        for g in range(N_GROUPS):
            r0 = EXPERT_ROW0 + g * EXPERTS_PER_GROUP
            e_sel = jnp.where(g_idx == g, lt[r0:r0 + EXPERTS_PER_GROUP, :], e_sel)
        e_exp = jnp.exp(e_sel - jnp.max(e_sel, axis=0, keepdims=True))
        e_prob = e_exp / jnp.sum(e_exp, axis=0, keepdims=True)
        p1 = jnp.max(e_prob, axis=0, keepdims=True)
        i1 = jnp.min(jnp.where(e_prob == p1, subf, big), axis=0, keepdims=True)
        rest = jnp.where(subf == i1, -1.0, e_prob)
        p2 = jnp.max(rest, axis=0, keepdims=True)
        i2 = jnp.min(jnp.where(rest == p2, subf, big), axis=0, keepdims=True)
        psum = p1 + p2
        gate1 = g_top * (p1 / psum)
        gate2 = g_top * (p2 / psum)
        gid1 = g_idx * EXPERTS_PER_GROUP + i1
        gid2 = g_idx * EXPERTS_PER_GROUP + i2

        hit1 = eid == gid1
        hit2 = eid == gid2
        cnt = jnp.where(hit1 | hit2, 1.0, 0.0)
        base = []
        for c in range(sub_rows // sb):
            part = cnt[:, c * sb:(c + 1) * sb]
            base.append(running + _dot(part.astype(BF16), before))
            running = running + jnp.sum(part, axis=1, keepdims=True)
        base = jnp.concatenate(base, axis=1)
        rank1 = jnp.sum(jnp.where(hit1, base, 0.0), axis=0, keepdims=True)
        rank2 = jnp.sum(jnp.where(hit2, base, 0.0), axis=0, keepdims=True)
        pos = jnp.concatenate([gid1 * cap + rank1, gid2 * cap + rank2, zero], axis=0)
        pos_ref[:, q * sub_rows:(q + 1) * sub_rows] = pos.astype(jnp.int32)
        gate_ref[q * sub_rows:(q + 1) * sub_rows, :] = jnp.transpose(
            jnp.concatenate([gate1, gate2, zero], axis=0))
    total = jnp.broadcast_to(running, ccar_ref.shape)
    ccar_ref[...] = total
    cnt_ref[...] = total


def _router_call(hp, w_group, w_expert, b_router):
    cfg = _tiles()
    ntok, half = hp.shape
    tr = cfg["router_rows"]
    return pl.pallas_call(
        functools.partial(_router_kernel, expert_capacity=ntok),
        grid=(ntok // tr,),
        in_specs=[
            pl.BlockSpec((tr, half), lambda i: (i, 0)),
            _const_spec(w_group.shape),
            _const_spec(w_expert.shape),
            _const_spec(b_router.shape),
        ],
        out_specs=[
            pl.BlockSpec((V7X_SUBLANES, tr), lambda i: (0, i)),
            pl.BlockSpec((tr, V7X_SUBLANES), lambda i: (i, 0)),
            pl.BlockSpec((N_EXPERTS, V7X_LANES), lambda i: (0, 0)),
        ],
        out_shape=[
            jax.ShapeDtypeStruct((V7X_SUBLANES, ntok), jnp.int32),
            jax.ShapeDtypeStruct((ntok, V7X_SUBLANES), F32),
            jax.ShapeDtypeStruct((N_EXPERTS, V7X_LANES), F32),
        ],
        scratch_shapes=[
            pltpu.VMEM((N_EXPERTS, V7X_LANES), F32),
            pltpu.VMEM((w_group.shape[0], ROUTER_ROWS), BF16),
        ],
        compiler_params=pltpu.CompilerParams(
            dimension_semantics=("arbitrary",),
            vmem_limit_bytes=cfg["router_vmem"]),
        name="router",
    )(hp, w_group, w_expert, b_router)


def _expert_kernel(nt_ref, base_ref, texp_ref, tloc_ref, hs_hbm, w1_ref, w3_ref, w2_ref, ys_hbm,
                   hbuf, ybuf, hsem, ysem, w1b_ref, w3b_ref, w2b_ref, *, capacity):
    e = pl.program_id(0)
    n_exp = pl.num_programs(0)
    nt = nt_ref[e]
    base = base_ref[e]
    total = base_ref[n_exp - 1] + nt_ref[n_exp - 1]
    n_in, tm, _ = hbuf.shape
    n_out = ybuf.shape[0]
    ahead = n_in - 1

    def load(g):
        slot = lax.rem(g, n_in)
        rows = pl.ds(texp_ref[g] * capacity + tloc_ref[g] * tm, tm)
        return pltpu.make_async_copy(hs_hbm.at[rows], hbuf.at[slot], hsem.at[slot])

    def store(t, slot):
        rows = pl.ds(e * capacity + t * tm, tm)
        return pltpu.make_async_copy(ybuf.at[slot], ys_hbm.at[rows], ysem.at[slot])

    @pl.when(e == 0)
    def _():
        for g0 in range(ahead):
            @pl.when(g0 < total)
            def _():
                load(g0).start()

    w1b_ref[...] = w1_ref[...].astype(BF16)
    w3b_ref[...] = w3_ref[...].astype(BF16)
    w2b_ref[...] = w2_ref[...].astype(BF16)

    @pl.loop(0, nt)
    def _(t):
        g = base + t

        @pl.when(g + ahead < total)
        def _():
            load(g + ahead).start()

        load(g).wait()
        slot = lax.rem(g, n_out)

        @pl.when(g >= n_out)
        def _():
            store(t, slot).wait()

        sub_rows = tm // EXPERT_SUBBLOCKS
        blocks = [pl.ds(q * sub_rows, sub_rows) for q in range(EXPERT_SUBBLOCKS)]
        rows_in = []
        for rs in blocks:
            lo, hi = _unpack_bf16_pair(hbuf[lax.rem(g, n_in), rs, :])
            rows_in.append(jnp.concatenate([lo, hi], axis=1).astype(BF16))
        up = [(_dot(h, w1b_ref[...]), _dot(h, w3b_ref[...])) for h in rows_in]
        down = []
        for a, b in up:
            hid = (a * _sigmoid(a)) * b
            down.append(_dot(hid.astype(BF16), w2b_ref[...]))
        for rs, y in zip(blocks, down):
            half = y.shape[1] // 2
            ybuf[slot, rs, :] = _pack_bf16_pair(y[:, :half], y[:, half:])
        store(t, slot).start()

    @pl.when(e + 1 == n_exp)
    def _():
        for back in range(1, n_out + 1):
            @pl.when(total >= back)
            def _():
                store(0, lax.rem(total - back, n_out)).wait()


def _expert_call(tiles_per_expert, hs, w1, w3, w2, capacity):
    cfg = _tiles()
    tm = cfg["expert_rows"]
    prow, half = hs.shape
    n_exp, d, f = w1.shape
    ends = jnp.cumsum(tiles_per_expert)
    base = ends - tiles_per_expert
    g = jnp.arange(capacity * TOP_K // tm + n_exp, dtype=jnp.int32)
    texp = jnp.minimum(jnp.sum((ends[None, :] <= g[:, None]).astype(jnp.int32), axis=1), n_exp - 1)
    onehot = texp[:, None] == jnp.arange(n_exp, dtype=jnp.int32)[None, :]
    tloc = g - jnp.sum(jnp.where(onehot, base[None, :], 0), axis=1)

    def w_map(e, *_):
        return (e, 0, 0)

    grid_spec = pltpu.PrefetchScalarGridSpec(
        num_scalar_prefetch=4,
        grid=(n_exp,),
        in_specs=[
            pl.BlockSpec(memory_space=pl.ANY),
            pl.BlockSpec((None, d, f), w_map),
            pl.BlockSpec((None, d, f), w_map),
            pl.BlockSpec((None, f, d), w_map),
        ],
        out_specs=pl.BlockSpec(memory_space=pl.ANY),
        scratch_shapes=[
            pltpu.VMEM((EXPERT_LOOKAHEAD + 1, tm, half), U32),
            pltpu.VMEM((2, tm, half), U32),
            pltpu.SemaphoreType.DMA((EXPERT_LOOKAHEAD + 1,)),
            pltpu.SemaphoreType.DMA((2,)),
            pltpu.VMEM((d, f), BF16),
            pltpu.VMEM((d, f), BF16),
            pltpu.VMEM((f, d), BF16),
        ],
    )
    return pl.pallas_call(
        functools.partial(_expert_kernel, capacity=capacity),
        grid_spec=grid_spec,
        out_shape=jax.ShapeDtypeStruct((prow, half), U32),
        compiler_params=pltpu.CompilerParams(
            dimension_semantics=("arbitrary",),
            vmem_limit_bytes=cfg["expert_vmem"]),
        name="experts",
    )(tiles_per_expert, base, texp, tloc, hs, w1, w3, w2)


def _sc_mesh():
    return plsc.VectorSubcoreMesh(core_axis_name="c", subcore_axis_name="s",
                                  num_cores=V7X_SC_CORES, num_subcores=V7X_SC_SUBCORES)


def _sc_worker_id():
    return lax.axis_index("s") * V7X_SC_CORES + lax.axis_index("c")


def _dispatch_call(hp, pos_w, out_rows):
    cfg = _tiles()
    ntok, half = hp.shape
    nw, topk, nch, ch = pos_w.shape
    per_w = nch * ch

    def body(hp_hbm, pos_hbm, hs_hbm, idx_v, buf, wsem):
        wid = _sc_worker_id()
        pltpu.sync_copy(pos_hbm.at[wid], idx_v)

        for c in range(nch):
            pltpu.sync_copy(hp_hbm.at[pl.ds(wid * per_w + c * ch, ch)], buf)
            writes = [pltpu.make_async_copy(buf, hs_hbm.at[idx_v.at[k, c]], wsem.at[k]) for k in range(topk)]
            for w in writes:
                w.start()
            for w in writes:
                w.wait()

    assert nw == V7X_SC_CORES * V7X_SC_SUBCORES and nw * per_w == ntok and ch == cfg["sc_rows"]
    return pl.kernel(
        body,
        out_type=jax.ShapeDtypeStruct((out_rows, half), U32),
        mesh=_sc_mesh(),
        scratch_types=[
            pltpu.VMEM((topk, nch, ch), jnp.int32),
            pltpu.VMEM((ch, half), U32),
            pltpu.SemaphoreType.DMA((topk,)),
        ],
        name="dispatch",
    )(hp, pos_w)


def _combine_call(ys, pos_w):
    cfg = _tiles()
    _, half = ys.shape
    nw, topk, nch, ch = pos_w.shape
    per_w = nch * ch
    ntok = nw * per_w

    def body(ys_hbm, pos_hbm, *rest):
        outs = rest[:topk]
        idx_v, buf = rest[topk:]
        wid = _sc_worker_id()
        pltpu.sync_copy(pos_hbm.at[wid], idx_v)
        for c in range(nch):
            for k in range(topk):
                pltpu.sync_copy(ys_hbm.at[idx_v.at[k, c]], buf)
                pltpu.sync_copy(buf, outs[k].at[pl.ds(wid * per_w + c * ch, ch)])

    assert nw == V7X_SC_CORES * V7X_SC_SUBCORES and ch == cfg["sc_rows"]
    return pl.kernel(
        body,
        out_type=[jax.ShapeDtypeStruct((ntok, half), U32)] * topk,
        mesh=_sc_mesh(),
        scratch_types=[
            pltpu.VMEM((topk, nch, ch), jnp.int32),
            pltpu.VMEM((ch, half), U32),
        ],
        name="combine",
    )(ys, pos_w)


def _ple_kernel(x1_ref, yg0_ref, yg1_ref, gate_ref, p_ref, plen_ref, wg32_ref, wu32_ref, fin_ref, o_ref,
                wg_ref, wu_ref):
    @pl.when(pl.program_id(0) == 0)
    def _():
        wg_ref[...] = (0.5 * wg32_ref[...]).astype(BF16)
        wu_ref[...] = (0.5 * wu32_ref[...]).astype(BF16)

    rows = x1_ref.shape[0]
    sub_rows = rows // PLE_SUBBLOCKS
    for q in range(PLE_SUBBLOCKS):
        rs = pl.ds(q * sub_rows, sub_rows)
        lo0, hi0 = _unpack_bf16_pair(yg0_ref[rs, :])
        lo1, hi1 = _unpack_bf16_pair(yg1_ref[rs, :])
        g0 = gate_ref[rs, 0:1]
        g1 = gate_ref[rs, 1:2]
        moe = g0 * jnp.concatenate([lo0, hi0], axis=1) + g1 * jnp.concatenate([lo1, hi1], axis=1)
        x2 = x1_ref[rs, :] + moe
        r = _rmsnorm(x2, plen_ref[...]).astype(BF16)
        gt2 = 1.0 + jnp.tanh(_dot(r, wg_ref[...]))
        up_h = _dot(p_ref[rs, :].astype(BF16), wu_ref[...])
        x3 = x2 + gt2 * up_h
        o_ref[rs, :] = _rmsnorm(x3, fin_ref[...])


def _ple_call(x1, yg0, yg1, gates, p, ple_norm, wg, wu, final_norm):
    cfg = _tiles()
    ntok, d = x1.shape
    tp = cfg["ple_rows"]
    pdim = p.shape[1]
    return pl.pallas_call(
        _ple_kernel,
        grid=(ntok // tp,),
        in_specs=[
            pl.BlockSpec((tp, d), lambda i: (i, 0)),
            pl.BlockSpec((tp, d // 2), lambda i: (i, 0)),
            pl.BlockSpec((tp, d // 2), lambda i: (i, 0)),
            pl.BlockSpec((tp, V7X_SUBLANES), lambda i: (i, 0)),
            pl.BlockSpec((tp, pdim), lambda i: (i, 0)),
            _const_spec((1, d)),
            _const_spec(wg.shape),
            _const_spec(wu.shape),
            _const_spec((1, d)),
        ],
        out_specs=pl.BlockSpec((tp, d), lambda i: (i, 0)),
        out_shape=jax.ShapeDtypeStruct((ntok, d), F32),
        scratch_shapes=[pltpu.VMEM(wg.shape, BF16), pltpu.VMEM(wu.shape, BF16)],
        compiler_params=pltpu.CompilerParams(
            dimension_semantics=("arbitrary",),
            vmem_limit_bytes=cfg["ple_vmem"]),
        name="ple",
    )(x1, yg0, yg1, gates, p, ple_norm, wg, wu, final_norm)


def _blockdiag_pack(w):
    nb, bd, _ = w.shape
    per = V7X_MXU_DIM // bd
    w4 = w.reshape(nb // per, per, bd, bd)
    eye = jnp.eye(per, dtype=w.dtype)
    out = jnp.einsum("jpab,pq->jpaqb", w4, eye)
    return out.reshape(nb // per, V7X_MXU_DIM, V7X_MXU_DIM).astype(BF16)


def kernel(x, p, mix_norm, w_in, conv_w, conv_b, lru_wa, lru_ba, lru_wi, lru_bi, lru_lambda, sgu_ln_g, sgu_ln_b, sgu_ws, sgu_bs, w_out, ffn_norm, router_group_w, router_group_b, router_expert_w, router_expert_b, expert_w1, expert_w3, expert_w2, ple_norm, ple_gate_w, ple_up_w, final_norm):
    cfg = _tiles()
    bsz, seq, d = x.shape
    ntok = bsz * seq
    tm = cfg["expert_rows"]
    depth = w_in.shape[0]
    assert depth == 1, "the ple kernel applies the final norm, so it must be the last layer"
    l = 0
    nw_rows = V7X_SC_CORES * V7X_SC_SUBCORES * cfg["sc_rows"]
    assert cfg["mixer_rows"] % CHUNK == 0 and seq % cfg["mixer_rows"] == 0
    assert ntok % cfg["router_rows"] == 0 and ntok % cfg["ple_rows"] == 0 and ntok % nw_rows == 0
    assert ntok % tm == 0 and lru_wa.shape[1:] == (LRU_BLOCKS, d // LRU_BLOCKS, d // LRU_BLOCKS)
    assert max(cfg[k] for k in cfg if k.endswith("_vmem")) < V7X_VMEM_BYTES
    b_router = jnp.concatenate([
        router_group_b[l], jnp.zeros((EXPERT_ROW0 - N_GROUPS,), F32), router_expert_b[l],
        jnp.zeros((ROUTER_ROWS - EXPERT_ROW0 - N_EXPERTS,), F32)])[:, None]
    ts = cfg["mixer_rows"]
    group = ts // V7X_SUBLANES
    bs_tile = jnp.tile(sgu_bs[l], (1, ts // CHUNK)).reshape(SGU_GROUPS, V7X_SUBLANES, group)
    bs_tile = jnp.transpose(bs_tile, (2, 1, 0)).reshape(ts, SGU_GROUPS)
    x1, hp = _mixer_call(
        x, mix_norm[l][None], w_in[l], conv_w[l], conv_b[l][None],
        _blockdiag_pack(lru_wa[l]), lru_ba[l][None], _blockdiag_pack(lru_wi[l]), lru_bi[l][None],
        lru_lambda[l][None], sgu_ln_g[l][None], sgu_ln_b[l][None], sgu_ws[l], bs_tile,
        w_out[l], ffn_norm[l][None])
    hp = hp.reshape(ntok, d // 2)
    pos, gate, cnt = _router_call(hp, router_group_w[l], router_expert_w[l], b_router)

    cap = ntok
    tiles_per_expert = (cnt[:, 0].astype(jnp.int32) + tm - 1) // tm
    nw = V7X_SC_CORES * V7X_SC_SUBCORES
    ch = cfg["sc_rows"]
    pos_w = jnp.transpose(pos[:TOP_K].reshape(TOP_K, nw, ntok // (nw * ch), ch), (1, 0, 2, 3))

    hs = _dispatch_call(hp, pos_w, N_EXPERTS * cap)
    ys = _expert_call(tiles_per_expert, hs, expert_w1[l], expert_w3[l], expert_w2[l], cap)
    yg0, yg1 = _combine_call(ys, pos_w)

    out = _ple_call(x1.reshape(ntok, d), yg0, yg1, gate, p[l].reshape(ntok, -1), ple_norm[l][None],
                    ple_gate_w[l], ple_up_w[l], final_norm[None])
    return out.reshape(bsz, seq, d)
```

```python
import functools

import jax
import jax.numpy as jnp
from jax import lax
from jax.experimental import pallas as pl
from jax.experimental.pallas import tpu as pltpu
from jax.experimental.pallas import tpu_sc as plsc

F32 = jnp.float32
BF16 = jnp.bfloat16
U32 = jnp.uint32

LRU_BLOCKS = 16
CONV_WIDTH = 4
LRU_C = 8.0
SGU_GROUPS = 8
CHUNK = 128
N_GROUPS = 4
EXPERTS_PER_GROUP = 8
N_EXPERTS = N_GROUPS * EXPERTS_PER_GROUP
TOP_K = 2
EPS = 1e-6

V7X_MXU_DIM = 256
V7X_SUBLANES = 8
V7X_LANES = 128
V7X_VMEM_BYTES = 64 * 1024 * 1024
V7X_SC_CORES = 2
V7X_SC_SUBCORES = 16

EXPERT_LOOKAHEAD = 3
ROUTER_SUBBLOCKS = 8
EXPERT_SUBBLOCKS = 2
PLE_SUBBLOCKS = 4
ROUTER_ROWS = V7X_LANES
EXPERT_ROW0 = V7X_SUBLANES


def _tiles():
    return dict(
        mixer_rows=256,
        expert_rows=512,
        ple_rows=1024,
        sc_rows=128,
        router_rows=4096,
        mixer_vmem=52 * 1024 * 1024,
        expert_vmem=40 * 1024 * 1024,
        ple_vmem=48 * 1024 * 1024,
        router_vmem=40 * 1024 * 1024,
    )


def _dot(a, b):
    return jnp.dot(a, b, preferred_element_type=F32)


def _sigmoid(x):
    return 0.5 * jnp.tanh(0.5 * x) + 0.5


def _rmsnorm(x, g):
    ms = jnp.mean(x * x, axis=-1, keepdims=True)
    return x * lax.rsqrt(ms + EPS) * g


def _pack_bf16_pair(lo, hi):
    lo_b = lax.bitcast_convert_type(lo.astype(BF16).astype(F32), U32)
    hi_b = lax.bitcast_convert_type(hi.astype(BF16).astype(F32), U32)
    return (hi_b & jnp.uint32(0xFFFF0000)) | lax.shift_right_logical(lo_b, jnp.uint32(16))


def _unpack_bf16_pair(w):
    lo = lax.bitcast_convert_type(lax.shift_left(w, jnp.uint32(16)), F32)
    hi = lax.bitcast_convert_type(w & jnp.uint32(0xFFFF0000), F32)
    return lo, hi


def _const_spec(shape):
    zeros = (0,) * len(shape)
    return pl.BlockSpec(shape, lambda *_: zeros, pipeline_mode=pl.Buffered(1))


def _tile_copies(hbm, buf, sem, b, row0, slot, to_hbm):
    group = buf.shape[1]
    copies = []
    for r in range(V7X_SUBLANES):
        hbm_rows = hbm.at[b, pl.ds(row0 + group * r, group), :]
        vmem_rows = buf.at[slot, :, r, :]
        src, dst = (vmem_rows, hbm_rows) if to_hbm else (hbm_rows, vmem_rows)
        copies.append(pltpu.make_async_copy(src, dst, sem.at[slot]))
    return copies


def _lru_scan(a, u, h0):
    group = a.shape[0]
    acc_a = [a[0]]
    acc_u = [u[0]]
    for g in range(1, group):
        acc_a.append(a[g] * acc_a[-1])
        acc_u.append(a[g] * acc_u[-1] + u[g])
    end_a, end_u = acc_a[-1], acc_u[-1]
    sub = lax.broadcasted_iota(jnp.int32, end_a.shape, 0)
    shift = 1
    while shift < V7X_SUBLANES:
        keep = sub >= shift
        a_sh = pltpu.roll(end_a, shift, axis=0)
        u_sh = pltpu.roll(end_u, shift, axis=0)
        end_u = jnp.where(keep, end_a * u_sh + end_u, end_u)
        end_a = jnp.where(keep, end_a * a_sh, end_a)
        shift *= 2
    h_end = end_a * h0 + end_u
    h_in = jnp.where(sub == 0, h0, pltpu.roll(h_end, 1, axis=0))
    out = [acc_a[g] * h_in + acc_u[g] for g in range(group)]
    return jnp.stack(out, axis=0), h_end[V7X_SUBLANES - 1:V7X_SUBLANES, :]


def _mixer_kernel(x_hbm, mixn_ref, win_hbm, convw_ref, convb_ref, wa_ref, ba_ref, wi_ref, bi_ref,
                  lam_ref, lng_ref, lnb_ref, ws_ref, bsp_ref, wout_hbm, ffn_ref,
                  x1_hbm, hp_hbm,
                  xbuf, z0_ref, z1_ref, x1buf, hpbuf, xsem, x1sem, hpsem, wsm_ref, ztail_ref, hcar_ref,
                  win_ref, wout_ref, wsem,
                  *, nseq):
    j = pl.program_id(0)
    ntile = pl.num_programs(0) - 1
    _, group, _, d = xbuf.shape
    rows = group * V7X_SUBLANES
    half = d // 2
    ta = jnp.minimum(j, ntile - 1)
    tb = jnp.maximum(j - 1, 0)
    s = lax.rem(tb, nseq)
    slot = lax.rem(tb, 2)

    def fetch(t):
        return _tile_copies(x_hbm, xbuf, xsem, lax.div(t, nseq), lax.rem(t, nseq) * rows,
                            lax.rem(t, 3), to_hbm=False)

    def put(t):
        tb_, ts_, sl = lax.div(t, nseq), lax.rem(t, nseq) * rows, lax.rem(t, 2)
        return (_tile_copies(x1_hbm, x1buf, x1sem, tb_, ts_, sl, to_hbm=True)
                + _tile_copies(hp_hbm, hpbuf, hpsem, tb_, ts_, sl, to_hbm=True))

    @pl.when(j == 0)
    def _():
        for c in fetch(0):
            c.start()
        stage = (z0_ref, z1_ref)
        n_in_chunks = win_hbm.shape[0] // rows

        def win_copy(c):
            return pltpu.make_async_copy(win_hbm.at[pl.ds(c * rows, rows), :], stage[c % 2], wsem.at[c % 2])

        win_copy(0).start()
        for c in range(n_in_chunks):
            if c + 1 < n_in_chunks:
                win_copy(c + 1).start()
            win_copy(c).wait()
            plain = 4 * d
            win_ref[c * rows:(c + 1) * rows, :plain] = stage[c % 2][:, :plain].astype(BF16)
            win_ref[c * rows:(c + 1) * rows, plain:] = (0.5 * stage[c % 2][:, plain:]).astype(BF16)
        n_out_chunks = wout_hbm.shape[0] // rows
        out_copies = [pltpu.make_async_copy(wout_hbm.at[pl.ds(c * rows, rows), :],
                                            z0_ref.at[:, c * d:(c + 1) * d], wsem.at[0])
                      for c in range(n_out_chunks)]
        for cp in out_copies:
            cp.start()
        for cp in out_copies:
            cp.wait()
        for c in range(n_out_chunks):
            wout_ref[c * rows:(c + 1) * rows, :] = (0.25 * z0_ref[:, c * d:(c + 1) * d]).astype(BF16)
        z1_ref[...] = jnp.zeros_like(z1_ref)
        i_idx = lax.broadcasted_iota(jnp.int32, (rows, rows), 0)
        j_idx = lax.broadcasted_iota(jnp.int32, (rows, rows), 1)
        t_i = group * lax.rem(i_idx, V7X_SUBLANES) + lax.div(i_idx, V7X_SUBLANES)
        t_j = group * lax.rem(j_idx, V7X_SUBLANES) + lax.div(j_idx, V7X_SUBLANES)
        keep = (t_i >= t_j) & (lax.div(t_i, CHUNK) == lax.div(t_j, CHUNK))
        pick_rows = jnp.where(t_i == j_idx, 1.0, 0.0).astype(BF16)
        pick_cols = jnp.where(i_idx == t_j, 1.0, 0.0).astype(BF16)
        reps = rows // CHUNK
        for g in range(SGU_GROUPS):
            w_chunk = ws_ref[g].astype(BF16)
            w_rows = jnp.concatenate([w_chunk] * reps, axis=1)
            w_full = jnp.concatenate([w_rows] * reps, axis=0)
            w_perm = _dot(_dot(pick_rows, w_full).astype(BF16), pick_cols)
            wsm_ref[g] = jnp.where(keep, w_perm, 0.0).astype(BF16)

    @pl.when(j + 1 < ntile)
    def _():
        for c in fetch(j + 1):
            c.start()

    @pl.when(j < ntile)
    def _():
        for c in fetch(j):
            c.wait()

    @pl.when(s == 0)
    def _():
        ztail_ref[...] = jnp.zeros_like(ztail_ref)
        hcar_ref[...] = jnp.zeros_like(hcar_ref)

    def compute(z_w, z_r):
        xa_in = xbuf[lax.rem(ta, 3)].reshape(rows, d)
        h_next = _rmsnorm(xa_in, mixn_ref[...]).astype(BF16)
        pw = d // 2

        def project(k):
            z_w[:, k * pw:(k + 1) * pw] = _dot(h_next, win_ref[:, k * pw:(k + 1) * pw])

        x = xbuf[lax.rem(tb, 3)].reshape(rows, d)

        def sec(k, c0, c1):
            return z_r[:, k * d + c0:k * d + c1]

        def one_plus_tanh_gelu(v):
            c = 0.7978845608028654
            return 1.0 + jnp.tanh(v * (c + (c * 0.044715) * (v * v)))

        cw = 0.5 * convw_ref[...]
        cb_h = 0.5 * convb_ref[...]
        ba_h = 0.5 * ba_ref[...]
        bi_h = 0.5 * bi_ref[...]
        neg_lam = -lam_ref[...]
        softplus = jnp.maximum(neg_lam, 0.0) + jnp.log1p(jnp.exp(-jnp.abs(neg_lam)))
        c_a = (-0.5 * LRU_C) * softplus
        blk = V7X_MXU_DIM
        sub3 = lax.broadcasted_iota(jnp.int32, (CONV_WIDTH - 1, V7X_SUBLANES, blk), 1)
        term_a = []
        for n in range(d // blk):
            project(2 * n)
            c0, c1 = n * blk, (n + 1) * blk
            z3 = sec(0, c0, c1).reshape(group, V7X_SUBLANES, blk)
            tail = z3[group - (CONV_WIDTH - 1):]
            halo = jnp.where(sub3 == 0, pltpu.roll(ztail_ref[:, :, c0:c1], 1, axis=1),
                             pltpu.roll(tail, 1, axis=1))
            ztail_ref[:, :, c0:c1] = tail
            zext = jnp.concatenate([halo, z3], axis=0)
            xa_h = cb_h[:, c0:c1] + cw[CONV_WIDTH - 1:CONV_WIDTH, c0:c1] * z3
            for k in range(1, CONV_WIDTH):
                lo = CONV_WIDTH - 1 - k
                xa_h = xa_h + cw[lo:lo + 1, c0:c1] * zext[lo:lo + group]
            xa2 = xa_h.reshape(rows, blk)
            xa_bf = xa2.astype(BF16)
            th_r = jnp.tanh(_dot(xa_bf, wa_ref[n]) + ba_h[:, c0:c1])
            th_i = jnp.tanh(_dot(xa_bf, wi_ref[n]) + bi_h[:, c0:c1])
            a = jnp.exp(c_a[:, c0:c1] + c_a[:, c0:c1] * th_r)
            u = jnp.sqrt(1.0 - a * a) * ((1.0 + th_i) * xa2)
            project(2 * n + 1)
            hseq, hlast = _lru_scan(a.reshape(group, V7X_SUBLANES, blk),
                                    u.reshape(group, V7X_SUBLANES, blk), hcar_ref[:, c0:c1])
            hcar_ref[:, c0:c1] = hlast
            zg = sec(1, c0, c1)
            term_a.append(((1.0 + jnp.tanh(sec(4, c0, c1))) * one_plus_tanh_gelu(zg))
                          * (zg * hseq.reshape(rows, blk)))

        project(8)
        zv = sec(3, 0, d)
        gv2 = zv * one_plus_tanh_gelu(zv)
        mu = jnp.mean(gv2, axis=-1, keepdims=True)
        xc = gv2 - mu
        var = jnp.mean(xc * xc, axis=-1, keepdims=True)
        v_bf = (xc * lax.rsqrt(var + 4.0 * EPS) * lng_ref[...] + lnb_ref[...]).astype(BF16)
        project(9)
        gdim = d // SGU_GROUPS
        term_b = []
        for g in range(SGU_GROUPS):
            c0, c1 = g * gdim, (g + 1) * gdim
            if g in (2, 5):
                project({2: 10, 5: 11}[g])
            sp = _dot(wsm_ref[g], v_bf[:, c0:c1]) + bsp_ref[:, g:g + 1]
            zu = sec(2, c0, c1)
            term_b.append(((1.0 + jnp.tanh(sec(5, c0, c1))) * one_plus_tanh_gelu(zu)) * (zu * sp))
        merged4 = jnp.concatenate(term_a, axis=1) + jnp.concatenate(term_b, axis=1)

        x1 = x + _dot(merged4.astype(BF16), wout_ref[...])

        hn = _rmsnorm(x1, ffn_ref[...])
        hp = _pack_bf16_pair(hn[:, :half], hn[:, half:])

        @pl.when(j >= 3)
        def _():
            for c in put(tb - 2):
                c.wait()

        x1buf[slot] = x1.reshape(group, V7X_SUBLANES, d)
        hpbuf[slot] = hp.reshape(group, V7X_SUBLANES, half)

        @pl.when(j >= 1)
        def _():
            for c in put(tb):
                c.start()

    @pl.when(lax.rem(j, 2) == 0)
    def _():
        compute(z0_ref, z1_ref)

    @pl.when(lax.rem(j, 2) == 1)
    def _():
        compute(z1_ref, z0_ref)

    @pl.when(j == ntile)
    def _():
        for c in put(tb):
            c.wait()

        @pl.when(ntile >= 2)
        def _():
            for c in put(tb - 1):
                c.wait()


def _mixer_call(x, mix_norm, w_in, conv_w, conv_b, wa_blk, ba, wi_blk, bi, lam, ln_g, ln_b, ws,
                bs_tile, w_out, ffn_norm):
    cfg = _tiles()
    bsz, seq, d = x.shape
    ts = cfg["mixer_rows"]
    group = ts // V7X_SUBLANES
    nseq = seq // ts
    ntile = bsz * nseq
    row1 = (1, d)
    in_specs = [
        pl.BlockSpec(memory_space=pl.ANY),
        _const_spec(row1),
        pl.BlockSpec(memory_space=pl.ANY),
        _const_spec(conv_w.shape), _const_spec(row1),
        _const_spec(wa_blk.shape), _const_spec(row1),
        _const_spec(wi_blk.shape), _const_spec(row1),
        _const_spec(row1),
        _const_spec(row1), _const_spec(row1),
        _const_spec(ws.shape), _const_spec(bs_tile.shape),
        pl.BlockSpec(memory_space=pl.ANY), _const_spec(row1),
    ]
    out_shape = [
        jax.ShapeDtypeStruct((bsz, seq, d), F32),
        jax.ShapeDtypeStruct((bsz, seq, d // 2), U32),
    ]
    out_specs = [
        pl.BlockSpec(memory_space=pl.ANY),
        pl.BlockSpec(memory_space=pl.ANY),
    ]
    scratch = [
        pltpu.VMEM((3, group, V7X_SUBLANES, d), F32),
        pltpu.VMEM((ts, w_in.shape[1]), F32),
        pltpu.VMEM((ts, w_in.shape[1]), F32),
        pltpu.VMEM((2, group, V7X_SUBLANES, d), F32),
        pltpu.VMEM((2, group, V7X_SUBLANES, d // 2), U32),
        pltpu.SemaphoreType.DMA((3,)),
        pltpu.SemaphoreType.DMA((2,)),
        pltpu.SemaphoreType.DMA((2,)),
        pltpu.VMEM((SGU_GROUPS, ts, ts), BF16),
        pltpu.VMEM((CONV_WIDTH - 1, V7X_SUBLANES, d), F32),
        pltpu.VMEM((1, d), F32),
        pltpu.VMEM(w_in.shape, BF16),
        pltpu.VMEM(w_out.shape, BF16),
        pltpu.SemaphoreType.DMA((2,)),
    ]
    return pl.pallas_call(
        functools.partial(_mixer_kernel, nseq=nseq),
        grid=(ntile + 1,),
        in_specs=in_specs,
        out_specs=out_specs,
        out_shape=out_shape,
        scratch_shapes=scratch,
        compiler_params=pltpu.CompilerParams(
            dimension_semantics=("arbitrary",),
            vmem_limit_bytes=cfg["mixer_vmem"]),
        name="mixer",
    )(x, mix_norm, w_in, conv_w, conv_b, wa_blk, ba, wi_blk, bi, lam, ln_g, ln_b, ws, bs_tile,
      w_out, ffn_norm)


def _router_kernel(hp_ref, wg_ref, we_ref, br_ref, pos_ref, gate_ref, cnt_ref, ccar_ref, wr_ref,
                   *, expert_capacity):
    rows = hp_ref.shape[0]

    @pl.when(pl.program_id(0) == 0)
    def _():
        ccar_ref[...] = jnp.zeros_like(ccar_ref)
        wr_ref[...] = jnp.zeros_like(wr_ref)
        wr_ref[:, 0:N_GROUPS] = wg_ref[...].astype(BF16)
        wr_ref[:, EXPERT_ROW0:EXPERT_ROW0 + N_EXPERTS] = we_ref[...].astype(BF16)

    sub_rows = rows // ROUTER_SUBBLOCKS
    lts = []
    for q in range(ROUTER_SUBBLOCKS):
        lo, hi = _unpack_bf16_pair(hp_ref[q * sub_rows:(q + 1) * sub_rows, :])
        hn = jnp.concatenate([lo, hi], axis=1)
        logits = _dot(hn.astype(BF16), wr_ref[...])
        lts.append(jnp.transpose(logits) + br_ref[...])
    sub = lax.broadcasted_iota(jnp.int32, (V7X_SUBLANES, sub_rows), 0)
    subf = sub.astype(F32)
    big = jnp.float32(1e9)
    eid = lax.broadcasted_iota(jnp.int32, (N_EXPERTS, sub_rows), 0).astype(F32)
    sb = V7X_MXU_DIM
    before = (lax.broadcasted_iota(jnp.int32, (sb, sb), 0)
              < lax.broadcasted_iota(jnp.int32, (sb, sb), 1))
    before = jnp.where(before, 1.0, 0.0).astype(BF16)
    cap = float(expert_capacity)
    zero = jnp.zeros((V7X_SUBLANES - TOP_K, sub_rows), F32)
    running = ccar_ref[:, 0:1]
    for q, lt in enumerate(lts):
        lg = jnp.where(sub < N_GROUPS, lt[0:V7X_SUBLANES, :], -jnp.inf)
        g_exp = jnp.exp(lg - jnp.max(lg, axis=0, keepdims=True))
        g_prob = g_exp / jnp.sum(g_exp, axis=0, keepdims=True)
        g_top = jnp.max(g_prob, axis=0, keepdims=True)
        g_idx = jnp.min(jnp.where(g_prob == g_top, subf, big), axis=0, keepdims=True)

        e_sel = jnp.zeros((EXPERTS_PER_GROUP, sub_rows), F32)
        for g in range(N_GROUPS):
            r0 = EXPERT_ROW0 + g * EXPERTS_PER_GROUP
            e_sel = jnp.where(g_idx == g, lt[r0:r0 + EXPERTS_PER_GROUP, :], e_sel)
        e_exp = jnp.exp(e_sel - jnp.max(e_sel, axis=0, keepdims=True))
        e_prob = e_exp / jnp.sum(e_exp, axis=0, keepdims=True)
        p1 = jnp.max(e_prob, axis=0, keepdims=True)
        i1 = jnp.min(jnp.where(e_prob == p1, subf, big), axis=0, keepdims=True)
        rest = jnp.where(subf == i1, -1.0, e_prob)
        p2 = jnp.max(rest, axis=0, keepdims=True)
        i2 = jnp.min(jnp.where(rest == p2, subf, big), axis=0, keepdims=True)
        psum = p1 + p2
        gate1 = g_top * (p1 / psum)
        gate2 = g_top * (p2 / psum)
        gid1 = g_idx * EXPERTS_PER_GROUP + i1
        gid2 = g_idx * EXPERTS_PER_GROUP + i2

        hit1 = eid == gid1
        hit2 = eid == gid2
        cnt = jnp.where(hit1 | hit2, 1.0, 0.0)
        base = []
        for c in range(sub_rows // sb):
            part = cnt[:, c * sb:(c + 1) * sb]
            base.append(running + _dot(part.astype(BF16), before))
            running = running + jnp.sum(part, axis=1, keepdims=True)
        base = jnp.concatenate(base, axis=1)
        rank1 = jnp.sum(jnp.where(hit1, base, 0.0), axis=0, keepdims=True)
        rank2 = jnp.sum(jnp.where(hit2, base, 0.0), axis=0, keepdims=True)
        pos = jnp.concatenate([gid1 * cap + rank1, gid2 * cap + rank2, zero], axis=0)
        pos_ref[:, q * sub_rows:(q + 1) * sub_rows] = pos.astype(jnp.int32)
        gate_ref[q * sub_rows:(q + 1) * sub_rows, :] = jnp.transpose(
            jnp.concatenate([gate1, gate2, zero], axis=0))
    total = jnp.broadcast_to(running, ccar_ref.shape)
    ccar_ref[...] = total
    cnt_ref[...] = total


def _router_call(hp, w_group, w_expert, b_router):
    cfg = _tiles()
    ntok, half = hp.shape
    tr = cfg["router_rows"]
    return pl.pallas_call(
        functools.partial(_router_kernel, expert_capacity=ntok),
        grid=(ntok // tr,),
        in_specs=[
            pl.BlockSpec((tr, half), lambda i: (i, 0)),
            _const_spec(w_group.shape),
            _const_spec(w_expert.shape),
            _const_spec(b_router.shape),
        ],
        out_specs=[
            pl.BlockSpec((V7X_SUBLANES, tr), lambda i: (0, i)),
            pl.BlockSpec((tr, V7X_SUBLANES), lambda i: (i, 0)),
            pl.BlockSpec((N_EXPERTS, V7X_LANES), lambda i: (0, 0)),
        ],
        out_shape=[
            jax.ShapeDtypeStruct((V7X_SUBLANES, ntok), jnp.int32),
            jax.ShapeDtypeStruct((ntok, V7X_SUBLANES), F32),
            jax.ShapeDtypeStruct((N_EXPERTS, V7X_LANES), F32),
        ],
        scratch_shapes=[
            pltpu.VMEM((N_EXPERTS, V7X_LANES), F32),
            pltpu.VMEM((w_group.shape[0], ROUTER_ROWS), BF16),
        ],
        compiler_params=pltpu.CompilerParams(
            dimension_semantics=("arbitrary",),
            vmem_limit_bytes=cfg["router_vmem"]),
        name="router",
    )(hp, w_group, w_expert, b_router)


def _expert_kernel(nt_ref, base_ref, texp_ref, tloc_ref, hs_hbm, w1_ref, w3_ref, w2_ref, ys_hbm,
                   hbuf, ybuf, hsem, ysem, w1b_ref, w3b_ref, w2b_ref, *, capacity):
    e = pl.program_id(0)
    n_exp = pl.num_programs(0)
    nt = nt_ref[e]
    base = base_ref[e]
    total = base_ref[n_exp - 1] + nt_ref[n_exp - 1]
    n_in, tm, _ = hbuf.shape
    n_out = ybuf.shape[0]
    ahead = n_in - 1

    def load(g):
        slot = lax.rem(g, n_in)
        rows = pl.ds(texp_ref[g] * capacity + tloc_ref[g] * tm, tm)
        return pltpu.make_async_copy(hs_hbm.at[rows], hbuf.at[slot], hsem.at[slot])

    def store(t, slot):
        rows = pl.ds(e * capacity + t * tm, tm)
        return pltpu.make_async_copy(ybuf.at[slot], ys_hbm.at[rows], ysem.at[slot])

    @pl.when(e == 0)
    def _():
        for g0 in range(ahead):
            @pl.when(g0 < total)
            def _():
                load(g0).start()

    w1b_ref[...] = w1_ref[...].astype(BF16)
    w3b_ref[...] = w3_ref[...].astype(BF16)
    w2b_ref[...] = w2_ref[...].astype(BF16)

    @pl.loop(0, nt)
    def _(t):
        g = base + t

        @pl.when(g + ahead < total)
        def _():
            load(g + ahead).start()

        load(g).wait()
        slot = lax.rem(g, n_out)

        @pl.when(g >= n_out)
        def _():
            store(t, slot).wait()

        sub_rows = tm // EXPERT_SUBBLOCKS
        blocks = [pl.ds(q * sub_rows, sub_rows) for q in range(EXPERT_SUBBLOCKS)]
        rows_in = []
        for rs in blocks:
            lo, hi = _unpack_bf16_pair(hbuf[lax.rem(g, n_in), rs, :])
            rows_in.append(jnp.concatenate([lo, hi], axis=1).astype(BF16))
        up = [(_dot(h, w1b_ref[...]), _dot(h, w3b_ref[...])) for h in rows_in]
        down = []
        for a, b in up:
            hid = (a * _sigmoid(a)) * b
            down.append(_dot(hid.astype(BF16), w2b_ref[...]))
        for rs, y in zip(blocks, down):
            half = y.shape[1] // 2
            ybuf[slot, rs, :] = _pack_bf16_pair(y[:, :half], y[:, half:])
        store(t, slot).start()

    @pl.when(e + 1 == n_exp)
    def _():
        for back in range(1, n_out + 1):
            @pl.when(total >= back)
            def _():
                store(0, lax.rem(total - back, n_out)).wait()


def _expert_call(tiles_per_expert, hs, w1, w3, w2, capacity):
    cfg = _tiles()
    tm = cfg["expert_rows"]
    prow, half = hs.shape
    n_exp, d, f = w1.shape
    ends = jnp.cumsum(tiles_per_expert)
    base = ends - tiles_per_expert
    g = jnp.arange(capacity * TOP_K // tm + n_exp, dtype=jnp.int32)
    texp = jnp.minimum(jnp.sum((ends[None, :] <= g[:, None]).astype(jnp.int32), axis=1), n_exp - 1)
    onehot = texp[:, None] == jnp.arange(n_exp, dtype=jnp.int32)[None, :]
    tloc = g - jnp.sum(jnp.where(onehot, base[None, :], 0), axis=1)

    def w_map(e, *_):
        return (e, 0, 0)

    grid_spec = pltpu.PrefetchScalarGridSpec(
        num_scalar_prefetch=4,
        grid=(n_exp,),
        in_specs=[
            pl.BlockSpec(memory_space=pl.ANY),
            pl.BlockSpec((None, d, f), w_map),
            pl.BlockSpec((None, d, f), w_map),
            pl.BlockSpec((None, f, d), w_map),
        ],
        out_specs=pl.BlockSpec(memory_space=pl.ANY),
        scratch_shapes=[
            pltpu.VMEM((EXPERT_LOOKAHEAD + 1, tm, half), U32),
            pltpu.VMEM((2, tm, half), U32),
            pltpu.SemaphoreType.DMA((EXPERT_LOOKAHEAD + 1,)),
            pltpu.SemaphoreType.DMA((2,)),
            pltpu.VMEM((d, f), BF16),
            pltpu.VMEM((d, f), BF16),
            pltpu.VMEM((f, d), BF16),
        ],
    )
    return pl.pallas_call(
        functools.partial(_expert_kernel, capacity=capacity),
        grid_spec=grid_spec,
        out_shape=jax.ShapeDtypeStruct((prow, half), U32),
        compiler_params=pltpu.CompilerParams(
            dimension_semantics=("arbitrary",),
            vmem_limit_bytes=cfg["expert_vmem"]),
        name="experts",
    )(tiles_per_expert, base, texp, tloc, hs, w1, w3, w2)


def _sc_mesh():
    return plsc.VectorSubcoreMesh(core_axis_name="c", subcore_axis_name="s",
                                  num_cores=V7X_SC_CORES, num_subcores=V7X_SC_SUBCORES)


def _sc_worker_id():
    return lax.axis_index("s") * V7X_SC_CORES + lax.axis_index("c")


def _dispatch_call(hp, pos_w, out_rows):
    cfg = _tiles()
    ntok, half = hp.shape
    nw, topk, nch, ch = pos_w.shape
    per_w = nch * ch

    def body(hp_hbm, pos_hbm, hs_hbm, idx_v, buf, wsem):
        wid = _sc_worker_id()
        pltpu.sync_copy(pos_hbm.at[wid], idx_v)

        for c in range(nch):
            pltpu.sync_copy(hp_hbm.at[pl.ds(wid * per_w + c * ch, ch)], buf)
            writes = [pltpu.make_async_copy(buf, hs_hbm.at[idx_v.at[k, c]], wsem.at[k]) for k in range(topk)]
            for w in writes:
                w.start()
            for w in writes:
                w.wait()

    assert nw == V7X_SC_CORES * V7X_SC_SUBCORES and nw * per_w == ntok and ch == cfg["sc_rows"]
    return pl.kernel(
        body,
        out_type=jax.ShapeDtypeStruct((out_rows, half), U32),
        mesh=_sc_mesh(),
        scratch_types=[
            pltpu.VMEM((topk, nch, ch), jnp.int32),
            pltpu.VMEM((ch, half), U32),
            pltpu.SemaphoreType.DMA((topk,)),
        ],
        name="dispatch",
    )(hp, pos_w)


def _combine_call(ys, pos_w):
    cfg = _tiles()
    _, half = ys.shape
    nw, topk, nch, ch = pos_w.shape
    per_w = nch * ch
    ntok = nw * per_w

    def body(ys_hbm, pos_hbm, *rest):
        outs = rest[:topk]
        idx_v, buf = rest[topk:]
        wid = _sc_worker_id()
        pltpu.sync_copy(pos_hbm.at[wid], idx_v)
        for c in range(nch):
            for k in range(topk):
                pltpu.sync_copy(ys_hbm.at[idx_v.at[k, c]], buf)
                pltpu.sync_copy(buf, outs[k].at[pl.ds(wid * per_w + c * ch, ch)])

    assert nw == V7X_SC_CORES * V7X_SC_SUBCORES and ch == cfg["sc_rows"]
    return pl.kernel(
        body,
        out_type=[jax.ShapeDtypeStruct((ntok, half), U32)] * topk,
        mesh=_sc_mesh(),
        scratch_types=[
            pltpu.VMEM((topk, nch, ch), jnp.int32),
            pltpu.VMEM((ch, half), U32),
        ],
        name="combine",
    )(ys, pos_w)


def _ple_kernel(x1_ref, yg0_ref, yg1_ref, gate_ref, p_ref, plen_ref, wg32_ref, wu32_ref, fin_ref, o_ref,
                wg_ref, wu_ref):
    @pl.when(pl.program_id(0) == 0)
    def _():
        wg_ref[...] = (0.5 * wg32_ref[...]).astype(BF16)
        wu_ref[...] = (0.5 * wu32_ref[...]).astype(BF16)

    rows = x1_ref.shape[0]
    sub_rows = rows // PLE_SUBBLOCKS
    for q in range(PLE_SUBBLOCKS):
        rs = pl.ds(q * sub_rows, sub_rows)
        lo0, hi0 = _unpack_bf16_pair(yg0_ref[rs, :])
        lo1, hi1 = _unpack_bf16_pair(yg1_ref[rs, :])
        g0 = gate_ref[rs, 0:1]
        g1 = gate_ref[rs, 1:2]
        moe = g0 * jnp.concatenate([lo0, hi0], axis=1) + g1 * jnp.concatenate([lo1, hi1], axis=1)
        x2 = x1_ref[rs, :] + moe
        r = _rmsnorm(x2, plen_ref[...]).astype(BF16)
        gt2 = 1.0 + jnp.tanh(_dot(r, wg_ref[...]))
        up_h = _dot(p_ref[rs, :].astype(BF16), wu_ref[...])
        x3 = x2 + gt2 * up_h
        o_ref[rs, :] = _rmsnorm(x3, fin_ref[...])


def _ple_call(x1, yg0, yg1, gates, p, ple_norm, wg, wu, final_norm):
    cfg = _tiles()
    ntok, d = x1.shape
    tp = cfg["ple_rows"]
    pdim = p.shape[1]
    return pl.pallas_call(
        _ple_kernel,
        grid=(ntok // tp,),
        in_specs=[
            pl.BlockSpec((tp, d), lambda i: (i, 0)),
            pl.BlockSpec((tp, d // 2), lambda i: (i, 0)),
            pl.BlockSpec((tp, d // 2), lambda i: (i, 0)),
            pl.BlockSpec((tp, V7X_SUBLANES), lambda i: (i, 0)),
            pl.BlockSpec((tp, pdim), lambda i: (i, 0)),
            _const_spec((1, d)),
            _const_spec(wg.shape),
            _const_spec(wu.shape),
            _const_spec((1, d)),
        ],
        out_specs=pl.BlockSpec((tp, d), lambda i: (i, 0)),
        out_shape=jax.ShapeDtypeStruct((ntok, d), F32),
        scratch_shapes=[pltpu.VMEM(wg.shape, BF16), pltpu.VMEM(wu.shape, BF16)],
        compiler_params=pltpu.CompilerParams(
            dimension_semantics=("arbitrary",),
            vmem_limit_bytes=cfg["ple_vmem"]),
        name="ple",
    )(x1, yg0, yg1, gates, p, ple_norm, wg, wu, final_norm)


def _blockdiag_pack(w):
    nb, bd, _ = w.shape
    per = V7X_MXU_DIM // bd
    w4 = w.reshape(nb // per, per, bd, bd)
    eye = jnp.eye(per, dtype=w.dtype)
    out = jnp.einsum("jpab,pq->jpaqb", w4, eye)
    return out.reshape(nb // per, V7X_MXU_DIM, V7X_MXU_DIM).astype(BF16)


def kernel(x, p, mix_norm, w_in, conv_w, conv_b, lru_wa, lru_ba, lru_wi, lru_bi, lru_lambda, sgu_ln_g, sgu_ln_b, sgu_ws, sgu_bs, w_out, ffn_norm, router_group_w, router_group_b, router_expert_w, router_expert_b, expert_w1, expert_w3, expert_w2, ple_norm, ple_gate_w, ple_up_w, final_norm):
    cfg = _tiles()
    bsz, seq, d = x.shape
    ntok = bsz * seq
    tm = cfg["expert_rows"]
    depth = w_in.shape[0]
    assert depth == 1, "the ple kernel applies the final norm, so it must be the last layer"
    l = 0
    nw_rows = V7X_SC_CORES * V7X_SC_SUBCORES * cfg["sc_rows"]
    assert cfg["mixer_rows"] % CHUNK == 0 and seq % cfg["mixer_rows"] == 0
    assert ntok % cfg["router_rows"] == 0 and ntok % cfg["ple_rows"] == 0 and ntok % nw_rows == 0
    assert ntok % tm == 0 and lru_wa.shape[1:] == (LRU_BLOCKS, d // LRU_BLOCKS, d // LRU_BLOCKS)
    assert max(cfg[k] for k in cfg if k.endswith("_vmem")) < V7X_VMEM_BYTES
    b_router = jnp.concatenate([
        router_group_b[l], jnp.zeros((EXPERT_ROW0 - N_GROUPS,), F32), router_expert_b[l],
        jnp.zeros((ROUTER_ROWS - EXPERT_ROW0 - N_EXPERTS,), F32)])[:, None]
    ts = cfg["mixer_rows"]
    group = ts // V7X_SUBLANES
    bs_tile = jnp.tile(sgu_bs[l], (1, ts // CHUNK)).reshape(SGU_GROUPS, V7X_SUBLANES, group)
    bs_tile = jnp.transpose(bs_tile, (2, 1, 0)).reshape(ts, SGU_GROUPS)
    x1, hp = _mixer_call(
        x, mix_norm[l][None], w_in[l], conv_w[l], conv_b[l][None],
        _blockdiag_pack(lru_wa[l]), lru_ba[l][None], _blockdiag_pack(lru_wi[l]), lru_bi[l][None],
        lru_lambda[l][None], sgu_ln_g[l][None], sgu_ln_b[l][None], sgu_ws[l], bs_tile,
        w_out[l], ffn_norm[l][None])
    hp = hp.reshape(ntok, d // 2)
    pos, gate, cnt = _router_call(hp, router_group_w[l], router_expert_w[l], b_router)

    cap = ntok
    tiles_per_expert = (cnt[:, 0].astype(jnp.int32) + tm - 1) // tm
    nw = V7X_SC_CORES * V7X_SC_SUBCORES
    ch = cfg["sc_rows"]
    pos_w = jnp.transpose(pos[:TOP_K].reshape(TOP_K, nw, ntok // (nw * ch), ch), (1, 0, 2, 3))

    hs = _dispatch_call(hp, pos_w, N_EXPERTS * cap)
    ys = _expert_call(tiles_per_expert, hs, expert_w1[l], expert_w3[l], expert_w2[l], cap)
    yg0, yg1 = _combine_call(ys, pos_w)

    out = _ple_call(x1.reshape(ntok, d), yg0, yg1, gate, p[l].reshape(ntok, -1), ple_norm[l][None],
                    ple_gate_w[l], ple_up_w[l], final_norm[None])
    return out.reshape(bsz, seq, d)
```

```python
import functools

import jax
import jax.numpy as jnp
from jax import lax
from jax.experimental import pallas as pl
from jax.experimental.pallas import tpu as pltpu
from jax.experimental.pallas import tpu_sc as plsc

F32 = jnp.float32
BF16 = jnp.bfloat16
U32 = jnp.uint32

LRU_BLOCKS = 16
CONV_WIDTH = 4
LRU_C = 8.0
SGU_GROUPS = 8
CHUNK = 128
N_GROUPS = 4
EXPERTS_PER_GROUP = 8
N_EXPERTS = N_GROUPS * EXPERTS_PER_GROUP
TOP_K = 2
EPS = 1e-6

V7X_MXU_DIM = 256
V7X_SUBLANES = 8
V7X_LANES = 128
V7X_VMEM_BYTES = 64 * 1024 * 1024
V7X_SC_CORES = 2
V7X_SC_SUBCORES = 16

EXPERT_LOOKAHEAD = 3
ROUTER_SUBBLOCKS = 8
EXPERT_SUBBLOCKS = 2
PLE_SUBBLOCKS = 4
ROUTER_ROWS = V7X_LANES
EXPERT_ROW0 = V7X_SUBLANES


def _tiles():
    return dict(
        mixer_rows=256,
        expert_rows=512,
        ple_rows=1024,
        sc_rows=128,
        router_rows=4096,
        mixer_vmem=52 * 1024 * 1024,
        expert_vmem=40 * 1024 * 1024,
        ple_vmem=48 * 1024 * 1024,
        router_vmem=40 * 1024 * 1024,
    )


def _dot(a, b):
    return jnp.dot(a, b, preferred_element_type=F32)


def _sigmoid(x):
    return 0.5 * jnp.tanh(0.5 * x) + 0.5


def _rmsnorm(x, g):
    ms = jnp.mean(x * x, axis=-1, keepdims=True)
    return x * lax.rsqrt(ms + EPS) * g


def _pack_bf16_pair(lo, hi):
    lo_b = lax.bitcast_convert_type(lo.astype(BF16).astype(F32), U32)
    hi_b = lax.bitcast_convert_type(hi.astype(BF16).astype(F32), U32)
    return (hi_b & jnp.uint32(0xFFFF0000)) | lax.shift_right_logical(lo_b, jnp.uint32(16))


def _unpack_bf16_pair(w):
    lo = lax.bitcast_convert_type(lax.shift_left(w, jnp.uint32(16)), F32)
    hi = lax.bitcast_convert_type(w & jnp.uint32(0xFFFF0000), F32)
    return lo, hi


def _const_spec(shape):
    zeros = (0,) * len(shape)
    return pl.BlockSpec(shape, lambda *_: zeros, pipeline_mode=pl.Buffered(1))


def _tile_copies(hbm, buf, sem, b, row0, slot, to_hbm):
    group = buf.shape[1]
    copies = []
    for r in range(V7X_SUBLANES):
        hbm_rows = hbm.at[b, pl.ds(row0 + group * r, group), :]
        vmem_rows = buf.at[slot, :, r, :]
        src, dst = (vmem_rows, hbm_rows) if to_hbm else (hbm_rows, vmem_rows)
        copies.append(pltpu.make_async_copy(src, dst, sem.at[slot]))
    return copies


def _lru_scan(a, u, h0):
    group = a.shape[0]
    acc_a = [a[0]]
    acc_u = [u[0]]
    for g in range(1, group):
        acc_a.append(a[g] * acc_a[-1])
        acc_u.append(a[g] * acc_u[-1] + u[g])
    end_a, end_u = acc_a[-1], acc_u[-1]
    sub = lax.broadcasted_iota(jnp.int32, end_a.shape, 0)
    shift = 1
    while shift < V7X_SUBLANES:
        keep = sub >= shift
        a_sh = pltpu.roll(end_a, shift, axis=0)
        u_sh = pltpu.roll(end_u, shift, axis=0)
        end_u = jnp.where(keep, end_a * u_sh + end_u, end_u)
        end_a = jnp.where(keep, end_a * a_sh, end_a)
        shift *= 2
    h_end = end_a * h0 + end_u
    h_in = jnp.where(sub == 0, h0, pltpu.roll(h_end, 1, axis=0))
    out = [acc_a[g] * h_in + acc_u[g] for g in range(group)]
    return jnp.stack(out, axis=0), h_end[V7X_SUBLANES - 1:V7X_SUBLANES, :]


def _mixer_kernel(x_hbm, mixn_ref, win_hbm, convw_ref, convb_ref, wa_ref, ba_ref, wi_ref, bi_ref,
                  lam_ref, lng_ref, lnb_ref, ws_ref, bsp_ref, wout_hbm, ffn_ref,
                  x1_hbm, hp_hbm,
                  xbuf, z0_ref, z1_ref, x1buf, hpbuf, xsem, x1sem, hpsem, wsm_ref, ztail_ref, hcar_ref,
                  win_ref, wout_ref, wsem,
                  *, nseq):
    j = pl.program_id(0)
    ntile = pl.num_programs(0) - 1
    _, group, _, d = xbuf.shape
    rows = group * V7X_SUBLANES
    half = d // 2
    ta = jnp.minimum(j, ntile - 1)
    tb = jnp.maximum(j - 1, 0)
    s = lax.rem(tb, nseq)
    slot = lax.rem(tb, 2)

    def fetch(t):
        return _tile_copies(x_hbm, xbuf, xsem, lax.div(t, nseq), lax.rem(t, nseq) * rows,
                            lax.rem(t, 3), to_hbm=False)

    def put(t):
        tb_, ts_, sl = lax.div(t, nseq), lax.rem(t, nseq) * rows, lax.rem(t, 2)
        return (_tile_copies(x1_hbm, x1buf, x1sem, tb_, ts_, sl, to_hbm=True)
                + _tile_copies(hp_hbm, hpbuf, hpsem, tb_, ts_, sl, to_hbm=True))

    @pl.when(j == 0)
    def _():
        for c in fetch(0):
            c.start()
        stage = (z0_ref, z1_ref)
        n_in_chunks = win_hbm.shape[0] // rows

        def win_copy(c):
            return pltpu.make_async_copy(win_hbm.at[pl.ds(c * rows, rows), :], stage[c % 2], wsem.at[c % 2])

        win_copy(0).start()
        for c in range(n_in_chunks):
            if c + 1 < n_in_chunks:
                win_copy(c + 1).start()
            win_copy(c).wait()
            plain = 4 * d
            win_ref[c * rows:(c + 1) * rows, :plain] = stage[c % 2][:, :plain].astype(BF16)
            win_ref[c * rows:(c + 1) * rows, plain:] = (0.5 * stage[c % 2][:, plain:]).astype(BF16)
        n_out_chunks = wout_hbm.shape[0] // rows
        out_copies = [pltpu.make_async_copy(wout_hbm.at[pl.ds(c * rows, rows), :],
                                            z0_ref.at[:, c * d:(c + 1) * d], wsem.at[0])
                      for c in range(n_out_chunks)]
        for cp in out_copies:
            cp.start()
        for cp in out_copies:
            cp.wait()
        for c in range(n_out_chunks):
            wout_ref[c * rows:(c + 1) * rows, :] = (0.25 * z0_ref[:, c * d:(c + 1) * d]).astype(BF16)
        z1_ref[...] = jnp.zeros_like(z1_ref)
        i_idx = lax.broadcasted_iota(jnp.int32, (rows, rows), 0)
        j_idx = lax.broadcasted_iota(jnp.int32, (rows, rows), 1)
        t_i = group * lax.rem(i_idx, V7X_SUBLANES) + lax.div(i_idx, V7X_SUBLANES)
        t_j = group * lax.rem(j_idx, V7X_SUBLANES) + lax.div(j_idx, V7X_SUBLANES)
        keep = (t_i >= t_j) & (lax.div(t_i, CHUNK) == lax.div(t_j, CHUNK))
        pick_rows = jnp.where(t_i == j_idx, 1.0, 0.0).astype(BF16)
        pick_cols = jnp.where(i_idx == t_j, 1.0, 0.0).astype(BF16)
        reps = rows // CHUNK
        for g in range(SGU_GROUPS):
            w_chunk = ws_ref[g].astype(BF16)
            w_rows = jnp.concatenate([w_chunk] * reps, axis=1)
            w_full = jnp.concatenate([w_rows] * reps, axis=0)
            w_perm = _dot(_dot(pick_rows, w_full).astype(BF16), pick_cols)
            wsm_ref[g] = jnp.where(keep, w_perm, 0.0).astype(BF16)

    @pl.when(j + 1 < ntile)
    def _():
        for c in fetch(j + 1):
            c.start()

    @pl.when(j < ntile)
    def _():
        for c in fetch(j):
            c.wait()

    @pl.when(s == 0)
    def _():
        ztail_ref[...] = jnp.zeros_like(ztail_ref)
        hcar_ref[...] = jnp.zeros_like(hcar_ref)

    def compute(z_w, z_r):
        xa_in = xbuf[lax.rem(ta, 3)].reshape(rows, d)
        h_next = _rmsnorm(xa_in, mixn_ref[...]).astype(BF16)
        pw = d // 2

        def project(k):
            z_w[:, k * pw:(k + 1) * pw] = _dot(h_next, win_ref[:, k * pw:(k + 1) * pw])

        x = xbuf[lax.rem(tb, 3)].reshape(rows, d)

        def sec(k, c0, c1):
            return z_r[:, k * d + c0:k * d + c1]

        def one_plus_tanh_gelu(v):
            c = 0.7978845608028654
            return 1.0 + jnp.tanh(v * (c + (c * 0.044715) * (v * v)))

        cw = 0.5 * convw_ref[...]
        cb_h = 0.5 * convb_ref[...]
        ba_h = 0.5 * ba_ref[...]
        bi_h = 0.5 * bi_ref[...]
        neg_lam = -lam_ref[...]
        softplus = jnp.maximum(neg_lam, 0.0) + jnp.log1p(jnp.exp(-jnp.abs(neg_lam)))
        c_a = (-0.5 * LRU_C) * softplus
        blk = V7X_MXU_DIM
        sub3 = lax.broadcasted_iota(jnp.int32, (CONV_WIDTH - 1, V7X_SUBLANES, blk), 1)
        term_a = []
        for n in range(d // blk):
            if n % 2 == 0:
                project(n // 2)
            c0, c1 = n * blk, (n + 1) * blk
            z3 = sec(0, c0, c1).reshape(group, V7X_SUBLANES, blk)
            tail = z3[group - (CONV_WIDTH - 1):]
            halo = jnp.where(sub3 == 0, pltpu.roll(ztail_ref[:, :, c0:c1], 1, axis=1),
                             pltpu.roll(tail, 1, axis=1))
            ztail_ref[:, :, c0:c1] = tail
            zext = jnp.concatenate([halo, z3], axis=0)
            xa_h = cb_h[:, c0:c1] + cw[CONV_WIDTH - 1:CONV_WIDTH, c0:c1] * z3
            for k in range(1, CONV_WIDTH):
                lo = CONV_WIDTH - 1 - k
                xa_h = xa_h + cw[lo:lo + 1, c0:c1] * zext[lo:lo + group]
            xa2 = xa_h.reshape(rows, blk)
            xa_bf = xa2.astype(BF16)
            th_r = jnp.tanh(_dot(xa_bf, wa_ref[n]) + ba_h[:, c0:c1])
            th_i = jnp.tanh(_dot(xa_bf, wi_ref[n]) + bi_h[:, c0:c1])
            a = jnp.exp(c_a[:, c0:c1] + c_a[:, c0:c1] * th_r)
            u = jnp.sqrt(1.0 - a * a) * ((1.0 + th_i) * xa2)
            hseq, hlast = _lru_scan(a.reshape(group, V7X_SUBLANES, blk),
                                    u.reshape(group, V7X_SUBLANES, blk), hcar_ref[:, c0:c1])
            hcar_ref[:, c0:c1] = hlast
            zg = sec(1, c0, c1)
            term_a.append(((1.0 + jnp.tanh(sec(4, c0, c1))) * one_plus_tanh_gelu(zg))
                          * (zg * hseq.reshape(rows, blk)))

        project(2)
        zv = sec(3, 0, d)
        gv2 = zv * one_plus_tanh_gelu(zv)
        project(3)
        mu = jnp.mean(gv2, axis=-1, keepdims=True)
        xc = gv2 - mu
        var = jnp.mean(xc * xc, axis=-1, keepdims=True)
        v_bf = (xc * lax.rsqrt(var + 4.0 * EPS) * lng_ref[...] + lnb_ref[...]).astype(BF16)
        project(4)
        gdim = d // SGU_GROUPS
        term_b = []
        for g in range(SGU_GROUPS):
            c0, c1 = g * gdim, (g + 1) * gdim
            if g < 7:
                project(5 + g)
            sp = _dot(wsm_ref[g], v_bf[:, c0:c1]) + bsp_ref[:, g:g + 1]
            zu = sec(2, c0, c1)
            term_b.append(((1.0 + jnp.tanh(sec(5, c0, c1))) * one_plus_tanh_gelu(zu)) * (zu * sp))
        merged4 = jnp.concatenate(term_a, axis=1) + jnp.concatenate(term_b, axis=1)

        x1 = x + _dot(merged4.astype(BF16), wout_ref[...])

        hn = _rmsnorm(x1, ffn_ref[...])
        hp = _pack_bf16_pair(hn[:, :half], hn[:, half:])

        @pl.when(j >= 3)
        def _():
            for c in put(tb - 2):
                c.wait()

        x1buf[slot] = x1.reshape(group, V7X_SUBLANES, d)
        hpbuf[slot] = hp.reshape(group, V7X_SUBLANES, half)

        @pl.when(j >= 1)
        def _():
            for c in put(tb):
                c.start()

    @pl.when(lax.rem(j, 2) == 0)
    def _():
        compute(z0_ref, z1_ref)

    @pl.when(lax.rem(j, 2) == 1)
    def _():
        compute(z1_ref, z0_ref)

    @pl.when(j == ntile)
    def _():
        for c in put(tb):
            c.wait()

        @pl.when(ntile >= 2)
        def _():
            for c in put(tb - 1):
                c.wait()


def _mixer_call(x, mix_norm, w_in, conv_w, conv_b, wa_blk, ba, wi_blk, bi, lam, ln_g, ln_b, ws,
                bs_tile, w_out, ffn_norm):
    cfg = _tiles()
    bsz, seq, d = x.shape
    ts = cfg["mixer_rows"]
    group = ts // V7X_SUBLANES
    nseq = seq // ts
    ntile = bsz * nseq
    row1 = (1, d)
    in_specs = [
        pl.BlockSpec(memory_space=pl.ANY),
        _const_spec(row1),
        pl.BlockSpec(memory_space=pl.ANY),
        _const_spec(conv_w.shape), _const_spec(row1),
        _const_spec(wa_blk.shape), _const_spec(row1),
        _const_spec(wi_blk.shape), _const_spec(row1),
        _const_spec(row1),
        _const_spec(row1), _const_spec(row1),
        _const_spec(ws.shape), _const_spec(bs_tile.shape),
        pl.BlockSpec(memory_space=pl.ANY), _const_spec(row1),
    ]
    out_shape = [
        jax.ShapeDtypeStruct((bsz, seq, d), F32),
        jax.ShapeDtypeStruct((bsz, seq, d // 2), U32),
    ]
    out_specs = [
        pl.BlockSpec(memory_space=pl.ANY),
        pl.BlockSpec(memory_space=pl.ANY),
    ]
    scratch = [
        pltpu.VMEM((3, group, V7X_SUBLANES, d), F32),
        pltpu.VMEM((ts, w_in.shape[1]), F32),
        pltpu.VMEM((ts, w_in.shape[1]), F32),
        pltpu.VMEM((2, group, V7X_SUBLANES, d), F32),
        pltpu.VMEM((2, group, V7X_SUBLANES, d // 2), U32),
        pltpu.SemaphoreType.DMA((3,)),
        pltpu.SemaphoreType.DMA((2,)),
        pltpu.SemaphoreType.DMA((2,)),
        pltpu.VMEM((SGU_GROUPS, ts, ts), BF16),
        pltpu.VMEM((CONV_WIDTH - 1, V7X_SUBLANES, d), F32),
        pltpu.VMEM((1, d), F32),
        pltpu.VMEM(w_in.shape, BF16),
        pltpu.VMEM(w_out.shape, BF16),
        pltpu.SemaphoreType.DMA((2,)),
    ]
    return pl.pallas_call(
        functools.partial(_mixer_kernel, nseq=nseq),
        grid=(ntile + 1,),
        in_specs=in_specs,
        out_specs=out_specs,
        out_shape=out_shape,
        scratch_shapes=scratch,
        compiler_params=pltpu.CompilerParams(
            dimension_semantics=("arbitrary",),
            vmem_limit_bytes=cfg["mixer_vmem"]),
        name="mixer",
    )(x, mix_norm, w_in, conv_w, conv_b, wa_blk, ba, wi_blk, bi, lam, ln_g, ln_b, ws, bs_tile,
      w_out, ffn_norm)


def _router_kernel(hp_ref, wg_ref, we_ref, br_ref, pos_ref, gate_ref, cnt_ref, ccar_ref, wr_ref,
                   *, expert_capacity):
    rows = hp_ref.shape[0]

    @pl.when(pl.program_id(0) == 0)
    def _():
        ccar_ref[...] = jnp.zeros_like(ccar_ref)
        wr_ref[...] = jnp.zeros_like(wr_ref)
        wr_ref[:, 0:N_GROUPS] = wg_ref[...].astype(BF16)
        wr_ref[:, EXPERT_ROW0:EXPERT_ROW0 + N_EXPERTS] = we_ref[...].astype(BF16)

    sub_rows = rows // ROUTER_SUBBLOCKS
    lts = []
    for q in range(ROUTER_SUBBLOCKS):
        lo, hi = _unpack_bf16_pair(hp_ref[q * sub_rows:(q + 1) * sub_rows, :])
        hn = jnp.concatenate([lo, hi], axis=1)
        logits = _dot(hn.astype(BF16), wr_ref[...])
        lts.append(jnp.transpose(logits) + br_ref[...])
    sub = lax.broadcasted_iota(jnp.int32, (V7X_SUBLANES, sub_rows), 0)
    subf = sub.astype(F32)
    big = jnp.float32(1e9)
    eid = lax.broadcasted_iota(jnp.int32, (N_EXPERTS, sub_rows), 0).astype(F32)
    sb = V7X_MXU_DIM
    before = (lax.broadcasted_iota(jnp.int32, (sb, sb), 0)
              < lax.broadcasted_iota(jnp.int32, (sb, sb), 1))
    before = jnp.where(before, 1.0, 0.0).astype(BF16)
    cap = float(expert_capacity)
    zero = jnp.zeros((V7X_SUBLANES - TOP_K, sub_rows), F32)
    running = ccar_ref[:, 0:1]
    for q, lt in enumerate(lts):
        lg = jnp.where(sub < N_GROUPS, lt[0:V7X_SUBLANES, :], -jnp.inf)
        g_exp = jnp.exp(lg - jnp.max(lg, axis=0, keepdims=True))
        g_prob = g_exp / jnp.sum(g_exp, axis=0, keepdims=True)
        g_top = jnp.max(g_prob, axis=0, keepdims=True)
        g_idx = jnp.min(jnp.where(g_prob == g_top, subf, big), axis=0, keepdims=True)

        e_sel = jnp.zeros((EXPERTS_PER_GROUP, sub_rows), F32)
        for g in range(N_GROUPS):
            r0 = EXPERT_ROW0 + g * EXPERTS_PER_GROUP
            e_sel = jnp.where(g_idx == g, lt[r0:r0 + EXPERTS_PER_GROUP, :], e_sel)
        e_exp = jnp.exp(e_sel - jnp.max(e_sel, axis=0, keepdims=True))
        e_prob = e_exp / jnp.sum(e_exp, axis=0, keepdims=True)
        p1 = jnp.max(e_prob, axis=0, keepdims=True)
        i1 = jnp.min(jnp.where(e_prob == p1, subf, big), axis=0, keepdims=True)
        rest = jnp.where(subf == i1, -1.0, e_prob)
        p2 = jnp.max(rest, axis=0, keepdims=True)
        i2 = jnp.min(jnp.where(rest == p2, subf, big), axis=0, keepdims=True)
        psum = p1 + p2
        gate1 = g_top * (p1 / psum)
        gate2 = g_top * (p2 / psum)
        gid1 = g_idx * EXPERTS_PER_GROUP + i1
        gid2 = g_idx * EXPERTS_PER_GROUP + i2

        hit1 = eid == gid1
        hit2 = eid == gid2
        cnt = jnp.where(hit1 | hit2, 1.0, 0.0)
        base = []
        for c in range(sub_rows // sb):
            part = cnt[:, c * sb:(c + 1) * sb]
            base.append(running + _dot(part.astype(BF16), before))
            running = running + jnp.sum(part, axis=1, keepdims=True)
        base = jnp.concatenate(base, axis=1)
        rank1 = jnp.sum(jnp.where(hit1, base, 0.0), axis=0, keepdims=True)
        rank2 = jnp.sum(jnp.where(hit2, base, 0.0), axis=0, keepdims=True)
        pos = jnp.concatenate([gid1 * cap + rank1, gid2 * cap + rank2, zero], axis=0)
        pos_ref[:, q * sub_rows:(q + 1) * sub_rows] = pos.astype(jnp.int32)
        gate_ref[q * sub_rows:(q + 1) * sub_rows, :] = jnp.transpose(
            jnp.concatenate([gate1, gate2, zero], axis=0))
    total = jnp.broadcast_to(running, ccar_ref.shape)
    ccar_ref[...] = total
    cnt_ref[...] = total


def _router_call(hp, w_group, w_expert, b_router):
    cfg = _tiles()
    ntok, half = hp.shape
    tr = cfg["router_rows"]
    return pl.pallas_call(
        functools.partial(_router_kernel, expert_capacity=ntok),
        grid=(ntok // tr,),
        in_specs=[
            pl.BlockSpec((tr, half), lambda i: (i, 0)),
            _const_spec(w_group.shape),
            _const_spec(w_expert.shape),
            _const_spec(b_router.shape),
        ],
        out_specs=[
            pl.BlockSpec((V7X_SUBLANES, tr), lambda i: (0, i)),
            pl.BlockSpec((tr, V7X_SUBLANES), lambda i: (i, 0)),
            pl.BlockSpec((N_EXPERTS, V7X_LANES), lambda i: (0, 0)),
        ],
        out_shape=[
            jax.ShapeDtypeStruct((V7X_SUBLANES, ntok), jnp.int32),
            jax.ShapeDtypeStruct((ntok, V7X_SUBLANES), F32),
            jax.ShapeDtypeStruct((N_EXPERTS, V7X_LANES), F32),
        ],
        scratch_shapes=[
            pltpu.VMEM((N_EXPERTS, V7X_LANES), F32),
            pltpu.VMEM((w_group.shape[0], ROUTER_ROWS), BF16),
        ],
        compiler_params=pltpu.CompilerParams(
            dimension_semantics=("arbitrary",),
            vmem_limit_bytes=cfg["router_vmem"]),
        name="router",
    )(hp, w_group, w_expert, b_router)


def _expert_kernel(nt_ref, base_ref, texp_ref, tloc_ref, hs_hbm, w1_ref, w3_ref, w2_ref, ys_hbm,
                   hbuf, ybuf, hsem, ysem, w1b_ref, w3b_ref, w2b_ref, *, capacity):
    e = pl.program_id(0)
    n_exp = pl.num_programs(0)
    nt = nt_ref[e]
    base = base_ref[e]
    total = base_ref[n_exp - 1] + nt_ref[n_exp - 1]
    n_in, tm, _ = hbuf.shape
    n_out = ybuf.shape[0]
    ahead = n_in - 1

    def load(g):
        slot = lax.rem(g, n_in)
        rows = pl.ds(texp_ref[g] * capacity + tloc_ref[g] * tm, tm)
        return pltpu.make_async_copy(hs_hbm.at[rows], hbuf.at[slot], hsem.at[slot])

    def store(t, slot):
        rows = pl.ds(e * capacity + t * tm, tm)
        return pltpu.make_async_copy(ybuf.at[slot], ys_hbm.at[rows], ysem.at[slot])

    @pl.when(e == 0)
    def _():
        for g0 in range(ahead):
            @pl.when(g0 < total)
            def _():
                load(g0).start()

    w1b_ref[...] = w1_ref[...].astype(BF16)
    w3b_ref[...] = w3_ref[...].astype(BF16)
    w2b_ref[...] = w2_ref[...].astype(BF16)

    @pl.loop(0, nt)
    def _(t):
        g = base + t

        @pl.when(g + ahead < total)
        def _():
            load(g + ahead).start()

        load(g).wait()
        slot = lax.rem(g, n_out)

        @pl.when(g >= n_out)
        def _():
            store(t, slot).wait()

        sub_rows = tm // EXPERT_SUBBLOCKS
        blocks = [pl.ds(q * sub_rows, sub_rows) for q in range(EXPERT_SUBBLOCKS)]
        rows_in = []
        for rs in blocks:
            lo, hi = _unpack_bf16_pair(hbuf[lax.rem(g, n_in), rs, :])
            rows_in.append(jnp.concatenate([lo, hi], axis=1).astype(BF16))
        up = [(_dot(h, w1b_ref[...]), _dot(h, w3b_ref[...])) for h in rows_in]
        down = []
        for a, b in up:
            hid = (a * _sigmoid(a)) * b
            down.append(_dot(hid.astype(BF16), w2b_ref[...]))
        for rs, y in zip(blocks, down):
            half = y.shape[1] // 2
            ybuf[slot, rs, :] = _pack_bf16_pair(y[:, :half], y[:, half:])
        store(t, slot).start()

    @pl.when(e + 1 == n_exp)
    def _():
        for back in range(1, n_out + 1):
            @pl.when(total >= back)
            def _():
                store(0, lax.rem(total - back, n_out)).wait()


def _expert_call(tiles_per_expert, hs, w1, w3, w2, capacity):
    cfg = _tiles()
    tm = cfg["expert_rows"]
    prow, half = hs.shape
    n_exp, d, f = w1.shape
    ends = jnp.cumsum(tiles_per_expert)
    base = ends - tiles_per_expert
    g = jnp.arange(capacity * TOP_K // tm + n_exp, dtype=jnp.int32)
    texp = jnp.minimum(jnp.sum((ends[None, :] <= g[:, None]).astype(jnp.int32), axis=1), n_exp - 1)
    onehot = texp[:, None] == jnp.arange(n_exp, dtype=jnp.int32)[None, :]
    tloc = g - jnp.sum(jnp.where(onehot, base[None, :], 0), axis=1)

    def w_map(e, *_):
        return (e, 0, 0)

    grid_spec = pltpu.PrefetchScalarGridSpec(
        num_scalar_prefetch=4,
        grid=(n_exp,),
        in_specs=[
            pl.BlockSpec(memory_space=pl.ANY),
            pl.BlockSpec((None, d, f), w_map),
            pl.BlockSpec((None, d, f), w_map),
            pl.BlockSpec((None, f, d), w_map),
        ],
        out_specs=pl.BlockSpec(memory_space=pl.ANY),
        scratch_shapes=[
            pltpu.VMEM((EXPERT_LOOKAHEAD + 1, tm, half), U32),
            pltpu.VMEM((2, tm, half), U32),
            pltpu.SemaphoreType.DMA((EXPERT_LOOKAHEAD + 1,)),
            pltpu.SemaphoreType.DMA((2,)),
            pltpu.VMEM((d, f), BF16),
            pltpu.VMEM((d, f), BF16),
            pltpu.VMEM((f, d), BF16),
        ],
    )
    return pl.pallas_call(
        functools.partial(_expert_kernel, capacity=capacity),
        grid_spec=grid_spec,
        out_shape=jax.ShapeDtypeStruct((prow, half), U32),
        compiler_params=pltpu.CompilerParams(
            dimension_semantics=("arbitrary",),
            vmem_limit_bytes=cfg["expert_vmem"]),
        name="experts",
    )(tiles_per_expert, base, texp, tloc, hs, w1, w3, w2)


def _sc_mesh():
    return plsc.VectorSubcoreMesh(core_axis_name="c", subcore_axis_name="s",
                                  num_cores=V7X_SC_CORES, num_subcores=V7X_SC_SUBCORES)


def _sc_worker_id():
    return lax.axis_index("s") * V7X_SC_CORES + lax.axis_index("c")


def _dispatch_call(hp, pos_w, out_rows):
    cfg = _tiles()
    ntok, half = hp.shape
    nw, topk, nch, ch = pos_w.shape
    per_w = nch * ch

    def body(hp_hbm, pos_hbm, hs_hbm, idx_v, buf, wsem):
        wid = _sc_worker_id()
        pltpu.sync_copy(pos_hbm.at[wid], idx_v)

        for c in range(nch):
            pltpu.sync_copy(hp_hbm.at[pl.ds(wid * per_w + c * ch, ch)], buf)
            writes = [pltpu.make_async_copy(buf, hs_hbm.at[idx_v.at[k, c]], wsem.at[k]) for k in range(topk)]
            for w in writes:
                w.start()
            for w in writes:
                w.wait()

    assert nw == V7X_SC_CORES * V7X_SC_SUBCORES and nw * per_w == ntok and ch == cfg["sc_rows"]
    return pl.kernel(
        body,
        out_type=jax.ShapeDtypeStruct((out_rows, half), U32),
        mesh=_sc_mesh(),
        scratch_types=[
            pltpu.VMEM((topk, nch, ch), jnp.int32),
            pltpu.VMEM((ch, half), U32),
            pltpu.SemaphoreType.DMA((topk,)),
        ],
        name="dispatch",
    )(hp, pos_w)


def _combine_call(ys, pos_w):
    cfg = _tiles()
    _, half = ys.shape
    nw, topk, nch, ch = pos_w.shape
    per_w = nch * ch
    ntok = nw * per_w

    def body(ys_hbm, pos_hbm, *rest):
        outs = rest[:topk]
        idx_v, buf = rest[topk:]
        wid = _sc_worker_id()
        pltpu.sync_copy(pos_hbm.at[wid], idx_v)
        for c in range(nch):
            for k in range(topk):
                pltpu.sync_copy(ys_hbm.at[idx_v.at[k, c]], buf)
                pltpu.sync_copy(buf, outs[k].at[pl.ds(wid * per_w + c * ch, ch)])

    assert nw == V7X_SC_CORES * V7X_SC_SUBCORES and ch == cfg["sc_rows"]
    return pl.kernel(
        body,
        out_type=[jax.ShapeDtypeStruct((ntok, half), U32)] * topk,
        mesh=_sc_mesh(),
        scratch_types=[
            pltpu.VMEM((topk, nch, ch), jnp.int32),
            pltpu.VMEM((ch, half), U32),
        ],
        name="combine",
    )(ys, pos_w)


def _ple_kernel(x1_ref, yg0_ref, yg1_ref, gate_ref, p_ref, plen_ref, wg32_ref, wu32_ref, fin_ref, o_ref,
                wg_ref, wu_ref):
    @pl.when(pl.program_id(0) == 0)
    def _():
        wg_ref[...] = (0.5 * wg32_ref[...]).astype(BF16)
        wu_ref[...] = (0.5 * wu32_ref[...]).astype(BF16)

    rows = x1_ref.shape[0]
    sub_rows = rows // PLE_SUBBLOCKS
    for q in range(PLE_SUBBLOCKS):
        rs = pl.ds(q * sub_rows, sub_rows)
        lo0, hi0 = _unpack_bf16_pair(yg0_ref[rs, :])
        lo1, hi1 = _unpack_bf16_pair(yg1_ref[rs, :])
        g0 = gate_ref[rs, 0:1]
        g1 = gate_ref[rs, 1:2]
        moe = g0 * jnp.concatenate([lo0, hi0], axis=1) + g1 * jnp.concatenate([lo1, hi1], axis=1)
        x2 = x1_ref[rs, :] + moe
        r = _rmsnorm(x2, plen_ref[...]).astype(BF16)
        gt2 = 1.0 + jnp.tanh(_dot(r, wg_ref[...]))
        up_h = _dot(p_ref[rs, :].astype(BF16), wu_ref[...])
        x3 = x2 + gt2 * up_h
        o_ref[rs, :] = _rmsnorm(x3, fin_ref[...])


def _ple_call(x1, yg0, yg1, gates, p, ple_norm, wg, wu, final_norm):
    cfg = _tiles()
    ntok, d = x1.shape
    tp = cfg["ple_rows"]
    pdim = p.shape[1]
    return pl.pallas_call(
        _ple_kernel,
        grid=(ntok // tp,),
        in_specs=[
            pl.BlockSpec((tp, d), lambda i: (i, 0)),
            pl.BlockSpec((tp, d // 2), lambda i: (i, 0)),
            pl.BlockSpec((tp, d // 2), lambda i: (i, 0)),
            pl.BlockSpec((tp, V7X_SUBLANES), lambda i: (i, 0)),
            pl.BlockSpec((tp, pdim), lambda i: (i, 0)),
            _const_spec((1, d)),
            _const_spec(wg.shape),
            _const_spec(wu.shape),
            _const_spec((1, d)),
        ],
        out_specs=pl.BlockSpec((tp, d), lambda i: (i, 0)),
        out_shape=jax.ShapeDtypeStruct((ntok, d), F32),
        scratch_shapes=[pltpu.VMEM(wg.shape, BF16), pltpu.VMEM(wu.shape, BF16)],
        compiler_params=pltpu.CompilerParams(
            dimension_semantics=("arbitrary",),
            vmem_limit_bytes=cfg["ple_vmem"]),
        name="ple",
    )(x1, yg0, yg1, gates, p, ple_norm, wg, wu, final_norm)


def _blockdiag_pack(w):
    nb, bd, _ = w.shape
    per = V7X_MXU_DIM // bd
    w4 = w.reshape(nb // per, per, bd, bd)
    eye = jnp.eye(per, dtype=w.dtype)
    out = jnp.einsum("jpab,pq->jpaqb", w4, eye)
    return out.reshape(nb // per, V7X_MXU_DIM, V7X_MXU_DIM).astype(BF16)


def kernel(x, p, mix_norm, w_in, conv_w, conv_b, lru_wa, lru_ba, lru_wi, lru_bi, lru_lambda, sgu_ln_g, sgu_ln_b, sgu_ws, sgu_bs, w_out, ffn_norm, router_group_w, router_group_b, router_expert_w, router_expert_b, expert_w1, expert_w3, expert_w2, ple_norm, ple_gate_w, ple_up_w, final_norm):
    cfg = _tiles()
    bsz, seq, d = x.shape
    ntok = bsz * seq
    tm = cfg["expert_rows"]
    depth = w_in.shape[0]
    assert depth == 1, "the ple kernel applies the final norm, so it must be the last layer"
    l = 0
    nw_rows = V7X_SC_CORES * V7X_SC_SUBCORES * cfg["sc_rows"]
    assert cfg["mixer_rows"] % CHUNK == 0 and seq % cfg["mixer_rows"] == 0
    assert ntok % cfg["router_rows"] == 0 and ntok % cfg["ple_rows"] == 0 and ntok % nw_rows == 0
    assert ntok % tm == 0 and lru_wa.shape[1:] == (LRU_BLOCKS, d // LRU_BLOCKS, d // LRU_BLOCKS)
    assert max(cfg[k] for k in cfg if k.endswith("_vmem")) < V7X_VMEM_BYTES
    b_router = jnp.concatenate([
        router_group_b[l], jnp.zeros((EXPERT_ROW0 - N_GROUPS,), F32), router_expert_b[l],
        jnp.zeros((ROUTER_ROWS - EXPERT_ROW0 - N_EXPERTS,), F32)])[:, None]
    ts = cfg["mixer_rows"]
    group = ts // V7X_SUBLANES
    bs_tile = jnp.tile(sgu_bs[l], (1, ts // CHUNK)).reshape(SGU_GROUPS, V7X_SUBLANES, group)
    bs_tile = jnp.transpose(bs_tile, (2, 1, 0)).reshape(ts, SGU_GROUPS)
    x1, hp = _mixer_call(
        x, mix_norm[l][None], w_in[l], conv_w[l], conv_b[l][None],
        _blockdiag_pack(lru_wa[l]), lru_ba[l][None], _blockdiag_pack(lru_wi[l]), lru_bi[l][None],
        lru_lambda[l][None], sgu_ln_g[l][None], sgu_ln_b[l][None], sgu_ws[l], bs_tile,
        w_out[l], ffn_norm[l][None])
    hp = hp.reshape(ntok, d // 2)
    pos, gate, cnt = _router_call(hp, router_group_w[l], router_expert_w[l], b_router)

    cap = ntok
    tiles_per_expert = (cnt[:, 0].astype(jnp.int32) + tm - 1) // tm
    nw = V7X_SC_CORES * V7X_SC_SUBCORES
    ch = cfg["sc_rows"]
    pos_w = jnp.transpose(pos[:TOP_K].reshape(TOP_K, nw, ntok // (nw * ch), ch), (1, 0, 2, 3))

    hs = _dispatch_call(hp, pos_w, N_EXPERTS * cap)
    ys = _expert_call(tiles_per_expert, hs, expert_w1[l], expert_w3[l], expert_w2[l], cap)
    yg0, yg1 = _combine_call(ys, pos_w)

    out = _ple_call(x1.reshape(ntok, d), yg0, yg1, gate, p[l].reshape(ntok, -1), ple_norm[l][None],
                    ple_gate_w[l], ple_up_w[l], final_norm[None])
    return out.reshape(bsz, seq, d)
```

```python
import functools

import jax
import jax.numpy as jnp
from jax import lax
from jax.experimental import pallas as pl
from jax.experimental.pallas import tpu as pltpu
from jax.experimental.pallas import tpu_sc as plsc

F32 = jnp.float32
BF16 = jnp.bfloat16
U32 = jnp.uint32

LRU_BLOCKS = 16
CONV_WIDTH = 4
LRU_C = 8.0
SGU_GROUPS = 8
CHUNK = 128
N_GROUPS = 4
EXPERTS_PER_GROUP = 8
N_EXPERTS = N_GROUPS * EXPERTS_PER_GROUP
TOP_K = 2
EPS = 1e-6

V7X_MXU_DIM = 256
V7X_SUBLANES = 8
V7X_LANES = 128
V7X_VMEM_BYTES = 64 * 1024 * 1024
V7X_SC_CORES = 2
V7X_SC_SUBCORES = 16

EXPERT_LOOKAHEAD = 3
ROUTER_SUBBLOCKS = 8
EXPERT_SUBBLOCKS = 2
PLE_SUBBLOCKS = 4
ROUTER_ROWS = V7X_LANES
EXPERT_ROW0 = V7X_SUBLANES


def _tiles():
    return dict(
        mixer_rows=256,
        expert_rows=512,
        ple_rows=1024,
        sc_rows=128,
        router_rows=4096,
        mixer_vmem=52 * 1024 * 1024,
        expert_vmem=40 * 1024 * 1024,
        ple_vmem=48 * 1024 * 1024,
        router_vmem=40 * 1024 * 1024,
    )


def _dot(a, b):
    return jnp.dot(a, b, preferred_element_type=F32)


def _sigmoid(x):
    return 0.5 * jnp.tanh(0.5 * x) + 0.5


def _rmsnorm(x, g):
    ms = jnp.mean(x * x, axis=-1, keepdims=True)
    return x * lax.rsqrt(ms + EPS) * g


def _pack_bf16_pair(lo, hi):
    lo_b = lax.bitcast_convert_type(lo.astype(BF16).astype(F32), U32)
    hi_b = lax.bitcast_convert_type(hi.astype(BF16).astype(F32), U32)
    return (hi_b & jnp.uint32(0xFFFF0000)) | lax.shift_right_logical(lo_b, jnp.uint32(16))


def _unpack_bf16_pair(w):
    lo = lax.bitcast_convert_type(lax.shift_left(w, jnp.uint32(16)), F32)
    hi = lax.bitcast_convert_type(w & jnp.uint32(0xFFFF0000), F32)
    return lo, hi


def _const_spec(shape):
    zeros = (0,) * len(shape)
    return pl.BlockSpec(shape, lambda *_: zeros, pipeline_mode=pl.Buffered(1))


def _tile_copies(hbm, buf, sem, b, row0, slot, to_hbm):
    group = buf.shape[1]
    copies = []
    for r in range(V7X_SUBLANES):
        hbm_rows = hbm.at[b, pl.ds(row0 + group * r, group), :]
        vmem_rows = buf.at[slot, :, r, :]
        src, dst = (vmem_rows, hbm_rows) if to_hbm else (hbm_rows, vmem_rows)
        copies.append(pltpu.make_async_copy(src, dst, sem.at[slot]))
    return copies


def _mixer_kernel(x_hbm, mixn_ref, win_hbm, convw_ref, convb_ref, wa_ref, ba_ref, wi_ref, bi_ref,
                  lam_ref, lng_ref, lnb_ref, ws_ref, bsp_ref, wout_hbm, ffn_ref,
                  x1_hbm, hp_hbm,
                  xbuf, z0_ref, z1_ref, x1buf, hpbuf, xsem, x1sem, hpsem, wsm_ref, ztail_ref, hcar_ref,
                  win_ref, wout_ref, wsem, au_ref,
                  *, nseq):
    j = pl.program_id(0)
    ntile = pl.num_programs(0) - 1
    _, group, _, d = xbuf.shape
    rows = group * V7X_SUBLANES
    half = d // 2
    ta = jnp.minimum(j, ntile - 1)
    tb = jnp.maximum(j - 1, 0)
    s = lax.rem(tb, nseq)
    slot = lax.rem(tb, 2)

    def fetch(t):
        return _tile_copies(x_hbm, xbuf, xsem, lax.div(t, nseq), lax.rem(t, nseq) * rows,
                            lax.rem(t, 3), to_hbm=False)

    def put(t):
        tb_, ts_, sl = lax.div(t, nseq), lax.rem(t, nseq) * rows, lax.rem(t, 2)
        return (_tile_copies(x1_hbm, x1buf, x1sem, tb_, ts_, sl, to_hbm=True)
                + _tile_copies(hp_hbm, hpbuf, hpsem, tb_, ts_, sl, to_hbm=True))

    @pl.when(j == 0)
    def _():
        for c in fetch(0):
            c.start()
        stage = (z0_ref, z1_ref)
        n_in_chunks = win_hbm.shape[0] // rows

        def win_copy(c):
            return pltpu.make_async_copy(win_hbm.at[pl.ds(c * rows, rows), :], stage[c % 2], wsem.at[c % 2])

        win_copy(0).start()
        for c in range(n_in_chunks):
            if c + 1 < n_in_chunks:
                win_copy(c + 1).start()
            win_copy(c).wait()
            plain = 4 * d
            win_ref[c * rows:(c + 1) * rows, :plain] = stage[c % 2][:, :plain].astype(BF16)
            win_ref[c * rows:(c + 1) * rows, plain:] = (0.5 * stage[c % 2][:, plain:]).astype(BF16)
        n_out_chunks = wout_hbm.shape[0] // rows
        out_copies = [pltpu.make_async_copy(wout_hbm.at[pl.ds(c * rows, rows), :],
                                            z0_ref.at[:, c * d:(c + 1) * d], wsem.at[0])
                      for c in range(n_out_chunks)]
        for cp in out_copies:
            cp.start()
        for cp in out_copies:
            cp.wait()
        for c in range(n_out_chunks):
            wout_ref[c * rows:(c + 1) * rows, :] = (0.25 * z0_ref[:, c * d:(c + 1) * d]).astype(BF16)
        z1_ref[...] = jnp.zeros_like(z1_ref)
        i_idx = lax.broadcasted_iota(jnp.int32, (rows, rows), 0)
        j_idx = lax.broadcasted_iota(jnp.int32, (rows, rows), 1)
        t_i = group * lax.rem(i_idx, V7X_SUBLANES) + lax.div(i_idx, V7X_SUBLANES)
        t_j = group * lax.rem(j_idx, V7X_SUBLANES) + lax.div(j_idx, V7X_SUBLANES)
        keep = (t_i >= t_j) & (lax.div(t_i, CHUNK) == lax.div(t_j, CHUNK))
        pick_rows = jnp.where(t_i == j_idx, 1.0, 0.0).astype(BF16)
        pick_cols = jnp.where(i_idx == t_j, 1.0, 0.0).astype(BF16)
        reps = rows // CHUNK
        for g in range(SGU_GROUPS):
            w_chunk = ws_ref[g].astype(BF16)
            w_rows = jnp.concatenate([w_chunk] * reps, axis=1)
            w_full = jnp.concatenate([w_rows] * reps, axis=0)
            w_perm = _dot(_dot(pick_rows, w_full).astype(BF16), pick_cols)
            wsm_ref[g] = jnp.where(keep, w_perm, 0.0).astype(BF16)

    @pl.when(j + 1 < ntile)
    def _():
        for c in fetch(j + 1):
            c.start()

    @pl.when(j < ntile)
    def _():
        for c in fetch(j):
            c.wait()

    @pl.when(s == 0)
    def _():
        ztail_ref[...] = jnp.zeros_like(ztail_ref)
        hcar_ref[...] = jnp.zeros_like(hcar_ref)

    def compute(z_w, z_r):
        xa_in = xbuf[lax.rem(ta, 3)].reshape(rows, d)
        h_next = _rmsnorm(xa_in, mixn_ref[...]).astype(BF16)
        pw = d // 2

        def project(k):
            z_w[:, k * pw:(k + 1) * pw] = _dot(h_next, win_ref[:, k * pw:(k + 1) * pw])

        x = xbuf[lax.rem(tb, 3)].reshape(rows, d)

        def sec(k, c0, c1):
            return z_r[:, k * d + c0:k * d + c1]

        def one_plus_tanh_gelu(v):
            c = 0.7978845608028654
            return 1.0 + jnp.tanh(v * (c + (c * 0.044715) * (v * v)))

        cw = 0.5 * convw_ref[...]
        cb_h = 0.5 * convb_ref[...]
        ba_h = 0.5 * ba_ref[...]
        bi_h = 0.5 * bi_ref[...]
        neg_lam = -lam_ref[...]
        softplus = jnp.maximum(neg_lam, 0.0) + jnp.log1p(jnp.exp(-jnp.abs(neg_lam)))
        c_a = (-0.5 * LRU_C) * softplus
        blk = V7X_MXU_DIM
        sub3 = lax.broadcasted_iota(jnp.int32, (CONV_WIDTH - 1, V7X_SUBLANES, blk), 1)
        for n in range(d // blk):
            project(n)
            c0, c1 = n * blk, (n + 1) * blk
            z3 = sec(0, c0, c1).reshape(group, V7X_SUBLANES, blk)
            tail = z3[group - (CONV_WIDTH - 1):]
            halo = jnp.where(sub3 == 0, pltpu.roll(ztail_ref[:, :, c0:c1], 1, axis=1),
                             pltpu.roll(tail, 1, axis=1))
            ztail_ref[:, :, c0:c1] = tail
            zext = jnp.concatenate([halo, z3], axis=0)
            xa_h = cb_h[:, c0:c1] + cw[CONV_WIDTH - 1:CONV_WIDTH, c0:c1] * z3
            for k in range(1, CONV_WIDTH):
                lo = CONV_WIDTH - 1 - k
                xa_h = xa_h + cw[lo:lo + 1, c0:c1] * zext[lo:lo + group]
            xa2 = xa_h.reshape(rows, blk)
            xa_bf = xa2.astype(BF16)
            th_r = jnp.tanh(_dot(xa_bf, wa_ref[n]) + ba_h[:, c0:c1])
            th_i = jnp.tanh(_dot(xa_bf, wi_ref[n]) + bi_h[:, c0:c1])
            a = jnp.exp(c_a[:, c0:c1] + c_a[:, c0:c1] * th_r)
            u = jnp.sqrt(1.0 - a * a) * ((1.0 + th_i) * xa2)
            au_ref[0, :, :, c0:c1] = a.reshape(group, V7X_SUBLANES, blk)
            au_ref[1, :, :, c0:c1] = u.reshape(group, V7X_SUBLANES, blk)

        acc_a = au_ref[0, 0]
        acc_u = au_ref[1, 0]
        for g in range(1, group):
            a_g = au_ref[0, g]
            acc_u = a_g * acc_u + au_ref[1, g]
            acc_a = a_g * acc_a
            au_ref[0, g] = acc_a
            au_ref[1, g] = acc_u
        h0 = hcar_ref[...]
        end_a, end_u = acc_a, acc_u
        sub8 = lax.broadcasted_iota(jnp.int32, end_a.shape, 0)
        shift = 1
        while shift < V7X_SUBLANES:
            keep = sub8 >= shift
            a_sh = pltpu.roll(end_a, shift, axis=0)
            u_sh = pltpu.roll(end_u, shift, axis=0)
            end_u = jnp.where(keep, end_a * u_sh + end_u, end_u)
            end_a = jnp.where(keep, end_a * a_sh, end_a)
            shift *= 2
        h_end = end_a * h0 + end_u
        h_in = jnp.where(sub8 == 0, h0, pltpu.roll(h_end, 1, axis=0))
        hcar_ref[...] = h_end[V7X_SUBLANES - 1:V7X_SUBLANES, :]
        term_a = []
        for n in range(d // blk):
            c0, c1 = n * blk, (n + 1) * blk
            hseq = au_ref[0, :, :, c0:c1] * h_in[:, c0:c1] + au_ref[1, :, :, c0:c1]
            zg = sec(1, c0, c1)
            term_a.append(((1.0 + jnp.tanh(sec(4, c0, c1))) * one_plus_tanh_gelu(zg))
                          * (zg * hseq.reshape(rows, blk)))

        project(4)
        zv = sec(3, 0, d)
        gv2 = zv * one_plus_tanh_gelu(zv)
        project(5)
        mu = jnp.mean(gv2, axis=-1, keepdims=True)
        xc = gv2 - mu
        var = jnp.mean(xc * xc, axis=-1, keepdims=True)
        v_bf = (xc * lax.rsqrt(var + 4.0 * EPS) * lng_ref[...] + lnb_ref[...]).astype(BF16)
        project(6)
        gdim = d // SGU_GROUPS
        term_b = []
        for g in range(SGU_GROUPS):
            c0, c1 = g * gdim, (g + 1) * gdim
            if g in (1, 3, 5, 6, 7):
                project({1: 7, 3: 8, 5: 9, 6: 10, 7: 11}[g])
            sp = _dot(wsm_ref[g], v_bf[:, c0:c1]) + bsp_ref[:, g:g + 1]
            zu = sec(2, c0, c1)
            term_b.append(((1.0 + jnp.tanh(sec(5, c0, c1))) * one_plus_tanh_gelu(zu)) * (zu * sp))
        merged4 = jnp.concatenate(term_a, axis=1) + jnp.concatenate(term_b, axis=1)

        x1 = x + _dot(merged4.astype(BF16), wout_ref[...])

        hn = _rmsnorm(x1, ffn_ref[...])
        hp = _pack_bf16_pair(hn[:, :half], hn[:, half:])

        @pl.when(j >= 3)
        def _():
            for c in put(tb - 2):
                c.wait()

        x1buf[slot] = x1.reshape(group, V7X_SUBLANES, d)
        hpbuf[slot] = hp.reshape(group, V7X_SUBLANES, half)

        @pl.when(j >= 1)
        def _():
            for c in put(tb):
                c.start()

    @pl.when(lax.rem(j, 2) == 0)
    def _():
        compute(z0_ref, z1_ref)

    @pl.when(lax.rem(j, 2) == 1)
    def _():
        compute(z1_ref, z0_ref)

    @pl.when(j == ntile)
    def _():
        for c in put(tb):
            c.wait()

        @pl.when(ntile >= 2)
        def _():
            for c in put(tb - 1):
                c.wait()


def _mixer_call(x, mix_norm, w_in, conv_w, conv_b, wa_blk, ba, wi_blk, bi, lam, ln_g, ln_b, ws,
                bs_tile, w_out, ffn_norm):
    cfg = _tiles()
    bsz, seq, d = x.shape
    ts = cfg["mixer_rows"]
    group = ts // V7X_SUBLANES
    nseq = seq // ts
    ntile = bsz * nseq
    row1 = (1, d)
    in_specs = [
        pl.BlockSpec(memory_space=pl.ANY),
        _const_spec(row1),
        pl.BlockSpec(memory_space=pl.ANY),
        _const_spec(conv_w.shape), _const_spec(row1),
        _const_spec(wa_blk.shape), _const_spec(row1),
        _const_spec(wi_blk.shape), _const_spec(row1),
        _const_spec(row1),
        _const_spec(row1), _const_spec(row1),
        _const_spec(ws.shape), _const_spec(bs_tile.shape),
        pl.BlockSpec(memory_space=pl.ANY), _const_spec(row1),
    ]
    out_shape = [
        jax.ShapeDtypeStruct((bsz, seq, d), F32),
        jax.ShapeDtypeStruct((bsz, seq, d // 2), U32),
    ]
    out_specs = [
        pl.BlockSpec(memory_space=pl.ANY),
        pl.BlockSpec(memory_space=pl.ANY),
    ]
    scratch = [
        pltpu.VMEM((3, group, V7X_SUBLANES, d), F32),
        pltpu.VMEM((ts, w_in.shape[1]), F32),
        pltpu.VMEM((ts, w_in.shape[1]), F32),
        pltpu.VMEM((2, group, V7X_SUBLANES, d), F32),
        pltpu.VMEM((2, group, V7X_SUBLANES, d // 2), U32),
        pltpu.SemaphoreType.DMA((3,)),
        pltpu.SemaphoreType.DMA((2,)),
        pltpu.SemaphoreType.DMA((2,)),
        pltpu.VMEM((SGU_GROUPS, ts, ts), BF16),
        pltpu.VMEM((CONV_WIDTH - 1, V7X_SUBLANES, d), F32),
        pltpu.VMEM((1, d), F32),
        pltpu.VMEM(w_in.shape, BF16),
        pltpu.VMEM(w_out.shape, BF16),
        pltpu.SemaphoreType.DMA((2,)),
        pltpu.VMEM((2, group, V7X_SUBLANES, d), F32),
    ]
    return pl.pallas_call(
        functools.partial(_mixer_kernel, nseq=nseq),
        grid=(ntile + 1,),
        in_specs=in_specs,
        out_specs=out_specs,
        out_shape=out_shape,
        scratch_shapes=scratch,
        compiler_params=pltpu.CompilerParams(
            dimension_semantics=("arbitrary",),
            vmem_limit_bytes=cfg["mixer_vmem"]),
        name="mixer",
    )(x, mix_norm, w_in, conv_w, conv_b, wa_blk, ba, wi_blk, bi, lam, ln_g, ln_b, ws, bs_tile,
      w_out, ffn_norm)


def _router_kernel(hp_ref, wg_ref, we_ref, br_ref, pos_ref, gate_ref, cnt_ref, ccar_ref, wr_ref,
                   *, expert_capacity):
    rows = hp_ref.shape[0]

    @pl.when(pl.program_id(0) == 0)
    def _():
        ccar_ref[...] = jnp.zeros_like(ccar_ref)
        wr_ref[...] = jnp.zeros_like(wr_ref)
        wr_ref[:, 0:N_GROUPS] = wg_ref[...].astype(BF16)
        wr_ref[:, EXPERT_ROW0:EXPERT_ROW0 + N_EXPERTS] = we_ref[...].astype(BF16)

    sub_rows = rows // ROUTER_SUBBLOCKS
    lts = []
    for q in range(ROUTER_SUBBLOCKS):
        lo, hi = _unpack_bf16_pair(hp_ref[q * sub_rows:(q + 1) * sub_rows, :])
        hn = jnp.concatenate([lo, hi], axis=1)
        logits = _dot(hn.astype(BF16), wr_ref[...])
        lts.append(jnp.transpose(logits) + br_ref[...])
    sub = lax.broadcasted_iota(jnp.int32, (V7X_SUBLANES, sub_rows), 0)
    subf = sub.astype(F32)
    big = jnp.float32(1e9)
    eid = lax.broadcasted_iota(jnp.int32, (N_EXPERTS, sub_rows), 0).astype(F32)
    sb = V7X_MXU_DIM
    before = (lax.broadcasted_iota(jnp.int32, (sb, sb), 0)
              < lax.broadcasted_iota(jnp.int32, (sb, sb), 1))
    before = jnp.where(before, 1.0, 0.0).astype(BF16)
    cap = float(expert_capacity)
    zero = jnp.zeros((V7X_SUBLANES - TOP_K, sub_rows), F32)
    running = ccar_ref[:, 0:1]
    for q, lt in enumerate(lts):
        lg = jnp.where(sub < N_GROUPS, lt[0:V7X_SUBLANES, :], -jnp.inf)
        g_exp = jnp.exp(lg - jnp.max(lg, axis=0, keepdims=True))
        g_prob = g_exp / jnp.sum(g_exp, axis=0, keepdims=True)
        g_top = jnp.max(g_prob, axis=0, keepdims=True)
        g_idx = jnp.min(jnp.where(g_prob == g_top, subf, big), axis=0, keepdims=True)

        e_sel = jnp.zeros((EXPERTS_PER_GROUP, sub_rows), F32)
        for g in range(N_GROUPS):
            r0 = EXPERT_ROW0 + g * EXPERTS_PER_GROUP
            e_sel = jnp.where(g_idx == g, lt[r0:r0 + EXPERTS_PER_GROUP, :], e_sel)
        e_exp = jnp.exp(e_sel - jnp.max(e_sel, axis=0, keepdims=True))
        e_prob = e_exp / jnp.sum(e_exp, axis=0, keepdims=True)
        p1 = jnp.max(e_prob, axis=0, keepdims=True)
        i1 = jnp.min(jnp.where(e_prob == p1, subf, big), axis=0, keepdims=True)
        rest = jnp.where(subf == i1, -1.0, e_prob)
        p2 = jnp.max(rest, axis=0, keepdims=True)
        i2 = jnp.min(jnp.where(rest == p2, subf, big), axis=0, keepdims=True)
        psum = p1 + p2
        gate1 = g_top * (p1 / psum)
        gate2 = g_top * (p2 / psum)
        gid1 = g_idx * EXPERTS_PER_GROUP + i1
        gid2 = g_idx * EXPERTS_PER_GROUP + i2

        hit1 = eid == gid1
        hit2 = eid == gid2
        cnt = jnp.where(hit1 | hit2, 1.0, 0.0)
        base = []
        for c in range(sub_rows // sb):
            part = cnt[:, c * sb:(c + 1) * sb]
            base.append(running + _dot(part.astype(BF16), before))
            running = running + jnp.sum(part, axis=1, keepdims=True)
        base = jnp.concatenate(base, axis=1)
        rank1 = jnp.sum(jnp.where(hit1, base, 0.0), axis=0, keepdims=True)
        rank2 = jnp.sum(jnp.where(hit2, base, 0.0), axis=0, keepdims=True)
        pos = jnp.concatenate([gid1 * cap + rank1, gid2 * cap + rank2, zero], axis=0)
        pos_ref[:, q * sub_rows:(q + 1) * sub_rows] = pos.astype(jnp.int32)
        gate_ref[q * sub_rows:(q + 1) * sub_rows, :] = jnp.transpose(
            jnp.concatenate([gate1, gate2, zero], axis=0))
    total = jnp.broadcast_to(running, ccar_ref.shape)
    ccar_ref[...] = total
    cnt_ref[...] = total


def _router_call(hp, w_group, w_expert, b_router):
    cfg = _tiles()
    ntok, half = hp.shape
    tr = cfg["router_rows"]
    return pl.pallas_call(
        functools.partial(_router_kernel, expert_capacity=ntok),
        grid=(ntok // tr,),
        in_specs=[
            pl.BlockSpec((tr, half), lambda i: (i, 0)),
            _const_spec(w_group.shape),
            _const_spec(w_expert.shape),
            _const_spec(b_router.shape),
        ],
        out_specs=[
            pl.BlockSpec((V7X_SUBLANES, tr), lambda i: (0, i)),
            pl.BlockSpec((tr, V7X_SUBLANES), lambda i: (i, 0)),
            pl.BlockSpec((N_EXPERTS, V7X_LANES), lambda i: (0, 0)),
        ],
        out_shape=[
            jax.ShapeDtypeStruct((V7X_SUBLANES, ntok), jnp.int32),
            jax.ShapeDtypeStruct((ntok, V7X_SUBLANES), F32),
            jax.ShapeDtypeStruct((N_EXPERTS, V7X_LANES), F32),
        ],
        scratch_shapes=[
            pltpu.VMEM((N_EXPERTS, V7X_LANES), F32),
            pltpu.VMEM((w_group.shape[0], ROUTER_ROWS), BF16),
        ],
        compiler_params=pltpu.CompilerParams(
            dimension_semantics=("arbitrary",),
            vmem_limit_bytes=cfg["router_vmem"]),
        name="router",
    )(hp, w_group, w_expert, b_router)


def _expert_kernel(nt_ref, base_ref, texp_ref, tloc_ref, hs_hbm, w1_ref, w3_ref, w2_ref, ys_hbm,
                   hbuf, ybuf, hsem, ysem, w1b_ref, w3b_ref, w2b_ref, *, capacity):
    e = pl.program_id(0)
    n_exp = pl.num_programs(0)
    nt = nt_ref[e]
    base = base_ref[e]
    total = base_ref[n_exp - 1] + nt_ref[n_exp - 1]
    n_in, tm, _ = hbuf.shape
    n_out = ybuf.shape[0]
    ahead = n_in - 1

    def load(g):
        slot = lax.rem(g, n_in)
        rows = pl.ds(texp_ref[g] * capacity + tloc_ref[g] * tm, tm)
        return pltpu.make_async_copy(hs_hbm.at[rows], hbuf.at[slot], hsem.at[slot])

    def store(t, slot):
        rows = pl.ds(e * capacity + t * tm, tm)
        return pltpu.make_async_copy(ybuf.at[slot], ys_hbm.at[rows], ysem.at[slot])

    @pl.when(e == 0)
    def _():
        for g0 in range(ahead):
            @pl.when(g0 < total)
            def _():
                load(g0).start()

    w1b_ref[...] = w1_ref[...].astype(BF16)
    w3b_ref[...] = w3_ref[...].astype(BF16)
    w2b_ref[...] = w2_ref[...].astype(BF16)

    @pl.loop(0, nt)
    def _(t):
        g = base + t

        @pl.when(g + ahead < total)
        def _():
            load(g + ahead).start()

        load(g).wait()
        slot = lax.rem(g, n_out)

        @pl.when(g >= n_out)
        def _():
            store(t, slot).wait()

        sub_rows = tm // EXPERT_SUBBLOCKS
        blocks = [pl.ds(q * sub_rows, sub_rows) for q in range(EXPERT_SUBBLOCKS)]
        rows_in = []
        for rs in blocks:
            lo, hi = _unpack_bf16_pair(hbuf[lax.rem(g, n_in), rs, :])
            rows_in.append(jnp.concatenate([lo, hi], axis=1).astype(BF16))
        up = [(_dot(h, w1b_ref[...]), _dot(h, w3b_ref[...])) for h in rows_in]
        down = []
        for a, b in up:
            hid = (a * _sigmoid(a)) * b
            down.append(_dot(hid.astype(BF16), w2b_ref[...]))
        for rs, y in zip(blocks, down):
            half = y.shape[1] // 2
            ybuf[slot, rs, :] = _pack_bf16_pair(y[:, :half], y[:, half:])
        store(t, slot).start()

    @pl.when(e + 1 == n_exp)
    def _():
        for back in range(1, n_out + 1):
            @pl.when(total >= back)
            def _():
                store(0, lax.rem(total - back, n_out)).wait()


def _expert_call(tiles_per_expert, hs, w1, w3, w2, capacity):
    cfg = _tiles()
    tm = cfg["expert_rows"]
    prow, half = hs.shape
    n_exp, d, f = w1.shape
    ends = jnp.cumsum(tiles_per_expert)
    base = ends - tiles_per_expert
    g = jnp.arange(capacity * TOP_K // tm + n_exp, dtype=jnp.int32)
    texp = jnp.minimum(jnp.sum((ends[None, :] <= g[:, None]).astype(jnp.int32), axis=1), n_exp - 1)
    onehot = texp[:, None] == jnp.arange(n_exp, dtype=jnp.int32)[None, :]
    tloc = g - jnp.sum(jnp.where(onehot, base[None, :], 0), axis=1)

    def w_map(e, *_):
        return (e, 0, 0)

    grid_spec = pltpu.PrefetchScalarGridSpec(
        num_scalar_prefetch=4,
        grid=(n_exp,),
        in_specs=[
            pl.BlockSpec(memory_space=pl.ANY),
            pl.BlockSpec((None, d, f), w_map),
            pl.BlockSpec((None, d, f), w_map),
            pl.BlockSpec((None, f, d), w_map),
        ],
        out_specs=pl.BlockSpec(memory_space=pl.ANY),
        scratch_shapes=[
            pltpu.VMEM((EXPERT_LOOKAHEAD + 1, tm, half), U32),
            pltpu.VMEM((2, tm, half), U32),
            pltpu.SemaphoreType.DMA((EXPERT_LOOKAHEAD + 1,)),
            pltpu.SemaphoreType.DMA((2,)),
            pltpu.VMEM((d, f), BF16),
            pltpu.VMEM((d, f), BF16),
            pltpu.VMEM((f, d), BF16),
        ],
    )
    return pl.pallas_call(
        functools.partial(_expert_kernel, capacity=capacity),
        grid_spec=grid_spec,
        out_shape=jax.ShapeDtypeStruct((prow, half), U32),
        compiler_params=pltpu.CompilerParams(
            dimension_semantics=("arbitrary",),
            vmem_limit_bytes=cfg["expert_vmem"]),
        name="experts",
    )(tiles_per_expert, base, texp, tloc, hs, w1, w3, w2)


def _sc_mesh():
    return plsc.VectorSubcoreMesh(core_axis_name="c", subcore_axis_name="s",
                                  num_cores=V7X_SC_CORES, num_subcores=V7X_SC_SUBCORES)


def _sc_worker_id():
    return lax.axis_index("s") * V7X_SC_CORES + lax.axis_index("c")


def _dispatch_call(hp, pos_w, out_rows):
    cfg = _tiles()
    ntok, half = hp.shape
    nw, topk, nch, ch = pos_w.shape
    per_w = nch * ch

    def body(hp_hbm, pos_hbm, hs_hbm, idx_v, buf, wsem):
        wid = _sc_worker_id()
        pltpu.sync_copy(pos_hbm.at[wid], idx_v)

        for c in range(nch):
            pltpu.sync_copy(hp_hbm.at[pl.ds(wid * per_w + c * ch, ch)], buf)
            writes = [pltpu.make_async_copy(buf, hs_hbm.at[idx_v.at[k, c]], wsem.at[k]) for k in range(topk)]
            for w in writes:
                w.start()
            for w in writes:
                w.wait()

    assert nw == V7X_SC_CORES * V7X_SC_SUBCORES and nw * per_w == ntok and ch == cfg["sc_rows"]
    return pl.kernel(
        body,
        out_type=jax.ShapeDtypeStruct((out_rows, half), U32),
        mesh=_sc_mesh(),
        scratch_types=[
            pltpu.VMEM((topk, nch, ch), jnp.int32),
            pltpu.VMEM((ch, half), U32),
            pltpu.SemaphoreType.DMA((topk,)),
        ],
        name="dispatch",
    )(hp, pos_w)


def _combine_call(ys, pos_w):
    cfg = _tiles()
    _, half = ys.shape
    nw, topk, nch, ch = pos_w.shape
    per_w = nch * ch
    ntok = nw * per_w

    def body(ys_hbm, pos_hbm, *rest):
        outs = rest[:topk]
        idx_v, buf = rest[topk:]
        wid = _sc_worker_id()
        pltpu.sync_copy(pos_hbm.at[wid], idx_v)
        for c in range(nch):
            for k in range(topk):
                pltpu.sync_copy(ys_hbm.at[idx_v.at[k, c]], buf)
                pltpu.sync_copy(buf, outs[k].at[pl.ds(wid * per_w + c * ch, ch)])

    assert nw == V7X_SC_CORES * V7X_SC_SUBCORES and ch == cfg["sc_rows"]
    return pl.kernel(
        body,
        out_type=[jax.ShapeDtypeStruct((ntok, half), U32)] * topk,
        mesh=_sc_mesh(),
        scratch_types=[
            pltpu.VMEM((topk, nch, ch), jnp.int32),
            pltpu.VMEM((ch, half), U32),
        ],
        name="combine",
    )(ys, pos_w)


def _ple_kernel(x1_ref, yg0_ref, yg1_ref, gate_ref, p_ref, plen_ref, wg32_ref, wu32_ref, fin_ref, o_ref,
                wg_ref, wu_ref):
    @pl.when(pl.program_id(0) == 0)
    def _():
        wg_ref[...] = (0.5 * wg32_ref[...]).astype(BF16)
        wu_ref[...] = (0.5 * wu32_ref[...]).astype(BF16)

    rows = x1_ref.shape[0]
    sub_rows = rows // PLE_SUBBLOCKS
    for q in range(PLE_SUBBLOCKS):
        rs = pl.ds(q * sub_rows, sub_rows)
        lo0, hi0 = _unpack_bf16_pair(yg0_ref[rs, :])
        lo1, hi1 = _unpack_bf16_pair(yg1_ref[rs, :])
        g0 = gate_ref[rs, 0:1]
        g1 = gate_ref[rs, 1:2]
        moe = g0 * jnp.concatenate([lo0, hi0], axis=1) + g1 * jnp.concatenate([lo1, hi1], axis=1)
        x2 = x1_ref[rs, :] + moe
        r = _rmsnorm(x2, plen_ref[...]).astype(BF16)
        gt2 = 1.0 + jnp.tanh(_dot(r, wg_ref[...]))
        up_h = _dot(p_ref[rs, :].astype(BF16), wu_ref[...])
        x3 = x2 + gt2 * up_h
        o_ref[rs, :] = _rmsnorm(x3, fin_ref[...])


def _ple_call(x1, yg0, yg1, gates, p, ple_norm, wg, wu, final_norm):
    cfg = _tiles()
    ntok, d = x1.shape
    tp = cfg["ple_rows"]
    pdim = p.shape[1]
    return pl.pallas_call(
        _ple_kernel,
        grid=(ntok // tp,),
        in_specs=[
            pl.BlockSpec((tp, d), lambda i: (i, 0)),
            pl.BlockSpec((tp, d // 2), lambda i: (i, 0)),
            pl.BlockSpec((tp, d // 2), lambda i: (i, 0)),
            pl.BlockSpec((tp, V7X_SUBLANES), lambda i: (i, 0)),
            pl.BlockSpec((tp, pdim), lambda i: (i, 0)),
            _const_spec((1, d)),
            _const_spec(wg.shape),
            _const_spec(wu.shape),
            _const_spec((1, d)),
        ],
        out_specs=pl.BlockSpec((tp, d), lambda i: (i, 0)),
        out_shape=jax.ShapeDtypeStruct((ntok, d), F32),
        scratch_shapes=[pltpu.VMEM(wg.shape, BF16), pltpu.VMEM(wu.shape, BF16)],
        compiler_params=pltpu.CompilerParams(
            dimension_semantics=("arbitrary",),
            vmem_limit_bytes=cfg["ple_vmem"]),
        name="ple",
    )(x1, yg0, yg1, gates, p, ple_norm, wg, wu, final_norm)


def _blockdiag_pack(w):
    nb, bd, _ = w.shape
    per = V7X_MXU_DIM // bd
    w4 = w.reshape(nb // per, per, bd, bd)
    eye = jnp.eye(per, dtype=w.dtype)
    out = jnp.einsum("jpab,pq->jpaqb", w4, eye)
    return out.reshape(nb // per, V7X_MXU_DIM, V7X_MXU_DIM).astype(BF16)


def kernel(x, p, mix_norm, w_in, conv_w, conv_b, lru_wa, lru_ba, lru_wi, lru_bi, lru_lambda, sgu_ln_g, sgu_ln_b, sgu_ws, sgu_bs, w_out, ffn_norm, router_group_w, router_group_b, router_expert_w, router_expert_b, expert_w1, expert_w3, expert_w2, ple_norm, ple_gate_w, ple_up_w, final_norm):
    cfg = _tiles()
    bsz, seq, d = x.shape
    ntok = bsz * seq
    tm = cfg["expert_rows"]
    depth = w_in.shape[0]
    assert depth == 1, "the ple kernel applies the final norm, so it must be the last layer"
    l = 0
    nw_rows = V7X_SC_CORES * V7X_SC_SUBCORES * cfg["sc_rows"]
    assert cfg["mixer_rows"] % CHUNK == 0 and seq % cfg["mixer_rows"] == 0
    assert ntok % cfg["router_rows"] == 0 and ntok % cfg["ple_rows"] == 0 and ntok % nw_rows == 0
    assert ntok % tm == 0 and lru_wa.shape[1:] == (LRU_BLOCKS, d // LRU_BLOCKS, d // LRU_BLOCKS)
    assert max(cfg[k] for k in cfg if k.endswith("_vmem")) < V7X_VMEM_BYTES
    b_router = jnp.concatenate([
        router_group_b[l], jnp.zeros((EXPERT_ROW0 - N_GROUPS,), F32), router_expert_b[l],
        jnp.zeros((ROUTER_ROWS - EXPERT_ROW0 - N_EXPERTS,), F32)])[:, None]
    ts = cfg["mixer_rows"]
    group = ts // V7X_SUBLANES
    bs_tile = jnp.tile(sgu_bs[l], (1, ts // CHUNK)).reshape(SGU_GROUPS, V7X_SUBLANES, group)
    bs_tile = jnp.transpose(bs_tile, (2, 1, 0)).reshape(ts, SGU_GROUPS)
    x1, hp = _mixer_call(
        x, mix_norm[l][None], w_in[l], conv_w[l], conv_b[l][None],
        _blockdiag_pack(lru_wa[l]), lru_ba[l][None], _blockdiag_pack(lru_wi[l]), lru_bi[l][None],
        lru_lambda[l][None], sgu_ln_g[l][None], sgu_ln_b[l][None], sgu_ws[l], bs_tile,
        w_out[l], ffn_norm[l][None])
    hp = hp.reshape(ntok, d // 2)
    pos, gate, cnt = _router_call(hp, router_group_w[l], router_expert_w[l], b_router)

    cap = ntok
    tiles_per_expert = (cnt[:, 0].astype(jnp.int32) + tm - 1) // tm
    nw = V7X_SC_CORES * V7X_SC_SUBCORES
    ch = cfg["sc_rows"]
    pos_w = jnp.transpose(pos[:TOP_K].reshape(TOP_K, nw, ntok // (nw * ch), ch), (1, 0, 2, 3))

    hs = _dispatch_call(hp, pos_w, N_EXPERTS * cap)
    ys = _expert_call(tiles_per_expert, hs, expert_w1[l], expert_w3[l], expert_w2[l], cap)
    yg0, yg1 = _combine_call(ys, pos_w)

    out = _ple_call(x1.reshape(ntok, d), yg0, yg1, gate, p[l].reshape(ntok, -1), ple_norm[l][None],
                    ple_gate_w[l], ple_up_w[l], final_norm[None])
    return out.reshape(bsz, seq, d)
```

```python
import functools

import jax
import jax.numpy as jnp
from jax import lax
from jax.experimental import pallas as pl
from jax.experimental.pallas import tpu as pltpu
from jax.experimental.pallas import tpu_sc as plsc

F32 = jnp.float32
BF16 = jnp.bfloat16
U32 = jnp.uint32

LRU_BLOCKS = 16
CONV_WIDTH = 4
LRU_C = 8.0
SGU_GROUPS = 8
CHUNK = 128
N_GROUPS = 4
EXPERTS_PER_GROUP = 8
N_EXPERTS = N_GROUPS * EXPERTS_PER_GROUP
TOP_K = 2
EPS = 1e-6

V7X_MXU_DIM = 256
V7X_SUBLANES = 8
V7X_LANES = 128
V7X_VMEM_BYTES = 64 * 1024 * 1024
V7X_SC_CORES = 2
V7X_SC_SUBCORES = 16

EXPERT_LOOKAHEAD = 3
ROUTER_SUBBLOCKS = 8
EXPERT_SUBBLOCKS = 2
PLE_SUBBLOCKS = 4
ROUTER_ROWS = V7X_LANES
EXPERT_ROW0 = V7X_SUBLANES


def _tiles():
    return dict(
        mixer_rows=256,
        expert_rows=512,
        ple_rows=1024,
        sc_rows=128,
        router_rows=4096,
        mixer_vmem=52 * 1024 * 1024,
        expert_vmem=40 * 1024 * 1024,
        ple_vmem=48 * 1024 * 1024,
        router_vmem=40 * 1024 * 1024,
    )


def _dot(a, b):
    return jnp.dot(a, b, preferred_element_type=F32)


def _sigmoid(x):
    return 0.5 * jnp.tanh(0.5 * x) + 0.5


def _rmsnorm(x, g):
    ms = jnp.mean(x * x, axis=-1, keepdims=True)
    return x * lax.rsqrt(ms + EPS) * g


def _pack_bf16_pair(lo, hi):
    lo_b = lax.bitcast_convert_type(lo.astype(BF16).astype(F32), U32)
    hi_b = lax.bitcast_convert_type(hi.astype(BF16).astype(F32), U32)
    return (hi_b & jnp.uint32(0xFFFF0000)) | lax.shift_right_logical(lo_b, jnp.uint32(16))


def _unpack_bf16_pair(w):
    lo = lax.bitcast_convert_type(lax.shift_left(w, jnp.uint32(16)), F32)
    hi = lax.bitcast_convert_type(w & jnp.uint32(0xFFFF0000), F32)
    return lo, hi


def _const_spec(shape):
    zeros = (0,) * len(shape)
    return pl.BlockSpec(shape, lambda *_: zeros, pipeline_mode=pl.Buffered(1))


def _tile_copies(hbm, buf, sem, b, row0, slot, to_hbm):
    group = buf.shape[1]
    copies = []
    for r in range(V7X_SUBLANES):
        hbm_rows = hbm.at[b, pl.ds(row0 + group * r, group), :]
        vmem_rows = buf.at[slot, :, r, :]
        src, dst = (vmem_rows, hbm_rows) if to_hbm else (hbm_rows, vmem_rows)
        copies.append(pltpu.make_async_copy(src, dst, sem.at[slot]))
    return copies


def _lru_scan(a, u, h0):
    group = a.shape[0]
    acc_a = [a[0]]
    acc_u = [u[0]]
    for g in range(1, group):
        acc_a.append(a[g] * acc_a[-1])
        acc_u.append(a[g] * acc_u[-1] + u[g])
    end_a, end_u = acc_a[-1], acc_u[-1]
    sub = lax.broadcasted_iota(jnp.int32, end_a.shape, 0)
    shift = 1
    while shift < V7X_SUBLANES:
        keep = sub >= shift
        a_sh = pltpu.roll(end_a, shift, axis=0)
        u_sh = pltpu.roll(end_u, shift, axis=0)
        end_u = jnp.where(keep, end_a * u_sh + end_u, end_u)
        end_a = jnp.where(keep, end_a * a_sh, end_a)
        shift *= 2
    h_end = end_a * h0 + end_u
    h_in = jnp.where(sub == 0, h0, pltpu.roll(h_end, 1, axis=0))
    out = [acc_a[g] * h_in + acc_u[g] for g in range(group)]
    return jnp.stack(out, axis=0), h_end[V7X_SUBLANES - 1:V7X_SUBLANES, :]


def _mixer_kernel(x_hbm, mixn_ref, win_hbm, convw_ref, convb_ref, wa_ref, ba_ref, wi_ref, bi_ref,
                  lam_ref, lng_ref, lnb_ref, ws_ref, bsp_ref, wout_hbm, ffn_ref,
                  x1_hbm, hp_hbm,
                  xbuf, z0_ref, z1_ref, x1buf, hpbuf, xsem, x1sem, hpsem, wsm_ref, ztail_ref, hcar_ref,
                  win_ref, wout_ref, wsem,
                  *, nseq):
    j = pl.program_id(0)
    ntile = pl.num_programs(0) - 1
    _, group, _, d = xbuf.shape
    rows = group * V7X_SUBLANES
    half = d // 2
    ta = jnp.minimum(j, ntile - 1)
    tb = jnp.maximum(j - 1, 0)
    s = lax.rem(tb, nseq)
    slot = lax.rem(tb, 2)

    def fetch(t):
        return _tile_copies(x_hbm, xbuf, xsem, lax.div(t, nseq), lax.rem(t, nseq) * rows,
                            lax.rem(t, 3), to_hbm=False)

    def put(t):
        tb_, ts_, sl = lax.div(t, nseq), lax.rem(t, nseq) * rows, lax.rem(t, 2)
        return (_tile_copies(x1_hbm, x1buf, x1sem, tb_, ts_, sl, to_hbm=True)
                + _tile_copies(hp_hbm, hpbuf, hpsem, tb_, ts_, sl, to_hbm=True))

    @pl.when(j == 0)
    def _():
        for c in fetch(0):
            c.start()
        stage = (z0_ref, z1_ref)
        n_in_chunks = win_hbm.shape[0] // rows

        def win_copy(c):
            return pltpu.make_async_copy(win_hbm.at[pl.ds(c * rows, rows), :], stage[c % 2], wsem.at[c % 2])

        win_copy(0).start()
        for c in range(n_in_chunks):
            if c + 1 < n_in_chunks:
                win_copy(c + 1).start()
            win_copy(c).wait()
            plain = 4 * d
            win_ref[c * rows:(c + 1) * rows, :plain] = stage[c % 2][:, :plain].astype(BF16)
            win_ref[c * rows:(c + 1) * rows, plain:] = (0.5 * stage[c % 2][:, plain:]).astype(BF16)
        n_out_chunks = wout_hbm.shape[0] // rows
        out_copies = [pltpu.make_async_copy(wout_hbm.at[pl.ds(c * rows, rows), :],
                                            z0_ref.at[:, c * d:(c + 1) * d], wsem.at[0])
                      for c in range(n_out_chunks)]
        for cp in out_copies:
            cp.start()
        for cp in out_copies:
            cp.wait()
        for c in range(n_out_chunks):
            wout_ref[c * rows:(c + 1) * rows, :] = (0.25 * z0_ref[:, c * d:(c + 1) * d]).astype(BF16)
        z1_ref[...] = jnp.zeros_like(z1_ref)
        i_idx = lax.broadcasted_iota(jnp.int32, (rows, rows), 0)
        j_idx = lax.broadcasted_iota(jnp.int32, (rows, rows), 1)
        t_i = group * lax.rem(i_idx, V7X_SUBLANES) + lax.div(i_idx, V7X_SUBLANES)
        t_j = group * lax.rem(j_idx, V7X_SUBLANES) + lax.div(j_idx, V7X_SUBLANES)
        keep = (t_i >= t_j) & (lax.div(t_i, CHUNK) == lax.div(t_j, CHUNK))
        pick_rows = jnp.where(t_i == j_idx, 1.0, 0.0).astype(BF16)
        pick_cols = jnp.where(i_idx == t_j, 1.0, 0.0).astype(BF16)
        reps = rows // CHUNK
        for g in range(SGU_GROUPS):
            w_chunk = ws_ref[g].astype(BF16)
            w_rows = jnp.concatenate([w_chunk] * reps, axis=1)
            w_full = jnp.concatenate([w_rows] * reps, axis=0)
            w_perm = _dot(_dot(pick_rows, w_full).astype(BF16), pick_cols)
            wsm_ref[g] = jnp.where(keep, w_perm, 0.0).astype(BF16)

    @pl.when(j + 1 < ntile)
    def _():
        for c in fetch(j + 1):
            c.start()

    @pl.when(j < ntile)
    def _():
        for c in fetch(j):
            c.wait()

    @pl.when(s == 0)
    def _():
        ztail_ref[...] = jnp.zeros_like(ztail_ref)
        hcar_ref[...] = jnp.zeros_like(hcar_ref)

    def compute(z_w, z_r):
        xa_in = xbuf[lax.rem(ta, 3)].reshape(rows, d)
        h_next = _rmsnorm(xa_in, mixn_ref[...]).astype(BF16)
        pw = d

        def project(k):
            z_w[:, k * pw:(k + 1) * pw] = _dot(h_next, win_ref[:, k * pw:(k + 1) * pw])

        x = xbuf[lax.rem(tb, 3)].reshape(rows, d)

        def sec(k, c0, c1):
            return z_r[:, k * d + c0:k * d + c1]

        def one_plus_tanh_gelu(v):
            c = 0.7978845608028654
            return 1.0 + jnp.tanh(v * (c + (c * 0.044715) * (v * v)))

        cw = 0.5 * convw_ref[...]
        cb_h = 0.5 * convb_ref[...]
        ba_h = 0.5 * ba_ref[...]
        bi_h = 0.5 * bi_ref[...]
        neg_lam = -lam_ref[...]
        softplus = jnp.maximum(neg_lam, 0.0) + jnp.log1p(jnp.exp(-jnp.abs(neg_lam)))
        c_a = (-0.5 * LRU_C) * softplus
        blk = V7X_MXU_DIM
        sub3 = lax.broadcasted_iota(jnp.int32, (CONV_WIDTH - 1, V7X_SUBLANES, blk), 1)
        term_a = []
        for n in range(d // blk):
            if n % 2 == 0:
                project(n // 2)
            c0, c1 = n * blk, (n + 1) * blk
            z3 = sec(0, c0, c1).reshape(group, V7X_SUBLANES, blk)
            tail = z3[group - (CONV_WIDTH - 1):]
            halo = jnp.where(sub3 == 0, pltpu.roll(ztail_ref[:, :, c0:c1], 1, axis=1),
                             pltpu.roll(tail, 1, axis=1))
            ztail_ref[:, :, c0:c1] = tail
            zext = jnp.concatenate([halo, z3], axis=0)
            xa_h = cb_h[:, c0:c1] + cw[CONV_WIDTH - 1:CONV_WIDTH, c0:c1] * z3
            for k in range(1, CONV_WIDTH):
                lo = CONV_WIDTH - 1 - k
                xa_h = xa_h + cw[lo:lo + 1, c0:c1] * zext[lo:lo + group]
            xa2 = xa_h.reshape(rows, blk)
            xa_bf = xa2.astype(BF16)
            th_r = jnp.tanh(_dot(xa_bf, wa_ref[n]) + ba_h[:, c0:c1])
            th_i = jnp.tanh(_dot(xa_bf, wi_ref[n]) + bi_h[:, c0:c1])
            a = jnp.exp(c_a[:, c0:c1] + c_a[:, c0:c1] * th_r)
            u = jnp.sqrt(1.0 - a * a) * ((1.0 + th_i) * xa2)
            hseq, hlast = _lru_scan(a.reshape(group, V7X_SUBLANES, blk),
                                    u.reshape(group, V7X_SUBLANES, blk), hcar_ref[:, c0:c1])
            hcar_ref[:, c0:c1] = hlast
            zg = sec(1, c0, c1)
            term_a.append(((1.0 + jnp.tanh(sec(4, c0, c1))) * one_plus_tanh_gelu(zg))
                          * (zg * hseq.reshape(rows, blk)))

        project(2)
        zv = sec(3, 0, d)
        gv2 = zv * one_plus_tanh_gelu(zv)
        mu = jnp.mean(gv2, axis=-1, keepdims=True)
        xc = gv2 - mu
        var = jnp.mean(xc * xc, axis=-1, keepdims=True)
        v_bf = (xc * lax.rsqrt(var + 4.0 * EPS) * lng_ref[...] + lnb_ref[...]).astype(BF16)
        project(3)
        gdim = d // SGU_GROUPS
        term_b = []
        for g in range(SGU_GROUPS):
            c0, c1 = g * gdim, (g + 1) * gdim
            if g in (2, 5):
                project({2: 4, 5: 5}[g])
            sp = _dot(wsm_ref[g], v_bf[:, c0:c1]) + bsp_ref[:, g:g + 1]
            zu = sec(2, c0, c1)
            term_b.append(((1.0 + jnp.tanh(sec(5, c0, c1))) * one_plus_tanh_gelu(zu)) * (zu * sp))
        merged4 = jnp.concatenate(term_a, axis=1) + jnp.concatenate(term_b, axis=1)

        x1 = x + _dot(merged4.astype(BF16), wout_ref[...])

        hn = _rmsnorm(x1, ffn_ref[...])
        hp = _pack_bf16_pair(hn[:, :half], hn[:, half:])

        @pl.when(j >= 3)
        def _():
            for c in put(tb - 2):
                c.wait()

        x1buf[slot] = x1.reshape(group, V7X_SUBLANES, d)
        hpbuf[slot] = hp.reshape(group, V7X_SUBLANES, half)

        @pl.when(j >= 1)
        def _():
            for c in put(tb):
                c.start()

    @pl.when(lax.rem(j, 2) == 0)
    def _():
        compute(z0_ref, z1_ref)

    @pl.when(lax.rem(j, 2) == 1)
    def _():
        compute(z1_ref, z0_ref)

    @pl.when(j == ntile)
    def _():
        for c in put(tb):
            c.wait()

        @pl.when(ntile >= 2)
        def _():
            for c in put(tb - 1):
                c.wait()


def _mixer_call(x, mix_norm, w_in, conv_w, conv_b, wa_blk, ba, wi_blk, bi, lam, ln_g, ln_b, ws,
                bs_tile, w_out, ffn_norm):
    cfg = _tiles()
    bsz, seq, d = x.shape
    ts = cfg["mixer_rows"]
    group = ts // V7X_SUBLANES
    nseq = seq // ts
    ntile = bsz * nseq
    row1 = (1, d)
    in_specs = [
        pl.BlockSpec(memory_space=pl.ANY),
        _const_spec(row1),
        pl.BlockSpec(memory_space=pl.ANY),
        _const_spec(conv_w.shape), _const_spec(row1),
        _const_spec(wa_blk.shape), _const_spec(row1),
        _const_spec(wi_blk.shape), _const_spec(row1),
        _const_spec(row1),
        _const_spec(row1), _const_spec(row1),
        _const_spec(ws.shape), _const_spec(bs_tile.shape),
        pl.BlockSpec(memory_space=pl.ANY), _const_spec(row1),
    ]
    out_shape = [
        jax.ShapeDtypeStruct((bsz, seq, d), F32),
        jax.ShapeDtypeStruct((bsz, seq, d // 2), U32),
    ]
    out_specs = [
        pl.BlockSpec(memory_space=pl.ANY),
        pl.BlockSpec(memory_space=pl.ANY),
    ]
    scratch = [
        pltpu.VMEM((3, group, V7X_SUBLANES, d), F32),
        pltpu.VMEM((ts, w_in.shape[1]), F32),
        pltpu.VMEM((ts, w_in.shape[1]), F32),
        pltpu.VMEM((2, group, V7X_SUBLANES, d), F32),
        pltpu.VMEM((2, group, V7X_SUBLANES, d // 2), U32),
        pltpu.SemaphoreType.DMA((3,)),
        pltpu.SemaphoreType.DMA((2,)),
        pltpu.SemaphoreType.DMA((2,)),
        pltpu.VMEM((SGU_GROUPS, ts, ts), BF16),
        pltpu.VMEM((CONV_WIDTH - 1, V7X_SUBLANES, d), F32),
        pltpu.VMEM((1, d), F32),
        pltpu.VMEM(w_in.shape, BF16),
        pltpu.VMEM(w_out.shape, BF16),
        pltpu.SemaphoreType.DMA((2,)),
    ]
    return pl.pallas_call(
        functools.partial(_mixer_kernel, nseq=nseq),
        grid=(ntile + 1,),
        in_specs=in_specs,
        out_specs=out_specs,
        out_shape=out_shape,
        scratch_shapes=scratch,
        compiler_params=pltpu.CompilerParams(
            dimension_semantics=("arbitrary",),
            vmem_limit_bytes=cfg["mixer_vmem"]),
        name="mixer",
    )(x, mix_norm, w_in, conv_w, conv_b, wa_blk, ba, wi_blk, bi, lam, ln_g, ln_b, ws, bs_tile,
      w_out, ffn_norm)


def _router_kernel(hp_ref, wg_ref, we_ref, br_ref, pos_ref, gate_ref, cnt_ref, ccar_ref, wr_ref,
                   *, expert_capacity):
    rows = hp_ref.shape[0]

    @pl.when(pl.program_id(0) == 0)
    def _():
        ccar_ref[...] = jnp.zeros_like(ccar_ref)
        wr_ref[...] = jnp.zeros_like(wr_ref)
        wr_ref[:, 0:N_GROUPS] = wg_ref[...].astype(BF16)
        wr_ref[:, EXPERT_ROW0:EXPERT_ROW0 + N_EXPERTS] = we_ref[...].astype(BF16)

    sub_rows = rows // ROUTER_SUBBLOCKS
    lts = []
    for q in range(ROUTER_SUBBLOCKS):
        lo, hi = _unpack_bf16_pair(hp_ref[q * sub_rows:(q + 1) * sub_rows, :])
        hn = jnp.concatenate([lo, hi], axis=1)
        logits = _dot(hn.astype(BF16), wr_ref[...])
        lts.append(jnp.transpose(logits) + br_ref[...])
    sub = lax.broadcasted_iota(jnp.int32, (V7X_SUBLANES, sub_rows), 0)
    subf = sub.astype(F32)
    big = jnp.float32(1e9)
    eid = lax.broadcasted_iota(jnp.int32, (N_EXPERTS, sub_rows), 0).astype(F32)
    sb = V7X_MXU_DIM
    before = (lax.broadcasted_iota(jnp.int32, (sb, sb), 0)
              < lax.broadcasted_iota(jnp.int32, (sb, sb), 1))
    before = jnp.where(before, 1.0, 0.0).astype(BF16)
    cap = float(expert_capacity)
    zero = jnp.zeros((V7X_SUBLANES - TOP_K, sub_rows), F32)
    running = ccar_ref[:, 0:1]
    for q, lt in enumerate(lts):
        lg = jnp.where(sub < N_GROUPS, lt[0:V7X_SUBLANES, :], -jnp.inf)
        g_exp = jnp.exp(lg - jnp.max(lg, axis=0, keepdims=True))
        g_prob = g_exp / jnp.sum(g_exp, axis=0, keepdims=True)
        g_top = jnp.max(g_prob, axis=0, keepdims=True)
        g_idx = jnp.min(jnp.where(g_prob == g_top, subf, big), axis=0, keepdims=True)

        e_sel = jnp.zeros((EXPERTS_PER_GROUP, sub_rows), F32)
        for g in range(N_GROUPS):
            r0 = EXPERT_ROW0 + g * EXPERTS_PER_GROUP
            e_sel = jnp.where(g_idx == g, lt[r0:r0 + EXPERTS_PER_GROUP, :], e_sel)
        e_exp = jnp.exp(e_sel - jnp.max(e_sel, axis=0, keepdims=True))
        e_prob = e_exp / jnp.sum(e_exp, axis=0, keepdims=True)
        p1 = jnp.max(e_prob, axis=0, keepdims=True)
        i1 = jnp.min(jnp.where(e_prob == p1, subf, big), axis=0, keepdims=True)
        rest = jnp.where(subf == i1, -1.0, e_prob)
        p2 = jnp.max(rest, axis=0, keepdims=True)
        i2 = jnp.min(jnp.where(rest == p2, subf, big), axis=0, keepdims=True)
        psum = p1 + p2
        gate1 = g_top * (p1 / psum)
        gate2 = g_top * (p2 / psum)
        gid1 = g_idx * EXPERTS_PER_GROUP + i1
        gid2 = g_idx * EXPERTS_PER_GROUP + i2

        hit1 = eid == gid1
        hit2 = eid == gid2
        cnt = jnp.where(hit1 | hit2, 1.0, 0.0)
        base = []
        for c in range(sub_rows // sb):
            part = cnt[:, c * sb:(c + 1) * sb]
            base.append(running + _dot(part.astype(BF16), before))
            running = running + jnp.sum(part, axis=1, keepdims=True)
        base = jnp.concatenate(base, axis=1)
        rank1 = jnp.sum(jnp.where(hit1, base, 0.0), axis=0, keepdims=True)
        rank2 = jnp.sum(jnp.where(hit2, base, 0.0), axis=0, keepdims=True)
        pos = jnp.concatenate([gid1 * cap + rank1, gid2 * cap + rank2, zero], axis=0)
        pos_ref[:, q * sub_rows:(q + 1) * sub_rows] = pos.astype(jnp.int32)
        gate_ref[q * sub_rows:(q + 1) * sub_rows, :] = jnp.transpose(
            jnp.concatenate([gate1, gate2, zero], axis=0))
    total = jnp.broadcast_to(running, ccar_ref.shape)
    ccar_ref[...] = total
    cnt_ref[...] = total


def _router_call(hp, w_group, w_expert, b_router):
    cfg = _tiles()
    ntok, half = hp.shape
    tr = cfg["router_rows"]
    return pl.pallas_call(
        functools.partial(_router_kernel, expert_capacity=ntok),
        grid=(ntok // tr,),
        in_specs=[
            pl.BlockSpec((tr, half), lambda i: (i, 0)),
            _const_spec(w_group.shape),
            _const_spec(w_expert.shape),
            _const_spec(b_router.shape),
        ],
        out_specs=[
            pl.BlockSpec((V7X_SUBLANES, tr), lambda i: (0, i)),
            pl.BlockSpec((tr, V7X_SUBLANES), lambda i: (i, 0)),
            pl.BlockSpec((N_EXPERTS, V7X_LANES), lambda i: (0, 0)),
        ],
        out_shape=[
            jax.ShapeDtypeStruct((V7X_SUBLANES, ntok), jnp.int32),
            jax.ShapeDtypeStruct((ntok, V7X_SUBLANES), F32),
            jax.ShapeDtypeStruct((N_EXPERTS, V7X_LANES), F32),
        ],
        scratch_shapes=[
            pltpu.VMEM((N_EXPERTS, V7X_LANES), F32),
            pltpu.VMEM((w_group.shape[0], ROUTER_ROWS), BF16),
        ],
        compiler_params=pltpu.CompilerParams(
            dimension_semantics=("arbitrary",),
            vmem_limit_bytes=cfg["router_vmem"]),
        name="router",
    )(hp, w_group, w_expert, b_router)


def _expert_kernel(nt_ref, base_ref, texp_ref, tloc_ref, hs_hbm, w1_ref, w3_ref, w2_ref, ys_hbm,
                   hbuf, ybuf, hsem, ysem, w1b_ref, w3b_ref, w2b_ref, *, capacity):
    e = pl.program_id(0)
    n_exp = pl.num_programs(0)
    nt = nt_ref[e]
    base = base_ref[e]
    total = base_ref[n_exp - 1] + nt_ref[n_exp - 1]
    n_in, tm, _ = hbuf.shape
    n_out = ybuf.shape[0]
    ahead = n_in - 1

    def load(g):
        slot = lax.rem(g, n_in)
        rows = pl.ds(texp_ref[g] * capacity + tloc_ref[g] * tm, tm)
        return pltpu.make_async_copy(hs_hbm.at[rows], hbuf.at[slot], hsem.at[slot])

    def store(t, slot):
        rows = pl.ds(e * capacity + t * tm, tm)
        return pltpu.make_async_copy(ybuf.at[slot], ys_hbm.at[rows], ysem.at[slot])

    @pl.when(e == 0)
    def _():
        for g0 in range(ahead):
            @pl.when(g0 < total)
            def _():
                load(g0).start()

    w1b_ref[...] = w1_ref[...].astype(BF16)
    w3b_ref[...] = w3_ref[...].astype(BF16)
    w2b_ref[...] = w2_ref[...].astype(BF16)

    @pl.loop(0, nt)
    def _(t):
        g = base + t

        @pl.when(g + ahead < total)
        def _():
            load(g + ahead).start()

        load(g).wait()
        slot = lax.rem(g, n_out)

        @pl.when(g >= n_out)
        def _():
            store(t, slot).wait()

        sub_rows = tm // EXPERT_SUBBLOCKS
        blocks = [pl.ds(q * sub_rows, sub_rows) for q in range(EXPERT_SUBBLOCKS)]
        rows_in = []
        for rs in blocks:
            lo, hi = _unpack_bf16_pair(hbuf[lax.rem(g, n_in), rs, :])
            rows_in.append(jnp.concatenate([lo, hi], axis=1).astype(BF16))
        up = [(_dot(h, w1b_ref[...]), _dot(h, w3b_ref[...])) for h in rows_in]
        down = []
        for a, b in up:
            hid = (a * _sigmoid(a)) * b
            down.append(_dot(hid.astype(BF16), w2b_ref[...]))
        for rs, y in zip(blocks, down):
            half = y.shape[1] // 2
            ybuf[slot, rs, :] = _pack_bf16_pair(y[:, :half], y[:, half:])
        store(t, slot).start()

    @pl.when(e + 1 == n_exp)
    def _():
        for back in range(1, n_out + 1):
            @pl.when(total >= back)
            def _():
                store(0, lax.rem(total - back, n_out)).wait()


def _expert_call(tiles_per_expert, hs, w1, w3, w2, capacity):
    cfg = _tiles()
    tm = cfg["expert_rows"]
    prow, half = hs.shape
    n_exp, d, f = w1.shape
    ends = jnp.cumsum(tiles_per_expert)
    base = ends - tiles_per_expert
    g = jnp.arange(capacity * TOP_K // tm + n_exp, dtype=jnp.int32)
    texp = jnp.minimum(jnp.sum((ends[None, :] <= g[:, None]).astype(jnp.int32), axis=1), n_exp - 1)
    onehot = texp[:, None] == jnp.arange(n_exp, dtype=jnp.int32)[None, :]
    tloc = g - jnp.sum(jnp.where(onehot, base[None, :], 0), axis=1)

    def w_map(e, *_):
        return (e, 0, 0)

    grid_spec = pltpu.PrefetchScalarGridSpec(
        num_scalar_prefetch=4,
        grid=(n_exp,),
        in_specs=[
            pl.BlockSpec(memory_space=pl.ANY),
            pl.BlockSpec((None, d, f), w_map),
            pl.BlockSpec((None, d, f), w_map),
            pl.BlockSpec((None, f, d), w_map),
        ],
        out_specs=pl.BlockSpec(memory_space=pl.ANY),
        scratch_shapes=[
            pltpu.VMEM((EXPERT_LOOKAHEAD + 1, tm, half), U32),
            pltpu.VMEM((2, tm, half), U32),
            pltpu.SemaphoreType.DMA((EXPERT_LOOKAHEAD + 1,)),
            pltpu.SemaphoreType.DMA((2,)),
            pltpu.VMEM((d, f), BF16),
            pltpu.VMEM((d, f), BF16),
            pltpu.VMEM((f, d), BF16),
        ],
    )
    return pl.pallas_call(
        functools.partial(_expert_kernel, capacity=capacity),
        grid_spec=grid_spec,
        out_shape=jax.ShapeDtypeStruct((prow, half), U32),
        compiler_params=pltpu.CompilerParams(
            dimension_semantics=("arbitrary",),
            vmem_limit_bytes=cfg["expert_vmem"]),
        name="experts",
    )(tiles_per_expert, base, texp, tloc, hs, w1, w3, w2)


def _sc_mesh():
    return plsc.VectorSubcoreMesh(core_axis_name="c", subcore_axis_name="s",
                                  num_cores=V7X_SC_CORES, num_subcores=V7X_SC_SUBCORES)


def _sc_worker_id():
    return lax.axis_index("s") * V7X_SC_CORES + lax.axis_index("c")


def _dispatch_call(hp, pos_w, out_rows):
    cfg = _tiles()
    ntok, half = hp.shape
    nw, topk, nch, ch = pos_w.shape
    per_w = nch * ch

    def body(hp_hbm, pos_hbm, hs_hbm, idx_v, buf, wsem):
        wid = _sc_worker_id()
        pltpu.sync_copy(pos_hbm.at[wid], idx_v)

        for c in range(nch):
            pltpu.sync_copy(hp_hbm.at[pl.ds(wid * per_w + c * ch, ch)], buf)
            writes = [pltpu.make_async_copy(buf, hs_hbm.at[idx_v.at[k, c]], wsem.at[k]) for k in range(topk)]
            for w in writes:
                w.start()
            for w in writes:
                w.wait()

    assert nw == V7X_SC_CORES * V7X_SC_SUBCORES and nw * per_w == ntok and ch == cfg["sc_rows"]
    return pl.kernel(
        body,
        out_type=jax.ShapeDtypeStruct((out_rows, half), U32),
        mesh=_sc_mesh(),
        scratch_types=[
            pltpu.VMEM((topk, nch, ch), jnp.int32),
            pltpu.VMEM((ch, half), U32),
            pltpu.SemaphoreType.DMA((topk,)),
        ],
        name="dispatch",
    )(hp, pos_w)


def _combine_call(ys, pos_w):
    cfg = _tiles()
    _, half = ys.shape
    nw, topk, nch, ch = pos_w.shape
    per_w = nch * ch
    ntok = nw * per_w

    def body(ys_hbm, pos_hbm, *rest):
        outs = rest[:topk]
        idx_v, buf = rest[topk:]
        wid = _sc_worker_id()
        pltpu.sync_copy(pos_hbm.at[wid], idx_v)
        for c in range(nch):
            for k in range(topk):
                pltpu.sync_copy(ys_hbm.at[idx_v.at[k, c]], buf)
                pltpu.sync_copy(buf, outs[k].at[pl.ds(wid * per_w + c * ch, ch)])

    assert nw == V7X_SC_CORES * V7X_SC_SUBCORES and ch == cfg["sc_rows"]
    return pl.kernel(
        body,
        out_type=[jax.ShapeDtypeStruct((ntok, half), U32)] * topk,
        mesh=_sc_mesh(),
        scratch_types=[
            pltpu.VMEM((topk, nch, ch), jnp.int32),
            pltpu.VMEM((ch, half), U32),
        ],
        name="combine",
    )(ys, pos_w)


def _ple_kernel(x1_ref, yg0_ref, yg1_ref, gate_ref, p_ref, plen_ref, wg32_ref, wu32_ref, fin_ref, o_ref,
                wg_ref, wu_ref):
    @pl.when(pl.program_id(0) == 0)
    def _():
        wg_ref[...] = (0.5 * wg32_ref[...]).astype(BF16)
        wu_ref[...] = (0.5 * wu32_ref[...]).astype(BF16)

    rows = x1_ref.shape[0]
    sub_rows = rows // PLE_SUBBLOCKS
    for q in range(PLE_SUBBLOCKS):
        rs = pl.ds(q * sub_rows, sub_rows)
        lo0, hi0 = _unpack_bf16_pair(yg0_ref[rs, :])
        lo1, hi1 = _unpack_bf16_pair(yg1_ref[rs, :])
        g0 = gate_ref[rs, 0:1]
        g1 = gate_ref[rs, 1:2]
        moe = g0 * jnp.concatenate([lo0, hi0], axis=1) + g1 * jnp.concatenate([lo1, hi1], axis=1)
        x2 = x1_ref[rs, :] + moe
        r = _rmsnorm(x2, plen_ref[...]).astype(BF16)
        gt2 = 1.0 + jnp.tanh(_dot(r, wg_ref[...]))
        up_h = _dot(p_ref[rs, :].astype(BF16), wu_ref[...])
        x3 = x2 + gt2 * up_h
        o_ref[rs, :] = _rmsnorm(x3, fin_ref[...])


def _ple_call(x1, yg0, yg1, gates, p, ple_norm, wg, wu, final_norm):
    cfg = _tiles()
    ntok, d = x1.shape
    tp = cfg["ple_rows"]
    pdim = p.shape[1]
    return pl.pallas_call(
        _ple_kernel,
        grid=(ntok // tp,),
        in_specs=[
            pl.BlockSpec((tp, d), lambda i: (i, 0)),
            pl.BlockSpec((tp, d // 2), lambda i: (i, 0)),
            pl.BlockSpec((tp, d // 2), lambda i: (i, 0)),
            pl.BlockSpec((tp, V7X_SUBLANES), lambda i: (i, 0)),
            pl.BlockSpec((tp, pdim), lambda i: (i, 0)),
            _const_spec((1, d)),
            _const_spec(wg.shape),
            _const_spec(wu.shape),
            _const_spec((1, d)),
        ],
        out_specs=pl.BlockSpec((tp, d), lambda i: (i, 0)),
        out_shape=jax.ShapeDtypeStruct((ntok, d), F32),
        scratch_shapes=[pltpu.VMEM(wg.shape, BF16), pltpu.VMEM(wu.shape, BF16)],
        compiler_params=pltpu.CompilerParams(
            dimension_semantics=("arbitrary",),
            vmem_limit_bytes=cfg["ple_vmem"]),
        name="ple",
    )(x1, yg0, yg1, gates, p, ple_norm, wg, wu, final_norm)


def _blockdiag_pack(w):
    nb, bd, _ = w.shape
    per = V7X_MXU_DIM // bd
    w4 = w.reshape(nb // per, per, bd, bd)
    eye = jnp.eye(per, dtype=w.dtype)
    out = jnp.einsum("jpab,pq->jpaqb", w4, eye)
    return out.reshape(nb // per, V7X_MXU_DIM, V7X_MXU_DIM).astype(BF16)


def kernel(x, p, mix_norm, w_in, conv_w, conv_b, lru_wa, lru_ba, lru_wi, lru_bi, lru_lambda, sgu_ln_g, sgu_ln_b, sgu_ws, sgu_bs, w_out, ffn_norm, router_group_w, router_group_b, router_expert_w, router_expert_b, expert_w1, expert_w3, expert_w2, ple_norm, ple_gate_w, ple_up_w, final_norm):
    cfg = _tiles()
    bsz, seq, d = x.shape
    ntok = bsz * seq
    tm = cfg["expert_rows"]
    depth = w_in.shape[0]
    assert depth == 1, "the ple kernel applies the final norm, so it must be the last layer"
    l = 0
    nw_rows = V7X_SC_CORES * V7X_SC_SUBCORES * cfg["sc_rows"]
    assert cfg["mixer_rows"] % CHUNK == 0 and seq % cfg["mixer_rows"] == 0
    assert ntok % cfg["router_rows"] == 0 and ntok % cfg["ple_rows"] == 0 and ntok % nw_rows == 0
    assert ntok % tm == 0 and lru_wa.shape[1:] == (LRU_BLOCKS, d // LRU_BLOCKS, d // LRU_BLOCKS)
    assert max(cfg[k] for k in cfg if k.endswith("_vmem")) < V7X_VMEM_BYTES
    b_router = jnp.concatenate([
        router_group_b[l], jnp.zeros((EXPERT_ROW0 - N_GROUPS,), F32), router_expert_b[l],
        jnp.zeros((ROUTER_ROWS - EXPERT_ROW0 - N_EXPERTS,), F32)])[:, None]
    ts = cfg["mixer_rows"]
    group = ts // V7X_SUBLANES
    bs_tile = jnp.tile(sgu_bs[l], (1, ts // CHUNK)).reshape(SGU_GROUPS, V7X_SUBLANES, group)
    bs_tile = jnp.transpose(bs_tile, (2, 1, 0)).reshape(ts, SGU_GROUPS)
    x1, hp = _mixer_call(
        x, mix_norm[l][None], w_in[l], conv_w[l], conv_b[l][None],
        _blockdiag_pack(lru_wa[l]), lru_ba[l][None], _blockdiag_pack(lru_wi[l]), lru_bi[l][None],
        lru_lambda[l][None], sgu_ln_g[l][None], sgu_ln_b[l][None], sgu_ws[l], bs_tile,
        w_out[l], ffn_norm[l][None])
    hp = hp.reshape(ntok, d // 2)
    pos, gate, cnt = _router_call(hp, router_group_w[l], router_expert_w[l], b_router)

    cap = ntok
    tiles_per_expert = (cnt[:, 0].astype(jnp.int32) + tm - 1) // tm
    nw = V7X_SC_CORES * V7X_SC_SUBCORES
    ch = cfg["sc_rows"]
    pos_w = jnp.transpose(pos[:TOP_K].reshape(TOP_K, nw, ntok // (nw * ch), ch), (1, 0, 2, 3))

    hs = _dispatch_call(hp, pos_w, N_EXPERTS * cap)
    ys = _expert_call(tiles_per_expert, hs, expert_w1[l], expert_w3[l], expert_w2[l], cap)
    yg0, yg1 = _combine_call(ys, pos_w)

    out = _ple_call(x1.reshape(ntok, d), yg0, yg1, gate, p[l].reshape(ntok, -1), ple_norm[l][None],
                    ple_gate_w[l], ple_up_w[l], final_norm[None])
    return out.reshape(bsz, seq, d)
```

```python
import functools

import jax
import jax.numpy as jnp
from jax import lax
from jax.experimental import pallas as pl
from jax.experimental.pallas import tpu as pltpu
from jax.experimental.pallas import tpu_sc as plsc

F32 = jnp.float32
BF16 = jnp.bfloat16
U32 = jnp.uint32

LRU_BLOCKS = 16
CONV_WIDTH = 4
LRU_C = 8.0
SGU_GROUPS = 8
CHUNK = 128
N_GROUPS = 4
EXPERTS_PER_GROUP = 8
N_EXPERTS = N_GROUPS * EXPERTS_PER_GROUP
TOP_K = 2
EPS = 1e-6

V7X_MXU_DIM = 256
V7X_SUBLANES = 8
V7X_LANES = 128
V7X_VMEM_BYTES = 64 * 1024 * 1024
V7X_SC_CORES = 2
V7X_SC_SUBCORES = 16

EXPERT_LOOKAHEAD = 3
ROUTER_SUBBLOCKS = 8
EXPERT_SUBBLOCKS = 2
PLE_SUBBLOCKS = 4
ROUTER_ROWS = V7X_LANES
EXPERT_ROW0 = V7X_SUBLANES


def _tiles():
    return dict(
        mixer_rows=256,
        expert_rows=512,
        ple_rows=1024,
        sc_rows=128,
        router_rows=4096,
        mixer_vmem=52 * 1024 * 1024,
        expert_vmem=40 * 1024 * 1024,
        ple_vmem=48 * 1024 * 1024,
        router_vmem=40 * 1024 * 1024,
    )


def _dot(a, b):
    return jnp.dot(a, b, preferred_element_type=F32)


def _sigmoid(x):
    return 0.5 * jnp.tanh(0.5 * x) + 0.5


def _rmsnorm(x, g):
    ms = jnp.mean(x * x, axis=-1, keepdims=True)
    return x * lax.rsqrt(ms + EPS) * g


def _pack_bf16_pair(lo, hi):
    lo_b = lax.bitcast_convert_type(lo.astype(BF16).astype(F32), U32)
    hi_b = lax.bitcast_convert_type(hi.astype(BF16).astype(F32), U32)
    return (hi_b & jnp.uint32(0xFFFF0000)) | lax.shift_right_logical(lo_b, jnp.uint32(16))


def _unpack_bf16_pair(w):
    lo = lax.bitcast_convert_type(lax.shift_left(w, jnp.uint32(16)), F32)
    hi = lax.bitcast_convert_type(w & jnp.uint32(0xFFFF0000), F32)
    return lo, hi


def _const_spec(shape):
    zeros = (0,) * len(shape)
    return pl.BlockSpec(shape, lambda *_: zeros, pipeline_mode=pl.Buffered(1))


def _tile_copies(hbm, buf, sem, b, row0, slot, to_hbm):
    group = buf.shape[1]
    copies = []
    for r in range(V7X_SUBLANES):
        hbm_rows = hbm.at[b, pl.ds(row0 + group * r, group), :]
        vmem_rows = buf.at[slot, :, r, :]
        src, dst = (vmem_rows, hbm_rows) if to_hbm else (hbm_rows, vmem_rows)
        copies.append(pltpu.make_async_copy(src, dst, sem.at[slot]))
    return copies


def _lru_scan(a, u, h0):
    group = a.shape[0]
    acc_a = [a[0]]
    acc_u = [u[0]]
    for g in range(1, group):
        acc_a.append(a[g] * acc_a[-1])
        acc_u.append(a[g] * acc_u[-1] + u[g])
    end_a, end_u = acc_a[-1], acc_u[-1]
    sub = lax.broadcasted_iota(jnp.int32, end_a.shape, 0)
    shift = 1
    while shift < V7X_SUBLANES:
        keep = sub >= shift
        a_sh = pltpu.roll(end_a, shift, axis=0)
        u_sh = pltpu.roll(end_u, shift, axis=0)
        end_u = jnp.where(keep, end_a * u_sh + end_u, end_u)
        end_a = jnp.where(keep, end_a * a_sh, end_a)
        shift *= 2
    h_end = end_a * h0 + end_u
    h_in = jnp.where(sub == 0, h0, pltpu.roll(h_end, 1, axis=0))
    out = [acc_a[g] * h_in + acc_u[g] for g in range(group)]
    return jnp.stack(out, axis=0), h_end[V7X_SUBLANES - 1:V7X_SUBLANES, :]


def _mixer_kernel(x_hbm, mixn_ref, win_hbm, convw_ref, convb_ref, wa_ref, ba_ref, wi_ref, bi_ref,
                  lam_ref, lng_ref, lnb_ref, ws_ref, bsp_ref, wout_hbm, ffn_ref,
                  x1_hbm, hp_hbm,
                  xbuf, z0_ref, z1_ref, x1buf, hpbuf, xsem, x1sem, hpsem, wsm_ref, ztail_ref, hcar_ref,
                  win_ref, wout_ref, wsem,
                  *, nseq):
    j = pl.program_id(0)
    ntile = pl.num_programs(0) - 1
    _, group, _, d = xbuf.shape
    rows = group * V7X_SUBLANES
    half = d // 2
    ta = jnp.minimum(j, ntile - 1)
    tb = jnp.maximum(j - 1, 0)
    s = lax.rem(tb, nseq)
    slot = lax.rem(tb, 2)

    def fetch(t):
        return _tile_copies(x_hbm, xbuf, xsem, lax.div(t, nseq), lax.rem(t, nseq) * rows,
                            lax.rem(t, 3), to_hbm=False)

    def put(t):
        tb_, ts_, sl = lax.div(t, nseq), lax.rem(t, nseq) * rows, lax.rem(t, 2)
        return (_tile_copies(x1_hbm, x1buf, x1sem, tb_, ts_, sl, to_hbm=True)
                + _tile_copies(hp_hbm, hpbuf, hpsem, tb_, ts_, sl, to_hbm=True))

    @pl.when(j == 0)
    def _():
        for i, c in enumerate(fetch(0)):
            c.start(priority=i % 2)
        stage = (z0_ref, z1_ref)
        n_in_chunks = win_hbm.shape[0] // rows

        def win_copy(c):
            return pltpu.make_async_copy(win_hbm.at[pl.ds(c * rows, rows), :], stage[c % 2], wsem.at[c % 2])

        win_copy(0).start()
        for c in range(n_in_chunks):
            if c + 1 < n_in_chunks:
                win_copy(c + 1).start()
            win_copy(c).wait()
            plain = 4 * d
            win_ref[c * rows:(c + 1) * rows, :plain] = stage[c % 2][:, :plain].astype(BF16)
            win_ref[c * rows:(c + 1) * rows, plain:] = (0.5 * stage[c % 2][:, plain:]).astype(BF16)
        n_out_chunks = wout_hbm.shape[0] // rows
        out_copies = [pltpu.make_async_copy(wout_hbm.at[pl.ds(c * rows, rows), :],
                                            z0_ref.at[:, c * d:(c + 1) * d], wsem.at[0])
                      for c in range(n_out_chunks)]
        for cp in out_copies:
            cp.start()
        for cp in out_copies:
            cp.wait()
        for c in range(n_out_chunks):
            wout_ref[c * rows:(c + 1) * rows, :] = (0.25 * z0_ref[:, c * d:(c + 1) * d]).astype(BF16)
        z1_ref[...] = jnp.zeros_like(z1_ref)
        i_idx = lax.broadcasted_iota(jnp.int32, (rows, rows), 0)
        j_idx = lax.broadcasted_iota(jnp.int32, (rows, rows), 1)
        t_i = group * lax.rem(i_idx, V7X_SUBLANES) + lax.div(i_idx, V7X_SUBLANES)
        t_j = group * lax.rem(j_idx, V7X_SUBLANES) + lax.div(j_idx, V7X_SUBLANES)
        keep = (t_i >= t_j) & (lax.div(t_i, CHUNK) == lax.div(t_j, CHUNK))
        pick_rows = jnp.where(t_i == j_idx, 1.0, 0.0).astype(BF16)
        pick_cols = jnp.where(i_idx == t_j, 1.0, 0.0).astype(BF16)
        reps = rows // CHUNK
        for g in range(SGU_GROUPS):
            w_chunk = ws_ref[g].astype(BF16)
            w_rows = jnp.concatenate([w_chunk] * reps, axis=1)
            w_full = jnp.concatenate([w_rows] * reps, axis=0)
            w_perm = _dot(_dot(pick_rows, w_full).astype(BF16), pick_cols)
            wsm_ref[g] = jnp.where(keep, w_perm, 0.0).astype(BF16)

    @pl.when(j + 1 < ntile)
    def _():
        for i, c in enumerate(fetch(j + 1)):
            c.start(priority=i % 2)

    @pl.when(j < ntile)
    def _():
        for c in fetch(j):
            c.wait()

    @pl.when(s == 0)
    def _():
        ztail_ref[...] = jnp.zeros_like(ztail_ref)
        hcar_ref[...] = jnp.zeros_like(hcar_ref)

    def compute(z_w, z_r):
        xa_in = xbuf[lax.rem(ta, 3)].reshape(rows, d)
        h_next = _rmsnorm(xa_in, mixn_ref[...]).astype(BF16)
        pw = d

        def project(k):
            z_w[:, k * pw:(k + 1) * pw] = _dot(h_next, win_ref[:, k * pw:(k + 1) * pw])

        x = xbuf[lax.rem(tb, 3)].reshape(rows, d)

        def sec(k, c0, c1):
            return z_r[:, k * d + c0:k * d + c1]

        def one_plus_tanh_gelu(v):
            c = 0.7978845608028654
            return 1.0 + jnp.tanh(v * (c + (c * 0.044715) * (v * v)))

        cw = 0.5 * convw_ref[...]
        cb_h = 0.5 * convb_ref[...]
        ba_h = 0.5 * ba_ref[...]
        bi_h = 0.5 * bi_ref[...]
        neg_lam = -lam_ref[...]
        softplus = jnp.maximum(neg_lam, 0.0) + jnp.log1p(jnp.exp(-jnp.abs(neg_lam)))
        c_a = (-0.5 * LRU_C) * softplus
        blk = V7X_MXU_DIM
        sub3 = lax.broadcasted_iota(jnp.int32, (CONV_WIDTH - 1, V7X_SUBLANES, blk), 1)
        term_a = []
        for n in range(d // blk):
            if n % 2 == 0:
                project(n // 2)
            c0, c1 = n * blk, (n + 1) * blk
            z3 = sec(0, c0, c1).reshape(group, V7X_SUBLANES, blk)
            tail = z3[group - (CONV_WIDTH - 1):]
            halo = jnp.where(sub3 == 0, pltpu.roll(ztail_ref[:, :, c0:c1], 1, axis=1),
                             pltpu.roll(tail, 1, axis=1))
            ztail_ref[:, :, c0:c1] = tail
            zext = jnp.concatenate([halo, z3], axis=0)
            xa_h = cb_h[:, c0:c1] + cw[CONV_WIDTH - 1:CONV_WIDTH, c0:c1] * z3
            for k in range(1, CONV_WIDTH):
                lo = CONV_WIDTH - 1 - k
                xa_h = xa_h + cw[lo:lo + 1, c0:c1] * zext[lo:lo + group]
            xa2 = xa_h.reshape(rows, blk)
            xa_bf = xa2.astype(BF16)
            th_r = jnp.tanh(_dot(xa_bf, wa_ref[n]) + ba_h[:, c0:c1])
            th_i = jnp.tanh(_dot(xa_bf, wi_ref[n]) + bi_h[:, c0:c1])
            a = jnp.exp(c_a[:, c0:c1] + c_a[:, c0:c1] * th_r)
            u = jnp.sqrt(1.0 - a * a) * ((1.0 + th_i) * xa2)
            hseq, hlast = _lru_scan(a.reshape(group, V7X_SUBLANES, blk),
                                    u.reshape(group, V7X_SUBLANES, blk), hcar_ref[:, c0:c1])
            hcar_ref[:, c0:c1] = hlast
            zg = sec(1, c0, c1)
            term_a.append(((1.0 + jnp.tanh(sec(4, c0, c1))) * one_plus_tanh_gelu(zg))
                          * (zg * hseq.reshape(rows, blk)))

        project(2)
        zv = sec(3, 0, d)
        gv2 = zv * one_plus_tanh_gelu(zv)
        mu = jnp.mean(gv2, axis=-1, keepdims=True)
        xc = gv2 - mu
        var = jnp.mean(xc * xc, axis=-1, keepdims=True)
        v_bf = (xc * lax.rsqrt(var + 4.0 * EPS) * lng_ref[...] + lnb_ref[...]).astype(BF16)
        project(3)
        gdim = d // SGU_GROUPS
        term_b = []
        for g in range(SGU_GROUPS):
            c0, c1 = g * gdim, (g + 1) * gdim
            if g in (2, 5):
                project({2: 4, 5: 5}[g])
            sp = _dot(wsm_ref[g], v_bf[:, c0:c1]) + bsp_ref[:, g:g + 1]
            zu = sec(2, c0, c1)
            term_b.append(((1.0 + jnp.tanh(sec(5, c0, c1))) * one_plus_tanh_gelu(zu)) * (zu * sp))
        merged4 = jnp.concatenate(term_a, axis=1) + jnp.concatenate(term_b, axis=1)

        x1 = x + _dot(merged4.astype(BF16), wout_ref[...])

        hn = _rmsnorm(x1, ffn_ref[...])
        hp = _pack_bf16_pair(hn[:, :half], hn[:, half:])

        @pl.when(j >= 3)
        def _():
            for c in put(tb - 2):
                c.wait()

        x1buf[slot] = x1.reshape(group, V7X_SUBLANES, d)
        hpbuf[slot] = hp.reshape(group, V7X_SUBLANES, half)

        @pl.when(j >= 1)
        def _():
            for i, c in enumerate(put(tb)):
                c.start(priority=i % 2)

    @pl.when(lax.rem(j, 2) == 0)
    def _():
        compute(z0_ref, z1_ref)

    @pl.when(lax.rem(j, 2) == 1)
    def _():
        compute(z1_ref, z0_ref)

    @pl.when(j == ntile)
    def _():
        for c in put(tb):
            c.wait()

        @pl.when(ntile >= 2)
        def _():
            for c in put(tb - 1):
                c.wait()


def _mixer_call(x, mix_norm, w_in, conv_w, conv_b, wa_blk, ba, wi_blk, bi, lam, ln_g, ln_b, ws,
                bs_tile, w_out, ffn_norm):
    cfg = _tiles()
    bsz, seq, d = x.shape
    ts = cfg["mixer_rows"]
    group = ts // V7X_SUBLANES
    nseq = seq // ts
    ntile = bsz * nseq
    row1 = (1, d)
    in_specs = [
        pl.BlockSpec(memory_space=pl.ANY),
        _const_spec(row1),
        pl.BlockSpec(memory_space=pl.ANY),
        _const_spec(conv_w.shape), _const_spec(row1),
        _const_spec(wa_blk.shape), _const_spec(row1),
        _const_spec(wi_blk.shape), _const_spec(row1),
        _const_spec(row1),
        _const_spec(row1), _const_spec(row1),
        _const_spec(ws.shape), _const_spec(bs_tile.shape),
        pl.BlockSpec(memory_space=pl.ANY), _const_spec(row1),
    ]
    out_shape = [
        jax.ShapeDtypeStruct((bsz, seq, d), F32),
        jax.ShapeDtypeStruct((bsz, seq, d // 2), U32),
    ]
    out_specs = [
        pl.BlockSpec(memory_space=pl.ANY),
        pl.BlockSpec(memory_space=pl.ANY),
    ]
    scratch = [
        pltpu.VMEM((3, group, V7X_SUBLANES, d), F32),
        pltpu.VMEM((ts, w_in.shape[1]), F32),
        pltpu.VMEM((ts, w_in.shape[1]), F32),
        pltpu.VMEM((2, group, V7X_SUBLANES, d), F32),
        pltpu.VMEM((2, group, V7X_SUBLANES, d // 2), U32),
        pltpu.SemaphoreType.DMA((3,)),
        pltpu.SemaphoreType.DMA((2,)),
        pltpu.SemaphoreType.DMA((2,)),
        pltpu.VMEM((SGU_GROUPS, ts, ts), BF16),
        pltpu.VMEM((CONV_WIDTH - 1, V7X_SUBLANES, d), F32),
        pltpu.VMEM((1, d), F32),
        pltpu.VMEM(w_in.shape, BF16),
        pltpu.VMEM(w_out.shape, BF16),
        pltpu.SemaphoreType.DMA((2,)),
    ]
    return pl.pallas_call(
        functools.partial(_mixer_kernel, nseq=nseq),
        grid=(ntile + 1,),
        in_specs=in_specs,
        out_specs=out_specs,
        out_shape=out_shape,
        scratch_shapes=scratch,
        compiler_params=pltpu.CompilerParams(
            dimension_semantics=("arbitrary",),
            vmem_limit_bytes=cfg["mixer_vmem"]),
        name="mixer",
    )(x, mix_norm, w_in, conv_w, conv_b, wa_blk, ba, wi_blk, bi, lam, ln_g, ln_b, ws, bs_tile,
      w_out, ffn_norm)


def _router_kernel(hp_ref, wg_ref, we_ref, br_ref, pos_ref, gate_ref, cnt_ref, ccar_ref, wr_ref,
                   *, expert_capacity):
    rows = hp_ref.shape[0]

    @pl.when(pl.program_id(0) == 0)
    def _():
        ccar_ref[...] = jnp.zeros_like(ccar_ref)
        wr_ref[...] = jnp.zeros_like(wr_ref)
        wr_ref[:, 0:N_GROUPS] = wg_ref[...].astype(BF16)
        wr_ref[:, EXPERT_ROW0:EXPERT_ROW0 + N_EXPERTS] = we_ref[...].astype(BF16)

    sub_rows = rows // ROUTER_SUBBLOCKS
    lts = []
    for q in range(ROUTER_SUBBLOCKS):
        lo, hi = _unpack_bf16_pair(hp_ref[q * sub_rows:(q + 1) * sub_rows, :])
        hn = jnp.concatenate([lo, hi], axis=1)
        logits = _dot(hn.astype(BF16), wr_ref[...])
        lts.append(jnp.transpose(logits) + br_ref[...])
    sub = lax.broadcasted_iota(jnp.int32, (V7X_SUBLANES, sub_rows), 0)
    subf = sub.astype(F32)
    big = jnp.float32(1e9)
    eid = lax.broadcasted_iota(jnp.int32, (N_EXPERTS, sub_rows), 0).astype(F32)
    sb = V7X_MXU_DIM
    before = (lax.broadcasted_iota(jnp.int32, (sb, sb), 0)
              < lax.broadcasted_iota(jnp.int32, (sb, sb), 1))
    before = jnp.where(before, 1.0, 0.0).astype(BF16)
    cap = float(expert_capacity)
    zero = jnp.zeros((V7X_SUBLANES - TOP_K, sub_rows), F32)
    running = ccar_ref[:, 0:1]
    for q, lt in enumerate(lts):
        lg = jnp.where(sub < N_GROUPS, lt[0:V7X_SUBLANES, :], -jnp.inf)
        g_exp = jnp.exp(lg - jnp.max(lg, axis=0, keepdims=True))
        g_prob = g_exp / jnp.sum(g_exp, axis=0, keepdims=True)
        g_top = jnp.max(g_prob, axis=0, keepdims=True)
        g_idx = jnp.min(jnp.where(g_prob == g_top, subf, big), axis=0, keepdims=True)

        e_sel = jnp.zeros((EXPERTS_PER_GROUP, sub_rows), F32)
        for g in range(N_GROUPS):
            r0 = EXPERT_ROW0 + g * EXPERTS_PER_GROUP
            e_sel = jnp.where(g_idx == g, lt[r0:r0 + EXPERTS_PER_GROUP, :], e_sel)
        e_exp = jnp.exp(e_sel - jnp.max(e_sel, axis=0, keepdims=True))
        e_prob = e_exp / jnp.sum(e_exp, axis=0, keepdims=True)
        p1 = jnp.max(e_prob, axis=0, keepdims=True)
        i1 = jnp.min(jnp.where(e_prob == p1, subf, big), axis=0, keepdims=True)
        rest = jnp.where(subf == i1, -1.0, e_prob)
        p2 = jnp.max(rest, axis=0, keepdims=True)
        i2 = jnp.min(jnp.where(rest == p2, subf, big), axis=0, keepdims=True)
        psum = p1 + p2
        gate1 = g_top * (p1 / psum)
        gate2 = g_top * (p2 / psum)
        gid1 = g_idx * EXPERTS_PER_GROUP + i1
        gid2 = g_idx * EXPERTS_PER_GROUP + i2

        hit1 = eid == gid1
        hit2 = eid == gid2
        cnt = jnp.where(hit1 | hit2, 1.0, 0.0)
        base = []
        for c in range(sub_rows // sb):
            part = cnt[:, c * sb:(c + 1) * sb]
            base.append(running + _dot(part.astype(BF16), before))
            running = running + jnp.sum(part, axis=1, keepdims=True)
        base = jnp.concatenate(base, axis=1)
        rank1 = jnp.sum(jnp.where(hit1, base, 0.0), axis=0, keepdims=True)
        rank2 = jnp.sum(jnp.where(hit2, base, 0.0), axis=0, keepdims=True)
        pos = jnp.concatenate([gid1 * cap + rank1, gid2 * cap + rank2, zero], axis=0)
        pos_ref[:, q * sub_rows:(q + 1) * sub_rows] = pos.astype(jnp.int32)
        gate_ref[q * sub_rows:(q + 1) * sub_rows, :] = jnp.transpose(
            jnp.concatenate([gate1, gate2, zero], axis=0))
    total = jnp.broadcast_to(running, ccar_ref.shape)
    ccar_ref[...] = total
    cnt_ref[...] = total


def _router_call(hp, w_group, w_expert, b_router):
    cfg = _tiles()
    ntok, half = hp.shape
    tr = cfg["router_rows"]
    return pl.pallas_call(
        functools.partial(_router_kernel, expert_capacity=ntok),
        grid=(ntok // tr,),
        in_specs=[
            pl.BlockSpec((tr, half), lambda i: (i, 0)),
            _const_spec(w_group.shape),
            _const_spec(w_expert.shape),
            _const_spec(b_router.shape),
        ],
        out_specs=[
            pl.BlockSpec((V7X_SUBLANES, tr), lambda i: (0, i)),
            pl.BlockSpec((tr, V7X_SUBLANES), lambda i: (i, 0)),
            pl.BlockSpec((N_EXPERTS, V7X_LANES), lambda i: (0, 0)),
        ],
        out_shape=[
            jax.ShapeDtypeStruct((V7X_SUBLANES, ntok), jnp.int32),
            jax.ShapeDtypeStruct((ntok, V7X_SUBLANES), F32),
            jax.ShapeDtypeStruct((N_EXPERTS, V7X_LANES), F32),
        ],
        scratch_shapes=[
            pltpu.VMEM((N_EXPERTS, V7X_LANES), F32),
            pltpu.VMEM((w_group.shape[0], ROUTER_ROWS), BF16),
        ],
        compiler_params=pltpu.CompilerParams(
            dimension_semantics=("arbitrary",),
            vmem_limit_bytes=cfg["router_vmem"]),
        name="router",
    )(hp, w_group, w_expert, b_router)


def _expert_kernel(nt_ref, base_ref, texp_ref, tloc_ref, hs_hbm, w1_ref, w3_ref, w2_ref, ys_hbm,
                   hbuf, ybuf, hsem, ysem, w1b_ref, w3b_ref, w2b_ref, *, capacity):
    e = pl.program_id(0)
    n_exp = pl.num_programs(0)
    nt = nt_ref[e]
    base = base_ref[e]
    total = base_ref[n_exp - 1] + nt_ref[n_exp - 1]
    n_in, tm, _ = hbuf.shape
    n_out = ybuf.shape[0]
    ahead = n_in - 1

    def load(g):
        slot = lax.rem(g, n_in)
        rows = pl.ds(texp_ref[g] * capacity + tloc_ref[g] * tm, tm)
        return pltpu.make_async_copy(hs_hbm.at[rows], hbuf.at[slot], hsem.at[slot])

    def store(t, slot):
        rows = pl.ds(e * capacity + t * tm, tm)
        return pltpu.make_async_copy(ybuf.at[slot], ys_hbm.at[rows], ysem.at[slot])

    @pl.when(e == 0)
    def _():
        for g0 in range(ahead):
            @pl.when(g0 < total)
            def _():
                load(g0).start(priority=1)

    w1b_ref[...] = w1_ref[...].astype(BF16)
    w3b_ref[...] = w3_ref[...].astype(BF16)
    w2b_ref[...] = w2_ref[...].astype(BF16)

    @pl.loop(0, nt)
    def _(t):
        g = base + t

        @pl.when(g + ahead < total)
        def _():
            load(g + ahead).start(priority=1)

        load(g).wait()
        slot = lax.rem(g, n_out)

        @pl.when(g >= n_out)
        def _():
            store(t, slot).wait()

        sub_rows = tm // EXPERT_SUBBLOCKS
        blocks = [pl.ds(q * sub_rows, sub_rows) for q in range(EXPERT_SUBBLOCKS)]
        rows_in = []
        for rs in blocks:
            lo, hi = _unpack_bf16_pair(hbuf[lax.rem(g, n_in), rs, :])
            rows_in.append(jnp.concatenate([lo, hi], axis=1).astype(BF16))
        up = [(_dot(h, w1b_ref[...]), _dot(h, w3b_ref[...])) for h in rows_in]
        down = []
        for a, b in up:
            hid = (a * _sigmoid(a)) * b
            down.append(_dot(hid.astype(BF16), w2b_ref[...]))
        for rs, y in zip(blocks, down):
            half = y.shape[1] // 2
            ybuf[slot, rs, :] = _pack_bf16_pair(y[:, :half], y[:, half:])
        store(t, slot).start()

    @pl.when(e + 1 == n_exp)
    def _():
        for back in range(1, n_out + 1):
            @pl.when(total >= back)
            def _():
                store(0, lax.rem(total - back, n_out)).wait()


def _expert_call(tiles_per_expert, hs, w1, w3, w2, capacity):
    cfg = _tiles()
    tm = cfg["expert_rows"]
    prow, half = hs.shape
    n_exp, d, f = w1.shape
    ends = jnp.cumsum(tiles_per_expert)
    base = ends - tiles_per_expert
    g = jnp.arange(capacity * TOP_K // tm + n_exp, dtype=jnp.int32)
    texp = jnp.minimum(jnp.sum((ends[None, :] <= g[:, None]).astype(jnp.int32), axis=1), n_exp - 1)
    onehot = texp[:, None] == jnp.arange(n_exp, dtype=jnp.int32)[None, :]
    tloc = g - jnp.sum(jnp.where(onehot, base[None, :], 0), axis=1)

    def w_map(e, *_):
        return (e, 0, 0)

    grid_spec = pltpu.PrefetchScalarGridSpec(
        num_scalar_prefetch=4,
        grid=(n_exp,),
        in_specs=[
            pl.BlockSpec(memory_space=pl.ANY),
            pl.BlockSpec((None, d, f), w_map),
            pl.BlockSpec((None, d, f), w_map),
            pl.BlockSpec((None, f, d), w_map),
        ],
        out_specs=pl.BlockSpec(memory_space=pl.ANY),
        scratch_shapes=[
            pltpu.VMEM((EXPERT_LOOKAHEAD + 1, tm, half), U32),
            pltpu.VMEM((2, tm, half), U32),
            pltpu.SemaphoreType.DMA((EXPERT_LOOKAHEAD + 1,)),
            pltpu.SemaphoreType.DMA((2,)),
            pltpu.VMEM((d, f), BF16),
            pltpu.VMEM((d, f), BF16),
            pltpu.VMEM((f, d), BF16),
        ],
    )
    return pl.pallas_call(
        functools.partial(_expert_kernel, capacity=capacity),
        grid_spec=grid_spec,
        out_shape=jax.ShapeDtypeStruct((prow, half), U32),
        compiler_params=pltpu.CompilerParams(
            dimension_semantics=("arbitrary",),
            vmem_limit_bytes=cfg["expert_vmem"]),
        name="experts",
    )(tiles_per_expert, base, texp, tloc, hs, w1, w3, w2)


def _sc_mesh():
    return plsc.VectorSubcoreMesh(core_axis_name="c", subcore_axis_name="s",
                                  num_cores=V7X_SC_CORES, num_subcores=V7X_SC_SUBCORES)


def _sc_worker_id():
    return lax.axis_index("s") * V7X_SC_CORES + lax.axis_index("c")


def _dispatch_call(hp, pos_w, out_rows):
    cfg = _tiles()
    ntok, half = hp.shape
    nw, topk, nch, ch = pos_w.shape
    per_w = nch * ch

    def body(hp_hbm, pos_hbm, hs_hbm, idx_v, buf, wsem):
        wid = _sc_worker_id()
        pltpu.sync_copy(pos_hbm.at[wid], idx_v)

        for c in range(nch):
            pltpu.sync_copy(hp_hbm.at[pl.ds(wid * per_w + c * ch, ch)], buf)
            writes = [pltpu.make_async_copy(buf, hs_hbm.at[idx_v.at[k, c]], wsem.at[k]) for k in range(topk)]
            for w in writes:
                w.start()
            for w in writes:
                w.wait()

    assert nw == V7X_SC_CORES * V7X_SC_SUBCORES and nw * per_w == ntok and ch == cfg["sc_rows"]
    return pl.kernel(
        body,
        out_type=jax.ShapeDtypeStruct((out_rows, half), U32),
        mesh=_sc_mesh(),
        scratch_types=[
            pltpu.VMEM((topk, nch, ch), jnp.int32),
            pltpu.VMEM((ch, half), U32),
            pltpu.SemaphoreType.DMA((topk,)),
        ],
        name="dispatch",
    )(hp, pos_w)


def _combine_call(ys, pos_w):
    cfg = _tiles()
    _, half = ys.shape
    nw, topk, nch, ch = pos_w.shape
    per_w = nch * ch
    ntok = nw * per_w

    def body(ys_hbm, pos_hbm, *rest):
        outs = rest[:topk]
        idx_v, buf = rest[topk:]
        wid = _sc_worker_id()
        pltpu.sync_copy(pos_hbm.at[wid], idx_v)
        for c in range(nch):
            for k in range(topk):
                pltpu.sync_copy(ys_hbm.at[idx_v.at[k, c]], buf)
                pltpu.sync_copy(buf, outs[k].at[pl.ds(wid * per_w + c * ch, ch)])

    assert nw == V7X_SC_CORES * V7X_SC_SUBCORES and ch == cfg["sc_rows"]
    return pl.kernel(
        body,
        out_type=[jax.ShapeDtypeStruct((ntok, half), U32)] * topk,
        mesh=_sc_mesh(),
        scratch_types=[
            pltpu.VMEM((topk, nch, ch), jnp.int32),
            pltpu.VMEM((ch, half), U32),
        ],
        name="combine",
    )(ys, pos_w)


def _ple_kernel(x1_ref, yg0_ref, yg1_ref, gate_ref, p_ref, plen_ref, wg32_ref, wu32_ref, fin_ref, o_ref,
                wg_ref, wu_ref):
    @pl.when(pl.program_id(0) == 0)
    def _():
        wg_ref[...] = (0.5 * wg32_ref[...]).astype(BF16)
        wu_ref[...] = (0.5 * wu32_ref[...]).astype(BF16)

    rows = x1_ref.shape[0]
    sub_rows = rows // PLE_SUBBLOCKS
    for q in range(PLE_SUBBLOCKS):
        rs = pl.ds(q * sub_rows, sub_rows)
        lo0, hi0 = _unpack_bf16_pair(yg0_ref[rs, :])
        lo1, hi1 = _unpack_bf16_pair(yg1_ref[rs, :])
        g0 = gate_ref[rs, 0:1]
        g1 = gate_ref[rs, 1:2]
        moe = g0 * jnp.concatenate([lo0, hi0], axis=1) + g1 * jnp.concatenate([lo1, hi1], axis=1)
        x2 = x1_ref[rs, :] + moe
        r = _rmsnorm(x2, plen_ref[...]).astype(BF16)
        gt2 = 1.0 + jnp.tanh(_dot(r, wg_ref[...]))
        up_h = _dot(p_ref[rs, :].astype(BF16), wu_ref[...])
        x3 = x2 + gt2 * up_h
        o_ref[rs, :] = _rmsnorm(x3, fin_ref[...])


def _ple_call(x1, yg0, yg1, gates, p, ple_norm, wg, wu, final_norm):
    cfg = _tiles()
    ntok, d = x1.shape
    tp = cfg["ple_rows"]
    pdim = p.shape[1]
    return pl.pallas_call(
        _ple_kernel,
        grid=(ntok // tp,),
        in_specs=[
            pl.BlockSpec((tp, d), lambda i: (i, 0)),
            pl.BlockSpec((tp, d // 2), lambda i: (i, 0)),
            pl.BlockSpec((tp, d // 2), lambda i: (i, 0)),
            pl.BlockSpec((tp, V7X_SUBLANES), lambda i: (i, 0)),
            pl.BlockSpec((tp, pdim), lambda i: (i, 0)),
            _const_spec((1, d)),
            _const_spec(wg.shape),
            _const_spec(wu.shape),
            _const_spec((1, d)),
        ],
        out_specs=pl.BlockSpec((tp, d), lambda i: (i, 0)),
        out_shape=jax.ShapeDtypeStruct((ntok, d), F32),
        scratch_shapes=[pltpu.VMEM(wg.shape, BF16), pltpu.VMEM(wu.shape, BF16)],
        compiler_params=pltpu.CompilerParams(
            dimension_semantics=("arbitrary",),
            vmem_limit_bytes=cfg["ple_vmem"]),
        name="ple",
    )(x1, yg0, yg1, gates, p, ple_norm, wg, wu, final_norm)


def _blockdiag_pack(w):
    nb, bd, _ = w.shape
    per = V7X_MXU_DIM // bd
    w4 = w.reshape(nb // per, per, bd, bd)
    eye = jnp.eye(per, dtype=w.dtype)
    out = jnp.einsum("jpab,pq->jpaqb", w4, eye)
    return out.reshape(nb // per, V7X_MXU_DIM, V7X_MXU_DIM).astype(BF16)


def kernel(x, p, mix_norm, w_in, conv_w, conv_b, lru_wa, lru_ba, lru_wi, lru_bi, lru_lambda, sgu_ln_g, sgu_ln_b, sgu_ws, sgu_bs, w_out, ffn_norm, router_group_w, router_group_b, router_expert_w, router_expert_b, expert_w1, expert_w3, expert_w2, ple_norm, ple_gate_w, ple_up_w, final_norm):
    cfg = _tiles()
    bsz, seq, d = x.shape
    ntok = bsz * seq
    tm = cfg["expert_rows"]
    depth = w_in.shape[0]
    assert depth == 1, "the ple kernel applies the final norm, so it must be the last layer"
    l = 0
    nw_rows = V7X_SC_CORES * V7X_SC_SUBCORES * cfg["sc_rows"]
    assert cfg["mixer_rows"] % CHUNK == 0 and seq % cfg["mixer_rows"] == 0
    assert ntok % cfg["router_rows"] == 0 and ntok % cfg["ple_rows"] == 0 and ntok % nw_rows == 0
    assert ntok % tm == 0 and lru_wa.shape[1:] == (LRU_BLOCKS, d // LRU_BLOCKS, d // LRU_BLOCKS)
    assert max(cfg[k] for k in cfg if k.endswith("_vmem")) < V7X_VMEM_BYTES
    b_router = jnp.concatenate([
        router_group_b[l], jnp.zeros((EXPERT_ROW0 - N_GROUPS,), F32), router_expert_b[l],
        jnp.zeros((ROUTER_ROWS - EXPERT_ROW0 - N_EXPERTS,), F32)])[:, None]
    ts = cfg["mixer_rows"]
    group = ts // V7X_SUBLANES
    bs_tile = jnp.tile(sgu_bs[l], (1, ts // CHUNK)).reshape(SGU_GROUPS, V7X_SUBLANES, group)
    bs_tile = jnp.transpose(bs_tile, (2, 1, 0)).reshape(ts, SGU_GROUPS)
    x1, hp = _mixer_call(
        x, mix_norm[l][None], w_in[l], conv_w[l], conv_b[l][None],
        _blockdiag_pack(lru_wa[l]), lru_ba[l][None], _blockdiag_pack(lru_wi[l]), lru_bi[l][None],
        lru_lambda[l][None], sgu_ln_g[l][None], sgu_ln_b[l][None], sgu_ws[l], bs_tile,
        w_out[l], ffn_norm[l][None])
    hp = hp.reshape(ntok, d // 2)
    pos, gate, cnt = _router_call(hp, router_group_w[l], router_expert_w[l], b_router)

    cap = ntok
    tiles_per_expert = (cnt[:, 0].astype(jnp.int32) + tm - 1) // tm
    nw = V7X_SC_CORES * V7X_SC_SUBCORES
    ch = cfg["sc_rows"]
    pos_w = jnp.transpose(pos[:TOP_K].reshape(TOP_K, nw, ntok // (nw * ch), ch), (1, 0, 2, 3))

    hs = _dispatch_call(hp, pos_w, N_EXPERTS * cap)
    ys = _expert_call(tiles_per_expert, hs, expert_w1[l], expert_w3[l], expert_w2[l], cap)
    yg0, yg1 = _combine_call(ys, pos_w)

    out = _ple_call(x1.reshape(ntok, d), yg0, yg1, gate, p[l].reshape(ntok, -1), ple_norm[l][None],
                    ple_gate_w[l], ple_up_w[l], final_norm[None])
    return out.reshape(bsz, seq, d)
```

```python
import functools

import jax
import jax.numpy as jnp
from jax import lax
from jax.experimental import pallas as pl
from jax.experimental.pallas import tpu as pltpu
from jax.experimental.pallas import tpu_sc as plsc

F32 = jnp.float32
BF16 = jnp.bfloat16
U32 = jnp.uint32

LRU_BLOCKS = 16
CONV_WIDTH = 4
LRU_C = 8.0
SGU_GROUPS = 8
CHUNK = 128
N_GROUPS = 4
EXPERTS_PER_GROUP = 8
N_EXPERTS = N_GROUPS * EXPERTS_PER_GROUP
TOP_K = 2
EPS = 1e-6

V7X_MXU_DIM = 256
V7X_SUBLANES = 8
V7X_LANES = 128
V7X_VMEM_BYTES = 64 * 1024 * 1024
V7X_SC_CORES = 2
V7X_SC_SUBCORES = 16

EXPERT_LOOKAHEAD = 3
ROUTER_SUBBLOCKS = 8
EXPERT_SUBBLOCKS = 2
PLE_SUBBLOCKS = 4
PLE_INPUT_BUFFERS = 3
ROUTER_ROWS = V7X_LANES
EXPERT_ROW0 = V7X_SUBLANES


def _tiles():
    return dict(
        mixer_rows=256,
        expert_rows=512,
        ple_rows=1024,
        sc_rows=128,
        router_rows=4096,
        mixer_vmem=52 * 1024 * 1024,
        expert_vmem=40 * 1024 * 1024,
        ple_vmem=58 * 1024 * 1024,
        router_vmem=40 * 1024 * 1024,
    )


def _dot(a, b):
    return jnp.dot(a, b, preferred_element_type=F32)


def _sigmoid(x):
    return 0.5 * jnp.tanh(0.5 * x) + 0.5


def _rmsnorm(x, g):
    ms = jnp.mean(x * x, axis=-1, keepdims=True)
    return x * lax.rsqrt(ms + EPS) * g


def _pack_bf16_pair(lo, hi):
    lo_b = lax.bitcast_convert_type(lo.astype(BF16).astype(F32), U32)
    hi_b = lax.bitcast_convert_type(hi.astype(BF16).astype(F32), U32)
    return (hi_b & jnp.uint32(0xFFFF0000)) | lax.shift_right_logical(lo_b, jnp.uint32(16))


def _unpack_bf16_pair(w):
    lo = lax.bitcast_convert_type(lax.shift_left(w, jnp.uint32(16)), F32)
    hi = lax.bitcast_convert_type(w & jnp.uint32(0xFFFF0000), F32)
    return lo, hi


def _const_spec(shape):
    zeros = (0,) * len(shape)
    return pl.BlockSpec(shape, lambda *_: zeros, pipeline_mode=pl.Buffered(1))


def _tile_copies(hbm, buf, sem, b, row0, slot, to_hbm):
    group = buf.shape[1]
    copies = []
    for r in range(V7X_SUBLANES):
        hbm_rows = hbm.at[b, pl.ds(row0 + group * r, group), :]
        vmem_rows = buf.at[slot, :, r, :]
        src, dst = (vmem_rows, hbm_rows) if to_hbm else (hbm_rows, vmem_rows)
        copies.append(pltpu.make_async_copy(src, dst, sem.at[slot]))
    return copies


def _lru_scan(a, u, h0):
    group = a.shape[0]
    acc_a = [a[0]]
    acc_u = [u[0]]
    for g in range(1, group):
        acc_a.append(a[g] * acc_a[-1])
        acc_u.append(a[g] * acc_u[-1] + u[g])
    end_a, end_u = acc_a[-1], acc_u[-1]
    sub = lax.broadcasted_iota(jnp.int32, end_a.shape, 0)
    shift = 1
    while shift < V7X_SUBLANES:
        keep = sub >= shift
        a_sh = pltpu.roll(end_a, shift, axis=0)
        u_sh = pltpu.roll(end_u, shift, axis=0)
        end_u = jnp.where(keep, end_a * u_sh + end_u, end_u)
        end_a = jnp.where(keep, end_a * a_sh, end_a)
        shift *= 2
    h_end = end_a * h0 + end_u
    h_in = jnp.where(sub == 0, h0, pltpu.roll(h_end, 1, axis=0))
    out = [acc_a[g] * h_in + acc_u[g] for g in range(group)]
    return jnp.stack(out, axis=0), h_end[V7X_SUBLANES - 1:V7X_SUBLANES, :]


def _mixer_kernel(x_hbm, mixn_ref, win_hbm, convw_ref, convb_ref, wa_ref, ba_ref, wi_ref, bi_ref,
                  lam_ref, lng_ref, lnb_ref, ws_ref, bsp_ref, wout_hbm, ffn_ref,
                  x1_hbm, hp_hbm,
                  xbuf, z0_ref, z1_ref, x1buf, hpbuf, xsem, x1sem, hpsem, wsm_ref, ztail_ref, hcar_ref,
                  win_ref, wout_ref, wsem,
                  *, nseq):
    j = pl.program_id(0)
    ntile = pl.num_programs(0) - 1
    _, group, _, d = xbuf.shape
    rows = group * V7X_SUBLANES
    half = d // 2
    ta = jnp.minimum(j, ntile - 1)
    tb = jnp.maximum(j - 1, 0)
    s = lax.rem(tb, nseq)
    slot = lax.rem(tb, 2)

    def fetch(t):
        return _tile_copies(x_hbm, xbuf, xsem, lax.div(t, nseq), lax.rem(t, nseq) * rows,
                            lax.rem(t, 3), to_hbm=False)

    def put(t):
        tb_, ts_, sl = lax.div(t, nseq), lax.rem(t, nseq) * rows, lax.rem(t, 2)
        return (_tile_copies(x1_hbm, x1buf, x1sem, tb_, ts_, sl, to_hbm=True)
                + _tile_copies(hp_hbm, hpbuf, hpsem, tb_, ts_, sl, to_hbm=True))

    @pl.when(j == 0)
    def _():
        for i, c in enumerate(fetch(0)):
            c.start(priority=i % 2)
        stage = (z0_ref, z1_ref)
        n_in_chunks = win_hbm.shape[0] // rows

        def win_copy(c):
            return pltpu.make_async_copy(win_hbm.at[pl.ds(c * rows, rows), :], stage[c % 2], wsem.at[c % 2])

        win_copy(0).start()
        for c in range(n_in_chunks):
            if c + 1 < n_in_chunks:
                win_copy(c + 1).start()
            win_copy(c).wait()
            plain = 4 * d
            win_ref[c * rows:(c + 1) * rows, :plain] = stage[c % 2][:, :plain].astype(BF16)
            win_ref[c * rows:(c + 1) * rows, plain:] = (0.5 * stage[c % 2][:, plain:]).astype(BF16)
        n_out_chunks = wout_hbm.shape[0] // rows
        out_copies = [pltpu.make_async_copy(wout_hbm.at[pl.ds(c * rows, rows), :],
                                            z0_ref.at[:, c * d:(c + 1) * d], wsem.at[0])
                      for c in range(n_out_chunks)]
        for cp in out_copies:
            cp.start()
        for cp in out_copies:
            cp.wait()
        for c in range(n_out_chunks):
            wout_ref[c * rows:(c + 1) * rows, :] = (0.25 * z0_ref[:, c * d:(c + 1) * d]).astype(BF16)
        z1_ref[...] = jnp.zeros_like(z1_ref)
        i_idx = lax.broadcasted_iota(jnp.int32, (rows, rows), 0)
        j_idx = lax.broadcasted_iota(jnp.int32, (rows, rows), 1)
        t_i = group * lax.rem(i_idx, V7X_SUBLANES) + lax.div(i_idx, V7X_SUBLANES)
        t_j = group * lax.rem(j_idx, V7X_SUBLANES) + lax.div(j_idx, V7X_SUBLANES)
        keep = (t_i >= t_j) & (lax.div(t_i, CHUNK) == lax.div(t_j, CHUNK))
        pick_rows = jnp.where(t_i == j_idx, 1.0, 0.0).astype(BF16)
        pick_cols = jnp.where(i_idx == t_j, 1.0, 0.0).astype(BF16)
        reps = rows // CHUNK
        for g in range(SGU_GROUPS):
            w_chunk = ws_ref[g].astype(BF16)
            w_rows = jnp.concatenate([w_chunk] * reps, axis=1)
            w_full = jnp.concatenate([w_rows] * reps, axis=0)
            w_perm = _dot(_dot(pick_rows, w_full).astype(BF16), pick_cols)
            wsm_ref[g] = jnp.where(keep, w_perm, 0.0).astype(BF16)

    @pl.when(j + 1 < ntile)
    def _():
        for i, c in enumerate(fetch(j + 1)):
            c.start(priority=i % 2)

    @pl.when(j < ntile)
    def _():
        for c in fetch(j):
            c.wait()

    @pl.when(s == 0)
    def _():
        ztail_ref[...] = jnp.zeros_like(ztail_ref)
        hcar_ref[...] = jnp.zeros_like(hcar_ref)

    def compute(z_w, z_r):
        xa_in = xbuf[lax.rem(ta, 3)].reshape(rows, d)
        h_next = _rmsnorm(xa_in, mixn_ref[...]).astype(BF16)
        pw = d

        def project(k):
            z_w[:, k * pw:(k + 1) * pw] = _dot(h_next, win_ref[:, k * pw:(k + 1) * pw])

        x = xbuf[lax.rem(tb, 3)].reshape(rows, d)

        def sec(k, c0, c1):
            return z_r[:, k * d + c0:k * d + c1]

        def one_plus_tanh_gelu(v):
            c = 0.7978845608028654
            return 1.0 + jnp.tanh(v * (c + (c * 0.044715) * (v * v)))

        cw = 0.5 * convw_ref[...]
        cb_h = 0.5 * convb_ref[...]
        ba_h = 0.5 * ba_ref[...]
        bi_h = 0.5 * bi_ref[...]
        neg_lam = -lam_ref[...]
        softplus = jnp.maximum(neg_lam, 0.0) + jnp.log1p(jnp.exp(-jnp.abs(neg_lam)))
        c_a = (-0.5 * LRU_C) * softplus
        blk = V7X_MXU_DIM
        sub3 = lax.broadcasted_iota(jnp.int32, (CONV_WIDTH - 1, V7X_SUBLANES, blk), 1)
        term_a = []
        for n in range(d // blk):
            if n % 2 == 0:
                project(n // 2)
            c0, c1 = n * blk, (n + 1) * blk
            z3 = sec(0, c0, c1).reshape(group, V7X_SUBLANES, blk)
            tail = z3[group - (CONV_WIDTH - 1):]
            halo = jnp.where(sub3 == 0, pltpu.roll(ztail_ref[:, :, c0:c1], 1, axis=1),
                             pltpu.roll(tail, 1, axis=1))
            ztail_ref[:, :, c0:c1] = tail
            zext = jnp.concatenate([halo, z3], axis=0)
            xa_h = cb_h[:, c0:c1] + cw[CONV_WIDTH - 1:CONV_WIDTH, c0:c1] * z3
            for k in range(1, CONV_WIDTH):
                lo = CONV_WIDTH - 1 - k
                xa_h = xa_h + cw[lo:lo + 1, c0:c1] * zext[lo:lo + group]
            xa2 = xa_h.reshape(rows, blk)
            xa_bf = xa2.astype(BF16)
            th_r = jnp.tanh(_dot(xa_bf, wa_ref[n]) + ba_h[:, c0:c1])
            th_i = jnp.tanh(_dot(xa_bf, wi_ref[n]) + bi_h[:, c0:c1])
            a = jnp.exp(c_a[:, c0:c1] + c_a[:, c0:c1] * th_r)
            u = jnp.sqrt(1.0 - a * a) * ((1.0 + th_i) * xa2)
            hseq, hlast = _lru_scan(a.reshape(group, V7X_SUBLANES, blk),
                                    u.reshape(group, V7X_SUBLANES, blk), hcar_ref[:, c0:c1])
            hcar_ref[:, c0:c1] = hlast
            zg = sec(1, c0, c1)
            term_a.append(((1.0 + jnp.tanh(sec(4, c0, c1))) * one_plus_tanh_gelu(zg))
                          * (zg * hseq.reshape(rows, blk)))

        project(2)
        zv = sec(3, 0, d)
        gv2 = zv * one_plus_tanh_gelu(zv)
        mu = jnp.mean(gv2, axis=-1, keepdims=True)
        xc = gv2 - mu
        var = jnp.mean(xc * xc, axis=-1, keepdims=True)
        v_bf = (xc * lax.rsqrt(var + 4.0 * EPS) * lng_ref[...] + lnb_ref[...]).astype(BF16)
        project(3)
        gdim = d // SGU_GROUPS
        term_b = []
        for g in range(SGU_GROUPS):
            c0, c1 = g * gdim, (g + 1) * gdim
            if g in (2, 5):
                project({2: 4, 5: 5}[g])
            sp = _dot(wsm_ref[g], v_bf[:, c0:c1]) + bsp_ref[:, g:g + 1]
            zu = sec(2, c0, c1)
            term_b.append(((1.0 + jnp.tanh(sec(5, c0, c1))) * one_plus_tanh_gelu(zu)) * (zu * sp))
        merged4 = jnp.concatenate(term_a, axis=1) + jnp.concatenate(term_b, axis=1)

        x1 = x + _dot(merged4.astype(BF16), wout_ref[...])

        hn = _rmsnorm(x1, ffn_ref[...])
        hp = _pack_bf16_pair(hn[:, :half], hn[:, half:])

        @pl.when(j >= 3)
        def _():
            for c in put(tb - 2):
                c.wait()

        x1buf[slot] = x1.reshape(group, V7X_SUBLANES, d)
        hpbuf[slot] = hp.reshape(group, V7X_SUBLANES, half)

        @pl.when(j >= 1)
        def _():
            for i, c in enumerate(put(tb)):
                c.start(priority=i % 2)

    @pl.when(lax.rem(j, 2) == 0)
    def _():
        compute(z0_ref, z1_ref)

    @pl.when(lax.rem(j, 2) == 1)
    def _():
        compute(z1_ref, z0_ref)

    @pl.when(j == ntile)
    def _():
        for c in put(tb):
            c.wait()

        @pl.when(ntile >= 2)
        def _():
            for c in put(tb - 1):
                c.wait()


def _mixer_call(x, mix_norm, w_in, conv_w, conv_b, wa_blk, ba, wi_blk, bi, lam, ln_g, ln_b, ws,
                bs_tile, w_out, ffn_norm):
    cfg = _tiles()
    bsz, seq, d = x.shape
    ts = cfg["mixer_rows"]
    group = ts // V7X_SUBLANES
    nseq = seq // ts
    ntile = bsz * nseq
    row1 = (1, d)
    in_specs = [
        pl.BlockSpec(memory_space=pl.ANY),
        _const_spec(row1),
        pl.BlockSpec(memory_space=pl.ANY),
        _const_spec(conv_w.shape), _const_spec(row1),
        _const_spec(wa_blk.shape), _const_spec(row1),
        _const_spec(wi_blk.shape), _const_spec(row1),
        _const_spec(row1),
        _const_spec(row1), _const_spec(row1),
        _const_spec(ws.shape), _const_spec(bs_tile.shape),
        pl.BlockSpec(memory_space=pl.ANY), _const_spec(row1),
    ]
    out_shape = [
        jax.ShapeDtypeStruct((bsz, seq, d), F32),
        jax.ShapeDtypeStruct((bsz, seq, d // 2), U32),
    ]
    out_specs = [
        pl.BlockSpec(memory_space=pl.ANY),
        pl.BlockSpec(memory_space=pl.ANY),
    ]
    scratch = [
        pltpu.VMEM((3, group, V7X_SUBLANES, d), F32),
        pltpu.VMEM((ts, w_in.shape[1]), F32),
        pltpu.VMEM((ts, w_in.shape[1]), F32),
        pltpu.VMEM((2, group, V7X_SUBLANES, d), F32),
        pltpu.VMEM((2, group, V7X_SUBLANES, d // 2), U32),
        pltpu.SemaphoreType.DMA((3,)),
        pltpu.SemaphoreType.DMA((2,)),
        pltpu.SemaphoreType.DMA((2,)),
        pltpu.VMEM((SGU_GROUPS, ts, ts), BF16),
        pltpu.VMEM((CONV_WIDTH - 1, V7X_SUBLANES, d), F32),
        pltpu.VMEM((1, d), F32),
        pltpu.VMEM(w_in.shape, BF16),
        pltpu.VMEM(w_out.shape, BF16),
        pltpu.SemaphoreType.DMA((2,)),
    ]
    return pl.pallas_call(
        functools.partial(_mixer_kernel, nseq=nseq),
        grid=(ntile + 1,),
        in_specs=in_specs,
        out_specs=out_specs,
        out_shape=out_shape,
        scratch_shapes=scratch,
        compiler_params=pltpu.CompilerParams(
            dimension_semantics=("arbitrary",),
            vmem_limit_bytes=cfg["mixer_vmem"]),
        name="mixer",
    )(x, mix_norm, w_in, conv_w, conv_b, wa_blk, ba, wi_blk, bi, lam, ln_g, ln_b, ws, bs_tile,
      w_out, ffn_norm)


def _router_kernel(hp_ref, wg_ref, we_ref, br_ref, pos_ref, gate_ref, cnt_ref, ccar_ref, wr_ref,
                   *, expert_capacity):
    rows = hp_ref.shape[0]

    @pl.when(pl.program_id(0) == 0)
    def _():
        ccar_ref[...] = jnp.zeros_like(ccar_ref)
        wr_ref[...] = jnp.zeros_like(wr_ref)
        wr_ref[:, 0:N_GROUPS] = wg_ref[...].astype(BF16)
        wr_ref[:, EXPERT_ROW0:EXPERT_ROW0 + N_EXPERTS] = we_ref[...].astype(BF16)

    sub_rows = rows // ROUTER_SUBBLOCKS
    lts = []
    for q in range(ROUTER_SUBBLOCKS):
        lo, hi = _unpack_bf16_pair(hp_ref[q * sub_rows:(q + 1) * sub_rows, :])
        hn = jnp.concatenate([lo, hi], axis=1)
        logits = _dot(hn.astype(BF16), wr_ref[...])
        lts.append(jnp.transpose(logits) + br_ref[...])
    sub = lax.broadcasted_iota(jnp.int32, (V7X_SUBLANES, sub_rows), 0)
    subf = sub.astype(F32)
    big = jnp.float32(1e9)
    eid = lax.broadcasted_iota(jnp.int32, (N_EXPERTS, sub_rows), 0).astype(F32)
    sb = V7X_MXU_DIM
    before = (lax.broadcasted_iota(jnp.int32, (sb, sb), 0)
              < lax.broadcasted_iota(jnp.int32, (sb, sb), 1))
    before = jnp.where(before, 1.0, 0.0).astype(BF16)
    cap = float(expert_capacity)
    zero = jnp.zeros((V7X_SUBLANES - TOP_K, sub_rows), F32)
    running = ccar_ref[:, 0:1]
    for q, lt in enumerate(lts):
        lg = jnp.where(sub < N_GROUPS, lt[0:V7X_SUBLANES, :], -jnp.inf)
        g_exp = jnp.exp(lg - jnp.max(lg, axis=0, keepdims=True))
        g_prob = g_exp / jnp.sum(g_exp, axis=0, keepdims=True)
        g_top = jnp.max(g_prob, axis=0, keepdims=True)
        g_idx = jnp.min(jnp.where(g_prob == g_top, subf, big), axis=0, keepdims=True)

        e_sel = jnp.zeros((EXPERTS_PER_GROUP, sub_rows), F32)
        for g in range(N_GROUPS):
            r0 = EXPERT_ROW0 + g * EXPERTS_PER_GROUP
            e_sel = jnp.where(g_idx == g, lt[r0:r0 + EXPERTS_PER_GROUP, :], e_sel)
        e_exp = jnp.exp(e_sel - jnp.max(e_sel, axis=0, keepdims=True))
        e_prob = e_exp / jnp.sum(e_exp, axis=0, keepdims=True)
        p1 = jnp.max(e_prob, axis=0, keepdims=True)
        i1 = jnp.min(jnp.where(e_prob == p1, subf, big), axis=0, keepdims=True)
        rest = jnp.where(subf == i1, -1.0, e_prob)
        p2 = jnp.max(rest, axis=0, keepdims=True)
        i2 = jnp.min(jnp.where(rest == p2, subf, big), axis=0, keepdims=True)
        psum = p1 + p2
        gate1 = g_top * (p1 / psum)
        gate2 = g_top * (p2 / psum)
        gid1 = g_idx * EXPERTS_PER_GROUP + i1
        gid2 = g_idx * EXPERTS_PER_GROUP + i2

        hit1 = eid == gid1
        hit2 = eid == gid2
        cnt = jnp.where(hit1 | hit2, 1.0, 0.0)
        base = []
        for c in range(sub_rows // sb):
            part = cnt[:, c * sb:(c + 1) * sb]
            base.append(running + _dot(part.astype(BF16), before))
            running = running + jnp.sum(part, axis=1, keepdims=True)
        base = jnp.concatenate(base, axis=1)
        rank1 = jnp.sum(jnp.where(hit1, base, 0.0), axis=0, keepdims=True)
        rank2 = jnp.sum(jnp.where(hit2, base, 0.0), axis=0, keepdims=True)
        pos = jnp.concatenate([gid1 * cap + rank1, gid2 * cap + rank2, zero], axis=0)
        pos_ref[:, q * sub_rows:(q + 1) * sub_rows] = pos.astype(jnp.int32)
        gate_ref[q * sub_rows:(q + 1) * sub_rows, :] = jnp.transpose(
            jnp.concatenate([gate1, gate2, zero], axis=0))
    total = jnp.broadcast_to(running, ccar_ref.shape)
    ccar_ref[...] = total
    cnt_ref[...] = total


def _router_call(hp, w_group, w_expert, b_router):
    cfg = _tiles()
    ntok, half = hp.shape
    tr = cfg["router_rows"]
    return pl.pallas_call(
        functools.partial(_router_kernel, expert_capacity=ntok),
        grid=(ntok // tr,),
        in_specs=[
            pl.BlockSpec((tr, half), lambda i: (i, 0)),
            _const_spec(w_group.shape),
            _const_spec(w_expert.shape),
            _const_spec(b_router.shape),
        ],
        out_specs=[
            pl.BlockSpec((V7X_SUBLANES, tr), lambda i: (0, i)),
            pl.BlockSpec((tr, V7X_SUBLANES), lambda i: (i, 0)),
            pl.BlockSpec((N_EXPERTS, V7X_LANES), lambda i: (0, 0)),
        ],
        out_shape=[
            jax.ShapeDtypeStruct((V7X_SUBLANES, ntok), jnp.int32),
            jax.ShapeDtypeStruct((ntok, V7X_SUBLANES), F32),
            jax.ShapeDtypeStruct((N_EXPERTS, V7X_LANES), F32),
        ],
        scratch_shapes=[
            pltpu.VMEM((N_EXPERTS, V7X_LANES), F32),
            pltpu.VMEM((w_group.shape[0], ROUTER_ROWS), BF16),
        ],
        compiler_params=pltpu.CompilerParams(
            dimension_semantics=("arbitrary",),
            vmem_limit_bytes=cfg["router_vmem"]),
        name="router",
    )(hp, w_group, w_expert, b_router)


def _expert_kernel(nt_ref, base_ref, texp_ref, tloc_ref, hs_hbm, w1_ref, w3_ref, w2_ref, ys_hbm,
                   hbuf, ybuf, hsem, ysem, w1b_ref, w3b_ref, w2b_ref, *, capacity):
    e = pl.program_id(0)
    n_exp = pl.num_programs(0)
    nt = nt_ref[e]
    base = base_ref[e]
    total = base_ref[n_exp - 1] + nt_ref[n_exp - 1]
    n_in, tm, _ = hbuf.shape
    n_out = ybuf.shape[0]
    ahead = n_in - 1

    def load(g):
        slot = lax.rem(g, n_in)
        rows = pl.ds(texp_ref[g] * capacity + tloc_ref[g] * tm, tm)
        return pltpu.make_async_copy(hs_hbm.at[rows], hbuf.at[slot], hsem.at[slot])

    def store(t, slot):
        rows = pl.ds(e * capacity + t * tm, tm)
        return pltpu.make_async_copy(ybuf.at[slot], ys_hbm.at[rows], ysem.at[slot])

    @pl.when(e == 0)
    def _():
        for g0 in range(ahead):
            @pl.when(g0 < total)
            def _():
                load(g0).start(priority=1)

    w1b_ref[...] = w1_ref[...].astype(BF16)
    w3b_ref[...] = w3_ref[...].astype(BF16)
    w2b_ref[...] = w2_ref[...].astype(BF16)

    @pl.loop(0, nt)
    def _(t):
        g = base + t

        @pl.when(g + ahead < total)
        def _():
            load(g + ahead).start(priority=1)

        load(g).wait()
        slot = lax.rem(g, n_out)

        @pl.when(g >= n_out)
        def _():
            store(t, slot).wait()

        sub_rows = tm // EXPERT_SUBBLOCKS
        blocks = [pl.ds(q * sub_rows, sub_rows) for q in range(EXPERT_SUBBLOCKS)]
        rows_in = []
        for rs in blocks:
            lo, hi = _unpack_bf16_pair(hbuf[lax.rem(g, n_in), rs, :])
            rows_in.append(jnp.concatenate([lo, hi], axis=1).astype(BF16))
        up = [(_dot(h, w1b_ref[...]), _dot(h, w3b_ref[...])) for h in rows_in]
        down = []
        for a, b in up:
            hid = (a * _sigmoid(a)) * b
            down.append(_dot(hid.astype(BF16), w2b_ref[...]))
        for rs, y in zip(blocks, down):
            half = y.shape[1] // 2
            ybuf[slot, rs, :] = _pack_bf16_pair(y[:, :half], y[:, half:])
        store(t, slot).start()

    @pl.when(e + 1 == n_exp)
    def _():
        for back in range(1, n_out + 1):
            @pl.when(total >= back)
            def _():
                store(0, lax.rem(total - back, n_out)).wait()


def _expert_call(tiles_per_expert, hs, w1, w3, w2, capacity):
    cfg = _tiles()
    tm = cfg["expert_rows"]
    prow, half = hs.shape
    n_exp, d, f = w1.shape
    ends = jnp.cumsum(tiles_per_expert)
    base = ends - tiles_per_expert
    g = jnp.arange(capacity * TOP_K // tm + n_exp, dtype=jnp.int32)
    texp = jnp.minimum(jnp.sum((ends[None, :] <= g[:, None]).astype(jnp.int32), axis=1), n_exp - 1)
    onehot = texp[:, None] == jnp.arange(n_exp, dtype=jnp.int32)[None, :]
    tloc = g - jnp.sum(jnp.where(onehot, base[None, :], 0), axis=1)

    def w_map(e, *_):
        return (e, 0, 0)

    grid_spec = pltpu.PrefetchScalarGridSpec(
        num_scalar_prefetch=4,
        grid=(n_exp,),
        in_specs=[
            pl.BlockSpec(memory_space=pl.ANY),
            pl.BlockSpec((None, d, f), w_map),
            pl.BlockSpec((None, d, f), w_map),
            pl.BlockSpec((None, f, d), w_map),
        ],
        out_specs=pl.BlockSpec(memory_space=pl.ANY),
        scratch_shapes=[
            pltpu.VMEM((EXPERT_LOOKAHEAD + 1, tm, half), U32),
            pltpu.VMEM((2, tm, half), U32),
            pltpu.SemaphoreType.DMA((EXPERT_LOOKAHEAD + 1,)),
            pltpu.SemaphoreType.DMA((2,)),
            pltpu.VMEM((d, f), BF16),
            pltpu.VMEM((d, f), BF16),
            pltpu.VMEM((f, d), BF16),
        ],
    )
    return pl.pallas_call(
        functools.partial(_expert_kernel, capacity=capacity),
        grid_spec=grid_spec,
        out_shape=jax.ShapeDtypeStruct((prow, half), U32),
        compiler_params=pltpu.CompilerParams(
            dimension_semantics=("arbitrary",),
            vmem_limit_bytes=cfg["expert_vmem"]),
        name="experts",
    )(tiles_per_expert, base, texp, tloc, hs, w1, w3, w2)


def _sc_mesh():
    return plsc.VectorSubcoreMesh(core_axis_name="c", subcore_axis_name="s",
                                  num_cores=V7X_SC_CORES, num_subcores=V7X_SC_SUBCORES)


def _sc_worker_id():
    return lax.axis_index("s") * V7X_SC_CORES + lax.axis_index("c")


def _dispatch_call(hp, pos_w, out_rows):
    cfg = _tiles()
    ntok, half = hp.shape
    nw, topk, nch, ch = pos_w.shape
    per_w = nch * ch

    def body(hp_hbm, pos_hbm, hs_hbm, idx_v, buf, wsem):
        wid = _sc_worker_id()
        pltpu.sync_copy(pos_hbm.at[wid], idx_v)

        for c in range(nch):
            pltpu.sync_copy(hp_hbm.at[pl.ds(wid * per_w + c * ch, ch)], buf)
            writes = [pltpu.make_async_copy(buf, hs_hbm.at[idx_v.at[k, c]], wsem.at[k]) for k in range(topk)]
            for w in writes:
                w.start()
            for w in writes:
                w.wait()

    assert nw == V7X_SC_CORES * V7X_SC_SUBCORES and nw * per_w == ntok and ch == cfg["sc_rows"]
    return pl.kernel(
        body,
        out_type=jax.ShapeDtypeStruct((out_rows, half), U32),
        mesh=_sc_mesh(),
        scratch_types=[
            pltpu.VMEM((topk, nch, ch), jnp.int32),
            pltpu.VMEM((ch, half), U32),
            pltpu.SemaphoreType.DMA((topk,)),
        ],
        name="dispatch",
    )(hp, pos_w)


def _combine_call(ys, pos_w):
    cfg = _tiles()
    _, half = ys.shape
    nw, topk, nch, ch = pos_w.shape
    per_w = nch * ch
    ntok = nw * per_w

    def body(ys_hbm, pos_hbm, *rest):
        outs = rest[:topk]
        idx_v, buf = rest[topk:]
        wid = _sc_worker_id()
        pltpu.sync_copy(pos_hbm.at[wid], idx_v)
        for c in range(nch):
            for k in range(topk):
                pltpu.sync_copy(ys_hbm.at[idx_v.at[k, c]], buf)
                pltpu.sync_copy(buf, outs[k].at[pl.ds(wid * per_w + c * ch, ch)])

    assert nw == V7X_SC_CORES * V7X_SC_SUBCORES and ch == cfg["sc_rows"]
    return pl.kernel(
        body,
        out_type=[jax.ShapeDtypeStruct((ntok, half), U32)] * topk,
        mesh=_sc_mesh(),
        scratch_types=[
            pltpu.VMEM((topk, nch, ch), jnp.int32),
            pltpu.VMEM((ch, half), U32),
        ],
        name="combine",
    )(ys, pos_w)


def _ple_kernel(x1_hbm, yg0_hbm, yg1_hbm, gate_hbm, p_hbm, plen_ref, wg32_ref, wu32_ref, fin_ref, o_hbm,
                wg_ref, wu_ref, *, tp):
    wg_ref[...] = (0.5 * wg32_ref[...]).astype(BF16)
    wu_ref[...] = (0.5 * wu32_ref[...]).astype(BF16)
    ntok, d = x1_hbm.shape
    pdim = p_hbm.shape[1]

    def tile(x1_ref, yg0_ref, yg1_ref, gate_ref, p_ref, o_ref):
        sub_rows = tp // PLE_SUBBLOCKS
        for q in range(PLE_SUBBLOCKS):
            rs = pl.ds(q * sub_rows, sub_rows)
            lo0, hi0 = _unpack_bf16_pair(yg0_ref[rs, :])
            lo1, hi1 = _unpack_bf16_pair(yg1_ref[rs, :])
            g0 = gate_ref[rs, 0:1]
            g1 = gate_ref[rs, 1:2]
            moe = g0 * jnp.concatenate([lo0, hi0], axis=1) + g1 * jnp.concatenate([lo1, hi1], axis=1)
            x2 = x1_ref[rs, :] + moe
            r = _rmsnorm(x2, plen_ref[...]).astype(BF16)
            gt2 = 1.0 + jnp.tanh(_dot(r, wg_ref[...]))
            up_h = _dot(p_ref[rs, :].astype(BF16), wu_ref[...])
            x3 = x2 + gt2 * up_h
            o_ref[rs, :] = _rmsnorm(x3, fin_ref[...])

    def rows(width, buffers=2):
        return pl.BlockSpec((tp, width), lambda i: (i, 0), pipeline_mode=pl.Buffered(buffers))

    pltpu.emit_pipeline(
        tile,
        grid=(ntok // tp,),
        in_specs=[rows(d, PLE_INPUT_BUFFERS), rows(d // 2, PLE_INPUT_BUFFERS),
                  rows(d // 2, PLE_INPUT_BUFFERS), rows(V7X_SUBLANES), rows(pdim)],
        out_specs=[rows(d)],
    )(x1_hbm, yg0_hbm, yg1_hbm, gate_hbm, p_hbm, o_hbm)


def _ple_call(x1, yg0, yg1, gates, p, ple_norm, wg, wu, final_norm):
    cfg = _tiles()
    ntok, d = x1.shape
    any_spec = pl.BlockSpec(memory_space=pl.ANY)
    vmem_spec = pl.BlockSpec(memory_space=pltpu.VMEM)
    return pl.pallas_call(
        functools.partial(_ple_kernel, tp=cfg["ple_rows"]),
        in_specs=[any_spec] * 5 + [vmem_spec] * 4,
        out_specs=any_spec,
        out_shape=jax.ShapeDtypeStruct((ntok, d), F32),
        scratch_shapes=[pltpu.VMEM(wg.shape, BF16), pltpu.VMEM(wu.shape, BF16)],
        compiler_params=pltpu.CompilerParams(vmem_limit_bytes=cfg["ple_vmem"]),
        name="ple",
    )(x1, yg0, yg1, gates, p, ple_norm, wg, wu, final_norm)


def _blockdiag_pack(w):
    nb, bd, _ = w.shape
    per = V7X_MXU_DIM // bd
    w4 = w.reshape(nb // per, per, bd, bd)
    eye = jnp.eye(per, dtype=w.dtype)
    out = jnp.einsum("jpab,pq->jpaqb", w4, eye)
    return out.reshape(nb // per, V7X_MXU_DIM, V7X_MXU_DIM).astype(BF16)


def kernel(x, p, mix_norm, w_in, conv_w, conv_b, lru_wa, lru_ba, lru_wi, lru_bi, lru_lambda, sgu_ln_g, sgu_ln_b, sgu_ws, sgu_bs, w_out, ffn_norm, router_group_w, router_group_b, router_expert_w, router_expert_b, expert_w1, expert_w3, expert_w2, ple_norm, ple_gate_w, ple_up_w, final_norm):
    cfg = _tiles()
    bsz, seq, d = x.shape
    ntok = bsz * seq
    tm = cfg["expert_rows"]
    depth = w_in.shape[0]
    assert depth == 1, "the ple kernel applies the final norm, so it must be the last layer"
    l = 0
    nw_rows = V7X_SC_CORES * V7X_SC_SUBCORES * cfg["sc_rows"]
    assert cfg["mixer_rows"] % CHUNK == 0 and seq % cfg["mixer_rows"] == 0
    assert ntok % cfg["router_rows"] == 0 and ntok % cfg["ple_rows"] == 0 and ntok % nw_rows == 0
    assert ntok % tm == 0 and lru_wa.shape[1:] == (LRU_BLOCKS, d // LRU_BLOCKS, d // LRU_BLOCKS)
    assert max(cfg[k] for k in cfg if k.endswith("_vmem")) < V7X_VMEM_BYTES
    b_router = jnp.concatenate([
        router_group_b[l], jnp.zeros((EXPERT_ROW0 - N_GROUPS,), F32), router_expert_b[l],
        jnp.zeros((ROUTER_ROWS - EXPERT_ROW0 - N_EXPERTS,), F32)])[:, None]
    ts = cfg["mixer_rows"]
    group = ts // V7X_SUBLANES
    bs_tile = jnp.tile(sgu_bs[l], (1, ts // CHUNK)).reshape(SGU_GROUPS, V7X_SUBLANES, group)
    bs_tile = jnp.transpose(bs_tile, (2, 1, 0)).reshape(ts, SGU_GROUPS)
    x1, hp = _mixer_call(
        x, mix_norm[l][None], w_in[l], conv_w[l], conv_b[l][None],
        _blockdiag_pack(lru_wa[l]), lru_ba[l][None], _blockdiag_pack(lru_wi[l]), lru_bi[l][None],
        lru_lambda[l][None], sgu_ln_g[l][None], sgu_ln_b[l][None], sgu_ws[l], bs_tile,
        w_out[l], ffn_norm[l][None])
    hp = hp.reshape(ntok, d // 2)
    pos, gate, cnt = _router_call(hp, router_group_w[l], router_expert_w[l], b_router)

    cap = ntok
    tiles_per_expert = (cnt[:, 0].astype(jnp.int32) + tm - 1) // tm
    nw = V7X_SC_CORES * V7X_SC_SUBCORES
    ch = cfg["sc_rows"]
    pos_w = jnp.transpose(pos[:TOP_K].reshape(TOP_K, nw, ntok // (nw * ch), ch), (1, 0, 2, 3))

    hs = _dispatch_call(hp, pos_w, N_EXPERTS * cap)
    ys = _expert_call(tiles_per_expert, hs, expert_w1[l], expert_w3[l], expert_w2[l], cap)
    yg0, yg1 = _combine_call(ys, pos_w)

    out = _ple_call(x1.reshape(ntok, d), yg0, yg1, gate, p[l].reshape(ntok, -1), ple_norm[l][None],
                    ple_gate_w[l], ple_up_w[l], final_norm[None])
    return out.reshape(bsz, seq, d)
```

```python
import functools

import jax
import jax.numpy as jnp
from jax import lax
from jax.experimental import pallas as pl
from jax.experimental.pallas import tpu as pltpu
from jax.experimental.pallas import tpu_sc as plsc

F32 = jnp.float32
BF16 = jnp.bfloat16
U32 = jnp.uint32

LRU_BLOCKS = 16
CONV_WIDTH = 4
LRU_C = 8.0
SGU_GROUPS = 8
CHUNK = 128
N_GROUPS = 4
EXPERTS_PER_GROUP = 8
N_EXPERTS = N_GROUPS * EXPERTS_PER_GROUP
TOP_K = 2
EPS = 1e-6

V7X_MXU_DIM = 256
V7X_SUBLANES = 8
V7X_LANES = 128
V7X_VMEM_BYTES = 64 * 1024 * 1024
V7X_SC_CORES = 2
V7X_SC_SUBCORES = 16

EXPERT_LOOKAHEAD = 3
ROUTER_SUBBLOCKS = 8
EXPERT_SUBBLOCKS = 2
PLE_SUBBLOCKS = 4
ROUTER_ROWS = V7X_LANES
EXPERT_ROW0 = V7X_SUBLANES


def _tiles():
    return dict(
        mixer_rows=256,
        expert_rows=512,
        ple_rows=1024,
        sc_rows=128,
        router_rows=4096,
        mixer_vmem=52 * 1024 * 1024,
        expert_vmem=40 * 1024 * 1024,
        ple_vmem=48 * 1024 * 1024,
        router_vmem=40 * 1024 * 1024,
    )


def _dot(a, b):
    return jnp.dot(a, b, preferred_element_type=F32)


def _sigmoid(x):
    return 0.5 * jnp.tanh(0.5 * x) + 0.5


def _rmsnorm(x, g):
    ms = jnp.mean(x * x, axis=-1, keepdims=True)
    return x * lax.rsqrt(ms + EPS) * g


def _pack_bf16_pair(lo, hi):
    lo_b = lax.bitcast_convert_type(lo.astype(BF16).astype(F32), U32)
    hi_b = lax.bitcast_convert_type(hi.astype(BF16).astype(F32), U32)
    return (hi_b & jnp.uint32(0xFFFF0000)) | lax.shift_right_logical(lo_b, jnp.uint32(16))


def _unpack_bf16_pair(w):
    lo = lax.bitcast_convert_type(lax.shift_left(w, jnp.uint32(16)), F32)
    hi = lax.bitcast_convert_type(w & jnp.uint32(0xFFFF0000), F32)
    return lo, hi


def _const_spec(shape):
    zeros = (0,) * len(shape)
    return pl.BlockSpec(shape, lambda *_: zeros, pipeline_mode=pl.Buffered(1))


def _tile_copies(hbm, buf, sem, b, row0, slot, to_hbm):
    group = buf.shape[1]
    copies = []
    for r in range(V7X_SUBLANES):
        hbm_rows = hbm.at[b, pl.ds(row0 + group * r, group), :]
        vmem_rows = buf.at[slot, :, r, :]
        src, dst = (vmem_rows, hbm_rows) if to_hbm else (hbm_rows, vmem_rows)
        copies.append(pltpu.make_async_copy(src, dst, sem.at[slot]))
    return copies


def _lru_scan(a, u, h0):
    group = a.shape[0]
    acc_a = [a[0]]
    acc_u = [u[0]]
    for g in range(1, group):
        acc_a.append(a[g] * acc_a[-1])
        acc_u.append(a[g] * acc_u[-1] + u[g])
    end_a, end_u = acc_a[-1], acc_u[-1]
    sub = lax.broadcasted_iota(jnp.int32, end_a.shape, 0)
    shift = 1
    while shift < V7X_SUBLANES:
        keep = sub >= shift
        a_sh = pltpu.roll(end_a, shift, axis=0)
        u_sh = pltpu.roll(end_u, shift, axis=0)
        end_u = jnp.where(keep, end_a * u_sh + end_u, end_u)
        end_a = jnp.where(keep, end_a * a_sh, end_a)
        shift *= 2
    h_end = end_a * h0 + end_u
    h_in = jnp.where(sub == 0, h0, pltpu.roll(h_end, 1, axis=0))
    out = [acc_a[g] * h_in + acc_u[g] for g in range(group)]
    return jnp.stack(out, axis=0), h_end[V7X_SUBLANES - 1:V7X_SUBLANES, :]


def _mixer_kernel(x_hbm, mixn_ref, win_hbm, convw_ref, convb_ref, wa_ref, ba_ref, wi_ref, bi_ref,
                  lam_ref, lng_ref, lnb_ref, ws_ref, bsp_ref, wout_hbm, ffn_ref,
                  x1_hbm, hp_hbm,
                  xbuf, z0_ref, z1_ref, x1buf, hpbuf, xsem, x1sem, hpsem, wsm_ref, ztail_ref, hcar_ref,
                  win_ref, wout_ref, wsem,
                  *, nseq):
    j = pl.program_id(0)
    ntile = pl.num_programs(0) - 1
    _, group, _, d = xbuf.shape
    rows = group * V7X_SUBLANES
    half = d // 2
    ta = jnp.minimum(j, ntile - 1)
    tb = jnp.maximum(j - 1, 0)
    s = lax.rem(tb, nseq)
    slot = lax.rem(tb, 2)

    def fetch(t):
        return _tile_copies(x_hbm, xbuf, xsem, lax.div(t, nseq), lax.rem(t, nseq) * rows,
                            lax.rem(t, 3), to_hbm=False)

    def put(t):
        tb_, ts_, sl = lax.div(t, nseq), lax.rem(t, nseq) * rows, lax.rem(t, 2)
        return (_tile_copies(x1_hbm, x1buf, x1sem, tb_, ts_, sl, to_hbm=True)
                + _tile_copies(hp_hbm, hpbuf, hpsem, tb_, ts_, sl, to_hbm=True))

    @pl.when(j == 0)
    def _():
        for i, c in enumerate(fetch(0)):
            c.start(priority=i % 2)
        stage = (z0_ref, z1_ref)
        n_in_chunks = win_hbm.shape[0] // rows

        def win_copy(c):
            return pltpu.make_async_copy(win_hbm.at[pl.ds(c * rows, rows), :], stage[c % 2], wsem.at[c % 2])

        win_copy(0).start()
        for c in range(n_in_chunks):
            if c + 1 < n_in_chunks:
                win_copy(c + 1).start()
            win_copy(c).wait()
            plain = 4 * d
            win_ref[c * rows:(c + 1) * rows, :plain] = stage[c % 2][:, :plain].astype(BF16)
            win_ref[c * rows:(c + 1) * rows, plain:] = (0.5 * stage[c % 2][:, plain:]).astype(BF16)
        n_out_chunks = wout_hbm.shape[0] // rows
        out_copies = [pltpu.make_async_copy(wout_hbm.at[pl.ds(c * rows, rows), :],
                                            z0_ref.at[:, c * d:(c + 1) * d], wsem.at[0])
                      for c in range(n_out_chunks)]
        for cp in out_copies:
            cp.start()
        for cp in out_copies:
            cp.wait()
        for c in range(n_out_chunks):
            wout_ref[c * rows:(c + 1) * rows, :] = (0.25 * z0_ref[:, c * d:(c + 1) * d]).astype(BF16)
        i_idx = lax.broadcasted_iota(jnp.int32, (rows, rows), 0)
        j_idx = lax.broadcasted_iota(jnp.int32, (rows, rows), 1)
        t_i = group * lax.rem(i_idx, V7X_SUBLANES) + lax.div(i_idx, V7X_SUBLANES)
        t_j = group * lax.rem(j_idx, V7X_SUBLANES) + lax.div(j_idx, V7X_SUBLANES)
        keep = (t_i >= t_j) & (lax.div(t_i, CHUNK) == lax.div(t_j, CHUNK))
        pick_rows = jnp.where(t_i == j_idx, 1.0, 0.0).astype(BF16)
        pick_cols = jnp.where(i_idx == t_j, 1.0, 0.0).astype(BF16)
        reps = rows // CHUNK
        for g in range(SGU_GROUPS):
            w_chunk = ws_ref[g].astype(BF16)
            w_rows = jnp.concatenate([w_chunk] * reps, axis=1)
            w_full = jnp.concatenate([w_rows] * reps, axis=0)
            w_perm = _dot(_dot(pick_rows, w_full).astype(BF16), pick_cols)
            wsm_ref[g] = jnp.where(keep, w_perm, 0.0).astype(BF16)

    @pl.when(j + 1 < ntile)
    def _():
        for i, c in enumerate(fetch(j + 1)):
            c.start(priority=i % 2)

    @pl.when(j < ntile)
    def _():
        for c in fetch(j):
            c.wait()

    @pl.when(s == 0)
    def _():
        ztail_ref[...] = jnp.zeros_like(ztail_ref)
        hcar_ref[...] = jnp.zeros_like(hcar_ref)

    def compute(z_w, z_r, phase_a=True, phase_b=True):
        pw = d
        if phase_a:
            xa_in = xbuf[lax.rem(ta, 3)].reshape(rows, d)
            h_next = _rmsnorm(xa_in, mixn_ref[...]).astype(BF16)

        def project(k):
            if phase_a:
                z_w[:, k * pw:(k + 1) * pw] = _dot(h_next, win_ref[:, k * pw:(k + 1) * pw])

        if not phase_b:
            for k in range(win_ref.shape[1] // pw):
                project(k)
            return

        x = xbuf[lax.rem(tb, 3)].reshape(rows, d)

        def sec(k, c0, c1):
            return z_r[:, k * d + c0:k * d + c1]

        def one_plus_tanh_gelu(v):
            c = 0.7978845608028654
            return 1.0 + jnp.tanh(v * (c + (c * 0.044715) * (v * v)))

        cw = 0.5 * convw_ref[...]
        cb_h = 0.5 * convb_ref[...]
        ba_h = 0.5 * ba_ref[...]
        bi_h = 0.5 * bi_ref[...]
        neg_lam = -lam_ref[...]
        softplus = jnp.maximum(neg_lam, 0.0) + jnp.log1p(jnp.exp(-jnp.abs(neg_lam)))
        c_a = (-0.5 * LRU_C) * softplus
        blk = V7X_MXU_DIM
        sub3 = lax.broadcasted_iota(jnp.int32, (CONV_WIDTH - 1, V7X_SUBLANES, blk), 1)
        term_a = []
        for n in range(d // blk):
            if n % 2 == 0:
                project(n // 2)
            c0, c1 = n * blk, (n + 1) * blk
            z3 = sec(0, c0, c1).reshape(group, V7X_SUBLANES, blk)
            tail = z3[group - (CONV_WIDTH - 1):]
            halo = jnp.where(sub3 == 0, pltpu.roll(ztail_ref[:, :, c0:c1], 1, axis=1),
                             pltpu.roll(tail, 1, axis=1))
            ztail_ref[:, :, c0:c1] = tail
            zext = jnp.concatenate([halo, z3], axis=0)
            xa_h = cb_h[:, c0:c1] + cw[CONV_WIDTH - 1:CONV_WIDTH, c0:c1] * z3
            for k in range(1, CONV_WIDTH):
                lo = CONV_WIDTH - 1 - k
                xa_h = xa_h + cw[lo:lo + 1, c0:c1] * zext[lo:lo + group]
            xa2 = xa_h.reshape(rows, blk)
            xa_bf = xa2.astype(BF16)
            th_r = jnp.tanh(_dot(xa_bf, wa_ref[n]) + ba_h[:, c0:c1])
            th_i = jnp.tanh(_dot(xa_bf, wi_ref[n]) + bi_h[:, c0:c1])
            a = jnp.exp(c_a[:, c0:c1] + c_a[:, c0:c1] * th_r)
            u = jnp.sqrt(1.0 - a * a) * ((1.0 + th_i) * xa2)
            hseq, hlast = _lru_scan(a.reshape(group, V7X_SUBLANES, blk),
                                    u.reshape(group, V7X_SUBLANES, blk), hcar_ref[:, c0:c1])
            hcar_ref[:, c0:c1] = hlast
            zg = sec(1, c0, c1)
            term_a.append(((1.0 + jnp.tanh(sec(4, c0, c1))) * one_plus_tanh_gelu(zg))
                          * (zg * hseq.reshape(rows, blk)))

        project(2)
        zv = sec(3, 0, d)
        gv2 = zv * one_plus_tanh_gelu(zv)
        mu = jnp.mean(gv2, axis=-1, keepdims=True)
        xc = gv2 - mu
        var = jnp.mean(xc * xc, axis=-1, keepdims=True)
        v_bf = (xc * lax.rsqrt(var + 4.0 * EPS) * lng_ref[...] + lnb_ref[...]).astype(BF16)
        project(3)
        gdim = d // SGU_GROUPS
        term_b = []
        for g in range(SGU_GROUPS):
            c0, c1 = g * gdim, (g + 1) * gdim
            if g in (2, 5):
                project({2: 4, 5: 5}[g])
            sp = _dot(wsm_ref[g], v_bf[:, c0:c1]) + bsp_ref[:, g:g + 1]
            zu = sec(2, c0, c1)
            term_b.append(((1.0 + jnp.tanh(sec(5, c0, c1))) * one_plus_tanh_gelu(zu)) * (zu * sp))
        merged4 = jnp.concatenate(term_a, axis=1) + jnp.concatenate(term_b, axis=1)

        x1 = x + _dot(merged4.astype(BF16), wout_ref[...])

        hn = _rmsnorm(x1, ffn_ref[...])
        hp = _pack_bf16_pair(hn[:, :half], hn[:, half:])

        @pl.when(j >= 3)
        def _():
            for c in put(tb - 2):
                c.wait()

        x1buf[slot] = x1.reshape(group, V7X_SUBLANES, d)
        hpbuf[slot] = hp.reshape(group, V7X_SUBLANES, half)

        @pl.when(j >= 1)
        def _():
            for i, c in enumerate(put(tb)):
                c.start(priority=i % 2)

    middle = (j > 0) & (j < ntile)

    @pl.when(j == 0)
    def _():
        compute(z0_ref, z1_ref, phase_b=False)

    @pl.when(middle & (lax.rem(j, 2) == 0))
    def _():
        compute(z0_ref, z1_ref)

    @pl.when(middle & (lax.rem(j, 2) == 1))
    def _():
        compute(z1_ref, z0_ref)

    @pl.when(j == ntile)
    def _():
        compute(z0_ref, z1_ref, phase_a=False)

    @pl.when(j == ntile)
    def _():
        for c in put(tb):
            c.wait()

        @pl.when(ntile >= 2)
        def _():
            for c in put(tb - 1):
                c.wait()


def _mixer_call(x, mix_norm, w_in, conv_w, conv_b, wa_blk, ba, wi_blk, bi, lam, ln_g, ln_b, ws,
                bs_tile, w_out, ffn_norm):
    cfg = _tiles()
    bsz, seq, d = x.shape
    ts = cfg["mixer_rows"]
    group = ts // V7X_SUBLANES
    nseq = seq // ts
    ntile = bsz * nseq
    assert ntile % 2 == 0, "the last grid step reads the in-projection buffer of an odd tile"
    row1 = (1, d)
    in_specs = [
        pl.BlockSpec(memory_space=pl.ANY),
        _const_spec(row1),
        pl.BlockSpec(memory_space=pl.ANY),
        _const_spec(conv_w.shape), _const_spec(row1),
        _const_spec(wa_blk.shape), _const_spec(row1),
        _const_spec(wi_blk.shape), _const_spec(row1),
        _const_spec(row1),
        _const_spec(row1), _const_spec(row1),
        _const_spec(ws.shape), _const_spec(bs_tile.shape),
        pl.BlockSpec(memory_space=pl.ANY), _const_spec(row1),
    ]
    out_shape = [
        jax.ShapeDtypeStruct((bsz, seq, d), F32),
        jax.ShapeDtypeStruct((bsz, seq, d // 2), U32),
    ]
    out_specs = [
        pl.BlockSpec(memory_space=pl.ANY),
        pl.BlockSpec(memory_space=pl.ANY),
    ]
    scratch = [
        pltpu.VMEM((3, group, V7X_SUBLANES, d), F32),
        pltpu.VMEM((ts, w_in.shape[1]), F32),
        pltpu.VMEM((ts, w_in.shape[1]), F32),
        pltpu.VMEM((2, group, V7X_SUBLANES, d), F32),
        pltpu.VMEM((2, group, V7X_SUBLANES, d // 2), U32),
        pltpu.SemaphoreType.DMA((3,)),
        pltpu.SemaphoreType.DMA((2,)),
        pltpu.SemaphoreType.DMA((2,)),
        pltpu.VMEM((SGU_GROUPS, ts, ts), BF16),
        pltpu.VMEM((CONV_WIDTH - 1, V7X_SUBLANES, d), F32),
        pltpu.VMEM((1, d), F32),
        pltpu.VMEM(w_in.shape, BF16),
        pltpu.VMEM(w_out.shape, BF16),
        pltpu.SemaphoreType.DMA((2,)),
    ]
    return pl.pallas_call(
        functools.partial(_mixer_kernel, nseq=nseq),
        grid=(ntile + 1,),
        in_specs=in_specs,
        out_specs=out_specs,
        out_shape=out_shape,
        scratch_shapes=scratch,
        compiler_params=pltpu.CompilerParams(
            dimension_semantics=("arbitrary",),
            vmem_limit_bytes=cfg["mixer_vmem"]),
        name="mixer",
    )(x, mix_norm, w_in, conv_w, conv_b, wa_blk, ba, wi_blk, bi, lam, ln_g, ln_b, ws, bs_tile,
      w_out, ffn_norm)


def _router_kernel(hp_ref, wg_ref, we_ref, br_ref, pos_ref, gate_ref, cnt_ref, ccar_ref, wr_ref,
                   *, expert_capacity):
    rows = hp_ref.shape[0]

    @pl.when(pl.program_id(0) == 0)
    def _():
        ccar_ref[...] = jnp.zeros_like(ccar_ref)
        wr_ref[...] = jnp.zeros_like(wr_ref)
        wr_ref[:, 0:N_GROUPS] = wg_ref[...].astype(BF16)
        wr_ref[:, EXPERT_ROW0:EXPERT_ROW0 + N_EXPERTS] = we_ref[...].astype(BF16)

    sub_rows = rows // ROUTER_SUBBLOCKS
    lts = []
    for q in range(ROUTER_SUBBLOCKS):
        lo, hi = _unpack_bf16_pair(hp_ref[q * sub_rows:(q + 1) * sub_rows, :])
        hn = jnp.concatenate([lo, hi], axis=1)
        logits = _dot(hn.astype(BF16), wr_ref[...])
        lts.append(jnp.transpose(logits) + br_ref[...])
    sub = lax.broadcasted_iota(jnp.int32, (V7X_SUBLANES, sub_rows), 0)
    subf = sub.astype(F32)
    big = jnp.float32(1e9)
    eid = lax.broadcasted_iota(jnp.int32, (N_EXPERTS, sub_rows), 0).astype(F32)
    sb = V7X_MXU_DIM
    before = (lax.broadcasted_iota(jnp.int32, (sb, sb), 0)
              < lax.broadcasted_iota(jnp.int32, (sb, sb), 1))
    before = jnp.where(before, 1.0, 0.0).astype(BF16)
    cap = float(expert_capacity)
    zero = jnp.zeros((V7X_SUBLANES - TOP_K, sub_rows), F32)
    running = ccar_ref[:, 0:1]
    for q, lt in enumerate(lts):
        lg = jnp.where(sub < N_GROUPS, lt[0:V7X_SUBLANES, :], -jnp.inf)
        g_exp = jnp.exp(lg - jnp.max(lg, axis=0, keepdims=True))
        g_prob = g_exp / jnp.sum(g_exp, axis=0, keepdims=True)
        g_top = jnp.max(g_prob, axis=0, keepdims=True)
        g_idx = jnp.min(jnp.where(g_prob == g_top, subf, big), axis=0, keepdims=True)

        e_sel = jnp.zeros((EXPERTS_PER_GROUP, sub_rows), F32)
        for g in range(N_GROUPS):
            r0 = EXPERT_ROW0 + g * EXPERTS_PER_GROUP
            e_sel = jnp.where(g_idx == g, lt[r0:r0 + EXPERTS_PER_GROUP, :], e_sel)
        e_exp = jnp.exp(e_sel - jnp.max(e_sel, axis=0, keepdims=True))
        e_prob = e_exp / jnp.sum(e_exp, axis=0, keepdims=True)
        p1 = jnp.max(e_prob, axis=0, keepdims=True)
        i1 = jnp.min(jnp.where(e_prob == p1, subf, big), axis=0, keepdims=True)
        rest = jnp.where(subf == i1, -1.0, e_prob)
        p2 = jnp.max(rest, axis=0, keepdims=True)
        i2 = jnp.min(jnp.where(rest == p2, subf, big), axis=0, keepdims=True)
        psum = p1 + p2
        gate1 = g_top * (p1 / psum)
        gate2 = g_top * (p2 / psum)
        gid1 = g_idx * EXPERTS_PER_GROUP + i1
        gid2 = g_idx * EXPERTS_PER_GROUP + i2

        hit1 = eid == gid1
        hit2 = eid == gid2
        cnt = jnp.where(hit1 | hit2, 1.0, 0.0)
        base = []
        for c in range(sub_rows // sb):
            part = cnt[:, c * sb:(c + 1) * sb]
            base.append(running + _dot(part.astype(BF16), before))
            running = running + jnp.sum(part, axis=1, keepdims=True)
        base = jnp.concatenate(base, axis=1)
        rank1 = jnp.sum(jnp.where(hit1, base, 0.0), axis=0, keepdims=True)
        rank2 = jnp.sum(jnp.where(hit2, base, 0.0), axis=0, keepdims=True)
        pos = jnp.concatenate([gid1 * cap + rank1, gid2 * cap + rank2, zero], axis=0)
        pos_ref[:, q * sub_rows:(q + 1) * sub_rows] = pos.astype(jnp.int32)
        gate_ref[q * sub_rows:(q + 1) * sub_rows, :] = jnp.transpose(
            jnp.concatenate([gate1, gate2, zero], axis=0))
    total = jnp.broadcast_to(running, ccar_ref.shape)
    ccar_ref[...] = total
    cnt_ref[...] = total


def _router_call(hp, w_group, w_expert, b_router):
    cfg = _tiles()
    ntok, half = hp.shape
    tr = cfg["router_rows"]
    return pl.pallas_call(
        functools.partial(_router_kernel, expert_capacity=ntok),
        grid=(ntok // tr,),
        in_specs=[
            pl.BlockSpec((tr, half), lambda i: (i, 0)),
            _const_spec(w_group.shape),
            _const_spec(w_expert.shape),
            _const_spec(b_router.shape),
        ],
        out_specs=[
            pl.BlockSpec((V7X_SUBLANES, tr), lambda i: (0, i)),
            pl.BlockSpec((tr, V7X_SUBLANES), lambda i: (i, 0)),
            pl.BlockSpec((N_EXPERTS, V7X_LANES), lambda i: (0, 0)),
        ],
        out_shape=[
            jax.ShapeDtypeStruct((V7X_SUBLANES, ntok), jnp.int32),
            jax.ShapeDtypeStruct((ntok, V7X_SUBLANES), F32),
            jax.ShapeDtypeStruct((N_EXPERTS, V7X_LANES), F32),
        ],
        scratch_shapes=[
            pltpu.VMEM((N_EXPERTS, V7X_LANES), F32),
            pltpu.VMEM((w_group.shape[0], ROUTER_ROWS), BF16),
        ],
        compiler_params=pltpu.CompilerParams(
            dimension_semantics=("arbitrary",),
            vmem_limit_bytes=cfg["router_vmem"]),
        name="router",
    )(hp, w_group, w_expert, b_router)


def _expert_kernel(nt_ref, base_ref, texp_ref, tloc_ref, hs_hbm, w1_ref, w3_ref, w2_ref, ys_hbm,
                   hbuf, ybuf, hsem, ysem, w1b_ref, w3b_ref, w2b_ref, *, capacity):
    e = pl.program_id(0)
    n_exp = pl.num_programs(0)
    nt = nt_ref[e]
    base = base_ref[e]
    total = base_ref[n_exp - 1] + nt_ref[n_exp - 1]
    n_in, tm, _ = hbuf.shape
    n_out = ybuf.shape[0]
    ahead = n_in - 1

    def load(g):
        slot = lax.rem(g, n_in)
        rows = pl.ds(texp_ref[g] * capacity + tloc_ref[g] * tm, tm)
        return pltpu.make_async_copy(hs_hbm.at[rows], hbuf.at[slot], hsem.at[slot])

    def store(t, slot):
        rows = pl.ds(e * capacity + t * tm, tm)
        return pltpu.make_async_copy(ybuf.at[slot], ys_hbm.at[rows], ysem.at[slot])

    @pl.when(e == 0)
    def _():
        for g0 in range(ahead):
            @pl.when(g0 < total)
            def _():
                load(g0).start(priority=1)

    w1b_ref[...] = w1_ref[...].astype(BF16)
    w3b_ref[...] = w3_ref[...].astype(BF16)
    w2b_ref[...] = w2_ref[...].astype(BF16)

    @pl.loop(0, nt)
    def _(t):
        g = base + t

        @pl.when(g + ahead < total)
        def _():
            load(g + ahead).start(priority=1)

        load(g).wait()
        slot = lax.rem(g, n_out)

        @pl.when(g >= n_out)
        def _():
            store(t, slot).wait()

        sub_rows = tm // EXPERT_SUBBLOCKS
        blocks = [pl.ds(q * sub_rows, sub_rows) for q in range(EXPERT_SUBBLOCKS)]
        rows_in = []
        for rs in blocks:
            lo, hi = _unpack_bf16_pair(hbuf[lax.rem(g, n_in), rs, :])
            rows_in.append(jnp.concatenate([lo, hi], axis=1).astype(BF16))
        up = [(_dot(h, w1b_ref[...]), _dot(h, w3b_ref[...])) for h in rows_in]
        down = []
        for a, b in up:
            hid = (a * _sigmoid(a)) * b
            down.append(_dot(hid.astype(BF16), w2b_ref[...]))
        for rs, y in zip(blocks, down):
            half = y.shape[1] // 2
            ybuf[slot, rs, :] = _pack_bf16_pair(y[:, :half], y[:, half:])
        store(t, slot).start()

    @pl.when(e + 1 == n_exp)
    def _():
        for back in range(1, n_out + 1):
            @pl.when(total >= back)
            def _():
                store(0, lax.rem(total - back, n_out)).wait()


def _expert_call(tiles_per_expert, hs, w1, w3, w2, capacity):
    cfg = _tiles()
    tm = cfg["expert_rows"]
    prow, half = hs.shape
    n_exp, d, f = w1.shape
    ends = jnp.cumsum(tiles_per_expert)
    base = ends - tiles_per_expert
    g = jnp.arange(capacity * TOP_K // tm + n_exp, dtype=jnp.int32)
    texp = jnp.minimum(jnp.sum((ends[None, :] <= g[:, None]).astype(jnp.int32), axis=1), n_exp - 1)
    onehot = texp[:, None] == jnp.arange(n_exp, dtype=jnp.int32)[None, :]
    tloc = g - jnp.sum(jnp.where(onehot, base[None, :], 0), axis=1)

    def w_map(e, *_):
        return (e, 0, 0)

    grid_spec = pltpu.PrefetchScalarGridSpec(
        num_scalar_prefetch=4,
        grid=(n_exp,),
        in_specs=[
            pl.BlockSpec(memory_space=pl.ANY),
            pl.BlockSpec((None, d, f), w_map),
            pl.BlockSpec((None, d, f), w_map),
            pl.BlockSpec((None, f, d), w_map),
        ],
        out_specs=pl.BlockSpec(memory_space=pl.ANY),
        scratch_shapes=[
            pltpu.VMEM((EXPERT_LOOKAHEAD + 1, tm, half), U32),
            pltpu.VMEM((2, tm, half), U32),
            pltpu.SemaphoreType.DMA((EXPERT_LOOKAHEAD + 1,)),
            pltpu.SemaphoreType.DMA((2,)),
            pltpu.VMEM((d, f), BF16),
            pltpu.VMEM((d, f), BF16),
            pltpu.VMEM((f, d), BF16),
        ],
    )
    return pl.pallas_call(
        functools.partial(_expert_kernel, capacity=capacity),
        grid_spec=grid_spec,
        out_shape=jax.ShapeDtypeStruct((prow, half), U32),
        compiler_params=pltpu.CompilerParams(
            dimension_semantics=("arbitrary",),
            vmem_limit_bytes=cfg["expert_vmem"]),
        name="experts",
    )(tiles_per_expert, base, texp, tloc, hs, w1, w3, w2)


def _sc_mesh():
    return plsc.VectorSubcoreMesh(core_axis_name="c", subcore_axis_name="s",
                                  num_cores=V7X_SC_CORES, num_subcores=V7X_SC_SUBCORES)


def _sc_worker_id():
    return lax.axis_index("s") * V7X_SC_CORES + lax.axis_index("c")


def _dispatch_call(hp, pos_w, out_rows):
    cfg = _tiles()
    ntok, half = hp.shape
    nw, topk, nch, ch = pos_w.shape
    per_w = nch * ch

    def body(hp_hbm, pos_hbm, hs_hbm, idx_v, buf, wsem):
        wid = _sc_worker_id()
        pltpu.sync_copy(pos_hbm.at[wid], idx_v)

        for c in range(nch):
            pltpu.sync_copy(hp_hbm.at[pl.ds(wid * per_w + c * ch, ch)], buf)
            writes = [pltpu.make_async_copy(buf, hs_hbm.at[idx_v.at[k, c]], wsem.at[k]) for k in range(topk)]
            for w in writes:
                w.start()
            for w in writes:
                w.wait()

    assert nw == V7X_SC_CORES * V7X_SC_SUBCORES and nw * per_w == ntok and ch == cfg["sc_rows"]
    return pl.kernel(
        body,
        out_type=jax.ShapeDtypeStruct((out_rows, half), U32),
        mesh=_sc_mesh(),
        scratch_types=[
            pltpu.VMEM((topk, nch, ch), jnp.int32),
            pltpu.VMEM((ch, half), U32),
            pltpu.SemaphoreType.DMA((topk,)),
        ],
        name="dispatch",
    )(hp, pos_w)


def _combine_call(ys, pos_w):
    cfg = _tiles()
    _, half = ys.shape
    nw, topk, nch, ch = pos_w.shape
    per_w = nch * ch
    ntok = nw * per_w

    def body(ys_hbm, pos_hbm, *rest):
        outs = rest[:topk]
        idx_v, buf = rest[topk:]
        wid = _sc_worker_id()
        pltpu.sync_copy(pos_hbm.at[wid], idx_v)
        for c in range(nch):
            for k in range(topk):
                pltpu.sync_copy(ys_hbm.at[idx_v.at[k, c]], buf)
                pltpu.sync_copy(buf, outs[k].at[pl.ds(wid * per_w + c * ch, ch)])

    assert nw == V7X_SC_CORES * V7X_SC_SUBCORES and ch == cfg["sc_rows"]
    return pl.kernel(
        body,
        out_type=[jax.ShapeDtypeStruct((ntok, half), U32)] * topk,
        mesh=_sc_mesh(),
        scratch_types=[
            pltpu.VMEM((topk, nch, ch), jnp.int32),
            pltpu.VMEM((ch, half), U32),
        ],
        name="combine",
    )(ys, pos_w)


def _ple_kernel(x1_ref, yg0_ref, yg1_ref, gate_ref, p_ref, plen_ref, wg32_ref, wu32_ref, fin_ref, o_ref,
                wg_ref, wu_ref):
    @pl.when(pl.program_id(0) == 0)
    def _():
        wg_ref[...] = (0.5 * wg32_ref[...]).astype(BF16)
        wu_ref[...] = (0.5 * wu32_ref[...]).astype(BF16)

    rows = x1_ref.shape[0]
    sub_rows = rows // PLE_SUBBLOCKS
    for q in range(PLE_SUBBLOCKS):
        rs = pl.ds(q * sub_rows, sub_rows)
        lo0, hi0 = _unpack_bf16_pair(yg0_ref[rs, :])
        lo1, hi1 = _unpack_bf16_pair(yg1_ref[rs, :])
        g0 = gate_ref[rs, 0:1]
        g1 = gate_ref[rs, 1:2]
        moe = g0 * jnp.concatenate([lo0, hi0], axis=1) + g1 * jnp.concatenate([lo1, hi1], axis=1)
        x2 = x1_ref[rs, :] + moe
        r = _rmsnorm(x2, plen_ref[...]).astype(BF16)
        gt2 = 1.0 + jnp.tanh(_dot(r, wg_ref[...]))
        up_h = _dot(p_ref[rs, :].astype(BF16), wu_ref[...])
        x3 = x2 + gt2 * up_h
        o_ref[rs, :] = _rmsnorm(x3, fin_ref[...])


def _ple_call(x1, yg0, yg1, gates, p, ple_norm, wg, wu, final_norm):
    cfg = _tiles()
    ntok, d = x1.shape
    tp = cfg["ple_rows"]
    pdim = p.shape[1]
    return pl.pallas_call(
        _ple_kernel,
        grid=(ntok // tp,),
        in_specs=[
            pl.BlockSpec((tp, d), lambda i: (i, 0)),
            pl.BlockSpec((tp, d // 2), lambda i: (i, 0)),
            pl.BlockSpec((tp, d // 2), lambda i: (i, 0)),
            pl.BlockSpec((tp, V7X_SUBLANES), lambda i: (i, 0)),
            pl.BlockSpec((tp, pdim), lambda i: (i, 0)),
            _const_spec((1, d)),
            _const_spec(wg.shape),
            _const_spec(wu.shape),
            _const_spec((1, d)),
        ],
        out_specs=pl.BlockSpec((tp, d), lambda i: (i, 0)),
        out_shape=jax.ShapeDtypeStruct((ntok, d), F32),
        scratch_shapes=[pltpu.VMEM(wg.shape, BF16), pltpu.VMEM(wu.shape, BF16)],
        compiler_params=pltpu.CompilerParams(
            dimension_semantics=("arbitrary",),
            vmem_limit_bytes=cfg["ple_vmem"]),
        name="ple",
    )(x1, yg0, yg1, gates, p, ple_norm, wg, wu, final_norm)


def _blockdiag_pack(w):
    nb, bd, _ = w.shape
    per = V7X_MXU_DIM // bd
    w4 = w.reshape(nb // per, per, bd, bd)
    eye = jnp.eye(per, dtype=w.dtype)
    out = jnp.einsum("jpab,pq->jpaqb", w4, eye)
    return out.reshape(nb // per, V7X_MXU_DIM, V7X_MXU_DIM).astype(BF16)


def kernel(x, p, mix_norm, w_in, conv_w, conv_b, lru_wa, lru_ba, lru_wi, lru_bi, lru_lambda, sgu_ln_g, sgu_ln_b, sgu_ws, sgu_bs, w_out, ffn_norm, router_group_w, router_group_b, router_expert_w, router_expert_b, expert_w1, expert_w3, expert_w2, ple_norm, ple_gate_w, ple_up_w, final_norm):
    cfg = _tiles()
    bsz, seq, d = x.shape
    ntok = bsz * seq
    tm = cfg["expert_rows"]
    depth = w_in.shape[0]
    assert depth == 1, "the ple kernel applies the final norm, so it must be the last layer"
    l = 0
    nw_rows = V7X_SC_CORES * V7X_SC_SUBCORES * cfg["sc_rows"]
    assert cfg["mixer_rows"] % CHUNK == 0 and seq % cfg["mixer_rows"] == 0
    assert ntok % cfg["router_rows"] == 0 and ntok % cfg["ple_rows"] == 0 and ntok % nw_rows == 0
    assert ntok % tm == 0 and lru_wa.shape[1:] == (LRU_BLOCKS, d // LRU_BLOCKS, d // LRU_BLOCKS)
    assert max(cfg[k] for k in cfg if k.endswith("_vmem")) < V7X_VMEM_BYTES
    b_router = jnp.concatenate([
        router_group_b[l], jnp.zeros((EXPERT_ROW0 - N_GROUPS,), F32), router_expert_b[l],
        jnp.zeros((ROUTER_ROWS - EXPERT_ROW0 - N_EXPERTS,), F32)])[:, None]
    ts = cfg["mixer_rows"]
    group = ts // V7X_SUBLANES
    bs_tile = jnp.tile(sgu_bs[l], (1, ts // CHUNK)).reshape(SGU_GROUPS, V7X_SUBLANES, group)
    bs_tile = jnp.transpose(bs_tile, (2, 1, 0)).reshape(ts, SGU_GROUPS)
    x1, hp = _mixer_call(
        x, mix_norm[l][None], w_in[l], conv_w[l], conv_b[l][None],
        _blockdiag_pack(lru_wa[l]), lru_ba[l][None], _blockdiag_pack(lru_wi[l]), lru_bi[l][None],
        lru_lambda[l][None], sgu_ln_g[l][None], sgu_ln_b[l][None], sgu_ws[l], bs_tile,
        w_out[l], ffn_norm[l][None])
    hp = hp.reshape(ntok, d // 2)
    pos, gate, cnt = _router_call(hp, router_group_w[l], router_expert_w[l], b_router)

    cap = ntok
    tiles_per_expert = (cnt[:, 0].astype(jnp.int32) + tm - 1) // tm
    nw = V7X_SC_CORES * V7X_SC_SUBCORES
    ch = cfg["sc_rows"]
    pos_w = jnp.transpose(pos[:TOP_K].reshape(TOP_K, nw, ntok // (nw * ch), ch), (1, 0, 2, 3))

    hs = _dispatch_call(hp, pos_w, N_EXPERTS * cap)
    ys = _expert_call(tiles_per_expert, hs, expert_w1[l], expert_w3[l], expert_w2[l], cap)
    yg0, yg1 = _combine_call(ys, pos_w)

    out = _ple_call(x1.reshape(ntok, d), yg0, yg1, gate, p[l].reshape(ntok, -1), ple_norm[l][None],
                    ple_gate_w[l], ple_up_w[l], final_norm[None])
    return out.reshape(bsz, seq, d)
```

```python
import functools

import jax
import jax.numpy as jnp
from jax import lax
from jax.experimental import pallas as pl
from jax.experimental.pallas import tpu as pltpu
from jax.experimental.pallas import tpu_sc as plsc

F32 = jnp.float32
BF16 = jnp.bfloat16
U32 = jnp.uint32

LRU_BLOCKS = 16
CONV_WIDTH = 4
LRU_C = 8.0
SGU_GROUPS = 8
CHUNK = 128
N_GROUPS = 4
EXPERTS_PER_GROUP = 8
N_EXPERTS = N_GROUPS * EXPERTS_PER_GROUP
TOP_K = 2
EPS = 1e-6

V7X_MXU_DIM = 256
V7X_SUBLANES = 8
V7X_LANES = 128
V7X_VMEM_BYTES = 64 * 1024 * 1024
V7X_SC_CORES = 2
V7X_SC_SUBCORES = 16

EXPERT_LOOKAHEAD = 3
ROUTER_SUBBLOCKS = 8
EXPERTS_PER_STEP = 2
EXPERT_SUBBLOCKS = 2
PLE_SUBBLOCKS = 4
ROUTER_ROWS = V7X_LANES
EXPERT_ROW0 = V7X_SUBLANES


def _tiles():
    return dict(
        mixer_rows=256,
        expert_rows=512,
        ple_rows=1024,
        sc_rows=128,
        router_rows=4096,
        mixer_vmem=52 * 1024 * 1024,
        expert_vmem=40 * 1024 * 1024,
        ple_vmem=48 * 1024 * 1024,
        router_vmem=40 * 1024 * 1024,
    )


def _dot(a, b):
    return jnp.dot(a, b, preferred_element_type=F32)


def _sigmoid(x):
    return 0.5 * jnp.tanh(0.5 * x) + 0.5


def _rmsnorm(x, g):
    ms = jnp.mean(x * x, axis=-1, keepdims=True)
    return x * lax.rsqrt(ms + EPS) * g


def _pack_bf16_pair(lo, hi):
    lo_b = lax.bitcast_convert_type(lo.astype(BF16).astype(F32), U32)
    hi_b = lax.bitcast_convert_type(hi.astype(BF16).astype(F32), U32)
    return (hi_b & jnp.uint32(0xFFFF0000)) | lax.shift_right_logical(lo_b, jnp.uint32(16))


def _unpack_bf16_pair(w):
    lo = lax.bitcast_convert_type(lax.shift_left(w, jnp.uint32(16)), F32)
    hi = lax.bitcast_convert_type(w & jnp.uint32(0xFFFF0000), F32)
    return lo, hi


def _const_spec(shape):
    zeros = (0,) * len(shape)
    return pl.BlockSpec(shape, lambda *_: zeros, pipeline_mode=pl.Buffered(1))


def _tile_copies(hbm, buf, sem, b, row0, slot, to_hbm):
    group = buf.shape[1]
    copies = []
    for r in range(V7X_SUBLANES):
        hbm_rows = hbm.at[b, pl.ds(row0 + group * r, group), :]
        vmem_rows = buf.at[slot, :, r, :]
        src, dst = (vmem_rows, hbm_rows) if to_hbm else (hbm_rows, vmem_rows)
        copies.append(pltpu.make_async_copy(src, dst, sem.at[slot]))
    return copies


def _lru_scan(a, u, h0):
    group = a.shape[0]
    acc_a = [a[0]]
    acc_u = [u[0]]
    for g in range(1, group):
        acc_a.append(a[g] * acc_a[-1])
        acc_u.append(a[g] * acc_u[-1] + u[g])
    end_a, end_u = acc_a[-1], acc_u[-1]
    sub = lax.broadcasted_iota(jnp.int32, end_a.shape, 0)
    shift = 1
    while shift < V7X_SUBLANES:
        keep = sub >= shift
        a_sh = pltpu.roll(end_a, shift, axis=0)
        u_sh = pltpu.roll(end_u, shift, axis=0)
        end_u = jnp.where(keep, end_a * u_sh + end_u, end_u)
        end_a = jnp.where(keep, end_a * a_sh, end_a)
        shift *= 2
    h_end = end_a * h0 + end_u
    h_in = jnp.where(sub == 0, h0, pltpu.roll(h_end, 1, axis=0))
    out = [acc_a[g] * h_in + acc_u[g] for g in range(group)]
    return jnp.stack(out, axis=0), h_end[V7X_SUBLANES - 1:V7X_SUBLANES, :]


def _mixer_kernel(x_hbm, mixn_ref, win_hbm, convw_ref, convb_ref, wa_ref, ba_ref, wi_ref, bi_ref,
                  lam_ref, lng_ref, lnb_ref, ws_ref, bsp_ref, wout_hbm, ffn_ref,
                  x1_hbm, hp_hbm,
                  xbuf, z0_ref, z1_ref, x1buf, hpbuf, xsem, x1sem, hpsem, wsm_ref, ztail_ref, hcar_ref,
                  win_ref, wout_ref, wsem,
                  *, nseq):
    j = pl.program_id(0)
    ntile = pl.num_programs(0) - 1
    _, group, _, d = xbuf.shape
    rows = group * V7X_SUBLANES
    half = d // 2
    ta = jnp.minimum(j, ntile - 1)
    tb = jnp.maximum(j - 1, 0)
    s = lax.rem(tb, nseq)
    slot = lax.rem(tb, 2)

    def fetch(t):
        return _tile_copies(x_hbm, xbuf, xsem, lax.div(t, nseq), lax.rem(t, nseq) * rows,
                            lax.rem(t, 3), to_hbm=False)

    def put(t):
        tb_, ts_, sl = lax.div(t, nseq), lax.rem(t, nseq) * rows, lax.rem(t, 2)
        return (_tile_copies(x1_hbm, x1buf, x1sem, tb_, ts_, sl, to_hbm=True)
                + _tile_copies(hp_hbm, hpbuf, hpsem, tb_, ts_, sl, to_hbm=True))

    @pl.when(j == 0)
    def _():
        for i, c in enumerate(fetch(0)):
            c.start(priority=i % 2)
        stage = (z0_ref, z1_ref)
        n_in_chunks = win_hbm.shape[0] // rows

        def win_copy(c):
            return pltpu.make_async_copy(win_hbm.at[pl.ds(c * rows, rows), :], stage[c % 2], wsem.at[c % 2])

        win_copy(0).start()
        for c in range(n_in_chunks):
            if c + 1 < n_in_chunks:
                win_copy(c + 1).start()
            win_copy(c).wait()
            plain = 4 * d
            win_ref[c * rows:(c + 1) * rows, :plain] = stage[c % 2][:, :plain].astype(BF16)
            win_ref[c * rows:(c + 1) * rows, plain:] = (0.5 * stage[c % 2][:, plain:]).astype(BF16)
        n_out_chunks = wout_hbm.shape[0] // rows
        out_copies = [pltpu.make_async_copy(wout_hbm.at[pl.ds(c * rows, rows), :],
                                            z0_ref.at[:, c * d:(c + 1) * d], wsem.at[0])
                      for c in range(n_out_chunks)]
        for cp in out_copies:
            cp.start()
        for cp in out_copies:
            cp.wait()
        for c in range(n_out_chunks):
            wout_ref[c * rows:(c + 1) * rows, :] = (0.25 * z0_ref[:, c * d:(c + 1) * d]).astype(BF16)
        i_idx = lax.broadcasted_iota(jnp.int32, (rows, rows), 0)
        j_idx = lax.broadcasted_iota(jnp.int32, (rows, rows), 1)
        t_i = group * lax.rem(i_idx, V7X_SUBLANES) + lax.div(i_idx, V7X_SUBLANES)
        t_j = group * lax.rem(j_idx, V7X_SUBLANES) + lax.div(j_idx, V7X_SUBLANES)
        keep = (t_i >= t_j) & (lax.div(t_i, CHUNK) == lax.div(t_j, CHUNK))
        pick_rows = jnp.where(t_i == j_idx, 1.0, 0.0).astype(BF16)
        pick_cols = jnp.where(i_idx == t_j, 1.0, 0.0).astype(BF16)
        reps = rows // CHUNK
        for g in range(SGU_GROUPS):
            w_chunk = ws_ref[g].astype(BF16)
            w_rows = jnp.concatenate([w_chunk] * reps, axis=1)
            w_full = jnp.concatenate([w_rows] * reps, axis=0)
            w_perm = _dot(_dot(pick_rows, w_full).astype(BF16), pick_cols)
            wsm_ref[g] = jnp.where(keep, w_perm, 0.0).astype(BF16)

    @pl.when(j + 1 < ntile)
    def _():
        for i, c in enumerate(fetch(j + 1)):
            c.start(priority=i % 2)

    @pl.when(j < ntile)
    def _():
        for c in fetch(j):
            c.wait()

    @pl.when(s == 0)
    def _():
        ztail_ref[...] = jnp.zeros_like(ztail_ref)
        hcar_ref[...] = jnp.zeros_like(hcar_ref)

    def compute(z_w, z_r, phase_a=True, phase_b=True):
        pw = d
        if phase_a:
            xa_in = xbuf[lax.rem(ta, 3)].reshape(rows, d)
            h_next = _rmsnorm(xa_in, mixn_ref[...]).astype(BF16)

        def project(k):
            if phase_a:
                z_w[:, k * pw:(k + 1) * pw] = _dot(h_next, win_ref[:, k * pw:(k + 1) * pw])

        if not phase_b:
            for k in range(win_ref.shape[1] // pw):
                project(k)
            return

        x = xbuf[lax.rem(tb, 3)].reshape(rows, d)

        def sec(k, c0, c1):
            return z_r[:, k * d + c0:k * d + c1]

        def one_plus_tanh_gelu(v):
            c = 0.7978845608028654
            return 1.0 + jnp.tanh(v * (c + (c * 0.044715) * (v * v)))

        cw = 0.5 * convw_ref[...]
        cb_h = 0.5 * convb_ref[...]
        ba_h = 0.5 * ba_ref[...]
        bi_h = 0.5 * bi_ref[...]
        neg_lam = -lam_ref[...]
        softplus = jnp.maximum(neg_lam, 0.0) + jnp.log1p(jnp.exp(-jnp.abs(neg_lam)))
        c_a = (-0.5 * LRU_C) * softplus
        blk = V7X_MXU_DIM
        sub3 = lax.broadcasted_iota(jnp.int32, (CONV_WIDTH - 1, V7X_SUBLANES, blk), 1)
        term_a = []
        for n in range(d // blk):
            if n % 2 == 0:
                project(n // 2)
            c0, c1 = n * blk, (n + 1) * blk
            z3 = sec(0, c0, c1).reshape(group, V7X_SUBLANES, blk)
            tail = z3[group - (CONV_WIDTH - 1):]
            halo = jnp.where(sub3 == 0, pltpu.roll(ztail_ref[:, :, c0:c1], 1, axis=1),
                             pltpu.roll(tail, 1, axis=1))
            ztail_ref[:, :, c0:c1] = tail
            zext = jnp.concatenate([halo, z3], axis=0)
            xa_h = cb_h[:, c0:c1] + cw[CONV_WIDTH - 1:CONV_WIDTH, c0:c1] * z3
            for k in range(1, CONV_WIDTH):
                lo = CONV_WIDTH - 1 - k
                xa_h = xa_h + cw[lo:lo + 1, c0:c1] * zext[lo:lo + group]
            xa2 = xa_h.reshape(rows, blk)
            xa_bf = xa2.astype(BF16)
            th_r = jnp.tanh(_dot(xa_bf, wa_ref[n]) + ba_h[:, c0:c1])
            th_i = jnp.tanh(_dot(xa_bf, wi_ref[n]) + bi_h[:, c0:c1])
            a = jnp.exp(c_a[:, c0:c1] + c_a[:, c0:c1] * th_r)
            u = jnp.sqrt(1.0 - a * a) * ((1.0 + th_i) * xa2)
            hseq, hlast = _lru_scan(a.reshape(group, V7X_SUBLANES, blk),
                                    u.reshape(group, V7X_SUBLANES, blk), hcar_ref[:, c0:c1])
            hcar_ref[:, c0:c1] = hlast
            zg = sec(1, c0, c1)
            term_a.append(((1.0 + jnp.tanh(sec(4, c0, c1))) * one_plus_tanh_gelu(zg))
                          * (zg * hseq.reshape(rows, blk)))

        project(2)
        zv = sec(3, 0, d)
        gv2 = zv * one_plus_tanh_gelu(zv)
        mu = jnp.mean(gv2, axis=-1, keepdims=True)
        xc = gv2 - mu
        var = jnp.mean(xc * xc, axis=-1, keepdims=True)
        v_bf = (xc * lax.rsqrt(var + 4.0 * EPS) * lng_ref[...] + lnb_ref[...]).astype(BF16)
        project(3)
        gdim = d // SGU_GROUPS
        term_b = []
        for g in range(SGU_GROUPS):
            c0, c1 = g * gdim, (g + 1) * gdim
            if g in (2, 5):
                project({2: 4, 5: 5}[g])
            sp = _dot(wsm_ref[g], v_bf[:, c0:c1]) + bsp_ref[:, g:g + 1]
            zu = sec(2, c0, c1)
            term_b.append(((1.0 + jnp.tanh(sec(5, c0, c1))) * one_plus_tanh_gelu(zu)) * (zu * sp))
        merged4 = jnp.concatenate(term_a, axis=1) + jnp.concatenate(term_b, axis=1)

        x1 = x + _dot(merged4.astype(BF16), wout_ref[...])

        hn = _rmsnorm(x1, ffn_ref[...])
        hp = _pack_bf16_pair(hn[:, :half], hn[:, half:])

        @pl.when(j >= 3)
        def _():
            for c in put(tb - 2):
                c.wait()

        x1buf[slot] = x1.reshape(group, V7X_SUBLANES, d)
        hpbuf[slot] = hp.reshape(group, V7X_SUBLANES, half)

        @pl.when(j >= 1)
        def _():
            for i, c in enumerate(put(tb)):
                c.start(priority=i % 2)

    middle = (j > 0) & (j < ntile)

    @pl.when(j == 0)
    def _():
        compute(z0_ref, z1_ref, phase_b=False)

    @pl.when(middle & (lax.rem(j, 2) == 0))
    def _():
        compute(z0_ref, z1_ref)

    @pl.when(middle & (lax.rem(j, 2) == 1))
    def _():
        compute(z1_ref, z0_ref)

    @pl.when(j == ntile)
    def _():
        compute(z0_ref, z1_ref, phase_a=False)

    @pl.when(j == ntile)
    def _():
        for c in put(tb):
            c.wait()

        @pl.when(ntile >= 2)
        def _():
            for c in put(tb - 1):
                c.wait()


def _mixer_call(x, mix_norm, w_in, conv_w, conv_b, wa_blk, ba, wi_blk, bi, lam, ln_g, ln_b, ws,
                bs_tile, w_out, ffn_norm):
    cfg = _tiles()
    bsz, seq, d = x.shape
    ts = cfg["mixer_rows"]
    group = ts // V7X_SUBLANES
    nseq = seq // ts
    ntile = bsz * nseq
    assert ntile % 2 == 0, "the last grid step reads the in-projection buffer of an odd tile"
    row1 = (1, d)
    in_specs = [
        pl.BlockSpec(memory_space=pl.ANY),
        _const_spec(row1),
        pl.BlockSpec(memory_space=pl.ANY),
        _const_spec(conv_w.shape), _const_spec(row1),
        _const_spec(wa_blk.shape), _const_spec(row1),
        _const_spec(wi_blk.shape), _const_spec(row1),
        _const_spec(row1),
        _const_spec(row1), _const_spec(row1),
        _const_spec(ws.shape), _const_spec(bs_tile.shape),
        pl.BlockSpec(memory_space=pl.ANY), _const_spec(row1),
    ]
    out_shape = [
        jax.ShapeDtypeStruct((bsz, seq, d), F32),
        jax.ShapeDtypeStruct((bsz, seq, d // 2), U32),
    ]
    out_specs = [
        pl.BlockSpec(memory_space=pl.ANY),
        pl.BlockSpec(memory_space=pl.ANY),
    ]
    scratch = [
        pltpu.VMEM((3, group, V7X_SUBLANES, d), F32),
        pltpu.VMEM((ts, w_in.shape[1]), F32),
        pltpu.VMEM((ts, w_in.shape[1]), F32),
        pltpu.VMEM((2, group, V7X_SUBLANES, d), F32),
        pltpu.VMEM((2, group, V7X_SUBLANES, d // 2), U32),
        pltpu.SemaphoreType.DMA((3,)),
        pltpu.SemaphoreType.DMA((2,)),
        pltpu.SemaphoreType.DMA((2,)),
        pltpu.VMEM((SGU_GROUPS, ts, ts), BF16),
        pltpu.VMEM((CONV_WIDTH - 1, V7X_SUBLANES, d), F32),
        pltpu.VMEM((1, d), F32),
        pltpu.VMEM(w_in.shape, BF16),
        pltpu.VMEM(w_out.shape, BF16),
        pltpu.SemaphoreType.DMA((2,)),
    ]
    return pl.pallas_call(
        functools.partial(_mixer_kernel, nseq=nseq),
        grid=(ntile + 1,),
        in_specs=in_specs,
        out_specs=out_specs,
        out_shape=out_shape,
        scratch_shapes=scratch,
        compiler_params=pltpu.CompilerParams(
            dimension_semantics=("arbitrary",),
            vmem_limit_bytes=cfg["mixer_vmem"]),
        name="mixer",
    )(x, mix_norm, w_in, conv_w, conv_b, wa_blk, ba, wi_blk, bi, lam, ln_g, ln_b, ws, bs_tile,
      w_out, ffn_norm)


def _router_kernel(hp_ref, wg_ref, we_ref, br_ref, pos_ref, gate_ref, cnt_ref, ccar_ref, wr_ref,
                   *, expert_capacity):
    rows = hp_ref.shape[0]

    @pl.when(pl.program_id(0) == 0)
    def _():
        ccar_ref[...] = jnp.zeros_like(ccar_ref)
        wr_ref[...] = jnp.zeros_like(wr_ref)
        wr_ref[:, 0:N_GROUPS] = wg_ref[...].astype(BF16)
        wr_ref[:, EXPERT_ROW0:EXPERT_ROW0 + N_EXPERTS] = we_ref[...].astype(BF16)

    sub_rows = rows // ROUTER_SUBBLOCKS
    lts = []
    for q in range(ROUTER_SUBBLOCKS):
        lo, hi = _unpack_bf16_pair(hp_ref[q * sub_rows:(q + 1) * sub_rows, :])
        hn = jnp.concatenate([lo, hi], axis=1)
        logits = _dot(hn.astype(BF16), wr_ref[...])
        lts.append(jnp.transpose(logits) + br_ref[...])
    sub = lax.broadcasted_iota(jnp.int32, (V7X_SUBLANES, sub_rows), 0)
    subf = sub.astype(F32)
    big = jnp.float32(1e9)
    eid = lax.broadcasted_iota(jnp.int32, (N_EXPERTS, sub_rows), 0).astype(F32)
    sb = V7X_MXU_DIM
    before = (lax.broadcasted_iota(jnp.int32, (sb, sb), 0)
              < lax.broadcasted_iota(jnp.int32, (sb, sb), 1))
    before = jnp.where(before, 1.0, 0.0).astype(BF16)
    cap = float(expert_capacity)
    zero = jnp.zeros((V7X_SUBLANES - TOP_K, sub_rows), F32)
    running = ccar_ref[:, 0:1]
    for q, lt in enumerate(lts):
        lg = jnp.where(sub < N_GROUPS, lt[0:V7X_SUBLANES, :], -jnp.inf)
        g_exp = jnp.exp(lg - jnp.max(lg, axis=0, keepdims=True))
        g_prob = g_exp / jnp.sum(g_exp, axis=0, keepdims=True)
        g_top = jnp.max(g_prob, axis=0, keepdims=True)
        g_idx = jnp.min(jnp.where(g_prob == g_top, subf, big), axis=0, keepdims=True)

        e_sel = jnp.zeros((EXPERTS_PER_GROUP, sub_rows), F32)
        for g in range(N_GROUPS):
            r0 = EXPERT_ROW0 + g * EXPERTS_PER_GROUP
            e_sel = jnp.where(g_idx == g, lt[r0:r0 + EXPERTS_PER_GROUP, :], e_sel)
        e_exp = jnp.exp(e_sel - jnp.max(e_sel, axis=0, keepdims=True))
        e_prob = e_exp / jnp.sum(e_exp, axis=0, keepdims=True)
        p1 = jnp.max(e_prob, axis=0, keepdims=True)
        i1 = jnp.min(jnp.where(e_prob == p1, subf, big), axis=0, keepdims=True)
        rest = jnp.where(subf == i1, -1.0, e_prob)
        p2 = jnp.max(rest, axis=0, keepdims=True)
        i2 = jnp.min(jnp.where(rest == p2, subf, big), axis=0, keepdims=True)
        psum = p1 + p2
        gate1 = g_top * (p1 / psum)
        gate2 = g_top * (p2 / psum)
        gid1 = g_idx * EXPERTS_PER_GROUP + i1
        gid2 = g_idx * EXPERTS_PER_GROUP + i2

        hit1 = eid == gid1
        hit2 = eid == gid2
        cnt = jnp.where(hit1 | hit2, 1.0, 0.0)
        base = []
        for c in range(sub_rows // sb):
            part = cnt[:, c * sb:(c + 1) * sb]
            base.append(running + _dot(part.astype(BF16), before))
            running = running + jnp.sum(part, axis=1, keepdims=True)
        base = jnp.concatenate(base, axis=1)
        rank1 = jnp.sum(jnp.where(hit1, base, 0.0), axis=0, keepdims=True)
        rank2 = jnp.sum(jnp.where(hit2, base, 0.0), axis=0, keepdims=True)
        pos = jnp.concatenate([gid1 * cap + rank1, gid2 * cap + rank2, zero], axis=0)
        pos_ref[:, q * sub_rows:(q + 1) * sub_rows] = pos.astype(jnp.int32)
        gate_ref[q * sub_rows:(q + 1) * sub_rows, :] = jnp.transpose(
            jnp.concatenate([gate1, gate2, zero], axis=0))
    total = jnp.broadcast_to(running, ccar_ref.shape)
    ccar_ref[...] = total
    cnt_ref[...] = total


def _router_call(hp, w_group, w_expert, b_router):
    cfg = _tiles()
    ntok, half = hp.shape
    tr = cfg["router_rows"]
    return pl.pallas_call(
        functools.partial(_router_kernel, expert_capacity=ntok),
        grid=(ntok // tr,),
        in_specs=[
            pl.BlockSpec((tr, half), lambda i: (i, 0)),
            _const_spec(w_group.shape),
            _const_spec(w_expert.shape),
            _const_spec(b_router.shape),
        ],
        out_specs=[
            pl.BlockSpec((V7X_SUBLANES, tr), lambda i: (0, i)),
            pl.BlockSpec((tr, V7X_SUBLANES), lambda i: (i, 0)),
            pl.BlockSpec((N_EXPERTS, V7X_LANES), lambda i: (0, 0)),
        ],
        out_shape=[
            jax.ShapeDtypeStruct((V7X_SUBLANES, ntok), jnp.int32),
            jax.ShapeDtypeStruct((ntok, V7X_SUBLANES), F32),
            jax.ShapeDtypeStruct((N_EXPERTS, V7X_LANES), F32),
        ],
        scratch_shapes=[
            pltpu.VMEM((N_EXPERTS, V7X_LANES), F32),
            pltpu.VMEM((w_group.shape[0], ROUTER_ROWS), BF16),
        ],
        compiler_params=pltpu.CompilerParams(
            dimension_semantics=("arbitrary",),
            vmem_limit_bytes=cfg["router_vmem"]),
        name="router",
    )(hp, w_group, w_expert, b_router)


def _expert_kernel(nt_ref, base_ref, texp_ref, tloc_ref, hs_hbm, w1_ref, w3_ref, w2_ref, ys_hbm,
                   hbuf, ybuf, hsem, ysem, w1b_ref, w3b_ref, w2b_ref, *, capacity):
    step = pl.program_id(0)
    per_step = w1_ref.shape[0]
    n_exp = pl.num_programs(0) * per_step
    total = base_ref[n_exp - 1] + nt_ref[n_exp - 1]
    n_in, tm, _ = hbuf.shape
    n_out = ybuf.shape[0]
    ahead = n_in - 1

    def load(g):
        slot = lax.rem(g, n_in)
        rows = pl.ds(texp_ref[g] * capacity + tloc_ref[g] * tm, tm)
        return pltpu.make_async_copy(hs_hbm.at[rows], hbuf.at[slot], hsem.at[slot])

    def store(e, t, slot):
        rows = pl.ds(e * capacity + t * tm, tm)
        return pltpu.make_async_copy(ybuf.at[slot], ys_hbm.at[rows], ysem.at[slot])

    @pl.when(step == 0)
    def _():
        for g0 in range(ahead):
            @pl.when(g0 < total)
            def _():
                load(g0).start(priority=1)

    for sub in range(per_step):
        e = step * per_step + sub
        nt = nt_ref[e]
        base = base_ref[e]
        w1b_ref[...] = w1_ref[sub].astype(BF16)
        w3b_ref[...] = w3_ref[sub].astype(BF16)
        w2b_ref[...] = w2_ref[sub].astype(BF16)

        @pl.loop(0, nt)
        def _(t):
            g = base + t

            @pl.when(g + ahead < total)
            def _():
                load(g + ahead).start(priority=1)

            load(g).wait()
            slot = lax.rem(g, n_out)

            @pl.when(g >= n_out)
            def _():
                store(e, t, slot).wait()

            sub_rows = tm // EXPERT_SUBBLOCKS
            blocks = [pl.ds(q * sub_rows, sub_rows) for q in range(EXPERT_SUBBLOCKS)]
            rows_in = []
            for rs in blocks:
                lo, hi = _unpack_bf16_pair(hbuf[lax.rem(g, n_in), rs, :])
                rows_in.append(jnp.concatenate([lo, hi], axis=1).astype(BF16))
            up = [(_dot(h, w1b_ref[...]), _dot(h, w3b_ref[...])) for h in rows_in]
            down = []
            for a, b in up:
                hid = (a * _sigmoid(a)) * b
                down.append(_dot(hid.astype(BF16), w2b_ref[...]))
            for rs, y in zip(blocks, down):
                half = y.shape[1] // 2
                ybuf[slot, rs, :] = _pack_bf16_pair(y[:, :half], y[:, half:])
            store(e, t, slot).start()

    @pl.when(step + 1 == pl.num_programs(0))
    def _():
        for back in range(1, n_out + 1):
            @pl.when(total >= back)
            def _():
                store(0, 0, lax.rem(total - back, n_out)).wait()


def _expert_call(tiles_per_expert, hs, w1, w3, w2, capacity):
    cfg = _tiles()
    tm = cfg["expert_rows"]
    prow, half = hs.shape
    n_exp, d, f = w1.shape
    ends = jnp.cumsum(tiles_per_expert)
    base = ends - tiles_per_expert
    g = jnp.arange(capacity * TOP_K // tm + n_exp, dtype=jnp.int32)
    texp = jnp.minimum(jnp.sum((ends[None, :] <= g[:, None]).astype(jnp.int32), axis=1), n_exp - 1)
    onehot = texp[:, None] == jnp.arange(n_exp, dtype=jnp.int32)[None, :]
    tloc = g - jnp.sum(jnp.where(onehot, base[None, :], 0), axis=1)

    def w_map(e, *_):
        return (e, 0, 0)

    grid_spec = pltpu.PrefetchScalarGridSpec(
        num_scalar_prefetch=4,
        grid=(n_exp // EXPERTS_PER_STEP,),
        in_specs=[
            pl.BlockSpec(memory_space=pl.ANY),
            pl.BlockSpec((EXPERTS_PER_STEP, d, f), w_map),
            pl.BlockSpec((EXPERTS_PER_STEP, d, f), w_map),
            pl.BlockSpec((EXPERTS_PER_STEP, f, d), w_map),
        ],
        out_specs=pl.BlockSpec(memory_space=pl.ANY),
        scratch_shapes=[
            pltpu.VMEM((EXPERT_LOOKAHEAD + 1, tm, half), U32),
            pltpu.VMEM((2, tm, half), U32),
            pltpu.SemaphoreType.DMA((EXPERT_LOOKAHEAD + 1,)),
            pltpu.SemaphoreType.DMA((2,)),
            pltpu.VMEM((d, f), BF16),
            pltpu.VMEM((d, f), BF16),
            pltpu.VMEM((f, d), BF16),
        ],
    )
    return pl.pallas_call(
        functools.partial(_expert_kernel, capacity=capacity),
        grid_spec=grid_spec,
        out_shape=jax.ShapeDtypeStruct((prow, half), U32),
        compiler_params=pltpu.CompilerParams(
            dimension_semantics=("arbitrary",),
            vmem_limit_bytes=cfg["expert_vmem"]),
        name="experts",
    )(tiles_per_expert, base, texp, tloc, hs, w1, w3, w2)


def _sc_mesh():
    return plsc.VectorSubcoreMesh(core_axis_name="c", subcore_axis_name="s",
                                  num_cores=V7X_SC_CORES, num_subcores=V7X_SC_SUBCORES)


def _sc_worker_id():
    return lax.axis_index("s") * V7X_SC_CORES + lax.axis_index("c")


def _dispatch_call(hp, pos_w, out_rows):
    cfg = _tiles()
    ntok, half = hp.shape
    nw, topk, nch, ch = pos_w.shape
    per_w = nch * ch

    def body(hp_hbm, pos_hbm, hs_hbm, idx_v, buf, wsem):
        wid = _sc_worker_id()
        pltpu.sync_copy(pos_hbm.at[wid], idx_v)

        for c in range(nch):
            pltpu.sync_copy(hp_hbm.at[pl.ds(wid * per_w + c * ch, ch)], buf)
            writes = [pltpu.make_async_copy(buf, hs_hbm.at[idx_v.at[k, c]], wsem.at[k]) for k in range(topk)]
            for w in writes:
                w.start()
            for w in writes:
                w.wait()

    assert nw == V7X_SC_CORES * V7X_SC_SUBCORES and nw * per_w == ntok and ch == cfg["sc_rows"]
    return pl.kernel(
        body,
        out_type=jax.ShapeDtypeStruct((out_rows, half), U32),
        mesh=_sc_mesh(),
        scratch_types=[
            pltpu.VMEM((topk, nch, ch), jnp.int32),
            pltpu.VMEM((ch, half), U32),
            pltpu.SemaphoreType.DMA((topk,)),
        ],
        name="dispatch",
    )(hp, pos_w)


def _combine_call(ys, pos_w):
    cfg = _tiles()
    _, half = ys.shape
    nw, topk, nch, ch = pos_w.shape
    per_w = nch * ch
    ntok = nw * per_w

    def body(ys_hbm, pos_hbm, *rest):
        outs = rest[:topk]
        idx_v, buf = rest[topk:]
        wid = _sc_worker_id()
        pltpu.sync_copy(pos_hbm.at[wid], idx_v)
        for c in range(nch):
            for k in range(topk):
                pltpu.sync_copy(ys_hbm.at[idx_v.at[k, c]], buf)
                pltpu.sync_copy(buf, outs[k].at[pl.ds(wid * per_w + c * ch, ch)])

    assert nw == V7X_SC_CORES * V7X_SC_SUBCORES and ch == cfg["sc_rows"]
    return pl.kernel(
        body,
        out_type=[jax.ShapeDtypeStruct((ntok, half), U32)] * topk,
        mesh=_sc_mesh(),
        scratch_types=[
            pltpu.VMEM((topk, nch, ch), jnp.int32),
            pltpu.VMEM((ch, half), U32),
        ],
        name="combine",
    )(ys, pos_w)


def _ple_kernel(x1_ref, yg0_ref, yg1_ref, gate_ref, p_ref, plen_ref, wg32_ref, wu32_ref, fin_ref, o_ref,
                wg_ref, wu_ref):
    @pl.when(pl.program_id(0) == 0)
    def _():
        wg_ref[...] = (0.5 * wg32_ref[...]).astype(BF16)
        wu_ref[...] = (0.5 * wu32_ref[...]).astype(BF16)

    rows = x1_ref.shape[0]
    sub_rows = rows // PLE_SUBBLOCKS
    for q in range(PLE_SUBBLOCKS):
        rs = pl.ds(q * sub_rows, sub_rows)
        lo0, hi0 = _unpack_bf16_pair(yg0_ref[rs, :])
        lo1, hi1 = _unpack_bf16_pair(yg1_ref[rs, :])
        g0 = gate_ref[rs, 0:1]
        g1 = gate_ref[rs, 1:2]
        moe = g0 * jnp.concatenate([lo0, hi0], axis=1) + g1 * jnp.concatenate([lo1, hi1], axis=1)
        x2 = x1_ref[rs, :] + moe
        r = _rmsnorm(x2, plen_ref[...]).astype(BF16)
        gt2 = 1.0 + jnp.tanh(_dot(r, wg_ref[...]))
        up_h = _dot(p_ref[rs, :].astype(BF16), wu_ref[...])
        x3 = x2 + gt2 * up_h
        o_ref[rs, :] = _rmsnorm(x3, fin_ref[...])


def _ple_call(x1, yg0, yg1, gates, p, ple_norm, wg, wu, final_norm):
    cfg = _tiles()
    ntok, d = x1.shape
    tp = cfg["ple_rows"]
    pdim = p.shape[1]
    return pl.pallas_call(
        _ple_kernel,
        grid=(ntok // tp,),
        in_specs=[
            pl.BlockSpec((tp, d), lambda i: (i, 0)),
            pl.BlockSpec((tp, d // 2), lambda i: (i, 0)),
            pl.BlockSpec((tp, d // 2), lambda i: (i, 0)),
            pl.BlockSpec((tp, V7X_SUBLANES), lambda i: (i, 0)),
            pl.BlockSpec((tp, pdim), lambda i: (i, 0)),
            _const_spec((1, d)),
            _const_spec(wg.shape),
            _const_spec(wu.shape),
            _const_spec((1, d)),
        ],
        out_specs=pl.BlockSpec((tp, d), lambda i: (i, 0)),
        out_shape=jax.ShapeDtypeStruct((ntok, d), F32),
        scratch_shapes=[pltpu.VMEM(wg.shape, BF16), pltpu.VMEM(wu.shape, BF16)],
        compiler_params=pltpu.CompilerParams(
            dimension_semantics=("arbitrary",),
            vmem_limit_bytes=cfg["ple_vmem"]),
        name="ple",
    )(x1, yg0, yg1, gates, p, ple_norm, wg, wu, final_norm)


def _blockdiag_pack(w):
    nb, bd, _ = w.shape
    per = V7X_MXU_DIM // bd
    w4 = w.reshape(nb // per, per, bd, bd)
    eye = jnp.eye(per, dtype=w.dtype)
    out = jnp.einsum("jpab,pq->jpaqb", w4, eye)
    return out.reshape(nb // per, V7X_MXU_DIM, V7X_MXU_DIM).astype(BF16)


def kernel(x, p, mix_norm, w_in, conv_w, conv_b, lru_wa, lru_ba, lru_wi, lru_bi, lru_lambda, sgu_ln_g, sgu_ln_b, sgu_ws, sgu_bs, w_out, ffn_norm, router_group_w, router_group_b, router_expert_w, router_expert_b, expert_w1, expert_w3, expert_w2, ple_norm, ple_gate_w, ple_up_w, final_norm):
    cfg = _tiles()
    bsz, seq, d = x.shape
    ntok = bsz * seq
    tm = cfg["expert_rows"]
    depth = w_in.shape[0]
    assert depth == 1, "the ple kernel applies the final norm, so it must be the last layer"
    l = 0
    nw_rows = V7X_SC_CORES * V7X_SC_SUBCORES * cfg["sc_rows"]
    assert cfg["mixer_rows"] % CHUNK == 0 and seq % cfg["mixer_rows"] == 0
    assert ntok % cfg["router_rows"] == 0 and ntok % cfg["ple_rows"] == 0 and ntok % nw_rows == 0
    assert ntok % tm == 0 and lru_wa.shape[1:] == (LRU_BLOCKS, d // LRU_BLOCKS, d // LRU_BLOCKS)
    assert max(cfg[k] for k in cfg if k.endswith("_vmem")) < V7X_VMEM_BYTES
    b_router = jnp.concatenate([
        router_group_b[l], jnp.zeros((EXPERT_ROW0 - N_GROUPS,), F32), router_expert_b[l],
        jnp.zeros((ROUTER_ROWS - EXPERT_ROW0 - N_EXPERTS,), F32)])[:, None]
    ts = cfg["mixer_rows"]
    group = ts // V7X_SUBLANES
    bs_tile = jnp.tile(sgu_bs[l], (1, ts // CHUNK)).reshape(SGU_GROUPS, V7X_SUBLANES, group)
    bs_tile = jnp.transpose(bs_tile, (2, 1, 0)).reshape(ts, SGU_GROUPS)
    x1, hp = _mixer_call(
        x, mix_norm[l][None], w_in[l], conv_w[l], conv_b[l][None],
        _blockdiag_pack(lru_wa[l]), lru_ba[l][None], _blockdiag_pack(lru_wi[l]), lru_bi[l][None],
        lru_lambda[l][None], sgu_ln_g[l][None], sgu_ln_b[l][None], sgu_ws[l], bs_tile,
        w_out[l], ffn_norm[l][None])
    hp = hp.reshape(ntok, d // 2)
    pos, gate, cnt = _router_call(hp, router_group_w[l], router_expert_w[l], b_router)

    cap = ntok
    tiles_per_expert = (cnt[:, 0].astype(jnp.int32) + tm - 1) // tm
    nw = V7X_SC_CORES * V7X_SC_SUBCORES
    ch = cfg["sc_rows"]
    pos_w = jnp.transpose(pos[:TOP_K].reshape(TOP_K, nw, ntok // (nw * ch), ch), (1, 0, 2, 3))

    hs = _dispatch_call(hp, pos_w, N_EXPERTS * cap)
    ys = _expert_call(tiles_per_expert, hs, expert_w1[l], expert_w3[l], expert_w2[l], cap)
    yg0, yg1 = _combine_call(ys, pos_w)

    out = _ple_call(x1.reshape(ntok, d), yg0, yg1, gate, p[l].reshape(ntok, -1), ple_norm[l][None],
                    ple_gate_w[l], ple_up_w[l], final_norm[None])
    return out.reshape(bsz, seq, d)
```

```python
import functools

import jax
import jax.numpy as jnp
from jax import lax
from jax.experimental import pallas as pl
from jax.experimental.pallas import tpu as pltpu
from jax.experimental.pallas import tpu_sc as plsc

F32 = jnp.float32
BF16 = jnp.bfloat16
U32 = jnp.uint32

LRU_BLOCKS = 16
CONV_WIDTH = 4
LRU_C = 8.0
SGU_GROUPS = 8
CHUNK = 128
N_GROUPS = 4
EXPERTS_PER_GROUP = 8
N_EXPERTS = N_GROUPS * EXPERTS_PER_GROUP
TOP_K = 2
EPS = 1e-6

V7X_MXU_DIM = 256
V7X_SUBLANES = 8
V7X_LANES = 128
V7X_VMEM_BYTES = 64 * 1024 * 1024
V7X_SC_CORES = 2
V7X_SC_SUBCORES = 16

EXPERT_LOOKAHEAD = 3
ROUTER_SUBBLOCKS = 8
EXPERT_SUBBLOCKS = 2
PLE_SUBBLOCKS = 4
ROUTER_ROWS = V7X_LANES
EXPERT_ROW0 = V7X_SUBLANES


def _tiles():
    return dict(
        mixer_rows=256,
        expert_rows=512,
        ple_rows=1024,
        sc_rows=128,
        router_rows=4096,
        mixer_vmem=52 * 1024 * 1024,
        expert_vmem=40 * 1024 * 1024,
        ple_vmem=48 * 1024 * 1024,
        router_vmem=40 * 1024 * 1024,
    )


def _dot(a, b):
    return jnp.dot(a, b, preferred_element_type=F32)


def _sigmoid(x):
    return 0.5 * jnp.tanh(0.5 * x) + 0.5


def _rmsnorm(x, g):
    ms = jnp.mean(x * x, axis=-1, keepdims=True)
    return x * lax.rsqrt(ms + EPS) * g


def _pack_bf16_pair(lo, hi):
    lo_b = lax.bitcast_convert_type(lo.astype(BF16).astype(F32), U32)
    hi_b = lax.bitcast_convert_type(hi.astype(BF16).astype(F32), U32)
    return (hi_b & jnp.uint32(0xFFFF0000)) | lax.shift_right_logical(lo_b, jnp.uint32(16))


def _unpack_bf16_pair(w):
    lo = lax.bitcast_convert_type(lax.shift_left(w, jnp.uint32(16)), F32)
    hi = lax.bitcast_convert_type(w & jnp.uint32(0xFFFF0000), F32)
    return lo, hi


def _const_spec(shape):
    zeros = (0,) * len(shape)
    return pl.BlockSpec(shape, lambda *_: zeros, pipeline_mode=pl.Buffered(1))


def _tile_copies(hbm, buf, sem, b, row0, slot, to_hbm):
    group = buf.shape[1]
    copies = []
    for r in range(V7X_SUBLANES):
        hbm_rows = hbm.at[b, pl.ds(row0 + group * r, group), :]
        vmem_rows = buf.at[slot, :, r, :]
        src, dst = (vmem_rows, hbm_rows) if to_hbm else (hbm_rows, vmem_rows)
        copies.append(pltpu.make_async_copy(src, dst, sem.at[slot]))
    return copies


def _lru_scan(a, u, h0):
    group = a.shape[0]
    acc_a = [a[0]]
    acc_u = [u[0]]
    for g in range(1, group):
        acc_a.append(a[g] * acc_a[-1])
        acc_u.append(a[g] * acc_u[-1] + u[g])
    end_a, end_u = acc_a[-1], acc_u[-1]
    sub = lax.broadcasted_iota(jnp.int32, end_a.shape, 0)
    shift = 1
    while shift < V7X_SUBLANES:
        keep = sub >= shift
        a_sh = pltpu.roll(end_a, shift, axis=0)
        u_sh = pltpu.roll(end_u, shift, axis=0)
        end_u = jnp.where(keep, end_a * u_sh + end_u, end_u)
        end_a = jnp.where(keep, end_a * a_sh, end_a)
        shift *= 2
    h_end = end_a * h0 + end_u
    h_in = jnp.where(sub == 0, h0, pltpu.roll(h_end, 1, axis=0))
    out = [acc_a[g] * h_in + acc_u[g] for g in range(group)]
    return jnp.stack(out, axis=0), h_end[V7X_SUBLANES - 1:V7X_SUBLANES, :]


def _mixer_kernel(x_hbm, mixn_ref, win_hbm, convw_ref, convb_ref, wa_ref, ba_ref, wi_ref, bi_ref,
                  lam_ref, lng_ref, lnb_ref, ws_ref, bsp_ref, wout_hbm, ffn_ref,
                  x1_hbm, hp_hbm,
                  xbuf, z0_ref, z1_ref, x1buf, hpbuf, xsem, x1sem, hpsem, wsm_ref, ztail_ref, hcar_ref,
                  win_ref, wout_ref, wsem,
                  *, nseq):
    j = pl.program_id(0)
    ntile = pl.num_programs(0) - 1
    _, group, _, d = xbuf.shape
    rows = group * V7X_SUBLANES
    half = d // 2
    ta = jnp.minimum(j, ntile - 1)
    tb = jnp.maximum(j - 1, 0)
    s = lax.rem(tb, nseq)
    slot = lax.rem(tb, 2)

    def fetch(t):
        return _tile_copies(x_hbm, xbuf, xsem, lax.div(t, nseq), lax.rem(t, nseq) * rows,
                            lax.rem(t, 3), to_hbm=False)

    def put(t):
        tb_, ts_, sl = lax.div(t, nseq), lax.rem(t, nseq) * rows, lax.rem(t, 2)
        return (_tile_copies(x1_hbm, x1buf, x1sem, tb_, ts_, sl, to_hbm=True)
                + _tile_copies(hp_hbm, hpbuf, hpsem, tb_, ts_, sl, to_hbm=True))

    @pl.when(j == 0)
    def _():
        for i, c in enumerate(fetch(0)):
            c.start(priority=i % 2)
        stage = (z0_ref, z1_ref)
        n_in_chunks = win_hbm.shape[0] // rows

        def win_copy(c):
            return pltpu.make_async_copy(win_hbm.at[pl.ds(c * rows, rows), :], stage[c % 2], wsem.at[c % 2])

        win_copy(0).start()
        for c in range(n_in_chunks):
            if c + 1 < n_in_chunks:
                win_copy(c + 1).start()
            win_copy(c).wait()
            plain = 4 * d
            win_ref[c * rows:(c + 1) * rows, :plain] = stage[c % 2][:, :plain].astype(BF16)
            win_ref[c * rows:(c + 1) * rows, plain:] = (0.5 * stage[c % 2][:, plain:]).astype(BF16)
        n_out_chunks = wout_hbm.shape[0] // rows
        out_copies = [pltpu.make_async_copy(wout_hbm.at[pl.ds(c * rows, rows), :],
                                            z0_ref.at[:, c * d:(c + 1) * d], wsem.at[0])
                      for c in range(n_out_chunks)]
        for cp in out_copies:
            cp.start()
        for cp in out_copies:
            cp.wait()
        for c in range(n_out_chunks):
            wout_ref[c * rows:(c + 1) * rows, :] = (0.25 * z0_ref[:, c * d:(c + 1) * d]).astype(BF16)
        i_idx = lax.broadcasted_iota(jnp.int32, (rows, rows), 0)
        j_idx = lax.broadcasted_iota(jnp.int32, (rows, rows), 1)
        t_i = group * lax.rem(i_idx, V7X_SUBLANES) + lax.div(i_idx, V7X_SUBLANES)
        t_j = group * lax.rem(j_idx, V7X_SUBLANES) + lax.div(j_idx, V7X_SUBLANES)
        keep = (t_i >= t_j) & (lax.div(t_i, CHUNK) == lax.div(t_j, CHUNK))
        pick_rows = jnp.where(t_i == j_idx, 1.0, 0.0).astype(BF16)
        pick_cols = jnp.where(i_idx == t_j, 1.0, 0.0).astype(BF16)
        reps = rows // CHUNK
        for g in range(SGU_GROUPS):
            w_chunk = ws_ref[g].astype(BF16)
            w_rows = jnp.concatenate([w_chunk] * reps, axis=1)
            w_full = jnp.concatenate([w_rows] * reps, axis=0)
            w_perm = _dot(_dot(pick_rows, w_full).astype(BF16), pick_cols)
            wsm_ref[g] = jnp.where(keep, w_perm, 0.0).astype(BF16)

    @pl.when(j + 1 < ntile)
    def _():
        for i, c in enumerate(fetch(j + 1)):
            c.start(priority=i % 2)

    @pl.when(j < ntile)
    def _():
        for c in fetch(j):
            c.wait()

    @pl.when(s == 0)
    def _():
        ztail_ref[...] = jnp.zeros_like(ztail_ref)
        hcar_ref[...] = jnp.zeros_like(hcar_ref)

    def compute(z_w, z_r, phase_a=True, phase_b=True):
        pw = d
        if phase_a:
            xa_in = xbuf[lax.rem(ta, 3)].reshape(rows, d)
            h_next = _rmsnorm(xa_in, mixn_ref[...]).astype(BF16)

        def project(k):
            if phase_a:
                z_w[:, k * pw:(k + 1) * pw] = _dot(h_next, win_ref[:, k * pw:(k + 1) * pw])

        if not phase_b:
            for k in range(win_ref.shape[1] // pw):
                project(k)
            return

        x = xbuf[lax.rem(tb, 3)].reshape(rows, d)

        def sec(k, c0, c1):
            return z_r[:, k * d + c0:k * d + c1]

        def one_plus_tanh_gelu(v):
            c = 0.7978845608028654
            return 1.0 + jnp.tanh(v * (c + (c * 0.044715) * (v * v)))

        cw = 0.5 * convw_ref[...]
        cb_h = 0.5 * convb_ref[...]
        ba_h = 0.5 * ba_ref[...]
        bi_h = 0.5 * bi_ref[...]
        neg_lam = -lam_ref[...]
        softplus = jnp.maximum(neg_lam, 0.0) + jnp.log1p(jnp.exp(-jnp.abs(neg_lam)))
        c_a = (-0.5 * LRU_C) * softplus
        blk = V7X_MXU_DIM
        sub3 = lax.broadcasted_iota(jnp.int32, (CONV_WIDTH - 1, V7X_SUBLANES, blk), 1)
        term_a = []
        for n in range(d // blk):
            if n % 2 == 0:
                project(n // 2)
            c0, c1 = n * blk, (n + 1) * blk
            z3 = sec(0, c0, c1).reshape(group, V7X_SUBLANES, blk)
            tail = z3[group - (CONV_WIDTH - 1):]
            halo = jnp.where(sub3 == 0, pltpu.roll(ztail_ref[:, :, c0:c1], 1, axis=1),
                             pltpu.roll(tail, 1, axis=1))
            ztail_ref[:, :, c0:c1] = tail
            zext = jnp.concatenate([halo, z3], axis=0)
            xa_h = cb_h[:, c0:c1] + cw[CONV_WIDTH - 1:CONV_WIDTH, c0:c1] * z3
            for k in range(1, CONV_WIDTH):
                lo = CONV_WIDTH - 1 - k
                xa_h = xa_h + cw[lo:lo + 1, c0:c1] * zext[lo:lo + group]
            xa2 = xa_h.reshape(rows, blk)
            xa_bf = xa2.astype(BF16)
            th_r = jnp.tanh(_dot(xa_bf, wa_ref[n]) + ba_h[:, c0:c1])
            th_i = jnp.tanh(_dot(xa_bf, wi_ref[n]) + bi_h[:, c0:c1])
            a = jnp.exp(c_a[:, c0:c1] + c_a[:, c0:c1] * th_r)
            u = jnp.sqrt(1.0 - a * a) * ((1.0 + th_i) * xa2)
            hseq, hlast = _lru_scan(a.reshape(group, V7X_SUBLANES, blk),
                                    u.reshape(group, V7X_SUBLANES, blk), hcar_ref[:, c0:c1])
            hcar_ref[:, c0:c1] = hlast
            zg = sec(1, c0, c1)
            term_a.append(((1.0 + jnp.tanh(sec(4, c0, c1))) * one_plus_tanh_gelu(zg))
                          * (zg * hseq.reshape(rows, blk)))

        project(2)
        zv = sec(3, 0, d)
        gv2 = zv * one_plus_tanh_gelu(zv)
        mu = jnp.mean(gv2, axis=-1, keepdims=True)
        xc = gv2 - mu
        var = jnp.mean(xc * xc, axis=-1, keepdims=True)
        v_bf = (xc * lax.rsqrt(var + 4.0 * EPS) * lng_ref[...] + lnb_ref[...]).astype(BF16)
        project(3)
        gdim = d // SGU_GROUPS
        term_b = []
        for g in range(SGU_GROUPS):
            c0, c1 = g * gdim, (g + 1) * gdim
            if g in (2, 5):
                project({2: 4, 5: 5}[g])
            sp = _dot(wsm_ref[g], v_bf[:, c0:c1]) + bsp_ref[:, g:g + 1]
            zu = sec(2, c0, c1)
            term_b.append(((1.0 + jnp.tanh(sec(5, c0, c1))) * one_plus_tanh_gelu(zu)) * (zu * sp))
        merged4 = jnp.concatenate(term_a, axis=1) + jnp.concatenate(term_b, axis=1)

        x1 = x + _dot(merged4.astype(BF16), wout_ref[...])

        hn = _rmsnorm(x1, ffn_ref[...])
        hp = _pack_bf16_pair(hn[:, :half], hn[:, half:])

        @pl.when(j >= 3)
        def _():
            for c in put(tb - 2):
                c.wait()

        x1buf[slot] = x1.reshape(group, V7X_SUBLANES, d)
        hpbuf[slot] = hp.reshape(group, V7X_SUBLANES, half)

        @pl.when(j >= 1)
        def _():
            for i, c in enumerate(put(tb)):
                c.start(priority=i % 2)

    middle = (j > 0) & (j < ntile)

    @pl.when(j == 0)
    def _():
        compute(z0_ref, z1_ref, phase_b=False)

    @pl.when(middle & (lax.rem(j, 2) == 0))
    def _():
        compute(z0_ref, z1_ref)

    @pl.when(middle & (lax.rem(j, 2) == 1))
    def _():
        compute(z1_ref, z0_ref)

    @pl.when(j == ntile)
    def _():
        compute(z0_ref, z1_ref, phase_a=False)

    @pl.when(j == ntile)
    def _():
        for c in put(tb):
            c.wait()

        @pl.when(ntile >= 2)
        def _():
            for c in put(tb - 1):
                c.wait()


def _mixer_call(x, mix_norm, w_in, conv_w, conv_b, wa_blk, ba, wi_blk, bi, lam, ln_g, ln_b, ws,
                bs_tile, w_out, ffn_norm):
    cfg = _tiles()
    bsz, seq, d = x.shape
    ts = cfg["mixer_rows"]
    group = ts // V7X_SUBLANES
    nseq = seq // ts
    ntile = bsz * nseq
    assert ntile % 2 == 0, "the last grid step reads the in-projection buffer of an odd tile"
    row1 = (1, d)
    in_specs = [
        pl.BlockSpec(memory_space=pl.ANY),
        _const_spec(row1),
        pl.BlockSpec(memory_space=pl.ANY),
        _const_spec(conv_w.shape), _const_spec(row1),
        _const_spec(wa_blk.shape), _const_spec(row1),
        _const_spec(wi_blk.shape), _const_spec(row1),
        _const_spec(row1),
        _const_spec(row1), _const_spec(row1),
        _const_spec(ws.shape), _const_spec(bs_tile.shape),
        pl.BlockSpec(memory_space=pl.ANY), _const_spec(row1),
    ]
    out_shape = [
        jax.ShapeDtypeStruct((bsz, seq, d), F32),
        jax.ShapeDtypeStruct((bsz, seq, d // 2), U32),
    ]
    out_specs = [
        pl.BlockSpec(memory_space=pl.ANY),
        pl.BlockSpec(memory_space=pl.ANY),
    ]
    scratch = [
        pltpu.VMEM((3, group, V7X_SUBLANES, d), F32),
        pltpu.VMEM((ts, w_in.shape[1]), F32),
        pltpu.VMEM((ts, w_in.shape[1]), F32),
        pltpu.VMEM((2, group, V7X_SUBLANES, d), F32),
        pltpu.VMEM((2, group, V7X_SUBLANES, d // 2), U32),
        pltpu.SemaphoreType.DMA((3,)),
        pltpu.SemaphoreType.DMA((2,)),
        pltpu.SemaphoreType.DMA((2,)),
        pltpu.VMEM((SGU_GROUPS, ts, ts), BF16),
        pltpu.VMEM((CONV_WIDTH - 1, V7X_SUBLANES, d), F32),
        pltpu.VMEM((1, d), F32),
        pltpu.VMEM(w_in.shape, BF16),
        pltpu.VMEM(w_out.shape, BF16),
        pltpu.SemaphoreType.DMA((2,)),
    ]
    return pl.pallas_call(
        functools.partial(_mixer_kernel, nseq=nseq),
        grid=(ntile + 1,),
        in_specs=in_specs,
        out_specs=out_specs,
        out_shape=out_shape,
        scratch_shapes=scratch,
        compiler_params=pltpu.CompilerParams(
            dimension_semantics=("arbitrary",),
            vmem_limit_bytes=cfg["mixer_vmem"]),
        name="mixer",
    )(x, mix_norm, w_in, conv_w, conv_b, wa_blk, ba, wi_blk, bi, lam, ln_g, ln_b, ws, bs_tile,
      w_out, ffn_norm)


def _router_kernel(hp_ref, wg_ref, we_ref, br_ref, pos_ref, gate_ref, cnt_ref, ccar_ref, wr_ref,
                   *, expert_capacity):
    rows = hp_ref.shape[0]

    @pl.when(pl.program_id(0) == 0)
    def _():
        ccar_ref[...] = jnp.zeros_like(ccar_ref)
        wr_ref[...] = jnp.zeros_like(wr_ref)
        wr_ref[:, 0:N_GROUPS] = wg_ref[...].astype(BF16)
        wr_ref[:, EXPERT_ROW0:EXPERT_ROW0 + N_EXPERTS] = we_ref[...].astype(BF16)

    sub_rows = rows // ROUTER_SUBBLOCKS
    lts = []
    for q in range(ROUTER_SUBBLOCKS):
        lo, hi = _unpack_bf16_pair(hp_ref[q * sub_rows:(q + 1) * sub_rows, :])
        hn = jnp.concatenate([lo, hi], axis=1)
        logits = _dot(hn.astype(BF16), wr_ref[...])
        lts.append(jnp.transpose(logits) + br_ref[...])
    sub = lax.broadcasted_iota(jnp.int32, (V7X_SUBLANES, sub_rows), 0)
    subf = sub.astype(F32)
    big = jnp.float32(1e9)
    eid = lax.broadcasted_iota(jnp.int32, (N_EXPERTS, sub_rows), 0).astype(F32)
    sb = V7X_MXU_DIM
    before = (lax.broadcasted_iota(jnp.int32, (sb, sb), 0)
              < lax.broadcasted_iota(jnp.int32, (sb, sb), 1))
    before = jnp.where(before, 1.0, 0.0).astype(BF16)
    cap = float(expert_capacity)
    zero = jnp.zeros((V7X_SUBLANES - TOP_K, sub_rows), F32)
    running = ccar_ref[:, 0:1]
    for q, lt in enumerate(lts):
        lg = jnp.where(sub < N_GROUPS, lt[0:V7X_SUBLANES, :], -jnp.inf)
        g_exp = jnp.exp(lg - jnp.max(lg, axis=0, keepdims=True))
        g_prob = g_exp / jnp.sum(g_exp, axis=0, keepdims=True)
        g_top = jnp.max(g_prob, axis=0, keepdims=True)
        g_idx = jnp.min(jnp.where(g_prob == g_top, subf, big), axis=0, keepdims=True)

        e_sel = jnp.zeros((EXPERTS_PER_GROUP, sub_rows), F32)
        for g in range(N_GROUPS):
            r0 = EXPERT_ROW0 + g * EXPERTS_PER_GROUP
            e_sel = jnp.where(g_idx == g, lt[r0:r0 + EXPERTS_PER_GROUP, :], e_sel)
        e_exp = jnp.exp(e_sel - jnp.max(e_sel, axis=0, keepdims=True))
        e_prob = e_exp / jnp.sum(e_exp, axis=0, keepdims=True)
        p1 = jnp.max(e_prob, axis=0, keepdims=True)
        i1 = jnp.min(jnp.where(e_prob == p1, subf, big), axis=0, keepdims=True)
        rest = jnp.where(subf == i1, -1.0, e_prob)
        p2 = jnp.max(rest, axis=0, keepdims=True)
        i2 = jnp.min(jnp.where(rest == p2, subf, big), axis=0, keepdims=True)
        psum = p1 + p2
        gate1 = g_top * (p1 / psum)
        gate2 = g_top * (p2 / psum)
        gid1 = g_idx * EXPERTS_PER_GROUP + i1
        gid2 = g_idx * EXPERTS_PER_GROUP + i2

        hit1 = eid == gid1
        hit2 = eid == gid2
        cnt = jnp.where(hit1 | hit2, 1.0, 0.0)
        base = []
        for c in range(sub_rows // sb):
            part = cnt[:, c * sb:(c + 1) * sb]
            base.append(running + _dot(part.astype(BF16), before))
            running = running + jnp.sum(part, axis=1, keepdims=True)
        base = jnp.concatenate(base, axis=1)
        rank1 = jnp.sum(jnp.where(hit1, base, 0.0), axis=0, keepdims=True)
        rank2 = jnp.sum(jnp.where(hit2, base, 0.0), axis=0, keepdims=True)
        pos = jnp.concatenate([gid1 * cap + rank1, gid2 * cap + rank2, zero], axis=0)
        pos_ref[:, q * sub_rows:(q + 1) * sub_rows] = pos.astype(jnp.int32)
        gate_ref[:, q * sub_rows:(q + 1) * sub_rows] = jnp.concatenate([gate1, gate2, zero], axis=0)
    total = jnp.broadcast_to(running, ccar_ref.shape)
    ccar_ref[...] = total
    cnt_ref[...] = total


def _router_call(hp, w_group, w_expert, b_router):
    cfg = _tiles()
    ntok, half = hp.shape
    tr = cfg["router_rows"]
    return pl.pallas_call(
        functools.partial(_router_kernel, expert_capacity=ntok),
        grid=(ntok // tr,),
        in_specs=[
            pl.BlockSpec((tr, half), lambda i: (i, 0)),
            _const_spec(w_group.shape),
            _const_spec(w_expert.shape),
            _const_spec(b_router.shape),
        ],
        out_specs=[
            pl.BlockSpec((V7X_SUBLANES, tr), lambda i: (0, i)),
            pl.BlockSpec((V7X_SUBLANES, tr), lambda i: (0, i)),
            pl.BlockSpec((N_EXPERTS, V7X_LANES), lambda i: (0, 0)),
        ],
        out_shape=[
            jax.ShapeDtypeStruct((V7X_SUBLANES, ntok), jnp.int32),
            jax.ShapeDtypeStruct((V7X_SUBLANES, ntok), F32),
            jax.ShapeDtypeStruct((N_EXPERTS, V7X_LANES), F32),
        ],
        scratch_shapes=[
            pltpu.VMEM((N_EXPERTS, V7X_LANES), F32),
            pltpu.VMEM((w_group.shape[0], ROUTER_ROWS), BF16),
        ],
        compiler_params=pltpu.CompilerParams(
            dimension_semantics=("arbitrary",),
            vmem_limit_bytes=cfg["router_vmem"]),
        name="router",
    )(hp, w_group, w_expert, b_router)


def _expert_kernel(nt_ref, base_ref, texp_ref, tloc_ref, hs_hbm, w1_ref, w3_ref, w2_ref, ys_hbm,
                   hbuf, ybuf, hsem, ysem, w1b_ref, w3b_ref, w2b_ref, *, capacity):
    e = pl.program_id(0)
    n_exp = pl.num_programs(0)
    nt = nt_ref[e]
    base = base_ref[e]
    total = base_ref[n_exp - 1] + nt_ref[n_exp - 1]
    n_in, tm, _ = hbuf.shape
    n_out = ybuf.shape[0]
    ahead = n_in - 1

    def load(g):
        slot = lax.rem(g, n_in)
        rows = pl.ds(texp_ref[g] * capacity + tloc_ref[g] * tm, tm)
        return pltpu.make_async_copy(hs_hbm.at[rows], hbuf.at[slot], hsem.at[slot])

    def store(t, slot):
        rows = pl.ds(e * capacity + t * tm, tm)
        return pltpu.make_async_copy(ybuf.at[slot], ys_hbm.at[rows], ysem.at[slot])

    @pl.when(e == 0)
    def _():
        for g0 in range(ahead):
            @pl.when(g0 < total)
            def _():
                load(g0).start(priority=1)

    w1b_ref[...] = w1_ref[...].astype(BF16)
    w3b_ref[...] = w3_ref[...].astype(BF16)
    w2b_ref[...] = w2_ref[...].astype(BF16)

    @pl.loop(0, nt)
    def _(t):
        g = base + t

        @pl.when(g + ahead < total)
        def _():
            load(g + ahead).start(priority=1)

        load(g).wait()
        slot = lax.rem(g, n_out)

        @pl.when(g >= n_out)
        def _():
            store(t, slot).wait()

        sub_rows = tm // EXPERT_SUBBLOCKS
        blocks = [pl.ds(q * sub_rows, sub_rows) for q in range(EXPERT_SUBBLOCKS)]
        rows_in = []
        for rs in blocks:
            lo, hi = _unpack_bf16_pair(hbuf[lax.rem(g, n_in), rs, :])
            rows_in.append(jnp.concatenate([lo, hi], axis=1).astype(BF16))
        up = [(_dot(h, w1b_ref[...]), _dot(h, w3b_ref[...])) for h in rows_in]
        down = []
        for a, b in up:
            hid = (a * _sigmoid(a)) * b
            down.append(_dot(hid.astype(BF16), w2b_ref[...]))
        for rs, y in zip(blocks, down):
            half = y.shape[1] // 2
            ybuf[slot, rs, :] = _pack_bf16_pair(y[:, :half], y[:, half:])
        store(t, slot).start()

    @pl.when(e + 1 == n_exp)
    def _():
        for back in range(1, n_out + 1):
            @pl.when(total >= back)
            def _():
                store(0, lax.rem(total - back, n_out)).wait()


def _expert_call(tiles_per_expert, hs, w1, w3, w2, capacity):
    cfg = _tiles()
    tm = cfg["expert_rows"]
    prow, half = hs.shape
    n_exp, d, f = w1.shape
    ends = jnp.cumsum(tiles_per_expert)
    base = ends - tiles_per_expert
    g = jnp.arange(capacity * TOP_K // tm + n_exp, dtype=jnp.int32)
    texp = jnp.minimum(jnp.sum((ends[None, :] <= g[:, None]).astype(jnp.int32), axis=1), n_exp - 1)
    onehot = texp[:, None] == jnp.arange(n_exp, dtype=jnp.int32)[None, :]
    tloc = g - jnp.sum(jnp.where(onehot, base[None, :], 0), axis=1)

    def w_map(e, *_):
        return (e, 0, 0)

    grid_spec = pltpu.PrefetchScalarGridSpec(
        num_scalar_prefetch=4,
        grid=(n_exp,),
        in_specs=[
            pl.BlockSpec(memory_space=pl.ANY),
            pl.BlockSpec((None, d, f), w_map),
            pl.BlockSpec((None, d, f), w_map),
            pl.BlockSpec((None, f, d), w_map),
        ],
        out_specs=pl.BlockSpec(memory_space=pl.ANY),
        scratch_shapes=[
            pltpu.VMEM((EXPERT_LOOKAHEAD + 1, tm, half), U32),
            pltpu.VMEM((2, tm, half), U32),
            pltpu.SemaphoreType.DMA((EXPERT_LOOKAHEAD + 1,)),
            pltpu.SemaphoreType.DMA((2,)),
            pltpu.VMEM((d, f), BF16),
            pltpu.VMEM((d, f), BF16),
            pltpu.VMEM((f, d), BF16),
        ],
    )
    return pl.pallas_call(
        functools.partial(_expert_kernel, capacity=capacity),
        grid_spec=grid_spec,
        out_shape=jax.ShapeDtypeStruct((prow, half), U32),
        compiler_params=pltpu.CompilerParams(
            dimension_semantics=("arbitrary",),
            vmem_limit_bytes=cfg["expert_vmem"]),
        name="experts",
    )(tiles_per_expert, base, texp, tloc, hs, w1, w3, w2)


def _sc_mesh():
    return plsc.VectorSubcoreMesh(core_axis_name="c", subcore_axis_name="s",
                                  num_cores=V7X_SC_CORES, num_subcores=V7X_SC_SUBCORES)


def _sc_worker_id():
    return lax.axis_index("s") * V7X_SC_CORES + lax.axis_index("c")


def _dispatch_call(hp, pos_w, out_rows):
    cfg = _tiles()
    ntok, half = hp.shape
    nw, topk, nch, ch = pos_w.shape
    per_w = nch * ch

    def body(hp_hbm, pos_hbm, hs_hbm, idx_v, buf, wsem):
        wid = _sc_worker_id()
        pltpu.sync_copy(pos_hbm.at[wid], idx_v)

        for c in range(nch):
            pltpu.sync_copy(hp_hbm.at[pl.ds(wid * per_w + c * ch, ch)], buf)
            writes = [pltpu.make_async_copy(buf, hs_hbm.at[idx_v.at[k, c]], wsem.at[k]) for k in range(topk)]
            for w in writes:
                w.start()
            for w in writes:
                w.wait()

    assert nw == V7X_SC_CORES * V7X_SC_SUBCORES and nw * per_w == ntok and ch == cfg["sc_rows"]
    return pl.kernel(
        body,
        out_type=jax.ShapeDtypeStruct((out_rows, half), U32),
        mesh=_sc_mesh(),
        scratch_types=[
            pltpu.VMEM((topk, nch, ch), jnp.int32),
            pltpu.VMEM((ch, half), U32),
            pltpu.SemaphoreType.DMA((topk,)),
        ],
        name="dispatch",
    )(hp, pos_w)


def _combine_call(ys, pos_w):
    cfg = _tiles()
    _, half = ys.shape
    nw, topk, nch, ch = pos_w.shape
    per_w = nch * ch
    ntok = nw * per_w

    def body(ys_hbm, pos_hbm, *rest):
        outs = rest[:topk]
        idx_v, buf = rest[topk:]
        wid = _sc_worker_id()
        pltpu.sync_copy(pos_hbm.at[wid], idx_v)
        for c in range(nch):
            for k in range(topk):
                pltpu.sync_copy(ys_hbm.at[idx_v.at[k, c]], buf)
                pltpu.sync_copy(buf, outs[k].at[pl.ds(wid * per_w + c * ch, ch)])

    assert nw == V7X_SC_CORES * V7X_SC_SUBCORES and ch == cfg["sc_rows"]
    return pl.kernel(
        body,
        out_type=[jax.ShapeDtypeStruct((ntok, half), U32)] * topk,
        mesh=_sc_mesh(),
        scratch_types=[
            pltpu.VMEM((topk, nch, ch), jnp.int32),
            pltpu.VMEM((ch, half), U32),
        ],
        name="combine",
    )(ys, pos_w)


def _ple_kernel(x1_ref, yg0_ref, yg1_ref, gate_ref, p_ref, plen_ref, wg32_ref, wu32_ref, fin_ref, o_ref,
                wg_ref, wu_ref):
    @pl.when(pl.program_id(0) == 0)
    def _():
        wg_ref[...] = (0.5 * wg32_ref[...]).astype(BF16)
        wu_ref[...] = (0.5 * wu32_ref[...]).astype(BF16)

    rows = x1_ref.shape[0]
    sub_rows = rows // PLE_SUBBLOCKS
    for q in range(PLE_SUBBLOCKS):
        rs = pl.ds(q * sub_rows, sub_rows)
        lo0, hi0 = _unpack_bf16_pair(yg0_ref[rs, :])
        lo1, hi1 = _unpack_bf16_pair(yg1_ref[rs, :])
        gates = jnp.transpose(gate_ref[:, rs])
        g0 = gates[:, 0:1]
        g1 = gates[:, 1:2]
        moe = g0 * jnp.concatenate([lo0, hi0], axis=1) + g1 * jnp.concatenate([lo1, hi1], axis=1)
        x2 = x1_ref[rs, :] + moe
        r = _rmsnorm(x2, plen_ref[...]).astype(BF16)
        gt2 = 1.0 + jnp.tanh(_dot(r, wg_ref[...]))
        up_h = _dot(p_ref[rs, :].astype(BF16), wu_ref[...])
        x3 = x2 + gt2 * up_h
        o_ref[rs, :] = _rmsnorm(x3, fin_ref[...])


def _ple_call(x1, yg0, yg1, gates, p, ple_norm, wg, wu, final_norm):
    cfg = _tiles()
    ntok, d = x1.shape
    tp = cfg["ple_rows"]
    pdim = p.shape[1]
    return pl.pallas_call(
        _ple_kernel,
        grid=(ntok // tp,),
        in_specs=[
            pl.BlockSpec((tp, d), lambda i: (i, 0)),
            pl.BlockSpec((tp, d // 2), lambda i: (i, 0)),
            pl.BlockSpec((tp, d // 2), lambda i: (i, 0)),
            pl.BlockSpec((V7X_SUBLANES, tp), lambda i: (0, i)),
            pl.BlockSpec((tp, pdim), lambda i: (i, 0)),
            _const_spec((1, d)),
            _const_spec(wg.shape),
            _const_spec(wu.shape),
            _const_spec((1, d)),
        ],
        out_specs=pl.BlockSpec((tp, d), lambda i: (i, 0)),
        out_shape=jax.ShapeDtypeStruct((ntok, d), F32),
        scratch_shapes=[pltpu.VMEM(wg.shape, BF16), pltpu.VMEM(wu.shape, BF16)],
        compiler_params=pltpu.CompilerParams(
            dimension_semantics=("arbitrary",),
            vmem_limit_bytes=cfg["ple_vmem"]),
        name="ple",
    )(x1, yg0, yg1, gates, p, ple_norm, wg, wu, final_norm)


def _blockdiag_pack(w):
    nb, bd, _ = w.shape
    per = V7X_MXU_DIM // bd
    w4 = w.reshape(nb // per, per, bd, bd)
    eye = jnp.eye(per, dtype=w.dtype)
    out = jnp.einsum("jpab,pq->jpaqb", w4, eye)
    return out.reshape(nb // per, V7X_MXU_DIM, V7X_MXU_DIM).astype(BF16)


def kernel(x, p, mix_norm, w_in, conv_w, conv_b, lru_wa, lru_ba, lru_wi, lru_bi, lru_lambda, sgu_ln_g, sgu_ln_b, sgu_ws, sgu_bs, w_out, ffn_norm, router_group_w, router_group_b, router_expert_w, router_expert_b, expert_w1, expert_w3, expert_w2, ple_norm, ple_gate_w, ple_up_w, final_norm):
    cfg = _tiles()
    bsz, seq, d = x.shape
    ntok = bsz * seq
    tm = cfg["expert_rows"]
    depth = w_in.shape[0]
    assert depth == 1, "the ple kernel applies the final norm, so it must be the last layer"
    l = 0
    nw_rows = V7X_SC_CORES * V7X_SC_SUBCORES * cfg["sc_rows"]
    assert cfg["mixer_rows"] % CHUNK == 0 and seq % cfg["mixer_rows"] == 0
    assert ntok % cfg["router_rows"] == 0 and ntok % cfg["ple_rows"] == 0 and ntok % nw_rows == 0
    assert ntok % tm == 0 and lru_wa.shape[1:] == (LRU_BLOCKS, d // LRU_BLOCKS, d // LRU_BLOCKS)
    assert max(cfg[k] for k in cfg if k.endswith("_vmem")) < V7X_VMEM_BYTES
    b_router = jnp.concatenate([
        router_group_b[l], jnp.zeros((EXPERT_ROW0 - N_GROUPS,), F32), router_expert_b[l],
        jnp.zeros((ROUTER_ROWS - EXPERT_ROW0 - N_EXPERTS,), F32)])[:, None]
    ts = cfg["mixer_rows"]
    group = ts // V7X_SUBLANES
    bs_tile = jnp.tile(sgu_bs[l], (1, ts // CHUNK)).reshape(SGU_GROUPS, V7X_SUBLANES, group)
    bs_tile = jnp.transpose(bs_tile, (2, 1, 0)).reshape(ts, SGU_GROUPS)
    x1, hp = _mixer_call(
        x, mix_norm[l][None], w_in[l], conv_w[l], conv_b[l][None],
        _blockdiag_pack(lru_wa[l]), lru_ba[l][None], _blockdiag_pack(lru_wi[l]), lru_bi[l][None],
        lru_lambda[l][None], sgu_ln_g[l][None], sgu_ln_b[l][None], sgu_ws[l], bs_tile,
        w_out[l], ffn_norm[l][None])
    hp = hp.reshape(ntok, d // 2)
    pos, gate, cnt = _router_call(hp, router_group_w[l], router_expert_w[l], b_router)

    cap = ntok
    tiles_per_expert = (cnt[:, 0].astype(jnp.int32) + tm - 1) // tm
    nw = V7X_SC_CORES * V7X_SC_SUBCORES
    ch = cfg["sc_rows"]
    pos_w = jnp.transpose(pos[:TOP_K].reshape(TOP_K, nw, ntok // (nw * ch), ch), (1, 0, 2, 3))

    hs = _dispatch_call(hp, pos_w, N_EXPERTS * cap)
    ys = _expert_call(tiles_per_expert, hs, expert_w1[l], expert_w3[l], expert_w2[l], cap)
    yg0, yg1 = _combine_call(ys, pos_w)

    out = _ple_call(x1.reshape(ntok, d), yg0, yg1, gate, p[l].reshape(ntok, -1), ple_norm[l][None],
                    ple_gate_w[l], ple_up_w[l], final_norm[None])
    return out.reshape(bsz, seq, d)
```

```python
import functools

import jax
import jax.numpy as jnp
from jax import lax
from jax.experimental import pallas as pl
from jax.experimental.pallas import tpu as pltpu
from jax.experimental.pallas import tpu_sc as plsc

F32 = jnp.float32
BF16 = jnp.bfloat16
U32 = jnp.uint32

LRU_BLOCKS = 16
CONV_WIDTH = 4
LRU_C = 8.0
SGU_GROUPS = 8
CHUNK = 128
N_GROUPS = 4
EXPERTS_PER_GROUP = 8
N_EXPERTS = N_GROUPS * EXPERTS_PER_GROUP
TOP_K = 2
EPS = 1e-6

V7X_MXU_DIM = 256
V7X_SUBLANES = 8
V7X_LANES = 128
V7X_VMEM_BYTES = 64 * 1024 * 1024
V7X_SC_CORES = 2
V7X_SC_SUBCORES = 16

EXPERT_LOOKAHEAD = 3
ROUTER_SUBBLOCKS = 8
EXPERT_SUBBLOCKS = 2
PLE_SUBBLOCKS = 4
ROUTER_ROWS = V7X_LANES
EXPERT_ROW0 = V7X_SUBLANES


def _tiles():
    return dict(
        mixer_rows=256,
        expert_rows=512,
        ple_rows=1024,
        sc_rows=128,
        router_rows=4096,
        mixer_vmem=52 * 1024 * 1024,
        expert_vmem=40 * 1024 * 1024,
        ple_vmem=48 * 1024 * 1024,
        router_vmem=40 * 1024 * 1024,
    )


def _dot(a, b):
    return jnp.dot(a, b, preferred_element_type=F32)


def _sigmoid(x):
    return 0.5 * jnp.tanh(0.5 * x) + 0.5


def _rmsnorm(x, g):
    ms = jnp.mean(x * x, axis=-1, keepdims=True)
    return x * lax.rsqrt(ms + EPS) * g


def _pack_bf16_pair(lo, hi):
    lo_b = lax.bitcast_convert_type(lo.astype(BF16).astype(F32), U32)
    hi_b = lax.bitcast_convert_type(hi.astype(BF16).astype(F32), U32)
    return (hi_b & jnp.uint32(0xFFFF0000)) | lax.shift_right_logical(lo_b, jnp.uint32(16))


def _unpack_bf16_pair(w):
    lo = lax.bitcast_convert_type(lax.shift_left(w, jnp.uint32(16)), F32)
    hi = lax.bitcast_convert_type(w & jnp.uint32(0xFFFF0000), F32)
    return lo, hi


def _const_spec(shape):
    zeros = (0,) * len(shape)
    return pl.BlockSpec(shape, lambda *_: zeros, pipeline_mode=pl.Buffered(1))


def _tile_copies(hbm, buf, sem, b, row0, slot, to_hbm):
    group = buf.shape[1]
    copies = []
    for r in range(V7X_SUBLANES):
        hbm_rows = hbm.at[b, pl.ds(row0 + group * r, group), :]
        vmem_rows = buf.at[slot, :, r, :]
        src, dst = (vmem_rows, hbm_rows) if to_hbm else (hbm_rows, vmem_rows)
        copies.append(pltpu.make_async_copy(src, dst, sem.at[slot]))
    return copies


def _lru_scan(a, u, h0):
    group = a.shape[0]
    acc_a = [a[0]]
    acc_u = [u[0]]
    for g in range(1, group):
        acc_a.append(a[g] * acc_a[-1])
        acc_u.append(a[g] * acc_u[-1] + u[g])
    end_a, end_u = acc_a[-1], acc_u[-1]
    sub = lax.broadcasted_iota(jnp.int32, end_a.shape, 0)
    shift = 1
    while shift < V7X_SUBLANES:
        keep = sub >= shift
        a_sh = pltpu.roll(end_a, shift, axis=0)
        u_sh = pltpu.roll(end_u, shift, axis=0)
        end_u = jnp.where(keep, end_a * u_sh + end_u, end_u)
        end_a = jnp.where(keep, end_a * a_sh, end_a)
        shift *= 2
    h_end = end_a * h0 + end_u
    h_in = jnp.where(sub == 0, h0, pltpu.roll(h_end, 1, axis=0))
    out = [acc_a[g] * h_in + acc_u[g] for g in range(group)]
    return jnp.stack(out, axis=0), h_end[V7X_SUBLANES - 1:V7X_SUBLANES, :]


def _mixer_kernel(x_hbm, mixn_ref, win_hbm, convw_ref, convb_ref, wa_ref, ba_ref, wi_ref, bi_ref,
                  lam_ref, lng_ref, lnb_ref, ws_ref, bsp_ref, wout_hbm, ffn_ref,
                  x1_hbm, hp_hbm,
                  xbuf, z0_ref, z1_ref, x1buf, hpbuf, xsem, x1sem, hpsem, wsm_ref, ztail_ref, hcar_ref,
                  win_ref, wout_ref, wsem,
                  *, nseq):
    j = pl.program_id(0)
    ntile = pl.num_programs(0) - 1
    _, group, _, d = xbuf.shape
    rows = group * V7X_SUBLANES
    half = d // 2
    ta = jnp.minimum(j, ntile - 1)
    tb = jnp.maximum(j - 1, 0)
    s = lax.rem(tb, nseq)
    slot = lax.rem(tb, 2)

    def fetch(t):
        return _tile_copies(x_hbm, xbuf, xsem, lax.div(t, nseq), lax.rem(t, nseq) * rows,
                            lax.rem(t, 3), to_hbm=False)

    def put(t):
        tb_, ts_, sl = lax.div(t, nseq), lax.rem(t, nseq) * rows, lax.rem(t, 2)
        return (_tile_copies(x1_hbm, x1buf, x1sem, tb_, ts_, sl, to_hbm=True)
                + _tile_copies(hp_hbm, hpbuf, hpsem, tb_, ts_, sl, to_hbm=True))

    @pl.when(j == 0)
    def _():
        for i, c in enumerate(fetch(0)):
            c.start(priority=i % 2)
        stage = (z0_ref, z1_ref)
        n_in_chunks = win_hbm.shape[0] // rows

        def win_copy(c):
            return pltpu.make_async_copy(win_hbm.at[pl.ds(c * rows, rows), :], stage[c % 2], wsem.at[c % 2])

        win_copy(0).start()
        for c in range(n_in_chunks):
            if c + 1 < n_in_chunks:
                win_copy(c + 1).start()
            win_copy(c).wait()
            plain = 4 * d
            win_ref[c * rows:(c + 1) * rows, :plain] = stage[c % 2][:, :plain].astype(BF16)
            win_ref[c * rows:(c + 1) * rows, plain:] = (0.5 * stage[c % 2][:, plain:]).astype(BF16)
        n_out_chunks = wout_hbm.shape[0] // rows
        out_copies = [pltpu.make_async_copy(wout_hbm.at[pl.ds(c * rows, rows), :],
                                            z0_ref.at[:, c * d:(c + 1) * d], wsem.at[0])
                      for c in range(n_out_chunks)]
        for cp in out_copies:
            cp.start()
        for cp in out_copies:
            cp.wait()
        for c in range(n_out_chunks):
            wout_ref[c * rows:(c + 1) * rows, :] = (0.25 * z0_ref[:, c * d:(c + 1) * d]).astype(BF16)
        i_idx = lax.broadcasted_iota(jnp.int32, (rows, rows), 0)
        j_idx = lax.broadcasted_iota(jnp.int32, (rows, rows), 1)
        t_i = group * lax.rem(i_idx, V7X_SUBLANES) + lax.div(i_idx, V7X_SUBLANES)
        t_j = group * lax.rem(j_idx, V7X_SUBLANES) + lax.div(j_idx, V7X_SUBLANES)
        keep = (t_i >= t_j) & (lax.div(t_i, CHUNK) == lax.div(t_j, CHUNK))
        pick_rows = jnp.where(t_i == j_idx, 1.0, 0.0).astype(BF16)
        pick_cols = jnp.where(i_idx == t_j, 1.0, 0.0).astype(BF16)
        reps = rows // CHUNK
        for g in range(SGU_GROUPS):
            w_chunk = ws_ref[g].astype(BF16)
            w_rows = jnp.concatenate([w_chunk] * reps, axis=1)
            w_full = jnp.concatenate([w_rows] * reps, axis=0)
            w_perm = _dot(_dot(pick_rows, w_full).astype(BF16), pick_cols)
            wsm_ref[g] = jnp.where(keep, w_perm, 0.0).astype(BF16)

    @pl.when(j + 1 < ntile)
    def _():
        for i, c in enumerate(fetch(j + 1)):
            c.start(priority=i % 2)

    @pl.when(j < ntile)
    def _():
        for c in fetch(j):
            c.wait()

    @pl.when(s == 0)
    def _():
        ztail_ref[...] = jnp.zeros_like(ztail_ref)
        hcar_ref[...] = jnp.zeros_like(hcar_ref)

    def compute(z_w, z_r, phase_a=True, phase_b=True):
        pw = d
        if phase_a:
            xa_in = xbuf[lax.rem(ta, 3)].reshape(rows, d)
            h_next = _rmsnorm(xa_in, mixn_ref[...]).astype(BF16)

        def project(k):
            if phase_a:
                z_w[:, k * pw:(k + 1) * pw] = _dot(h_next, win_ref[:, k * pw:(k + 1) * pw])

        if not phase_b:
            for k in range(win_ref.shape[1] // pw):
                project(k)
            return

        x = xbuf[lax.rem(tb, 3)].reshape(rows, d)

        def sec(k, c0, c1):
            return z_r[:, k * d + c0:k * d + c1]

        def one_plus_tanh_gelu(v):
            c = 0.7978845608028654
            return 1.0 + jnp.tanh(v * (c + (c * 0.044715) * (v * v)))

        cw = 0.5 * convw_ref[...]
        cb_h = 0.5 * convb_ref[...]
        ba_h = 0.5 * ba_ref[...]
        bi_h = 0.5 * bi_ref[...]
        neg_lam = -lam_ref[...]
        softplus = jnp.maximum(neg_lam, 0.0) + jnp.log1p(jnp.exp(-jnp.abs(neg_lam)))
        c_a = (-0.5 * LRU_C) * softplus
        blk = V7X_MXU_DIM
        sub3 = lax.broadcasted_iota(jnp.int32, (CONV_WIDTH - 1, V7X_SUBLANES, blk), 1)
        term_a = []
        for n in range(d // blk):
            if n % 2 == 0:
                project(n // 2)
            c0, c1 = n * blk, (n + 1) * blk
            z3 = sec(0, c0, c1).reshape(group, V7X_SUBLANES, blk)
            tail = z3[group - (CONV_WIDTH - 1):]
            halo = jnp.where(sub3 == 0, pltpu.roll(ztail_ref[:, :, c0:c1], 1, axis=1),
                             pltpu.roll(tail, 1, axis=1))
            ztail_ref[:, :, c0:c1] = tail
            zext = jnp.concatenate([halo, z3], axis=0)
            xa_h = cb_h[:, c0:c1] + cw[CONV_WIDTH - 1:CONV_WIDTH, c0:c1] * z3
            for k in range(1, CONV_WIDTH):
                lo = CONV_WIDTH - 1 - k
                xa_h = xa_h + cw[lo:lo + 1, c0:c1] * zext[lo:lo + group]
            xa2 = xa_h.reshape(rows, blk)
            xa_bf = xa2.astype(BF16)
            th_r = jnp.tanh(_dot(xa_bf, wa_ref[n]) + ba_h[:, c0:c1])
            th_i = jnp.tanh(_dot(xa_bf, wi_ref[n]) + bi_h[:, c0:c1])
            a = jnp.exp(c_a[:, c0:c1] + c_a[:, c0:c1] * th_r)
            u = jnp.sqrt(1.0 - a * a) * ((1.0 + th_i) * xa2)
            hseq, hlast = _lru_scan(a.reshape(group, V7X_SUBLANES, blk),
                                    u.reshape(group, V7X_SUBLANES, blk), hcar_ref[:, c0:c1])
            hcar_ref[:, c0:c1] = hlast
            zg = sec(1, c0, c1)
            term_a.append(((1.0 + jnp.tanh(sec(4, c0, c1))) * one_plus_tanh_gelu(zg))
                          * (zg * hseq.reshape(rows, blk)))

        project(2)
        zv = sec(3, 0, d)
        gv2 = zv * one_plus_tanh_gelu(zv)
        mu = jnp.mean(gv2, axis=-1, keepdims=True)
        xc = gv2 - mu
        var = jnp.mean(xc * xc, axis=-1, keepdims=True)
        v_bf = (xc * lax.rsqrt(var + 4.0 * EPS) * lng_ref[...] + lnb_ref[...]).astype(BF16)
        project(3)
        gdim = d // SGU_GROUPS
        term_b = []
        for g in range(SGU_GROUPS):
            c0, c1 = g * gdim, (g + 1) * gdim
            if g in (2, 5):
                project({2: 4, 5: 5}[g])
            sp = _dot(wsm_ref[g], v_bf[:, c0:c1]) + bsp_ref[:, g:g + 1]
            zu = sec(2, c0, c1)
            term_b.append(((1.0 + jnp.tanh(sec(5, c0, c1))) * one_plus_tanh_gelu(zu)) * (zu * sp))
        merged4 = jnp.concatenate(term_a, axis=1) + jnp.concatenate(term_b, axis=1)

        x1 = x + _dot(merged4.astype(BF16), wout_ref[...])

        hn = _rmsnorm(x1, ffn_ref[...])
        hp = _pack_bf16_pair(hn[:, :half], hn[:, half:])

        @pl.when(j >= 3)
        def _():
            for c in put(tb - 2):
                c.wait()

        x1buf[slot] = x1.reshape(group, V7X_SUBLANES, d)
        hpbuf[slot] = hp.reshape(group, V7X_SUBLANES, half)

        @pl.when(j >= 1)
        def _():
            for i, c in enumerate(put(tb)):
                c.start(priority=i % 2)

    middle = (j > 0) & (j < ntile)

    @pl.when(j == 0)
    def _():
        compute(z0_ref, z1_ref, phase_b=False)

    @pl.when(middle & (lax.rem(j, 2) == 0))
    def _():
        compute(z0_ref, z1_ref)

    @pl.when(middle & (lax.rem(j, 2) == 1))
    def _():
        compute(z1_ref, z0_ref)

    @pl.when(j == ntile)
    def _():
        compute(z0_ref, z1_ref, phase_a=False)

    @pl.when(j == ntile)
    def _():
        for c in put(tb):
            c.wait()

        @pl.when(ntile >= 2)
        def _():
            for c in put(tb - 1):
                c.wait()


def _mixer_call(x, mix_norm, w_in, conv_w, conv_b, wa_blk, ba, wi_blk, bi, lam, ln_g, ln_b, ws,
                bs_tile, w_out, ffn_norm):
    cfg = _tiles()
    bsz, seq, d = x.shape
    ts = cfg["mixer_rows"]
    group = ts // V7X_SUBLANES
    nseq = seq // ts
    ntile = bsz * nseq
    assert ntile % 2 == 0, "the last grid step reads the in-projection buffer of an odd tile"
    row1 = (1, d)
    in_specs = [
        pl.BlockSpec(memory_space=pl.ANY),
        _const_spec(row1),
        pl.BlockSpec(memory_space=pl.ANY),
        _const_spec(conv_w.shape), _const_spec(row1),
        _const_spec(wa_blk.shape), _const_spec(row1),
        _const_spec(wi_blk.shape), _const_spec(row1),
        _const_spec(row1),
        _const_spec(row1), _const_spec(row1),
        _const_spec(ws.shape), _const_spec(bs_tile.shape),
        pl.BlockSpec(memory_space=pl.ANY), _const_spec(row1),
    ]
    out_shape = [
        jax.ShapeDtypeStruct((bsz, seq, d), F32),
        jax.ShapeDtypeStruct((bsz, seq, d // 2), U32),
    ]
    out_specs = [
        pl.BlockSpec(memory_space=pl.ANY),
        pl.BlockSpec(memory_space=pl.ANY),
    ]
    scratch = [
        pltpu.VMEM((3, group, V7X_SUBLANES, d), F32),
        pltpu.VMEM((ts, w_in.shape[1]), F32),
        pltpu.VMEM((ts, w_in.shape[1]), F32),
        pltpu.VMEM((2, group, V7X_SUBLANES, d), F32),
        pltpu.VMEM((2, group, V7X_SUBLANES, d // 2), U32),
        pltpu.SemaphoreType.DMA((3,)),
        pltpu.SemaphoreType.DMA((2,)),
        pltpu.SemaphoreType.DMA((2,)),
        pltpu.VMEM((SGU_GROUPS, ts, ts), BF16),
        pltpu.VMEM((CONV_WIDTH - 1, V7X_SUBLANES, d), F32),
        pltpu.VMEM((1, d), F32),
        pltpu.VMEM(w_in.shape, BF16),
        pltpu.VMEM(w_out.shape, BF16),
        pltpu.SemaphoreType.DMA((2,)),
    ]
    return pl.pallas_call(
        functools.partial(_mixer_kernel, nseq=nseq),
        grid=(ntile + 1,),
        in_specs=in_specs,
        out_specs=out_specs,
        out_shape=out_shape,
        scratch_shapes=scratch,
        compiler_params=pltpu.CompilerParams(
            dimension_semantics=("arbitrary",),
            vmem_limit_bytes=cfg["mixer_vmem"]),
        name="mixer",
    )(x, mix_norm, w_in, conv_w, conv_b, wa_blk, ba, wi_blk, bi, lam, ln_g, ln_b, ws, bs_tile,
      w_out, ffn_norm)


def _router_kernel(hp_ref, wg_ref, we_ref, br_ref, pos_ref, gate_ref, cnt_ref, ccar_ref, wr_ref,
                   *, expert_capacity):
    rows = hp_ref.shape[0]

    @pl.when(pl.program_id(0) == 0)
    def _():
        ccar_ref[...] = jnp.zeros_like(ccar_ref)
        wr_ref[...] = jnp.zeros_like(wr_ref)
        wr_ref[:, 0:N_GROUPS] = wg_ref[...].astype(BF16)
        wr_ref[:, EXPERT_ROW0:EXPERT_ROW0 + N_EXPERTS] = we_ref[...].astype(BF16)

    sub_rows = rows // ROUTER_SUBBLOCKS
    lts = []
    for q in range(ROUTER_SUBBLOCKS):
        lo, hi = _unpack_bf16_pair(hp_ref[q * sub_rows:(q + 1) * sub_rows, :])
        hn = jnp.concatenate([lo, hi], axis=1)
        logits = _dot(hn.astype(BF16), wr_ref[...])
        lts.append(jnp.transpose(logits) + br_ref[...])
    sub = lax.broadcasted_iota(jnp.int32, (V7X_SUBLANES, sub_rows), 0)
    subf = sub.astype(F32)
    big = jnp.float32(1e9)
    eid = lax.broadcasted_iota(jnp.int32, (N_EXPERTS, sub_rows), 0).astype(F32)
    sb = V7X_MXU_DIM
    before = (lax.broadcasted_iota(jnp.int32, (sb, sb), 0)
              < lax.broadcasted_iota(jnp.int32, (sb, sb), 1))
    before = jnp.where(before, 1.0, 0.0).astype(BF16)
    cap = float(expert_capacity)
    zero = jnp.zeros((V7X_SUBLANES - TOP_K, sub_rows), F32)
    running = ccar_ref[:, 0:1]
    for q, lt in enumerate(lts):
        lg = jnp.where(sub < N_GROUPS, lt[0:V7X_SUBLANES, :], -jnp.inf)
        g_exp = jnp.exp(lg - jnp.max(lg, axis=0, keepdims=True))
        g_prob = g_exp / jnp.sum(g_exp, axis=0, keepdims=True)
        g_top = jnp.max(g_prob, axis=0, keepdims=True)
        g_idx = jnp.min(jnp.where(g_prob == g_top, subf, big), axis=0, keepdims=True)

        e_sel = jnp.zeros((EXPERTS_PER_GROUP, sub_rows), F32)
        for g in range(N_GROUPS):
            r0 = EXPERT_ROW0 + g * EXPERTS_PER_GROUP
            e_sel = jnp.where(g_idx == g, lt[r0:r0 + EXPERTS_PER_GROUP, :], e_sel)
        e_exp = jnp.exp(e_sel - jnp.max(e_sel, axis=0, keepdims=True))
        e_prob = e_exp / jnp.sum(e_exp, axis=0, keepdims=True)
        p1 = jnp.max(e_prob, axis=0, keepdims=True)
        i1 = jnp.min(jnp.where(e_prob == p1, subf, big), axis=0, keepdims=True)
        rest = jnp.where(subf == i1, -1.0, e_prob)
        p2 = jnp.max(rest, axis=0, keepdims=True)
        i2 = jnp.min(jnp.where(rest == p2, subf, big), axis=0, keepdims=True)
        psum = p1 + p2
        gate1 = g_top * (p1 / psum)
        gate2 = g_top * (p2 / psum)
        gid1 = g_idx * EXPERTS_PER_GROUP + i1
        gid2 = g_idx * EXPERTS_PER_GROUP + i2

        hit1 = eid == gid1
        hit2 = eid == gid2
        cnt = jnp.where(hit1 | hit2, 1.0, 0.0)
        base = []
        for c in range(sub_rows // sb):
            part = cnt[:, c * sb:(c + 1) * sb]
            base.append(running + _dot(part.astype(BF16), before))
            running = running + jnp.sum(part, axis=1, keepdims=True)
        base = jnp.concatenate(base, axis=1)
        rank1 = jnp.sum(jnp.where(hit1, base, 0.0), axis=0, keepdims=True)
        rank2 = jnp.sum(jnp.where(hit2, base, 0.0), axis=0, keepdims=True)
        pos = jnp.concatenate([gid1 * cap + rank1, gid2 * cap + rank2, zero], axis=0)
        pos_ref[:, q * sub_rows:(q + 1) * sub_rows] = pos.astype(jnp.int32)
        gate_ref[q * sub_rows:(q + 1) * sub_rows, :] = jnp.transpose(
            jnp.concatenate([gate1, gate2, zero], axis=0))
    total = jnp.broadcast_to(running, ccar_ref.shape)
    ccar_ref[...] = total
    cnt_ref[...] = total


def _router_call(hp, w_group, w_expert, b_router):
    cfg = _tiles()
    ntok, half = hp.shape
    tr = cfg["router_rows"]
    return pl.pallas_call(
        functools.partial(_router_kernel, expert_capacity=ntok),
        grid=(ntok // tr,),
        in_specs=[
            pl.BlockSpec((tr, half), lambda i: (i, 0)),
            _const_spec(w_group.shape),
            _const_spec(w_expert.shape),
            _const_spec(b_router.shape),
        ],
        out_specs=[
            pl.BlockSpec((V7X_SUBLANES, tr), lambda i: (0, i)),
            pl.BlockSpec((tr, V7X_SUBLANES), lambda i: (i, 0)),
            pl.BlockSpec((N_EXPERTS, V7X_LANES), lambda i: (0, 0)),
        ],
        out_shape=[
            jax.ShapeDtypeStruct((V7X_SUBLANES, ntok), jnp.int32),
            jax.ShapeDtypeStruct((ntok, V7X_SUBLANES), F32),
            jax.ShapeDtypeStruct((N_EXPERTS, V7X_LANES), F32),
        ],
        scratch_shapes=[
            pltpu.VMEM((N_EXPERTS, V7X_LANES), F32),
            pltpu.VMEM((w_group.shape[0], ROUTER_ROWS), BF16),
        ],
        compiler_params=pltpu.CompilerParams(
            dimension_semantics=("arbitrary",),
            vmem_limit_bytes=cfg["router_vmem"]),
        name="router",
    )(hp, w_group, w_expert, b_router)


def _expert_kernel(nt_ref, base_ref, texp_ref, tloc_ref, hs_hbm, w1_ref, w3_ref, w2_ref, ys_hbm,
                   hbuf, ybuf, hsem, ysem, w1b_ref, w3b_ref, w2b_ref, *, capacity):
    e = pl.program_id(0)
    n_exp = pl.num_programs(0)
    nt = nt_ref[e]
    base = base_ref[e]
    total = base_ref[n_exp - 1] + nt_ref[n_exp - 1]
    n_in, tm, _ = hbuf.shape
    n_out = ybuf.shape[0]
    ahead = n_in - 1

    def load(g):
        slot = lax.rem(g, n_in)
        rows = pl.ds(texp_ref[g] * capacity + tloc_ref[g] * tm, tm)
        return pltpu.make_async_copy(hs_hbm.at[rows], hbuf.at[slot], hsem.at[slot])

    def store(t, slot):
        rows = pl.ds(e * capacity + t * tm, tm)
        return pltpu.make_async_copy(ybuf.at[slot], ys_hbm.at[rows], ysem.at[slot])

    @pl.when(e == 0)
    def _():
        for g0 in range(ahead):
            @pl.when(g0 < total)
            def _():
                load(g0).start(priority=1)

    w1b_ref[...] = w1_ref[...].astype(BF16)
    w3b_ref[...] = w3_ref[...].astype(BF16)
    w2b_ref[...] = w2_ref[...].astype(BF16)

    @pl.loop(0, nt)
    def _(t):
        g = base + t

        @pl.when(g + ahead < total)
        def _():
            load(g + ahead).start(priority=1)

        load(g).wait()
        slot = lax.rem(g, n_out)

        @pl.when(g >= n_out)
        def _():
            store(t, slot).wait()

        sub_rows = tm // EXPERT_SUBBLOCKS
        blocks = [pl.ds(q * sub_rows, sub_rows) for q in range(EXPERT_SUBBLOCKS)]
        rows_in = []
        for rs in blocks:
            lo, hi = _unpack_bf16_pair(hbuf[lax.rem(g, n_in), rs, :])
            rows_in.append(jnp.concatenate([lo, hi], axis=1).astype(BF16))
        up = [(_dot(h, w1b_ref[...]), _dot(h, w3b_ref[...])) for h in rows_in]
        down = []
        for a, b in up:
            hid = (a * _sigmoid(a)) * b
            down.append(_dot(hid.astype(BF16), w2b_ref[...]))
        for rs, y in zip(blocks, down):
            half = y.shape[1] // 2
            ybuf[slot, rs, :] = _pack_bf16_pair(y[:, :half], y[:, half:])
        store(t, slot).start()

    @pl.when(e + 1 == n_exp)
    def _():
        for back in range(1, n_out + 1):
            @pl.when(total >= back)
            def _():
                store(0, lax.rem(total - back, n_out)).wait()


def _expert_call(tiles_per_expert, hs, w1, w3, w2, capacity):
    cfg = _tiles()
    tm = cfg["expert_rows"]
    prow, half = hs.shape
    n_exp, d, f = w1.shape
    ends = jnp.cumsum(tiles_per_expert)
    base = ends - tiles_per_expert
    g = jnp.arange(capacity * TOP_K // tm + n_exp, dtype=jnp.int32)
    texp = jnp.minimum(jnp.sum((ends[None, :] <= g[:, None]).astype(jnp.int32), axis=1), n_exp - 1)
    onehot = texp[:, None] == jnp.arange(n_exp, dtype=jnp.int32)[None, :]
    tloc = g - jnp.sum(jnp.where(onehot, base[None, :], 0), axis=1)

    def w_map(e, *_):
        return (e, 0, 0)

    grid_spec = pltpu.PrefetchScalarGridSpec(
        num_scalar_prefetch=4,
        grid=(n_exp,),
        in_specs=[
            pl.BlockSpec(memory_space=pl.ANY),
            pl.BlockSpec((None, d, f), w_map),
            pl.BlockSpec((None, d, f), w_map),
            pl.BlockSpec((None, f, d), w_map),
        ],
        out_specs=pl.BlockSpec(memory_space=pl.ANY),
        scratch_shapes=[
            pltpu.VMEM((EXPERT_LOOKAHEAD + 1, tm, half), U32),
            pltpu.VMEM((2, tm, half), U32),
            pltpu.SemaphoreType.DMA((EXPERT_LOOKAHEAD + 1,)),
            pltpu.SemaphoreType.DMA((2,)),
            pltpu.VMEM((d, f), BF16),
            pltpu.VMEM((d, f), BF16),
            pltpu.VMEM((f, d), BF16),
        ],
    )
    return pl.pallas_call(
        functools.partial(_expert_kernel, capacity=capacity),
        grid_spec=grid_spec,
        out_shape=jax.ShapeDtypeStruct((prow, half), U32),
        compiler_params=pltpu.CompilerParams(
            dimension_semantics=("arbitrary",),
            vmem_limit_bytes=cfg["expert_vmem"]),
        name="experts",
    )(tiles_per_expert, base, texp, tloc, hs, w1, w3, w2)


def _sc_mesh():
    return plsc.VectorSubcoreMesh(core_axis_name="c", subcore_axis_name="s",
                                  num_cores=V7X_SC_CORES, num_subcores=V7X_SC_SUBCORES)


def _sc_worker_id():
    return lax.axis_index("s") * V7X_SC_CORES + lax.axis_index("c")


def _dispatch_call(hp, pos_w, out_rows):
    cfg = _tiles()
    ntok, half = hp.shape
    _, nw, nch, ch = pos_w.shape
    topk = TOP_K
    per_w = nch * ch

    def body(hp_hbm, pos_hbm, hs_hbm, idx_v, buf, wsem):
        wid = _sc_worker_id()
        for k in range(topk):
            pltpu.sync_copy(pos_hbm.at[k, wid], idx_v.at[k])

        for c in range(nch):
            pltpu.sync_copy(hp_hbm.at[pl.ds(wid * per_w + c * ch, ch)], buf)
            writes = [pltpu.make_async_copy(buf, hs_hbm.at[idx_v.at[k, c]], wsem.at[k]) for k in range(topk)]
            for w in writes:
                w.start()
            for w in writes:
                w.wait()

    assert nw == V7X_SC_CORES * V7X_SC_SUBCORES and nw * per_w == ntok and ch == cfg["sc_rows"]
    return pl.kernel(
        body,
        out_type=jax.ShapeDtypeStruct((out_rows, half), U32),
        mesh=_sc_mesh(),
        scratch_types=[
            pltpu.VMEM((topk, nch, ch), jnp.int32),
            pltpu.VMEM((ch, half), U32),
            pltpu.SemaphoreType.DMA((topk,)),
        ],
        name="dispatch",
    )(hp, pos_w)


def _combine_call(ys, pos_w):
    cfg = _tiles()
    _, half = ys.shape
    _, nw, nch, ch = pos_w.shape
    topk = TOP_K
    per_w = nch * ch
    ntok = nw * per_w

    def body(ys_hbm, pos_hbm, *rest):
        outs = rest[:topk]
        idx_v, buf = rest[topk:]
        wid = _sc_worker_id()
        for k in range(topk):
            pltpu.sync_copy(pos_hbm.at[k, wid], idx_v.at[k])
        for c in range(nch):
            for k in range(topk):
                pltpu.sync_copy(ys_hbm.at[idx_v.at[k, c]], buf)
                pltpu.sync_copy(buf, outs[k].at[pl.ds(wid * per_w + c * ch, ch)])

    assert nw == V7X_SC_CORES * V7X_SC_SUBCORES and ch == cfg["sc_rows"]
    return pl.kernel(
        body,
        out_type=[jax.ShapeDtypeStruct((ntok, half), U32)] * topk,
        mesh=_sc_mesh(),
        scratch_types=[
            pltpu.VMEM((topk, nch, ch), jnp.int32),
            pltpu.VMEM((ch, half), U32),
        ],
        name="combine",
    )(ys, pos_w)


def _ple_kernel(x1_ref, yg0_ref, yg1_ref, gate_ref, p_ref, plen_ref, wg32_ref, wu32_ref, fin_ref, o_ref,
                wg_ref, wu_ref):
    @pl.when(pl.program_id(0) == 0)
    def _():
        wg_ref[...] = (0.5 * wg32_ref[...]).astype(BF16)
        wu_ref[...] = (0.5 * wu32_ref[...]).astype(BF16)

    rows = x1_ref.shape[0]
    sub_rows = rows // PLE_SUBBLOCKS
    for q in range(PLE_SUBBLOCKS):
        rs = pl.ds(q * sub_rows, sub_rows)
        lo0, hi0 = _unpack_bf16_pair(yg0_ref[rs, :])
        lo1, hi1 = _unpack_bf16_pair(yg1_ref[rs, :])
        g0 = gate_ref[rs, 0:1]
        g1 = gate_ref[rs, 1:2]
        moe = g0 * jnp.concatenate([lo0, hi0], axis=1) + g1 * jnp.concatenate([lo1, hi1], axis=1)
        x2 = x1_ref[rs, :] + moe
        r = _rmsnorm(x2, plen_ref[...]).astype(BF16)
        gt2 = 1.0 + jnp.tanh(_dot(r, wg_ref[...]))
        up_h = _dot(p_ref[rs, :].astype(BF16), wu_ref[...])
        x3 = x2 + gt2 * up_h
        o_ref[rs, :] = _rmsnorm(x3, fin_ref[...])


def _ple_call(x1, yg0, yg1, gates, p, ple_norm, wg, wu, final_norm):
    cfg = _tiles()
    ntok, d = x1.shape
    tp = cfg["ple_rows"]
    pdim = p.shape[1]
    return pl.pallas_call(
        _ple_kernel,
        grid=(ntok // tp,),
        in_specs=[
            pl.BlockSpec((tp, d), lambda i: (i, 0)),
            pl.BlockSpec((tp, d // 2), lambda i: (i, 0)),
            pl.BlockSpec((tp, d // 2), lambda i: (i, 0)),
            pl.BlockSpec((tp, V7X_SUBLANES), lambda i: (i, 0)),
            pl.BlockSpec((tp, pdim), lambda i: (i, 0)),
            _const_spec((1, d)),
            _const_spec(wg.shape),
            _const_spec(wu.shape),
            _const_spec((1, d)),
        ],
        out_specs=pl.BlockSpec((tp, d), lambda i: (i, 0)),
        out_shape=jax.ShapeDtypeStruct((ntok, d), F32),
        scratch_shapes=[pltpu.VMEM(wg.shape, BF16), pltpu.VMEM(wu.shape, BF16)],
        compiler_params=pltpu.CompilerParams(
            dimension_semantics=("arbitrary",),
            vmem_limit_bytes=cfg["ple_vmem"]),
        name="ple",
    )(x1, yg0, yg1, gates, p, ple_norm, wg, wu, final_norm)


def _blockdiag_pack(w):
    nb, bd, _ = w.shape
    per = V7X_MXU_DIM // bd
    w4 = w.reshape(nb // per, per, bd, bd)
    eye = jnp.eye(per, dtype=w.dtype)
    out = jnp.einsum("jpab,pq->jpaqb", w4, eye)
    return out.reshape(nb // per, V7X_MXU_DIM, V7X_MXU_DIM).astype(BF16)


def kernel(x, p, mix_norm, w_in, conv_w, conv_b, lru_wa, lru_ba, lru_wi, lru_bi, lru_lambda, sgu_ln_g, sgu_ln_b, sgu_ws, sgu_bs, w_out, ffn_norm, router_group_w, router_group_b, router_expert_w, router_expert_b, expert_w1, expert_w3, expert_w2, ple_norm, ple_gate_w, ple_up_w, final_norm):
    cfg = _tiles()
    bsz, seq, d = x.shape
    ntok = bsz * seq
    tm = cfg["expert_rows"]
    depth = w_in.shape[0]
    assert depth == 1, "the ple kernel applies the final norm, so it must be the last layer"
    l = 0
    nw_rows = V7X_SC_CORES * V7X_SC_SUBCORES * cfg["sc_rows"]
    assert cfg["mixer_rows"] % CHUNK == 0 and seq % cfg["mixer_rows"] == 0
    assert ntok % cfg["router_rows"] == 0 and ntok % cfg["ple_rows"] == 0 and ntok % nw_rows == 0
    assert ntok % tm == 0 and lru_wa.shape[1:] == (LRU_BLOCKS, d // LRU_BLOCKS, d // LRU_BLOCKS)
    assert max(cfg[k] for k in cfg if k.endswith("_vmem")) < V7X_VMEM_BYTES
    b_router = jnp.concatenate([
        router_group_b[l], jnp.zeros((EXPERT_ROW0 - N_GROUPS,), F32), router_expert_b[l],
        jnp.zeros((ROUTER_ROWS - EXPERT_ROW0 - N_EXPERTS,), F32)])[:, None]
    ts = cfg["mixer_rows"]
    group = ts // V7X_SUBLANES
    bs_tile = jnp.tile(sgu_bs[l], (1, ts // CHUNK)).reshape(SGU_GROUPS, V7X_SUBLANES, group)
    bs_tile = jnp.transpose(bs_tile, (2, 1, 0)).reshape(ts, SGU_GROUPS)
    x1, hp = _mixer_call(
        x, mix_norm[l][None], w_in[l], conv_w[l], conv_b[l][None],
        _blockdiag_pack(lru_wa[l]), lru_ba[l][None], _blockdiag_pack(lru_wi[l]), lru_bi[l][None],
        lru_lambda[l][None], sgu_ln_g[l][None], sgu_ln_b[l][None], sgu_ws[l], bs_tile,
        w_out[l], ffn_norm[l][None])
    hp = hp.reshape(ntok, d // 2)
    pos, gate, cnt = _router_call(hp, router_group_w[l], router_expert_w[l], b_router)

    cap = ntok
    tiles_per_expert = (cnt[:, 0].astype(jnp.int32) + tm - 1) // tm
    nw = V7X_SC_CORES * V7X_SC_SUBCORES
    ch = cfg["sc_rows"]
    pos_w = pos.reshape(pos.shape[0], nw, ntok // (nw * ch), ch)

    hs = _dispatch_call(hp, pos_w, N_EXPERTS * cap)
    ys = _expert_call(tiles_per_expert, hs, expert_w1[l], expert_w3[l], expert_w2[l], cap)
    yg0, yg1 = _combine_call(ys, pos_w)

    out = _ple_call(x1.reshape(ntok, d), yg0, yg1, gate, p[l].reshape(ntok, -1), ple_norm[l][None],
                    ple_gate_w[l], ple_up_w[l], final_norm[None])
    return out.reshape(bsz, seq, d)
```
